```python
import math
import jax, jax.numpy as jnp
from jax import lax
import numpy as np

D_MODEL = 1024
BATCH = 8
SEQ = 2048
DEPTH = 4

N_MIXERS = 4
N_A = (DEPTH + 3) // 4
N_B = (DEPTH + 2) // 4
N_C = (DEPTH + 1) // 4
N_D = DEPTH // 4
EPS = 1e-6
NEG_INF = -1e30

SSM_WIDTH = D_MODEL
SSM_GROUP = 16
SSM_GROUPS = SSM_WIDTH // SSM_GROUP
SSM_STATE = 64
DT_MIN = 1e-3
DT_MAX = 1e-1

HEAD_DIM = 64
SWA_HEADS = D_MODEL // HEAD_DIM
SWA_KV_HEADS = SWA_HEADS // 8
SWA_WIDTH = SWA_HEADS * HEAD_DIM
WINDOW = 128

REL_BUCKETS = 32
REL_MAX_DIST = 128

MLA_HEADS = 16
MLA_NOPE = 64
MLA_ROPE = 32
MLA_V = 64
MLA_KV_RANK = 256
MLA_Q_RANK = 768
MLA_WIDTH = MLA_HEADS * MLA_V
ROPE_BASE = 10000.0
Q_BLOCK = 128

SGU_WIDTH = D_MODEL
SGU_CHUNK = 128
SGU_GROUPS = 16
SGU_GROUP_DIM = SGU_WIDTH // SGU_GROUPS

kernel_name = 'hybrid_interleaved_s5_swa_mla_sgu'


def rmsnorm(x, g):
    xf = x.astype(jnp.float32)
    y = xf * lax.rsqrt(jnp.mean(xf * xf, axis=-1, keepdims=True) + EPS)
    return (y * g.astype(jnp.float32)).astype(x.dtype)


def layernorm(x, g, b):
    xf = x.astype(jnp.float32)
    mu = jnp.mean(xf, axis=-1, keepdims=True)
    var = jnp.mean(jnp.square(xf - mu), axis=-1, keepdims=True)
    y = (xf - mu) * lax.rsqrt(var + EPS) * g.astype(jnp.float32) + b.astype(jnp.float32)
    return y.astype(x.dtype)


def _ssm_combine(left, right):
    a1r, a1i, b1r, b1i = left
    a2r, a2i, b2r, b2i = right
    return (a2r * a1r - a2i * a1i,
            a2r * a1i + a2i * a1r,
            a2r * b1r - a2i * b1i + b2r,
            a2r * b1i + a2i * b1r + b2i)


def s5_mixer(u, lam_re, lam_im, log_dt, b_re, b_im, c_re, c_im, d_skip, w_glu, b_glu):
    f32 = jnp.float32
    bsz, L, _ = u.shape
    ug = u.astype(f32).reshape(bsz, L, SSM_GROUPS, SSM_GROUP)
    lr = lam_re.astype(f32)
    li = lam_im.astype(f32)
    dt = jnp.exp(log_dt.astype(f32))[:, None]
    mag = jnp.exp(lr * dt)
    ab_re = mag * jnp.cos(li * dt)
    ab_im = mag * jnp.sin(li * dt)
    den = lr * lr + li * li
    nr = ab_re - 1.0
    f_re = (nr * lr + ab_im * li) / den
    f_im = (ab_im * lr - nr * li) / den
    br = b_re.astype(f32)
    bi = b_im.astype(f32)
    bb_re = f_re[..., None] * br - f_im[..., None] * bi
    bb_im = f_re[..., None] * bi + f_im[..., None] * br
    bu_re = jnp.einsum('blgh,gph->blgp', ug, bb_re)
    bu_im = jnp.einsum('blgh,gph->blgp', ug, bb_im)
    a_re = jnp.broadcast_to(ab_re, (1, L) + ab_re.shape)
    a_im = jnp.broadcast_to(ab_im, (1, L) + ab_im.shape)
    _, _, s_re, s_im = lax.associative_scan(_ssm_combine, (a_re, a_im, bu_re, bu_im), axis=1)
    y = (jnp.einsum('blgp,ghp->blgh', s_re, c_re.astype(f32))
         - jnp.einsum('blgp,ghp->blgh', s_im, c_im.astype(f32)))
    y = y.reshape(bsz, L, SSM_WIDTH) + d_skip.astype(f32) * u.astype(f32)
    y = jax.nn.gelu(y).astype(u.dtype)
    return y * jax.nn.sigmoid(y @ w_glu + b_glu)


def s5_branch(h, w_in, lam_re, lam_im, log_dt, b_re, b_im, c_re, c_im, d_skip, w_glu, b_glu, w_out):
    u, z = jnp.split(h @ w_in, [SSM_WIDTH], axis=-1)
    y = s5_mixer(u, lam_re, lam_im, log_dt, b_re, b_im, c_re, c_im, d_skip, w_glu, b_glu)
    return (y * jax.nn.silu(z)) @ w_out


def t5_bucket(dist):
    max_exact = REL_BUCKETS // 2
    dist_f = jnp.maximum(dist, 1).astype(jnp.float32)
    large = max_exact + (jnp.log(dist_f / max_exact) / math.log(REL_MAX_DIST / max_exact)
                         * (REL_BUCKETS - max_exact)).astype(jnp.int32)
    large = jnp.minimum(large, REL_BUCKETS - 1)
    return jnp.where(dist < max_exact, dist, large)


def sliding_window_attention(q, k, v, sinks, rel_bias):
    bsz, L = q.shape[0], q.shape[1]
    nb = L // WINDOW
    grp = SWA_HEADS // SWA_KV_HEADS
    qb = q.reshape(bsz, nb, WINDOW, SWA_KV_HEADS, grp, HEAD_DIM)

    def band(t):
        prev = jnp.pad(t, ((0, 0), (WINDOW, 0), (0, 0), (0, 0)))[:, :L]
        shp = (bsz, nb, WINDOW, SWA_KV_HEADS, HEAD_DIM)
        return jnp.concatenate([prev.reshape(shp), t.reshape(shp)], axis=2)

    kb = band(k)
    vb = band(v)
    s = jnp.einsum('bnqhgd,bnkhd->bnhgqk', qb, kb).astype(jnp.float32) * (HEAD_DIM ** -0.5)
    qi = jnp.arange(WINDOW)[:, None]
    kj = jnp.arange(2 * WINDOW)[None, :]
    dist = qi + WINDOW - kj
    blk = jnp.arange(nb)[:, None, None]
    valid = (dist >= 0) & (dist < WINDOW) & (blk * WINDOW + kj - WINDOW >= 0)
    bias = rel_bias[t5_bucket(jnp.maximum(dist, 0))]
    bias = jnp.transpose(bias, (2, 0, 1)).reshape(SWA_KV_HEADS, grp, WINDOW, 2 * WINDOW).astype(jnp.float32)
    s = jnp.where(valid[None, :, None, None], s + bias, NEG_INF)
    sink = jnp.broadcast_to(sinks.astype(jnp.float32).reshape(SWA_KV_HEADS, grp, 1, 1), s.shape[:-1] + (1,))
    p = jax.nn.softmax(jnp.concatenate([s, sink], axis=-1), axis=-1)[..., :-1]
    o = jnp.einsum('bnhgqk,bnkhd->bnqhgd', p.astype(v.dtype), vb)
    return o.reshape(bsz, L, SWA_WIDTH)


def swa_branch(h, w_in, sinks, w_out, rel_bias):
    bsz, L, _ = h.shape
    kv_w = SWA_KV_HEADS * HEAD_DIM
    q, k, v, z = jnp.split(h @ w_in, [SWA_WIDTH, SWA_WIDTH + kv_w, SWA_WIDTH + 2 * kv_w], axis=-1)
    q = q.reshape(bsz, L, SWA_HEADS, HEAD_DIM)
    k = k.reshape(bsz, L, SWA_KV_HEADS, HEAD_DIM)
    v = v.reshape(bsz, L, SWA_KV_HEADS, HEAD_DIM)
    o = sliding_window_attention(q, k, v, sinks, rel_bias)
    return (o * jax.nn.silu(z)) @ w_out


def rope_tables(L):
    inv = ROPE_BASE ** (-jnp.arange(0, MLA_ROPE, 2, dtype=jnp.float32) / MLA_ROPE)
    ang = jnp.arange(L, dtype=jnp.float32)[:, None] * inv[None, :]
    return jnp.cos(ang), jnp.sin(ang)


def apply_rope(x, cos, sin):
    xf = x.astype(jnp.float32)
    x1, x2 = jnp.split(xf, 2, axis=-1)
    return jnp.concatenate([x1 * cos - x2 * sin, x2 * cos + x1 * sin], axis=-1).astype(x.dtype)


def causal_block_attention(q, k, v):
    bsz, L, H, dk = q.shape
    nb = L // Q_BLOCK
    scale = dk ** -0.5
    qb = q.reshape(bsz, nb, Q_BLOCK, H, dk).transpose(1, 0, 2, 3, 4)
    kpos = jnp.arange(L)

    def one_block(args):
        qi, n = args
        s = jnp.einsum('bqhd,bkhd->bhqk', qi, k).astype(jnp.float32) * scale
        qpos = n * Q_BLOCK + jnp.arange(Q_BLOCK)
        s = jnp.where(kpos[None, :] <= qpos[:, None], s, NEG_INF)
        p = jax.nn.softmax(s, axis=-1).astype(v.dtype)
        return jnp.einsum('bhqk,bkhd->bqhd', p, v)

    o = lax.map(one_block, (qb, jnp.arange(nb)))
    return o.transpose(1, 0, 2, 3, 4).reshape(bsz, L, H, v.shape[-1])


def mla_branch(h, w_in, q_norm, kv_norm, w_uq, w_ukv, w_out):
    bsz, L, _ = h.shape
    c_q, c_kv, k_rope, z = jnp.split(
        h @ w_in, [MLA_Q_RANK, MLA_Q_RANK + MLA_KV_RANK, MLA_Q_RANK + MLA_KV_RANK + MLA_ROPE], axis=-1)
    q = (rmsnorm(c_q, q_norm) @ w_uq).reshape(bsz, L, MLA_HEADS, MLA_NOPE + MLA_ROPE)
    kv = (rmsnorm(c_kv, kv_norm) @ w_ukv).reshape(bsz, L, MLA_HEADS, MLA_NOPE + MLA_V)
    cos, sin = rope_tables(L)
    q = jnp.concatenate([q[..., :MLA_NOPE], apply_rope(q[..., MLA_NOPE:], cos[:, None], sin[:, None])], axis=-1)
    k_rope = apply_rope(k_rope, cos, sin)
    k = jnp.concatenate([kv[..., :MLA_NOPE],
                         jnp.broadcast_to(k_rope[:, :, None, :], (bsz, L, MLA_HEADS, MLA_ROPE))], axis=-1)
    o = causal_block_attention(q, k, kv[..., MLA_NOPE:])
    return (o.reshape(bsz, L, MLA_WIDTH) * jax.nn.silu(z)) @ w_out


def sgu_branch(h, w_in, ln_g, ln_b, w_s, b_s, w_out):
    bsz, L, _ = h.shape
    uv, z = jnp.split(h @ w_in, [2 * SGU_WIDTH], axis=-1)
    u, v = jnp.split(jax.nn.gelu(uv), 2, axis=-1)
    v = layernorm(v, ln_g, ln_b).reshape(bsz, L // SGU_CHUNK, SGU_CHUNK, SGU_GROUPS, SGU_GROUP_DIM)
    tril = jnp.tril(jnp.ones((SGU_CHUNK, SGU_CHUNK), dtype=bool))
    w = jnp.where(tril[None], w_s, 0.0)
    s = jnp.einsum('gts,bnsgc->bntgc', w, v) + b_s.T[:, :, None]
    s = s.reshape(bsz, L, SGU_WIDTH)
    return (u * s * jax.nn.silu(z)) @ w_out


def _fwd_setup_inputs(seed: int = 0) -> dict:
    key = jax.random.key(seed)
    ks = iter(jax.random.split(key, 40))
    f32 = jnp.float32

    def nrm(shape, scale):
        return jax.random.normal(next(ks), shape, f32) * scale

    x = nrm((BATCH, SEQ, D_MODEL), 1.0)
    pre_norm = 1.0 + nrm((DEPTH, D_MODEL), 0.05)
    post_norm = 1.0 + nrm((DEPTH, D_MODEL), 0.05)
    rel_bias = nrm((REL_BUCKETS, SWA_HEADS), 0.5)
    a_w_in = nrm((N_A, D_MODEL, 2 * SSM_WIDTH), D_MODEL ** -0.5)
    n_idx = jnp.arange(SSM_STATE, dtype=f32)
    a_lam_re = -0.5 + nrm((N_A, SSM_GROUPS, SSM_STATE), 0.01)
    a_lam_im = jnp.pi * n_idx + nrm((N_A, SSM_GROUPS, SSM_STATE), 0.01)
    a_log_dt = jax.random.uniform(next(ks), (N_A, SSM_GROUPS), f32, math.log(DT_MIN), math.log(DT_MAX))
    a_b_re = nrm((N_A, SSM_GROUPS, SSM_STATE, SSM_GROUP), (2 * SSM_GROUP) ** -0.5)
    a_b_im = nrm((N_A, SSM_GROUPS, SSM_STATE, SSM_GROUP), (2 * SSM_GROUP) ** -0.5)
    a_c_re = nrm((N_A, SSM_GROUPS, SSM_GROUP, SSM_STATE), SSM_STATE ** -0.5)
    a_c_im = nrm((N_A, SSM_GROUPS, SSM_GROUP, SSM_STATE), SSM_STATE ** -0.5)
    a_d = nrm((N_A, SSM_WIDTH), 1.0)
    a_w_glu = nrm((N_A, SSM_WIDTH, SSM_WIDTH), SSM_WIDTH ** -0.5)
    a_b_glu = nrm((N_A, SSM_WIDTH), 0.02)
    a_w_out = nrm((N_A, SSM_WIDTH, D_MODEL), SSM_WIDTH ** -0.5)
    b_w_in = nrm((N_B, D_MODEL, 2 * SWA_WIDTH + 2 * SWA_KV_HEADS * HEAD_DIM), D_MODEL ** -0.5)
    b_sinks = nrm((N_B, SWA_HEADS), 1.0)
    b_w_out = nrm((N_B, SWA_WIDTH, D_MODEL), SWA_WIDTH ** -0.5)
    c_w_in = nrm((N_C, D_MODEL, MLA_Q_RANK + MLA_KV_RANK + MLA_ROPE + MLA_WIDTH), D_MODEL ** -0.5)
    c_q_norm = 1.0 + nrm((N_C, MLA_Q_RANK), 0.05)
    c_kv_norm = 1.0 + nrm((N_C, MLA_KV_RANK), 0.05)
    c_w_uq = nrm((N_C, MLA_Q_RANK, MLA_HEADS * (MLA_NOPE + MLA_ROPE)), MLA_Q_RANK ** -0.5)
    c_w_ukv = nrm((N_C, MLA_KV_RANK, MLA_HEADS * (MLA_NOPE + MLA_V)), MLA_KV_RANK ** -0.5)
    c_w_out = nrm((N_C, MLA_WIDTH, D_MODEL), MLA_WIDTH ** -0.5)
    d_w_in = nrm((N_D, D_MODEL, 3 * SGU_WIDTH), D_MODEL ** -0.5)
    d_ln_g = 1.0 + nrm((N_D, SGU_WIDTH), 0.05)
    d_ln_b = nrm((N_D, SGU_WIDTH), 0.02)
    d_w_s = nrm((N_D, SGU_GROUPS, SGU_CHUNK, SGU_CHUNK), 0.5 * SGU_CHUNK ** -0.5)
    d_b_s = 1.0 + nrm((N_D, SGU_GROUPS, SGU_CHUNK), 0.1)
    d_w_out = nrm((N_D, SGU_WIDTH, D_MODEL), SGU_WIDTH ** -0.5)
    return {'x': x, 'pre_norm': pre_norm, 'post_norm': post_norm, 'rel_bias': rel_bias,
            'a_w_in': a_w_in, 'a_lam_re': a_lam_re, 'a_lam_im': a_lam_im, 'a_log_dt': a_log_dt,
            'a_b_re': a_b_re, 'a_b_im': a_b_im, 'a_c_re': a_c_re, 'a_c_im': a_c_im, 'a_d': a_d,
            'a_w_glu': a_w_glu, 'a_b_glu': a_b_glu, 'a_w_out': a_w_out,
            'b_w_in': b_w_in, 'b_sinks': b_sinks, 'b_w_out': b_w_out,
            'c_w_in': c_w_in, 'c_q_norm': c_q_norm, 'c_kv_norm': c_kv_norm, 'c_w_uq': c_w_uq,
            'c_w_ukv': c_w_ukv, 'c_w_out': c_w_out,
            'd_w_in': d_w_in, 'd_ln_g': d_ln_g, 'd_ln_b': d_ln_b, 'd_w_s': d_w_s, 'd_b_s': d_b_s,
            'd_w_out': d_w_out}


def _fwd_reference(x, pre_norm, post_norm, rel_bias,
              a_w_in, a_lam_re, a_lam_im, a_log_dt, a_b_re, a_b_im, a_c_re, a_c_im, a_d,
              a_w_glu, a_b_glu, a_w_out,
              b_w_in, b_sinks, b_w_out,
              c_w_in, c_q_norm, c_kv_norm, c_w_uq, c_w_ukv, c_w_out,
              d_w_in, d_ln_g, d_ln_b, d_w_s, d_b_s, d_w_out):
    for i in range(DEPTH):
        kind = i % N_MIXERS
        j = i // N_MIXERS
        h = rmsnorm(x, pre_norm[i])
        if kind == 0:
            y = s5_branch(h, a_w_in[j], a_lam_re[j], a_lam_im[j], a_log_dt[j], a_b_re[j], a_b_im[j],
                          a_c_re[j], a_c_im[j], a_d[j], a_w_glu[j], a_b_glu[j], a_w_out[j])
        elif kind == 1:
            y = swa_branch(h, b_w_in[j], b_sinks[j], b_w_out[j], rel_bias)
        elif kind == 2:
            y = mla_branch(h, c_w_in[j], c_q_norm[j], c_kv_norm[j], c_w_uq[j], c_w_ukv[j], c_w_out[j])
        else:
            y = sgu_branch(h, d_w_in[j], d_ln_g[j], d_ln_b[j], d_w_s[j], d_b_s[j], d_w_out[j])
        x = x + rmsnorm(y, post_norm[i])
    return x


import jax as _jax
import jax.numpy as _jnp

TWIN_FORMAT = 'train_step'
FWD_PARAMS = ['x', 'pre_norm', 'post_norm', 'rel_bias', 'a_w_in', 'a_lam_re', 'a_lam_im', 'a_log_dt', 'a_b_re', 'a_b_im', 'a_c_re', 'a_c_im', 'a_d', 'a_w_glu', 'a_b_glu', 'a_w_out', 'b_w_in', 'b_sinks', 'b_w_out', 'c_w_in', 'c_q_norm', 'c_kv_norm', 'c_w_uq', 'c_w_ukv', 'c_w_out', 'd_w_in', 'd_ln_g', 'd_ln_b', 'd_w_s', 'd_b_s', 'd_w_out']
TWIN_WEIGHTS = ['pre_norm', 'post_norm', 'rel_bias', 'a_w_in', 'a_lam_re', 'a_lam_im', 'a_log_dt', 'a_b_re', 'a_b_im', 'a_c_re', 'a_c_im', 'a_d', 'a_w_glu', 'a_b_glu', 'a_w_out', 'b_w_in', 'b_sinks', 'b_w_out', 'c_w_in', 'c_q_norm', 'c_kv_norm', 'c_w_uq', 'c_w_ukv', 'c_w_out', 'd_w_in', 'd_ln_g', 'd_ln_b', 'd_w_s', 'd_b_s', 'd_w_out']
TWIN_DIFF_INPUT = 'x'
TWIN_INPUTS = ['x', 'pre_norm', 'post_norm', 'rel_bias', 'a_w_in', 'a_lam_re', 'a_lam_im', 'a_log_dt', 'a_b_re', 'a_b_im', 'a_c_re', 'a_c_im', 'a_d', 'a_w_glu', 'a_b_glu', 'a_w_out', 'b_w_in', 'b_sinks', 'b_w_out', 'c_w_in', 'c_q_norm', 'c_kv_norm', 'c_w_uq', 'c_w_ukv', 'c_w_out', 'd_w_in', 'd_ln_g', 'd_ln_b', 'd_w_s', 'd_b_s', 'd_w_out', 'loss_target', 'm_pre_norm', 'm_post_norm', 'm_rel_bias', 'm_a_w_in', 'm_a_lam_re', 'm_a_lam_im', 'm_a_log_dt', 'm_a_b_re', 'm_a_b_im', 'm_a_c_re', 'm_a_c_im', 'm_a_d', 'm_a_w_glu', 'm_a_b_glu', 'm_a_w_out', 'm_b_w_in', 'm_b_sinks', 'm_b_w_out', 'm_c_w_in', 'm_c_q_norm', 'm_c_kv_norm', 'm_c_w_uq', 'm_c_w_ukv', 'm_c_w_out', 'm_d_w_in', 'm_d_ln_g', 'm_d_ln_b', 'm_d_w_s', 'm_d_b_s', 'm_d_w_out', 'v_pre_norm', 'v_post_norm', 'v_rel_bias', 'v_a_w_in', 'v_a_lam_re', 'v_a_lam_im', 'v_a_log_dt', 'v_a_b_re', 'v_a_b_im', 'v_a_c_re', 'v_a_c_im', 'v_a_d', 'v_a_w_glu', 'v_a_b_glu', 'v_a_w_out', 'v_b_w_in', 'v_b_sinks', 'v_b_w_out', 'v_c_w_in', 'v_c_q_norm', 'v_c_kv_norm', 'v_c_w_uq', 'v_c_w_ukv', 'v_c_w_out', 'v_d_w_in', 'v_d_ln_g', 'v_d_ln_b', 'v_d_w_s', 'v_d_b_s', 'v_d_w_out']
TWIN_OUTPUTS = ['loss', 'grad_x', 'grad_pre_norm', 'grad_post_norm', 'grad_rel_bias', 'grad_a_w_in', 'grad_a_lam_re', 'grad_a_lam_im', 'grad_a_log_dt', 'grad_a_b_re', 'grad_a_b_im', 'grad_a_c_re', 'grad_a_c_im', 'grad_a_d', 'grad_a_w_glu', 'grad_a_b_glu', 'grad_a_w_out', 'grad_b_w_in', 'grad_b_sinks', 'grad_b_w_out', 'grad_c_w_in', 'grad_c_q_norm', 'grad_c_kv_norm', 'grad_c_w_uq', 'grad_c_w_ukv', 'grad_c_w_out', 'grad_d_w_in', 'grad_d_ln_g', 'grad_d_ln_b', 'grad_d_w_s', 'grad_d_b_s', 'grad_d_w_out', 'delta_pre_norm', 'delta_post_norm', 'delta_rel_bias', 'delta_a_w_in', 'delta_a_lam_re', 'delta_a_lam_im', 'delta_a_log_dt', 'delta_a_b_re', 'delta_a_b_im', 'delta_a_c_re', 'delta_a_c_im', 'delta_a_d', 'delta_a_w_glu', 'delta_a_b_glu', 'delta_a_w_out', 'delta_b_w_in', 'delta_b_sinks', 'delta_b_w_out', 'delta_c_w_in', 'delta_c_q_norm', 'delta_c_kv_norm', 'delta_c_w_uq', 'delta_c_w_ukv', 'delta_c_w_out', 'delta_d_w_in', 'delta_d_ln_g', 'delta_d_ln_b', 'delta_d_w_s', 'delta_d_b_s', 'delta_d_w_out', 'new_m_pre_norm', 'new_m_post_norm', 'new_m_rel_bias', 'new_m_a_w_in', 'new_m_a_lam_re', 'new_m_a_lam_im', 'new_m_a_log_dt', 'new_m_a_b_re', 'new_m_a_b_im', 'new_m_a_c_re', 'new_m_a_c_im', 'new_m_a_d', 'new_m_a_w_glu', 'new_m_a_b_glu', 'new_m_a_w_out', 'new_m_b_w_in', 'new_m_b_sinks', 'new_m_b_w_out', 'new_m_c_w_in', 'new_m_c_q_norm', 'new_m_c_kv_norm', 'new_m_c_w_uq', 'new_m_c_w_ukv', 'new_m_c_w_out', 'new_m_d_w_in', 'new_m_d_ln_g', 'new_m_d_ln_b', 'new_m_d_w_s', 'new_m_d_b_s', 'new_m_d_w_out', 'new_v_pre_norm', 'new_v_post_norm', 'new_v_rel_bias', 'new_v_a_w_in', 'new_v_a_lam_re', 'new_v_a_lam_im', 'new_v_a_log_dt', 'new_v_a_b_re', 'new_v_a_b_im', 'new_v_a_c_re', 'new_v_a_c_im', 'new_v_a_d', 'new_v_a_w_glu', 'new_v_a_b_glu', 'new_v_a_w_out', 'new_v_b_w_in', 'new_v_b_sinks', 'new_v_b_w_out', 'new_v_c_w_in', 'new_v_c_q_norm', 'new_v_c_kv_norm', 'new_v_c_w_uq', 'new_v_c_w_ukv', 'new_v_c_w_out', 'new_v_d_w_in', 'new_v_d_ln_g', 'new_v_d_ln_b', 'new_v_d_w_s', 'new_v_d_b_s', 'new_v_d_w_out']
TWIN_LEAF_KINDS = {'loss': 'loss', 'grad_x': 'grad_x', 'grad_pre_norm': 'grad_w', 'grad_post_norm': 'grad_w', 'grad_rel_bias': 'grad_w', 'grad_a_w_in': 'grad_w', 'grad_a_lam_re': 'grad_w', 'grad_a_lam_im': 'grad_w', 'grad_a_log_dt': 'grad_w', 'grad_a_b_re': 'grad_w', 'grad_a_b_im': 'grad_w', 'grad_a_c_re': 'grad_w', 'grad_a_c_im': 'grad_w', 'grad_a_d': 'grad_w', 'grad_a_w_glu': 'grad_w', 'grad_a_b_glu': 'grad_w', 'grad_a_w_out': 'grad_w', 'grad_b_w_in': 'grad_w', 'grad_b_sinks': 'grad_w', 'grad_b_w_out': 'grad_w', 'grad_c_w_in': 'grad_w', 'grad_c_q_norm': 'grad_w', 'grad_c_kv_norm': 'grad_w', 'grad_c_w_uq': 'grad_w', 'grad_c_w_ukv': 'grad_w', 'grad_c_w_out': 'grad_w', 'grad_d_w_in': 'grad_w', 'grad_d_ln_g': 'grad_w', 'grad_d_ln_b': 'grad_w', 'grad_d_w_s': 'grad_w', 'grad_d_b_s': 'grad_w', 'grad_d_w_out': 'grad_w', 'delta_pre_norm': 'delta_w', 'delta_post_norm': 'delta_w', 'delta_rel_bias': 'delta_w', 'delta_a_w_in': 'delta_w', 'delta_a_lam_re': 'delta_w', 'delta_a_lam_im': 'delta_w', 'delta_a_log_dt': 'delta_w', 'delta_a_b_re': 'delta_w', 'delta_a_b_im': 'delta_w', 'delta_a_c_re': 'delta_w', 'delta_a_c_im': 'delta_w', 'delta_a_d': 'delta_w', 'delta_a_w_glu': 'delta_w', 'delta_a_b_glu': 'delta_w', 'delta_a_w_out': 'delta_w', 'delta_b_w_in': 'delta_w', 'delta_b_sinks': 'delta_w', 'delta_b_w_out': 'delta_w', 'delta_c_w_in': 'delta_w', 'delta_c_q_norm': 'delta_w', 'delta_c_kv_norm': 'delta_w', 'delta_c_w_uq': 'delta_w', 'delta_c_w_ukv': 'delta_w', 'delta_c_w_out': 'delta_w', 'delta_d_w_in': 'delta_w', 'delta_d_ln_g': 'delta_w', 'delta_d_ln_b': 'delta_w', 'delta_d_w_s': 'delta_w', 'delta_d_b_s': 'delta_w', 'delta_d_w_out': 'delta_w', 'new_m_pre_norm': 'new_m', 'new_m_post_norm': 'new_m', 'new_m_rel_bias': 'new_m', 'new_m_a_w_in': 'new_m', 'new_m_a_lam_re': 'new_m', 'new_m_a_lam_im': 'new_m', 'new_m_a_log_dt': 'new_m', 'new_m_a_b_re': 'new_m', 'new_m_a_b_im': 'new_m', 'new_m_a_c_re': 'new_m', 'new_m_a_c_im': 'new_m', 'new_m_a_d': 'new_m', 'new_m_a_w_glu': 'new_m', 'new_m_a_b_glu': 'new_m', 'new_m_a_w_out': 'new_m', 'new_m_b_w_in': 'new_m', 'new_m_b_sinks': 'new_m', 'new_m_b_w_out': 'new_m', 'new_m_c_w_in': 'new_m', 'new_m_c_q_norm': 'new_m', 'new_m_c_kv_norm': 'new_m', 'new_m_c_w_uq': 'new_m', 'new_m_c_w_ukv': 'new_m', 'new_m_c_w_out': 'new_m', 'new_m_d_w_in': 'new_m', 'new_m_d_ln_g': 'new_m', 'new_m_d_ln_b': 'new_m', 'new_m_d_w_s': 'new_m', 'new_m_d_b_s': 'new_m', 'new_m_d_w_out': 'new_m', 'new_v_pre_norm': 'new_v', 'new_v_post_norm': 'new_v', 'new_v_rel_bias': 'new_v', 'new_v_a_w_in': 'new_v', 'new_v_a_lam_re': 'new_v', 'new_v_a_lam_im': 'new_v', 'new_v_a_log_dt': 'new_v', 'new_v_a_b_re': 'new_v', 'new_v_a_b_im': 'new_v', 'new_v_a_c_re': 'new_v', 'new_v_a_c_im': 'new_v', 'new_v_a_d': 'new_v', 'new_v_a_w_glu': 'new_v', 'new_v_a_b_glu': 'new_v', 'new_v_a_w_out': 'new_v', 'new_v_b_w_in': 'new_v', 'new_v_b_sinks': 'new_v', 'new_v_b_w_out': 'new_v', 'new_v_c_w_in': 'new_v', 'new_v_c_q_norm': 'new_v', 'new_v_c_kv_norm': 'new_v', 'new_v_c_w_uq': 'new_v', 'new_v_c_w_ukv': 'new_v', 'new_v_c_w_out': 'new_v', 'new_v_d_w_in': 'new_v', 'new_v_d_ln_g': 'new_v', 'new_v_d_ln_b': 'new_v', 'new_v_d_w_s': 'new_v', 'new_v_d_b_s': 'new_v', 'new_v_d_w_out': 'new_v'}


def _forward(args):
    return _fwd_reference(*[args[k] for k in FWD_PARAMS])


def _output_shape():
    out = _jax.eval_shape(lambda: _forward(_fwd_setup_inputs(0)))
    return out.shape, out.dtype

N_MICROBATCH = 1
ADAM_LR = 0.001
ADAM_B1 = 0.9
ADAM_B2 = 0.999
ADAM_EPS = 1e-08
ADAM_WD = 0.01
ADAM_STEP = 10
PER_EXAMPLE_BATCH_AXIS = {'x': 0, 'loss_target': 0}
SHARED_INPUTS = []
_WEIGHT_DTYPES = {'pre_norm': _jnp.float32, 'post_norm': _jnp.float32, 'rel_bias': _jnp.float32, 'a_w_in': _jnp.float32, 'a_lam_re': _jnp.float32, 'a_lam_im': _jnp.float32, 'a_log_dt': _jnp.float32, 'a_b_re': _jnp.float32, 'a_b_im': _jnp.float32, 'a_c_re': _jnp.float32, 'a_c_im': _jnp.float32, 'a_d': _jnp.float32, 'a_w_glu': _jnp.float32, 'a_b_glu': _jnp.float32, 'a_w_out': _jnp.float32, 'b_w_in': _jnp.float32, 'b_sinks': _jnp.float32, 'b_w_out': _jnp.float32, 'c_w_in': _jnp.float32, 'c_q_norm': _jnp.float32, 'c_kv_norm': _jnp.float32, 'c_w_uq': _jnp.float32, 'c_w_ukv': _jnp.float32, 'c_w_out': _jnp.float32, 'd_w_in': _jnp.float32, 'd_ln_g': _jnp.float32, 'd_ln_b': _jnp.float32, 'd_w_s': _jnp.float32, 'd_b_s': _jnp.float32, 'd_w_out': _jnp.float32}
MOMENT_SCALE = {'pre_norm': 1.478622e+00, 'post_norm': 1.624690e+01, 'rel_bias': 5.177320e-01, 'a_w_in': 8.277101e-01, 'a_lam_re': 5.659082e-02, 'a_lam_im': 5.140396e-02, 'a_log_dt': 3.373168e+01, 'a_b_re': 3.631427e-02, 'a_b_im': 3.605762e-02, 'a_c_re': 5.106996e-02, 'a_c_im': 5.195636e-02, 'a_d': 2.850426e+00, 'a_w_glu': 3.311276e-01, 'a_b_glu': 1.023759e+00, 'a_w_out': 2.739718e+00, 'b_w_in': 1.422478e+00, 'b_sinks': 1.764785e-01, 'b_w_out': 1.946881e+00, 'c_w_in': 1.289128e+00, 'c_q_norm': 2.717629e-01, 'c_kv_norm': 3.181578e+00, 'c_w_uq': 1.907918e-01, 'c_w_ukv': 1.223114e+00, 'c_w_out': 1.862286e+00, 'd_w_in': 4.853376e-01, 'd_ln_g': 9.706910e-02, 'd_ln_b': 8.873856e-02, 'd_w_s': 1.294403e-01, 'd_b_s': 1.865849e-01, 'd_w_out': 1.214007e+00}


def _to_microbatches(a, axis):
    t = _jnp.moveaxis(a, axis, 0)
    t = t.reshape((N_MICROBATCH, t.shape[0] // N_MICROBATCH) + t.shape[1:])
    return _jnp.moveaxis(t, 1, axis + 1)


def setup_inputs(seed: int = 0) -> dict:
    inp = _fwd_setup_inputs(seed)
    key = _jax.random.fold_in(_jax.random.key(seed), 7919)
    shape, _ = _output_shape()
    out = dict(inp)
    out["loss_target"] = _jax.random.normal(_jax.random.fold_in(key, 0), shape, _jnp.float32)
    for i, name in enumerate(TWIN_WEIGHTS):
        w = inp[name].astype(_jnp.float32)
        if MOMENT_SCALE is None:
            s = _jnp.sqrt(_jnp.mean(_jnp.square(w)) + 1e-30)
        else:
            s = MOMENT_SCALE[name]
        km, kv = _jax.random.split(_jax.random.fold_in(key, i + 1))
        out[name] = w
        out["m_" + name] = s * _jax.random.normal(km, w.shape, _jnp.float32)
        out["v_" + name] = (s * s) * _jax.random.uniform(kv, w.shape, _jnp.float32, 0.5, 1.5)
    if N_MICROBATCH > 1:
        for name, axis in PER_EXAMPLE_BATCH_AXIS.items():
            out[name] = _to_microbatches(out[name], axis)
    return {'x': out['x'], 'pre_norm': out['pre_norm'], 'post_norm': out['post_norm'], 'rel_bias': out['rel_bias'], 'a_w_in': out['a_w_in'], 'a_lam_re': out['a_lam_re'], 'a_lam_im': out['a_lam_im'], 'a_log_dt': out['a_log_dt'], 'a_b_re': out['a_b_re'], 'a_b_im': out['a_b_im'], 'a_c_re': out['a_c_re'], 'a_c_im': out['a_c_im'], 'a_d': out['a_d'], 'a_w_glu': out['a_w_glu'], 'a_b_glu': out['a_b_glu'], 'a_w_out': out['a_w_out'], 'b_w_in': out['b_w_in'], 'b_sinks': out['b_sinks'], 'b_w_out': out['b_w_out'], 'c_w_in': out['c_w_in'], 'c_q_norm': out['c_q_norm'], 'c_kv_norm': out['c_kv_norm'], 'c_w_uq': out['c_w_uq'], 'c_w_ukv': out['c_w_ukv'], 'c_w_out': out['c_w_out'], 'd_w_in': out['d_w_in'], 'd_ln_g': out['d_ln_g'], 'd_ln_b': out['d_ln_b'], 'd_w_s': out['d_w_s'], 'd_b_s': out['d_b_s'], 'd_w_out': out['d_w_out'], 'loss_target': out['loss_target'], 'm_pre_norm': out['m_pre_norm'], 'm_post_norm': out['m_post_norm'], 'm_rel_bias': out['m_rel_bias'], 'm_a_w_in': out['m_a_w_in'], 'm_a_lam_re': out['m_a_lam_re'], 'm_a_lam_im': out['m_a_lam_im'], 'm_a_log_dt': out['m_a_log_dt'], 'm_a_b_re': out['m_a_b_re'], 'm_a_b_im': out['m_a_b_im'], 'm_a_c_re': out['m_a_c_re'], 'm_a_c_im': out['m_a_c_im'], 'm_a_d': out['m_a_d'], 'm_a_w_glu': out['m_a_w_glu'], 'm_a_b_glu': out['m_a_b_glu'], 'm_a_w_out': out['m_a_w_out'], 'm_b_w_in': out['m_b_w_in'], 'm_b_sinks': out['m_b_sinks'], 'm_b_w_out': out['m_b_w_out'], 'm_c_w_in': out['m_c_w_in'], 'm_c_q_norm': out['m_c_q_norm'], 'm_c_kv_norm': out['m_c_kv_norm'], 'm_c_w_uq': out['m_c_w_uq'], 'm_c_w_ukv': out['m_c_w_ukv'], 'm_c_w_out': out['m_c_w_out'], 'm_d_w_in': out['m_d_w_in'], 'm_d_ln_g': out['m_d_ln_g'], 'm_d_ln_b': out['m_d_ln_b'], 'm_d_w_s': out['m_d_w_s'], 'm_d_b_s': out['m_d_b_s'], 'm_d_w_out': out['m_d_w_out'], 'v_pre_norm': out['v_pre_norm'], 'v_post_norm': out['v_post_norm'], 'v_rel_bias': out['v_rel_bias'], 'v_a_w_in': out['v_a_w_in'], 'v_a_lam_re': out['v_a_lam_re'], 'v_a_lam_im': out['v_a_lam_im'], 'v_a_log_dt': out['v_a_log_dt'], 'v_a_b_re': out['v_a_b_re'], 'v_a_b_im': out['v_a_b_im'], 'v_a_c_re': out['v_a_c_re'], 'v_a_c_im': out['v_a_c_im'], 'v_a_d': out['v_a_d'], 'v_a_w_glu': out['v_a_w_glu'], 'v_a_b_glu': out['v_a_b_glu'], 'v_a_w_out': out['v_a_w_out'], 'v_b_w_in': out['v_b_w_in'], 'v_b_sinks': out['v_b_sinks'], 'v_b_w_out': out['v_b_w_out'], 'v_c_w_in': out['v_c_w_in'], 'v_c_q_norm': out['v_c_q_norm'], 'v_c_kv_norm': out['v_c_kv_norm'], 'v_c_w_uq': out['v_c_w_uq'], 'v_c_w_ukv': out['v_c_w_ukv'], 'v_c_w_out': out['v_c_w_out'], 'v_d_w_in': out['v_d_w_in'], 'v_d_ln_g': out['v_d_ln_g'], 'v_d_ln_b': out['v_d_ln_b'], 'v_d_w_s': out['v_d_w_s'], 'v_d_b_s': out['v_d_b_s'], 'v_d_w_out': out['v_d_w_out']}


def _loss(weights, diff, rest, loss_target):
    with _jax.named_scope("forward"):
        args = {**rest, TWIN_DIFF_INPUT: diff, **{k: w.astype(_WEIGHT_DTYPES[k]) for k, w in weights.items()}}
        y = _forward(args)
    with _jax.named_scope("loss_head"):
        err = _jnp.square(y.astype(_jnp.float32) - loss_target)
        return 0.5 * _jnp.sum(_jnp.mean(err, axis=-1)) if err.ndim else 0.5 * err


def _adamw(w, g, m, v):
    m = ADAM_B1 * m + (1.0 - ADAM_B1) * g
    v = ADAM_B2 * v + (1.0 - ADAM_B2) * _jnp.square(g)
    m_hat = m / (1.0 - ADAM_B1 ** ADAM_STEP)
    v_hat = v / (1.0 - ADAM_B2 ** ADAM_STEP)
    delta = -ADAM_LR * (m_hat / (_jnp.sqrt(v_hat) + ADAM_EPS) + ADAM_WD * w)
    return delta, m, v


def reference(x, pre_norm, post_norm, rel_bias, a_w_in, a_lam_re, a_lam_im, a_log_dt, a_b_re, a_b_im, a_c_re, a_c_im, a_d, a_w_glu, a_b_glu, a_w_out, b_w_in, b_sinks, b_w_out, c_w_in, c_q_norm, c_kv_norm, c_w_uq, c_w_ukv, c_w_out, d_w_in, d_ln_g, d_ln_b, d_w_s, d_b_s, d_w_out, loss_target, m_pre_norm, m_post_norm, m_rel_bias, m_a_w_in, m_a_lam_re, m_a_lam_im, m_a_log_dt, m_a_b_re, m_a_b_im, m_a_c_re, m_a_c_im, m_a_d, m_a_w_glu, m_a_b_glu, m_a_w_out, m_b_w_in, m_b_sinks, m_b_w_out, m_c_w_in, m_c_q_norm, m_c_kv_norm, m_c_w_uq, m_c_w_ukv, m_c_w_out, m_d_w_in, m_d_ln_g, m_d_ln_b, m_d_w_s, m_d_b_s, m_d_w_out, v_pre_norm, v_post_norm, v_rel_bias, v_a_w_in, v_a_lam_re, v_a_lam_im, v_a_log_dt, v_a_b_re, v_a_b_im, v_a_c_re, v_a_c_im, v_a_d, v_a_w_glu, v_a_b_glu, v_a_w_out, v_b_w_in, v_b_sinks, v_b_w_out, v_c_w_in, v_c_q_norm, v_c_kv_norm, v_c_w_uq, v_c_w_ukv, v_c_w_out, v_d_w_in, v_d_ln_g, v_d_ln_b, v_d_w_s, v_d_b_s, v_d_w_out):
    given = dict(x=x, pre_norm=pre_norm, post_norm=post_norm, rel_bias=rel_bias, a_w_in=a_w_in, a_lam_re=a_lam_re, a_lam_im=a_lam_im, a_log_dt=a_log_dt, a_b_re=a_b_re, a_b_im=a_b_im, a_c_re=a_c_re, a_c_im=a_c_im, a_d=a_d, a_w_glu=a_w_glu, a_b_glu=a_b_glu, a_w_out=a_w_out, b_w_in=b_w_in, b_sinks=b_sinks, b_w_out=b_w_out, c_w_in=c_w_in, c_q_norm=c_q_norm, c_kv_norm=c_kv_norm, c_w_uq=c_w_uq, c_w_ukv=c_w_ukv, c_w_out=c_w_out, d_w_in=d_w_in, d_ln_g=d_ln_g, d_ln_b=d_ln_b, d_w_s=d_w_s, d_b_s=d_b_s, d_w_out=d_w_out, loss_target=loss_target, m_pre_norm=m_pre_norm, m_post_norm=m_post_norm, m_rel_bias=m_rel_bias, m_a_w_in=m_a_w_in, m_a_lam_re=m_a_lam_re, m_a_lam_im=m_a_lam_im, m_a_log_dt=m_a_log_dt, m_a_b_re=m_a_b_re, m_a_b_im=m_a_b_im, m_a_c_re=m_a_c_re, m_a_c_im=m_a_c_im, m_a_d=m_a_d, m_a_w_glu=m_a_w_glu, m_a_b_glu=m_a_b_glu, m_a_w_out=m_a_w_out, m_b_w_in=m_b_w_in, m_b_sinks=m_b_sinks, m_b_w_out=m_b_w_out, m_c_w_in=m_c_w_in, m_c_q_norm=m_c_q_norm, m_c_kv_norm=m_c_kv_norm, m_c_w_uq=m_c_w_uq, m_c_w_ukv=m_c_w_ukv, m_c_w_out=m_c_w_out, m_d_w_in=m_d_w_in, m_d_ln_g=m_d_ln_g, m_d_ln_b=m_d_ln_b, m_d_w_s=m_d_w_s, m_d_b_s=m_d_b_s, m_d_w_out=m_d_w_out, v_pre_norm=v_pre_norm, v_post_norm=v_post_norm, v_rel_bias=v_rel_bias, v_a_w_in=v_a_w_in, v_a_lam_re=v_a_lam_re, v_a_lam_im=v_a_lam_im, v_a_log_dt=v_a_log_dt, v_a_b_re=v_a_b_re, v_a_b_im=v_a_b_im, v_a_c_re=v_a_c_re, v_a_c_im=v_a_c_im, v_a_d=v_a_d, v_a_w_glu=v_a_w_glu, v_a_b_glu=v_a_b_glu, v_a_w_out=v_a_w_out, v_b_w_in=v_b_w_in, v_b_sinks=v_b_sinks, v_b_w_out=v_b_w_out, v_c_w_in=v_c_w_in, v_c_q_norm=v_c_q_norm, v_c_kv_norm=v_c_kv_norm, v_c_w_uq=v_c_w_uq, v_c_w_ukv=v_c_w_ukv, v_c_w_out=v_c_w_out, v_d_w_in=v_d_w_in, v_d_ln_g=v_d_ln_g, v_d_ln_b=v_d_ln_b, v_d_w_s=v_d_w_s, v_d_b_s=v_d_b_s, v_d_w_out=v_d_w_out)
    weights = {n: given[n] for n in TWIN_WEIGHTS}
    shared = {n: given[n] for n in SHARED_INPUTS}
    per_example = {n: given[n] for n in ['x']}
    grad_fn = _jax.value_and_grad(_loss, argnums=(0, 1))

    def one_microbatch(ex, loss_target):
        ex = dict(ex)
        diff = ex.pop(TWIN_DIFF_INPUT)
        return grad_fn(weights, diff, {**shared, **ex}, loss_target)

    if N_MICROBATCH == 1:
        loss, (grad_w, grad_x) = one_microbatch(per_example, given["loss_target"])
    else:
        def body(carry, xs):
            loss_sum, grad_sum = carry
            l_k, (gw_k, gx_k) = one_microbatch(xs[0], xs[1])
            with _jax.named_scope("update"):
                return (loss_sum + l_k, _jax.tree.map(_jnp.add, grad_sum, gw_k)), gx_k

        init = (_jnp.zeros((), _jnp.float32), _jax.tree.map(_jnp.zeros_like, weights))
        (loss, grad_w), grad_x = _jax.lax.scan(body, init, (per_example, given["loss_target"]))
    with _jax.named_scope("update"):
        delta_w, new_m, new_v = {}, {}, {}
        for n in TWIN_WEIGHTS:
            delta_w[n], new_m[n], new_v[n] = _adamw(weights[n], grad_w[n], given["m_" + n], given["v_" + n])
    return (loss, grad_x, *[grad_w[n] for n in TWIN_WEIGHTS], *[delta_w[n] for n in TWIN_WEIGHTS],
            *[new_m[n] for n in TWIN_WEIGHTS], *[new_v[n] for n in TWIN_WEIGHTS])
```

```python
import functools
import math

import numpy as np
import jax
import jax.numpy as jnp
from jax import lax
from jax.experimental import pallas as pl
from jax.experimental.pallas import tpu as pltpu

F32 = jnp.float32
BF16 = jnp.bfloat16
S = jax.ShapeDtypeStruct
MESH = pl.DeviceIdType.MESH

D_MODEL = 1024
EPS = 1e-6
NEG_INF = -1e30
N_CHIPS = 4

SSM_GROUPS = 64
SSM_GROUP = 16
SSM_STATE = 64
SSM_CHUNK = 16
SSM_BLOCK = SSM_CHUNK * SSM_GROUP

HEAD_DIM = 64
SWA_HEADS = 16
SWA_GROUP = 8
WINDOW = 128
REL_BUCKETS = 32
REL_MAX_DIST = 128

MLA_HEADS = 16
MLA_NOPE = 64
MLA_ROPE = 32
MLA_Q_RANK = 768
MLA_KV_RANK = 256
MLA_SCALE = (MLA_NOPE + MLA_ROPE) ** -0.5
MLA_BLK = 256
ROPE_BASE = 10000.0

SGU_CHUNK = 128
SGU_GROUPS = 16
SGU_GDIM = 64

ADAM_LR = 0.001
ADAM_B1 = 0.9
ADAM_B2 = 0.999
ADAM_EPS = 1e-08
ADAM_WD = 0.01
ADAM_STEP = 10

VMEM_BIG = 56 * 1024 * 1024


def _cp(sem, vmem=None):
    return pltpu.CompilerParams(dimension_semantics=sem, vmem_limit_bytes=vmem)


def _row_spec(tr, c):
    return pl.BlockSpec((tr, c), lambda i: (i, 0))


def _full_spec(shape):
    nd = len(shape)
    return pl.BlockSpec(shape, lambda *_: (0,) * nd)


def _gelu(x):
    c = 0.7978845608028654
    return 0.5 * x * (1.0 + jnp.tanh(c * (x + 0.044715 * x * x * x)))


def _gelu_grad(x):
    c = 0.7978845608028654
    t = jnp.tanh(c * (x + 0.044715 * x * x * x))
    return 0.5 * (1.0 + t) + 0.5 * x * (1.0 - t * t) * c * (1.0 + 3.0 * 0.044715 * x * x)


def _silu(z):
    return z * jax.nn.sigmoid(z)


def _silu_grad(z):
    s = jax.nn.sigmoid(z)
    return s * (1.0 + z * (1.0 - s))


def _dot(a, b):
    return lax.dot_general(a, b, (((1,), (0,)), ((), ())), preferred_element_type=F32)


def _dot_nt(a, b):
    return lax.dot_general(a, b, (((1,), (1,)), ((), ())), preferred_element_type=F32)


def _dot_tn(a, b):
    return lax.dot_general(a, b, (((0,), (0,)), ((), ())), preferred_element_type=F32)


def _pick(n, cap):
    best = 0
    for d in range(128, min(n, cap) + 1, 128):
        if n % d == 0:
            best = d
    if best < 256:
        return n
    return best


def _mm(a, b, *, ta=False, tb=False, out_dtype=F32, name):
    m, k = (a.shape[1], a.shape[0]) if ta else a.shape
    n = b.shape[0] if tb else b.shape[1]
    assert (b.shape[1] if tb else b.shape[0]) == k
    tm, tn, tk = _pick(m, 512), _pick(n, 512), _pick(k, 512)
    if tn > 1024:
        tm = _pick(m, 256)
    if tk > 1024:
        tm, tn = _pick(m, 256), _pick(n, 256)
    nk = k // tk
    dims = (((0 if ta else 1,), (1 if tb else 0,)), ((), ()))

    def body(a_ref, b_ref, o_ref, acc_ref):
        kk = pl.program_id(2)

        @pl.when(kk == 0)
        def _():
            acc_ref[...] = jnp.zeros_like(acc_ref)

        acc_ref[...] += lax.dot_general(a_ref[...].astype(BF16), b_ref[...].astype(BF16), dims,
                                        preferred_element_type=F32)

        @pl.when(kk == nk - 1)
        def _():
            o_ref[...] = acc_ref[...].astype(out_dtype)

    a_spec = pl.BlockSpec((tk, tm), lambda i, j, kk: (kk, i)) if ta else pl.BlockSpec((tm, tk), lambda i, j, kk: (i, kk))
    b_spec = pl.BlockSpec((tn, tk), lambda i, j, kk: (j, kk)) if tb else pl.BlockSpec((tk, tn), lambda i, j, kk: (kk, j))
    return pl.pallas_call(
        body, grid=(m // tm, n // tn, nk), in_specs=[a_spec, b_spec],
        out_specs=pl.BlockSpec((tm, tn), lambda i, j, kk: (i, j)), out_shape=S((m, n), out_dtype),
        scratch_shapes=[pltpu.VMEM((tm, tn), F32)], name=name,
        compiler_params=_cp(("parallel", "parallel", "arbitrary"), VMEM_BIG))(a, b)


def _prenorm(x, g, name):
    t, d = x.shape
    tr = 256

    def body(x_ref, g_ref, o_ref):
        xv = x_ref[...]
        r = lax.rsqrt(jnp.mean(xv * xv, axis=-1, keepdims=True) + EPS)
        o_ref[...] = (xv * r * g_ref[...]).astype(BF16)

    return pl.pallas_call(body, grid=(t // tr,), in_specs=[_row_spec(tr, d), _full_spec((1, d))],
                          out_specs=_row_spec(tr, d), out_shape=S((t, d), BF16), name=name,
                          compiler_params=_cp(("parallel",)))(x, g)


def _postnorm(x, y, g, name):
    t, d = x.shape
    tr = 256

    def body(x_ref, y_ref, g_ref, o_ref):
        yv = y_ref[...]
        r = lax.rsqrt(jnp.mean(yv * yv, axis=-1, keepdims=True) + EPS)
        o_ref[...] = x_ref[...] + yv * r * g_ref[...]

    return pl.pallas_call(body, grid=(t // tr,), in_specs=[_row_spec(tr, d), _row_spec(tr, d), _full_spec((1, d))],
                          out_specs=_row_spec(tr, d), out_shape=S((t, d), F32), name=name,
                          compiler_params=_cp(("parallel",)))(x, y, g)


def _rms_bwd_rows(dout, xin, g):
    r = lax.rsqrt(jnp.mean(xin * xin, axis=-1, keepdims=True) + EPS)
    xh = xin * r
    gd = dout * g
    dx = r * (gd - xh * jnp.mean(gd * xh, axis=-1, keepdims=True))
    return dx, jnp.sum(dout * xh, axis=0, keepdims=True)


def _postnorm_bwd(dxn, y, g, name):
    t, d = y.shape
    tr = 256

    def body(dx_ref, y_ref, g_ref, dy_ref, dg_ref):
        @pl.when(pl.program_id(0) == 0)
        def _():
            dg_ref[...] = jnp.zeros_like(dg_ref)

        dy, dg = _rms_bwd_rows(dx_ref[...], y_ref[...], g_ref[...])
        dy_ref[...] = dy.astype(BF16)
        dg_ref[...] += dg

    return pl.pallas_call(body, grid=(t // tr,), in_specs=[_row_spec(tr, d), _row_spec(tr, d), _full_spec((1, d))],
                          out_specs=[_row_spec(tr, d), _full_spec((1, d))],
                          out_shape=[S((t, d), BF16), S((1, d), F32)], name=name,
                          compiler_params=_cp(("arbitrary",)))(dxn, y, g)


def _prenorm_bwd(dxn, dh, x, g, name):
    t, d = x.shape
    tr = 256

    def body(dxn_ref, dh_ref, x_ref, g_ref, dx_ref, dg_ref):
        @pl.when(pl.program_id(0) == 0)
        def _():
            dg_ref[...] = jnp.zeros_like(dg_ref)

        dx, dg = _rms_bwd_rows(dh_ref[...], x_ref[...], g_ref[...])
        dx_ref[...] = dxn_ref[...] + dx
        dg_ref[...] += dg

    return pl.pallas_call(body, grid=(t // tr,),
                          in_specs=[_row_spec(tr, d), _row_spec(tr, d), _row_spec(tr, d), _full_spec((1, d))],
                          out_specs=[_row_spec(tr, d), _full_spec((1, d))],
                          out_shape=[S((t, d), F32), S((1, d), F32)], name=name,
                          compiler_params=_cp(("arbitrary",)))(dxn, dh, x, g)


def _loss_head(y, target):
    t, d = y.shape
    tr = 256

    def body(y_ref, t_ref, l_ref, dy_ref):
        @pl.when(pl.program_id(0) == 0)
        def _():
            l_ref[...] = jnp.zeros_like(l_ref)

        e = y_ref[...] - t_ref[...]
        dy_ref[...] = e * (1.0 / d)
        l_ref[...] += 0.5 * jnp.sum(jnp.mean(e * e, axis=-1, keepdims=True), axis=0, keepdims=True)

    return pl.pallas_call(body, grid=(t // tr,), in_specs=[_row_spec(tr, d), _row_spec(tr, d)],
                          out_specs=[_full_spec((1, 1)), _row_spec(tr, d)],
                          out_shape=[S((1, 1), F32), S((t, d), F32)], name="loss_head",
                          compiler_params=_cp(("arbitrary",)))(y, target)


def _sgu_parts(proj, lg, lb, ws_ref, bst):
    a = proj[:, 0:1024]
    bb = proj[:, 1024:2048]
    z = proj[:, 2048:3072]
    u = _gelu(a)
    vp = _gelu(bb)
    mu = jnp.mean(vp, axis=-1, keepdims=True)
    xc = vp - mu
    rstd = lax.rsqrt(jnp.mean(xc * xc, axis=-1, keepdims=True) + EPS)
    xh = xc * rstd
    vn = xh * lg + lb
    row = lax.broadcasted_iota(jnp.int32, (SGU_CHUNK, SGU_CHUNK), 0)
    col = lax.broadcasted_iota(jnp.int32, (SGU_CHUNK, SGU_CHUNK), 1)
    tril = row >= col
    ws = [jnp.where(tril, ws_ref[g], 0.0).astype(BF16) for g in range(SGU_GROUPS)]
    s = jnp.concatenate(
        [_dot(ws[g], vn[:, g * 64:(g + 1) * 64].astype(BF16)) + bst[:, g:g + 1] for g in range(SGU_GROUPS)], axis=1)
    return a, bb, z, u, vp, rstd, xh, vn, ws, tril, s


def _sgu_fwd(proj, lg, lb, w_s, bst):
    t = proj.shape[0]
    tr = SGU_CHUNK

    def body(p_ref, lg_ref, lb_ref, ws_ref, bst_ref, o_ref):
        _, _, z, u, _, _, _, _, _, _, s = _sgu_parts(p_ref[...], lg_ref[...], lb_ref[...], ws_ref, bst_ref[...])
        o_ref[...] = (u * s * _silu(z)).astype(BF16)

    return pl.pallas_call(
        body, grid=(t // tr,),
        in_specs=[_row_spec(tr, 3072), _full_spec((1, 1024)), _full_spec((1, 1024)), _full_spec((16, 128, 128)),
                  _full_spec((128, 16))],
        out_specs=_row_spec(tr, 1024), out_shape=S((t, 1024), BF16), name="sgu_fwd",
        compiler_params=_cp(("parallel",)))(proj, lg, lb, w_s, bst)


def _sgu_bwd(proj, dp, lg, lb, w_s, bst):
    t = proj.shape[0]
    tr = SGU_CHUNK

    def body(p_ref, dp_ref, lg_ref, lb_ref, ws_ref, bst_ref, dproj_ref, dws_ref, dbst_ref, dlg_ref, dlb_ref):
        @pl.when(pl.program_id(0) == 0)
        def _():
            dws_ref[...] = jnp.zeros_like(dws_ref)
            dbst_ref[...] = jnp.zeros_like(dbst_ref)
            dlg_ref[...] = jnp.zeros_like(dlg_ref)
            dlb_ref[...] = jnp.zeros_like(dlb_ref)

        lgv = lg_ref[...]
        a, bb, z, u, vp, rstd, xh, vn, ws, tril, s = _sgu_parts(p_ref[...], lgv, lb_ref[...], ws_ref, bst_ref[...])
        dpv = dp_ref[...].astype(F32)
        sz = _silu(z)
        du = dpv * s * sz
        ds = dpv * u * sz
        dz = dpv * u * s * _silu_grad(z)
        dvn_parts, dbs_cols = [], []
        for g in range(SGU_GROUPS):
            dsg = ds[:, g * 64:(g + 1) * 64]
            dsg_b = dsg.astype(BF16)
            dvn_parts.append(_dot_tn(ws[g], dsg_b))
            dws_ref[g] += jnp.where(tril, _dot_nt(dsg_b, vn[:, g * 64:(g + 1) * 64].astype(BF16)), 0.0)
            dbs_cols.append(jnp.sum(dsg, axis=1, keepdims=True))
        dvn = jnp.concatenate(dvn_parts, axis=1)
        dbst_ref[...] += jnp.concatenate(dbs_cols, axis=1)
        dlg_ref[...] += jnp.sum(dvn * xh, axis=0, keepdims=True)
        dlb_ref[...] += jnp.sum(dvn, axis=0, keepdims=True)
        dxh = dvn * lgv
        dvp = rstd * (dxh - jnp.mean(dxh, axis=-1, keepdims=True) - xh * jnp.mean(dxh * xh, axis=-1, keepdims=True))
        dproj_ref[:, 0:1024] = (du * _gelu_grad(a)).astype(BF16)
        dproj_ref[:, 1024:2048] = (dvp * _gelu_grad(bb)).astype(BF16)
        dproj_ref[:, 2048:3072] = dz.astype(BF16)

    return pl.pallas_call(
        body, grid=(t // tr,),
        in_specs=[_row_spec(tr, 3072), _row_spec(tr, 1024), _full_spec((1, 1024)), _full_spec((1, 1024)),
                  _full_spec((16, 128, 128)), _full_spec((128, 16))],
        out_specs=[_row_spec(tr, 3072), _full_spec((16, 128, 128)), _full_spec((128, 16)), _full_spec((1, 1024)),
                   _full_spec((1, 1024))],
        out_shape=[S((t, 3072), BF16), S((16, 128, 128), F32), S((128, 16), F32), S((1, 1024), F32),
                   S((1, 1024), F32)],
        name="sgu_bwd", compiler_params=_cp(("arbitrary",)))(proj, dp, lg, lb, w_s, bst)


def _t5_bucket_table():
    qi = np.arange(WINDOW)[:, None]
    kj = np.arange(2 * WINDOW)[None, :]
    dist = np.maximum(qi + WINDOW - kj, 0)
    max_exact = REL_BUCKETS // 2
    dist_f = np.maximum(dist, 1).astype(np.float32)
    large = max_exact + (np.log(dist_f / np.float32(max_exact)) / np.float32(math.log(REL_MAX_DIST / max_exact))
                         * np.float32(REL_BUCKETS - max_exact)).astype(np.int32)
    large = np.minimum(large, REL_BUCKETS - 1)
    return np.where(dist < max_exact, dist, large).astype(np.int32)


def _swa_bias(rel_bias, bucket):
    def body(rb_ref, bk_ref, o_ref):
        bk = bk_ref[...]
        for h in range(SWA_HEADS):
            acc = jnp.zeros((WINDOW, 2 * WINDOW), F32)
            for b in range(REL_BUCKETS):
                acc = jnp.where(bk == b, rb_ref[b, h], acc)
            o_ref[h] = acc

    return pl.pallas_call(
        body, in_specs=[pl.BlockSpec(memory_space=pltpu.SMEM), pl.BlockSpec(memory_space=pltpu.VMEM)],
        out_specs=pl.BlockSpec(memory_space=pltpu.VMEM), out_shape=S((SWA_HEADS, WINDOW, 2 * WINDOW), F32),
        name="swa_bias")(rel_bias, bucket)


def _swa_dbias(ds_sum, dsink_rows, bucket):
    def body(ds_ref, dsk_ref, bk_ref, drb_ref, dsink_ref):
        bk = bk_ref[...]
        r = lax.broadcasted_iota(jnp.int32, (REL_BUCKETS, SWA_HEADS), 0)
        c = lax.broadcasted_iota(jnp.int32, (REL_BUCKETS, SWA_HEADS), 1)
        out = jnp.zeros((REL_BUCKETS, SWA_HEADS), F32)
        for b in range(REL_BUCKETS):
            m = bk == b
            for h in range(SWA_HEADS):
                val = jnp.sum(jnp.where(m, ds_ref[h], 0.0))
                out = jnp.where((r == b) & (c == h), val, out)
        drb_ref[...] = out
        dsink_ref[...] = jnp.sum(dsk_ref[...], axis=0, keepdims=True)

    return pl.pallas_call(
        body, out_shape=[S((REL_BUCKETS, SWA_HEADS), F32), S((1, SWA_HEADS), F32)], name="swa_dbias",
    )(ds_sum, dsink_rows, bucket)


def _swa_band(cur_ref, prev_ref):
    kband = jnp.concatenate([prev_ref[:, 0:128], cur_ref[:, 1024:1152]], axis=0).astype(BF16)
    vband = jnp.concatenate([prev_ref[:, 128:256], cur_ref[:, 1152:1280]], axis=0).astype(BF16)
    return kband, vband


def _swa_valid(n):
    qi = lax.broadcasted_iota(jnp.int32, (WINDOW, 2 * WINDOW), 0)
    kj = lax.broadcasted_iota(jnp.int32, (WINDOW, 2 * WINDOW), 1)
    dist = qi + WINDOW - kj
    return (dist >= 0) & (dist < WINDOW) & (n * WINDOW + kj - WINDOW >= 0)


def _swa_probs(qh, kh, bias_h, sink, valid):
    s = _dot_nt(qh, kh) * (HEAD_DIM ** -0.5) + bias_h
    s = jnp.where(valid, s, NEG_INF)
    m = jnp.maximum(jnp.max(s, axis=-1, keepdims=True), sink)
    e = jnp.exp(s - m)
    es = jnp.exp(sink - m)
    inv = 1.0 / (jnp.sum(e, axis=-1, keepdims=True) + es)
    return e * inv, es * inv


def _swa_in_specs():
    return [_row_spec(WINDOW, 2304),
            pl.BlockSpec((WINDOW, 256), lambda n: (jnp.maximum(n - 1, 0), 4)),
            _full_spec((SWA_HEADS, WINDOW, 2 * WINDOW)),
            pl.BlockSpec(memory_space=pltpu.SMEM)]


def _swa_fwd(proj, bias, sinks):
    t = proj.shape[0]

    def body(cur_ref, prev_ref, bias_ref, sink_ref, o_ref):
        n = pl.program_id(0)
        kband, vband = _swa_band(cur_ref, prev_ref)
        valid = _swa_valid(n)
        outs = []
        for h in range(SWA_HEADS):
            j = h // SWA_GROUP
            qh = cur_ref[:, h * 64:(h + 1) * 64].astype(BF16)
            p, _ = _swa_probs(qh, kband[:, j * 64:(j + 1) * 64], bias_ref[h], sink_ref[0, h], valid)
            outs.append(_dot(p.astype(BF16), vband[:, j * 64:(j + 1) * 64]))
        o = jnp.concatenate(outs, axis=1)
        o_ref[...] = (o * _silu(cur_ref[:, 1280:2304])).astype(BF16)

    return pl.pallas_call(body, grid=(t // WINDOW,), in_specs=_swa_in_specs(), out_specs=_row_spec(WINDOW, 1024),
                          out_shape=S((t, 1024), BF16), name="swa_fwd",
                          compiler_params=_cp(("parallel",)))(proj, proj, bias, sinks)


def _swa_bwd(proj, dp, bias, sinks):
    t = proj.shape[0]
    nb = t // WINDOW

    def body(cur_ref, prev_ref, bias_ref, sink_ref, dp_ref, dqz_ref, dkv_ref, dss_ref, dsk_ref):
        n = pl.program_id(0)

        @pl.when(n == 0)
        def _():
            dkv_ref[...] = jnp.zeros_like(dkv_ref)
            dss_ref[...] = jnp.zeros_like(dss_ref)
            dsk_ref[...] = jnp.zeros_like(dsk_ref)

        kband, vband = _swa_band(cur_ref, prev_ref)
        valid = _swa_valid(n)
        z = cur_ref[:, 1280:2304]
        dpv = dp_ref[...].astype(F32)
        do = dpv * _silu(z)
        outs, dqs, dsinks = [], [], []
        dk = [jnp.zeros((2 * WINDOW, 64), F32) for _ in range(2)]
        dv = [jnp.zeros((2 * WINDOW, 64), F32) for _ in range(2)]
        for h in range(SWA_HEADS):
            j = h // SWA_GROUP
            qh = cur_ref[:, h * 64:(h + 1) * 64].astype(BF16)
            kh = kband[:, j * 64:(j + 1) * 64]
            vh = vband[:, j * 64:(j + 1) * 64]
            p, ps = _swa_probs(qh, kh, bias_ref[h], sink_ref[0, h], valid)
            pb = p.astype(BF16)
            outs.append(_dot(pb, vh))
            doh = do[:, h * 64:(h + 1) * 64].astype(BF16)
            dpr = _dot_nt(doh, vh)
            dlt = jnp.sum(p * dpr, axis=-1, keepdims=True)
            ds = p * (dpr - dlt)
            dsinks.append(-ps * dlt)
            dss_ref[h] += ds
            dsb = (ds * (HEAD_DIM ** -0.5)).astype(BF16)
            dqs.append(_dot(dsb, kh))
            dk[j] = dk[j] + _dot_tn(dsb, qh)
            dv[j] = dv[j] + _dot_tn(pb, doh)
        o = jnp.concatenate(outs, axis=1)
        dqz_ref[:, 0:1024] = jnp.concatenate(dqs, axis=1).astype(BF16)
        dqz_ref[:, 1024:2048] = (dpv * o * _silu_grad(z)).astype(BF16)
        dsk_ref[...] += jnp.concatenate(dsinks, axis=1)
        dband = jnp.concatenate([dk[0], dk[1], dv[0], dv[1]], axis=1)
        prow = pl.multiple_of(jnp.maximum(n - 1, 0) * WINDOW, WINDOW)
        dkv_ref[pl.ds(prow, WINDOW), :] += dband[0:WINDOW]
        crow = pl.multiple_of(n * WINDOW, WINDOW)
        dkv_ref[pl.ds(crow, WINDOW), :] += dband[WINDOW:]

    return pl.pallas_call(
        body, grid=(nb,), in_specs=_swa_in_specs() + [_row_spec(WINDOW, 1024)],
        out_specs=[_row_spec(WINDOW, 2048), _full_spec((t, 256)), _full_spec((SWA_HEADS, WINDOW, 2 * WINDOW)),
                   _full_spec((WINDOW, SWA_HEADS))],
        out_shape=[S((t, 2048), BF16), S((t, 256), F32), S((SWA_HEADS, WINDOW, 2 * WINDOW), F32),
                   S((WINDOW, SWA_HEADS), F32)],
        name="swa_bwd", compiler_params=_cp(("arbitrary",)))(proj, proj, bias, sinks, dp)


MLA_PROJ = MLA_Q_RANK + MLA_KV_RANK + 128 + 1024
MLA_Z0 = MLA_Q_RANK + MLA_KV_RANK + 128


def _rope_tables(t):
    inv = ROPE_BASE ** (-jnp.arange(0, MLA_ROPE, 2, dtype=F32) / MLA_ROPE)
    ang = jnp.arange(t, dtype=F32)[:, None] * inv[None, :]
    return jnp.tile(jnp.cos(ang), (1, MLA_HEADS)), jnp.tile(jnp.sin(ang), (1, MLA_HEADS))


def _mla_norms(proj, gq, gkv):
    t = proj.shape[0]
    tr = 256

    def body(p_ref, gq_ref, gkv_ref, q_ref, kv_ref):
        for lo, hi, g_ref, o_ref in ((0, 768, gq_ref, q_ref), (768, 1024, gkv_ref, kv_ref)):
            xv = p_ref[:, lo:hi]
            r = lax.rsqrt(jnp.mean(xv * xv, axis=-1, keepdims=True) + EPS)
            o_ref[...] = (xv * r * g_ref[...]).astype(BF16)

    return pl.pallas_call(
        body, grid=(t // tr,), in_specs=[_row_spec(tr, 1024), _full_spec((1, 768)), _full_spec((1, 256))],
        out_specs=[_row_spec(tr, 768), _row_spec(tr, 256)], out_shape=[S((t, 768), BF16), S((t, 256), BF16)],
        name="mla_norms", compiler_params=_cp(("parallel",)))(proj, gq, gkv)


def _mla_norms_bwd(proj, dcqn, dckvn, dkr, dz, gq, gkv):
    t = proj.shape[0]
    tr = 256

    def body(p_ref, dq_ref, dkv_ref, dkr_ref, dz_ref, gq_ref, gkv_ref, dproj_ref, dgq_ref, dgkv_ref):
        @pl.when(pl.program_id(0) == 0)
        def _():
            dgq_ref[...] = jnp.zeros_like(dgq_ref)
            dgkv_ref[...] = jnp.zeros_like(dgkv_ref)

        dx, dg = _rms_bwd_rows(dq_ref[...], p_ref[:, 0:768], gq_ref[...])
        dproj_ref[:, 0:768] = dx.astype(BF16)
        dgq_ref[...] += dg
        dx, dg = _rms_bwd_rows(dkv_ref[...], p_ref[:, 768:1024], gkv_ref[...])
        dproj_ref[:, 768:1024] = dx.astype(BF16)
        dgkv_ref[...] += dg
        dproj_ref[:, 1024:MLA_Z0] = dkr_ref[...].astype(BF16)
        dproj_ref[:, MLA_Z0:MLA_PROJ] = dz_ref[...]

    return pl.pallas_call(
        body, grid=(t // tr,),
        in_specs=[_row_spec(tr, 1024), _row_spec(tr, 768), _row_spec(tr, 256), _row_spec(tr, 128), _row_spec(tr, 1024),
                  _full_spec((1, 768)), _full_spec((1, 256))],
        out_specs=[_row_spec(tr, MLA_PROJ), _full_spec((1, 768)), _full_spec((1, 256))],
        out_shape=[S((t, MLA_PROJ), BF16), S((1, 768), F32), S((1, 256), F32)],
        name="mla_norms_bwd", compiler_params=_cp(("arbitrary",)))(proj, dcqn, dckvn, dkr, dz, gq, gkv)


def _mla_pack(qf, kvf, proj, cos, sin):
    t = qf.shape[0]
    tr = 256

    def body(q_ref, kn_ref, kr_ref, cos_ref, sin_ref, qh_ref, kh_ref):
        cs, sn = cos_ref[...], sin_ref[...]
        x1, x2 = q_ref[:, 1024:1280], q_ref[:, 1280:1536]
        r1 = x1 * cs - x2 * sn
        r2 = x2 * cs + x1 * sn
        c16, s16 = cs[:, 0:16], sn[:, 0:16]
        k1, k2 = kr_ref[:, 0:16], kr_ref[:, 16:32]
        kr1 = k1 * c16 - k2 * s16
        kr2 = k2 * c16 + k1 * s16
        zpad = jnp.zeros((tr, 32), F32)
        for h in range(MLA_HEADS):
            qh_ref[h] = jnp.concatenate([q_ref[:, h * 64:(h + 1) * 64], r1[:, h * 16:(h + 1) * 16],
                                         r2[:, h * 16:(h + 1) * 16], zpad], axis=1).astype(BF16)
            kh_ref[h] = jnp.concatenate([kn_ref[:, h * 64:(h + 1) * 64], kr1, kr2, zpad], axis=1).astype(BF16)

    hspec = pl.BlockSpec((MLA_HEADS, tr, 128), lambda i: (0, i, 0))
    return pl.pallas_call(
        body, grid=(t // tr,),
        in_specs=[_row_spec(tr, 1536), _row_spec(tr, 1024), pl.BlockSpec((tr, 128), lambda i: (i, 8)),
                  _row_spec(tr, 256), _row_spec(tr, 256)],
        out_specs=[hspec, hspec], out_shape=[S((MLA_HEADS, t, 128), BF16)] * 2,
        name="mla_pack", compiler_params=_cp(("parallel",)))(qf, kvf, proj, cos, sin)


def _mla_unpack(dqh, dkh, dv, cos, sin):
    t = dqh.shape[1]
    tr = 256

    def body(dqh_ref, dkh_ref, dv_ref, cos_ref, sin_ref, dq_ref, dkv_ref, dkr_ref):
        cs, sn = cos_ref[...], sin_ref[...]
        dqn = jnp.concatenate([dqh_ref[h, :, 0:64] for h in range(MLA_HEADS)], axis=1)
        dr1 = jnp.concatenate([dqh_ref[h, :, 64:80] for h in range(MLA_HEADS)], axis=1)
        dr2 = jnp.concatenate([dqh_ref[h, :, 80:96] for h in range(MLA_HEADS)], axis=1)
        dq_ref[:, 0:1024] = dqn.astype(BF16)
        dq_ref[:, 1024:1280] = (dr1 * cs + dr2 * sn).astype(BF16)
        dq_ref[:, 1280:1536] = (dr2 * cs - dr1 * sn).astype(BF16)
        dkv_ref[:, 0:1024] = jnp.concatenate([dkh_ref[h, :, 0:64] for h in range(MLA_HEADS)], axis=1).astype(BF16)
        dkv_ref[:, 1024:2048] = dv_ref[...].astype(BF16)
        d1 = dkh_ref[0, :, 64:80]
        d2 = dkh_ref[0, :, 80:96]
        for h in range(1, MLA_HEADS):
            d1 = d1 + dkh_ref[h, :, 64:80]
            d2 = d2 + dkh_ref[h, :, 80:96]
        c16, s16 = cs[:, 0:16], sn[:, 0:16]
        dkr_ref[...] = jnp.concatenate([d1 * c16 + d2 * s16, d2 * c16 - d1 * s16, jnp.zeros((tr, 96), F32)], axis=1)

    hspec = pl.BlockSpec((MLA_HEADS, tr, 128), lambda i: (0, i, 0))
    return pl.pallas_call(
        body, grid=(t // tr,), in_specs=[hspec, hspec, _row_spec(tr, 1024), _row_spec(tr, 256), _row_spec(tr, 256)],
        out_specs=[_row_spec(tr, 1536), _row_spec(tr, 2048), _row_spec(tr, 128)],
        out_shape=[S((t, 1536), BF16), S((t, 2048), BF16), S((t, 128), F32)],
        name="mla_unpack", compiler_params=_cp(("parallel",)))(dqh, dkh, dv, cos, sin)


def _mla_attn_fwd(qh, kh, kvf, proj):
    t = qh.shape[1]
    b = MLA_BLK

    def body(q_ref, k_ref, v_ref, z_ref, o_ref, p_ref, lse_ref):
        r = pl.program_id(1)
        r0 = r * b
        row = lax.broadcasted_iota(jnp.int32, (b, b), 0)
        col = lax.broadcasted_iota(jnp.int32, (b, b), 1)
        outs, lses = [], []
        for a in range(2):
            q = q_ref[a]

            def kstep(c, carry, a=a, q=q):
                m, l, acc = carry
                c0 = pl.multiple_of(c * b, b)
                k = k_ref[a, pl.ds(c0, b), :]
                v = v_ref[pl.ds(c0, b), a * 64:(a + 1) * 64].astype(BF16)
                s = _dot_nt(q, k) * MLA_SCALE
                s = jnp.where(c0 + col <= r0 + row, s, NEG_INF)
                mn = jnp.maximum(m, jnp.max(s, axis=-1, keepdims=True))
                alpha = jnp.exp(m - mn)
                p = jnp.exp(s - mn)
                return mn, alpha * l + jnp.sum(p, axis=-1, keepdims=True), alpha * acc + _dot(p.astype(BF16), v)

            m, l, acc = lax.fori_loop(0, r + 1, kstep, (jnp.full((b, 1), NEG_INF, F32), jnp.zeros((b, 1), F32),
                                                        jnp.zeros((b, 64), F32)))
            outs.append(acc / l)
            lses.append(jnp.broadcast_to(m + jnp.log(l), (b, 64)))
        o = jnp.concatenate(outs, axis=1)
        o_ref[...] = o
        p_ref[...] = (o * _silu(z_ref[...])).astype(BF16)
        lse_ref[0] = jnp.concatenate(lses, axis=1)

    return pl.pallas_call(
        body, grid=(MLA_HEADS // 2, t // b),
        in_specs=[pl.BlockSpec((2, b, 128), lambda i, r: (i, r, 0)), pl.BlockSpec((2, t, 128), lambda i, r: (i, 0, 0)),
                  pl.BlockSpec((t, 128), lambda i, r: (0, 8 + i)), pl.BlockSpec((b, 128), lambda i, r: (r, 9 + i))],
        out_specs=[pl.BlockSpec((b, 128), lambda i, r: (r, i)), pl.BlockSpec((b, 128), lambda i, r: (r, i)),
                   pl.BlockSpec((1, b, 128), lambda i, r: (i, r, 0))],
        out_shape=[S((t, 1024), F32), S((t, 1024), BF16), S((MLA_HEADS // 2, t, 128), F32)],
        name="mla_attn_fwd", compiler_params=_cp(("parallel", "parallel")))(qh, kh, kvf, proj)


def _mla_attn_bwd(qh, kh, kvf, proj, o, dpout, lse):
    t = qh.shape[1]
    b = MLA_BLK
    nb = t // b

    def body(q_ref, k_ref, v_ref, z_ref, o_ref, dp_ref, lse_ref, dq_ref, dk_ref, dv_ref, dz_ref):
        dk_ref[...] = jnp.zeros_like(dk_ref)
        dv_ref[...] = jnp.zeros_like(dv_ref)
        row = lax.broadcasted_iota(jnp.int32, (b, b), 0)
        col = lax.broadcasted_iota(jnp.int32, (b, b), 1)
        for a in range(2):
            la = slice(a * 64, (a + 1) * 64)

            def rbody(r, carry, a=a, la=la):
                r0 = pl.multiple_of(r * b, b)
                rows = pl.ds(r0, b)
                q = q_ref[a, rows, :]
                zz = z_ref[rows, la]
                oo = o_ref[rows, la]
                dpo = dp_ref[rows, la].astype(F32)
                do = dpo * _silu(zz)
                dz_ref[rows, la] = (dpo * oo * _silu_grad(zz)).astype(BF16)
                delta = jnp.sum(do * oo, axis=-1, keepdims=True)
                lse_r = lse_ref[0, rows, a * 64:a * 64 + 1]
                dob = do.astype(BF16)

                def cbody(c, dq):
                    c0 = pl.multiple_of(c * b, b)
                    cols = pl.ds(c0, b)
                    k = k_ref[a, cols, :]
                    v = v_ref[cols, la].astype(BF16)
                    s = _dot_nt(q, k) * MLA_SCALE
                    s = jnp.where(c0 + col <= r0 + row, s, NEG_INF)
                    p = jnp.exp(s - lse_r)
                    ds = p * (_dot_nt(dob, v) - delta) * MLA_SCALE
                    dsb = ds.astype(BF16)
                    dk_ref[a, cols, :] += _dot_tn(dsb, q)
                    dv_ref[cols, la] += _dot_tn(p.astype(BF16), dob)
                    return dq + _dot(dsb, k)

                dq_ref[a, rows, :] = lax.fori_loop(0, r + 1, cbody, jnp.zeros((b, 128), F32))
                return carry

            lax.fori_loop(0, nb, rbody, 0)

    pair3 = pl.BlockSpec((2, t, 128), lambda i: (i, 0, 0))
    return pl.pallas_call(
        body, grid=(MLA_HEADS // 2,),
        in_specs=[pair3, pair3, pl.BlockSpec((t, 128), lambda i: (0, 8 + i)), pl.BlockSpec((t, 128), lambda i: (0, 9 + i)),
                  pl.BlockSpec((t, 128), lambda i: (0, i)), pl.BlockSpec((t, 128), lambda i: (0, i)),
                  pl.BlockSpec((1, t, 128), lambda i: (i, 0, 0))],
        out_specs=[pair3, pair3, pl.BlockSpec((t, 128), lambda i: (0, i)), pl.BlockSpec((t, 128), lambda i: (0, i))],
        out_shape=[S((MLA_HEADS, t, 128), F32), S((MLA_HEADS, t, 128), F32), S((t, 1024), F32), S((t, 1024), BF16)],
        name="mla_attn_bwd", compiler_params=_cp(("parallel",), VMEM_BIG))(qh, kh, kvf, proj, o, dpout, lse)


def _dot_nt_hi(a, b):
    return lax.dot_general(a, b, (((1,), (1,)), ((), ())), preferred_element_type=F32,
                           precision=lax.Precision.HIGHEST)


def _ssm_gen(lr, li, ldt, btr, bti, cr, ci):
    n = SSM_BLOCK
    dt = jnp.exp(ldt)
    x = lr * dt
    w = li * dt
    mag = jnp.exp(x)
    ab_re = mag * jnp.cos(w)
    ab_im = mag * jnp.sin(w)
    den = lr * lr + li * li
    nr = ab_re - 1.0
    f_re = (nr * lr + ab_im * li) / den
    f_im = (ab_im * lr - nr * li) / den
    bb_re = f_re * btr - f_im * bti
    bb_im = f_re * bti + f_im * btr
    tt = (lax.broadcasted_iota(jnp.int32, (n, SSM_STATE), 0) // SSM_GROUP).astype(F32)

    def power(tau):
        m = jnp.exp(tau * x)
        return m * jnp.cos(tau * w), m * jnp.sin(tau * w)

    def rep(v):
        return jnp.concatenate([v] * SSM_CHUNK, axis=0)

    ct_re, ct_im, bt_re, bt_im = rep(cr), rep(ci), rep(bb_re), rep(bb_im)
    p_re, p_im = power(tt)
    l_re = ct_re * p_re - ct_im * p_im
    l_im = ct_re * p_im + ct_im * p_re
    n_re, n_im = power(-tt)
    r_re = bt_re * n_re - bt_im * n_im
    r_im = bt_re * n_im + bt_im * n_re
    rk = lax.broadcasted_iota(jnp.int32, (n, n), 0) // SSM_GROUP
    ct = lax.broadcasted_iota(jnp.int32, (n, n), 1) // SSM_GROUP
    mt = jnp.where(ct >= rk, _dot_nt_hi(r_re, l_re) - _dot_nt_hi(r_im, l_im), 0.0)
    e_re, e_im = power((SSM_CHUNK - 1.0) - tt)
    pin_re = bt_re * e_re - bt_im * e_im
    pin_im = bt_re * e_im + bt_im * e_re
    q_re, q_im = power(tt + 1.0)
    qt_re = ct_re * q_re - ct_im * q_im
    qt_im = -(ct_re * q_im + ct_im * q_re)
    ad_re, ad_im = power(jnp.full((1, SSM_STATE), float(SSM_CHUNK), F32))
    return mt, pin_re, pin_im, qt_re, qt_im, ad_re, ad_im


_SSM_GEN_OUT = [(SSM_BLOCK, SSM_BLOCK)] + [(SSM_BLOCK, SSM_STATE)] * 4 + [(1, SSM_STATE)] * 2
_SSM_GEN_IN = [(1, SSM_STATE)] * 3 + [(SSM_GROUP, SSM_STATE)] * 4


def _gspec(shape2):
    return pl.BlockSpec((1,) + shape2, lambda g: (g, 0, 0))


def _ssm_operators(params):
    def body(*refs):
        ins, outs = refs[:7], refs[7:]
        for o_ref, val in zip(outs, _ssm_gen(*[r[0] for r in ins])):
            o_ref[0] = val

    return pl.pallas_call(
        body, grid=(SSM_GROUPS,), in_specs=[_gspec(s) for s in _SSM_GEN_IN], out_specs=[_gspec(s) for s in _SSM_GEN_OUT],
        out_shape=[S((SSM_GROUPS,) + s, F32) for s in _SSM_GEN_OUT], name="ssm_operators",
        compiler_params=_cp(("parallel",)))(*params)


def _ssm_operators_bwd(params, cots):
    def body(*refs):
        ins, cts, outs = refs[:7], refs[7:14], refs[14:]
        _, vjp = jax.vjp(_ssm_gen, *[r[0] for r in ins])
        grads = list(vjp(tuple(r[0] for r in cts)))
        grads[2] = jnp.broadcast_to(jnp.sum(grads[2], axis=-1, keepdims=True), (1, SSM_STATE))
        for o_ref, val in zip(outs, grads):
            o_ref[0] = val

    return pl.pallas_call(
        body, grid=(SSM_GROUPS,), in_specs=[_gspec(s) for s in _SSM_GEN_IN] + [_gspec(s) for s in _SSM_GEN_OUT],
        out_specs=[_gspec(s) for s in _SSM_GEN_IN], out_shape=[S((SSM_GROUPS,) + s, F32) for s in _SSM_GEN_IN],
        name="ssm_operators_bwd", compiler_params=_cp(("parallel",)))(*params, *cots)


def _shift_rows(x, sh, row, up):
    n = x.shape[0]
    if up:
        return jnp.where(row < n - sh, pltpu.roll(x, n - sh, axis=0), 0.0)
    return jnp.where(row >= sh, pltpu.roll(x, sh, axis=0), 0.0)


def _carry_scan(sr, si, ar, ai, row, up):
    n = sr.shape[0]
    sh = 1
    while sh < n:
        tr, ti = _shift_rows(sr, sh, row, up), _shift_rows(si, sh, row, up)
        sr, si = sr + ar * tr - ai * ti, si + ar * ti + ai * tr
        ar, ai = ar * ar - ai * ai, 2.0 * ar * ai
        sh *= 2
    return sr, si


def _ssm_states(u, pr_ref, pi_ref, ar_ref, ai_ref, row):
    lre = _dot(u, pr_ref[0].astype(BF16))
    lim = _dot(u, pi_ref[0].astype(BF16))
    sr, si = _carry_scan(lre, lim, ar_ref[0], ai_ref[0], row, False)
    return _shift_rows(sr, 1, row, False), _shift_rows(si, 1, row, False)


def _ssm_core_fwd(u_r, ops):
    g, nc, n = u_r.shape

    def body(u_ref, mt_ref, pr_ref, pi_ref, qr_ref, qi_ref, ar_ref, ai_ref, y_ref):
        u = u_ref[0]
        row = lax.broadcasted_iota(jnp.int32, (nc, SSM_STATE), 0)
        s_re, s_im = _ssm_states(u, pr_ref, pi_ref, ar_ref, ai_ref, row)
        y_ref[0] = (_dot(u, mt_ref[0].astype(BF16)) + _dot_nt(s_re.astype(BF16), qr_ref[0].astype(BF16))
                    + _dot_nt(s_im.astype(BF16), qi_ref[0].astype(BF16)))

    return pl.pallas_call(
        body, grid=(g,), in_specs=[_gspec((nc, n))] + [_gspec(s) for s in _SSM_GEN_OUT], out_specs=_gspec((nc, n)),
        out_shape=S((g, nc, n), F32), name="ssm_core_fwd", compiler_params=_cp(("parallel",)))(u_r, *ops)


def _ssm_core_bwd(u_r, dy_r, ops):
    g, nc, n = u_r.shape

    def body(u_ref, dy_ref, mt_ref, pr_ref, pi_ref, qr_ref, qi_ref, ar_ref, ai_ref,
             du_ref, dmt_ref, dpr_ref, dpi_ref, dqr_ref, dqi_ref, dar_ref, dai_ref):
        u = u_ref[0]
        dy = dy_ref[0].astype(BF16)
        row = lax.broadcasted_iota(jnp.int32, (nc, SSM_STATE), 0)
        s_re, s_im = _ssm_states(u, pr_ref, pi_ref, ar_ref, ai_ref, row)
        ar, ai = ar_ref[0], ai_ref[0]
        g_re = _shift_rows(_dot(dy, qr_ref[0].astype(BF16)), 1, row, True)
        g_im = _shift_rows(_dot(dy, qi_ref[0].astype(BF16)), 1, row, True)
        l_re, l_im = _carry_scan(g_re, g_im, ar, -ai, row, True)
        lrb, lib = l_re.astype(BF16), l_im.astype(BF16)
        du_ref[0] = (_dot_nt(dy, mt_ref[0].astype(BF16)) + _dot_nt(lrb, pr_ref[0].astype(BF16))
                     + _dot_nt(lib, pi_ref[0].astype(BF16)))
        dmt_ref[0] = _dot_tn(u, dy)
        dpr_ref[0] = _dot_tn(u, lrb)
        dpi_ref[0] = _dot_tn(u, lib)
        dqr_ref[0] = _dot_tn(dy, s_re.astype(BF16))
        dqi_ref[0] = _dot_tn(dy, s_im.astype(BF16))
        dar_ref[0] = jnp.sum(l_re * s_re + l_im * s_im, axis=0, keepdims=True)
        dai_ref[0] = jnp.sum(l_im * s_re - l_re * s_im, axis=0, keepdims=True)

    return pl.pallas_call(
        body, grid=(g,), in_specs=[_gspec((nc, n)), _gspec((nc, n))] + [_gspec(s) for s in _SSM_GEN_OUT],
        out_specs=[_gspec((nc, n))] + [_gspec(s) for s in _SSM_GEN_OUT],
        out_shape=[S((g, nc, n), F32)] + [S((g,) + s, F32) for s in _SSM_GEN_OUT],
        name="ssm_core_bwd", compiler_params=_cp(("parallel",)))(u_r, dy_r, *ops)


def _to_groups(a, dtype):
    t = a.shape[0]
    return (a.astype(dtype).reshape(t // SSM_CHUNK, SSM_CHUNK, SSM_GROUPS, SSM_GROUP)
            .transpose(2, 0, 1, 3).reshape(SSM_GROUPS, t // SSM_CHUNK, SSM_BLOCK))


def _from_groups(a):
    g, nc, _ = a.shape
    return a.reshape(g, nc, SSM_CHUNK, SSM_GROUP).transpose(1, 2, 0, 3).reshape(nc * SSM_CHUNK, g * SSM_GROUP)


def _s5_act(y_core, proj, d_skip):
    t = y_core.shape[0]
    tr = 256

    def body(y_ref, u_ref, d_ref, o_ref):
        o_ref[...] = _gelu(y_ref[...] + d_ref[...] * u_ref[...])

    return pl.pallas_call(body, grid=(t // tr,), in_specs=[_row_spec(tr, 1024), _row_spec(tr, 1024), _full_spec((1, 1024))],
                          out_specs=_row_spec(tr, 1024), out_shape=S((t, 1024), F32), name="s5_act",
                          compiler_params=_cp(("parallel",)))(y_core, proj, d_skip)


def _s5_gate(yg, tg, b_glu, proj):
    t = yg.shape[0]
    tr = 256

    def body(y_ref, t_ref, b_ref, z_ref, o_ref):
        o_ref[...] = (y_ref[...] * jax.nn.sigmoid(t_ref[...] + b_ref[...]) * _silu(z_ref[...])).astype(BF16)

    return pl.pallas_call(
        body, grid=(t // tr,),
        in_specs=[_row_spec(tr, 1024), _row_spec(tr, 1024), _full_spec((1, 1024)), pl.BlockSpec((tr, 1024), lambda i: (i, 1))],
        out_specs=_row_spec(tr, 1024), out_shape=S((t, 1024), BF16), name="s5_gate",
        compiler_params=_cp(("parallel",)))(yg, tg, b_glu, proj)


def _s5_gate_bwd(dp, yg, tg, b_glu, proj):
    t = yg.shape[0]
    tr = 256

    def body(dp_ref, y_ref, t_ref, b_ref, z_ref, dt_ref, dy_ref, dz_ref, db_ref):
        @pl.when(pl.program_id(0) == 0)
        def _():
            db_ref[...] = jnp.zeros_like(db_ref)

        z = z_ref[...]
        yv = y_ref[...]
        sg = jax.nn.sigmoid(t_ref[...] + b_ref[...])
        dpv = dp_ref[...].astype(F32)
        do = dpv * _silu(z)
        dz_ref[...] = (dpv * yv * sg * _silu_grad(z)).astype(BF16)
        dy_ref[...] = do * sg
        dtv = do * yv * sg * (1.0 - sg)
        dt_ref[...] = dtv.astype(BF16)
        db_ref[...] += jnp.sum(dtv, axis=0, keepdims=True)

    return pl.pallas_call(
        body, grid=(t // tr,),
        in_specs=[_row_spec(tr, 1024), _row_spec(tr, 1024), _row_spec(tr, 1024), _full_spec((1, 1024)),
                  pl.BlockSpec((tr, 1024), lambda i: (i, 1))],
        out_specs=[_row_spec(tr, 1024), _row_spec(tr, 1024), _row_spec(tr, 1024), _full_spec((1, 1024))],
        out_shape=[S((t, 1024), BF16), S((t, 1024), F32), S((t, 1024), BF16), S((1, 1024), F32)],
        name="s5_gate_bwd", compiler_params=_cp(("arbitrary",)))(dp, yg, tg, b_glu, proj)


def _s5_act_bwd(dyg1, dyg2, y_core, proj, d_skip):
    t = y_core.shape[0]
    tr = 256

    def body(a_ref, b_ref, y_ref, u_ref, d_ref, dy_ref, du_ref, dd_ref):
        @pl.when(pl.program_id(0) == 0)
        def _():
            dd_ref[...] = jnp.zeros_like(dd_ref)

        u = u_ref[...]
        dv = (a_ref[...] + b_ref[...]) * _gelu_grad(y_ref[...] + d_ref[...] * u)
        dy_ref[...] = dv.astype(BF16)
        du_ref[...] = dv * d_ref[...]
        dd_ref[...] += jnp.sum(dv * u, axis=0, keepdims=True)

    return pl.pallas_call(
        body, grid=(t // tr,),
        in_specs=[_row_spec(tr, 1024)] * 4 + [_full_spec((1, 1024))],
        out_specs=[_row_spec(tr, 1024), _row_spec(tr, 1024), _full_spec((1, 1024))],
        out_shape=[S((t, 1024), BF16), S((t, 1024), F32), S((1, 1024), F32)],
        name="s5_act_bwd", compiler_params=_cp(("arbitrary",)))(dyg1, dyg2, y_core, proj, d_skip)


def _s5_dproj(du_skip, du_core, dz):
    t = du_skip.shape[0]
    tr = 256

    def body(a_ref, b_ref, z_ref, o_ref):
        o_ref[:, 0:1024] = (a_ref[...] + b_ref[...]).astype(BF16)
        o_ref[:, 1024:2048] = z_ref[...]

    return pl.pallas_call(body, grid=(t // tr,), in_specs=[_row_spec(tr, 1024)] * 3, out_specs=_row_spec(tr, 2048),
                          out_shape=S((t, 2048), BF16), name="s5_dproj",
                          compiler_params=_cp(("parallel",)))(du_skip, du_core, dz)


def _s5_params(lam_re, lam_im, log_dt, b_re, b_im, c_re, c_im):
    g = SSM_GROUPS
    return (lam_re.reshape(g, 1, SSM_STATE), lam_im.reshape(g, 1, SSM_STATE),
            jnp.broadcast_to(log_dt.reshape(g, 1, 1), (g, 1, SSM_STATE)),
            b_re.transpose(0, 2, 1), b_im.transpose(0, 2, 1), c_re, c_im)


def _s5_fwd(proj, params, d_skip, w_glu, b_glu):
    ops = _ssm_operators(params)
    y_core = _from_groups(_ssm_core_fwd(_to_groups(proj[:, :1024], BF16), ops))
    yg = _s5_act(y_core, proj, d_skip)
    tg = _mm(yg, w_glu, name="s5_glu")
    return _s5_gate(yg, tg, b_glu, proj), (ops, y_core, yg, tg)


def _s5_bwd(proj, res, dp, params, d_skip, w_glu, b_glu):
    ops, y_core, yg, tg = res
    dtg, dyg1, dz, db_glu = _s5_gate_bwd(dp, yg, tg, b_glu, proj)
    dyg2 = _mm(dtg, w_glu, tb=True, name="s5_glu_dx")
    dw_glu = _mm(yg, dtg, ta=True, name="s5_glu_dw")
    dy_core, du_skip, dd = _s5_act_bwd(dyg1, dyg2, y_core, proj, d_skip)
    outs = _ssm_core_bwd(_to_groups(proj[:, :1024], BF16), _to_groups(dy_core, BF16), ops)
    dproj = _s5_dproj(du_skip, _from_groups(outs[0]), dz)
    glr, gli, gdt, gbtr, gbti, gcr, gci = _ssm_operators_bwd(params, outs[1:])
    g = SSM_GROUPS
    grads = (glr.reshape(g, SSM_STATE), gli.reshape(g, SSM_STATE), gdt[:, 0, 0].reshape(1, g),
             gbtr.transpose(0, 2, 1), gbti.transpose(0, 2, 1), gcr, gci)
    return dproj, grads, dd, dw_glu, db_glu


def _perm_uq(w):
    w3 = w.reshape(w.shape[0], MLA_HEADS, MLA_NOPE + MLA_ROPE)
    return jnp.concatenate([w3[:, :, :64].reshape(-1, 1024), w3[:, :, 64:80].reshape(-1, 256),
                            w3[:, :, 80:96].reshape(-1, 256)], axis=1)


def _unperm_uq(w):
    r = w.shape[0]
    return jnp.concatenate([w[:, :1024].reshape(r, 16, 64), w[:, 1024:1280].reshape(r, 16, 16),
                            w[:, 1280:1536].reshape(r, 16, 16)], axis=2).reshape(r, 1536)


def _perm_ukv(w):
    w3 = w.reshape(w.shape[0], MLA_HEADS, 128)
    return jnp.concatenate([w3[:, :, :64].reshape(-1, 1024), w3[:, :, 64:].reshape(-1, 1024)], axis=1)


def _unperm_ukv(w):
    r = w.shape[0]
    return jnp.concatenate([w[:, :1024].reshape(r, 16, 64), w[:, 1024:].reshape(r, 16, 64)], axis=2).reshape(r, 2048)


def _mla_fwd(proj, gq, gkv, wuq, wukv, cos, sin):
    cqn, ckvn = _mla_norms(proj, gq, gkv)
    qf = _mm(cqn, wuq, name="mla_uq")
    kvf = _mm(ckvn, wukv, name="mla_ukv")
    qh, kh = _mla_pack(qf, kvf, proj, cos, sin)
    o, pout, lse = _mla_attn_fwd(qh, kh, kvf, proj)
    return pout, (cqn, ckvn, kvf, qh, kh, o, lse)


def _mla_bwd(proj, res, dpout, gq, gkv, wuq, wukv, cos, sin):
    cqn, ckvn, kvf, qh, kh, o, lse = res
    dqh, dkh, dv, dz = _mla_attn_bwd(qh, kh, kvf, proj, o, dpout, lse)
    dqf, dkvf, dkr = _mla_unpack(dqh, dkh, dv, cos, sin)
    dcqn = _mm(dqf, wuq, tb=True, name="mla_uq_dx")
    dwuq = _mm(cqn, dqf, ta=True, name="mla_uq_dw")
    dckvn = _mm(dkvf, wukv, tb=True, name="mla_ukv_dx")
    dwukv = _mm(ckvn, dkvf, ta=True, name="mla_ukv_dw")
    dproj, dgq, dgkv = _mla_norms_bwd(proj, dcqn, dckvn, dkr, dz, gq, gkv)
    return dproj, dgq, dgkv, dwuq, dwukv


_ANY = pl.BlockSpec(memory_space=pl.ANY)


def _chip_peers():
    x, y, c = lax.axis_index("x"), lax.axis_index("y"), lax.axis_index("c")
    return 2 * x + y, c, [(1 - x, y), (x, 1 - y), (1 - x, 1 - y)]


def _exchange_chips(src_of, dst_of, send_sems, recv_sems, loc_sem):
    me, c, peers = _chip_peers()
    loc = pltpu.make_async_copy(src_of(me), dst_of(me), loc_sem)
    loc.start()

    def copy(k, px, py, src_chip, dst_chip):
        return pltpu.make_async_remote_copy(src_ref=src_of(src_chip), dst_ref=dst_of(dst_chip), send_sem=send_sems.at[k],
                                            recv_sem=recv_sems.at[k], device_id=(px, py, c), device_id_type=MESH)

    sends = [copy(k, px, py, 2 * px + py, me) for k, (px, py) in enumerate(peers)]
    for cp in sends:
        cp.start()
    for k, (px, py) in enumerate(peers):
        copy(k, px, py, me, 2 * px + py).wait_recv()
    for cp in sends:
        cp.wait_send()
    loc.wait()


_EXCH_SCRATCH = [pltpu.SemaphoreType.DMA((3,)), pltpu.SemaphoreType.DMA((3,)), pltpu.SemaphoreType.DMA(())]


def _all_gather_chips(xs, name):
    def body(x_ref, o_ref, send_sems, recv_sems, loc_sem):
        _exchange_chips(lambda s: x_ref, lambda s: o_ref.at[s], send_sems, recv_sems, loc_sem)

    return pl.pallas_call(body, in_specs=[_ANY], out_specs=_ANY, out_shape=S((N_CHIPS,) + xs.shape, xs.dtype),
                          scratch_shapes=_EXCH_SCRATCH, name=name)(xs)


def _all_to_all_chips(g, name):
    def body(g_ref, o_ref, send_sems, recv_sems, loc_sem):
        _exchange_chips(lambda s: g_ref.at[s], lambda s: o_ref.at[s], send_sems, recv_sems, loc_sem)

    return pl.pallas_call(body, in_specs=[_ANY], out_specs=_ANY, out_shape=S(g.shape, g.dtype),
                          scratch_shapes=_EXCH_SCRATCH, name=name)(g)


def _sibling_swap(p, name):
    def body(p_ref, q_ref, send_sem, recv_sem):
        x, y, c = lax.axis_index("x"), lax.axis_index("y"), lax.axis_index("c")
        cp = pltpu.make_async_remote_copy(src_ref=p_ref, dst_ref=q_ref, send_sem=send_sem, recv_sem=recv_sem,
                                          device_id=(x, y, 1 - c), device_id_type=MESH)
        cp.start()
        cp.wait()

    return pl.pallas_call(body, in_specs=[_ANY], out_specs=_ANY, out_shape=S(p.shape, p.dtype),
                          scratch_shapes=[pltpu.SemaphoreType.DMA(()), pltpu.SemaphoreType.DMA(())], name=name)(p)


def _sum_chips(l):
    _, r, c = l.shape
    tr = 128

    def body(l_ref, o_ref):
        o_ref[...] = ((l_ref[0] + l_ref[1]) + l_ref[2]) + l_ref[3]

    return pl.pallas_call(body, grid=(r // tr,), in_specs=[pl.BlockSpec((N_CHIPS, tr, c), lambda i: (0, i, 0))],
                          out_specs=_row_spec(tr, c), out_shape=S((r, c), F32), name="sum_chips",
                          compiler_params=_cp(("parallel",)))(l)


def _add2(p, q):
    r, c = p.shape
    tr = 128

    def body(p_ref, q_ref, o_ref):
        o_ref[...] = p_ref[...] + q_ref[...]

    return pl.pallas_call(body, grid=(r // tr,), in_specs=[_row_spec(tr, c), _row_spec(tr, c)], out_specs=_row_spec(tr, c),
                          out_shape=S((r, c), F32), name="add_cores", compiler_params=_cp(("parallel",)))(p, q)


def _adamw(w, g, m, v, name):
    r, c = w.shape
    tr = 256 if r % 256 == 0 else (128 if r % 128 == 0 else r)
    c1 = 1.0 / (1.0 - ADAM_B1 ** ADAM_STEP)
    c2 = 1.0 / (1.0 - ADAM_B2 ** ADAM_STEP)

    def body(w_ref, g_ref, m_ref, v_ref, d_ref, nm_ref, nv_ref):
        gv = g_ref[...]
        nm = ADAM_B1 * m_ref[...] + (1.0 - ADAM_B1) * gv
        nv = ADAM_B2 * v_ref[...] + (1.0 - ADAM_B2) * (gv * gv)
        nm_ref[...] = nm
        nv_ref[...] = nv
        d_ref[...] = -ADAM_LR * ((nm * c1) / (jnp.sqrt(nv * c2) + ADAM_EPS) + ADAM_WD * w_ref[...])

    return pl.pallas_call(body, grid=(r // tr,), in_specs=[_row_spec(tr, c)] * 4, out_specs=[_row_spec(tr, c)] * 3,
                          out_shape=[S((r, c), F32)] * 3, name=name, compiler_params=_cp(("parallel",)))(w, g, m, v)


_WEIGHTS = ['pre_norm', 'post_norm', 'rel_bias', 'a_w_in', 'a_lam_re', 'a_lam_im', 'a_log_dt', 'a_b_re', 'a_b_im',
            'a_c_re', 'a_c_im', 'a_d', 'a_w_glu', 'a_b_glu', 'a_w_out', 'b_w_in', 'b_sinks', 'b_w_out', 'c_w_in',
            'c_q_norm', 'c_kv_norm', 'c_w_uq', 'c_w_ukv', 'c_w_out', 'd_w_in', 'd_ln_g', 'd_ln_b', 'd_w_s', 'd_b_s',
            'd_w_out']
_BIG = [('a_w_in', 1024, 2048, 1), ('a_w_glu', 1024, 1024, 0), ('a_w_out', 1024, 1024, 0), ('b_w_in', 1024, 2304, 1),
        ('b_w_out', 1024, 1024, 0), ('c_w_in', 1024, 2080, 1), ('c_w_uq', 768, 1536, 1), ('c_w_ukv', 256, 2048, 1),
        ('c_w_out', 1024, 1024, 0), ('d_w_in', 1024, 3072, 1), ('d_w_out', 1024, 1024, 0)]
_SHARDED_VECS = [('c_q_norm', 768), ('c_kv_norm', 256), ('d_ln_g', 1024), ('d_ln_b', 1024)]
_REPLICATED = [n for n in _WEIGHTS if n not in [b[0] for b in _BIG] and n not in [s[0] for s in _SHARDED_VECS]]
_LANES = 1024
_BIG_ROWS = sum(r * c for _, r, c, _ in _BIG) // N_CHIPS // _LANES
_SMALL_Q_ROWS = 136
_GRAD_ROWS = 4224
_AG_ROWS = 4080
_SMALL_PACK_ROWS = 640


def _pad_rows(flat, rows):
    return jnp.pad(flat, (0, rows * _LANES - flat.shape[0])).reshape(rows, _LANES)


def _shards_of(full, dim):
    r, c = full.shape
    if dim == 0:
        return full.reshape(N_CHIPS, -1)
    return full.reshape(r, N_CHIPS, c // N_CHIPS).transpose(1, 0, 2).reshape(N_CHIPS, -1)


def _whole_of(shards, r, c, dim):
    if dim == 0:
        return shards.reshape(r, c)
    return shards.reshape(N_CHIPS, r, c // N_CHIPS).transpose(1, 0, 2).reshape(r, c)


def kernel(x, pre_norm, post_norm, rel_bias, a_w_in, a_lam_re, a_lam_im, a_log_dt, a_b_re, a_b_im, a_c_re, a_c_im, a_d, a_w_glu, a_b_glu, a_w_out, b_w_in, b_sinks, b_w_out, c_w_in, c_q_norm, c_kv_norm, c_w_uq, c_w_ukv, c_w_out, d_w_in, d_ln_g, d_ln_b, d_w_s, d_b_s, d_w_out, loss_target, m_pre_norm, m_post_norm, m_rel_bias, m_a_w_in, m_a_lam_re, m_a_lam_im, m_a_log_dt, m_a_b_re, m_a_b_im, m_a_c_re, m_a_c_im, m_a_d, m_a_w_glu, m_a_b_glu, m_a_w_out, m_b_w_in, m_b_sinks, m_b_w_out, m_c_w_in, m_c_q_norm, m_c_kv_norm, m_c_w_uq, m_c_w_ukv, m_c_w_out, m_d_w_in, m_d_ln_g, m_d_ln_b, m_d_w_s, m_d_b_s, m_d_w_out, v_pre_norm, v_post_norm, v_rel_bias, v_a_w_in, v_a_lam_re, v_a_lam_im, v_a_log_dt, v_a_b_re, v_a_b_im, v_a_c_re, v_a_c_im, v_a_d, v_a_w_glu, v_a_b_glu, v_a_w_out, v_b_w_in, v_b_sinks, v_b_w_out, v_c_w_in, v_c_q_norm, v_c_kv_norm, v_c_w_uq, v_c_w_ukv, v_c_w_out, v_d_w_in, v_d_ln_g, v_d_ln_b, v_d_w_s, v_d_b_s, v_d_w_out):
    arg = dict(locals())
    me = 2 * lax.axis_index("x") + lax.axis_index("y")
    xin = x[0]
    t = xin.shape[0]

    parts = [arg[n].astype(BF16).reshape(-1) for n, _, _, _ in _BIG]
    parts += [lax.bitcast_convert_type(lax.bitcast_convert_type(arg[n].reshape(-1), jnp.uint16), BF16).reshape(-1)
              for n, _ in _SHARDED_VECS]
    gathered = _all_gather_chips(_pad_rows(jnp.concatenate(parts), _AG_ROWS), "gather_weights").reshape(N_CHIPS, -1)
    wt, off = {}, 0
    for n, r, c, dim in _BIG:
        sz = r * c // N_CHIPS
        wt[n] = _whole_of(gathered[:, off:off + sz], r, c, dim)
        off += sz
    vec = {}
    for n, sz in _SHARDED_VECS:
        q = sz // N_CHIPS
        bits = lax.bitcast_convert_type(gathered[:, off:off + 2 * q].reshape(N_CHIPS, q, 2), jnp.uint16)
        vec[n] = lax.bitcast_convert_type(bits, F32).reshape(1, sz)
        off += 2 * q
    w_c_in = jnp.concatenate([wt['c_w_in'][:, :1056], jnp.zeros((1024, 96), BF16), wt['c_w_in'][:, 1056:]], axis=1)
    w_uq = _perm_uq(wt['c_w_uq'])
    w_ukv = _perm_ukv(wt['c_w_ukv'])
    cos, sin = _rope_tables(t)
    bucket = jnp.asarray(_t5_bucket_table())
    s5p = _s5_params(a_lam_re[0], a_lam_im[0], a_log_dt[0], a_b_re[0], a_b_im[0], a_c_re[0], a_c_im[0])
    bst = d_b_s[0].T

    x0 = xin
    h0 = _prenorm(x0, pre_norm[0:1], "pre0")
    proj0 = _mm(h0, wt['a_w_in'], name="a_in")
    p0, res0 = _s5_fwd(proj0, s5p, a_d, wt['a_w_glu'], a_b_glu)
    y0 = _mm(p0, wt['a_w_out'], name="a_out")
    x1 = _postnorm(x0, y0, post_norm[0:1], "post0")

    h1 = _prenorm(x1, pre_norm[1:2], "pre1")
    proj1 = _mm(h1, wt['b_w_in'], name="b_in")
    bias = _swa_bias(rel_bias, bucket)
    p1 = _swa_fwd(proj1, bias, b_sinks)
    y1 = _mm(p1, wt['b_w_out'], name="b_out")
    x2 = _postnorm(x1, y1, post_norm[1:2], "post1")

    h2 = _prenorm(x2, pre_norm[2:3], "pre2")
    proj2 = _mm(h2, w_c_in, name="c_in")
    p2, res2 = _mla_fwd(proj2, vec['c_q_norm'], vec['c_kv_norm'], w_uq, w_ukv, cos, sin)
    y2 = _mm(p2, wt['c_w_out'], name="c_out")
    x3 = _postnorm(x2, y2, post_norm[2:3], "post2")

    h3 = _prenorm(x3, pre_norm[3:4], "pre3")
    proj3 = _mm(h3, wt['d_w_in'], name="d_in")
    p3 = _sgu_fwd(proj3, vec['d_ln_g'], vec['d_ln_b'], d_w_s[0], bst)
    y3 = _mm(p3, wt['d_w_out'], name="d_out")
    x4 = _postnorm(x3, y3, post_norm[3:4], "post3")

    loss_part, dx = _loss_head(x4, loss_target[0])
    loss = lax.psum(loss_part[0, 0], ("x", "y", "c"))

    gr = {}
    d_pre, d_post = [None] * 4, [None] * 4

    dy, d_post[3] = _postnorm_bwd(dx, y3, post_norm[3:4], "post3_bwd")
    dp = _mm(dy, wt['d_w_out'], tb=True, name="d_out_dx")
    gr['d_w_out'] = _mm(p3, dy, ta=True, name="d_out_dw")
    dproj, gr['d_w_s'], dbst, gr['d_ln_g'], gr['d_ln_b'] = _sgu_bwd(proj3, dp, vec['d_ln_g'], vec['d_ln_b'], d_w_s[0], bst)
    gr['d_b_s'] = dbst.T
    dh = _mm(dproj, wt['d_w_in'], tb=True, name="d_in_dx")
    gr['d_w_in'] = _mm(h3, dproj, ta=True, name="d_in_dw")
    dx, d_pre[3] = _prenorm_bwd(dx, dh, x3, pre_norm[3:4], "pre3_bwd")

    dy, d_post[2] = _postnorm_bwd(dx, y2, post_norm[2:3], "post2_bwd")
    dp = _mm(dy, wt['c_w_out'], tb=True, name="c_out_dx")
    gr['c_w_out'] = _mm(p2, dy, ta=True, name="c_out_dw")
    dproj, gr['c_q_norm'], gr['c_kv_norm'], dwuq, dwukv = _mla_bwd(proj2, res2, dp, vec['c_q_norm'], vec['c_kv_norm'],
                                                                  w_uq, w_ukv, cos, sin)
    gr['c_w_uq'] = _unperm_uq(dwuq)
    gr['c_w_ukv'] = _unperm_ukv(dwukv)
    dh = _mm(dproj, w_c_in, tb=True, name="c_in_dx")
    dwc = _mm(h2, dproj, ta=True, name="c_in_dw")
    gr['c_w_in'] = jnp.concatenate([dwc[:, :1056], dwc[:, 1152:]], axis=1)
    dx, d_pre[2] = _prenorm_bwd(dx, dh, x2, pre_norm[2:3], "pre2_bwd")

    dy, d_post[1] = _postnorm_bwd(dx, y1, post_norm[1:2], "post1_bwd")
    dp = _mm(dy, wt['b_w_out'], tb=True, name="b_out_dx")
    gr['b_w_out'] = _mm(p1, dy, ta=True, name="b_out_dw")
    dqz, dkv, dss, dsk = _swa_bwd(proj1, dp, bias, b_sinks)
    gr['rel_bias'], gr['b_sinks'] = _swa_dbias(dss, dsk, bucket)
    dproj = jnp.concatenate([dqz[:, :1024], dkv.astype(BF16), dqz[:, 1024:]], axis=1)
    dh = _mm(dproj, wt['b_w_in'], tb=True, name="b_in_dx")
    gr['b_w_in'] = _mm(h1, dproj, ta=True, name="b_in_dw")
    dx, d_pre[1] = _prenorm_bwd(dx, dh, x1, pre_norm[1:2], "pre1_bwd")

    dy, d_post[0] = _postnorm_bwd(dx, y0, post_norm[0:1], "post0_bwd")
    dp = _mm(dy, wt['a_w_out'], tb=True, name="a_out_dx")
    gr['a_w_out'] = _mm(p0, dy, ta=True, name="a_out_dw")
    dproj, s5g, gr['a_d'], gr['a_w_glu'], gr['a_b_glu'] = _s5_bwd(proj0, res0, dp, s5p, a_d, wt['a_w_glu'], a_b_glu)
    for n, g in zip(['a_lam_re', 'a_lam_im', 'a_log_dt', 'a_b_re', 'a_b_im', 'a_c_re', 'a_c_im'], s5g):
        gr[n] = g
    dh = _mm(dproj, wt['a_w_in'], tb=True, name="a_in_dx")
    gr['a_w_in'] = _mm(h0, dproj, ta=True, name="a_in_dw")
    dx, d_pre[0] = _prenorm_bwd(dx, dh, x0, pre_norm[0:1], "pre0_bwd")
    gr['pre_norm'] = jnp.concatenate(d_pre, axis=0)
    gr['post_norm'] = jnp.concatenate(d_post, axis=0)

    small_names = _REPLICATED + [n for n, _ in _SHARDED_VECS]
    small = jnp.concatenate([gr[n].reshape(-1) for n in small_names])
    small = jnp.pad(small, (0, N_CHIPS * _SMALL_Q_ROWS * _LANES - small.shape[0])).reshape(N_CHIPS, -1)
    contrib = jnp.concatenate([_shards_of(gr[n], dim) for n, _, _, dim in _BIG] + [small], axis=1)
    contrib = jnp.pad(contrib, ((0, 0), (0, _GRAD_ROWS * _LANES - contrib.shape[1]))).reshape(N_CHIPS, _GRAD_ROWS, _LANES)
    partial = _sum_chips(_all_to_all_chips(contrib, "scatter_grads"))
    total = _add2(partial, _sibling_swap(partial, "swap_cores"))
    small_all = _all_gather_chips(total[_BIG_ROWS:_BIG_ROWS + _SMALL_Q_ROWS], "gather_small_grads").reshape(-1)
    grads, off = {}, 0
    flat = total.reshape(-1)
    for n, r, c, dim in _BIG:
        sz = r * c // N_CHIPS
        grads[n] = flat[off:off + sz].reshape(arg[n].shape)
        off += sz
    off = 0
    for n in _REPLICATED:
        sz = math.prod(arg[n].shape)
        grads[n] = small_all[off:off + sz].reshape(arg[n].shape)
        off += sz
    for n, sz in _SHARDED_VECS:
        q = sz // N_CHIPS
        grads[n] = lax.dynamic_slice(small_all, (off + me * q,), (q,)).reshape(arg[n].shape)
        off += sz

    delta, new_m, new_v = {}, {}, {}
    for n, _, _, _ in _BIG:
        shp = arg[n].shape
        two = (shp[1], shp[2])
        outs = _adamw(arg[n].reshape(two), grads[n].reshape(two), arg['m_' + n].reshape(two),
                      arg['v_' + n].reshape(two), "adamw_" + n)
        delta[n], new_m[n], new_v[n] = [o.reshape(shp) for o in outs]

    def pack(prefix, src):
        return _pad_rows(jnp.concatenate([src[prefix + n].reshape(-1) for n in small_names]), _SMALL_PACK_ROWS)

    outs = _adamw(pack('', arg), pack('', grads), pack('m_', arg), pack('v_', arg), "adamw_small")
    outs = [o.reshape(-1) for o in outs]
    off = 0
    for n in small_names:
        shp = arg[n].shape
        sz = math.prod(shp)
        delta[n], new_m[n], new_v[n] = [o[off:off + sz].reshape(shp) for o in outs]
        off += sz

    return (loss, dx.reshape(x.shape), *[grads[n] for n in _WEIGHTS], *[delta[n] for n in _WEIGHTS],
            *[new_m[n] for n in _WEIGHTS], *[new_v[n] for n in _WEIGHTS])
```

```python
import functools
import math

import numpy as np
import jax
import jax.numpy as jnp
from jax import lax
from jax.experimental import pallas as pl
from jax.experimental.pallas import tpu as pltpu

F32 = jnp.float32
BF16 = jnp.bfloat16
S = jax.ShapeDtypeStruct
MESH = pl.DeviceIdType.MESH

D_MODEL = 1024
EPS = 1e-6
NEG_INF = -1e30
N_CHIPS = 4

SSM_GROUPS = 64
SSM_GROUP = 16
SSM_STATE = 64
SSM_CHUNK = 16
SSM_BLOCK = SSM_CHUNK * SSM_GROUP

HEAD_DIM = 64
SWA_HEADS = 16
SWA_GROUP = 8
WINDOW = 128
REL_BUCKETS = 32
REL_MAX_DIST = 128

MLA_HEADS = 16
MLA_NOPE = 64
MLA_ROPE = 32
MLA_Q_RANK = 768
MLA_KV_RANK = 256
MLA_SCALE = (MLA_NOPE + MLA_ROPE) ** -0.5
MLA_BLK = 256
ROPE_BASE = 10000.0

SGU_CHUNK = 128
SGU_GROUPS = 16
SGU_GDIM = 64

ADAM_LR = 0.001
ADAM_B1 = 0.9
ADAM_B2 = 0.999
ADAM_EPS = 1e-08
ADAM_WD = 0.01
ADAM_STEP = 10

VMEM_BIG = 56 * 1024 * 1024


def _cp(sem, vmem=None):
    return pltpu.CompilerParams(dimension_semantics=sem, vmem_limit_bytes=vmem)


def _row_spec(tr, c):
    return pl.BlockSpec((tr, c), lambda i: (i, 0))


def _full_spec(shape):
    nd = len(shape)
    return pl.BlockSpec(shape, lambda *_: (0,) * nd)


def _gelu(x):
    c = 0.7978845608028654
    return 0.5 * x * (1.0 + jnp.tanh(c * (x + 0.044715 * x * x * x)))


def _gelu_grad(x):
    c = 0.7978845608028654
    t = jnp.tanh(c * (x + 0.044715 * x * x * x))
    return 0.5 * (1.0 + t) + 0.5 * x * (1.0 - t * t) * c * (1.0 + 3.0 * 0.044715 * x * x)


def _silu(z):
    return z * jax.nn.sigmoid(z)


def _silu_grad(z):
    s = jax.nn.sigmoid(z)
    return s * (1.0 + z * (1.0 - s))


def _dot(a, b):
    return lax.dot_general(a, b, (((1,), (0,)), ((), ())), preferred_element_type=F32)


def _dot_nt(a, b):
    return lax.dot_general(a, b, (((1,), (1,)), ((), ())), preferred_element_type=F32)


def _dot_tn(a, b):
    return lax.dot_general(a, b, (((0,), (0,)), ((), ())), preferred_element_type=F32)


def _pick(n, cap):
    best = 0
    for d in range(128, min(n, cap) + 1, 128):
        if n % d == 0:
            best = d
    if best < 256:
        return n
    return best


def _mm(a, b, *, ta=False, tb=False, out_dtype=F32, name):
    m, k = (a.shape[1], a.shape[0]) if ta else a.shape
    n = b.shape[0] if tb else b.shape[1]
    assert (b.shape[1] if tb else b.shape[0]) == k
    tm, tn, tk = _pick(m, 1024), _pick(n, 1024), _pick(k, 1024)
    if tn > 1024:
        tm = _pick(m, 256)
    if tk > 1024:
        tm, tn = _pick(m, 256), _pick(n, 256)
    nk = k // tk
    dims = (((0 if ta else 1,), (1 if tb else 0,)), ((), ()))

    def body(a_ref, b_ref, o_ref, acc_ref):
        kk = pl.program_id(2)

        @pl.when(kk == 0)
        def _():
            acc_ref[...] = jnp.zeros_like(acc_ref)

        acc_ref[...] += lax.dot_general(a_ref[...].astype(BF16), b_ref[...].astype(BF16), dims,
                                        preferred_element_type=F32)

        @pl.when(kk == nk - 1)
        def _():
            o_ref[...] = acc_ref[...].astype(out_dtype)

    a_spec = pl.BlockSpec((tk, tm), lambda i, j, kk: (kk, i)) if ta else pl.BlockSpec((tm, tk), lambda i, j, kk: (i, kk))
    b_spec = pl.BlockSpec((tn, tk), lambda i, j, kk: (j, kk)) if tb else pl.BlockSpec((tk, tn), lambda i, j, kk: (kk, j))
    return pl.pallas_call(
        body, grid=(m // tm, n // tn, nk), in_specs=[a_spec, b_spec],
        out_specs=pl.BlockSpec((tm, tn), lambda i, j, kk: (i, j)), out_shape=S((m, n), out_dtype),
        scratch_shapes=[pltpu.VMEM((tm, tn), F32)], name=name,
        compiler_params=_cp(("parallel", "parallel", "arbitrary"), VMEM_BIG))(a, b)


def _prenorm(x, g, name):
    t, d = x.shape
    tr = 256

    def body(x_ref, g_ref, o_ref):
        xv = x_ref[...]
        r = lax.rsqrt(jnp.mean(xv * xv, axis=-1, keepdims=True) + EPS)
        o_ref[...] = (xv * r * g_ref[...]).astype(BF16)

    return pl.pallas_call(body, grid=(t // tr,), in_specs=[_row_spec(tr, d), _full_spec((1, d))],
                          out_specs=_row_spec(tr, d), out_shape=S((t, d), BF16), name=name,
                          compiler_params=_cp(("parallel",)))(x, g)


def _postnorm(x, y, g, name):
    t, d = x.shape
    tr = 256

    def body(x_ref, y_ref, g_ref, o_ref):
        yv = y_ref[...]
        r = lax.rsqrt(jnp.mean(yv * yv, axis=-1, keepdims=True) + EPS)
        o_ref[...] = x_ref[...] + yv * r * g_ref[...]

    return pl.pallas_call(body, grid=(t // tr,), in_specs=[_row_spec(tr, d), _row_spec(tr, d), _full_spec((1, d))],
                          out_specs=_row_spec(tr, d), out_shape=S((t, d), F32), name=name,
                          compiler_params=_cp(("parallel",)))(x, y, g)


def _rms_bwd_rows(dout, xin, g):
    r = lax.rsqrt(jnp.mean(xin * xin, axis=-1, keepdims=True) + EPS)
    xh = xin * r
    gd = dout * g
    dx = r * (gd - xh * jnp.mean(gd * xh, axis=-1, keepdims=True))
    return dx, jnp.sum(dout * xh, axis=0, keepdims=True)


def _postnorm_bwd(dxn, y, g, name):
    t, d = y.shape
    tr = 256

    def body(dx_ref, y_ref, g_ref, dy_ref, dg_ref):
        @pl.when(pl.program_id(0) == 0)
        def _():
            dg_ref[...] = jnp.zeros_like(dg_ref)

        dy, dg = _rms_bwd_rows(dx_ref[...], y_ref[...], g_ref[...])
        dy_ref[...] = dy.astype(BF16)
        dg_ref[...] += dg

    return pl.pallas_call(body, grid=(t // tr,), in_specs=[_row_spec(tr, d), _row_spec(tr, d), _full_spec((1, d))],
                          out_specs=[_row_spec(tr, d), _full_spec((1, d))],
                          out_shape=[S((t, d), BF16), S((1, d), F32)], name=name,
                          compiler_params=_cp(("arbitrary",)))(dxn, y, g)


def _prenorm_bwd(dxn, dh, x, g, name):
    t, d = x.shape
    tr = 256

    def body(dxn_ref, dh_ref, x_ref, g_ref, dx_ref, dg_ref):
        @pl.when(pl.program_id(0) == 0)
        def _():
            dg_ref[...] = jnp.zeros_like(dg_ref)

        dx, dg = _rms_bwd_rows(dh_ref[...], x_ref[...], g_ref[...])
        dx_ref[...] = dxn_ref[...] + dx
        dg_ref[...] += dg

    return pl.pallas_call(body, grid=(t // tr,),
                          in_specs=[_row_spec(tr, d), _row_spec(tr, d), _row_spec(tr, d), _full_spec((1, d))],
                          out_specs=[_row_spec(tr, d), _full_spec((1, d))],
                          out_shape=[S((t, d), F32), S((1, d), F32)], name=name,
                          compiler_params=_cp(("arbitrary",)))(dxn, dh, x, g)


def _loss_head(y, target):
    t, d = y.shape
    tr = 256

    def body(y_ref, t_ref, l_ref, dy_ref):
        @pl.when(pl.program_id(0) == 0)
        def _():
            l_ref[...] = jnp.zeros_like(l_ref)

        e = y_ref[...] - t_ref[...]
        dy_ref[...] = e * (1.0 / d)
        l_ref[...] += 0.5 * jnp.sum(jnp.mean(e * e, axis=-1, keepdims=True), axis=0, keepdims=True)

    return pl.pallas_call(body, grid=(t // tr,), in_specs=[_row_spec(tr, d), _row_spec(tr, d)],
                          out_specs=[_full_spec((1, 1)), _row_spec(tr, d)],
                          out_shape=[S((1, 1), F32), S((t, d), F32)], name="loss_head",
                          compiler_params=_cp(("arbitrary",)))(y, target)


def _sgu_parts(proj, lg, lb, ws_ref, bst):
    a = proj[:, 0:1024]
    bb = proj[:, 1024:2048]
    z = proj[:, 2048:3072]
    u = _gelu(a)
    vp = _gelu(bb)
    mu = jnp.mean(vp, axis=-1, keepdims=True)
    xc = vp - mu
    rstd = lax.rsqrt(jnp.mean(xc * xc, axis=-1, keepdims=True) + EPS)
    xh = xc * rstd
    vn = xh * lg + lb
    row = lax.broadcasted_iota(jnp.int32, (SGU_CHUNK, SGU_CHUNK), 0)
    col = lax.broadcasted_iota(jnp.int32, (SGU_CHUNK, SGU_CHUNK), 1)
    tril = row >= col
    ws = [jnp.where(tril, ws_ref[g], 0.0).astype(BF16) for g in range(SGU_GROUPS)]
    s = jnp.concatenate(
        [_dot(ws[g], vn[:, g * 64:(g + 1) * 64].astype(BF16)) + bst[:, g:g + 1] for g in range(SGU_GROUPS)], axis=1)
    return a, bb, z, u, vp, rstd, xh, vn, ws, tril, s


def _sgu_fwd(proj, lg, lb, w_s, bst):
    t = proj.shape[0]
    tr = SGU_CHUNK

    def body(p_ref, lg_ref, lb_ref, ws_ref, bst_ref, o_ref):
        _, _, z, u, _, _, _, _, _, _, s = _sgu_parts(p_ref[...], lg_ref[...], lb_ref[...], ws_ref, bst_ref[...])
        o_ref[...] = (u * s * _silu(z)).astype(BF16)

    return pl.pallas_call(
        body, grid=(t // tr,),
        in_specs=[_row_spec(tr, 3072), _full_spec((1, 1024)), _full_spec((1, 1024)), _full_spec((16, 128, 128)),
                  _full_spec((128, 16))],
        out_specs=_row_spec(tr, 1024), out_shape=S((t, 1024), BF16), name="sgu_fwd",
        compiler_params=_cp(("parallel",)))(proj, lg, lb, w_s, bst)


def _sgu_bwd(proj, dp, lg, lb, w_s, bst):
    t = proj.shape[0]
    tr = SGU_CHUNK

    def body(p_ref, dp_ref, lg_ref, lb_ref, ws_ref, bst_ref, dproj_ref, dws_ref, dbst_ref, dlg_ref, dlb_ref):
        @pl.when(pl.program_id(0) == 0)
        def _():
            dws_ref[...] = jnp.zeros_like(dws_ref)
            dbst_ref[...] = jnp.zeros_like(dbst_ref)
            dlg_ref[...] = jnp.zeros_like(dlg_ref)
            dlb_ref[...] = jnp.zeros_like(dlb_ref)

        lgv = lg_ref[...]
        a, bb, z, u, vp, rstd, xh, vn, ws, tril, s = _sgu_parts(p_ref[...], lgv, lb_ref[...], ws_ref, bst_ref[...])
        dpv = dp_ref[...].astype(F32)
        sz = _silu(z)
        du = dpv * s * sz
        ds = dpv * u * sz
        dz = dpv * u * s * _silu_grad(z)
        dvn_parts, dbs_cols = [], []
        for g in range(SGU_GROUPS):
            dsg = ds[:, g * 64:(g + 1) * 64]
            dsg_b = dsg.astype(BF16)
            dvn_parts.append(_dot_tn(ws[g], dsg_b))
            dws_ref[g] += jnp.where(tril, _dot_nt(dsg_b, vn[:, g * 64:(g + 1) * 64].astype(BF16)), 0.0)
            dbs_cols.append(jnp.sum(dsg, axis=1, keepdims=True))
        dvn = jnp.concatenate(dvn_parts, axis=1)
        dbst_ref[...] += jnp.concatenate(dbs_cols, axis=1)
        dlg_ref[...] += jnp.sum(dvn * xh, axis=0, keepdims=True)
        dlb_ref[...] += jnp.sum(dvn, axis=0, keepdims=True)
        dxh = dvn * lgv
        dvp = rstd * (dxh - jnp.mean(dxh, axis=-1, keepdims=True) - xh * jnp.mean(dxh * xh, axis=-1, keepdims=True))
        dproj_ref[:, 0:1024] = (du * _gelu_grad(a)).astype(BF16)
        dproj_ref[:, 1024:2048] = (dvp * _gelu_grad(bb)).astype(BF16)
        dproj_ref[:, 2048:3072] = dz.astype(BF16)

    return pl.pallas_call(
        body, grid=(t // tr,),
        in_specs=[_row_spec(tr, 3072), _row_spec(tr, 1024), _full_spec((1, 1024)), _full_spec((1, 1024)),
                  _full_spec((16, 128, 128)), _full_spec((128, 16))],
        out_specs=[_row_spec(tr, 3072), _full_spec((16, 128, 128)), _full_spec((128, 16)), _full_spec((1, 1024)),
                   _full_spec((1, 1024))],
        out_shape=[S((t, 3072), BF16), S((16, 128, 128), F32), S((128, 16), F32), S((1, 1024), F32),
                   S((1, 1024), F32)],
        name="sgu_bwd", compiler_params=_cp(("arbitrary",)))(proj, dp, lg, lb, w_s, bst)


def _t5_bucket_table():
    qi = np.arange(WINDOW)[:, None]
    kj = np.arange(2 * WINDOW)[None, :]
    dist = np.maximum(qi + WINDOW - kj, 0)
    max_exact = REL_BUCKETS // 2
    dist_f = np.maximum(dist, 1).astype(np.float32)
    large = max_exact + (np.log(dist_f / np.float32(max_exact)) / np.float32(math.log(REL_MAX_DIST / max_exact))
                         * np.float32(REL_BUCKETS - max_exact)).astype(np.int32)
    large = np.minimum(large, REL_BUCKETS - 1)
    return np.where(dist < max_exact, dist, large).astype(np.int32)


def _swa_bias(rel_bias, bucket):
    def body(rb_ref, bk_ref, o_ref):
        bk = bk_ref[...]
        for h in range(SWA_HEADS):
            acc = jnp.zeros((WINDOW, 2 * WINDOW), F32)
            for b in range(REL_BUCKETS):
                acc = jnp.where(bk == b, rb_ref[b, h], acc)
            o_ref[h] = acc

    return pl.pallas_call(
        body, in_specs=[pl.BlockSpec(memory_space=pltpu.SMEM), pl.BlockSpec(memory_space=pltpu.VMEM)],
        out_specs=pl.BlockSpec(memory_space=pltpu.VMEM), out_shape=S((SWA_HEADS, WINDOW, 2 * WINDOW), F32),
        name="swa_bias")(rel_bias, bucket)


def _swa_dbias(ds_sum, dsink_rows, bucket):
    def body(ds_ref, dsk_ref, bk_ref, drb_ref, dsink_ref):
        bk = bk_ref[...]
        r = lax.broadcasted_iota(jnp.int32, (REL_BUCKETS, SWA_HEADS), 0)
        c = lax.broadcasted_iota(jnp.int32, (REL_BUCKETS, SWA_HEADS), 1)
        out = jnp.zeros((REL_BUCKETS, SWA_HEADS), F32)
        for b in range(REL_BUCKETS):
            m = bk == b
            for h in range(SWA_HEADS):
                val = jnp.sum(jnp.where(m, ds_ref[h], 0.0))
                out = jnp.where((r == b) & (c == h), val, out)
        drb_ref[...] = out
        dsink_ref[...] = jnp.sum(dsk_ref[...], axis=0, keepdims=True)

    return pl.pallas_call(
        body, out_shape=[S((REL_BUCKETS, SWA_HEADS), F32), S((1, SWA_HEADS), F32)], name="swa_dbias",
    )(ds_sum, dsink_rows, bucket)


def _swa_band(cur_ref, prev_ref):
    kband = jnp.concatenate([prev_ref[:, 0:128], cur_ref[:, 1024:1152]], axis=0).astype(BF16)
    vband = jnp.concatenate([prev_ref[:, 128:256], cur_ref[:, 1152:1280]], axis=0).astype(BF16)
    return kband, vband


def _swa_valid(n):
    qi = lax.broadcasted_iota(jnp.int32, (WINDOW, 2 * WINDOW), 0)
    kj = lax.broadcasted_iota(jnp.int32, (WINDOW, 2 * WINDOW), 1)
    dist = qi + WINDOW - kj
    return (dist >= 0) & (dist < WINDOW) & (n * WINDOW + kj - WINDOW >= 0)


def _swa_probs(qh, kh, bias_h, sink, valid):
    s = _dot_nt(qh, kh) * (HEAD_DIM ** -0.5) + bias_h
    s = jnp.where(valid, s, NEG_INF)
    m = jnp.maximum(jnp.max(s, axis=-1, keepdims=True), sink)
    e = jnp.exp(s - m)
    es = jnp.exp(sink - m)
    inv = 1.0 / (jnp.sum(e, axis=-1, keepdims=True) + es)
    return e * inv, es * inv


def _swa_in_specs():
    return [_row_spec(WINDOW, 2304),
            pl.BlockSpec((WINDOW, 256), lambda n: (jnp.maximum(n - 1, 0), 4)),
            _full_spec((SWA_HEADS, WINDOW, 2 * WINDOW)),
            pl.BlockSpec(memory_space=pltpu.SMEM)]


def _swa_fwd(proj, bias, sinks):
    t = proj.shape[0]

    def body(cur_ref, prev_ref, bias_ref, sink_ref, o_ref):
        n = pl.program_id(0)
        kband, vband = _swa_band(cur_ref, prev_ref)
        valid = _swa_valid(n)
        outs = []
        for h in range(SWA_HEADS):
            j = h // SWA_GROUP
            qh = cur_ref[:, h * 64:(h + 1) * 64].astype(BF16)
            p, _ = _swa_probs(qh, kband[:, j * 64:(j + 1) * 64], bias_ref[h], sink_ref[0, h], valid)
            outs.append(_dot(p.astype(BF16), vband[:, j * 64:(j + 1) * 64]))
        o = jnp.concatenate(outs, axis=1)
        o_ref[...] = (o * _silu(cur_ref[:, 1280:2304])).astype(BF16)

    return pl.pallas_call(body, grid=(t // WINDOW,), in_specs=_swa_in_specs(), out_specs=_row_spec(WINDOW, 1024),
                          out_shape=S((t, 1024), BF16), name="swa_fwd",
                          compiler_params=_cp(("parallel",)))(proj, proj, bias, sinks)


def _swa_bwd(proj, dp, bias, sinks):
    t = proj.shape[0]
    nb = t // WINDOW

    def body(cur_ref, prev_ref, bias_ref, sink_ref, dp_ref, dqz_ref, dkv_ref, dss_ref, dsk_ref):
        n = pl.program_id(0)

        @pl.when(n == 0)
        def _():
            dkv_ref[...] = jnp.zeros_like(dkv_ref)
            dss_ref[...] = jnp.zeros_like(dss_ref)
            dsk_ref[...] = jnp.zeros_like(dsk_ref)

        kband, vband = _swa_band(cur_ref, prev_ref)
        valid = _swa_valid(n)
        z = cur_ref[:, 1280:2304]
        dpv = dp_ref[...].astype(F32)
        do = dpv * _silu(z)
        outs, dqs, dsinks = [], [], []
        dk = [jnp.zeros((2 * WINDOW, 64), F32) for _ in range(2)]
        dv = [jnp.zeros((2 * WINDOW, 64), F32) for _ in range(2)]
        for h in range(SWA_HEADS):
            j = h // SWA_GROUP
            qh = cur_ref[:, h * 64:(h + 1) * 64].astype(BF16)
            kh = kband[:, j * 64:(j + 1) * 64]
            vh = vband[:, j * 64:(j + 1) * 64]
            p, ps = _swa_probs(qh, kh, bias_ref[h], sink_ref[0, h], valid)
            pb = p.astype(BF16)
            outs.append(_dot(pb, vh))
            doh = do[:, h * 64:(h + 1) * 64].astype(BF16)
            dpr = _dot_nt(doh, vh)
            dlt = jnp.sum(p * dpr, axis=-1, keepdims=True)
            ds = p * (dpr - dlt)
            dsinks.append(-ps * dlt)
            dss_ref[h] += ds
            dsb = (ds * (HEAD_DIM ** -0.5)).astype(BF16)
            dqs.append(_dot(dsb, kh))
            dk[j] = dk[j] + _dot_tn(dsb, qh)
            dv[j] = dv[j] + _dot_tn(pb, doh)
        o = jnp.concatenate(outs, axis=1)
        dqz_ref[:, 0:1024] = jnp.concatenate(dqs, axis=1).astype(BF16)
        dqz_ref[:, 1024:2048] = (dpv * o * _silu_grad(z)).astype(BF16)
        dsk_ref[...] += jnp.concatenate(dsinks, axis=1)
        dband = jnp.concatenate([dk[0], dk[1], dv[0], dv[1]], axis=1)
        prow = pl.multiple_of(jnp.maximum(n - 1, 0) * WINDOW, WINDOW)
        dkv_ref[pl.ds(prow, WINDOW), :] += dband[0:WINDOW]
        crow = pl.multiple_of(n * WINDOW, WINDOW)
        dkv_ref[pl.ds(crow, WINDOW), :] += dband[WINDOW:]

    return pl.pallas_call(
        body, grid=(nb,), in_specs=_swa_in_specs() + [_row_spec(WINDOW, 1024)],
        out_specs=[_row_spec(WINDOW, 2048), _full_spec((t, 256)), _full_spec((SWA_HEADS, WINDOW, 2 * WINDOW)),
                   _full_spec((WINDOW, SWA_HEADS))],
        out_shape=[S((t, 2048), BF16), S((t, 256), F32), S((SWA_HEADS, WINDOW, 2 * WINDOW), F32),
                   S((WINDOW, SWA_HEADS), F32)],
        name="swa_bwd", compiler_params=_cp(("arbitrary",)))(proj, proj, bias, sinks, dp)


MLA_Z0 = MLA_Q_RANK + MLA_KV_RANK + 128
MLA_Z1 = MLA_Z0 + 1024
MLA_PROJ = MLA_Z1 + 128


def _rope_tables(t):
    inv = ROPE_BASE ** (-jnp.arange(0, MLA_ROPE, 2, dtype=F32) / MLA_ROPE)
    ang = jnp.arange(t, dtype=F32)[:, None] * inv[None, :]
    return jnp.tile(jnp.cos(ang), (1, MLA_HEADS)), jnp.tile(jnp.sin(ang), (1, MLA_HEADS))


def _mla_norms(proj, gq, gkv):
    t = proj.shape[0]
    tr = 256

    def body(p_ref, gq_ref, gkv_ref, q_ref, kv_ref):
        for lo, hi, g_ref, o_ref in ((0, 768, gq_ref, q_ref), (768, 1024, gkv_ref, kv_ref)):
            xv = p_ref[:, lo:hi]
            r = lax.rsqrt(jnp.mean(xv * xv, axis=-1, keepdims=True) + EPS)
            o_ref[...] = (xv * r * g_ref[...]).astype(BF16)

    return pl.pallas_call(
        body, grid=(t // tr,), in_specs=[_row_spec(tr, 1024), _full_spec((1, 768)), _full_spec((1, 256))],
        out_specs=[_row_spec(tr, 768), _row_spec(tr, 256)], out_shape=[S((t, 768), BF16), S((t, 256), BF16)],
        name="mla_norms", compiler_params=_cp(("parallel",)))(proj, gq, gkv)


def _mla_norms_bwd(proj, dcqn, dckvn, dkr, dz, gq, gkv):
    t = proj.shape[0]
    tr = 256

    def body(p_ref, dq_ref, dkv_ref, dkr_ref, dz_ref, gq_ref, gkv_ref, dproj_ref, dgq_ref, dgkv_ref):
        @pl.when(pl.program_id(0) == 0)
        def _():
            dgq_ref[...] = jnp.zeros_like(dgq_ref)
            dgkv_ref[...] = jnp.zeros_like(dgkv_ref)

        dx, dg = _rms_bwd_rows(dq_ref[...], p_ref[:, 0:768], gq_ref[...])
        dproj_ref[:, 0:768] = dx.astype(BF16)
        dgq_ref[...] += dg
        dx, dg = _rms_bwd_rows(dkv_ref[...], p_ref[:, 768:1024], gkv_ref[...])
        dproj_ref[:, 768:1024] = dx.astype(BF16)
        dgkv_ref[...] += dg
        dproj_ref[:, 1024:MLA_Z0] = dkr_ref[...].astype(BF16)
        dproj_ref[:, MLA_Z0:MLA_Z1] = dz_ref[...]
        dproj_ref[:, MLA_Z1:MLA_PROJ] = jnp.zeros((tr, MLA_PROJ - MLA_Z1), BF16)

    return pl.pallas_call(
        body, grid=(t // tr,),
        in_specs=[_row_spec(tr, 1024), _row_spec(tr, 768), _row_spec(tr, 256), _row_spec(tr, 128), _row_spec(tr, 1024),
                  _full_spec((1, 768)), _full_spec((1, 256))],
        out_specs=[_row_spec(tr, MLA_PROJ), _full_spec((1, 768)), _full_spec((1, 256))],
        out_shape=[S((t, MLA_PROJ), BF16), S((1, 768), F32), S((1, 256), F32)],
        name="mla_norms_bwd", compiler_params=_cp(("arbitrary",)))(proj, dcqn, dckvn, dkr, dz, gq, gkv)


def _mla_pack(qf, kvf, proj, cos, sin):
    t = qf.shape[0]
    tr = 256

    def body(q_ref, kn_ref, kr_ref, cos_ref, sin_ref, qh_ref, kh_ref):
        cs, sn = cos_ref[...], sin_ref[...]
        x1, x2 = q_ref[:, 1024:1280], q_ref[:, 1280:1536]
        r1 = x1 * cs - x2 * sn
        r2 = x2 * cs + x1 * sn
        c16, s16 = cs[:, 0:16], sn[:, 0:16]
        k1, k2 = kr_ref[:, 0:16], kr_ref[:, 16:32]
        kr1 = k1 * c16 - k2 * s16
        kr2 = k2 * c16 + k1 * s16
        zpad = jnp.zeros((tr, 32), F32)
        for h in range(MLA_HEADS):
            qh_ref[h] = jnp.concatenate([q_ref[:, h * 64:(h + 1) * 64], r1[:, h * 16:(h + 1) * 16],
                                         r2[:, h * 16:(h + 1) * 16], zpad], axis=1).astype(BF16)
            kh_ref[h] = jnp.concatenate([kn_ref[:, h * 64:(h + 1) * 64], kr1, kr2, zpad], axis=1).astype(BF16)

    hspec = pl.BlockSpec((MLA_HEADS, tr, 128), lambda i: (0, i, 0))
    return pl.pallas_call(
        body, grid=(t // tr,),
        in_specs=[_row_spec(tr, 1536), _row_spec(tr, 1024), pl.BlockSpec((tr, 128), lambda i: (i, 8)),
                  _row_spec(tr, 256), _row_spec(tr, 256)],
        out_specs=[hspec, hspec], out_shape=[S((MLA_HEADS, t, 128), BF16)] * 2,
        name="mla_pack", compiler_params=_cp(("parallel",)))(qf, kvf, proj, cos, sin)


def _mla_unpack(dqh, dkh, dv, cos, sin):
    t = dqh.shape[1]
    tr = 256

    def body(dqh_ref, dkh_ref, dv_ref, cos_ref, sin_ref, dq_ref, dkv_ref, dkr_ref):
        cs, sn = cos_ref[...], sin_ref[...]
        dqn = jnp.concatenate([dqh_ref[h, :, 0:64] for h in range(MLA_HEADS)], axis=1)
        dr1 = jnp.concatenate([dqh_ref[h, :, 64:80] for h in range(MLA_HEADS)], axis=1)
        dr2 = jnp.concatenate([dqh_ref[h, :, 80:96] for h in range(MLA_HEADS)], axis=1)
        dq_ref[:, 0:1024] = dqn.astype(BF16)
        dq_ref[:, 1024:1280] = (dr1 * cs + dr2 * sn).astype(BF16)
        dq_ref[:, 1280:1536] = (dr2 * cs - dr1 * sn).astype(BF16)
        dkv_ref[:, 0:1024] = jnp.concatenate([dkh_ref[h, :, 0:64] for h in range(MLA_HEADS)], axis=1).astype(BF16)
        dkv_ref[:, 1024:2048] = dv_ref[...].astype(BF16)
        d1 = dkh_ref[0, :, 64:80]
        d2 = dkh_ref[0, :, 80:96]
        for h in range(1, MLA_HEADS):
            d1 = d1 + dkh_ref[h, :, 64:80]
            d2 = d2 + dkh_ref[h, :, 80:96]
        c16, s16 = cs[:, 0:16], sn[:, 0:16]
        dkr_ref[...] = jnp.concatenate([d1 * c16 + d2 * s16, d2 * c16 - d1 * s16, jnp.zeros((tr, 96), F32)], axis=1)

    hspec = pl.BlockSpec((MLA_HEADS, tr, 128), lambda i: (0, i, 0))
    return pl.pallas_call(
        body, grid=(t // tr,), in_specs=[hspec, hspec, _row_spec(tr, 1024), _row_spec(tr, 256), _row_spec(tr, 256)],
        out_specs=[_row_spec(tr, 1536), _row_spec(tr, 2048), _row_spec(tr, 128)],
        out_shape=[S((t, 1536), BF16), S((t, 2048), BF16), S((t, 128), F32)],
        name="mla_unpack", compiler_params=_cp(("parallel",)))(dqh, dkh, dv, cos, sin)


def _mla_attn_fwd(qh, kh, kvf, proj):
    t = qh.shape[1]
    b = MLA_BLK

    def body(q_ref, k_ref, v_ref, z_ref, o_ref, p_ref, lse_ref):
        r = pl.program_id(1)
        row = lax.broadcasted_iota(jnp.int32, (b, b), 0)
        col = lax.broadcasted_iota(jnp.int32, (b, b), 1)
        qs = [q_ref[0], q_ref[1]]

        def block(c0, carry, masked):
            new = []
            for a in range(2):
                m, l, acc = carry[a]
                k = k_ref[a, pl.ds(c0, b), :]
                v = v_ref[pl.ds(c0, b), a * 64:(a + 1) * 64].astype(BF16)
                s = _dot_nt(qs[a], k) * MLA_SCALE
                if masked:
                    s = jnp.where(col <= row, s, NEG_INF)
                mn = jnp.maximum(m, jnp.max(s, axis=-1, keepdims=True))
                alpha = jnp.exp(m - mn)
                p = jnp.exp(s - mn)
                new.append((mn, alpha * l + jnp.sum(p, axis=-1, keepdims=True), alpha * acc + _dot(p.astype(BF16), v)))
            return tuple(new)

        init = (jnp.full((b, 1), NEG_INF, F32), jnp.zeros((b, 1), F32), jnp.zeros((b, 64), F32))
        carry = lax.fori_loop(0, r, lambda c, cr: block(pl.multiple_of(c * b, b), cr, False), (init, init))
        carry = block(pl.multiple_of(r * b, b), carry, True)
        outs = [acc / l for _, l, acc in carry]
        lses = [jnp.broadcast_to(m + jnp.log(l), (b, 64)) for m, l, _ in carry]
        o = jnp.concatenate(outs, axis=1)
        o_ref[...] = o
        p_ref[...] = (o * _silu(z_ref[...])).astype(BF16)
        lse_ref[0] = jnp.concatenate(lses, axis=1)

    return pl.pallas_call(
        body, grid=(MLA_HEADS // 2, t // b),
        in_specs=[pl.BlockSpec((2, b, 128), lambda i, r: (i, r, 0)), pl.BlockSpec((2, t, 128), lambda i, r: (i, 0, 0)),
                  pl.BlockSpec((t, 128), lambda i, r: (0, 8 + i)), pl.BlockSpec((b, 128), lambda i, r: (r, 9 + i))],
        out_specs=[pl.BlockSpec((b, 128), lambda i, r: (r, i)), pl.BlockSpec((b, 128), lambda i, r: (r, i)),
                   pl.BlockSpec((1, b, 128), lambda i, r: (i, r, 0))],
        out_shape=[S((t, 1024), F32), S((t, 1024), BF16), S((MLA_HEADS // 2, t, 128), F32)],
        name="mla_attn_fwd", compiler_params=_cp(("parallel", "parallel")))(qh, kh, kvf, proj)


def _mla_attn_bwd(qh, kh, kvf, proj, o, dpout, lse):
    t = qh.shape[1]
    b = MLA_BLK
    nb = t // b

    def body(q_ref, k_ref, v_ref, z_ref, o_ref, dp_ref, lse_ref, dq_ref, dk_ref, dv_ref, dz_ref):
        dk_ref[...] = jnp.zeros_like(dk_ref)
        dv_ref[...] = jnp.zeros_like(dv_ref)
        row = lax.broadcasted_iota(jnp.int32, (b, b), 0)
        col = lax.broadcasted_iota(jnp.int32, (b, b), 1)
        lanes = [slice(0, 64), slice(64, 128)]

        def rbody(r, carry):
            r0 = pl.multiple_of(r * b, b)
            rows = pl.ds(r0, b)
            qs, dobs, deltas, lses = [], [], [], []
            for a in range(2):
                zz = z_ref[rows, lanes[a]]
                oo = o_ref[rows, lanes[a]]
                dpo = dp_ref[rows, lanes[a]].astype(F32)
                do = dpo * _silu(zz)
                dz_ref[rows, lanes[a]] = (dpo * oo * _silu_grad(zz)).astype(BF16)
                qs.append(q_ref[a, rows, :])
                dobs.append(do.astype(BF16))
                deltas.append(jnp.sum(do * oo, axis=-1, keepdims=True))
                lses.append(lse_ref[0, rows, a * 64:a * 64 + 1])

            def block(c0, dqs, masked):
                cols = pl.ds(c0, b)
                out = []
                for a in range(2):
                    k = k_ref[a, cols, :]
                    v = v_ref[cols, lanes[a]].astype(BF16)
                    s = _dot_nt(qs[a], k) * MLA_SCALE
                    if masked:
                        s = jnp.where(col <= row, s, NEG_INF)
                    p = jnp.exp(s - lses[a])
                    ds = p * (_dot_nt(dobs[a], v) - deltas[a]) * MLA_SCALE
                    dsb = ds.astype(BF16)
                    dk_ref[a, cols, :] += _dot_tn(dsb, qs[a])
                    dv_ref[cols, lanes[a]] += _dot_tn(p.astype(BF16), dobs[a])
                    out.append(dqs[a] + _dot(dsb, k))
                return tuple(out)

            zero = jnp.zeros((b, 128), F32)
            dqs = lax.fori_loop(0, r, lambda c, d: block(pl.multiple_of(c * b, b), d, False), (zero, zero))
            dqs = block(r0, dqs, True)
            dq_ref[0, rows, :] = dqs[0]
            dq_ref[1, rows, :] = dqs[1]
            return carry

        lax.fori_loop(0, nb, rbody, 0)

    pair3 = pl.BlockSpec((2, t, 128), lambda i: (i, 0, 0))
    return pl.pallas_call(
        body, grid=(MLA_HEADS // 2,),
        in_specs=[pair3, pair3, pl.BlockSpec((t, 128), lambda i: (0, 8 + i)), pl.BlockSpec((t, 128), lambda i: (0, 9 + i)),
                  pl.BlockSpec((t, 128), lambda i: (0, i)), pl.BlockSpec((t, 128), lambda i: (0, i)),
                  pl.BlockSpec((1, t, 128), lambda i: (i, 0, 0))],
        out_specs=[pair3, pair3, pl.BlockSpec((t, 128), lambda i: (0, i)), pl.BlockSpec((t, 128), lambda i: (0, i))],
        out_shape=[S((MLA_HEADS, t, 128), F32), S((MLA_HEADS, t, 128), F32), S((t, 1024), F32), S((t, 1024), BF16)],
        name="mla_attn_bwd", compiler_params=_cp(("parallel",), VMEM_BIG))(qh, kh, kvf, proj, o, dpout, lse)


def _dot_nt_hi(a, b):
    return lax.dot_general(a, b, (((1,), (1,)), ((), ())), preferred_element_type=F32,
                           precision=lax.Precision.HIGHEST)


def _ssm_gen(lr, li, ldt, btr, bti, cr, ci):
    n = SSM_BLOCK
    dt = jnp.exp(ldt)
    x = lr * dt
    w = li * dt
    mag = jnp.exp(x)
    ab_re = mag * jnp.cos(w)
    ab_im = mag * jnp.sin(w)
    den = lr * lr + li * li
    nr = ab_re - 1.0
    f_re = (nr * lr + ab_im * li) / den
    f_im = (ab_im * lr - nr * li) / den
    bb_re = f_re * btr - f_im * bti
    bb_im = f_re * bti + f_im * btr
    inv_mag = jnp.exp(-x)
    inv_re, inv_im = inv_mag * jnp.cos(w), -inv_mag * jnp.sin(w)
    one, zero = jnp.ones_like(x), jnp.zeros_like(x)
    pos, neg = [(one, zero)], [(one, zero)]
    for _ in range(SSM_CHUNK):
        pr, pi = pos[-1]
        pos.append((pr * ab_re - pi * ab_im, pr * ab_im + pi * ab_re))
        pr, pi = neg[-1]
        neg.append((pr * inv_re - pi * inv_im, pr * inv_im + pi * inv_re))

    def stack(table, idx):
        return (jnp.concatenate([jnp.broadcast_to(table[idx(t)][0], (SSM_GROUP, SSM_STATE)) for t in range(SSM_CHUNK)], axis=0),
                jnp.concatenate([jnp.broadcast_to(table[idx(t)][1], (SSM_GROUP, SSM_STATE)) for t in range(SSM_CHUNK)], axis=0))

    def rep(v):
        return jnp.concatenate([v] * SSM_CHUNK, axis=0)

    ct_re, ct_im, bt_re, bt_im = rep(cr), rep(ci), rep(bb_re), rep(bb_im)
    p_re, p_im = stack(pos, lambda t: t)
    l_re = ct_re * p_re - ct_im * p_im
    l_im = ct_re * p_im + ct_im * p_re
    n_re, n_im = stack(neg, lambda t: t)
    r_re = bt_re * n_re - bt_im * n_im
    r_im = bt_re * n_im + bt_im * n_re
    rk = lax.broadcasted_iota(jnp.int32, (n, n), 0) // SSM_GROUP
    ct = lax.broadcasted_iota(jnp.int32, (n, n), 1) // SSM_GROUP
    mt = jnp.where(ct >= rk, _dot_nt_hi(r_re, l_re) - _dot_nt_hi(r_im, l_im), 0.0)
    e_re, e_im = stack(pos, lambda t: SSM_CHUNK - 1 - t)
    pin_re = bt_re * e_re - bt_im * e_im
    pin_im = bt_re * e_im + bt_im * e_re
    q_re, q_im = stack(pos, lambda t: t + 1)
    qt_re = ct_re * q_re - ct_im * q_im
    qt_im = -(ct_re * q_im + ct_im * q_re)
    ad_re, ad_im = pos[SSM_CHUNK]
    return mt, pin_re, pin_im, qt_re, qt_im, ad_re, ad_im


_SSM_GEN_OUT = [(SSM_BLOCK, SSM_BLOCK)] + [(SSM_BLOCK, SSM_STATE)] * 4 + [(1, SSM_STATE)] * 2
_SSM_GEN_IN = [(1, SSM_STATE)] * 3 + [(SSM_GROUP, SSM_STATE)] * 4


def _gspec(shape2):
    return pl.BlockSpec((1,) + shape2, lambda g: (g, 0, 0))


def _ssm_operators(params):
    def body(*refs):
        ins, outs = refs[:7], refs[7:]
        for o_ref, val in zip(outs, _ssm_gen(*[r[0] for r in ins])):
            o_ref[0] = val

    return pl.pallas_call(
        body, grid=(SSM_GROUPS,), in_specs=[_gspec(s) for s in _SSM_GEN_IN], out_specs=[_gspec(s) for s in _SSM_GEN_OUT],
        out_shape=[S((SSM_GROUPS,) + s, F32) for s in _SSM_GEN_OUT], name="ssm_operators",
        compiler_params=_cp(("parallel",)))(*params)


def _ssm_operators_bwd(params, cots):
    def body(*refs):
        ins, cts, outs = refs[:7], refs[7:14], refs[14:]
        _, vjp = jax.vjp(_ssm_gen, *[r[0] for r in ins])
        grads = list(vjp(tuple(r[0] for r in cts)))
        grads[2] = jnp.broadcast_to(jnp.sum(grads[2], axis=-1, keepdims=True), (1, SSM_STATE))
        for o_ref, val in zip(outs, grads):
            o_ref[0] = val

    return pl.pallas_call(
        body, grid=(SSM_GROUPS,), in_specs=[_gspec(s) for s in _SSM_GEN_IN] + [_gspec(s) for s in _SSM_GEN_OUT],
        out_specs=[_gspec(s) for s in _SSM_GEN_IN], out_shape=[S((SSM_GROUPS,) + s, F32) for s in _SSM_GEN_IN],
        name="ssm_operators_bwd", compiler_params=_cp(("parallel",)))(*params, *cots)


def _shift_rows(x, sh, row, up):
    n = x.shape[0]
    if up:
        return jnp.where(row < n - sh, pltpu.roll(x, n - sh, axis=0), 0.0)
    return jnp.where(row >= sh, pltpu.roll(x, sh, axis=0), 0.0)


def _carry_scan(sr, si, ar, ai, row, up):
    n = sr.shape[0]
    sh = 1
    while sh < n:
        tr, ti = _shift_rows(sr, sh, row, up), _shift_rows(si, sh, row, up)
        sr, si = sr + ar * tr - ai * ti, si + ar * ti + ai * tr
        ar, ai = ar * ar - ai * ai, 2.0 * ar * ai
        sh *= 2
    return sr, si


def _ssm_states(u, pr_ref, pi_ref, ar_ref, ai_ref, row):
    lre = _dot(u, pr_ref[0].astype(BF16))
    lim = _dot(u, pi_ref[0].astype(BF16))
    sr, si = _carry_scan(lre, lim, ar_ref[0], ai_ref[0], row, False)
    return _shift_rows(sr, 1, row, False), _shift_rows(si, 1, row, False)


def _ssm_core_fwd(u_r, ops):
    g, nc, n = u_r.shape

    def body(u_ref, mt_ref, pr_ref, pi_ref, qr_ref, qi_ref, ar_ref, ai_ref, y_ref):
        u = u_ref[0]
        row = lax.broadcasted_iota(jnp.int32, (nc, SSM_STATE), 0)
        s_re, s_im = _ssm_states(u, pr_ref, pi_ref, ar_ref, ai_ref, row)
        y_ref[0] = (_dot(u, mt_ref[0].astype(BF16)) + _dot_nt(s_re.astype(BF16), qr_ref[0].astype(BF16))
                    + _dot_nt(s_im.astype(BF16), qi_ref[0].astype(BF16)))

    return pl.pallas_call(
        body, grid=(g,), in_specs=[_gspec((nc, n))] + [_gspec(s) for s in _SSM_GEN_OUT], out_specs=_gspec((nc, n)),
        out_shape=S((g, nc, n), F32), name="ssm_core_fwd", compiler_params=_cp(("parallel",)))(u_r, *ops)


def _ssm_core_bwd(u_r, dy_r, ops):
    g, nc, n = u_r.shape

    def body(u_ref, dy_ref, mt_ref, pr_ref, pi_ref, qr_ref, qi_ref, ar_ref, ai_ref,
             du_ref, dmt_ref, dpr_ref, dpi_ref, dqr_ref, dqi_ref, dar_ref, dai_ref):
        u = u_ref[0]
        dy = dy_ref[0].astype(BF16)
        row = lax.broadcasted_iota(jnp.int32, (nc, SSM_STATE), 0)
        s_re, s_im = _ssm_states(u, pr_ref, pi_ref, ar_ref, ai_ref, row)
        ar, ai = ar_ref[0], ai_ref[0]
        g_re = _shift_rows(_dot(dy, qr_ref[0].astype(BF16)), 1, row, True)
        g_im = _shift_rows(_dot(dy, qi_ref[0].astype(BF16)), 1, row, True)
        l_re, l_im = _carry_scan(g_re, g_im, ar, -ai, row, True)
        lrb, lib = l_re.astype(BF16), l_im.astype(BF16)
        du_ref[0] = (_dot_nt(dy, mt_ref[0].astype(BF16)) + _dot_nt(lrb, pr_ref[0].astype(BF16))
                     + _dot_nt(lib, pi_ref[0].astype(BF16)))
        dmt_ref[0] = _dot_tn(u, dy)
        dpr_ref[0] = _dot_tn(u, lrb)
        dpi_ref[0] = _dot_tn(u, lib)
        dqr_ref[0] = _dot_tn(dy, s_re.astype(BF16))
        dqi_ref[0] = _dot_tn(dy, s_im.astype(BF16))
        dar_ref[0] = jnp.sum(l_re * s_re + l_im * s_im, axis=0, keepdims=True)
        dai_ref[0] = jnp.sum(l_im * s_re - l_re * s_im, axis=0, keepdims=True)

    return pl.pallas_call(
        body, grid=(g,), in_specs=[_gspec((nc, n)), _gspec((nc, n))] + [_gspec(s) for s in _SSM_GEN_OUT],
        out_specs=[_gspec((nc, n))] + [_gspec(s) for s in _SSM_GEN_OUT],
        out_shape=[S((g, nc, n), F32)] + [S((g,) + s, F32) for s in _SSM_GEN_OUT],
        name="ssm_core_bwd", compiler_params=_cp(("parallel",)))(u_r, dy_r, *ops)


def _to_groups(a, dtype):
    t = a.shape[0]
    return (a.astype(dtype).reshape(t // SSM_CHUNK, SSM_CHUNK, SSM_GROUPS, SSM_GROUP)
            .transpose(2, 0, 1, 3).reshape(SSM_GROUPS, t // SSM_CHUNK, SSM_BLOCK))


def _from_groups(a):
    g, nc, _ = a.shape
    return a.reshape(g, nc, SSM_CHUNK, SSM_GROUP).transpose(1, 2, 0, 3).reshape(nc * SSM_CHUNK, g * SSM_GROUP)


def _s5_act(y_core, proj, d_skip):
    t = y_core.shape[0]
    tr = 256

    def body(y_ref, u_ref, d_ref, o_ref):
        o_ref[...] = _gelu(y_ref[...] + d_ref[...] * u_ref[...])

    return pl.pallas_call(body, grid=(t // tr,), in_specs=[_row_spec(tr, 1024), _row_spec(tr, 1024), _full_spec((1, 1024))],
                          out_specs=_row_spec(tr, 1024), out_shape=S((t, 1024), F32), name="s5_act",
                          compiler_params=_cp(("parallel",)))(y_core, proj, d_skip)


def _s5_gate(yg, tg, b_glu, proj):
    t = yg.shape[0]
    tr = 256

    def body(y_ref, t_ref, b_ref, z_ref, o_ref):
        o_ref[...] = (y_ref[...] * jax.nn.sigmoid(t_ref[...] + b_ref[...]) * _silu(z_ref[...])).astype(BF16)

    return pl.pallas_call(
        body, grid=(t // tr,),
        in_specs=[_row_spec(tr, 1024), _row_spec(tr, 1024), _full_spec((1, 1024)), pl.BlockSpec((tr, 1024), lambda i: (i, 1))],
        out_specs=_row_spec(tr, 1024), out_shape=S((t, 1024), BF16), name="s5_gate",
        compiler_params=_cp(("parallel",)))(yg, tg, b_glu, proj)


def _s5_gate_bwd(dp, yg, tg, b_glu, proj):
    t = yg.shape[0]
    tr = 256

    def body(dp_ref, y_ref, t_ref, b_ref, z_ref, dt_ref, dy_ref, dz_ref, db_ref):
        @pl.when(pl.program_id(0) == 0)
        def _():
            db_ref[...] = jnp.zeros_like(db_ref)

        z = z_ref[...]
        yv = y_ref[...]
        sg = jax.nn.sigmoid(t_ref[...] + b_ref[...])
        dpv = dp_ref[...].astype(F32)
        do = dpv * _silu(z)
        dz_ref[...] = (dpv * yv * sg * _silu_grad(z)).astype(BF16)
        dy_ref[...] = do * sg
        dtv = do * yv * sg * (1.0 - sg)
        dt_ref[...] = dtv.astype(BF16)
        db_ref[...] += jnp.sum(dtv, axis=0, keepdims=True)

    return pl.pallas_call(
        body, grid=(t // tr,),
        in_specs=[_row_spec(tr, 1024), _row_spec(tr, 1024), _row_spec(tr, 1024), _full_spec((1, 1024)),
                  pl.BlockSpec((tr, 1024), lambda i: (i, 1))],
        out_specs=[_row_spec(tr, 1024), _row_spec(tr, 1024), _row_spec(tr, 1024), _full_spec((1, 1024))],
        out_shape=[S((t, 1024), BF16), S((t, 1024), F32), S((t, 1024), BF16), S((1, 1024), F32)],
        name="s5_gate_bwd", compiler_params=_cp(("arbitrary",)))(dp, yg, tg, b_glu, proj)


def _s5_act_bwd(dyg1, dyg2, y_core, proj, d_skip):
    t = y_core.shape[0]
    tr = 256

    def body(a_ref, b_ref, y_ref, u_ref, d_ref, dy_ref, du_ref, dd_ref):
        @pl.when(pl.program_id(0) == 0)
        def _():
            dd_ref[...] = jnp.zeros_like(dd_ref)

        u = u_ref[...]
        dv = (a_ref[...] + b_ref[...]) * _gelu_grad(y_ref[...] + d_ref[...] * u)
        dy_ref[...] = dv.astype(BF16)
        du_ref[...] = dv * d_ref[...]
        dd_ref[...] += jnp.sum(dv * u, axis=0, keepdims=True)

    return pl.pallas_call(
        body, grid=(t // tr,),
        in_specs=[_row_spec(tr, 1024)] * 4 + [_full_spec((1, 1024))],
        out_specs=[_row_spec(tr, 1024), _row_spec(tr, 1024), _full_spec((1, 1024))],
        out_shape=[S((t, 1024), BF16), S((t, 1024), F32), S((1, 1024), F32)],
        name="s5_act_bwd", compiler_params=_cp(("arbitrary",)))(dyg1, dyg2, y_core, proj, d_skip)


def _s5_dproj(du_skip, du_core, dz):
    t = du_skip.shape[0]
    tr = 256

    def body(a_ref, b_ref, z_ref, o_ref):
        o_ref[:, 0:1024] = (a_ref[...] + b_ref[...]).astype(BF16)
        o_ref[:, 1024:2048] = z_ref[...]

    return pl.pallas_call(body, grid=(t // tr,), in_specs=[_row_spec(tr, 1024)] * 3, out_specs=_row_spec(tr, 2048),
                          out_shape=S((t, 2048), BF16), name="s5_dproj",
                          compiler_params=_cp(("parallel",)))(du_skip, du_core, dz)


def _s5_params(lam_re, lam_im, log_dt, b_re, b_im, c_re, c_im):
    g = SSM_GROUPS
    return (lam_re.reshape(g, 1, SSM_STATE), lam_im.reshape(g, 1, SSM_STATE),
            jnp.broadcast_to(log_dt.reshape(g, 1, 1), (g, 1, SSM_STATE)),
            b_re.transpose(0, 2, 1), b_im.transpose(0, 2, 1), c_re, c_im)


def _s5_fwd(proj, params, d_skip, w_glu, b_glu):
    ops = _ssm_operators(params)
    y_core = _from_groups(_ssm_core_fwd(_to_groups(proj[:, :1024], BF16), ops))
    yg = _s5_act(y_core, proj, d_skip)
    tg = _mm(yg, w_glu, name="s5_glu")
    return _s5_gate(yg, tg, b_glu, proj), (ops, y_core, yg, tg)


def _s5_bwd(proj, res, dp, params, d_skip, w_glu, b_glu):
    ops, y_core, yg, tg = res
    dtg, dyg1, dz, db_glu = _s5_gate_bwd(dp, yg, tg, b_glu, proj)
    dyg2 = _mm(dtg, w_glu, tb=True, name="s5_glu_dx")
    dw_glu = _mm(yg, dtg, ta=True, name="s5_glu_dw")
    dy_core, du_skip, dd = _s5_act_bwd(dyg1, dyg2, y_core, proj, d_skip)
    outs = _ssm_core_bwd(_to_groups(proj[:, :1024], BF16), _to_groups(dy_core, BF16), ops)
    dproj = _s5_dproj(du_skip, _from_groups(outs[0]), dz)
    glr, gli, gdt, gbtr, gbti, gcr, gci = _ssm_operators_bwd(params, outs[1:])
    g = SSM_GROUPS
    grads = (glr.reshape(g, SSM_STATE), gli.reshape(g, SSM_STATE), gdt[:, 0, 0].reshape(1, g),
             gbtr.transpose(0, 2, 1), gbti.transpose(0, 2, 1), gcr, gci)
    return dproj, grads, dd, dw_glu, db_glu


def _perm_uq(w):
    w3 = w.reshape(w.shape[0], MLA_HEADS, MLA_NOPE + MLA_ROPE)
    return jnp.concatenate([w3[:, :, :64].reshape(-1, 1024), w3[:, :, 64:80].reshape(-1, 256),
                            w3[:, :, 80:96].reshape(-1, 256)], axis=1)


def _unperm_uq(w):
    r = w.shape[0]
    return jnp.concatenate([w[:, :1024].reshape(r, 16, 64), w[:, 1024:1280].reshape(r, 16, 16),
                            w[:, 1280:1536].reshape(r, 16, 16)], axis=2).reshape(r, 1536)


def _perm_ukv(w):
    w3 = w.reshape(w.shape[0], MLA_HEADS, 128)
    return jnp.concatenate([w3[:, :, :64].reshape(-1, 1024), w3[:, :, 64:].reshape(-1, 1024)], axis=1)


def _unperm_ukv(w):
    r = w.shape[0]
    return jnp.concatenate([w[:, :1024].reshape(r, 16, 64), w[:, 1024:].reshape(r, 16, 64)], axis=2).reshape(r, 2048)


def _mla_fwd(proj, gq, gkv, wuq, wukv, cos, sin):
    cqn, ckvn = _mla_norms(proj, gq, gkv)
    qf = _mm(cqn, wuq, name="mla_uq")
    kvf = _mm(ckvn, wukv, name="mla_ukv")
    qh, kh = _mla_pack(qf, kvf, proj, cos, sin)
    o, pout, lse = _mla_attn_fwd(qh, kh, kvf, proj)
    return pout, (cqn, ckvn, kvf, qh, kh, o, lse)


def _mla_bwd(proj, res, dpout, gq, gkv, wuq, wukv, cos, sin):
    cqn, ckvn, kvf, qh, kh, o, lse = res
    dqh, dkh, dv, dz = _mla_attn_bwd(qh, kh, kvf, proj, o, dpout, lse)
    dqf, dkvf, dkr = _mla_unpack(dqh, dkh, dv, cos, sin)
    dcqn = _mm(dqf, wuq, tb=True, name="mla_uq_dx")
    dwuq = _mm(cqn, dqf, ta=True, name="mla_uq_dw")
    dckvn = _mm(dkvf, wukv, tb=True, name="mla_ukv_dx")
    dwukv = _mm(ckvn, dkvf, ta=True, name="mla_ukv_dw")
    dproj, dgq, dgkv = _mla_norms_bwd(proj, dcqn, dckvn, dkr, dz, gq, gkv)
    return dproj, dgq, dgkv, dwuq, dwukv


_ANY = pl.BlockSpec(memory_space=pl.ANY)


def _chip_peers():
    x, y, c = lax.axis_index("x"), lax.axis_index("y"), lax.axis_index("c")
    return 2 * x + y, c, [(1 - x, y), (x, 1 - y), (1 - x, 1 - y)]


def _exchange_chips(src_of, dst_of, send_sems, recv_sems, loc_sem):
    me, c, peers = _chip_peers()
    loc = pltpu.make_async_copy(src_of(me), dst_of(me), loc_sem)
    loc.start()

    def copy(k, px, py, src_chip, dst_chip):
        return pltpu.make_async_remote_copy(src_ref=src_of(src_chip), dst_ref=dst_of(dst_chip), send_sem=send_sems.at[k],
                                            recv_sem=recv_sems.at[k], device_id=(px, py, c), device_id_type=MESH)

    sends = [copy(k, px, py, 2 * px + py, me) for k, (px, py) in enumerate(peers)]
    for cp in sends:
        cp.start()
    for k, (px, py) in enumerate(peers):
        copy(k, px, py, me, 2 * px + py).wait_recv()
    for cp in sends:
        cp.wait_send()
    loc.wait()


_EXCH_SCRATCH = [pltpu.SemaphoreType.DMA((3,)), pltpu.SemaphoreType.DMA((3,)), pltpu.SemaphoreType.DMA(())]


def _all_gather_chips(xs, name):
    def body(x_ref, o_ref, send_sems, recv_sems, loc_sem):
        _exchange_chips(lambda s: x_ref, lambda s: o_ref.at[s], send_sems, recv_sems, loc_sem)

    return pl.pallas_call(body, in_specs=[_ANY], out_specs=_ANY, out_shape=S((N_CHIPS,) + xs.shape, xs.dtype),
                          scratch_shapes=_EXCH_SCRATCH, name=name)(xs)


def _all_to_all_chips(g, name):
    def body(g_ref, o_ref, send_sems, recv_sems, loc_sem):
        _exchange_chips(lambda s: g_ref.at[s], lambda s: o_ref.at[s], send_sems, recv_sems, loc_sem)

    return pl.pallas_call(body, in_specs=[_ANY], out_specs=_ANY, out_shape=S(g.shape, g.dtype),
                          scratch_shapes=_EXCH_SCRATCH, name=name)(g)


def _sibling_swap(p, name):
    def body(p_ref, q_ref, send_sem, recv_sem):
        x, y, c = lax.axis_index("x"), lax.axis_index("y"), lax.axis_index("c")
        cp = pltpu.make_async_remote_copy(src_ref=p_ref, dst_ref=q_ref, send_sem=send_sem, recv_sem=recv_sem,
                                          device_id=(x, y, 1 - c), device_id_type=MESH)
        cp.start()
        cp.wait()

    return pl.pallas_call(body, in_specs=[_ANY], out_specs=_ANY, out_shape=S(p.shape, p.dtype),
                          scratch_shapes=[pltpu.SemaphoreType.DMA(()), pltpu.SemaphoreType.DMA(())], name=name)(p)


def _sum_chips(l):
    _, r, c = l.shape
    tr = 128

    def body(l_ref, o_ref):
        o_ref[...] = ((l_ref[0] + l_ref[1]) + l_ref[2]) + l_ref[3]

    return pl.pallas_call(body, grid=(r // tr,), in_specs=[pl.BlockSpec((N_CHIPS, tr, c), lambda i: (0, i, 0))],
                          out_specs=_row_spec(tr, c), out_shape=S((r, c), F32), name="sum_chips",
                          compiler_params=_cp(("parallel",)))(l)


def _add2(p, q):
    r, c = p.shape
    tr = 128

    def body(p_ref, q_ref, o_ref):
        o_ref[...] = p_ref[...] + q_ref[...]

    return pl.pallas_call(body, grid=(r // tr,), in_specs=[_row_spec(tr, c), _row_spec(tr, c)], out_specs=_row_spec(tr, c),
                          out_shape=S((r, c), F32), name="add_cores", compiler_params=_cp(("parallel",)))(p, q)


def _adamw(w, g, m, v, name):
    r, c = w.shape
    tr = 256 if r % 256 == 0 else (128 if r % 128 == 0 else r)
    c1 = 1.0 / (1.0 - ADAM_B1 ** ADAM_STEP)
    c2 = 1.0 / (1.0 - ADAM_B2 ** ADAM_STEP)

    def body(w_ref, g_ref, m_ref, v_ref, d_ref, nm_ref, nv_ref):
        gv = g_ref[...]
        nm = ADAM_B1 * m_ref[...] + (1.0 - ADAM_B1) * gv
        nv = ADAM_B2 * v_ref[...] + (1.0 - ADAM_B2) * (gv * gv)
        nm_ref[...] = nm
        nv_ref[...] = nv
        d_ref[...] = -ADAM_LR * ((nm * c1) / (jnp.sqrt(nv * c2) + ADAM_EPS) + ADAM_WD * w_ref[...])

    return pl.pallas_call(body, grid=(r // tr,), in_specs=[_row_spec(tr, c)] * 4, out_specs=[_row_spec(tr, c)] * 3,
                          out_shape=[S((r, c), F32)] * 3, name=name, compiler_params=_cp(("parallel",)))(w, g, m, v)


_WEIGHTS = ['pre_norm', 'post_norm', 'rel_bias', 'a_w_in', 'a_lam_re', 'a_lam_im', 'a_log_dt', 'a_b_re', 'a_b_im',
            'a_c_re', 'a_c_im', 'a_d', 'a_w_glu', 'a_b_glu', 'a_w_out', 'b_w_in', 'b_sinks', 'b_w_out', 'c_w_in',
            'c_q_norm', 'c_kv_norm', 'c_w_uq', 'c_w_ukv', 'c_w_out', 'd_w_in', 'd_ln_g', 'd_ln_b', 'd_w_s', 'd_b_s',
            'd_w_out']
_BIG = [('a_w_in', 1024, 2048, 1), ('a_w_glu', 1024, 1024, 0), ('a_w_out', 1024, 1024, 0), ('b_w_in', 1024, 2304, 1),
        ('b_w_out', 1024, 1024, 0), ('c_w_in', 1024, 2080, 1), ('c_w_uq', 768, 1536, 1), ('c_w_ukv', 256, 2048, 1),
        ('c_w_out', 1024, 1024, 0), ('d_w_in', 1024, 3072, 1), ('d_w_out', 1024, 1024, 0)]
_SHARDED_VECS = [('c_q_norm', 768), ('c_kv_norm', 256), ('d_ln_g', 1024), ('d_ln_b', 1024)]
_REPLICATED = [n for n in _WEIGHTS if n not in [b[0] for b in _BIG] and n not in [s[0] for s in _SHARDED_VECS]]
_LANES = 1024
_BIG_ROWS = sum(r * c for _, r, c, _ in _BIG) // N_CHIPS // _LANES
_SMALL_Q_ROWS = 136
_GRAD_ROWS = 4224
_AG_ROWS = 4080
_SMALL_PACK_ROWS = 640


def _pad_rows(flat, rows):
    return jnp.pad(flat, (0, rows * _LANES - flat.shape[0])).reshape(rows, _LANES)


def _shards_of(full, dim):
    r, c = full.shape
    if dim == 0:
        return full.reshape(N_CHIPS, -1)
    return full.reshape(r, N_CHIPS, c // N_CHIPS).transpose(1, 0, 2).reshape(N_CHIPS, -1)


def _whole_of(shards, r, c, dim):
    if dim == 0:
        return shards.reshape(r, c)
    return shards.reshape(N_CHIPS, r, c // N_CHIPS).transpose(1, 0, 2).reshape(r, c)


def kernel(x, pre_norm, post_norm, rel_bias, a_w_in, a_lam_re, a_lam_im, a_log_dt, a_b_re, a_b_im, a_c_re, a_c_im, a_d, a_w_glu, a_b_glu, a_w_out, b_w_in, b_sinks, b_w_out, c_w_in, c_q_norm, c_kv_norm, c_w_uq, c_w_ukv, c_w_out, d_w_in, d_ln_g, d_ln_b, d_w_s, d_b_s, d_w_out, loss_target, m_pre_norm, m_post_norm, m_rel_bias, m_a_w_in, m_a_lam_re, m_a_lam_im, m_a_log_dt, m_a_b_re, m_a_b_im, m_a_c_re, m_a_c_im, m_a_d, m_a_w_glu, m_a_b_glu, m_a_w_out, m_b_w_in, m_b_sinks, m_b_w_out, m_c_w_in, m_c_q_norm, m_c_kv_norm, m_c_w_uq, m_c_w_ukv, m_c_w_out, m_d_w_in, m_d_ln_g, m_d_ln_b, m_d_w_s, m_d_b_s, m_d_w_out, v_pre_norm, v_post_norm, v_rel_bias, v_a_w_in, v_a_lam_re, v_a_lam_im, v_a_log_dt, v_a_b_re, v_a_b_im, v_a_c_re, v_a_c_im, v_a_d, v_a_w_glu, v_a_b_glu, v_a_w_out, v_b_w_in, v_b_sinks, v_b_w_out, v_c_w_in, v_c_q_norm, v_c_kv_norm, v_c_w_uq, v_c_w_ukv, v_c_w_out, v_d_w_in, v_d_ln_g, v_d_ln_b, v_d_w_s, v_d_b_s, v_d_w_out):
    arg = dict(locals())
    me = 2 * lax.axis_index("x") + lax.axis_index("y")
    xin = x[0]
    t = xin.shape[0]

    parts = [arg[n].astype(BF16).reshape(-1) for n, _, _, _ in _BIG]
    parts += [lax.bitcast_convert_type(lax.bitcast_convert_type(arg[n].reshape(-1), jnp.uint16), BF16).reshape(-1)
              for n, _ in _SHARDED_VECS]
    gathered = _all_gather_chips(_pad_rows(jnp.concatenate(parts), _AG_ROWS), "gather_weights").reshape(N_CHIPS, -1)
    wt, off = {}, 0
    for n, r, c, dim in _BIG:
        sz = r * c // N_CHIPS
        wt[n] = _whole_of(gathered[:, off:off + sz], r, c, dim)
        off += sz
    vec = {}
    for n, sz in _SHARDED_VECS:
        q = sz // N_CHIPS
        bits = lax.bitcast_convert_type(gathered[:, off:off + 2 * q].reshape(N_CHIPS, q, 2), jnp.uint16)
        vec[n] = lax.bitcast_convert_type(bits, F32).reshape(1, sz)
        off += 2 * q
    w_c_in = jnp.concatenate([wt['c_w_in'][:, :1056], jnp.zeros((1024, 96), BF16), wt['c_w_in'][:, 1056:],
                              jnp.zeros((1024, MLA_PROJ - MLA_Z1), BF16)], axis=1)
    w_uq = _perm_uq(wt['c_w_uq'])
    w_ukv = _perm_ukv(wt['c_w_ukv'])
    cos, sin = _rope_tables(t)
    bucket = jnp.asarray(_t5_bucket_table())
    s5p = _s5_params(a_lam_re[0], a_lam_im[0], a_log_dt[0], a_b_re[0], a_b_im[0], a_c_re[0], a_c_im[0])
    bst = d_b_s[0].T

    x0 = xin
    h0 = _prenorm(x0, pre_norm[0:1], "pre0")
    proj0 = _mm(h0, wt['a_w_in'], name="a_in")
    p0, res0 = _s5_fwd(proj0, s5p, a_d, wt['a_w_glu'], a_b_glu)
    y0 = _mm(p0, wt['a_w_out'], name="a_out")
    x1 = _postnorm(x0, y0, post_norm[0:1], "post0")

    h1 = _prenorm(x1, pre_norm[1:2], "pre1")
    proj1 = _mm(h1, wt['b_w_in'], name="b_in")
    bias = _swa_bias(rel_bias, bucket)
    p1 = _swa_fwd(proj1, bias, b_sinks)
    y1 = _mm(p1, wt['b_w_out'], name="b_out")
    x2 = _postnorm(x1, y1, post_norm[1:2], "post1")

    h2 = _prenorm(x2, pre_norm[2:3], "pre2")
    proj2 = _mm(h2, w_c_in, name="c_in")
    p2, res2 = _mla_fwd(proj2, vec['c_q_norm'], vec['c_kv_norm'], w_uq, w_ukv, cos, sin)
    y2 = _mm(p2, wt['c_w_out'], name="c_out")
    x3 = _postnorm(x2, y2, post_norm[2:3], "post2")

    h3 = _prenorm(x3, pre_norm[3:4], "pre3")
    proj3 = _mm(h3, wt['d_w_in'], name="d_in")
    p3 = _sgu_fwd(proj3, vec['d_ln_g'], vec['d_ln_b'], d_w_s[0], bst)
    y3 = _mm(p3, wt['d_w_out'], name="d_out")
    x4 = _postnorm(x3, y3, post_norm[3:4], "post3")

    loss_part, dx = _loss_head(x4, loss_target[0])
    loss = lax.psum(loss_part[0, 0], ("x", "y", "c"))

    gr = {}
    d_pre, d_post = [None] * 4, [None] * 4

    dy, d_post[3] = _postnorm_bwd(dx, y3, post_norm[3:4], "post3_bwd")
    dp = _mm(dy, wt['d_w_out'], tb=True, name="d_out_dx")
    gr['d_w_out'] = _mm(p3, dy, ta=True, name="d_out_dw")
    dproj, gr['d_w_s'], dbst, gr['d_ln_g'], gr['d_ln_b'] = _sgu_bwd(proj3, dp, vec['d_ln_g'], vec['d_ln_b'], d_w_s[0], bst)
    gr['d_b_s'] = dbst.T
    dh = _mm(dproj, wt['d_w_in'], tb=True, name="d_in_dx")
    gr['d_w_in'] = _mm(h3, dproj, ta=True, name="d_in_dw")
    dx, d_pre[3] = _prenorm_bwd(dx, dh, x3, pre_norm[3:4], "pre3_bwd")

    dy, d_post[2] = _postnorm_bwd(dx, y2, post_norm[2:3], "post2_bwd")
    dp = _mm(dy, wt['c_w_out'], tb=True, name="c_out_dx")
    gr['c_w_out'] = _mm(p2, dy, ta=True, name="c_out_dw")
    dproj, gr['c_q_norm'], gr['c_kv_norm'], dwuq, dwukv = _mla_bwd(proj2, res2, dp, vec['c_q_norm'], vec['c_kv_norm'],
                                                                  w_uq, w_ukv, cos, sin)
    gr['c_w_uq'] = _unperm_uq(dwuq)
    gr['c_w_ukv'] = _unperm_ukv(dwukv)
    dh = _mm(dproj, w_c_in, tb=True, name="c_in_dx")
    dwc = _mm(h2, dproj, ta=True, name="c_in_dw")
    gr['c_w_in'] = jnp.concatenate([dwc[:, :1056], dwc[:, MLA_Z0:MLA_Z1]], axis=1)
    dx, d_pre[2] = _prenorm_bwd(dx, dh, x2, pre_norm[2:3], "pre2_bwd")

    dy, d_post[1] = _postnorm_bwd(dx, y1, post_norm[1:2], "post1_bwd")
    dp = _mm(dy, wt['b_w_out'], tb=True, name="b_out_dx")
    gr['b_w_out'] = _mm(p1, dy, ta=True, name="b_out_dw")
    dqz, dkv, dss, dsk = _swa_bwd(proj1, dp, bias, b_sinks)
    gr['rel_bias'], gr['b_sinks'] = _swa_dbias(dss, dsk, bucket)
    dproj = jnp.concatenate([dqz[:, :1024], dkv.astype(BF16), dqz[:, 1024:]], axis=1)
    dh = _mm(dproj, wt['b_w_in'], tb=True, name="b_in_dx")
    gr['b_w_in'] = _mm(h1, dproj, ta=True, name="b_in_dw")
    dx, d_pre[1] = _prenorm_bwd(dx, dh, x1, pre_norm[1:2], "pre1_bwd")

    dy, d_post[0] = _postnorm_bwd(dx, y0, post_norm[0:1], "post0_bwd")
    dp = _mm(dy, wt['a_w_out'], tb=True, name="a_out_dx")
    gr['a_w_out'] = _mm(p0, dy, ta=True, name="a_out_dw")
    dproj, s5g, gr['a_d'], gr['a_w_glu'], gr['a_b_glu'] = _s5_bwd(proj0, res0, dp, s5p, a_d, wt['a_w_glu'], a_b_glu)
    for n, g in zip(['a_lam_re', 'a_lam_im', 'a_log_dt', 'a_b_re', 'a_b_im', 'a_c_re', 'a_c_im'], s5g):
        gr[n] = g
    dh = _mm(dproj, wt['a_w_in'], tb=True, name="a_in_dx")
    gr['a_w_in'] = _mm(h0, dproj, ta=True, name="a_in_dw")
    dx, d_pre[0] = _prenorm_bwd(dx, dh, x0, pre_norm[0:1], "pre0_bwd")
    gr['pre_norm'] = jnp.concatenate(d_pre, axis=0)
    gr['post_norm'] = jnp.concatenate(d_post, axis=0)

    small_names = _REPLICATED + [n for n, _ in _SHARDED_VECS]
    small = jnp.concatenate([gr[n].reshape(-1) for n in small_names])
    small = jnp.pad(small, (0, N_CHIPS * _SMALL_Q_ROWS * _LANES - small.shape[0])).reshape(N_CHIPS, -1)
    contrib = jnp.concatenate([_shards_of(gr[n], dim) for n, _, _, dim in _BIG] + [small], axis=1)
    contrib = jnp.pad(contrib, ((0, 0), (0, _GRAD_ROWS * _LANES - contrib.shape[1]))).reshape(N_CHIPS, _GRAD_ROWS, _LANES)
    partial = _sum_chips(_all_to_all_chips(contrib, "scatter_grads"))
    total = _add2(partial, _sibling_swap(partial, "swap_cores"))
    small_all = _all_gather_chips(total[_BIG_ROWS:_BIG_ROWS + _SMALL_Q_ROWS], "gather_small_grads").reshape(-1)
    grads, off = {}, 0
    flat = total.reshape(-1)
    for n, r, c, dim in _BIG:
        sz = r * c // N_CHIPS
        grads[n] = flat[off:off + sz].reshape(arg[n].shape)
        off += sz
    off = 0
    for n in _REPLICATED:
        sz = math.prod(arg[n].shape)
        grads[n] = small_all[off:off + sz].reshape(arg[n].shape)
        off += sz
    for n, sz in _SHARDED_VECS:
        q = sz // N_CHIPS
        grads[n] = lax.dynamic_slice(small_all, (off + me * q,), (q,)).reshape(arg[n].shape)
        off += sz

    delta, new_m, new_v = {}, {}, {}
    for n, _, _, _ in _BIG:
        shp = arg[n].shape
        two = (shp[1], shp[2])
        outs = _adamw(arg[n].reshape(two), grads[n].reshape(two), arg['m_' + n].reshape(two),
                      arg['v_' + n].reshape(two), "adamw_" + n)
        delta[n], new_m[n], new_v[n] = [o.reshape(shp) for o in outs]

    def pack(prefix, src):
        return _pad_rows(jnp.concatenate([src[prefix + n].reshape(-1) for n in small_names]), _SMALL_PACK_ROWS)

    outs = _adamw(pack('', arg), pack('', grads), pack('m_', arg), pack('v_', arg), "adamw_small")
    outs = [o.reshape(-1) for o in outs]
    off = 0
    for n in small_names:
        shp = arg[n].shape
        sz = math.prod(shp)
        delta[n], new_m[n], new_v[n] = [o[off:off + sz].reshape(shp) for o in outs]
        off += sz

    return (loss, dx.reshape(x.shape), *[grads[n] for n in _WEIGHTS], *[delta[n] for n in _WEIGHTS],
            *[new_m[n] for n in _WEIGHTS], *[new_v[n] for n in _WEIGHTS])
```

```python
import functools
import math

import numpy as np
import jax
import jax.numpy as jnp
from jax import lax
from jax.experimental import pallas as pl
from jax.experimental.pallas import tpu as pltpu

F32 = jnp.float32
BF16 = jnp.bfloat16
S = jax.ShapeDtypeStruct
MESH = pl.DeviceIdType.MESH

D_MODEL = 1024
EPS = 1e-6
NEG_INF = -1e30
N_CHIPS = 4

SSM_GROUPS = 64
SSM_GROUP = 16
SSM_STATE = 64
SSM_CHUNK = 16
SSM_BLOCK = SSM_CHUNK * SSM_GROUP

HEAD_DIM = 64
SWA_HEADS = 16
SWA_GROUP = 8
WINDOW = 128
REL_BUCKETS = 32
REL_MAX_DIST = 128

MLA_HEADS = 16
MLA_NOPE = 64
MLA_ROPE = 32
MLA_Q_RANK = 768
MLA_KV_RANK = 256
MLA_SCALE = (MLA_NOPE + MLA_ROPE) ** -0.5
MLA_BLK = 256
ROPE_BASE = 10000.0

SGU_CHUNK = 128
SGU_GROUPS = 16
SGU_GDIM = 64

ADAM_LR = 0.001
ADAM_B1 = 0.9
ADAM_B2 = 0.999
ADAM_EPS = 1e-08
ADAM_WD = 0.01
ADAM_STEP = 10

VMEM_BIG = 56 * 1024 * 1024


def _cp(sem, vmem=None):
    return pltpu.CompilerParams(dimension_semantics=sem, vmem_limit_bytes=vmem)


def _row_spec(tr, c):
    return pl.BlockSpec((tr, c), lambda i: (i, 0))


def _full_spec(shape):
    nd = len(shape)
    return pl.BlockSpec(shape, lambda *_: (0,) * nd)


def _gelu(x):
    c = 0.7978845608028654
    return 0.5 * x * (1.0 + jnp.tanh(c * (x + 0.044715 * x * x * x)))


def _gelu_grad(x):
    c = 0.7978845608028654
    t = jnp.tanh(c * (x + 0.044715 * x * x * x))
    return 0.5 * (1.0 + t) + 0.5 * x * (1.0 - t * t) * c * (1.0 + 3.0 * 0.044715 * x * x)


def _silu(z):
    return z * jax.nn.sigmoid(z)


def _silu_grad(z):
    s = jax.nn.sigmoid(z)
    return s * (1.0 + z * (1.0 - s))


def _dot(a, b):
    return lax.dot_general(a, b, (((1,), (0,)), ((), ())), preferred_element_type=F32)


def _dot_nt(a, b):
    return lax.dot_general(a, b, (((1,), (1,)), ((), ())), preferred_element_type=F32)


def _dot_tn(a, b):
    return lax.dot_general(a, b, (((0,), (0,)), ((), ())), preferred_element_type=F32)


def _pick(n, cap):
    best = 0
    for d in range(128, min(n, cap) + 1, 128):
        if n % d == 0:
            best = d
    if best < 256:
        return n
    return best


def _mm(a, b, *, ta=False, tb=False, out_dtype=F32, name):
    m, k = (a.shape[1], a.shape[0]) if ta else a.shape
    n = b.shape[0] if tb else b.shape[1]
    assert (b.shape[1] if tb else b.shape[0]) == k
    tm, tn, tk = _pick(m, 1024), _pick(n, 1024), _pick(k, 1024)
    if tn > 1024:
        tm = _pick(m, 256)
    if tk > 1024:
        tm, tn = _pick(m, 256), _pick(n, 256)
    nk = k // tk
    dims = (((0 if ta else 1,), (1 if tb else 0,)), ((), ()))

    def body(a_ref, b_ref, o_ref, acc_ref):
        kk = pl.program_id(2)

        @pl.when(kk == 0)
        def _():
            acc_ref[...] = jnp.zeros_like(acc_ref)

        acc_ref[...] += lax.dot_general(a_ref[...].astype(BF16), b_ref[...].astype(BF16), dims,
                                        preferred_element_type=F32)

        @pl.when(kk == nk - 1)
        def _():
            o_ref[...] = acc_ref[...].astype(out_dtype)

    a_spec = pl.BlockSpec((tk, tm), lambda i, j, kk: (kk, i)) if ta else pl.BlockSpec((tm, tk), lambda i, j, kk: (i, kk))
    b_spec = pl.BlockSpec((tn, tk), lambda i, j, kk: (j, kk)) if tb else pl.BlockSpec((tk, tn), lambda i, j, kk: (kk, j))
    return pl.pallas_call(
        body, grid=(m // tm, n // tn, nk), in_specs=[a_spec, b_spec],
        out_specs=pl.BlockSpec((tm, tn), lambda i, j, kk: (i, j)), out_shape=S((m, n), out_dtype),
        scratch_shapes=[pltpu.VMEM((tm, tn), F32)], name=name,
        compiler_params=_cp(("parallel", "parallel", "arbitrary"), VMEM_BIG))(a, b)


def _prenorm(x, g, name, after=()):
    t, d = x.shape
    tr = 256

    def body(x_ref, g_ref, *rest):
        o_ref = rest[-1]
        xv = x_ref[...]
        r = lax.rsqrt(jnp.mean(xv * xv, axis=-1, keepdims=True) + EPS)
        o_ref[...] = (xv * r * g_ref[...]).astype(BF16)

    return pl.pallas_call(body, grid=(t // tr,),
                          in_specs=[_row_spec(tr, d), _full_spec((1, d))] + [pl.BlockSpec(memory_space=pl.ANY)] * len(after),
                          out_specs=_row_spec(tr, d), out_shape=S((t, d), BF16), name=name,
                          compiler_params=_cp(("parallel",)))(x, g, *after)


def _postnorm(x, y, g, name):
    t, d = x.shape
    tr = 256

    def body(x_ref, y_ref, g_ref, o_ref):
        yv = y_ref[...]
        r = lax.rsqrt(jnp.mean(yv * yv, axis=-1, keepdims=True) + EPS)
        o_ref[...] = x_ref[...] + yv * r * g_ref[...]

    return pl.pallas_call(body, grid=(t // tr,), in_specs=[_row_spec(tr, d), _row_spec(tr, d), _full_spec((1, d))],
                          out_specs=_row_spec(tr, d), out_shape=S((t, d), F32), name=name,
                          compiler_params=_cp(("parallel",)))(x, y, g)


def _rms_bwd_rows(dout, xin, g):
    r = lax.rsqrt(jnp.mean(xin * xin, axis=-1, keepdims=True) + EPS)
    xh = xin * r
    gd = dout * g
    dx = r * (gd - xh * jnp.mean(gd * xh, axis=-1, keepdims=True))
    return dx, jnp.sum(dout * xh, axis=0, keepdims=True)


def _postnorm_bwd(dxn, y, g, name, after=()):
    t, d = y.shape
    tr = 256

    def body(dx_ref, y_ref, g_ref, *rest):
        dy_ref, dg_ref = rest[-2], rest[-1]

        @pl.when(pl.program_id(0) == 0)
        def _():
            dg_ref[...] = jnp.zeros_like(dg_ref)

        dy, dg = _rms_bwd_rows(dx_ref[...], y_ref[...], g_ref[...])
        dy_ref[...] = dy.astype(BF16)
        dg_ref[...] += dg

    return pl.pallas_call(body, grid=(t // tr,),
                          in_specs=[_row_spec(tr, d), _row_spec(tr, d), _full_spec((1, d))]
                          + [pl.BlockSpec(memory_space=pl.ANY)] * len(after),
                          out_specs=[_row_spec(tr, d), _full_spec((1, d))],
                          out_shape=[S((t, d), BF16), S((1, d), F32)], name=name,
                          compiler_params=_cp(("arbitrary",)))(dxn, y, g, *after)


def _prenorm_bwd(dxn, dh, x, g, name):
    t, d = x.shape
    tr = 256

    def body(dxn_ref, dh_ref, x_ref, g_ref, dx_ref, dg_ref):
        @pl.when(pl.program_id(0) == 0)
        def _():
            dg_ref[...] = jnp.zeros_like(dg_ref)

        dx, dg = _rms_bwd_rows(dh_ref[...], x_ref[...], g_ref[...])
        dx_ref[...] = dxn_ref[...] + dx
        dg_ref[...] += dg

    return pl.pallas_call(body, grid=(t // tr,),
                          in_specs=[_row_spec(tr, d), _row_spec(tr, d), _row_spec(tr, d), _full_spec((1, d))],
                          out_specs=[_row_spec(tr, d), _full_spec((1, d))],
                          out_shape=[S((t, d), F32), S((1, d), F32)], name=name,
                          compiler_params=_cp(("arbitrary",)))(dxn, dh, x, g)


def _loss_head(y, target):
    t, d = y.shape
    tr = 256

    def body(y_ref, t_ref, l_ref, dy_ref):
        @pl.when(pl.program_id(0) == 0)
        def _():
            l_ref[...] = jnp.zeros_like(l_ref)

        e = y_ref[...] - t_ref[...]
        dy_ref[...] = e * (1.0 / d)
        l_ref[...] += 0.5 * jnp.sum(jnp.mean(e * e, axis=-1, keepdims=True), axis=0, keepdims=True)

    return pl.pallas_call(body, grid=(t // tr,), in_specs=[_row_spec(tr, d), _row_spec(tr, d)],
                          out_specs=[_full_spec((1, 1)), _row_spec(tr, d)],
                          out_shape=[S((1, 1), F32), S((t, d), F32)], name="loss_head",
                          compiler_params=_cp(("arbitrary",)))(y, target)


def _sgu_parts(proj, lg, lb, ws_ref, bst):
    a = proj[:, 0:1024]
    bb = proj[:, 1024:2048]
    z = proj[:, 2048:3072]
    u = _gelu(a)
    vp = _gelu(bb)
    mu = jnp.mean(vp, axis=-1, keepdims=True)
    xc = vp - mu
    rstd = lax.rsqrt(jnp.mean(xc * xc, axis=-1, keepdims=True) + EPS)
    xh = xc * rstd
    vn = xh * lg + lb
    row = lax.broadcasted_iota(jnp.int32, (SGU_CHUNK, SGU_CHUNK), 0)
    col = lax.broadcasted_iota(jnp.int32, (SGU_CHUNK, SGU_CHUNK), 1)
    tril = row >= col
    ws = [jnp.where(tril, ws_ref[g], 0.0).astype(BF16) for g in range(SGU_GROUPS)]
    s = jnp.concatenate(
        [_dot(ws[g], vn[:, g * 64:(g + 1) * 64].astype(BF16)) + bst[:, g:g + 1] for g in range(SGU_GROUPS)], axis=1)
    return a, bb, z, u, vp, rstd, xh, vn, ws, tril, s


def _sgu_fwd(proj, lg, lb, w_s, bst):
    t = proj.shape[0]
    tr = SGU_CHUNK

    def body(p_ref, lg_ref, lb_ref, ws_ref, bst_ref, o_ref):
        _, _, z, u, _, _, _, _, _, _, s = _sgu_parts(p_ref[...], lg_ref[...], lb_ref[...], ws_ref, bst_ref[...])
        o_ref[...] = (u * s * _silu(z)).astype(BF16)

    return pl.pallas_call(
        body, grid=(t // tr,),
        in_specs=[_row_spec(tr, 3072), _full_spec((1, 1024)), _full_spec((1, 1024)), _full_spec((16, 128, 128)),
                  _full_spec((128, 16))],
        out_specs=_row_spec(tr, 1024), out_shape=S((t, 1024), BF16), name="sgu_fwd",
        compiler_params=_cp(("parallel",)))(proj, lg, lb, w_s, bst)


def _sgu_bwd(proj, dp, lg, lb, w_s, bst):
    t = proj.shape[0]
    tr = SGU_CHUNK

    def body(p_ref, dp_ref, lg_ref, lb_ref, ws_ref, bst_ref, dproj_ref, dws_ref, dbst_ref, dlg_ref, dlb_ref):
        @pl.when(pl.program_id(0) == 0)
        def _():
            dws_ref[...] = jnp.zeros_like(dws_ref)
            dbst_ref[...] = jnp.zeros_like(dbst_ref)
            dlg_ref[...] = jnp.zeros_like(dlg_ref)
            dlb_ref[...] = jnp.zeros_like(dlb_ref)

        lgv = lg_ref[...]
        a, bb, z, u, vp, rstd, xh, vn, ws, tril, s = _sgu_parts(p_ref[...], lgv, lb_ref[...], ws_ref, bst_ref[...])
        dpv = dp_ref[...].astype(F32)
        sz = _silu(z)
        du = dpv * s * sz
        ds = dpv * u * sz
        dz = dpv * u * s * _silu_grad(z)
        dvn_parts, dbs_cols = [], []
        for g in range(SGU_GROUPS):
            dsg = ds[:, g * 64:(g + 1) * 64]
            dsg_b = dsg.astype(BF16)
            dvn_parts.append(_dot_tn(ws[g], dsg_b))
            dws_ref[g] += jnp.where(tril, _dot_nt(dsg_b, vn[:, g * 64:(g + 1) * 64].astype(BF16)), 0.0)
            dbs_cols.append(jnp.sum(dsg, axis=1, keepdims=True))
        dvn = jnp.concatenate(dvn_parts, axis=1)
        dbst_ref[...] += jnp.concatenate(dbs_cols, axis=1)
        dlg_ref[...] += jnp.sum(dvn * xh, axis=0, keepdims=True)
        dlb_ref[...] += jnp.sum(dvn, axis=0, keepdims=True)
        dxh = dvn * lgv
        dvp = rstd * (dxh - jnp.mean(dxh, axis=-1, keepdims=True) - xh * jnp.mean(dxh * xh, axis=-1, keepdims=True))
        dproj_ref[:, 0:1024] = (du * _gelu_grad(a)).astype(BF16)
        dproj_ref[:, 1024:2048] = (dvp * _gelu_grad(bb)).astype(BF16)
        dproj_ref[:, 2048:3072] = dz.astype(BF16)

    return pl.pallas_call(
        body, grid=(t // tr,),
        in_specs=[_row_spec(tr, 3072), _row_spec(tr, 1024), _full_spec((1, 1024)), _full_spec((1, 1024)),
                  _full_spec((16, 128, 128)), _full_spec((128, 16))],
        out_specs=[_row_spec(tr, 3072), _full_spec((16, 128, 128)), _full_spec((128, 16)), _full_spec((1, 1024)),
                   _full_spec((1, 1024))],
        out_shape=[S((t, 3072), BF16), S((16, 128, 128), F32), S((128, 16), F32), S((1, 1024), F32),
                   S((1, 1024), F32)],
        name="sgu_bwd", compiler_params=_cp(("arbitrary",)))(proj, dp, lg, lb, w_s, bst)


def _t5_bucket_table():
    qi = np.arange(WINDOW)[:, None]
    kj = np.arange(2 * WINDOW)[None, :]
    dist = np.maximum(qi + WINDOW - kj, 0)
    max_exact = REL_BUCKETS // 2
    dist_f = np.maximum(dist, 1).astype(np.float32)
    large = max_exact + (np.log(dist_f / np.float32(max_exact)) / np.float32(math.log(REL_MAX_DIST / max_exact))
                         * np.float32(REL_BUCKETS - max_exact)).astype(np.int32)
    large = np.minimum(large, REL_BUCKETS - 1)
    return np.where(dist < max_exact, dist, large).astype(np.int32)


def _swa_bias(rel_bias, bucket):
    def body(rb_ref, bk_ref, o_ref):
        bk = bk_ref[...]
        for h in range(SWA_HEADS):
            acc = jnp.zeros((WINDOW, 2 * WINDOW), F32)
            for b in range(REL_BUCKETS):
                acc = jnp.where(bk == b, rb_ref[b, h], acc)
            o_ref[h] = acc

    return pl.pallas_call(
        body, in_specs=[pl.BlockSpec(memory_space=pltpu.SMEM), pl.BlockSpec(memory_space=pltpu.VMEM)],
        out_specs=pl.BlockSpec(memory_space=pltpu.VMEM), out_shape=S((SWA_HEADS, WINDOW, 2 * WINDOW), F32),
        name="swa_bias")(rel_bias, bucket)


def _swa_dbias(ds_sum, dsink_rows, bucket):
    def body(ds_ref, dsk_ref, bk_ref, drb_ref, dsink_ref):
        bk = bk_ref[...]
        r = lax.broadcasted_iota(jnp.int32, (REL_BUCKETS, SWA_HEADS), 0)
        c = lax.broadcasted_iota(jnp.int32, (REL_BUCKETS, SWA_HEADS), 1)
        out = jnp.zeros((REL_BUCKETS, SWA_HEADS), F32)
        for b in range(REL_BUCKETS):
            m = bk == b
            for h in range(SWA_HEADS):
                val = jnp.sum(jnp.where(m, ds_ref[h], 0.0))
                out = jnp.where((r == b) & (c == h), val, out)
        drb_ref[...] = out
        dsink_ref[...] = jnp.sum(dsk_ref[...], axis=0, keepdims=True)

    return pl.pallas_call(
        body, out_shape=[S((REL_BUCKETS, SWA_HEADS), F32), S((1, SWA_HEADS), F32)], name="swa_dbias",
    )(ds_sum, dsink_rows, bucket)


def _swa_band(cur_ref, prev_ref):
    kband = jnp.concatenate([prev_ref[:, 0:128], cur_ref[:, 1024:1152]], axis=0).astype(BF16)
    vband = jnp.concatenate([prev_ref[:, 128:256], cur_ref[:, 1152:1280]], axis=0).astype(BF16)
    return kband, vband


def _swa_valid(n):
    qi = lax.broadcasted_iota(jnp.int32, (WINDOW, 2 * WINDOW), 0)
    kj = lax.broadcasted_iota(jnp.int32, (WINDOW, 2 * WINDOW), 1)
    dist = qi + WINDOW - kj
    return (dist >= 0) & (dist < WINDOW) & (n * WINDOW + kj - WINDOW >= 0)


def _swa_probs(qh, kh, bias_h, sink, valid):
    s = _dot_nt(qh, kh) * (HEAD_DIM ** -0.5) + bias_h
    s = jnp.where(valid, s, NEG_INF)
    m = jnp.maximum(jnp.max(s, axis=-1, keepdims=True), sink)
    e = jnp.exp(s - m)
    es = jnp.exp(sink - m)
    inv = 1.0 / (jnp.sum(e, axis=-1, keepdims=True) + es)
    return e * inv, es * inv


def _swa_in_specs():
    return [_row_spec(WINDOW, 2304),
            pl.BlockSpec((WINDOW, 256), lambda n: (jnp.maximum(n - 1, 0), 4)),
            _full_spec((SWA_HEADS, WINDOW, 2 * WINDOW)),
            pl.BlockSpec(memory_space=pltpu.SMEM)]


def _swa_fwd(proj, bias, sinks):
    t = proj.shape[0]

    def body(cur_ref, prev_ref, bias_ref, sink_ref, o_ref):
        n = pl.program_id(0)
        kband, vband = _swa_band(cur_ref, prev_ref)
        valid = _swa_valid(n)
        outs = []
        for h in range(SWA_HEADS):
            j = h // SWA_GROUP
            qh = cur_ref[:, h * 64:(h + 1) * 64].astype(BF16)
            p, _ = _swa_probs(qh, kband[:, j * 64:(j + 1) * 64], bias_ref[h], sink_ref[0, h], valid)
            outs.append(_dot(p.astype(BF16), vband[:, j * 64:(j + 1) * 64]))
        o = jnp.concatenate(outs, axis=1)
        o_ref[...] = (o * _silu(cur_ref[:, 1280:2304])).astype(BF16)

    return pl.pallas_call(body, grid=(t // WINDOW,), in_specs=_swa_in_specs(), out_specs=_row_spec(WINDOW, 1024),
                          out_shape=S((t, 1024), BF16), name="swa_fwd",
                          compiler_params=_cp(("parallel",)))(proj, proj, bias, sinks)


def _swa_bwd(proj, dp, bias, sinks):
    t = proj.shape[0]
    nb = t // WINDOW

    def body(cur_ref, prev_ref, bias_ref, sink_ref, dp_ref, dqz_ref, dkv_ref, dss_ref, dsk_ref):
        n = pl.program_id(0)

        @pl.when(n == 0)
        def _():
            dkv_ref[...] = jnp.zeros_like(dkv_ref)
            dss_ref[...] = jnp.zeros_like(dss_ref)
            dsk_ref[...] = jnp.zeros_like(dsk_ref)

        kband, vband = _swa_band(cur_ref, prev_ref)
        valid = _swa_valid(n)
        z = cur_ref[:, 1280:2304]
        dpv = dp_ref[...].astype(F32)
        do = dpv * _silu(z)
        outs, dqs, dsinks = [], [], []
        dk = [jnp.zeros((2 * WINDOW, 64), F32) for _ in range(2)]
        dv = [jnp.zeros((2 * WINDOW, 64), F32) for _ in range(2)]
        for h in range(SWA_HEADS):
            j = h // SWA_GROUP
            qh = cur_ref[:, h * 64:(h + 1) * 64].astype(BF16)
            kh = kband[:, j * 64:(j + 1) * 64]
            vh = vband[:, j * 64:(j + 1) * 64]
            p, ps = _swa_probs(qh, kh, bias_ref[h], sink_ref[0, h], valid)
            pb = p.astype(BF16)
            outs.append(_dot(pb, vh))
            doh = do[:, h * 64:(h + 1) * 64].astype(BF16)
            dpr = _dot_nt(doh, vh)
            dlt = jnp.sum(p * dpr, axis=-1, keepdims=True)
            ds = p * (dpr - dlt)
            dsinks.append(-ps * dlt)
            dss_ref[h] += ds
            dsb = (ds * (HEAD_DIM ** -0.5)).astype(BF16)
            dqs.append(_dot(dsb, kh))
            dk[j] = dk[j] + _dot_tn(dsb, qh)
            dv[j] = dv[j] + _dot_tn(pb, doh)
        o = jnp.concatenate(outs, axis=1)
        dqz_ref[:, 0:1024] = jnp.concatenate(dqs, axis=1).astype(BF16)
        dqz_ref[:, 1024:2048] = (dpv * o * _silu_grad(z)).astype(BF16)
        dsk_ref[...] += jnp.concatenate(dsinks, axis=1)
        dband = jnp.concatenate([dk[0], dk[1], dv[0], dv[1]], axis=1)
        prow = pl.multiple_of(jnp.maximum(n - 1, 0) * WINDOW, WINDOW)
        dkv_ref[pl.ds(prow, WINDOW), :] += dband[0:WINDOW]
        crow = pl.multiple_of(n * WINDOW, WINDOW)
        dkv_ref[pl.ds(crow, WINDOW), :] += dband[WINDOW:]

    return pl.pallas_call(
        body, grid=(nb,), in_specs=_swa_in_specs() + [_row_spec(WINDOW, 1024)],
        out_specs=[_row_spec(WINDOW, 2048), _full_spec((t, 256)), _full_spec((SWA_HEADS, WINDOW, 2 * WINDOW)),
                   _full_spec((WINDOW, SWA_HEADS))],
        out_shape=[S((t, 2048), BF16), S((t, 256), F32), S((SWA_HEADS, WINDOW, 2 * WINDOW), F32),
                   S((WINDOW, SWA_HEADS), F32)],
        name="swa_bwd", compiler_params=_cp(("arbitrary",)))(proj, proj, bias, sinks, dp)


MLA_Z0 = MLA_Q_RANK + MLA_KV_RANK + 128
MLA_Z1 = MLA_Z0 + 1024
MLA_PROJ = MLA_Z1 + 128


def _rope_tables(t):
    inv = ROPE_BASE ** (-jnp.arange(0, MLA_ROPE, 2, dtype=F32) / MLA_ROPE)
    ang = jnp.arange(t, dtype=F32)[:, None] * inv[None, :]
    return jnp.tile(jnp.cos(ang), (1, MLA_HEADS)), jnp.tile(jnp.sin(ang), (1, MLA_HEADS))


def _mla_norms(proj, gq, gkv):
    t = proj.shape[0]
    tr = 256

    def body(p_ref, gq_ref, gkv_ref, q_ref, kv_ref):
        for lo, hi, g_ref, o_ref in ((0, 768, gq_ref, q_ref), (768, 1024, gkv_ref, kv_ref)):
            xv = p_ref[:, lo:hi]
            r = lax.rsqrt(jnp.mean(xv * xv, axis=-1, keepdims=True) + EPS)
            o_ref[...] = (xv * r * g_ref[...]).astype(BF16)

    return pl.pallas_call(
        body, grid=(t // tr,), in_specs=[_row_spec(tr, 1024), _full_spec((1, 768)), _full_spec((1, 256))],
        out_specs=[_row_spec(tr, 768), _row_spec(tr, 256)], out_shape=[S((t, 768), BF16), S((t, 256), BF16)],
        name="mla_norms", compiler_params=_cp(("parallel",)))(proj, gq, gkv)


def _mla_norms_bwd(proj, dcqn, dckvn, dkr, dz, gq, gkv):
    t = proj.shape[0]
    tr = 256

    def body(p_ref, dq_ref, dkv_ref, dkr_ref, dz_ref, gq_ref, gkv_ref, dproj_ref, dgq_ref, dgkv_ref):
        @pl.when(pl.program_id(0) == 0)
        def _():
            dgq_ref[...] = jnp.zeros_like(dgq_ref)
            dgkv_ref[...] = jnp.zeros_like(dgkv_ref)

        dx, dg = _rms_bwd_rows(dq_ref[...], p_ref[:, 0:768], gq_ref[...])
        dproj_ref[:, 0:768] = dx.astype(BF16)
        dgq_ref[...] += dg
        dx, dg = _rms_bwd_rows(dkv_ref[...], p_ref[:, 768:1024], gkv_ref[...])
        dproj_ref[:, 768:1024] = dx.astype(BF16)
        dgkv_ref[...] += dg
        dproj_ref[:, 1024:MLA_Z0] = dkr_ref[...].astype(BF16)
        dproj_ref[:, MLA_Z0:MLA_Z1] = dz_ref[...]
        dproj_ref[:, MLA_Z1:MLA_PROJ] = jnp.zeros((tr, MLA_PROJ - MLA_Z1), BF16)

    return pl.pallas_call(
        body, grid=(t // tr,),
        in_specs=[_row_spec(tr, 1024), _row_spec(tr, 768), _row_spec(tr, 256), _row_spec(tr, 128), _row_spec(tr, 1024),
                  _full_spec((1, 768)), _full_spec((1, 256))],
        out_specs=[_row_spec(tr, MLA_PROJ), _full_spec((1, 768)), _full_spec((1, 256))],
        out_shape=[S((t, MLA_PROJ), BF16), S((1, 768), F32), S((1, 256), F32)],
        name="mla_norms_bwd", compiler_params=_cp(("arbitrary",)))(proj, dcqn, dckvn, dkr, dz, gq, gkv)


def _mla_pack(qf, kvf, proj, cos, sin):
    t = qf.shape[0]
    tr = 256

    def body(q_ref, kn_ref, kr_ref, cos_ref, sin_ref, qh_ref, kh_ref):
        cs, sn = cos_ref[...], sin_ref[...]
        x1, x2 = q_ref[:, 1024:1280], q_ref[:, 1280:1536]
        r1 = x1 * cs - x2 * sn
        r2 = x2 * cs + x1 * sn
        c16, s16 = cs[:, 0:16], sn[:, 0:16]
        k1, k2 = kr_ref[:, 0:16], kr_ref[:, 16:32]
        kr1 = k1 * c16 - k2 * s16
        kr2 = k2 * c16 + k1 * s16
        zpad = jnp.zeros((tr, 32), F32)
        for h in range(MLA_HEADS):
            qh_ref[h] = jnp.concatenate([q_ref[:, h * 64:(h + 1) * 64], r1[:, h * 16:(h + 1) * 16],
                                         r2[:, h * 16:(h + 1) * 16], zpad], axis=1).astype(BF16)
            kh_ref[h] = jnp.concatenate([kn_ref[:, h * 64:(h + 1) * 64], kr1, kr2, zpad], axis=1).astype(BF16)

    hspec = pl.BlockSpec((MLA_HEADS, tr, 128), lambda i: (0, i, 0))
    return pl.pallas_call(
        body, grid=(t // tr,),
        in_specs=[_row_spec(tr, 1536), _row_spec(tr, 1024), pl.BlockSpec((tr, 128), lambda i: (i, 8)),
                  _row_spec(tr, 256), _row_spec(tr, 256)],
        out_specs=[hspec, hspec], out_shape=[S((MLA_HEADS, t, 128), BF16)] * 2,
        name="mla_pack", compiler_params=_cp(("parallel",)))(qf, kvf, proj, cos, sin)


def _mla_unpack(dqh, dkh, dv, cos, sin):
    t = dqh.shape[1]
    tr = 256

    def body(dqh_ref, dkh_ref, dv_ref, cos_ref, sin_ref, dq_ref, dkv_ref, dkr_ref):
        cs, sn = cos_ref[...], sin_ref[...]
        dqn = jnp.concatenate([dqh_ref[h, :, 0:64] for h in range(MLA_HEADS)], axis=1)
        dr1 = jnp.concatenate([dqh_ref[h, :, 64:80] for h in range(MLA_HEADS)], axis=1)
        dr2 = jnp.concatenate([dqh_ref[h, :, 80:96] for h in range(MLA_HEADS)], axis=1)
        dq_ref[:, 0:1024] = dqn.astype(BF16)
        dq_ref[:, 1024:1280] = (dr1 * cs + dr2 * sn).astype(BF16)
        dq_ref[:, 1280:1536] = (dr2 * cs - dr1 * sn).astype(BF16)
        dkv_ref[:, 0:1024] = jnp.concatenate([dkh_ref[h, :, 0:64] for h in range(MLA_HEADS)], axis=1).astype(BF16)
        dkv_ref[:, 1024:2048] = dv_ref[...].astype(BF16)
        d1 = dkh_ref[0, :, 64:80]
        d2 = dkh_ref[0, :, 80:96]
        for h in range(1, MLA_HEADS):
            d1 = d1 + dkh_ref[h, :, 64:80]
            d2 = d2 + dkh_ref[h, :, 80:96]
        c16, s16 = cs[:, 0:16], sn[:, 0:16]
        dkr_ref[...] = jnp.concatenate([d1 * c16 + d2 * s16, d2 * c16 - d1 * s16, jnp.zeros((tr, 96), F32)], axis=1)

    hspec = pl.BlockSpec((MLA_HEADS, tr, 128), lambda i: (0, i, 0))
    return pl.pallas_call(
        body, grid=(t // tr,), in_specs=[hspec, hspec, _row_spec(tr, 1024), _row_spec(tr, 256), _row_spec(tr, 256)],
        out_specs=[_row_spec(tr, 1536), _row_spec(tr, 2048), _row_spec(tr, 128)],
        out_shape=[S((t, 1536), BF16), S((t, 2048), BF16), S((t, 128), F32)],
        name="mla_unpack", compiler_params=_cp(("parallel",)))(dqh, dkh, dv, cos, sin)


def _mla_attn_fwd(qh, kh, kvf, proj):
    t = qh.shape[1]
    b = MLA_BLK

    def body(q_ref, k_ref, v_ref, z_ref, o_ref, p_ref, lse_ref):
        r = pl.program_id(1)
        row = lax.broadcasted_iota(jnp.int32, (b, b), 0)
        col = lax.broadcasted_iota(jnp.int32, (b, b), 1)
        qs = [q_ref[0], q_ref[1]]

        def block(c0, carry, masked):
            new = []
            for a in range(2):
                m, l, acc = carry[a]
                k = k_ref[a, pl.ds(c0, b), :]
                v = v_ref[pl.ds(c0, b), a * 64:(a + 1) * 64].astype(BF16)
                s = _dot_nt(qs[a], k) * MLA_SCALE
                if masked:
                    s = jnp.where(col <= row, s, NEG_INF)
                mn = jnp.maximum(m, jnp.max(s, axis=-1, keepdims=True))
                alpha = jnp.exp(m - mn)
                p = jnp.exp(s - mn)
                new.append((mn, alpha * l + jnp.sum(p, axis=-1, keepdims=True), alpha * acc + _dot(p.astype(BF16), v)))
            return tuple(new)

        init = (jnp.full((b, 1), NEG_INF, F32), jnp.zeros((b, 1), F32), jnp.zeros((b, 64), F32))
        carry = lax.fori_loop(0, r, lambda c, cr: block(pl.multiple_of(c * b, b), cr, False), (init, init))
        carry = block(pl.multiple_of(r * b, b), carry, True)
        outs = [acc / l for _, l, acc in carry]
        lses = [jnp.broadcast_to(m + jnp.log(l), (b, 64)) for m, l, _ in carry]
        o = jnp.concatenate(outs, axis=1)
        o_ref[...] = o
        p_ref[...] = (o * _silu(z_ref[...])).astype(BF16)
        lse_ref[0] = jnp.concatenate(lses, axis=1)

    return pl.pallas_call(
        body, grid=(MLA_HEADS // 2, t // b),
        in_specs=[pl.BlockSpec((2, b, 128), lambda i, r: (i, r, 0)), pl.BlockSpec((2, t, 128), lambda i, r: (i, 0, 0)),
                  pl.BlockSpec((t, 128), lambda i, r: (0, 8 + i)), pl.BlockSpec((b, 128), lambda i, r: (r, 9 + i))],
        out_specs=[pl.BlockSpec((b, 128), lambda i, r: (r, i)), pl.BlockSpec((b, 128), lambda i, r: (r, i)),
                   pl.BlockSpec((1, b, 128), lambda i, r: (i, r, 0))],
        out_shape=[S((t, 1024), F32), S((t, 1024), BF16), S((MLA_HEADS // 2, t, 128), F32)],
        name="mla_attn_fwd", compiler_params=_cp(("parallel", "parallel")))(qh, kh, kvf, proj)


def _mla_attn_bwd(qh, kh, kvf, proj, o, dpout, lse):
    t = qh.shape[1]
    b = MLA_BLK
    nb = t // b

    def body(q_ref, k_ref, v_ref, z_ref, o_ref, dp_ref, lse_ref, dq_ref, dk_ref, dv_ref, dz_ref):
        dk_ref[...] = jnp.zeros_like(dk_ref)
        dv_ref[...] = jnp.zeros_like(dv_ref)
        row = lax.broadcasted_iota(jnp.int32, (b, b), 0)
        col = lax.broadcasted_iota(jnp.int32, (b, b), 1)
        lanes = [slice(0, 64), slice(64, 128)]

        def rbody(r, carry):
            r0 = pl.multiple_of(r * b, b)
            rows = pl.ds(r0, b)
            qs, dobs, deltas, lses = [], [], [], []
            for a in range(2):
                zz = z_ref[rows, lanes[a]]
                oo = o_ref[rows, lanes[a]]
                dpo = dp_ref[rows, lanes[a]].astype(F32)
                do = dpo * _silu(zz)
                dz_ref[rows, lanes[a]] = (dpo * oo * _silu_grad(zz)).astype(BF16)
                qs.append(q_ref[a, rows, :])
                dobs.append(do.astype(BF16))
                deltas.append(jnp.sum(do * oo, axis=-1, keepdims=True))
                lses.append(lse_ref[0, rows, a * 64:a * 64 + 1])

            def block(c0, dqs, masked):
                cols = pl.ds(c0, b)
                out = []
                for a in range(2):
                    k = k_ref[a, cols, :]
                    v = v_ref[cols, lanes[a]].astype(BF16)
                    s = _dot_nt(qs[a], k) * MLA_SCALE
                    if masked:
                        s = jnp.where(col <= row, s, NEG_INF)
                    p = jnp.exp(s - lses[a])
                    ds = p * (_dot_nt(dobs[a], v) - deltas[a]) * MLA_SCALE
                    dsb = ds.astype(BF16)
                    dk_ref[a, cols, :] += _dot_tn(dsb, qs[a])
                    dv_ref[cols, lanes[a]] += _dot_tn(p.astype(BF16), dobs[a])
                    out.append(dqs[a] + _dot(dsb, k))
                return tuple(out)

            zero = jnp.zeros((b, 128), F32)
            dqs = lax.fori_loop(0, r, lambda c, d: block(pl.multiple_of(c * b, b), d, False), (zero, zero))
            dqs = block(r0, dqs, True)
            dq_ref[0, rows, :] = dqs[0]
            dq_ref[1, rows, :] = dqs[1]
            return carry

        lax.fori_loop(0, nb, rbody, 0)

    pair3 = pl.BlockSpec((2, t, 128), lambda i: (i, 0, 0))
    return pl.pallas_call(
        body, grid=(MLA_HEADS // 2,),
        in_specs=[pair3, pair3, pl.BlockSpec((t, 128), lambda i: (0, 8 + i)), pl.BlockSpec((t, 128), lambda i: (0, 9 + i)),
                  pl.BlockSpec((t, 128), lambda i: (0, i)), pl.BlockSpec((t, 128), lambda i: (0, i)),
                  pl.BlockSpec((1, t, 128), lambda i: (i, 0, 0))],
        out_specs=[pair3, pair3, pl.BlockSpec((t, 128), lambda i: (0, i)), pl.BlockSpec((t, 128), lambda i: (0, i))],
        out_shape=[S((MLA_HEADS, t, 128), F32), S((MLA_HEADS, t, 128), F32), S((t, 1024), F32), S((t, 1024), BF16)],
        name="mla_attn_bwd", compiler_params=_cp(("parallel",), VMEM_BIG))(qh, kh, kvf, proj, o, dpout, lse)


def _dot_nt_hi(a, b):
    return lax.dot_general(a, b, (((1,), (1,)), ((), ())), preferred_element_type=F32,
                           precision=lax.Precision.HIGHEST)


def _ssm_gen(lr, li, ldt, btr, bti, cr, ci):
    n = SSM_BLOCK
    dt = jnp.exp(ldt)
    x = lr * dt
    w = li * dt
    mag = jnp.exp(x)
    ab_re = mag * jnp.cos(w)
    ab_im = mag * jnp.sin(w)
    den = lr * lr + li * li
    nr = ab_re - 1.0
    f_re = (nr * lr + ab_im * li) / den
    f_im = (ab_im * lr - nr * li) / den
    bb_re = f_re * btr - f_im * bti
    bb_im = f_re * bti + f_im * btr
    inv_mag = jnp.exp(-x)
    inv_re, inv_im = inv_mag * jnp.cos(w), -inv_mag * jnp.sin(w)
    one, zero = jnp.ones_like(x), jnp.zeros_like(x)
    pos, neg = [(one, zero)], [(one, zero)]
    for _ in range(SSM_CHUNK):
        pr, pi = pos[-1]
        pos.append((pr * ab_re - pi * ab_im, pr * ab_im + pi * ab_re))
        pr, pi = neg[-1]
        neg.append((pr * inv_re - pi * inv_im, pr * inv_im + pi * inv_re))

    def stack(table, idx):
        return (jnp.concatenate([jnp.broadcast_to(table[idx(t)][0], (SSM_GROUP, SSM_STATE)) for t in range(SSM_CHUNK)], axis=0),
                jnp.concatenate([jnp.broadcast_to(table[idx(t)][1], (SSM_GROUP, SSM_STATE)) for t in range(SSM_CHUNK)], axis=0))

    def rep(v):
        return jnp.concatenate([v] * SSM_CHUNK, axis=0)

    ct_re, ct_im, bt_re, bt_im = rep(cr), rep(ci), rep(bb_re), rep(bb_im)
    p_re, p_im = stack(pos, lambda t: t)
    l_re = ct_re * p_re - ct_im * p_im
    l_im = ct_re * p_im + ct_im * p_re
    n_re, n_im = stack(neg, lambda t: t)
    r_re = bt_re * n_re - bt_im * n_im
    r_im = bt_re * n_im + bt_im * n_re
    rk = lax.broadcasted_iota(jnp.int32, (n, n), 0) // SSM_GROUP
    ct = lax.broadcasted_iota(jnp.int32, (n, n), 1) // SSM_GROUP
    mt = jnp.where(ct >= rk, _dot_nt_hi(r_re, l_re) - _dot_nt_hi(r_im, l_im), 0.0)
    e_re, e_im = stack(pos, lambda t: SSM_CHUNK - 1 - t)
    pin_re = bt_re * e_re - bt_im * e_im
    pin_im = bt_re * e_im + bt_im * e_re
    q_re, q_im = stack(pos, lambda t: t + 1)
    qt_re = ct_re * q_re - ct_im * q_im
    qt_im = -(ct_re * q_im + ct_im * q_re)
    ad_re, ad_im = pos[SSM_CHUNK]
    return mt, pin_re, pin_im, qt_re, qt_im, ad_re, ad_im


_SSM_GEN_OUT = [(SSM_BLOCK, SSM_BLOCK)] + [(SSM_BLOCK, SSM_STATE)] * 4 + [(1, SSM_STATE)] * 2
_SSM_GEN_IN = [(1, SSM_STATE)] * 3 + [(SSM_GROUP, SSM_STATE)] * 4


def _gspec(shape2):
    return pl.BlockSpec((1,) + shape2, lambda g: (g, 0, 0))


def _ssm_operators(params):
    def body(*refs):
        ins, outs = refs[:7], refs[7:]
        for o_ref, val in zip(outs, _ssm_gen(*[r[0] for r in ins])):
            o_ref[0] = val

    return pl.pallas_call(
        body, grid=(SSM_GROUPS,), in_specs=[_gspec(s) for s in _SSM_GEN_IN], out_specs=[_gspec(s) for s in _SSM_GEN_OUT],
        out_shape=[S((SSM_GROUPS,) + s, F32) for s in _SSM_GEN_OUT], name="ssm_operators",
        compiler_params=_cp(("parallel",)))(*params)


def _ssm_operators_bwd(params, cots):
    def body(*refs):
        ins, cts, outs = refs[:7], refs[7:14], refs[14:]
        _, vjp = jax.vjp(_ssm_gen, *[r[0] for r in ins])
        grads = list(vjp(tuple(r[0] for r in cts)))
        grads[2] = jnp.broadcast_to(jnp.sum(grads[2], axis=-1, keepdims=True), (1, SSM_STATE))
        for o_ref, val in zip(outs, grads):
            o_ref[0] = val

    return pl.pallas_call(
        body, grid=(SSM_GROUPS,), in_specs=[_gspec(s) for s in _SSM_GEN_IN] + [_gspec(s) for s in _SSM_GEN_OUT],
        out_specs=[_gspec(s) for s in _SSM_GEN_IN], out_shape=[S((SSM_GROUPS,) + s, F32) for s in _SSM_GEN_IN],
        name="ssm_operators_bwd", compiler_params=_cp(("parallel",)))(*params, *cots)


def _shift_rows(x, sh, row, up):
    n = x.shape[0]
    if up:
        return jnp.where(row < n - sh, pltpu.roll(x, n - sh, axis=0), 0.0)
    return jnp.where(row >= sh, pltpu.roll(x, sh, axis=0), 0.0)


def _carry_scan(sr, si, ar, ai, row, up):
    n = sr.shape[0]
    sh = 1
    while sh < n:
        tr, ti = _shift_rows(sr, sh, row, up), _shift_rows(si, sh, row, up)
        sr, si = sr + ar * tr - ai * ti, si + ar * ti + ai * tr
        ar, ai = ar * ar - ai * ai, 2.0 * ar * ai
        sh *= 2
    return sr, si


def _ssm_states(u, pr_ref, pi_ref, ar_ref, ai_ref, row):
    lre = _dot(u, pr_ref[0].astype(BF16))
    lim = _dot(u, pi_ref[0].astype(BF16))
    sr, si = _carry_scan(lre, lim, ar_ref[0], ai_ref[0], row, False)
    return _shift_rows(sr, 1, row, False), _shift_rows(si, 1, row, False)


def _ssm_core_fwd(u_r, ops):
    g, nc, n = u_r.shape

    def body(u_ref, mt_ref, pr_ref, pi_ref, qr_ref, qi_ref, ar_ref, ai_ref, y_ref):
        u = u_ref[0]
        row = lax.broadcasted_iota(jnp.int32, (nc, SSM_STATE), 0)
        s_re, s_im = _ssm_states(u, pr_ref, pi_ref, ar_ref, ai_ref, row)
        y_ref[0] = (_dot(u, mt_ref[0].astype(BF16)) + _dot_nt(s_re.astype(BF16), qr_ref[0].astype(BF16))
                    + _dot_nt(s_im.astype(BF16), qi_ref[0].astype(BF16)))

    return pl.pallas_call(
        body, grid=(g,), in_specs=[_gspec((nc, n))] + [_gspec(s) for s in _SSM_GEN_OUT], out_specs=_gspec((nc, n)),
        out_shape=S((g, nc, n), F32), name="ssm_core_fwd", compiler_params=_cp(("parallel",)))(u_r, *ops)


def _ssm_core_bwd(u_r, dy_r, ops):
    g, nc, n = u_r.shape

    def body(u_ref, dy_ref, mt_ref, pr_ref, pi_ref, qr_ref, qi_ref, ar_ref, ai_ref,
             du_ref, dmt_ref, dpr_ref, dpi_ref, dqr_ref, dqi_ref, dar_ref, dai_ref):
        u = u_ref[0]
        dy = dy_ref[0].astype(BF16)
        row = lax.broadcasted_iota(jnp.int32, (nc, SSM_STATE), 0)
        s_re, s_im = _ssm_states(u, pr_ref, pi_ref, ar_ref, ai_ref, row)
        ar, ai = ar_ref[0], ai_ref[0]
        g_re = _shift_rows(_dot(dy, qr_ref[0].astype(BF16)), 1, row, True)
        g_im = _shift_rows(_dot(dy, qi_ref[0].astype(BF16)), 1, row, True)
        l_re, l_im = _carry_scan(g_re, g_im, ar, -ai, row, True)
        lrb, lib = l_re.astype(BF16), l_im.astype(BF16)
        du_ref[0] = (_dot_nt(dy, mt_ref[0].astype(BF16)) + _dot_nt(lrb, pr_ref[0].astype(BF16))
                     + _dot_nt(lib, pi_ref[0].astype(BF16)))
        dmt_ref[0] = _dot_tn(u, dy)
        dpr_ref[0] = _dot_tn(u, lrb)
        dpi_ref[0] = _dot_tn(u, lib)
        dqr_ref[0] = _dot_tn(dy, s_re.astype(BF16))
        dqi_ref[0] = _dot_tn(dy, s_im.astype(BF16))
        dar_ref[0] = jnp.sum(l_re * s_re + l_im * s_im, axis=0, keepdims=True)
        dai_ref[0] = jnp.sum(l_im * s_re - l_re * s_im, axis=0, keepdims=True)

    return pl.pallas_call(
        body, grid=(g,), in_specs=[_gspec((nc, n)), _gspec((nc, n))] + [_gspec(s) for s in _SSM_GEN_OUT],
        out_specs=[_gspec((nc, n))] + [_gspec(s) for s in _SSM_GEN_OUT],
        out_shape=[S((g, nc, n), F32)] + [S((g,) + s, F32) for s in _SSM_GEN_OUT],
        name="ssm_core_bwd", compiler_params=_cp(("parallel",)))(u_r, dy_r, *ops)


def _to_groups(a, dtype):
    t = a.shape[0]
    return (a.astype(dtype).reshape(t // SSM_CHUNK, SSM_CHUNK, SSM_GROUPS, SSM_GROUP)
            .transpose(2, 0, 1, 3).reshape(SSM_GROUPS, t // SSM_CHUNK, SSM_BLOCK))


def _from_groups(a):
    g, nc, _ = a.shape
    return a.reshape(g, nc, SSM_CHUNK, SSM_GROUP).transpose(1, 2, 0, 3).reshape(nc * SSM_CHUNK, g * SSM_GROUP)


def _s5_act(y_core, proj, d_skip):
    t = y_core.shape[0]
    tr = 256

    def body(y_ref, u_ref, d_ref, o_ref):
        o_ref[...] = _gelu(y_ref[...] + d_ref[...] * u_ref[...])

    return pl.pallas_call(body, grid=(t // tr,), in_specs=[_row_spec(tr, 1024), _row_spec(tr, 1024), _full_spec((1, 1024))],
                          out_specs=_row_spec(tr, 1024), out_shape=S((t, 1024), F32), name="s5_act",
                          compiler_params=_cp(("parallel",)))(y_core, proj, d_skip)


def _s5_gate(yg, tg, b_glu, proj):
    t = yg.shape[0]
    tr = 256

    def body(y_ref, t_ref, b_ref, z_ref, o_ref):
        o_ref[...] = (y_ref[...] * jax.nn.sigmoid(t_ref[...] + b_ref[...]) * _silu(z_ref[...])).astype(BF16)

    return pl.pallas_call(
        body, grid=(t // tr,),
        in_specs=[_row_spec(tr, 1024), _row_spec(tr, 1024), _full_spec((1, 1024)), pl.BlockSpec((tr, 1024), lambda i: (i, 1))],
        out_specs=_row_spec(tr, 1024), out_shape=S((t, 1024), BF16), name="s5_gate",
        compiler_params=_cp(("parallel",)))(yg, tg, b_glu, proj)


def _s5_gate_bwd(dp, yg, tg, b_glu, proj):
    t = yg.shape[0]
    tr = 256

    def body(dp_ref, y_ref, t_ref, b_ref, z_ref, dt_ref, dy_ref, dz_ref, db_ref):
        @pl.when(pl.program_id(0) == 0)
        def _():
            db_ref[...] = jnp.zeros_like(db_ref)

        z = z_ref[...]
        yv = y_ref[...]
        sg = jax.nn.sigmoid(t_ref[...] + b_ref[...])
        dpv = dp_ref[...].astype(F32)
        do = dpv * _silu(z)
        dz_ref[...] = (dpv * yv * sg * _silu_grad(z)).astype(BF16)
        dy_ref[...] = do * sg
        dtv = do * yv * sg * (1.0 - sg)
        dt_ref[...] = dtv.astype(BF16)
        db_ref[...] += jnp.sum(dtv, axis=0, keepdims=True)

    return pl.pallas_call(
        body, grid=(t // tr,),
        in_specs=[_row_spec(tr, 1024), _row_spec(tr, 1024), _row_spec(tr, 1024), _full_spec((1, 1024)),
                  pl.BlockSpec((tr, 1024), lambda i: (i, 1))],
        out_specs=[_row_spec(tr, 1024), _row_spec(tr, 1024), _row_spec(tr, 1024), _full_spec((1, 1024))],
        out_shape=[S((t, 1024), BF16), S((t, 1024), F32), S((t, 1024), BF16), S((1, 1024), F32)],
        name="s5_gate_bwd", compiler_params=_cp(("arbitrary",)))(dp, yg, tg, b_glu, proj)


def _s5_act_bwd(dyg1, dyg2, y_core, proj, d_skip):
    t = y_core.shape[0]
    tr = 256

    def body(a_ref, b_ref, y_ref, u_ref, d_ref, dy_ref, du_ref, dd_ref):
        @pl.when(pl.program_id(0) == 0)
        def _():
            dd_ref[...] = jnp.zeros_like(dd_ref)

        u = u_ref[...]
        dv = (a_ref[...] + b_ref[...]) * _gelu_grad(y_ref[...] + d_ref[...] * u)
        dy_ref[...] = dv.astype(BF16)
        du_ref[...] = dv * d_ref[...]
        dd_ref[...] += jnp.sum(dv * u, axis=0, keepdims=True)

    return pl.pallas_call(
        body, grid=(t // tr,),
        in_specs=[_row_spec(tr, 1024)] * 4 + [_full_spec((1, 1024))],
        out_specs=[_row_spec(tr, 1024), _row_spec(tr, 1024), _full_spec((1, 1024))],
        out_shape=[S((t, 1024), BF16), S((t, 1024), F32), S((1, 1024), F32)],
        name="s5_act_bwd", compiler_params=_cp(("arbitrary",)))(dyg1, dyg2, y_core, proj, d_skip)


def _s5_dproj(du_skip, du_core, dz):
    t = du_skip.shape[0]
    tr = 256

    def body(a_ref, b_ref, z_ref, o_ref):
        o_ref[:, 0:1024] = (a_ref[...] + b_ref[...]).astype(BF16)
        o_ref[:, 1024:2048] = z_ref[...]

    return pl.pallas_call(body, grid=(t // tr,), in_specs=[_row_spec(tr, 1024)] * 3, out_specs=_row_spec(tr, 2048),
                          out_shape=S((t, 2048), BF16), name="s5_dproj",
                          compiler_params=_cp(("parallel",)))(du_skip, du_core, dz)


def _s5_params(lam_re, lam_im, log_dt, b_re, b_im, c_re, c_im):
    g = SSM_GROUPS
    return (lam_re.reshape(g, 1, SSM_STATE), lam_im.reshape(g, 1, SSM_STATE),
            jnp.broadcast_to(log_dt.reshape(g, 1, 1), (g, 1, SSM_STATE)),
            b_re.transpose(0, 2, 1), b_im.transpose(0, 2, 1), c_re, c_im)


def _s5_fwd(proj, params, d_skip, w_glu, b_glu):
    ops = _ssm_operators(params)
    y_core = _from_groups(_ssm_core_fwd(_to_groups(proj[:, :1024], BF16), ops))
    yg = _s5_act(y_core, proj, d_skip)
    tg = _mm(yg, w_glu, name="s5_glu")
    return _s5_gate(yg, tg, b_glu, proj), (ops, y_core, yg, tg)


def _s5_bwd(proj, res, dp, params, d_skip, w_glu, b_glu):
    ops, y_core, yg, tg = res
    dtg, dyg1, dz, db_glu = _s5_gate_bwd(dp, yg, tg, b_glu, proj)
    dyg2 = _mm(dtg, w_glu, tb=True, name="s5_glu_dx")
    dw_glu = _mm(yg, dtg, ta=True, out_dtype=BF16, name="s5_glu_dw")
    dy_core, du_skip, dd = _s5_act_bwd(dyg1, dyg2, y_core, proj, d_skip)
    outs = _ssm_core_bwd(_to_groups(proj[:, :1024], BF16), _to_groups(dy_core, BF16), ops)
    dproj = _s5_dproj(du_skip, _from_groups(outs[0]), dz)
    glr, gli, gdt, gbtr, gbti, gcr, gci = _ssm_operators_bwd(params, outs[1:])
    g = SSM_GROUPS
    grads = (glr.reshape(g, SSM_STATE), gli.reshape(g, SSM_STATE), gdt[:, 0, 0].reshape(1, g),
             gbtr.transpose(0, 2, 1), gbti.transpose(0, 2, 1), gcr, gci)
    return dproj, grads, dd, dw_glu, db_glu


def _perm_uq(w):
    w3 = w.reshape(w.shape[0], MLA_HEADS, MLA_NOPE + MLA_ROPE)
    return jnp.concatenate([w3[:, :, :64].reshape(-1, 1024), w3[:, :, 64:80].reshape(-1, 256),
                            w3[:, :, 80:96].reshape(-1, 256)], axis=1)


def _unperm_uq(w):
    r = w.shape[0]
    return jnp.concatenate([w[:, :1024].reshape(r, 16, 64), w[:, 1024:1280].reshape(r, 16, 16),
                            w[:, 1280:1536].reshape(r, 16, 16)], axis=2).reshape(r, 1536)


def _perm_ukv(w):
    w3 = w.reshape(w.shape[0], MLA_HEADS, 128)
    return jnp.concatenate([w3[:, :, :64].reshape(-1, 1024), w3[:, :, 64:].reshape(-1, 1024)], axis=1)


def _unperm_ukv(w):
    r = w.shape[0]
    return jnp.concatenate([w[:, :1024].reshape(r, 16, 64), w[:, 1024:].reshape(r, 16, 64)], axis=2).reshape(r, 2048)


def _mla_fwd(proj, gq, gkv, wuq, wukv, cos, sin):
    cqn, ckvn = _mla_norms(proj, gq, gkv)
    qf = _mm(cqn, wuq, name="mla_uq")
    kvf = _mm(ckvn, wukv, name="mla_ukv")
    qh, kh = _mla_pack(qf, kvf, proj, cos, sin)
    o, pout, lse = _mla_attn_fwd(qh, kh, kvf, proj)
    return pout, (cqn, ckvn, kvf, qh, kh, o, lse)


def _mla_bwd(proj, res, dpout, gq, gkv, wuq, wukv, cos, sin):
    cqn, ckvn, kvf, qh, kh, o, lse = res
    dqh, dkh, dv, dz = _mla_attn_bwd(qh, kh, kvf, proj, o, dpout, lse)
    dqf, dkvf, dkr = _mla_unpack(dqh, dkh, dv, cos, sin)
    dcqn = _mm(dqf, wuq, tb=True, name="mla_uq_dx")
    dwuq = _mm(cqn, dqf, ta=True, out_dtype=BF16, name="mla_uq_dw")
    dckvn = _mm(dkvf, wukv, tb=True, name="mla_ukv_dx")
    dwukv = _mm(ckvn, dkvf, ta=True, out_dtype=BF16, name="mla_ukv_dw")
    dproj, dgq, dgkv = _mla_norms_bwd(proj, dcqn, dckvn, dkr, dz, gq, gkv)
    return dproj, dgq, dgkv, dwuq, dwukv


_ANY = pl.BlockSpec(memory_space=pl.ANY)


def _chip_peers():
    x, y, c = lax.axis_index("x"), lax.axis_index("y"), lax.axis_index("c")
    return 2 * x + y, c, [(1 - x, y), (x, 1 - y), (1 - x, 1 - y)]


def _exchange_chips(src_of, dst_of, send_sems, recv_sems, loc_sem):
    me, c, peers = _chip_peers()
    loc = pltpu.make_async_copy(src_of(me), dst_of(me), loc_sem)
    loc.start()

    def copy(k, px, py, src_chip, dst_chip):
        return pltpu.make_async_remote_copy(src_ref=src_of(src_chip), dst_ref=dst_of(dst_chip), send_sem=send_sems.at[k],
                                            recv_sem=recv_sems.at[k], device_id=(px, py, c), device_id_type=MESH)

    sends = [copy(k, px, py, 2 * px + py, me) for k, (px, py) in enumerate(peers)]
    for cp in sends:
        cp.start()
    for k, (px, py) in enumerate(peers):
        copy(k, px, py, me, 2 * px + py).wait_recv()
    for cp in sends:
        cp.wait_send()
    loc.wait()


_EXCH_SCRATCH = [pltpu.SemaphoreType.DMA((3,)), pltpu.SemaphoreType.DMA((3,)), pltpu.SemaphoreType.DMA(())]


def _all_gather_chips(xs, name):
    def body(x_ref, o_ref, send_sems, recv_sems, loc_sem):
        _exchange_chips(lambda s: x_ref, lambda s: o_ref.at[s], send_sems, recv_sems, loc_sem)

    return pl.pallas_call(body, in_specs=[_ANY], out_specs=_ANY, out_shape=S((N_CHIPS,) + xs.shape, xs.dtype),
                          scratch_shapes=_EXCH_SCRATCH, name=name)(xs)


def _sibling_swap(ps, name):
    n = len(ps)

    def body(*refs):
        p_refs, q_refs, send_sems, recv_sems = refs[:n], refs[n:2 * n], refs[2 * n], refs[2 * n + 1]
        x, y, c = lax.axis_index("x"), lax.axis_index("y"), lax.axis_index("c")
        cps = [pltpu.make_async_remote_copy(src_ref=p_refs[i], dst_ref=q_refs[i], send_sem=send_sems.at[i],
                                            recv_sem=recv_sems.at[i], device_id=(x, y, 1 - c), device_id_type=MESH)
               for i in range(n)]
        for cp in cps:
            cp.start()
        for cp in cps:
            cp.wait()

    return pl.pallas_call(body, in_specs=[_ANY] * n, out_specs=[_ANY] * n, out_shape=[S(p.shape, p.dtype) for p in ps],
                          scratch_shapes=[pltpu.SemaphoreType.DMA((n,)), pltpu.SemaphoreType.DMA((n,))], name=name)(*ps)


def _sum_chips(l, name):
    _, r, c = l.shape
    tr = 256 if r % 256 == 0 else (128 if r % 128 == 0 else r)

    def body(l_ref, o_ref):
        o_ref[...] = ((l_ref[0].astype(F32) + l_ref[1].astype(F32)) + l_ref[2].astype(F32)) + l_ref[3].astype(F32)

    return pl.pallas_call(body, grid=(r // tr,), in_specs=[pl.BlockSpec((N_CHIPS, tr, c), lambda i: (0, i, 0))],
                          out_specs=_row_spec(tr, c), out_shape=S((r, c), F32), name=name,
                          compiler_params=_cp(("parallel",)))(l)


_HBM = pl.BlockSpec(memory_space=pltpu.HBM)
_SEM = pl.BlockSpec(memory_space=pltpu.SEMAPHORE)
_EFFECT = pltpu.SideEffectType.DATAFLOW_SIDE_EFFECTING


def _quarter(ref, s, dim):
    n = ref.shape[dim] // N_CHIPS
    start = pl.multiple_of(s * n, n)
    return ref.at[pl.ds(start, n), :] if dim == 0 else ref.at[:, pl.ds(start, n)]


def _gather_plan(dim):
    if dim is None:
        return (lambda ref, s: ref), (lambda ref, j: ref.at[j])
    return (lambda ref, s: ref), (lambda ref, j: _quarter(ref, j, dim))


def _scatter_plan(dim):
    if dim is None:
        return (lambda ref, s: ref.at[s]), (lambda ref, j: ref.at[j])
    return (lambda ref, s: _quarter(ref, s, dim)), (lambda ref, j: ref.at[j])


def _hbm(a):
    return pltpu.with_memory_space_constraint(a, pltpu.HBM)


def _place_own(srcs, land_shapes, plans, name):
    n = len(srcs)

    def body(*refs):
        src_refs, land_refs, sems = refs[:n], refs[n:2 * n], refs[2 * n]
        me, _, _ = _chip_peers()
        cps = [pltpu.make_async_copy(plans[i][0](src_refs[i], me), plans[i][1](land_refs[i], me), sems.at[i])
               for i in range(n)]
        for cp in cps:
            cp.start()
        for cp in cps:
            cp.wait()

    return pl.pallas_call(body, in_specs=[_ANY] * n, out_specs=[_ANY] * n, out_shape=list(land_shapes),
                          scratch_shapes=[pltpu.SemaphoreType.DMA((n,))], name=name)(*srcs)


def _remote_copies(src_refs, land_refs, send_sems, recv_sems, plans):
    me, c, peers = _chip_peers()
    sends, recvs = [], []
    for i, (src, dst) in enumerate(plans):
        for k, (px, py) in enumerate(peers):
            peer = 2 * px + py
            sems = dict(send_sem=send_sems.at[3 * i + k], recv_sem=recv_sems.at[3 * i + k], device_id=(px, py, c),
                        device_id_type=MESH)
            sends.append(pltpu.make_async_remote_copy(src_ref=src(src_refs[i], peer), dst_ref=dst(land_refs[i], me), **sems))
            recvs.append(pltpu.make_async_remote_copy(src_ref=src(src_refs[i], me), dst_ref=dst(land_refs[i], peer), **sems))
    return sends, recvs


def _exchange_start(srcs, lands, plans, name):
    n = len(srcs)

    def body(*refs):
        src_refs, land_refs = refs[:n], refs[n:2 * n]
        send_sems, recv_sems, token = refs[2 * n], refs[2 * n + 1], refs[-1]
        sends, _ = _remote_copies(src_refs, land_refs, send_sems, recv_sems, plans)
        for cp in sends:
            cp.start()
        token[...] = jnp.zeros_like(token)

    both = list(srcs) + list(lands)
    outs = pl.pallas_call(
        body, name=name,
        out_shape=(pltpu.SemaphoreType.DMA((3 * n,)), pltpu.SemaphoreType.DMA((3 * n,)),
                   *[pltpu.HBM(a.shape, a.dtype) for a in both], S((8, 128), F32)),
        in_specs=[_HBM] * (2 * n), out_specs=(_SEM, _SEM, *[_HBM] * (2 * n), pl.BlockSpec(memory_space=pltpu.VMEM)),
        input_output_aliases={i: 2 + i for i in range(2 * n)},
        compiler_params=pltpu.CompilerParams(has_side_effects=_EFFECT))(*[_hbm(a) for a in both])
    return outs[0], outs[1], outs[2:2 + n], outs[2 + n:2 + 2 * n], outs[-1]


def _exchange_wait(started, plans, after, name):
    send_sems, recv_sems, srcs, lands, _ = started
    n = len(srcs)

    def body(*refs):
        src_refs, land_refs = refs[:n], refs[n:2 * n]
        sends, recvs = _remote_copies(src_refs, land_refs, refs[2 * n], refs[2 * n + 1], plans)
        for cp in sends:
            cp.wait_send()
        for cp in recvs:
            cp.wait_recv()

    both = list(srcs) + list(lands)
    outs = pl.pallas_call(
        body, name=name, out_shape=tuple(pltpu.HBM(a.shape, a.dtype) for a in both),
        in_specs=[_HBM] * (2 * n) + [_SEM, _SEM, _ANY], out_specs=tuple([_HBM] * (2 * n)),
        input_output_aliases={i: i for i in range(2 * n)},
        compiler_params=pltpu.CompilerParams(has_side_effects=_EFFECT))(*both, send_sems, recv_sems, after)
    return outs[n:]


def _add2(p, q):
    r, c = p.shape
    tr = 128 if r % 128 == 0 else r

    def body(p_ref, q_ref, o_ref):
        o_ref[...] = p_ref[...] + q_ref[...]

    return pl.pallas_call(body, grid=(r // tr,), in_specs=[_row_spec(tr, c), _row_spec(tr, c)], out_specs=_row_spec(tr, c),
                          out_shape=S((r, c), F32), name="add_cores", compiler_params=_cp(("parallel",)))(p, q)


def _adamw(w, g, m, v, name, g2=None):
    r, c = w.shape
    tr = 256 if r % 256 == 0 else (128 if r % 128 == 0 else r)
    c1 = 1.0 / (1.0 - ADAM_B1 ** ADAM_STEP)
    c2 = 1.0 / (1.0 - ADAM_B2 ** ADAM_STEP)
    gs = [g] if g2 is None else [g, g2]

    def body(*refs):
        w_ref, m_ref, v_ref = refs[0], refs[1], refs[2]
        g_refs = refs[3:3 + len(gs)]
        go_ref, d_ref, nm_ref, nv_ref = refs[3 + len(gs):]
        gv = g_refs[0][...]
        if len(gs) == 2:
            gv = gv + g_refs[1][...]
        go_ref[...] = gv
        nm = ADAM_B1 * m_ref[...] + (1.0 - ADAM_B1) * gv
        nv = ADAM_B2 * v_ref[...] + (1.0 - ADAM_B2) * (gv * gv)
        nm_ref[...] = nm
        nv_ref[...] = nv
        d_ref[...] = -ADAM_LR * ((nm * c1) / (jnp.sqrt(nv * c2) + ADAM_EPS) + ADAM_WD * w_ref[...])

    return pl.pallas_call(body, grid=(r // tr,), in_specs=[_row_spec(tr, c)] * (3 + len(gs)),
                          out_specs=[_row_spec(tr, c)] * 4, out_shape=[S((r, c), F32)] * 4, name=name,
                          compiler_params=_cp(("parallel",)))(w, m, v, *gs)


_WEIGHTS = ['pre_norm', 'post_norm', 'rel_bias', 'a_w_in', 'a_lam_re', 'a_lam_im', 'a_log_dt', 'a_b_re', 'a_b_im',
            'a_c_re', 'a_c_im', 'a_d', 'a_w_glu', 'a_b_glu', 'a_w_out', 'b_w_in', 'b_sinks', 'b_w_out', 'c_w_in',
            'c_q_norm', 'c_kv_norm', 'c_w_uq', 'c_w_ukv', 'c_w_out', 'd_w_in', 'd_ln_g', 'd_ln_b', 'd_w_s', 'd_b_s',
            'd_w_out']
_BIG = [('a_w_in', 1024, 2048, 1), ('a_w_glu', 1024, 1024, 0), ('a_w_out', 1024, 1024, 0), ('b_w_in', 1024, 2304, 1),
        ('b_w_out', 1024, 1024, 0), ('c_w_in', 1024, 2080, 1), ('c_w_uq', 768, 1536, 1), ('c_w_ukv', 256, 2048, 1),
        ('c_w_out', 1024, 1024, 0), ('d_w_in', 1024, 3072, 1), ('d_w_out', 1024, 1024, 0)]
_SHARDED_VECS = [('c_q_norm', 768), ('c_kv_norm', 256), ('d_ln_g', 1024), ('d_ln_b', 1024)]
_REPLICATED = [n for n in _WEIGHTS if n not in [b[0] for b in _BIG] and n not in [s[0] for s in _SHARDED_VECS]]
_LANES = 1024
_SMALL_Q_ROWS = 136
_SMALL_PACK_ROWS = 640


def _pad_rows(flat, rows):
    return jnp.pad(flat, (0, rows * _LANES - flat.shape[0])).reshape(rows, _LANES)


def kernel(x, pre_norm, post_norm, rel_bias, a_w_in, a_lam_re, a_lam_im, a_log_dt, a_b_re, a_b_im, a_c_re, a_c_im, a_d, a_w_glu, a_b_glu, a_w_out, b_w_in, b_sinks, b_w_out, c_w_in, c_q_norm, c_kv_norm, c_w_uq, c_w_ukv, c_w_out, d_w_in, d_ln_g, d_ln_b, d_w_s, d_b_s, d_w_out, loss_target, m_pre_norm, m_post_norm, m_rel_bias, m_a_w_in, m_a_lam_re, m_a_lam_im, m_a_log_dt, m_a_b_re, m_a_b_im, m_a_c_re, m_a_c_im, m_a_d, m_a_w_glu, m_a_b_glu, m_a_w_out, m_b_w_in, m_b_sinks, m_b_w_out, m_c_w_in, m_c_q_norm, m_c_kv_norm, m_c_w_uq, m_c_w_ukv, m_c_w_out, m_d_w_in, m_d_ln_g, m_d_ln_b, m_d_w_s, m_d_b_s, m_d_w_out, v_pre_norm, v_post_norm, v_rel_bias, v_a_w_in, v_a_lam_re, v_a_lam_im, v_a_log_dt, v_a_b_re, v_a_b_im, v_a_c_re, v_a_c_im, v_a_d, v_a_w_glu, v_a_b_glu, v_a_w_out, v_b_w_in, v_b_sinks, v_b_w_out, v_c_w_in, v_c_q_norm, v_c_kv_norm, v_c_w_uq, v_c_w_ukv, v_c_w_out, v_d_w_in, v_d_ln_g, v_d_ln_b, v_d_w_s, v_d_b_s, v_d_w_out):
    arg = dict(locals())
    me = 2 * lax.axis_index("x") + lax.axis_index("y")
    xin = x[0]
    t = xin.shape[0]

    big = {n: (r, c, dim) for n, r, c, dim in _BIG}

    def gather_start(names, tag):
        srcs, shapes, plans = [], [], []
        for n in names:
            if n in big:
                r, c, dim = big[n]
                src = arg[n][0].astype(BF16)
                if dim == 1 and (c // N_CHIPS) % 128:
                    dim = None
                shapes.append(S((N_CHIPS,) + src.shape if dim is None else (r, c), BF16))
            else:
                src, dim = arg[n], None
                shapes.append(S((N_CHIPS,) + src.shape, F32))
            srcs.append(src)
            plans.append(_gather_plan(dim))
        lands = _place_own(srcs, shapes, plans, "gather_own_" + tag)
        return _exchange_start(srcs, lands, plans, "gather_start_" + tag), plans

    def gather_wait(names, started, after, tag):
        st, plans = started
        out = {}
        for n, a in zip(names, _exchange_wait(st, plans, after, "gather_wait_" + tag)):
            if n in big:
                r, c, dim = big[n]
                out[n] = a if a.ndim == 2 else a.transpose(1, 0, 2).reshape(r, c)
            else:
                out[n] = a.reshape(1, -1)
        return out

    groups = {'a1': ['a_w_in'], 'a2': ['a_w_glu', 'a_w_out'], 'b': ['b_w_in', 'b_w_out'],
              'c': ['c_w_in', 'c_w_uq', 'c_w_ukv', 'c_w_out', 'c_q_norm', 'c_kv_norm'],
              'd': ['d_w_in', 'd_w_out', 'd_ln_g', 'd_ln_b']}
    started = {tag: gather_start(names, tag) for tag, names in groups.items()}
    tokens = [st[0][4] for st in started.values()]
    wt = gather_wait(groups['a1'], started['a1'], xin, 'a1')
    cos, sin = _rope_tables(t)
    bucket = jnp.asarray(_t5_bucket_table())
    s5p = _s5_params(a_lam_re[0], a_lam_im[0], a_log_dt[0], a_b_re[0], a_b_im[0], a_c_re[0], a_c_im[0])
    bst = d_b_s[0].T

    x0 = xin
    h0 = _prenorm(x0, pre_norm[0:1], "pre0", after=tokens)
    proj0 = _mm(h0, wt['a_w_in'], name="a_in")
    wt.update(gather_wait(groups['a2'], started['a2'], proj0, 'a2'))
    p0, res0 = _s5_fwd(proj0, s5p, a_d, wt['a_w_glu'], a_b_glu)
    y0 = _mm(p0, wt['a_w_out'], name="a_out")
    x1 = _postnorm(x0, y0, post_norm[0:1], "post0")

    wt.update(gather_wait(groups['b'], started['b'], x1, 'b'))
    h1 = _prenorm(x1, pre_norm[1:2], "pre1")
    proj1 = _mm(h1, wt['b_w_in'], name="b_in")
    bias = _swa_bias(rel_bias, bucket)
    p1 = _swa_fwd(proj1, bias, b_sinks)
    y1 = _mm(p1, wt['b_w_out'], name="b_out")
    x2 = _postnorm(x1, y1, post_norm[1:2], "post1")

    wt.update(gather_wait(groups['c'], started['c'], x2, 'c'))
    w_c_in = jnp.concatenate([wt['c_w_in'][:, :1056], jnp.zeros((1024, 96), BF16), wt['c_w_in'][:, 1056:],
                              jnp.zeros((1024, MLA_PROJ - MLA_Z1), BF16)], axis=1)
    w_uq = _perm_uq(wt['c_w_uq'])
    w_ukv = _perm_ukv(wt['c_w_ukv'])
    h2 = _prenorm(x2, pre_norm[2:3], "pre2")
    proj2 = _mm(h2, w_c_in, name="c_in")
    p2, res2 = _mla_fwd(proj2, wt['c_q_norm'], wt['c_kv_norm'], w_uq, w_ukv, cos, sin)
    y2 = _mm(p2, wt['c_w_out'], name="c_out")
    x3 = _postnorm(x2, y2, post_norm[2:3], "post2")

    wt.update(gather_wait(groups['d'], started['d'], x3, 'd'))
    vec = wt
    h3 = _prenorm(x3, pre_norm[3:4], "pre3")
    proj3 = _mm(h3, wt['d_w_in'], name="d_in")
    p3 = _sgu_fwd(proj3, vec['d_ln_g'], vec['d_ln_b'], d_w_s[0], bst)
    y3 = _mm(p3, wt['d_w_out'], name="d_out")
    x4 = _postnorm(x3, y3, post_norm[3:4], "post3")

    loss_part, dx = _loss_head(x4, loss_target[0])
    loss = lax.psum(loss_part[0, 0], ("x", "y", "c"))

    gr = {}
    d_pre, d_post = [None] * 4, [None] * 4
    scattered = []

    def scatter_start(items, tag):
        srcs = [a for _, a, _ in items]
        plans = [_scatter_plan(dim) for _, _, dim in items]
        shapes = []
        for _, a, dim in items:
            part = a.shape[1:] if dim is None else tuple(s // N_CHIPS if i == dim else s for i, s in enumerate(a.shape))
            shapes.append(S((N_CHIPS,) + part, a.dtype))
        lands = _place_own(srcs, shapes, plans, "scatter_own_" + tag)
        st = _exchange_start(srcs, lands, plans, "scatter_start_" + tag)
        scattered.append((st, plans, [n for n, _, _ in items], tag))
        return [st[4]]

    def stacked(g):
        r, c = g.shape
        return g.reshape(r, N_CHIPS, c // N_CHIPS).transpose(1, 0, 2)

    dy, d_post[3] = _postnorm_bwd(dx, y3, post_norm[3:4], "post3_bwd")
    dp = _mm(dy, wt['d_w_out'], tb=True, name="d_out_dx")
    g_out = _mm(p3, dy, ta=True, out_dtype=BF16, name="d_out_dw")
    dproj, gr['d_w_s'], dbst, gr['d_ln_g'], gr['d_ln_b'] = _sgu_bwd(proj3, dp, vec['d_ln_g'], vec['d_ln_b'], d_w_s[0], bst)
    gr['d_b_s'] = dbst.T
    dh = _mm(dproj, wt['d_w_in'], tb=True, name="d_in_dx")
    g_in = _mm(h3, dproj, ta=True, out_dtype=BF16, name="d_in_dw")
    tok = scatter_start([('d_w_in', g_in, 1), ('d_w_out', g_out, 0)], 'd')
    dx, d_pre[3] = _prenorm_bwd(dx, dh, x3, pre_norm[3:4], "pre3_bwd")

    dy, d_post[2] = _postnorm_bwd(dx, y2, post_norm[2:3], "post2_bwd", after=tok)
    dp = _mm(dy, wt['c_w_out'], tb=True, name="c_out_dx")
    g_out = _mm(p2, dy, ta=True, out_dtype=BF16, name="c_out_dw")
    dproj, gr['c_q_norm'], gr['c_kv_norm'], dwuq, dwukv = _mla_bwd(proj2, res2, dp, wt['c_q_norm'], wt['c_kv_norm'],
                                                                  w_uq, w_ukv, cos, sin)
    dh = _mm(dproj, w_c_in, tb=True, name="c_in_dx")
    dwc = _mm(h2, dproj, ta=True, out_dtype=BF16, name="c_in_dw")
    g_in = stacked(jnp.concatenate([dwc[:, :1056], dwc[:, MLA_Z0:MLA_Z1]], axis=1))
    tok = scatter_start([('c_w_in', g_in, None), ('c_w_uq', _unperm_uq(dwuq), 1), ('c_w_ukv', _unperm_ukv(dwukv), 1),
                         ('c_w_out', g_out, 0)], 'c')
    dx, d_pre[2] = _prenorm_bwd(dx, dh, x2, pre_norm[2:3], "pre2_bwd")

    dy, d_post[1] = _postnorm_bwd(dx, y1, post_norm[1:2], "post1_bwd", after=tok)
    dp = _mm(dy, wt['b_w_out'], tb=True, name="b_out_dx")
    g_out = _mm(p1, dy, ta=True, out_dtype=BF16, name="b_out_dw")
    dqz, dkv, dss, dsk = _swa_bwd(proj1, dp, bias, b_sinks)
    gr['rel_bias'], gr['b_sinks'] = _swa_dbias(dss, dsk, bucket)
    dproj = jnp.concatenate([dqz[:, :1024], dkv.astype(BF16), dqz[:, 1024:]], axis=1)
    dh = _mm(dproj, wt['b_w_in'], tb=True, name="b_in_dx")
    g_in = stacked(_mm(h1, dproj, ta=True, out_dtype=BF16, name="b_in_dw"))
    tok = scatter_start([('b_w_in', g_in, None), ('b_w_out', g_out, 0)], 'b')
    dx, d_pre[1] = _prenorm_bwd(dx, dh, x1, pre_norm[1:2], "pre1_bwd")

    dy, d_post[0] = _postnorm_bwd(dx, y0, post_norm[0:1], "post0_bwd", after=tok)
    dp = _mm(dy, wt['a_w_out'], tb=True, name="a_out_dx")
    g_out = _mm(p0, dy, ta=True, out_dtype=BF16, name="a_out_dw")
    dproj, s5g, gr['a_d'], g_glu, gr['a_b_glu'] = _s5_bwd(proj0, res0, dp, s5p, a_d, wt['a_w_glu'], a_b_glu)
    for n, g in zip(['a_lam_re', 'a_lam_im', 'a_log_dt', 'a_b_re', 'a_b_im', 'a_c_re', 'a_c_im'], s5g):
        gr[n] = g
    dh = _mm(dproj, wt['a_w_in'], tb=True, name="a_in_dx")
    g_in = _mm(h0, dproj, ta=True, out_dtype=BF16, name="a_in_dw")
    dx, d_pre[0] = _prenorm_bwd(dx, dh, x0, pre_norm[0:1], "pre0_bwd")
    gr['pre_norm'] = jnp.concatenate(d_pre, axis=0)
    gr['post_norm'] = jnp.concatenate(d_post, axis=0)
    small_names = _REPLICATED + [n for n, _ in _SHARDED_VECS]
    small = jnp.concatenate([gr[n].reshape(-1) for n in small_names])
    small = jnp.pad(small, (0, N_CHIPS * _SMALL_Q_ROWS * _LANES - small.shape[0])).reshape(N_CHIPS, _SMALL_Q_ROWS, _LANES)
    scatter_start([('a_w_in', g_in, 1), ('a_w_glu', g_glu, 0), ('a_w_out', g_out, 0), ('small', small, None)], 'a')

    partial = {}
    for st, plans, names, tag in scattered:
        for n, land in zip(names, _exchange_wait(st, plans, dx, "scatter_wait_" + tag)):
            land = land.reshape((N_CHIPS, -1, land.shape[-1]))
            partial[n] = _sum_chips(land, "sum_chips_" + n)
    order = [n for n, _, _, _ in _BIG] + ['small']
    other = dict(zip(order, _sibling_swap([partial[n] for n in order], "swap_cores")))
    small_q = _add2(partial['small'], other['small'])
    small_all = _all_gather_chips(small_q, "gather_small_grads").reshape(-1)
    grads = {}
    off = 0
    for n in _REPLICATED:
        sz = math.prod(arg[n].shape)
        grads[n] = small_all[off:off + sz].reshape(arg[n].shape)
        off += sz
    for n, sz in _SHARDED_VECS:
        q = sz // N_CHIPS
        grads[n] = lax.dynamic_slice(small_all, (off + me * q,), (q,)).reshape(arg[n].shape)
        off += sz

    delta, new_m, new_v = {}, {}, {}
    for n, _, _, _ in _BIG:
        shp = arg[n].shape
        two = (shp[1], shp[2])
        outs = _adamw(arg[n].reshape(two), partial[n], arg['m_' + n].reshape(two), arg['v_' + n].reshape(two),
                      "adamw_" + n, g2=other[n])
        grads[n], delta[n], new_m[n], new_v[n] = [o.reshape(shp) for o in outs]

    def pack(prefix, src):
        return _pad_rows(jnp.concatenate([src[prefix + n].reshape(-1) for n in small_names]), _SMALL_PACK_ROWS)

    outs = _adamw(pack('', arg), pack('', grads), pack('m_', arg), pack('v_', arg), "adamw_small")
    outs = [o.reshape(-1) for o in outs[1:]]
    off = 0
    for n in small_names:
        shp = arg[n].shape
        sz = math.prod(shp)
        delta[n], new_m[n], new_v[n] = [o[off:off + sz].reshape(shp) for o in outs]
        off += sz

    return (loss, dx.reshape(x.shape), *[grads[n] for n in _WEIGHTS], *[delta[n] for n in _WEIGHTS],
            *[new_m[n] for n in _WEIGHTS], *[new_v[n] for n in _WEIGHTS])
```

```python
import functools
import math

import numpy as np
import jax
import jax.numpy as jnp
from jax import lax
from jax.experimental import pallas as pl
from jax.experimental.pallas import tpu as pltpu

F32 = jnp.float32
BF16 = jnp.bfloat16
S = jax.ShapeDtypeStruct
MESH = pl.DeviceIdType.MESH

D_MODEL = 1024
EPS = 1e-6
NEG_INF = -1e30
N_CHIPS = 4

SSM_GROUPS = 64
SSM_GROUP = 16
SSM_STATE = 64
SSM_CHUNK = 16
SSM_BLOCK = SSM_CHUNK * SSM_GROUP

HEAD_DIM = 64
SWA_HEADS = 16
SWA_GROUP = 8
WINDOW = 128
REL_BUCKETS = 32
REL_MAX_DIST = 128

MLA_HEADS = 16
MLA_NOPE = 64
MLA_ROPE = 32
MLA_Q_RANK = 768
MLA_KV_RANK = 256
MLA_SCALE = (MLA_NOPE + MLA_ROPE) ** -0.5
MLA_BLK = 256
ROPE_BASE = 10000.0

SGU_CHUNK = 128
SGU_GROUPS = 16
SGU_GDIM = 64

ADAM_LR = 0.001
ADAM_B1 = 0.9
ADAM_B2 = 0.999
ADAM_EPS = 1e-08
ADAM_WD = 0.01
ADAM_STEP = 10

VMEM_BIG = 56 * 1024 * 1024


def _cp(sem, vmem=None):
    return pltpu.CompilerParams(dimension_semantics=sem, vmem_limit_bytes=vmem)


def _row_spec(tr, c):
    return pl.BlockSpec((tr, c), lambda i: (i, 0))


def _full_spec(shape):
    nd = len(shape)
    return pl.BlockSpec(shape, lambda *_: (0,) * nd)


def _gelu(x):
    c = 0.7978845608028654
    return 0.5 * x * (1.0 + jnp.tanh(c * (x + 0.044715 * x * x * x)))


def _gelu_grad(x):
    c = 0.7978845608028654
    t = jnp.tanh(c * (x + 0.044715 * x * x * x))
    return 0.5 * (1.0 + t) + 0.5 * x * (1.0 - t * t) * c * (1.0 + 3.0 * 0.044715 * x * x)


def _silu(z):
    return z * jax.nn.sigmoid(z)


def _silu_grad(z):
    s = jax.nn.sigmoid(z)
    return s * (1.0 + z * (1.0 - s))


def _dot(a, b):
    return lax.dot_general(a, b, (((1,), (0,)), ((), ())), preferred_element_type=F32)


def _dot_nt(a, b):
    return lax.dot_general(a, b, (((1,), (1,)), ((), ())), preferred_element_type=F32)


def _dot_tn(a, b):
    return lax.dot_general(a, b, (((0,), (0,)), ((), ())), preferred_element_type=F32)


def _pick(n, cap):
    best = 0
    for d in range(128, min(n, cap) + 1, 128):
        if n % d == 0:
            best = d
    if best < 256:
        return n
    return best


def _mm(a, b, *, ta=False, tb=False, out_dtype=F32, name):
    m, k = (a.shape[1], a.shape[0]) if ta else a.shape
    n = b.shape[0] if tb else b.shape[1]
    assert (b.shape[1] if tb else b.shape[0]) == k
    tm, tn, tk = _pick(m, 1024), _pick(n, 1024), _pick(k, 1024)
    if tn > 1024:
        tm = _pick(m, 256)
    if tk > 1024:
        tm, tn = _pick(m, 256), _pick(n, 256)
    nk = k // tk
    dims = (((0 if ta else 1,), (1 if tb else 0,)), ((), ()))

    def body(a_ref, b_ref, o_ref, acc_ref):
        kk = pl.program_id(2)

        @pl.when(kk == 0)
        def _():
            acc_ref[...] = jnp.zeros_like(acc_ref)

        acc_ref[...] += lax.dot_general(a_ref[...].astype(BF16), b_ref[...].astype(BF16), dims,
                                        preferred_element_type=F32)

        @pl.when(kk == nk - 1)
        def _():
            o_ref[...] = acc_ref[...].astype(out_dtype)

    a_spec = pl.BlockSpec((tk, tm), lambda i, j, kk: (kk, i)) if ta else pl.BlockSpec((tm, tk), lambda i, j, kk: (i, kk))
    b_spec = pl.BlockSpec((tn, tk), lambda i, j, kk: (j, kk)) if tb else pl.BlockSpec((tk, tn), lambda i, j, kk: (kk, j))
    return pl.pallas_call(
        body, grid=(m // tm, n // tn, nk), in_specs=[a_spec, b_spec],
        out_specs=pl.BlockSpec((tm, tn), lambda i, j, kk: (i, j)), out_shape=S((m, n), out_dtype),
        scratch_shapes=[pltpu.VMEM((tm, tn), F32)], name=name,
        compiler_params=_cp(("parallel", "parallel", "arbitrary"), VMEM_BIG))(a, b)


def _prenorm(x, g, name, after=()):
    t, d = x.shape
    tr = 256

    def body(x_ref, g_ref, *rest):
        o_ref = rest[-1]
        xv = x_ref[...]
        r = lax.rsqrt(jnp.mean(xv * xv, axis=-1, keepdims=True) + EPS)
        o_ref[...] = (xv * r * g_ref[...]).astype(BF16)

    return pl.pallas_call(body, grid=(t // tr,),
                          in_specs=[_row_spec(tr, d), _full_spec((1, d))] + [pl.BlockSpec(memory_space=pl.ANY)] * len(after),
                          out_specs=_row_spec(tr, d), out_shape=S((t, d), BF16), name=name,
                          compiler_params=_cp(("parallel",)))(x, g, *after)


def _postnorm(x, y, g, name):
    t, d = x.shape
    tr = 256

    def body(x_ref, y_ref, g_ref, o_ref):
        yv = y_ref[...]
        r = lax.rsqrt(jnp.mean(yv * yv, axis=-1, keepdims=True) + EPS)
        o_ref[...] = x_ref[...] + yv * r * g_ref[...]

    return pl.pallas_call(body, grid=(t // tr,), in_specs=[_row_spec(tr, d), _row_spec(tr, d), _full_spec((1, d))],
                          out_specs=_row_spec(tr, d), out_shape=S((t, d), F32), name=name,
                          compiler_params=_cp(("parallel",)))(x, y, g)


def _rms_bwd_rows(dout, xin, g):
    r = lax.rsqrt(jnp.mean(xin * xin, axis=-1, keepdims=True) + EPS)
    xh = xin * r
    gd = dout * g
    dx = r * (gd - xh * jnp.mean(gd * xh, axis=-1, keepdims=True))
    return dx, jnp.sum(dout * xh, axis=0, keepdims=True)


def _postnorm_bwd(dxn, y, g, name, after=()):
    t, d = y.shape
    tr = 256

    def body(dx_ref, y_ref, g_ref, *rest):
        dy_ref, dg_ref = rest[-2], rest[-1]

        @pl.when(pl.program_id(0) == 0)
        def _():
            dg_ref[...] = jnp.zeros_like(dg_ref)

        dy, dg = _rms_bwd_rows(dx_ref[...], y_ref[...], g_ref[...])
        dy_ref[...] = dy.astype(BF16)
        dg_ref[...] += dg

    return pl.pallas_call(body, grid=(t // tr,),
                          in_specs=[_row_spec(tr, d), _row_spec(tr, d), _full_spec((1, d))]
                          + [pl.BlockSpec(memory_space=pl.ANY)] * len(after),
                          out_specs=[_row_spec(tr, d), _full_spec((1, d))],
                          out_shape=[S((t, d), BF16), S((1, d), F32)], name=name,
                          compiler_params=_cp(("arbitrary",)))(dxn, y, g, *after)


def _prenorm_bwd(dxn, dh, x, g, name):
    t, d = x.shape
    tr = 256

    def body(dxn_ref, dh_ref, x_ref, g_ref, dx_ref, dg_ref):
        @pl.when(pl.program_id(0) == 0)
        def _():
            dg_ref[...] = jnp.zeros_like(dg_ref)

        dx, dg = _rms_bwd_rows(dh_ref[...], x_ref[...], g_ref[...])
        dx_ref[...] = dxn_ref[...] + dx
        dg_ref[...] += dg

    return pl.pallas_call(body, grid=(t // tr,),
                          in_specs=[_row_spec(tr, d), _row_spec(tr, d), _row_spec(tr, d), _full_spec((1, d))],
                          out_specs=[_row_spec(tr, d), _full_spec((1, d))],
                          out_shape=[S((t, d), F32), S((1, d), F32)], name=name,
                          compiler_params=_cp(("arbitrary",)))(dxn, dh, x, g)


def _loss_head(y, target):
    t, d = y.shape
    tr = 256

    def body(y_ref, t_ref, l_ref, dy_ref):
        @pl.when(pl.program_id(0) == 0)
        def _():
            l_ref[...] = jnp.zeros_like(l_ref)

        e = y_ref[...] - t_ref[...]
        dy_ref[...] = e * (1.0 / d)
        l_ref[...] += 0.5 * jnp.sum(jnp.mean(e * e, axis=-1, keepdims=True), axis=0, keepdims=True)

    return pl.pallas_call(body, grid=(t // tr,), in_specs=[_row_spec(tr, d), _row_spec(tr, d)],
                          out_specs=[_full_spec((1, 1)), _row_spec(tr, d)],
                          out_shape=[S((1, 1), F32), S((t, d), F32)], name="loss_head",
                          compiler_params=_cp(("arbitrary",)))(y, target)


def _sgu_parts(proj, lg, lb, ws_ref, bst):
    a = proj[:, 0:1024]
    bb = proj[:, 1024:2048]
    z = proj[:, 2048:3072]
    u = _gelu(a)
    vp = _gelu(bb)
    mu = jnp.mean(vp, axis=-1, keepdims=True)
    xc = vp - mu
    rstd = lax.rsqrt(jnp.mean(xc * xc, axis=-1, keepdims=True) + EPS)
    xh = xc * rstd
    vn = xh * lg + lb
    row = lax.broadcasted_iota(jnp.int32, (SGU_CHUNK, SGU_CHUNK), 0)
    col = lax.broadcasted_iota(jnp.int32, (SGU_CHUNK, SGU_CHUNK), 1)
    tril = row >= col
    ws = [jnp.where(tril, ws_ref[g], 0.0).astype(BF16) for g in range(SGU_GROUPS)]
    s = jnp.concatenate(
        [_dot(ws[g], vn[:, g * 64:(g + 1) * 64].astype(BF16)) + bst[:, g:g + 1] for g in range(SGU_GROUPS)], axis=1)
    return a, bb, z, u, vp, rstd, xh, vn, ws, tril, s


def _sgu_fwd(proj, lg, lb, w_s, bst):
    t = proj.shape[0]
    tr = SGU_CHUNK

    def body(p_ref, lg_ref, lb_ref, ws_ref, bst_ref, o_ref):
        _, _, z, u, _, _, _, _, _, _, s = _sgu_parts(p_ref[...], lg_ref[...], lb_ref[...], ws_ref, bst_ref[...])
        o_ref[...] = (u * s * _silu(z)).astype(BF16)

    return pl.pallas_call(
        body, grid=(t // tr,),
        in_specs=[_row_spec(tr, 3072), _full_spec((1, 1024)), _full_spec((1, 1024)), _full_spec((16, 128, 128)),
                  _full_spec((128, 16))],
        out_specs=_row_spec(tr, 1024), out_shape=S((t, 1024), BF16), name="sgu_fwd",
        compiler_params=_cp(("parallel",)))(proj, lg, lb, w_s, bst)


def _sgu_bwd(proj, dp, lg, lb, w_s, bst):
    t = proj.shape[0]
    tr = SGU_CHUNK

    def body(p_ref, dp_ref, lg_ref, lb_ref, ws_ref, bst_ref, dproj_ref, dws_ref, dbst_ref, dlg_ref, dlb_ref):
        @pl.when(pl.program_id(0) == 0)
        def _():
            dws_ref[...] = jnp.zeros_like(dws_ref)
            dbst_ref[...] = jnp.zeros_like(dbst_ref)
            dlg_ref[...] = jnp.zeros_like(dlg_ref)
            dlb_ref[...] = jnp.zeros_like(dlb_ref)

        lgv = lg_ref[...]
        a, bb, z, u, vp, rstd, xh, vn, ws, tril, s = _sgu_parts(p_ref[...], lgv, lb_ref[...], ws_ref, bst_ref[...])
        dpv = dp_ref[...].astype(F32)
        sz = _silu(z)
        du = dpv * s * sz
        ds = dpv * u * sz
        dz = dpv * u * s * _silu_grad(z)
        dvn_parts, dbs_cols = [], []
        for g in range(SGU_GROUPS):
            dsg = ds[:, g * 64:(g + 1) * 64]
            dsg_b = dsg.astype(BF16)
            dvn_parts.append(_dot_tn(ws[g], dsg_b))
            dws_ref[g] += jnp.where(tril, _dot_nt(dsg_b, vn[:, g * 64:(g + 1) * 64].astype(BF16)), 0.0)
            dbs_cols.append(jnp.sum(dsg, axis=1, keepdims=True))
        dvn = jnp.concatenate(dvn_parts, axis=1)
        dbst_ref[...] += jnp.concatenate(dbs_cols, axis=1)
        dlg_ref[...] += jnp.sum(dvn * xh, axis=0, keepdims=True)
        dlb_ref[...] += jnp.sum(dvn, axis=0, keepdims=True)
        dxh = dvn * lgv
        dvp = rstd * (dxh - jnp.mean(dxh, axis=-1, keepdims=True) - xh * jnp.mean(dxh * xh, axis=-1, keepdims=True))
        dproj_ref[:, 0:1024] = (du * _gelu_grad(a)).astype(BF16)
        dproj_ref[:, 1024:2048] = (dvp * _gelu_grad(bb)).astype(BF16)
        dproj_ref[:, 2048:3072] = dz.astype(BF16)

    return pl.pallas_call(
        body, grid=(t // tr,),
        in_specs=[_row_spec(tr, 3072), _row_spec(tr, 1024), _full_spec((1, 1024)), _full_spec((1, 1024)),
                  _full_spec((16, 128, 128)), _full_spec((128, 16))],
        out_specs=[_row_spec(tr, 3072), _full_spec((16, 128, 128)), _full_spec((128, 16)), _full_spec((1, 1024)),
                   _full_spec((1, 1024))],
        out_shape=[S((t, 3072), BF16), S((16, 128, 128), F32), S((128, 16), F32), S((1, 1024), F32),
                   S((1, 1024), F32)],
        name="sgu_bwd", compiler_params=_cp(("arbitrary",)))(proj, dp, lg, lb, w_s, bst)


def _t5_bucket_table():
    qi = np.arange(WINDOW)[:, None]
    kj = np.arange(2 * WINDOW)[None, :]
    dist = np.maximum(qi + WINDOW - kj, 0)
    max_exact = REL_BUCKETS // 2
    dist_f = np.maximum(dist, 1).astype(np.float32)
    large = max_exact + (np.log(dist_f / np.float32(max_exact)) / np.float32(math.log(REL_MAX_DIST / max_exact))
                         * np.float32(REL_BUCKETS - max_exact)).astype(np.int32)
    large = np.minimum(large, REL_BUCKETS - 1)
    return np.where(dist < max_exact, dist, large).astype(np.int32)


def _swa_bias(rel_bias, bucket):
    def body(rb_ref, bk_ref, o_ref):
        bk = bk_ref[...]
        for h in range(SWA_HEADS):
            acc = jnp.zeros((WINDOW, 2 * WINDOW), F32)
            for b in range(REL_BUCKETS):
                acc = jnp.where(bk == b, rb_ref[b, h], acc)
            o_ref[h] = acc

    return pl.pallas_call(
        body, in_specs=[pl.BlockSpec(memory_space=pltpu.SMEM), pl.BlockSpec(memory_space=pltpu.VMEM)],
        out_specs=pl.BlockSpec(memory_space=pltpu.VMEM), out_shape=S((SWA_HEADS, WINDOW, 2 * WINDOW), F32),
        name="swa_bias")(rel_bias, bucket)


def _swa_dbias(ds_sum, dsink_rows, bucket):
    def body(ds_ref, dsk_ref, bk_ref, drb_ref, dsink_ref):
        bk = bk_ref[...]
        r = lax.broadcasted_iota(jnp.int32, (REL_BUCKETS, SWA_HEADS), 0)
        c = lax.broadcasted_iota(jnp.int32, (REL_BUCKETS, SWA_HEADS), 1)
        out = jnp.zeros((REL_BUCKETS, SWA_HEADS), F32)
        for b in range(REL_BUCKETS):
            m = bk == b
            for h in range(SWA_HEADS):
                val = jnp.sum(jnp.where(m, ds_ref[h], 0.0))
                out = jnp.where((r == b) & (c == h), val, out)
        drb_ref[...] = out
        dsink_ref[...] = jnp.sum(dsk_ref[...], axis=0, keepdims=True)

    return pl.pallas_call(
        body, out_shape=[S((REL_BUCKETS, SWA_HEADS), F32), S((1, SWA_HEADS), F32)], name="swa_dbias",
    )(ds_sum, dsink_rows, bucket)


def _swa_band(cur_ref, prev_ref):
    kband = jnp.concatenate([prev_ref[:, 0:128], cur_ref[:, 1024:1152]], axis=0).astype(BF16)
    vband = jnp.concatenate([prev_ref[:, 128:256], cur_ref[:, 1152:1280]], axis=0).astype(BF16)
    return kband, vband


def _swa_valid(n):
    qi = lax.broadcasted_iota(jnp.int32, (WINDOW, 2 * WINDOW), 0)
    kj = lax.broadcasted_iota(jnp.int32, (WINDOW, 2 * WINDOW), 1)
    dist = qi + WINDOW - kj
    return (dist >= 0) & (dist < WINDOW) & (n * WINDOW + kj - WINDOW >= 0)


def _swa_probs(qh, kh, bias_h, sink, valid):
    s = _dot_nt(qh, kh) * (HEAD_DIM ** -0.5) + bias_h
    s = jnp.where(valid, s, NEG_INF)
    m = jnp.maximum(jnp.max(s, axis=-1, keepdims=True), sink)
    e = jnp.exp(s - m)
    es = jnp.exp(sink - m)
    inv = 1.0 / (jnp.sum(e, axis=-1, keepdims=True) + es)
    return e * inv, es * inv


def _swa_in_specs():
    return [_row_spec(WINDOW, 2304),
            pl.BlockSpec((WINDOW, 256), lambda n: (jnp.maximum(n - 1, 0), 4)),
            _full_spec((SWA_HEADS, WINDOW, 2 * WINDOW)),
            pl.BlockSpec(memory_space=pltpu.SMEM)]


def _swa_fwd(proj, bias, sinks):
    t = proj.shape[0]

    def body(cur_ref, prev_ref, bias_ref, sink_ref, o_ref):
        n = pl.program_id(0)
        kband, vband = _swa_band(cur_ref, prev_ref)
        valid = _swa_valid(n)
        outs = []
        for h in range(SWA_HEADS):
            j = h // SWA_GROUP
            qh = cur_ref[:, h * 64:(h + 1) * 64].astype(BF16)
            p, _ = _swa_probs(qh, kband[:, j * 64:(j + 1) * 64], bias_ref[h], sink_ref[0, h], valid)
            outs.append(_dot(p.astype(BF16), vband[:, j * 64:(j + 1) * 64]))
        o = jnp.concatenate(outs, axis=1)
        o_ref[...] = (o * _silu(cur_ref[:, 1280:2304])).astype(BF16)

    return pl.pallas_call(body, grid=(t // WINDOW,), in_specs=_swa_in_specs(), out_specs=_row_spec(WINDOW, 1024),
                          out_shape=S((t, 1024), BF16), name="swa_fwd",
                          compiler_params=_cp(("parallel",)))(proj, proj, bias, sinks)


def _swa_bwd(proj, dp, bias, sinks):
    t = proj.shape[0]
    nb = t // WINDOW

    def body(cur_ref, prev_ref, bias_ref, sink_ref, dp_ref, dqz_ref, dkv_ref, dss_ref, dsk_ref):
        n = pl.program_id(0)

        @pl.when(n == 0)
        def _():
            dkv_ref[...] = jnp.zeros_like(dkv_ref)
            dss_ref[...] = jnp.zeros_like(dss_ref)
            dsk_ref[...] = jnp.zeros_like(dsk_ref)

        kband, vband = _swa_band(cur_ref, prev_ref)
        valid = _swa_valid(n)
        z = cur_ref[:, 1280:2304]
        dpv = dp_ref[...].astype(F32)
        do = dpv * _silu(z)
        outs, dqs, dsinks = [], [], []
        dk = [jnp.zeros((2 * WINDOW, 64), F32) for _ in range(2)]
        dv = [jnp.zeros((2 * WINDOW, 64), F32) for _ in range(2)]
        for h in range(SWA_HEADS):
            j = h // SWA_GROUP
            qh = cur_ref[:, h * 64:(h + 1) * 64].astype(BF16)
            kh = kband[:, j * 64:(j + 1) * 64]
            vh = vband[:, j * 64:(j + 1) * 64]
            p, ps = _swa_probs(qh, kh, bias_ref[h], sink_ref[0, h], valid)
            pb = p.astype(BF16)
            outs.append(_dot(pb, vh))
            doh = do[:, h * 64:(h + 1) * 64].astype(BF16)
            dpr = _dot_nt(doh, vh)
            dlt = jnp.sum(p * dpr, axis=-1, keepdims=True)
            ds = p * (dpr - dlt)
            dsinks.append(-ps * dlt)
            dss_ref[h] += ds
            dsb = (ds * (HEAD_DIM ** -0.5)).astype(BF16)
            dqs.append(_dot(dsb, kh))
            dk[j] = dk[j] + _dot_tn(dsb, qh)
            dv[j] = dv[j] + _dot_tn(pb, doh)
        o = jnp.concatenate(outs, axis=1)
        dqz_ref[:, 0:1024] = jnp.concatenate(dqs, axis=1).astype(BF16)
        dqz_ref[:, 1024:2048] = (dpv * o * _silu_grad(z)).astype(BF16)
        dsk_ref[...] += jnp.concatenate(dsinks, axis=1)
        dband = jnp.concatenate([dk[0], dk[1], dv[0], dv[1]], axis=1)
        prow = pl.multiple_of(jnp.maximum(n - 1, 0) * WINDOW, WINDOW)
        dkv_ref[pl.ds(prow, WINDOW), :] += dband[0:WINDOW]
        crow = pl.multiple_of(n * WINDOW, WINDOW)
        dkv_ref[pl.ds(crow, WINDOW), :] += dband[WINDOW:]

    return pl.pallas_call(
        body, grid=(nb,), in_specs=_swa_in_specs() + [_row_spec(WINDOW, 1024)],
        out_specs=[_row_spec(WINDOW, 2048), _full_spec((t, 256)), _full_spec((SWA_HEADS, WINDOW, 2 * WINDOW)),
                   _full_spec((WINDOW, SWA_HEADS))],
        out_shape=[S((t, 2048), BF16), S((t, 256), F32), S((SWA_HEADS, WINDOW, 2 * WINDOW), F32),
                   S((WINDOW, SWA_HEADS), F32)],
        name="swa_bwd", compiler_params=_cp(("arbitrary",)))(proj, proj, bias, sinks, dp)


MLA_Z0 = MLA_Q_RANK + MLA_KV_RANK + 128
MLA_Z1 = MLA_Z0 + 1024
MLA_PROJ = MLA_Z1 + 128


def _rope_tables(t):
    inv = ROPE_BASE ** (-jnp.arange(0, MLA_ROPE, 2, dtype=F32) / MLA_ROPE)
    ang = jnp.arange(t, dtype=F32)[:, None] * inv[None, :]
    return jnp.tile(jnp.cos(ang), (1, MLA_HEADS)), jnp.tile(jnp.sin(ang), (1, MLA_HEADS))


def _mla_norms(proj, gq, gkv):
    t = proj.shape[0]
    tr = 256

    def body(p_ref, gq_ref, gkv_ref, q_ref, kv_ref):
        for lo, hi, g_ref, o_ref in ((0, 768, gq_ref, q_ref), (768, 1024, gkv_ref, kv_ref)):
            xv = p_ref[:, lo:hi]
            r = lax.rsqrt(jnp.mean(xv * xv, axis=-1, keepdims=True) + EPS)
            o_ref[...] = (xv * r * g_ref[...]).astype(BF16)

    return pl.pallas_call(
        body, grid=(t // tr,), in_specs=[_row_spec(tr, 1024), _full_spec((1, 768)), _full_spec((1, 256))],
        out_specs=[_row_spec(tr, 768), _row_spec(tr, 256)], out_shape=[S((t, 768), BF16), S((t, 256), BF16)],
        name="mla_norms", compiler_params=_cp(("parallel",)))(proj, gq, gkv)


def _mla_norms_bwd(proj, dcqn, dckvn, dkr, dz, gq, gkv):
    t = proj.shape[0]
    tr = 256

    def body(p_ref, dq_ref, dkv_ref, dkr_ref, dz_ref, gq_ref, gkv_ref, dproj_ref, dgq_ref, dgkv_ref):
        @pl.when(pl.program_id(0) == 0)
        def _():
            dgq_ref[...] = jnp.zeros_like(dgq_ref)
            dgkv_ref[...] = jnp.zeros_like(dgkv_ref)

        dx, dg = _rms_bwd_rows(dq_ref[...], p_ref[:, 0:768], gq_ref[...])
        dproj_ref[:, 0:768] = dx.astype(BF16)
        dgq_ref[...] += dg
        dx, dg = _rms_bwd_rows(dkv_ref[...], p_ref[:, 768:1024], gkv_ref[...])
        dproj_ref[:, 768:1024] = dx.astype(BF16)
        dgkv_ref[...] += dg
        dproj_ref[:, 1024:MLA_Z0] = dkr_ref[...].astype(BF16)
        dproj_ref[:, MLA_Z0:MLA_Z1] = dz_ref[...]
        dproj_ref[:, MLA_Z1:MLA_PROJ] = jnp.zeros((tr, MLA_PROJ - MLA_Z1), BF16)

    return pl.pallas_call(
        body, grid=(t // tr,),
        in_specs=[_row_spec(tr, 1024), _row_spec(tr, 768), _row_spec(tr, 256), _row_spec(tr, 128), _row_spec(tr, 1024),
                  _full_spec((1, 768)), _full_spec((1, 256))],
        out_specs=[_row_spec(tr, MLA_PROJ), _full_spec((1, 768)), _full_spec((1, 256))],
        out_shape=[S((t, MLA_PROJ), BF16), S((1, 768), F32), S((1, 256), F32)],
        name="mla_norms_bwd", compiler_params=_cp(("arbitrary",)))(proj, dcqn, dckvn, dkr, dz, gq, gkv)


def _mla_pack(qf, kvf, proj, cos, sin):
    t = qf.shape[0]
    tr = 256

    def body(q_ref, kn_ref, kr_ref, cos_ref, sin_ref, qh_ref, kh_ref):
        cs, sn = cos_ref[...], sin_ref[...]
        x1, x2 = q_ref[:, 1024:1280], q_ref[:, 1280:1536]
        r1 = x1 * cs - x2 * sn
        r2 = x2 * cs + x1 * sn
        c16, s16 = cs[:, 0:16], sn[:, 0:16]
        k1, k2 = kr_ref[:, 0:16], kr_ref[:, 16:32]
        kr1 = k1 * c16 - k2 * s16
        kr2 = k2 * c16 + k1 * s16
        zpad = jnp.zeros((tr, 32), F32)
        for h in range(MLA_HEADS):
            qh_ref[h] = jnp.concatenate([q_ref[:, h * 64:(h + 1) * 64], r1[:, h * 16:(h + 1) * 16],
                                         r2[:, h * 16:(h + 1) * 16], zpad], axis=1).astype(BF16)
            kh_ref[h] = jnp.concatenate([kn_ref[:, h * 64:(h + 1) * 64], kr1, kr2, zpad], axis=1).astype(BF16)

    hspec = pl.BlockSpec((MLA_HEADS, tr, 128), lambda i: (0, i, 0))
    return pl.pallas_call(
        body, grid=(t // tr,),
        in_specs=[_row_spec(tr, 1536), _row_spec(tr, 1024), pl.BlockSpec((tr, 128), lambda i: (i, 8)),
                  _row_spec(tr, 256), _row_spec(tr, 256)],
        out_specs=[hspec, hspec], out_shape=[S((MLA_HEADS, t, 128), BF16)] * 2,
        name="mla_pack", compiler_params=_cp(("parallel",)))(qf, kvf, proj, cos, sin)


def _mla_unpack(dqh, dkh, dv, cos, sin):
    t = dqh.shape[1]
    tr = 256

    def body(dqh_ref, dkh_ref, dv_ref, cos_ref, sin_ref, dq_ref, dkv_ref, dkr_ref):
        cs, sn = cos_ref[...], sin_ref[...]
        dqn = jnp.concatenate([dqh_ref[h, :, 0:64] for h in range(MLA_HEADS)], axis=1)
        dr1 = jnp.concatenate([dqh_ref[h, :, 64:80] for h in range(MLA_HEADS)], axis=1)
        dr2 = jnp.concatenate([dqh_ref[h, :, 80:96] for h in range(MLA_HEADS)], axis=1)
        dq_ref[:, 0:1024] = dqn.astype(BF16)
        dq_ref[:, 1024:1280] = (dr1 * cs + dr2 * sn).astype(BF16)
        dq_ref[:, 1280:1536] = (dr2 * cs - dr1 * sn).astype(BF16)
        dkv_ref[:, 0:1024] = jnp.concatenate([dkh_ref[h, :, 0:64] for h in range(MLA_HEADS)], axis=1).astype(BF16)
        dkv_ref[:, 1024:2048] = dv_ref[...].astype(BF16)
        d1 = dkh_ref[0, :, 64:80]
        d2 = dkh_ref[0, :, 80:96]
        for h in range(1, MLA_HEADS):
            d1 = d1 + dkh_ref[h, :, 64:80]
            d2 = d2 + dkh_ref[h, :, 80:96]
        c16, s16 = cs[:, 0:16], sn[:, 0:16]
        dkr_ref[...] = jnp.concatenate([d1 * c16 + d2 * s16, d2 * c16 - d1 * s16, jnp.zeros((tr, 96), F32)], axis=1)

    hspec = pl.BlockSpec((MLA_HEADS, tr, 128), lambda i: (0, i, 0))
    return pl.pallas_call(
        body, grid=(t // tr,), in_specs=[hspec, hspec, _row_spec(tr, 1024), _row_spec(tr, 256), _row_spec(tr, 256)],
        out_specs=[_row_spec(tr, 1536), _row_spec(tr, 2048), _row_spec(tr, 128)],
        out_shape=[S((t, 1536), BF16), S((t, 2048), BF16), S((t, 128), F32)],
        name="mla_unpack", compiler_params=_cp(("parallel",)))(dqh, dkh, dv, cos, sin)


def _mla_attn_fwd(qh, kh, kvf, proj):
    t = qh.shape[1]
    b = MLA_BLK

    def body(q_ref, k_ref, v_ref, z_ref, o_ref, p_ref, lse_ref):
        r = pl.program_id(1)
        row = lax.broadcasted_iota(jnp.int32, (b, b), 0)
        col = lax.broadcasted_iota(jnp.int32, (b, b), 1)
        qs = [q_ref[0], q_ref[1]]

        def block(c0, carry, masked):
            new = []
            for a in range(2):
                m, l, acc = carry[a]
                k = k_ref[a, pl.ds(c0, b), :]
                v = v_ref[pl.ds(c0, b), a * 64:(a + 1) * 64].astype(BF16)
                s = _dot_nt(qs[a], k) * MLA_SCALE
                if masked:
                    s = jnp.where(col <= row, s, NEG_INF)
                mn = jnp.maximum(m, jnp.max(s, axis=-1, keepdims=True))
                alpha = jnp.exp(m - mn)
                p = jnp.exp(s - mn)
                new.append((mn, alpha * l + jnp.sum(p, axis=-1, keepdims=True), alpha * acc + _dot(p.astype(BF16), v)))
            return tuple(new)

        init = (jnp.full((b, 1), NEG_INF, F32), jnp.zeros((b, 1), F32), jnp.zeros((b, 64), F32))
        carry = lax.fori_loop(0, r, lambda c, cr: block(pl.multiple_of(c * b, b), cr, False), (init, init))
        carry = block(pl.multiple_of(r * b, b), carry, True)
        outs = [acc / l for _, l, acc in carry]
        lses = [jnp.broadcast_to(m + jnp.log(l), (b, 64)) for m, l, _ in carry]
        o = jnp.concatenate(outs, axis=1)
        o_ref[...] = o
        p_ref[...] = (o * _silu(z_ref[...])).astype(BF16)
        lse_ref[0] = jnp.concatenate(lses, axis=1)

    return pl.pallas_call(
        body, grid=(MLA_HEADS // 2, t // b),
        in_specs=[pl.BlockSpec((2, b, 128), lambda i, r: (i, r, 0)), pl.BlockSpec((2, t, 128), lambda i, r: (i, 0, 0)),
                  pl.BlockSpec((t, 128), lambda i, r: (0, 8 + i)), pl.BlockSpec((b, 128), lambda i, r: (r, 9 + i))],
        out_specs=[pl.BlockSpec((b, 128), lambda i, r: (r, i)), pl.BlockSpec((b, 128), lambda i, r: (r, i)),
                   pl.BlockSpec((1, b, 128), lambda i, r: (i, r, 0))],
        out_shape=[S((t, 1024), F32), S((t, 1024), BF16), S((MLA_HEADS // 2, t, 128), F32)],
        name="mla_attn_fwd", compiler_params=_cp(("parallel", "parallel")))(qh, kh, kvf, proj)


def _mla_attn_bwd(qh, kh, kvf, proj, o, dpout, lse):
    t = qh.shape[1]
    b = MLA_BLK
    nb = t // b

    def body(q_ref, k_ref, v_ref, z_ref, o_ref, dp_ref, lse_ref, dq_ref, dk_ref, dv_ref, dz_ref):
        dk_ref[...] = jnp.zeros_like(dk_ref)
        dv_ref[...] = jnp.zeros_like(dv_ref)
        row = lax.broadcasted_iota(jnp.int32, (b, b), 0)
        col = lax.broadcasted_iota(jnp.int32, (b, b), 1)
        lanes = [slice(0, 64), slice(64, 128)]

        def rbody(r, carry):
            r0 = pl.multiple_of(r * b, b)
            rows = pl.ds(r0, b)
            qs, dobs, deltas, lses = [], [], [], []
            for a in range(2):
                zz = z_ref[rows, lanes[a]]
                oo = o_ref[rows, lanes[a]]
                dpo = dp_ref[rows, lanes[a]].astype(F32)
                do = dpo * _silu(zz)
                dz_ref[rows, lanes[a]] = (dpo * oo * _silu_grad(zz)).astype(BF16)
                qs.append(q_ref[a, rows, :])
                dobs.append(do.astype(BF16))
                deltas.append(jnp.sum(do * oo, axis=-1, keepdims=True))
                lses.append(lse_ref[0, rows, a * 64:a * 64 + 1])

            def block(c0, dqs, masked):
                cols = pl.ds(c0, b)
                out = []
                for a in range(2):
                    k = k_ref[a, cols, :]
                    v = v_ref[cols, lanes[a]].astype(BF16)
                    s = _dot_nt(qs[a], k) * MLA_SCALE
                    if masked:
                        s = jnp.where(col <= row, s, NEG_INF)
                    p = jnp.exp(s - lses[a])
                    ds = p * (_dot_nt(dobs[a], v) - deltas[a]) * MLA_SCALE
                    dsb = ds.astype(BF16)
                    dk_ref[a, cols, :] += _dot_tn(dsb, qs[a])
                    dv_ref[cols, lanes[a]] += _dot_tn(p.astype(BF16), dobs[a])
                    out.append(dqs[a] + _dot(dsb, k))
                return tuple(out)

            zero = jnp.zeros((b, 128), F32)
            dqs = lax.fori_loop(0, r, lambda c, d: block(pl.multiple_of(c * b, b), d, False), (zero, zero))
            dqs = block(r0, dqs, True)
            dq_ref[0, rows, :] = dqs[0]
            dq_ref[1, rows, :] = dqs[1]
            return carry

        lax.fori_loop(0, nb, rbody, 0)

    pair3 = pl.BlockSpec((2, t, 128), lambda i: (i, 0, 0))
    return pl.pallas_call(
        body, grid=(MLA_HEADS // 2,),
        in_specs=[pair3, pair3, pl.BlockSpec((t, 128), lambda i: (0, 8 + i)), pl.BlockSpec((t, 128), lambda i: (0, 9 + i)),
                  pl.BlockSpec((t, 128), lambda i: (0, i)), pl.BlockSpec((t, 128), lambda i: (0, i)),
                  pl.BlockSpec((1, t, 128), lambda i: (i, 0, 0))],
        out_specs=[pair3, pair3, pl.BlockSpec((t, 128), lambda i: (0, i)), pl.BlockSpec((t, 128), lambda i: (0, i))],
        out_shape=[S((MLA_HEADS, t, 128), F32), S((MLA_HEADS, t, 128), F32), S((t, 1024), F32), S((t, 1024), BF16)],
        name="mla_attn_bwd", compiler_params=_cp(("parallel",), VMEM_BIG))(qh, kh, kvf, proj, o, dpout, lse)


def _dot_nt_hi(a, b):
    return lax.dot_general(a, b, (((1,), (1,)), ((), ())), preferred_element_type=F32,
                           precision=lax.Precision.HIGHEST)


def _ssm_gen(lr, li, ldt, btr, bti, cr, ci):
    n = SSM_BLOCK
    dt = jnp.exp(ldt)
    x = lr * dt
    w = li * dt
    mag = jnp.exp(x)
    ab_re = mag * jnp.cos(w)
    ab_im = mag * jnp.sin(w)
    den = lr * lr + li * li
    nr = ab_re - 1.0
    f_re = (nr * lr + ab_im * li) / den
    f_im = (ab_im * lr - nr * li) / den
    bb_re = f_re * btr - f_im * bti
    bb_im = f_re * bti + f_im * btr
    inv_mag = jnp.exp(-x)
    inv_re, inv_im = inv_mag * jnp.cos(w), -inv_mag * jnp.sin(w)
    one, zero = jnp.ones_like(x), jnp.zeros_like(x)
    pos, neg = [(one, zero)], [(one, zero)]
    for _ in range(SSM_CHUNK):
        pr, pi = pos[-1]
        pos.append((pr * ab_re - pi * ab_im, pr * ab_im + pi * ab_re))
        pr, pi = neg[-1]
        neg.append((pr * inv_re - pi * inv_im, pr * inv_im + pi * inv_re))

    def stack(table, idx):
        return (jnp.concatenate([jnp.broadcast_to(table[idx(t)][0], (SSM_GROUP, SSM_STATE)) for t in range(SSM_CHUNK)], axis=0),
                jnp.concatenate([jnp.broadcast_to(table[idx(t)][1], (SSM_GROUP, SSM_STATE)) for t in range(SSM_CHUNK)], axis=0))

    def rep(v):
        return jnp.concatenate([v] * SSM_CHUNK, axis=0)

    ct_re, ct_im, bt_re, bt_im = rep(cr), rep(ci), rep(bb_re), rep(bb_im)
    p_re, p_im = stack(pos, lambda t: t)
    l_re = ct_re * p_re - ct_im * p_im
    l_im = ct_re * p_im + ct_im * p_re
    n_re, n_im = stack(neg, lambda t: t)
    r_re = bt_re * n_re - bt_im * n_im
    r_im = bt_re * n_im + bt_im * n_re
    rk = lax.broadcasted_iota(jnp.int32, (n, n), 0) // SSM_GROUP
    ct = lax.broadcasted_iota(jnp.int32, (n, n), 1) // SSM_GROUP
    mt = jnp.where(ct >= rk, _dot_nt_hi(r_re, l_re) - _dot_nt_hi(r_im, l_im), 0.0)
    e_re, e_im = stack(pos, lambda t: SSM_CHUNK - 1 - t)
    pin_re = bt_re * e_re - bt_im * e_im
    pin_im = bt_re * e_im + bt_im * e_re
    q_re, q_im = stack(pos, lambda t: t + 1)
    qt_re = ct_re * q_re - ct_im * q_im
    qt_im = -(ct_re * q_im + ct_im * q_re)
    ad_re, ad_im = pos[SSM_CHUNK]
    return mt, pin_re, pin_im, qt_re, qt_im, ad_re, ad_im


_SSM_GEN_OUT = [(SSM_BLOCK, SSM_BLOCK)] + [(SSM_BLOCK, SSM_STATE)] * 4 + [(1, SSM_STATE)] * 2
_SSM_GEN_IN = [(1, SSM_STATE)] * 3 + [(SSM_GROUP, SSM_STATE)] * 4


def _gspec(shape2):
    return pl.BlockSpec((1,) + shape2, lambda g: (g, 0, 0))


def _ssm_operators(params):
    def body(*refs):
        ins, outs = refs[:7], refs[7:]
        for o_ref, val in zip(outs, _ssm_gen(*[r[0] for r in ins])):
            o_ref[0] = val

    return pl.pallas_call(
        body, grid=(SSM_GROUPS,), in_specs=[_gspec(s) for s in _SSM_GEN_IN], out_specs=[_gspec(s) for s in _SSM_GEN_OUT],
        out_shape=[S((SSM_GROUPS,) + s, F32) for s in _SSM_GEN_OUT], name="ssm_operators",
        compiler_params=_cp(("parallel",)))(*params)


def _ssm_operators_bwd(params, cots):
    def body(*refs):
        ins, cts, outs = refs[:7], refs[7:14], refs[14:]
        _, vjp = jax.vjp(_ssm_gen, *[r[0] for r in ins])
        grads = list(vjp(tuple(r[0] for r in cts)))
        grads[2] = jnp.broadcast_to(jnp.sum(grads[2], axis=-1, keepdims=True), (1, SSM_STATE))
        for o_ref, val in zip(outs, grads):
            o_ref[0] = val

    return pl.pallas_call(
        body, grid=(SSM_GROUPS,), in_specs=[_gspec(s) for s in _SSM_GEN_IN] + [_gspec(s) for s in _SSM_GEN_OUT],
        out_specs=[_gspec(s) for s in _SSM_GEN_IN], out_shape=[S((SSM_GROUPS,) + s, F32) for s in _SSM_GEN_IN],
        name="ssm_operators_bwd", compiler_params=_cp(("parallel",)))(*params, *cots)


def _shift_rows(x, sh, row, up):
    n = x.shape[0]
    if up:
        return jnp.where(row < n - sh, pltpu.roll(x, n - sh, axis=0), 0.0)
    return jnp.where(row >= sh, pltpu.roll(x, sh, axis=0), 0.0)


def _carry_scan(sr, si, ar, ai, row, up):
    n = sr.shape[0]
    sh = 1
    while sh < n:
        tr, ti = _shift_rows(sr, sh, row, up), _shift_rows(si, sh, row, up)
        sr, si = sr + ar * tr - ai * ti, si + ar * ti + ai * tr
        ar, ai = ar * ar - ai * ai, 2.0 * ar * ai
        sh *= 2
    return sr, si


def _ssm_states(u, pr_ref, pi_ref, ar_ref, ai_ref, row):
    lre = _dot(u, pr_ref[0].astype(BF16))
    lim = _dot(u, pi_ref[0].astype(BF16))
    sr, si = _carry_scan(lre, lim, ar_ref[0], ai_ref[0], row, False)
    return _shift_rows(sr, 1, row, False), _shift_rows(si, 1, row, False)


def _ssm_core_fwd(u_r, ops):
    g, nc, n = u_r.shape

    def body(u_ref, mt_ref, pr_ref, pi_ref, qr_ref, qi_ref, ar_ref, ai_ref, y_ref):
        u = u_ref[0]
        row = lax.broadcasted_iota(jnp.int32, (nc, SSM_STATE), 0)
        s_re, s_im = _ssm_states(u, pr_ref, pi_ref, ar_ref, ai_ref, row)
        y_ref[0] = (_dot(u, mt_ref[0].astype(BF16)) + _dot_nt(s_re.astype(BF16), qr_ref[0].astype(BF16))
                    + _dot_nt(s_im.astype(BF16), qi_ref[0].astype(BF16)))

    return pl.pallas_call(
        body, grid=(g,), in_specs=[_gspec((nc, n))] + [_gspec(s) for s in _SSM_GEN_OUT], out_specs=_gspec((nc, n)),
        out_shape=S((g, nc, n), F32), name="ssm_core_fwd", compiler_params=_cp(("parallel",)))(u_r, *ops)


def _ssm_core_bwd(u_r, dy_r, ops):
    g, nc, n = u_r.shape

    def body(u_ref, dy_ref, mt_ref, pr_ref, pi_ref, qr_ref, qi_ref, ar_ref, ai_ref,
             du_ref, dmt_ref, dpr_ref, dpi_ref, dqr_ref, dqi_ref, dar_ref, dai_ref):
        u = u_ref[0]
        dy = dy_ref[0].astype(BF16)
        row = lax.broadcasted_iota(jnp.int32, (nc, SSM_STATE), 0)
        s_re, s_im = _ssm_states(u, pr_ref, pi_ref, ar_ref, ai_ref, row)
        ar, ai = ar_ref[0], ai_ref[0]
        g_re = _shift_rows(_dot(dy, qr_ref[0].astype(BF16)), 1, row, True)
        g_im = _shift_rows(_dot(dy, qi_ref[0].astype(BF16)), 1, row, True)
        l_re, l_im = _carry_scan(g_re, g_im, ar, -ai, row, True)
        lrb, lib = l_re.astype(BF16), l_im.astype(BF16)
        du_ref[0] = (_dot_nt(dy, mt_ref[0].astype(BF16)) + _dot_nt(lrb, pr_ref[0].astype(BF16))
                     + _dot_nt(lib, pi_ref[0].astype(BF16)))
        dmt_ref[0] = _dot_tn(u, dy)
        dpr_ref[0] = _dot_tn(u, lrb)
        dpi_ref[0] = _dot_tn(u, lib)
        dqr_ref[0] = _dot_tn(dy, s_re.astype(BF16))
        dqi_ref[0] = _dot_tn(dy, s_im.astype(BF16))
        dar_ref[0] = jnp.sum(l_re * s_re + l_im * s_im, axis=0, keepdims=True)
        dai_ref[0] = jnp.sum(l_im * s_re - l_re * s_im, axis=0, keepdims=True)

    return pl.pallas_call(
        body, grid=(g,), in_specs=[_gspec((nc, n)), _gspec((nc, n))] + [_gspec(s) for s in _SSM_GEN_OUT],
        out_specs=[_gspec((nc, n))] + [_gspec(s) for s in _SSM_GEN_OUT],
        out_shape=[S((g, nc, n), F32)] + [S((g,) + s, F32) for s in _SSM_GEN_OUT],
        name="ssm_core_bwd", compiler_params=_cp(("parallel",)))(u_r, dy_r, *ops)


def _to_groups(a, dtype):
    t = a.shape[0]
    return (a.astype(dtype).reshape(t // SSM_CHUNK, SSM_CHUNK, SSM_GROUPS, SSM_GROUP)
            .transpose(2, 0, 1, 3).reshape(SSM_GROUPS, t // SSM_CHUNK, SSM_BLOCK))


def _from_groups(a):
    g, nc, _ = a.shape
    return a.reshape(g, nc, SSM_CHUNK, SSM_GROUP).transpose(1, 2, 0, 3).reshape(nc * SSM_CHUNK, g * SSM_GROUP)


def _s5_act(y_core, proj, d_skip):
    t = y_core.shape[0]
    tr = 256

    def body(y_ref, u_ref, d_ref, o_ref):
        o_ref[...] = _gelu(y_ref[...] + d_ref[...] * u_ref[...])

    return pl.pallas_call(body, grid=(t // tr,), in_specs=[_row_spec(tr, 1024), _row_spec(tr, 1024), _full_spec((1, 1024))],
                          out_specs=_row_spec(tr, 1024), out_shape=S((t, 1024), F32), name="s5_act",
                          compiler_params=_cp(("parallel",)))(y_core, proj, d_skip)


def _s5_gate(yg, tg, b_glu, proj):
    t = yg.shape[0]
    tr = 256

    def body(y_ref, t_ref, b_ref, z_ref, o_ref):
        o_ref[...] = (y_ref[...] * jax.nn.sigmoid(t_ref[...] + b_ref[...]) * _silu(z_ref[...])).astype(BF16)

    return pl.pallas_call(
        body, grid=(t // tr,),
        in_specs=[_row_spec(tr, 1024), _row_spec(tr, 1024), _full_spec((1, 1024)), pl.BlockSpec((tr, 1024), lambda i: (i, 1))],
        out_specs=_row_spec(tr, 1024), out_shape=S((t, 1024), BF16), name="s5_gate",
        compiler_params=_cp(("parallel",)))(yg, tg, b_glu, proj)


def _s5_gate_bwd(dp, yg, tg, b_glu, proj):
    t = yg.shape[0]
    tr = 256

    def body(dp_ref, y_ref, t_ref, b_ref, z_ref, dt_ref, dy_ref, dz_ref, db_ref):
        @pl.when(pl.program_id(0) == 0)
        def _():
            db_ref[...] = jnp.zeros_like(db_ref)

        z = z_ref[...]
        yv = y_ref[...]
        sg = jax.nn.sigmoid(t_ref[...] + b_ref[...])
        dpv = dp_ref[...].astype(F32)
        do = dpv * _silu(z)
        dz_ref[...] = (dpv * yv * sg * _silu_grad(z)).astype(BF16)
        dy_ref[...] = do * sg
        dtv = do * yv * sg * (1.0 - sg)
        dt_ref[...] = dtv.astype(BF16)
        db_ref[...] += jnp.sum(dtv, axis=0, keepdims=True)

    return pl.pallas_call(
        body, grid=(t // tr,),
        in_specs=[_row_spec(tr, 1024), _row_spec(tr, 1024), _row_spec(tr, 1024), _full_spec((1, 1024)),
                  pl.BlockSpec((tr, 1024), lambda i: (i, 1))],
        out_specs=[_row_spec(tr, 1024), _row_spec(tr, 1024), _row_spec(tr, 1024), _full_spec((1, 1024))],
        out_shape=[S((t, 1024), BF16), S((t, 1024), F32), S((t, 1024), BF16), S((1, 1024), F32)],
        name="s5_gate_bwd", compiler_params=_cp(("arbitrary",)))(dp, yg, tg, b_glu, proj)


def _s5_act_bwd(dyg1, dyg2, y_core, proj, d_skip):
    t = y_core.shape[0]
    tr = 256

    def body(a_ref, b_ref, y_ref, u_ref, d_ref, dy_ref, du_ref, dd_ref):
        @pl.when(pl.program_id(0) == 0)
        def _():
            dd_ref[...] = jnp.zeros_like(dd_ref)

        u = u_ref[...]
        dv = (a_ref[...] + b_ref[...]) * _gelu_grad(y_ref[...] + d_ref[...] * u)
        dy_ref[...] = dv.astype(BF16)
        du_ref[...] = dv * d_ref[...]
        dd_ref[...] += jnp.sum(dv * u, axis=0, keepdims=True)

    return pl.pallas_call(
        body, grid=(t // tr,),
        in_specs=[_row_spec(tr, 1024)] * 4 + [_full_spec((1, 1024))],
        out_specs=[_row_spec(tr, 1024), _row_spec(tr, 1024), _full_spec((1, 1024))],
        out_shape=[S((t, 1024), BF16), S((t, 1024), F32), S((1, 1024), F32)],
        name="s5_act_bwd", compiler_params=_cp(("arbitrary",)))(dyg1, dyg2, y_core, proj, d_skip)


def _s5_dproj(du_skip, du_core, dz):
    t = du_skip.shape[0]
    tr = 256

    def body(a_ref, b_ref, z_ref, o_ref):
        o_ref[:, 0:1024] = (a_ref[...] + b_ref[...]).astype(BF16)
        o_ref[:, 1024:2048] = z_ref[...]

    return pl.pallas_call(body, grid=(t // tr,), in_specs=[_row_spec(tr, 1024)] * 3, out_specs=_row_spec(tr, 2048),
                          out_shape=S((t, 2048), BF16), name="s5_dproj",
                          compiler_params=_cp(("parallel",)))(du_skip, du_core, dz)


def _s5_params(lam_re, lam_im, log_dt, b_re, b_im, c_re, c_im):
    g = SSM_GROUPS
    return (lam_re.reshape(g, 1, SSM_STATE), lam_im.reshape(g, 1, SSM_STATE),
            jnp.broadcast_to(log_dt.reshape(g, 1, 1), (g, 1, SSM_STATE)),
            b_re.transpose(0, 2, 1), b_im.transpose(0, 2, 1), c_re, c_im)


def _s5_fwd(proj, params, d_skip, w_glu, b_glu):
    ops = _ssm_operators(params)
    y_core = _from_groups(_ssm_core_fwd(_to_groups(proj[:, :1024], BF16), ops))
    yg = _s5_act(y_core, proj, d_skip)
    tg = _mm(yg, w_glu, name="s5_glu")
    return _s5_gate(yg, tg, b_glu, proj), (ops, y_core, yg, tg)


def _s5_bwd(proj, res, dp, params, d_skip, w_glu, b_glu):
    ops, y_core, yg, tg = res
    dtg, dyg1, dz, db_glu = _s5_gate_bwd(dp, yg, tg, b_glu, proj)
    dyg2 = _mm(dtg, w_glu, tb=True, name="s5_glu_dx")
    dw_glu = _mm(yg, dtg, ta=True, out_dtype=BF16, name="s5_glu_dw")
    dy_core, du_skip, dd = _s5_act_bwd(dyg1, dyg2, y_core, proj, d_skip)
    outs = _ssm_core_bwd(_to_groups(proj[:, :1024], BF16), _to_groups(dy_core, BF16), ops)
    dproj = _s5_dproj(du_skip, _from_groups(outs[0]), dz)
    glr, gli, gdt, gbtr, gbti, gcr, gci = _ssm_operators_bwd(params, outs[1:])
    g = SSM_GROUPS
    grads = (glr.reshape(g, SSM_STATE), gli.reshape(g, SSM_STATE), gdt[:, 0, 0].reshape(1, g),
             gbtr.transpose(0, 2, 1), gbti.transpose(0, 2, 1), gcr, gci)
    return dproj, grads, dd, dw_glu, db_glu


def _perm_uq(w):
    w3 = w.reshape(w.shape[0], MLA_HEADS, MLA_NOPE + MLA_ROPE)
    return jnp.concatenate([w3[:, :, :64].reshape(-1, 1024), w3[:, :, 64:80].reshape(-1, 256),
                            w3[:, :, 80:96].reshape(-1, 256)], axis=1)


def _unperm_uq(w):
    r = w.shape[0]
    return jnp.concatenate([w[:, :1024].reshape(r, 16, 64), w[:, 1024:1280].reshape(r, 16, 16),
                            w[:, 1280:1536].reshape(r, 16, 16)], axis=2).reshape(r, 1536)


def _perm_ukv(w):
    w3 = w.reshape(w.shape[0], MLA_HEADS, 128)
    return jnp.concatenate([w3[:, :, :64].reshape(-1, 1024), w3[:, :, 64:].reshape(-1, 1024)], axis=1)


def _unperm_ukv(w):
    r = w.shape[0]
    return jnp.concatenate([w[:, :1024].reshape(r, 16, 64), w[:, 1024:].reshape(r, 16, 64)], axis=2).reshape(r, 2048)


def _mla_fwd(proj, gq, gkv, wuq, wukv, cos, sin):
    cqn, ckvn = _mla_norms(proj, gq, gkv)
    qf = _mm(cqn, wuq, name="mla_uq")
    kvf = _mm(ckvn, wukv, name="mla_ukv")
    qh, kh = _mla_pack(qf, kvf, proj, cos, sin)
    o, pout, lse = _mla_attn_fwd(qh, kh, kvf, proj)
    return pout, (cqn, ckvn, kvf, qh, kh, o, lse)


def _mla_bwd(proj, res, dpout, gq, gkv, wuq, wukv, cos, sin):
    cqn, ckvn, kvf, qh, kh, o, lse = res
    dqh, dkh, dv, dz = _mla_attn_bwd(qh, kh, kvf, proj, o, dpout, lse)
    dqf, dkvf, dkr = _mla_unpack(dqh, dkh, dv, cos, sin)
    dcqn = _mm(dqf, wuq, tb=True, name="mla_uq_dx")
    dwuq = _mm(cqn, dqf, ta=True, out_dtype=BF16, name="mla_uq_dw")
    dckvn = _mm(dkvf, wukv, tb=True, name="mla_ukv_dx")
    dwukv = _mm(ckvn, dkvf, ta=True, out_dtype=BF16, name="mla_ukv_dw")
    dproj, dgq, dgkv = _mla_norms_bwd(proj, dcqn, dckvn, dkr, dz, gq, gkv)
    return dproj, dgq, dgkv, dwuq, dwukv


_ANY = pl.BlockSpec(memory_space=pl.ANY)


def _chip_peers():
    x, y, c = lax.axis_index("x"), lax.axis_index("y"), lax.axis_index("c")
    return 2 * x + y, c, [(1 - x, y), (x, 1 - y), (1 - x, 1 - y)]


def _exchange_chips(src_of, dst_of, send_sems, recv_sems, loc_sem):
    me, c, peers = _chip_peers()
    loc = pltpu.make_async_copy(src_of(me), dst_of(me), loc_sem)
    loc.start()

    def copy(k, px, py, src_chip, dst_chip):
        return pltpu.make_async_remote_copy(src_ref=src_of(src_chip), dst_ref=dst_of(dst_chip), send_sem=send_sems.at[k],
                                            recv_sem=recv_sems.at[k], device_id=(px, py, c), device_id_type=MESH)

    sends = [copy(k, px, py, 2 * px + py, me) for k, (px, py) in enumerate(peers)]
    for cp in sends:
        cp.start()
    for k, (px, py) in enumerate(peers):
        copy(k, px, py, me, 2 * px + py).wait_recv()
    for cp in sends:
        cp.wait_send()
    loc.wait()


_EXCH_SCRATCH = [pltpu.SemaphoreType.DMA((3,)), pltpu.SemaphoreType.DMA((3,)), pltpu.SemaphoreType.DMA(())]


def _all_gather_chips(xs, name):
    def body(x_ref, o_ref, send_sems, recv_sems, loc_sem):
        _exchange_chips(lambda s: x_ref, lambda s: o_ref.at[s], send_sems, recv_sems, loc_sem)

    return pl.pallas_call(body, in_specs=[_ANY], out_specs=_ANY, out_shape=S((N_CHIPS,) + xs.shape, xs.dtype),
                          scratch_shapes=_EXCH_SCRATCH, name=name)(xs)


def _sibling_swap(ps, name):
    n = len(ps)

    def body(*refs):
        p_refs, q_refs, send_sems, recv_sems = refs[:n], refs[n:2 * n], refs[2 * n], refs[2 * n + 1]
        x, y, c = lax.axis_index("x"), lax.axis_index("y"), lax.axis_index("c")
        cps = [pltpu.make_async_remote_copy(src_ref=p_refs[i], dst_ref=q_refs[i], send_sem=send_sems.at[i],
                                            recv_sem=recv_sems.at[i], device_id=(x, y, 1 - c), device_id_type=MESH)
               for i in range(n)]
        for cp in cps:
            cp.start()
        for cp in cps:
            cp.wait()

    return pl.pallas_call(body, in_specs=[_ANY] * n, out_specs=[_ANY] * n, out_shape=[S(p.shape, p.dtype) for p in ps],
                          scratch_shapes=[pltpu.SemaphoreType.DMA((n,)), pltpu.SemaphoreType.DMA((n,))], name=name)(*ps)


def _quarter_spec(tr, r, c, dim):
    if dim is None:
        return pl.BlockSpec((1, tr, c), lambda i, me: (me[0], i, 0))
    if dim == 0:
        return pl.BlockSpec((tr, c), lambda i, me: (me[0] * (r // tr) + i, 0))
    return pl.BlockSpec((tr, c), lambda i, me: (i, me[0]))


def _whole_shape(r, c, dim):
    return (N_CHIPS, r, c) if dim is None else ((N_CHIPS * r, c) if dim == 0 else (r, N_CHIPS * c))


def _row_tile(r):
    return 256 if r % 256 == 0 else (128 if r % 128 == 0 else r)


def _cast_place(me, x, dim, dtype, name):
    r, c = x.shape
    tr = _row_tile(r)

    def body(me_ref, x_ref, o_ref):
        o_ref[...] = x_ref[...].astype(dtype).reshape(o_ref.shape)

    return pl.pallas_call(
        body, out_shape=S(_whole_shape(r, c, dim), dtype), name=name,
        grid_spec=pltpu.PrefetchScalarGridSpec(num_scalar_prefetch=1, grid=(r // tr,),
                                               in_specs=[pl.BlockSpec((tr, c), lambda i, me: (i, 0))],
                                               out_specs=_quarter_spec(tr, r, c, dim)),
        compiler_params=_cp(("parallel",)))(me, x)


def _sum_parts(me, g, land, dim, name):
    _, r, c = land.shape
    tr = _row_tile(r)

    def body(me_ref, g_ref, l_ref, o_ref):
        own = g_ref[...].astype(F32).reshape(tr, c)
        o_ref[...] = ((own + l_ref[0].astype(F32)) + l_ref[1].astype(F32)) + l_ref[2].astype(F32)

    return pl.pallas_call(
        body, out_shape=S((r, c), F32), name=name,
        grid_spec=pltpu.PrefetchScalarGridSpec(
            num_scalar_prefetch=1, grid=(r // tr,),
            in_specs=[_quarter_spec(tr, r, c, dim), pl.BlockSpec((3, tr, c), lambda i, me: (0, i, 0))],
            out_specs=pl.BlockSpec((tr, c), lambda i, me: (i, 0))),
        compiler_params=_cp(("parallel",)))(me, g, land)


_HBM = pl.BlockSpec(memory_space=pltpu.HBM)
_SEM = pl.BlockSpec(memory_space=pltpu.SEMAPHORE)
_EFFECT = pltpu.SideEffectType.DATAFLOW_SIDE_EFFECTING


def _quarter(ref, s, dim):
    n = ref.shape[dim] // N_CHIPS
    start = pl.multiple_of(s * n, n)
    return ref.at[pl.ds(start, n), :] if dim == 0 else ref.at[:, pl.ds(start, n)]


def _part(ref, s, dim):
    return ref.at[s] if dim is None else _quarter(ref, s, dim)


def _gather_plan(i, dim):
    return dict(src=i, dst=i, send_src=lambda ref, me, peer, k: _part(ref, me, dim),
                send_dst=lambda ref, me, peer, k: _part(ref, me, dim), recv_dst=lambda ref, me, peer, k: _part(ref, peer, dim))


def _scatter_plan(i, j, dim):
    return dict(src=i, dst=j, send_src=lambda ref, me, peer, k: _part(ref, peer, dim),
                send_dst=lambda ref, me, peer, k: ref.at[k], recv_dst=lambda ref, me, peer, k: ref.at[k])


def _hbm(a):
    return pltpu.with_memory_space_constraint(a, pltpu.HBM)


def _remote_copies(refs, send_sems, recv_sems, plans):
    me, c, peers = _chip_peers()
    sends, recvs = [], []
    for i, p in enumerate(plans):
        for k, (px, py) in enumerate(peers):
            peer = 2 * px + py
            sems = dict(send_sem=send_sems.at[3 * i + k], recv_sem=recv_sems.at[3 * i + k], device_id=(px, py, c),
                        device_id_type=MESH)
            src = p['send_src'](refs[p['src']], me, peer, k)
            sends.append(pltpu.make_async_remote_copy(src_ref=src, dst_ref=p['send_dst'](refs[p['dst']], me, peer, k), **sems))
            recvs.append(pltpu.make_async_remote_copy(src_ref=src, dst_ref=p['recv_dst'](refs[p['dst']], me, peer, k), **sems))
    return sends, recvs


def _exchange_start(arrays, plans, name):
    n, m = len(arrays), len(plans)

    def body(*refs):
        send_sems, recv_sems, token = refs[n], refs[n + 1], refs[-1]
        sends, _ = _remote_copies(refs[:n], send_sems, recv_sems, plans)
        for cp in sends:
            cp.start()
        token[...] = jnp.zeros_like(token)

    outs = pl.pallas_call(
        body, name=name,
        out_shape=(pltpu.SemaphoreType.DMA((3 * m,)), pltpu.SemaphoreType.DMA((3 * m,)),
                   *[pltpu.HBM(a.shape, a.dtype) for a in arrays], S((8, 128), F32)),
        in_specs=[_HBM] * n, out_specs=(_SEM, _SEM, *[_HBM] * n, pl.BlockSpec(memory_space=pltpu.VMEM)),
        input_output_aliases={i: 2 + i for i in range(n)},
        compiler_params=pltpu.CompilerParams(has_side_effects=_EFFECT))(*[_hbm(a) for a in arrays])
    return outs[0], outs[1], outs[2:2 + n], outs[-1]


def _exchange_wait(started, plans, after, name):
    send_sems, recv_sems, arrays, _ = started
    n = len(arrays)

    def body(*refs):
        sends, recvs = _remote_copies(refs[:n], refs[n], refs[n + 1], plans)
        for cp in sends:
            cp.wait_send()
        for cp in recvs:
            cp.wait_recv()

    return pl.pallas_call(
        body, name=name, out_shape=tuple(pltpu.HBM(a.shape, a.dtype) for a in arrays),
        in_specs=[_HBM] * n + [_SEM, _SEM, _ANY], out_specs=tuple([_HBM] * n),
        input_output_aliases={i: i for i in range(n)},
        compiler_params=pltpu.CompilerParams(has_side_effects=_EFFECT))(*arrays, send_sems, recv_sems, after)


def _add2(p, q):
    r, c = p.shape
    tr = 128 if r % 128 == 0 else r

    def body(p_ref, q_ref, o_ref):
        o_ref[...] = p_ref[...] + q_ref[...]

    return pl.pallas_call(body, grid=(r // tr,), in_specs=[_row_spec(tr, c), _row_spec(tr, c)], out_specs=_row_spec(tr, c),
                          out_shape=S((r, c), F32), name="add_cores", compiler_params=_cp(("parallel",)))(p, q)


def _adamw(w, g, m, v, name, g2=None):
    r, c = w.shape
    tr = 256 if r % 256 == 0 else (128 if r % 128 == 0 else r)
    c1 = 1.0 / (1.0 - ADAM_B1 ** ADAM_STEP)
    c2 = 1.0 / (1.0 - ADAM_B2 ** ADAM_STEP)
    gs = [g] if g2 is None else [g, g2]

    def body(*refs):
        w_ref, m_ref, v_ref = refs[0], refs[1], refs[2]
        g_refs = refs[3:3 + len(gs)]
        go_ref, d_ref, nm_ref, nv_ref = refs[3 + len(gs):]
        gv = g_refs[0][...]
        if len(gs) == 2:
            gv = gv + g_refs[1][...]
        go_ref[...] = gv
        nm = ADAM_B1 * m_ref[...] + (1.0 - ADAM_B1) * gv
        nv = ADAM_B2 * v_ref[...] + (1.0 - ADAM_B2) * (gv * gv)
        nm_ref[...] = nm
        nv_ref[...] = nv
        d_ref[...] = -ADAM_LR * ((nm * c1) / (jnp.sqrt(nv * c2) + ADAM_EPS) + ADAM_WD * w_ref[...])

    return pl.pallas_call(body, grid=(r // tr,), in_specs=[_row_spec(tr, c)] * (3 + len(gs)),
                          out_specs=[_row_spec(tr, c)] * 4, out_shape=[S((r, c), F32)] * 4, name=name,
                          compiler_params=_cp(("parallel",)))(w, m, v, *gs)


_WEIGHTS = ['pre_norm', 'post_norm', 'rel_bias', 'a_w_in', 'a_lam_re', 'a_lam_im', 'a_log_dt', 'a_b_re', 'a_b_im',
            'a_c_re', 'a_c_im', 'a_d', 'a_w_glu', 'a_b_glu', 'a_w_out', 'b_w_in', 'b_sinks', 'b_w_out', 'c_w_in',
            'c_q_norm', 'c_kv_norm', 'c_w_uq', 'c_w_ukv', 'c_w_out', 'd_w_in', 'd_ln_g', 'd_ln_b', 'd_w_s', 'd_b_s',
            'd_w_out']
_BIG = [('a_w_in', 1024, 2048, 1), ('a_w_glu', 1024, 1024, 0), ('a_w_out', 1024, 1024, 0), ('b_w_in', 1024, 2304, 1),
        ('b_w_out', 1024, 1024, 0), ('c_w_in', 1024, 2080, 1), ('c_w_uq', 768, 1536, 1), ('c_w_ukv', 256, 2048, 1),
        ('c_w_out', 1024, 1024, 0), ('d_w_in', 1024, 3072, 1), ('d_w_out', 1024, 1024, 0)]
_SHARDED_VECS = [('c_q_norm', 768), ('c_kv_norm', 256), ('d_ln_g', 1024), ('d_ln_b', 1024)]
_REPLICATED = [n for n in _WEIGHTS if n not in [b[0] for b in _BIG] and n not in [s[0] for s in _SHARDED_VECS]]
_LANES = 1024
_SMALL_Q_ROWS = 136
_SMALL_PACK_ROWS = 640


def _pad_rows(flat, rows):
    return jnp.pad(flat, (0, rows * _LANES - flat.shape[0])).reshape(rows, _LANES)


def kernel(x, pre_norm, post_norm, rel_bias, a_w_in, a_lam_re, a_lam_im, a_log_dt, a_b_re, a_b_im, a_c_re, a_c_im, a_d, a_w_glu, a_b_glu, a_w_out, b_w_in, b_sinks, b_w_out, c_w_in, c_q_norm, c_kv_norm, c_w_uq, c_w_ukv, c_w_out, d_w_in, d_ln_g, d_ln_b, d_w_s, d_b_s, d_w_out, loss_target, m_pre_norm, m_post_norm, m_rel_bias, m_a_w_in, m_a_lam_re, m_a_lam_im, m_a_log_dt, m_a_b_re, m_a_b_im, m_a_c_re, m_a_c_im, m_a_d, m_a_w_glu, m_a_b_glu, m_a_w_out, m_b_w_in, m_b_sinks, m_b_w_out, m_c_w_in, m_c_q_norm, m_c_kv_norm, m_c_w_uq, m_c_w_ukv, m_c_w_out, m_d_w_in, m_d_ln_g, m_d_ln_b, m_d_w_s, m_d_b_s, m_d_w_out, v_pre_norm, v_post_norm, v_rel_bias, v_a_w_in, v_a_lam_re, v_a_lam_im, v_a_log_dt, v_a_b_re, v_a_b_im, v_a_c_re, v_a_c_im, v_a_d, v_a_w_glu, v_a_b_glu, v_a_w_out, v_b_w_in, v_b_sinks, v_b_w_out, v_c_w_in, v_c_q_norm, v_c_kv_norm, v_c_w_uq, v_c_w_ukv, v_c_w_out, v_d_w_in, v_d_ln_g, v_d_ln_b, v_d_w_s, v_d_b_s, v_d_w_out):
    arg = dict(locals())
    me = 2 * lax.axis_index("x") + lax.axis_index("y")
    xin = x[0]
    t = xin.shape[0]

    big = {n: (r, c, dim) for n, r, c, dim in _BIG}

    me1 = me.reshape(1).astype(jnp.int32)

    def gather_start(names, tag):
        wholes, plans = [], []
        for i, n in enumerate(names):
            if n in big:
                r, c, dim = big[n]
                src, dtype = arg[n][0], BF16
                if dim == 1 and (c // N_CHIPS) % 128:
                    dim = None
            else:
                src, dim, dtype = arg[n], None, F32
            wholes.append(_cast_place(me1, src, dim, dtype, "place_" + n))
            plans.append(_gather_plan(i, dim))
        return _exchange_start(wholes, plans, "gather_start_" + tag), plans

    def gather_wait(names, started, after, tag):
        st, plans = started
        out = {}
        for n, a in zip(names, _exchange_wait(st, plans, after, "gather_wait_" + tag)):
            if n in big:
                r, c, dim = big[n]
                out[n] = a if a.ndim == 2 else a.transpose(1, 0, 2).reshape(r, c)
            else:
                out[n] = a.reshape(1, -1)
        return out

    groups = {'a1': ['a_w_in'], 'a2': ['a_w_glu', 'a_w_out'], 'b': ['b_w_in', 'b_w_out'],
              'c': ['c_w_in', 'c_w_uq', 'c_w_ukv', 'c_w_out', 'c_q_norm', 'c_kv_norm'],
              'd': ['d_w_in', 'd_w_out', 'd_ln_g', 'd_ln_b']}
    started = {tag: gather_start(names, tag) for tag, names in groups.items()}
    tokens = [st[0][3] for st in started.values()]
    wt = gather_wait(groups['a1'], started['a1'], xin, 'a1')
    cos, sin = _rope_tables(t)
    bucket = jnp.asarray(_t5_bucket_table())
    s5p = _s5_params(a_lam_re[0], a_lam_im[0], a_log_dt[0], a_b_re[0], a_b_im[0], a_c_re[0], a_c_im[0])
    bst = d_b_s[0].T

    x0 = xin
    h0 = _prenorm(x0, pre_norm[0:1], "pre0", after=tokens)
    proj0 = _mm(h0, wt['a_w_in'], name="a_in")
    wt.update(gather_wait(groups['a2'], started['a2'], proj0, 'a2'))
    p0, res0 = _s5_fwd(proj0, s5p, a_d, wt['a_w_glu'], a_b_glu)
    y0 = _mm(p0, wt['a_w_out'], name="a_out")
    x1 = _postnorm(x0, y0, post_norm[0:1], "post0")

    wt.update(gather_wait(groups['b'], started['b'], x1, 'b'))
    h1 = _prenorm(x1, pre_norm[1:2], "pre1")
    proj1 = _mm(h1, wt['b_w_in'], name="b_in")
    bias = _swa_bias(rel_bias, bucket)
    p1 = _swa_fwd(proj1, bias, b_sinks)
    y1 = _mm(p1, wt['b_w_out'], name="b_out")
    x2 = _postnorm(x1, y1, post_norm[1:2], "post1")

    wt.update(gather_wait(groups['c'], started['c'], x2, 'c'))
    w_c_in = jnp.concatenate([wt['c_w_in'][:, :1056], jnp.zeros((1024, 96), BF16), wt['c_w_in'][:, 1056:],
                              jnp.zeros((1024, MLA_PROJ - MLA_Z1), BF16)], axis=1)
    w_uq = _perm_uq(wt['c_w_uq'])
    w_ukv = _perm_ukv(wt['c_w_ukv'])
    h2 = _prenorm(x2, pre_norm[2:3], "pre2")
    proj2 = _mm(h2, w_c_in, name="c_in")
    p2, res2 = _mla_fwd(proj2, wt['c_q_norm'], wt['c_kv_norm'], w_uq, w_ukv, cos, sin)
    y2 = _mm(p2, wt['c_w_out'], name="c_out")
    x3 = _postnorm(x2, y2, post_norm[2:3], "post2")

    wt.update(gather_wait(groups['d'], started['d'], x3, 'd'))
    vec = wt
    h3 = _prenorm(x3, pre_norm[3:4], "pre3")
    proj3 = _mm(h3, wt['d_w_in'], name="d_in")
    p3 = _sgu_fwd(proj3, vec['d_ln_g'], vec['d_ln_b'], d_w_s[0], bst)
    y3 = _mm(p3, wt['d_w_out'], name="d_out")
    x4 = _postnorm(x3, y3, post_norm[3:4], "post3")

    loss_part, dx = _loss_head(x4, loss_target[0])
    loss = lax.psum(loss_part[0, 0], ("x", "y", "c"))

    gr = {}
    d_pre, d_post = [None] * 4, [None] * 4
    scattered = []

    def scatter_start(items, tag):
        srcs = [a for _, a, _ in items]
        n = len(items)
        plans = [_scatter_plan(i, n + i, dim) for i, (_, _, dim) in enumerate(items)]
        lands = []
        for _, a, dim in items:
            part = a.shape[1:] if dim is None else tuple(s // N_CHIPS if i == dim else s for i, s in enumerate(a.shape))
            lands.append(lax.empty((3,) + part, a.dtype))
        st = _exchange_start(srcs + lands, plans, "scatter_start_" + tag)
        scattered.append((st, plans, items, tag))
        return [st[3]]

    def stacked(g):
        r, c = g.shape
        return g.reshape(r, N_CHIPS, c // N_CHIPS).transpose(1, 0, 2)

    dy, d_post[3] = _postnorm_bwd(dx, y3, post_norm[3:4], "post3_bwd")
    dp = _mm(dy, wt['d_w_out'], tb=True, name="d_out_dx")
    g_out = _mm(p3, dy, ta=True, out_dtype=BF16, name="d_out_dw")
    dproj, gr['d_w_s'], dbst, gr['d_ln_g'], gr['d_ln_b'] = _sgu_bwd(proj3, dp, vec['d_ln_g'], vec['d_ln_b'], d_w_s[0], bst)
    gr['d_b_s'] = dbst.T
    dh = _mm(dproj, wt['d_w_in'], tb=True, name="d_in_dx")
    g_in = _mm(h3, dproj, ta=True, out_dtype=BF16, name="d_in_dw")
    tok = scatter_start([('d_w_in', g_in, 1), ('d_w_out', g_out, 0)], 'd')
    dx, d_pre[3] = _prenorm_bwd(dx, dh, x3, pre_norm[3:4], "pre3_bwd")

    dy, d_post[2] = _postnorm_bwd(dx, y2, post_norm[2:3], "post2_bwd", after=tok)
    dp = _mm(dy, wt['c_w_out'], tb=True, name="c_out_dx")
    g_out = _mm(p2, dy, ta=True, out_dtype=BF16, name="c_out_dw")
    dproj, gr['c_q_norm'], gr['c_kv_norm'], dwuq, dwukv = _mla_bwd(proj2, res2, dp, wt['c_q_norm'], wt['c_kv_norm'],
                                                                  w_uq, w_ukv, cos, sin)
    dh = _mm(dproj, w_c_in, tb=True, name="c_in_dx")
    dwc = _mm(h2, dproj, ta=True, out_dtype=BF16, name="c_in_dw")
    g_in = stacked(jnp.concatenate([dwc[:, :1056], dwc[:, MLA_Z0:MLA_Z1]], axis=1))
    tok = scatter_start([('c_w_in', g_in, None), ('c_w_uq', _unperm_uq(dwuq), 1), ('c_w_ukv', _unperm_ukv(dwukv), 1),
                         ('c_w_out', g_out, 0)], 'c')
    dx, d_pre[2] = _prenorm_bwd(dx, dh, x2, pre_norm[2:3], "pre2_bwd")

    dy, d_post[1] = _postnorm_bwd(dx, y1, post_norm[1:2], "post1_bwd", after=tok)
    dp = _mm(dy, wt['b_w_out'], tb=True, name="b_out_dx")
    g_out = _mm(p1, dy, ta=True, out_dtype=BF16, name="b_out_dw")
    dqz, dkv, dss, dsk = _swa_bwd(proj1, dp, bias, b_sinks)
    gr['rel_bias'], gr['b_sinks'] = _swa_dbias(dss, dsk, bucket)
    dproj = jnp.concatenate([dqz[:, :1024], dkv.astype(BF16), dqz[:, 1024:]], axis=1)
    dh = _mm(dproj, wt['b_w_in'], tb=True, name="b_in_dx")
    g_in = stacked(_mm(h1, dproj, ta=True, out_dtype=BF16, name="b_in_dw"))
    tok = scatter_start([('b_w_in', g_in, None), ('b_w_out', g_out, 0)], 'b')
    dx, d_pre[1] = _prenorm_bwd(dx, dh, x1, pre_norm[1:2], "pre1_bwd")

    dy, d_post[0] = _postnorm_bwd(dx, y0, post_norm[0:1], "post0_bwd", after=tok)
    dp = _mm(dy, wt['a_w_out'], tb=True, name="a_out_dx")
    g_out = _mm(p0, dy, ta=True, out_dtype=BF16, name="a_out_dw")
    dproj, s5g, gr['a_d'], g_glu, gr['a_b_glu'] = _s5_bwd(proj0, res0, dp, s5p, a_d, wt['a_w_glu'], a_b_glu)
    for n, g in zip(['a_lam_re', 'a_lam_im', 'a_log_dt', 'a_b_re', 'a_b_im', 'a_c_re', 'a_c_im'], s5g):
        gr[n] = g
    scatter_start([('a_w_glu', g_glu, 0), ('a_w_out', g_out, 0)], 'a2')
    dh = _mm(dproj, wt['a_w_in'], tb=True, name="a_in_dx")
    g_in = _mm(h0, dproj, ta=True, out_dtype=BF16, name="a_in_dw")
    dx, d_pre[0] = _prenorm_bwd(dx, dh, x0, pre_norm[0:1], "pre0_bwd")
    gr['pre_norm'] = jnp.concatenate(d_pre, axis=0)
    gr['post_norm'] = jnp.concatenate(d_post, axis=0)
    small_names = _REPLICATED + [n for n, _ in _SHARDED_VECS]
    small = jnp.concatenate([gr[n].reshape(-1) for n in small_names])
    small = jnp.pad(small, (0, N_CHIPS * _SMALL_Q_ROWS * _LANES - small.shape[0])).reshape(N_CHIPS, _SMALL_Q_ROWS, _LANES)
    scatter_start([('a_w_in', g_in, 1), ('small', small, None)], 'a1')

    partial = {}
    for st, plans, items, tag in scattered:
        done = _exchange_wait(st, plans, dx, "scatter_wait_" + tag)
        for i, (n, _, dim) in enumerate(items):
            partial[n] = _sum_parts(me1, done[i], done[len(items) + i], dim, "sum_parts_" + n)
    order = [n for n, _, _, _ in _BIG] + ['small']
    other = dict(zip(order, _sibling_swap([partial[n] for n in order], "swap_cores")))
    small_q = _add2(partial['small'], other['small'])
    small_all = _all_gather_chips(small_q, "gather_small_grads").reshape(-1)
    grads = {}
    off = 0
    for n in _REPLICATED:
        sz = math.prod(arg[n].shape)
        grads[n] = small_all[off:off + sz].reshape(arg[n].shape)
        off += sz
    for n, sz in _SHARDED_VECS:
        q = sz // N_CHIPS
        grads[n] = lax.dynamic_slice(small_all, (off + me * q,), (q,)).reshape(arg[n].shape)
        off += sz

    delta, new_m, new_v = {}, {}, {}
    for n, _, _, _ in _BIG:
        shp = arg[n].shape
        two = (shp[1], shp[2])
        outs = _adamw(arg[n].reshape(two), partial[n], arg['m_' + n].reshape(two), arg['v_' + n].reshape(two),
                      "adamw_" + n, g2=other[n])
        grads[n], delta[n], new_m[n], new_v[n] = [o.reshape(shp) for o in outs]

    def pack(prefix, src):
        return _pad_rows(jnp.concatenate([src[prefix + n].reshape(-1) for n in small_names]), _SMALL_PACK_ROWS)

    outs = _adamw(pack('', arg), pack('', grads), pack('m_', arg), pack('v_', arg), "adamw_small")
    outs = [o.reshape(-1) for o in outs[1:]]
    off = 0
    for n in small_names:
        shp = arg[n].shape
        sz = math.prod(shp)
        delta[n], new_m[n], new_v[n] = [o[off:off + sz].reshape(shp) for o in outs]
        off += sz

    return (loss, dx.reshape(x.shape), *[grads[n] for n in _WEIGHTS], *[delta[n] for n in _WEIGHTS],
            *[new_m[n] for n in _WEIGHTS], *[new_v[n] for n in _WEIGHTS])
```

```python
import functools
import math

import numpy as np
import jax
import jax.numpy as jnp
from jax import lax
from jax.experimental import pallas as pl
from jax.experimental.pallas import tpu as pltpu

F32 = jnp.float32
BF16 = jnp.bfloat16
S = jax.ShapeDtypeStruct
MESH = pl.DeviceIdType.MESH

D_MODEL = 1024
EPS = 1e-6
NEG_INF = -1e30
N_CHIPS = 4

SSM_GROUPS = 64
SSM_GROUP = 16
SSM_STATE = 64
SSM_CHUNK = 16
SSM_BLOCK = SSM_CHUNK * SSM_GROUP

HEAD_DIM = 64
SWA_HEADS = 16
SWA_GROUP = 8
WINDOW = 128
REL_BUCKETS = 32
REL_MAX_DIST = 128

MLA_HEADS = 16
MLA_NOPE = 64
MLA_ROPE = 32
MLA_Q_RANK = 768
MLA_KV_RANK = 256
MLA_SCALE = (MLA_NOPE + MLA_ROPE) ** -0.5
MLA_BLK = 256
ROPE_BASE = 10000.0

SGU_CHUNK = 128
SGU_GROUPS = 16
SGU_GDIM = 64

ADAM_LR = 0.001
ADAM_B1 = 0.9
ADAM_B2 = 0.999
ADAM_EPS = 1e-08
ADAM_WD = 0.01
ADAM_STEP = 10

VMEM_BIG = 56 * 1024 * 1024


def _cp(sem, vmem=None):
    return pltpu.CompilerParams(dimension_semantics=sem, vmem_limit_bytes=vmem)


def _row_spec(tr, c):
    return pl.BlockSpec((tr, c), lambda i: (i, 0))


def _full_spec(shape):
    nd = len(shape)
    return pl.BlockSpec(shape, lambda *_: (0,) * nd)


def _gelu(x):
    c = 0.7978845608028654
    return 0.5 * x * (1.0 + jnp.tanh(c * (x + 0.044715 * x * x * x)))


def _gelu_grad(x):
    c = 0.7978845608028654
    t = jnp.tanh(c * (x + 0.044715 * x * x * x))
    return 0.5 * (1.0 + t) + 0.5 * x * (1.0 - t * t) * c * (1.0 + 3.0 * 0.044715 * x * x)


def _silu(z):
    return z * jax.nn.sigmoid(z)


def _silu_grad(z):
    s = jax.nn.sigmoid(z)
    return s * (1.0 + z * (1.0 - s))


def _dot(a, b):
    return lax.dot_general(a, b, (((1,), (0,)), ((), ())), preferred_element_type=F32)


def _dot_nt(a, b):
    return lax.dot_general(a, b, (((1,), (1,)), ((), ())), preferred_element_type=F32)


def _dot_tn(a, b):
    return lax.dot_general(a, b, (((0,), (0,)), ((), ())), preferred_element_type=F32)


def _pick(n, cap):
    best = 0
    for d in range(128, min(n, cap) + 1, 128):
        if n % d == 0:
            best = d
    if best < 256:
        return n
    return best


def _mm(a, b, *, ta=False, tb=False, out_dtype=F32, name):
    m, k = (a.shape[1], a.shape[0]) if ta else a.shape
    n = b.shape[0] if tb else b.shape[1]
    assert (b.shape[1] if tb else b.shape[0]) == k
    tm, tn, tk = _pick(m, 1024), _pick(n, 1024), _pick(k, 1024)
    if tn > 1024:
        tm = _pick(m, 256)
    if tk > 1024:
        tm, tn = _pick(m, 256), _pick(n, 256)
    nk = k // tk
    dims = (((0 if ta else 1,), (1 if tb else 0,)), ((), ()))

    def body(a_ref, b_ref, o_ref, acc_ref):
        kk = pl.program_id(2)

        @pl.when(kk == 0)
        def _():
            acc_ref[...] = jnp.zeros_like(acc_ref)

        acc_ref[...] += lax.dot_general(a_ref[...].astype(BF16), b_ref[...].astype(BF16), dims,
                                        preferred_element_type=F32)

        @pl.when(kk == nk - 1)
        def _():
            o_ref[...] = acc_ref[...].astype(out_dtype)

    a_spec = pl.BlockSpec((tk, tm), lambda i, j, kk: (kk, i)) if ta else pl.BlockSpec((tm, tk), lambda i, j, kk: (i, kk))
    b_spec = pl.BlockSpec((tn, tk), lambda i, j, kk: (j, kk)) if tb else pl.BlockSpec((tk, tn), lambda i, j, kk: (kk, j))
    return pl.pallas_call(
        body, grid=(m // tm, n // tn, nk), in_specs=[a_spec, b_spec],
        out_specs=pl.BlockSpec((tm, tn), lambda i, j, kk: (i, j)), out_shape=S((m, n), out_dtype),
        scratch_shapes=[pltpu.VMEM((tm, tn), F32)], name=name,
        compiler_params=_cp(("parallel", "parallel", "arbitrary"), VMEM_BIG))(a, b)


def _prenorm(x, g, name, after=()):
    t, d = x.shape
    tr = 256

    def body(x_ref, g_ref, *rest):
        o_ref = rest[-1]
        xv = x_ref[...]
        r = lax.rsqrt(jnp.mean(xv * xv, axis=-1, keepdims=True) + EPS)
        o_ref[...] = (xv * r * g_ref[...]).astype(BF16)

    return pl.pallas_call(body, grid=(t // tr,),
                          in_specs=[_row_spec(tr, d), _full_spec((1, d))] + [pl.BlockSpec(memory_space=pl.ANY)] * len(after),
                          out_specs=_row_spec(tr, d), out_shape=S((t, d), BF16), name=name,
                          compiler_params=_cp(("parallel",)))(x, g, *after)


def _postnorm(x, y, g, name):
    t, d = x.shape
    tr = 256

    def body(x_ref, y_ref, g_ref, o_ref):
        yv = y_ref[...]
        r = lax.rsqrt(jnp.mean(yv * yv, axis=-1, keepdims=True) + EPS)
        o_ref[...] = x_ref[...] + yv * r * g_ref[...]

    return pl.pallas_call(body, grid=(t // tr,), in_specs=[_row_spec(tr, d), _row_spec(tr, d), _full_spec((1, d))],
                          out_specs=_row_spec(tr, d), out_shape=S((t, d), F32), name=name,
                          compiler_params=_cp(("parallel",)))(x, y, g)


def _rms_bwd_rows(dout, xin, g):
    r = lax.rsqrt(jnp.mean(xin * xin, axis=-1, keepdims=True) + EPS)
    xh = xin * r
    gd = dout * g
    dx = r * (gd - xh * jnp.mean(gd * xh, axis=-1, keepdims=True))
    return dx, jnp.sum(dout * xh, axis=0, keepdims=True)


def _postnorm_bwd(dxn, y, g, name, after=()):
    t, d = y.shape
    tr = 256

    def body(dx_ref, y_ref, g_ref, *rest):
        dy_ref, dg_ref = rest[-2], rest[-1]

        @pl.when(pl.program_id(0) == 0)
        def _():
            dg_ref[...] = jnp.zeros_like(dg_ref)

        dy, dg = _rms_bwd_rows(dx_ref[...], y_ref[...], g_ref[...])
        dy_ref[...] = dy.astype(BF16)
        dg_ref[...] += dg

    return pl.pallas_call(body, grid=(t // tr,),
                          in_specs=[_row_spec(tr, d), _row_spec(tr, d), _full_spec((1, d))]
                          + [pl.BlockSpec(memory_space=pl.ANY)] * len(after),
                          out_specs=[_row_spec(tr, d), _full_spec((1, d))],
                          out_shape=[S((t, d), BF16), S((1, d), F32)], name=name,
                          compiler_params=_cp(("arbitrary",)))(dxn, y, g, *after)


def _prenorm_bwd(dxn, dh, x, g, name):
    t, d = x.shape
    tr = 256

    def body(dxn_ref, dh_ref, x_ref, g_ref, dx_ref, dg_ref):
        @pl.when(pl.program_id(0) == 0)
        def _():
            dg_ref[...] = jnp.zeros_like(dg_ref)

        dx, dg = _rms_bwd_rows(dh_ref[...], x_ref[...], g_ref[...])
        dx_ref[...] = dxn_ref[...] + dx
        dg_ref[...] += dg

    return pl.pallas_call(body, grid=(t // tr,),
                          in_specs=[_row_spec(tr, d), _row_spec(tr, d), _row_spec(tr, d), _full_spec((1, d))],
                          out_specs=[_row_spec(tr, d), _full_spec((1, d))],
                          out_shape=[S((t, d), F32), S((1, d), F32)], name=name,
                          compiler_params=_cp(("arbitrary",)))(dxn, dh, x, g)


def _loss_head(y, target):
    t, d = y.shape
    tr = 256

    def body(y_ref, t_ref, l_ref, dy_ref):
        @pl.when(pl.program_id(0) == 0)
        def _():
            l_ref[...] = jnp.zeros_like(l_ref)

        e = y_ref[...] - t_ref[...]
        dy_ref[...] = e * (1.0 / d)
        l_ref[...] += 0.5 * jnp.sum(jnp.mean(e * e, axis=-1, keepdims=True), axis=0, keepdims=True)

    return pl.pallas_call(body, grid=(t // tr,), in_specs=[_row_spec(tr, d), _row_spec(tr, d)],
                          out_specs=[_full_spec((1, 1)), _row_spec(tr, d)],
                          out_shape=[S((1, 1), F32), S((t, d), F32)], name="loss_head",
                          compiler_params=_cp(("arbitrary",)))(y, target)


def _sgu_parts(proj, lg, lb, ws_ref, bst):
    a = proj[:, 0:1024]
    bb = proj[:, 1024:2048]
    z = proj[:, 2048:3072]
    u = _gelu(a)
    vp = _gelu(bb)
    mu = jnp.mean(vp, axis=-1, keepdims=True)
    xc = vp - mu
    rstd = lax.rsqrt(jnp.mean(xc * xc, axis=-1, keepdims=True) + EPS)
    xh = xc * rstd
    vn = xh * lg + lb
    row = lax.broadcasted_iota(jnp.int32, (SGU_CHUNK, SGU_CHUNK), 0)
    col = lax.broadcasted_iota(jnp.int32, (SGU_CHUNK, SGU_CHUNK), 1)
    tril = row >= col
    ws = [jnp.where(tril, ws_ref[g], 0.0).astype(BF16) for g in range(SGU_GROUPS)]
    s = jnp.concatenate(
        [_dot(ws[g], vn[:, g * 64:(g + 1) * 64].astype(BF16)) + bst[:, g:g + 1] for g in range(SGU_GROUPS)], axis=1)
    return a, bb, z, u, vp, rstd, xh, vn, ws, tril, s


def _sgu_fwd(proj, lg, lb, w_s, bst):
    t = proj.shape[0]
    tr = SGU_CHUNK

    def body(p_ref, lg_ref, lb_ref, ws_ref, bst_ref, o_ref):
        _, _, z, u, _, _, _, _, _, _, s = _sgu_parts(p_ref[...], lg_ref[...], lb_ref[...], ws_ref, bst_ref[...])
        o_ref[...] = (u * s * _silu(z)).astype(BF16)

    return pl.pallas_call(
        body, grid=(t // tr,),
        in_specs=[_row_spec(tr, 3072), _full_spec((1, 1024)), _full_spec((1, 1024)), _full_spec((16, 128, 128)),
                  _full_spec((128, 16))],
        out_specs=_row_spec(tr, 1024), out_shape=S((t, 1024), BF16), name="sgu_fwd",
        compiler_params=_cp(("parallel",)))(proj, lg, lb, w_s, bst)


def _sgu_bwd(proj, dp, lg, lb, w_s, bst):
    t = proj.shape[0]
    tr = SGU_CHUNK

    def body(p_ref, dp_ref, lg_ref, lb_ref, ws_ref, bst_ref, dproj_ref, dws_ref, dbst_ref, dlg_ref, dlb_ref):
        @pl.when(pl.program_id(0) == 0)
        def _():
            dws_ref[...] = jnp.zeros_like(dws_ref)
            dbst_ref[...] = jnp.zeros_like(dbst_ref)
            dlg_ref[...] = jnp.zeros_like(dlg_ref)
            dlb_ref[...] = jnp.zeros_like(dlb_ref)

        lgv = lg_ref[...]
        a, bb, z, u, vp, rstd, xh, vn, ws, tril, s = _sgu_parts(p_ref[...], lgv, lb_ref[...], ws_ref, bst_ref[...])
        dpv = dp_ref[...].astype(F32)
        sz = _silu(z)
        du = dpv * s * sz
        ds = dpv * u * sz
        dz = dpv * u * s * _silu_grad(z)
        dvn_parts, dbs_cols = [], []
        for g in range(SGU_GROUPS):
            dsg = ds[:, g * 64:(g + 1) * 64]
            dsg_b = dsg.astype(BF16)
            dvn_parts.append(_dot_tn(ws[g], dsg_b))
            dws_ref[g] += jnp.where(tril, _dot_nt(dsg_b, vn[:, g * 64:(g + 1) * 64].astype(BF16)), 0.0)
            dbs_cols.append(jnp.sum(dsg, axis=1, keepdims=True))
        dvn = jnp.concatenate(dvn_parts, axis=1)
        dbst_ref[...] += jnp.concatenate(dbs_cols, axis=1)
        dlg_ref[...] += jnp.sum(dvn * xh, axis=0, keepdims=True)
        dlb_ref[...] += jnp.sum(dvn, axis=0, keepdims=True)
        dxh = dvn * lgv
        dvp = rstd * (dxh - jnp.mean(dxh, axis=-1, keepdims=True) - xh * jnp.mean(dxh * xh, axis=-1, keepdims=True))
        dproj_ref[:, 0:1024] = (du * _gelu_grad(a)).astype(BF16)
        dproj_ref[:, 1024:2048] = (dvp * _gelu_grad(bb)).astype(BF16)
        dproj_ref[:, 2048:3072] = dz.astype(BF16)

    return pl.pallas_call(
        body, grid=(t // tr,),
        in_specs=[_row_spec(tr, 3072), _row_spec(tr, 1024), _full_spec((1, 1024)), _full_spec((1, 1024)),
                  _full_spec((16, 128, 128)), _full_spec((128, 16))],
        out_specs=[_row_spec(tr, 3072), _full_spec((16, 128, 128)), _full_spec((128, 16)), _full_spec((1, 1024)),
                   _full_spec((1, 1024))],
        out_shape=[S((t, 3072), BF16), S((16, 128, 128), F32), S((128, 16), F32), S((1, 1024), F32),
                   S((1, 1024), F32)],
        name="sgu_bwd", compiler_params=_cp(("arbitrary",)))(proj, dp, lg, lb, w_s, bst)


def _t5_bucket_table():
    qi = np.arange(WINDOW)[:, None]
    kj = np.arange(2 * WINDOW)[None, :]
    dist = np.maximum(qi + WINDOW - kj, 0)
    max_exact = REL_BUCKETS // 2
    dist_f = np.maximum(dist, 1).astype(np.float32)
    large = max_exact + (np.log(dist_f / np.float32(max_exact)) / np.float32(math.log(REL_MAX_DIST / max_exact))
                         * np.float32(REL_BUCKETS - max_exact)).astype(np.int32)
    large = np.minimum(large, REL_BUCKETS - 1)
    return np.where(dist < max_exact, dist, large).astype(np.int32)


def _swa_bias(rel_bias, bucket):
    def body(rb_ref, bk_ref, o_ref):
        bk = bk_ref[...]
        for h in range(SWA_HEADS):
            acc = jnp.zeros((WINDOW, 2 * WINDOW), F32)
            for b in range(REL_BUCKETS):
                acc = jnp.where(bk == b, rb_ref[b, h], acc)
            o_ref[h] = acc

    return pl.pallas_call(
        body, in_specs=[pl.BlockSpec(memory_space=pltpu.SMEM), pl.BlockSpec(memory_space=pltpu.VMEM)],
        out_specs=pl.BlockSpec(memory_space=pltpu.VMEM), out_shape=S((SWA_HEADS, WINDOW, 2 * WINDOW), F32),
        name="swa_bias")(rel_bias, bucket)


def _swa_dbias(ds_sum, dsink_rows, bucket):
    def body(ds_ref, dsk_ref, bk_ref, drb_ref, dsink_ref):
        bk = bk_ref[...]
        r = lax.broadcasted_iota(jnp.int32, (REL_BUCKETS, SWA_HEADS), 0)
        c = lax.broadcasted_iota(jnp.int32, (REL_BUCKETS, SWA_HEADS), 1)
        out = jnp.zeros((REL_BUCKETS, SWA_HEADS), F32)
        for b in range(REL_BUCKETS):
            m = bk == b
            for h in range(SWA_HEADS):
                val = jnp.sum(jnp.where(m, ds_ref[h], 0.0))
                out = jnp.where((r == b) & (c == h), val, out)
        drb_ref[...] = out
        dsink_ref[...] = jnp.sum(dsk_ref[...], axis=0, keepdims=True)

    return pl.pallas_call(
        body, out_shape=[S((REL_BUCKETS, SWA_HEADS), F32), S((1, SWA_HEADS), F32)], name="swa_dbias",
    )(ds_sum, dsink_rows, bucket)


def _swa_band(cur_ref, prev_ref):
    kband = jnp.concatenate([prev_ref[:, 0:128], cur_ref[:, 1024:1152]], axis=0).astype(BF16)
    vband = jnp.concatenate([prev_ref[:, 128:256], cur_ref[:, 1152:1280]], axis=0).astype(BF16)
    return kband, vband


def _swa_valid(n):
    qi = lax.broadcasted_iota(jnp.int32, (WINDOW, 2 * WINDOW), 0)
    kj = lax.broadcasted_iota(jnp.int32, (WINDOW, 2 * WINDOW), 1)
    dist = qi + WINDOW - kj
    return (dist >= 0) & (dist < WINDOW) & (n * WINDOW + kj - WINDOW >= 0)


def _swa_probs(qh, kh, bias_h, sink, valid):
    s = _dot_nt(qh, kh) * (HEAD_DIM ** -0.5) + bias_h
    s = jnp.where(valid, s, NEG_INF)
    m = jnp.maximum(jnp.max(s, axis=-1, keepdims=True), sink)
    e = jnp.exp(s - m)
    es = jnp.exp(sink - m)
    inv = 1.0 / (jnp.sum(e, axis=-1, keepdims=True) + es)
    return e * inv, es * inv


def _swa_in_specs():
    return [_row_spec(WINDOW, 2304),
            pl.BlockSpec((WINDOW, 256), lambda n: (jnp.maximum(n - 1, 0), 4)),
            _full_spec((SWA_HEADS, WINDOW, 2 * WINDOW)),
            pl.BlockSpec(memory_space=pltpu.SMEM)]


def _swa_fwd(proj, bias, sinks):
    t = proj.shape[0]

    def body(cur_ref, prev_ref, bias_ref, sink_ref, o_ref):
        n = pl.program_id(0)
        kband, vband = _swa_band(cur_ref, prev_ref)
        valid = _swa_valid(n)
        outs = []
        for h in range(SWA_HEADS):
            j = h // SWA_GROUP
            qh = cur_ref[:, h * 64:(h + 1) * 64].astype(BF16)
            p, _ = _swa_probs(qh, kband[:, j * 64:(j + 1) * 64], bias_ref[h], sink_ref[0, h], valid)
            outs.append(_dot(p.astype(BF16), vband[:, j * 64:(j + 1) * 64]))
        o = jnp.concatenate(outs, axis=1)
        o_ref[...] = (o * _silu(cur_ref[:, 1280:2304])).astype(BF16)

    return pl.pallas_call(body, grid=(t // WINDOW,), in_specs=_swa_in_specs(), out_specs=_row_spec(WINDOW, 1024),
                          out_shape=S((t, 1024), BF16), name="swa_fwd",
                          compiler_params=_cp(("parallel",)))(proj, proj, bias, sinks)


def _swa_bwd(proj, dp, bias, sinks):
    t = proj.shape[0]
    nb = t // WINDOW

    def body(cur_ref, prev_ref, bias_ref, sink_ref, dp_ref, dqz_ref, dkv_ref, dss_ref, dsk_ref):
        n = pl.program_id(0)

        @pl.when(n == 0)
        def _():
            dkv_ref[...] = jnp.zeros_like(dkv_ref)
            dss_ref[...] = jnp.zeros_like(dss_ref)
            dsk_ref[...] = jnp.zeros_like(dsk_ref)

        kband, vband = _swa_band(cur_ref, prev_ref)
        valid = _swa_valid(n)
        z = cur_ref[:, 1280:2304]
        dpv = dp_ref[...].astype(F32)
        do = dpv * _silu(z)
        outs, dqs, dsinks = [], [], []
        dk = [jnp.zeros((2 * WINDOW, 64), F32) for _ in range(2)]
        dv = [jnp.zeros((2 * WINDOW, 64), F32) for _ in range(2)]
        for h in range(SWA_HEADS):
            j = h // SWA_GROUP
            qh = cur_ref[:, h * 64:(h + 1) * 64].astype(BF16)
            kh = kband[:, j * 64:(j + 1) * 64]
            vh = vband[:, j * 64:(j + 1) * 64]
            p, ps = _swa_probs(qh, kh, bias_ref[h], sink_ref[0, h], valid)
            pb = p.astype(BF16)
            outs.append(_dot(pb, vh))
            doh = do[:, h * 64:(h + 1) * 64].astype(BF16)
            dpr = _dot_nt(doh, vh)
            dlt = jnp.sum(p * dpr, axis=-1, keepdims=True)
            ds = p * (dpr - dlt)
            dsinks.append(-ps * dlt)
            dss_ref[h] += ds
            dsb = (ds * (HEAD_DIM ** -0.5)).astype(BF16)
            dqs.append(_dot(dsb, kh))
            dk[j] = dk[j] + _dot_tn(dsb, qh)
            dv[j] = dv[j] + _dot_tn(pb, doh)
        o = jnp.concatenate(outs, axis=1)
        dqz_ref[:, 0:1024] = jnp.concatenate(dqs, axis=1).astype(BF16)
        dqz_ref[:, 1024:2048] = (dpv * o * _silu_grad(z)).astype(BF16)
        dsk_ref[...] += jnp.concatenate(dsinks, axis=1)
        dband = jnp.concatenate([dk[0], dk[1], dv[0], dv[1]], axis=1)
        prow = pl.multiple_of(jnp.maximum(n - 1, 0) * WINDOW, WINDOW)
        dkv_ref[pl.ds(prow, WINDOW), :] += dband[0:WINDOW]
        crow = pl.multiple_of(n * WINDOW, WINDOW)
        dkv_ref[pl.ds(crow, WINDOW), :] += dband[WINDOW:]

    return pl.pallas_call(
        body, grid=(nb,), in_specs=_swa_in_specs() + [_row_spec(WINDOW, 1024)],
        out_specs=[_row_spec(WINDOW, 2048), _full_spec((t, 256)), _full_spec((SWA_HEADS, WINDOW, 2 * WINDOW)),
                   _full_spec((WINDOW, SWA_HEADS))],
        out_shape=[S((t, 2048), BF16), S((t, 256), F32), S((SWA_HEADS, WINDOW, 2 * WINDOW), F32),
                   S((WINDOW, SWA_HEADS), F32)],
        name="swa_bwd", compiler_params=_cp(("arbitrary",)))(proj, proj, bias, sinks, dp)


MLA_Z0 = MLA_Q_RANK + MLA_KV_RANK + 128
MLA_Z1 = MLA_Z0 + 1024
MLA_PROJ = MLA_Z1 + 128


def _rope_tables(t):
    inv = ROPE_BASE ** (-jnp.arange(0, MLA_ROPE, 2, dtype=F32) / MLA_ROPE)
    ang = jnp.arange(t, dtype=F32)[:, None] * inv[None, :]
    return jnp.tile(jnp.cos(ang), (1, MLA_HEADS)), jnp.tile(jnp.sin(ang), (1, MLA_HEADS))


def _mla_norms(proj, gq, gkv):
    t = proj.shape[0]
    tr = 256

    def body(p_ref, gq_ref, gkv_ref, q_ref, kv_ref):
        for lo, hi, g_ref, o_ref in ((0, 768, gq_ref, q_ref), (768, 1024, gkv_ref, kv_ref)):
            xv = p_ref[:, lo:hi]
            r = lax.rsqrt(jnp.mean(xv * xv, axis=-1, keepdims=True) + EPS)
            o_ref[...] = (xv * r * g_ref[...]).astype(BF16)

    return pl.pallas_call(
        body, grid=(t // tr,), in_specs=[_row_spec(tr, 1024), _full_spec((1, 768)), _full_spec((1, 256))],
        out_specs=[_row_spec(tr, 768), _row_spec(tr, 256)], out_shape=[S((t, 768), BF16), S((t, 256), BF16)],
        name="mla_norms", compiler_params=_cp(("parallel",)))(proj, gq, gkv)


def _mla_norms_bwd(proj, dcqn, dckvn, dkr, dz, gq, gkv):
    t = proj.shape[0]
    tr = 256

    def body(p_ref, dq_ref, dkv_ref, dkr_ref, dz_ref, gq_ref, gkv_ref, dproj_ref, dgq_ref, dgkv_ref):
        @pl.when(pl.program_id(0) == 0)
        def _():
            dgq_ref[...] = jnp.zeros_like(dgq_ref)
            dgkv_ref[...] = jnp.zeros_like(dgkv_ref)

        dx, dg = _rms_bwd_rows(dq_ref[...], p_ref[:, 0:768], gq_ref[...])
        dproj_ref[:, 0:768] = dx.astype(BF16)
        dgq_ref[...] += dg
        dx, dg = _rms_bwd_rows(dkv_ref[...], p_ref[:, 768:1024], gkv_ref[...])
        dproj_ref[:, 768:1024] = dx.astype(BF16)
        dgkv_ref[...] += dg
        dproj_ref[:, 1024:MLA_Z0] = dkr_ref[...].astype(BF16)
        dproj_ref[:, MLA_Z0:MLA_Z1] = dz_ref[...]
        dproj_ref[:, MLA_Z1:MLA_PROJ] = jnp.zeros((tr, MLA_PROJ - MLA_Z1), BF16)

    return pl.pallas_call(
        body, grid=(t // tr,),
        in_specs=[_row_spec(tr, 1024), _row_spec(tr, 768), _row_spec(tr, 256), _row_spec(tr, 128), _row_spec(tr, 1024),
                  _full_spec((1, 768)), _full_spec((1, 256))],
        out_specs=[_row_spec(tr, MLA_PROJ), _full_spec((1, 768)), _full_spec((1, 256))],
        out_shape=[S((t, MLA_PROJ), BF16), S((1, 768), F32), S((1, 256), F32)],
        name="mla_norms_bwd", compiler_params=_cp(("arbitrary",)))(proj, dcqn, dckvn, dkr, dz, gq, gkv)


def _mla_pack(qf, kvf, proj, cos, sin):
    t = qf.shape[0]
    tr = 256

    def body(q_ref, kn_ref, kr_ref, cos_ref, sin_ref, qh_ref, kh_ref):
        cs, sn = cos_ref[...], sin_ref[...]
        x1, x2 = q_ref[:, 1024:1280], q_ref[:, 1280:1536]
        r1 = x1 * cs - x2 * sn
        r2 = x2 * cs + x1 * sn
        c16, s16 = cs[:, 0:16], sn[:, 0:16]
        k1, k2 = kr_ref[:, 0:16], kr_ref[:, 16:32]
        kr1 = k1 * c16 - k2 * s16
        kr2 = k2 * c16 + k1 * s16
        zpad = jnp.zeros((tr, 32), F32)
        for h in range(MLA_HEADS):
            qh_ref[h] = jnp.concatenate([q_ref[:, h * 64:(h + 1) * 64], r1[:, h * 16:(h + 1) * 16],
                                         r2[:, h * 16:(h + 1) * 16], zpad], axis=1).astype(BF16)
            kh_ref[h] = jnp.concatenate([kn_ref[:, h * 64:(h + 1) * 64], kr1, kr2, zpad], axis=1).astype(BF16)

    hspec = pl.BlockSpec((MLA_HEADS, tr, 128), lambda i: (0, i, 0))
    return pl.pallas_call(
        body, grid=(t // tr,),
        in_specs=[_row_spec(tr, 1536), _row_spec(tr, 1024), pl.BlockSpec((tr, 128), lambda i: (i, 8)),
                  _row_spec(tr, 256), _row_spec(tr, 256)],
        out_specs=[hspec, hspec], out_shape=[S((MLA_HEADS, t, 128), BF16)] * 2,
        name="mla_pack", compiler_params=_cp(("parallel",)))(qf, kvf, proj, cos, sin)


def _mla_unpack(dqh, dkh, dv, cos, sin):
    t = dqh.shape[1]
    tr = 256

    def body(dqh_ref, dkh_ref, dv_ref, cos_ref, sin_ref, dq_ref, dkv_ref, dkr_ref):
        cs, sn = cos_ref[...], sin_ref[...]
        dqn = jnp.concatenate([dqh_ref[h, :, 0:64] for h in range(MLA_HEADS)], axis=1)
        dr1 = jnp.concatenate([dqh_ref[h, :, 64:80] for h in range(MLA_HEADS)], axis=1)
        dr2 = jnp.concatenate([dqh_ref[h, :, 80:96] for h in range(MLA_HEADS)], axis=1)
        dq_ref[:, 0:1024] = dqn.astype(BF16)
        dq_ref[:, 1024:1280] = (dr1 * cs + dr2 * sn).astype(BF16)
        dq_ref[:, 1280:1536] = (dr2 * cs - dr1 * sn).astype(BF16)
        dkv_ref[:, 0:1024] = jnp.concatenate([dkh_ref[h, :, 0:64] for h in range(MLA_HEADS)], axis=1).astype(BF16)
        dkv_ref[:, 1024:2048] = dv_ref[...].astype(BF16)
        d1 = dkh_ref[0, :, 64:80]
        d2 = dkh_ref[0, :, 80:96]
        for h in range(1, MLA_HEADS):
            d1 = d1 + dkh_ref[h, :, 64:80]
            d2 = d2 + dkh_ref[h, :, 80:96]
        c16, s16 = cs[:, 0:16], sn[:, 0:16]
        dkr_ref[...] = jnp.concatenate([d1 * c16 + d2 * s16, d2 * c16 - d1 * s16, jnp.zeros((tr, 96), F32)], axis=1)

    hspec = pl.BlockSpec((MLA_HEADS, tr, 128), lambda i: (0, i, 0))
    return pl.pallas_call(
        body, grid=(t // tr,), in_specs=[hspec, hspec, _row_spec(tr, 1024), _row_spec(tr, 256), _row_spec(tr, 256)],
        out_specs=[_row_spec(tr, 1536), _row_spec(tr, 2048), _row_spec(tr, 128)],
        out_shape=[S((t, 1536), BF16), S((t, 2048), BF16), S((t, 128), F32)],
        name="mla_unpack", compiler_params=_cp(("parallel",)))(dqh, dkh, dv, cos, sin)


def _mla_attn_fwd(qh, kh, kvf, proj):
    t = qh.shape[1]
    b = MLA_BLK

    def body(q_ref, k_ref, v_ref, z_ref, o_ref, p_ref, lse_ref):
        r = pl.program_id(1)
        row = lax.broadcasted_iota(jnp.int32, (b, b), 0)
        col = lax.broadcasted_iota(jnp.int32, (b, b), 1)
        qs = [q_ref[0], q_ref[1]]

        def block(c0, carry, masked):
            new = []
            for a in range(2):
                m, l, acc = carry[a]
                k = k_ref[a, pl.ds(c0, b), :]
                v = v_ref[pl.ds(c0, b), a * 64:(a + 1) * 64].astype(BF16)
                s = _dot_nt(qs[a], k) * MLA_SCALE
                if masked:
                    s = jnp.where(col <= row, s, NEG_INF)
                mn = jnp.maximum(m, jnp.max(s, axis=-1, keepdims=True))
                alpha = jnp.exp(m - mn)
                p = jnp.exp(s - mn)
                new.append((mn, alpha * l + jnp.sum(p, axis=-1, keepdims=True), alpha * acc + _dot(p.astype(BF16), v)))
            return tuple(new)

        init = (jnp.full((b, 1), NEG_INF, F32), jnp.zeros((b, 1), F32), jnp.zeros((b, 64), F32))
        carry = lax.fori_loop(0, r, lambda c, cr: block(pl.multiple_of(c * b, b), cr, False), (init, init))
        carry = block(pl.multiple_of(r * b, b), carry, True)
        outs = [acc / l for _, l, acc in carry]
        lses = [jnp.broadcast_to(m + jnp.log(l), (b, 64)) for m, l, _ in carry]
        o = jnp.concatenate(outs, axis=1)
        o_ref[...] = o
        p_ref[...] = (o * _silu(z_ref[...])).astype(BF16)
        lse_ref[0] = jnp.concatenate(lses, axis=1)

    return pl.pallas_call(
        body, grid=(MLA_HEADS // 2, t // b),
        in_specs=[pl.BlockSpec((2, b, 128), lambda i, r: (i, r, 0)), pl.BlockSpec((2, t, 128), lambda i, r: (i, 0, 0)),
                  pl.BlockSpec((t, 128), lambda i, r: (0, 8 + i)), pl.BlockSpec((b, 128), lambda i, r: (r, 9 + i))],
        out_specs=[pl.BlockSpec((b, 128), lambda i, r: (r, i)), pl.BlockSpec((b, 128), lambda i, r: (r, i)),
                   pl.BlockSpec((1, b, 128), lambda i, r: (i, r, 0))],
        out_shape=[S((t, 1024), F32), S((t, 1024), BF16), S((MLA_HEADS // 2, t, 128), F32)],
        name="mla_attn_fwd", compiler_params=_cp(("parallel", "parallel")))(qh, kh, kvf, proj)


def _mla_attn_bwd(qh, kh, kvf, proj, o, dpout, lse):
    t = qh.shape[1]
    b = MLA_BLK
    nb = t // b

    def body(q_ref, k_ref, v_ref, z_ref, o_ref, dp_ref, lse_ref, dq_ref, dk_ref, dv_ref, dz_ref):
        dk_ref[...] = jnp.zeros_like(dk_ref)
        dv_ref[...] = jnp.zeros_like(dv_ref)
        row = lax.broadcasted_iota(jnp.int32, (b, b), 0)
        col = lax.broadcasted_iota(jnp.int32, (b, b), 1)
        lanes = [slice(0, 64), slice(64, 128)]

        def rbody(r, carry):
            r0 = pl.multiple_of(r * b, b)
            rows = pl.ds(r0, b)
            qs, dobs, deltas, lses = [], [], [], []
            for a in range(2):
                zz = z_ref[rows, lanes[a]]
                oo = o_ref[rows, lanes[a]]
                dpo = dp_ref[rows, lanes[a]].astype(F32)
                do = dpo * _silu(zz)
                dz_ref[rows, lanes[a]] = (dpo * oo * _silu_grad(zz)).astype(BF16)
                qs.append(q_ref[a, rows, :])
                dobs.append(do.astype(BF16))
                deltas.append(jnp.sum(do * oo, axis=-1, keepdims=True))
                lses.append(lse_ref[0, rows, a * 64:a * 64 + 1])

            def block(c0, dqs, masked):
                cols = pl.ds(c0, b)
                out = []
                for a in range(2):
                    k = k_ref[a, cols, :]
                    v = v_ref[cols, lanes[a]].astype(BF16)
                    s = _dot_nt(qs[a], k) * MLA_SCALE
                    if masked:
                        s = jnp.where(col <= row, s, NEG_INF)
                    p = jnp.exp(s - lses[a])
                    ds = p * (_dot_nt(dobs[a], v) - deltas[a]) * MLA_SCALE
                    dsb = ds.astype(BF16)
                    dk_ref[a, cols, :] += _dot_tn(dsb, qs[a])
                    dv_ref[cols, lanes[a]] += _dot_tn(p.astype(BF16), dobs[a])
                    out.append(dqs[a] + _dot(dsb, k))
                return tuple(out)

            zero = jnp.zeros((b, 128), F32)
            dqs = lax.fori_loop(0, r, lambda c, d: block(pl.multiple_of(c * b, b), d, False), (zero, zero))
            dqs = block(r0, dqs, True)
            dq_ref[0, rows, :] = dqs[0]
            dq_ref[1, rows, :] = dqs[1]
            return carry

        lax.fori_loop(0, nb, rbody, 0)

    pair3 = pl.BlockSpec((2, t, 128), lambda i: (i, 0, 0))
    return pl.pallas_call(
        body, grid=(MLA_HEADS // 2,),
        in_specs=[pair3, pair3, pl.BlockSpec((t, 128), lambda i: (0, 8 + i)), pl.BlockSpec((t, 128), lambda i: (0, 9 + i)),
                  pl.BlockSpec((t, 128), lambda i: (0, i)), pl.BlockSpec((t, 128), lambda i: (0, i)),
                  pl.BlockSpec((1, t, 128), lambda i: (i, 0, 0))],
        out_specs=[pair3, pair3, pl.BlockSpec((t, 128), lambda i: (0, i)), pl.BlockSpec((t, 128), lambda i: (0, i))],
        out_shape=[S((MLA_HEADS, t, 128), F32), S((MLA_HEADS, t, 128), F32), S((t, 1024), F32), S((t, 1024), BF16)],
        name="mla_attn_bwd", compiler_params=_cp(("parallel",), VMEM_BIG))(qh, kh, kvf, proj, o, dpout, lse)


def _dot_nt_hi(a, b):
    return lax.dot_general(a, b, (((1,), (1,)), ((), ())), preferred_element_type=F32,
                           precision=lax.Precision.HIGHEST)


def _ssm_gen(lr, li, ldt, btr, bti, cr, ci):
    n = SSM_BLOCK
    dt = jnp.exp(ldt)
    x = lr * dt
    w = li * dt
    mag = jnp.exp(x)
    ab_re = mag * jnp.cos(w)
    ab_im = mag * jnp.sin(w)
    den = lr * lr + li * li
    nr = ab_re - 1.0
    f_re = (nr * lr + ab_im * li) / den
    f_im = (ab_im * lr - nr * li) / den
    bb_re = f_re * btr - f_im * bti
    bb_im = f_re * bti + f_im * btr
    inv_mag = jnp.exp(-x)
    inv_re, inv_im = inv_mag * jnp.cos(w), -inv_mag * jnp.sin(w)
    one, zero = jnp.ones_like(x), jnp.zeros_like(x)
    pos, neg = [(one, zero)], [(one, zero)]
    for _ in range(SSM_CHUNK):
        pr, pi = pos[-1]
        pos.append((pr * ab_re - pi * ab_im, pr * ab_im + pi * ab_re))
        pr, pi = neg[-1]
        neg.append((pr * inv_re - pi * inv_im, pr * inv_im + pi * inv_re))

    def stack(table, idx):
        return (jnp.concatenate([jnp.broadcast_to(table[idx(t)][0], (SSM_GROUP, SSM_STATE)) for t in range(SSM_CHUNK)], axis=0),
                jnp.concatenate([jnp.broadcast_to(table[idx(t)][1], (SSM_GROUP, SSM_STATE)) for t in range(SSM_CHUNK)], axis=0))

    def rep(v):
        return jnp.concatenate([v] * SSM_CHUNK, axis=0)

    ct_re, ct_im, bt_re, bt_im = rep(cr), rep(ci), rep(bb_re), rep(bb_im)
    p_re, p_im = stack(pos, lambda t: t)
    l_re = ct_re * p_re - ct_im * p_im
    l_im = ct_re * p_im + ct_im * p_re
    n_re, n_im = stack(neg, lambda t: t)
    r_re = bt_re * n_re - bt_im * n_im
    r_im = bt_re * n_im + bt_im * n_re
    rk = lax.broadcasted_iota(jnp.int32, (n, n), 0) // SSM_GROUP
    ct = lax.broadcasted_iota(jnp.int32, (n, n), 1) // SSM_GROUP
    mt = jnp.where(ct >= rk, _dot_nt_hi(r_re, l_re) - _dot_nt_hi(r_im, l_im), 0.0)
    e_re, e_im = stack(pos, lambda t: SSM_CHUNK - 1 - t)
    pin_re = bt_re * e_re - bt_im * e_im
    pin_im = bt_re * e_im + bt_im * e_re
    q_re, q_im = stack(pos, lambda t: t + 1)
    qt_re = ct_re * q_re - ct_im * q_im
    qt_im = -(ct_re * q_im + ct_im * q_re)
    ad_re, ad_im = pos[SSM_CHUNK]
    return mt, pin_re, pin_im, qt_re, qt_im, ad_re, ad_im


_SSM_GEN_OUT = [(SSM_BLOCK, SSM_BLOCK)] + [(SSM_BLOCK, SSM_STATE)] * 4 + [(1, SSM_STATE)] * 2
_SSM_GEN_IN = [(1, SSM_STATE)] * 3 + [(SSM_GROUP, SSM_STATE)] * 4


def _gspec(shape2):
    return pl.BlockSpec((1,) + shape2, lambda g: (g, 0, 0))


def _ssm_operators(params):
    def body(*refs):
        ins, outs = refs[:7], refs[7:]
        for o_ref, val in zip(outs, _ssm_gen(*[r[0] for r in ins])):
            o_ref[0] = val

    return pl.pallas_call(
        body, grid=(SSM_GROUPS,), in_specs=[_gspec(s) for s in _SSM_GEN_IN], out_specs=[_gspec(s) for s in _SSM_GEN_OUT],
        out_shape=[S((SSM_GROUPS,) + s, F32) for s in _SSM_GEN_OUT], name="ssm_operators",
        compiler_params=_cp(("parallel",)))(*params)


def _ssm_operators_bwd(params, cots):
    def body(*refs):
        ins, cts, outs = refs[:7], refs[7:14], refs[14:]
        _, vjp = jax.vjp(_ssm_gen, *[r[0] for r in ins])
        grads = list(vjp(tuple(r[0] for r in cts)))
        grads[2] = jnp.broadcast_to(jnp.sum(grads[2], axis=-1, keepdims=True), (1, SSM_STATE))
        for o_ref, val in zip(outs, grads):
            o_ref[0] = val

    return pl.pallas_call(
        body, grid=(SSM_GROUPS,), in_specs=[_gspec(s) for s in _SSM_GEN_IN] + [_gspec(s) for s in _SSM_GEN_OUT],
        out_specs=[_gspec(s) for s in _SSM_GEN_IN], out_shape=[S((SSM_GROUPS,) + s, F32) for s in _SSM_GEN_IN],
        name="ssm_operators_bwd", compiler_params=_cp(("parallel",)))(*params, *cots)


def _shift_rows(x, sh, row, up):
    n = x.shape[0]
    if up:
        return jnp.where(row < n - sh, pltpu.roll(x, n - sh, axis=0), 0.0)
    return jnp.where(row >= sh, pltpu.roll(x, sh, axis=0), 0.0)


def _carry_scan(sr, si, ar, ai, row, up):
    n = sr.shape[0]
    sh = 1
    while sh < n:
        tr, ti = _shift_rows(sr, sh, row, up), _shift_rows(si, sh, row, up)
        sr, si = sr + ar * tr - ai * ti, si + ar * ti + ai * tr
        ar, ai = ar * ar - ai * ai, 2.0 * ar * ai
        sh *= 2
    return sr, si


def _ssm_states(u, p_re, p_im, ar, ai, row):
    lre = _dot(u, p_re.astype(BF16))
    lim = _dot(u, p_im.astype(BF16))
    sr, si = _carry_scan(lre, lim, ar, ai, row, False)
    return _shift_rows(sr, 1, row, False), _shift_rows(si, 1, row, False)


SSM_GPB = 128 // SSM_GROUP


def _step_rows(ref, nc):
    return [ref[pl.ds(k, nc, stride=SSM_CHUNK), :].astype(BF16) for k in range(SSM_CHUNK)]


def _group_of(rows, j):
    return jnp.concatenate([r[:, j * SSM_GROUP:(j + 1) * SSM_GROUP] for r in rows], axis=1)


def _store_groups(o_ref, per_group, nc):
    for k in range(SSM_CHUNK):
        o_ref[pl.ds(k, nc, stride=SSM_CHUNK), :] = jnp.concatenate(
            [y[:, k * SSM_GROUP:(k + 1) * SSM_GROUP] for y in per_group], axis=1)


def _bspec(shape2):
    return pl.BlockSpec((SSM_GPB,) + shape2, lambda j: (j, 0, 0))


def _ssm_core_fwd(proj, ops):
    t = proj.shape[0]
    nc = t // SSM_CHUNK

    def body(u_ref, mt_ref, pr_ref, pi_ref, qr_ref, qi_ref, ar_ref, ai_ref, y_ref):
        rows = _step_rows(u_ref, nc)
        row = lax.broadcasted_iota(jnp.int32, (nc, SSM_STATE), 0)
        ys = []
        for j in range(SSM_GPB):
            u = _group_of(rows, j)
            s_re, s_im = _ssm_states(u, pr_ref[j], pi_ref[j], ar_ref[j], ai_ref[j], row)
            ys.append(_dot(u, mt_ref[j].astype(BF16)) + _dot_nt(s_re.astype(BF16), qr_ref[j].astype(BF16))
                      + _dot_nt(s_im.astype(BF16), qi_ref[j].astype(BF16)))
        _store_groups(y_ref, ys, nc)

    blk = pl.BlockSpec((t, 128), lambda j: (0, j))
    return pl.pallas_call(
        body, grid=(SSM_GROUPS // SSM_GPB,), in_specs=[blk] + [_bspec(s) for s in _SSM_GEN_OUT], out_specs=blk,
        out_shape=S((t, 1024), F32), name="ssm_core_fwd", compiler_params=_cp(("parallel",)))(proj, *ops)


def _ssm_core_bwd(proj, dy, ops):
    t = proj.shape[0]
    nc = t // SSM_CHUNK

    def body(u_ref, dy_ref, mt_ref, pr_ref, pi_ref, qr_ref, qi_ref, ar_ref, ai_ref,
             du_ref, dmt_ref, dpr_ref, dpi_ref, dqr_ref, dqi_ref, dar_ref, dai_ref):
        u_rows = _step_rows(u_ref, nc)
        dy_rows = _step_rows(dy_ref, nc)
        row = lax.broadcasted_iota(jnp.int32, (nc, SSM_STATE), 0)
        dus = []
        for j in range(SSM_GPB):
            u = _group_of(u_rows, j)
            dyj = _group_of(dy_rows, j)
            ar, ai = ar_ref[j], ai_ref[j]
            s_re, s_im = _ssm_states(u, pr_ref[j], pi_ref[j], ar, ai, row)
            g_re = _shift_rows(_dot(dyj, qr_ref[j].astype(BF16)), 1, row, True)
            g_im = _shift_rows(_dot(dyj, qi_ref[j].astype(BF16)), 1, row, True)
            l_re, l_im = _carry_scan(g_re, g_im, ar, -ai, row, True)
            lrb, lib = l_re.astype(BF16), l_im.astype(BF16)
            dus.append(_dot_nt(dyj, mt_ref[j].astype(BF16)) + _dot_nt(lrb, pr_ref[j].astype(BF16))
                       + _dot_nt(lib, pi_ref[j].astype(BF16)))
            dmt_ref[j] = _dot_tn(u, dyj)
            dpr_ref[j] = _dot_tn(u, lrb)
            dpi_ref[j] = _dot_tn(u, lib)
            dqr_ref[j] = _dot_tn(dyj, s_re.astype(BF16))
            dqi_ref[j] = _dot_tn(dyj, s_im.astype(BF16))
            dar_ref[j] = jnp.sum(l_re * s_re + l_im * s_im, axis=0, keepdims=True)
            dai_ref[j] = jnp.sum(l_im * s_re - l_re * s_im, axis=0, keepdims=True)
        _store_groups(du_ref, dus, nc)

    blk = pl.BlockSpec((t, 128), lambda j: (0, j))
    return pl.pallas_call(
        body, grid=(SSM_GROUPS // SSM_GPB,), in_specs=[blk, blk] + [_bspec(s) for s in _SSM_GEN_OUT],
        out_specs=[blk] + [_bspec(s) for s in _SSM_GEN_OUT],
        out_shape=[S((t, 1024), F32)] + [S((SSM_GROUPS,) + s, F32) for s in _SSM_GEN_OUT],
        name="ssm_core_bwd", compiler_params=_cp(("parallel",)))(proj, dy, *ops)


def _s5_act(y_core, proj, d_skip):
    t = y_core.shape[0]
    tr = 256

    def body(y_ref, u_ref, d_ref, o_ref):
        o_ref[...] = _gelu(y_ref[...] + d_ref[...] * u_ref[...])

    return pl.pallas_call(body, grid=(t // tr,), in_specs=[_row_spec(tr, 1024), _row_spec(tr, 1024), _full_spec((1, 1024))],
                          out_specs=_row_spec(tr, 1024), out_shape=S((t, 1024), F32), name="s5_act",
                          compiler_params=_cp(("parallel",)))(y_core, proj, d_skip)


def _s5_gate(yg, tg, b_glu, proj):
    t = yg.shape[0]
    tr = 256

    def body(y_ref, t_ref, b_ref, z_ref, o_ref):
        o_ref[...] = (y_ref[...] * jax.nn.sigmoid(t_ref[...] + b_ref[...]) * _silu(z_ref[...])).astype(BF16)

    return pl.pallas_call(
        body, grid=(t // tr,),
        in_specs=[_row_spec(tr, 1024), _row_spec(tr, 1024), _full_spec((1, 1024)), pl.BlockSpec((tr, 1024), lambda i: (i, 1))],
        out_specs=_row_spec(tr, 1024), out_shape=S((t, 1024), BF16), name="s5_gate",
        compiler_params=_cp(("parallel",)))(yg, tg, b_glu, proj)


def _s5_gate_bwd(dp, yg, tg, b_glu, proj):
    t = yg.shape[0]
    tr = 256

    def body(dp_ref, y_ref, t_ref, b_ref, z_ref, dt_ref, dy_ref, dz_ref, db_ref):
        @pl.when(pl.program_id(0) == 0)
        def _():
            db_ref[...] = jnp.zeros_like(db_ref)

        z = z_ref[...]
        yv = y_ref[...]
        sg = jax.nn.sigmoid(t_ref[...] + b_ref[...])
        dpv = dp_ref[...].astype(F32)
        do = dpv * _silu(z)
        dz_ref[...] = (dpv * yv * sg * _silu_grad(z)).astype(BF16)
        dy_ref[...] = do * sg
        dtv = do * yv * sg * (1.0 - sg)
        dt_ref[...] = dtv.astype(BF16)
        db_ref[...] += jnp.sum(dtv, axis=0, keepdims=True)

    return pl.pallas_call(
        body, grid=(t // tr,),
        in_specs=[_row_spec(tr, 1024), _row_spec(tr, 1024), _row_spec(tr, 1024), _full_spec((1, 1024)),
                  pl.BlockSpec((tr, 1024), lambda i: (i, 1))],
        out_specs=[_row_spec(tr, 1024), _row_spec(tr, 1024), _row_spec(tr, 1024), _full_spec((1, 1024))],
        out_shape=[S((t, 1024), BF16), S((t, 1024), F32), S((t, 1024), BF16), S((1, 1024), F32)],
        name="s5_gate_bwd", compiler_params=_cp(("arbitrary",)))(dp, yg, tg, b_glu, proj)


def _s5_act_bwd(dyg1, dyg2, y_core, proj, d_skip):
    t = y_core.shape[0]
    tr = 256

    def body(a_ref, b_ref, y_ref, u_ref, d_ref, dy_ref, du_ref, dd_ref):
        @pl.when(pl.program_id(0) == 0)
        def _():
            dd_ref[...] = jnp.zeros_like(dd_ref)

        u = u_ref[...]
        dv = (a_ref[...] + b_ref[...]) * _gelu_grad(y_ref[...] + d_ref[...] * u)
        dy_ref[...] = dv
        du_ref[...] = dv * d_ref[...]
        dd_ref[...] += jnp.sum(dv * u, axis=0, keepdims=True)

    return pl.pallas_call(
        body, grid=(t // tr,),
        in_specs=[_row_spec(tr, 1024)] * 4 + [_full_spec((1, 1024))],
        out_specs=[_row_spec(tr, 1024), _row_spec(tr, 1024), _full_spec((1, 1024))],
        out_shape=[S((t, 1024), F32), S((t, 1024), F32), S((1, 1024), F32)],
        name="s5_act_bwd", compiler_params=_cp(("arbitrary",)))(dyg1, dyg2, y_core, proj, d_skip)


def _s5_dproj(du_skip, du_core, dz):
    t = du_skip.shape[0]
    tr = 256

    def body(a_ref, b_ref, z_ref, o_ref):
        o_ref[:, 0:1024] = (a_ref[...] + b_ref[...]).astype(BF16)
        o_ref[:, 1024:2048] = z_ref[...]

    return pl.pallas_call(body, grid=(t // tr,), in_specs=[_row_spec(tr, 1024)] * 3, out_specs=_row_spec(tr, 2048),
                          out_shape=S((t, 2048), BF16), name="s5_dproj",
                          compiler_params=_cp(("parallel",)))(du_skip, du_core, dz)


def _s5_params(lam_re, lam_im, log_dt, b_re, b_im, c_re, c_im):
    g = SSM_GROUPS
    return (lam_re.reshape(g, 1, SSM_STATE), lam_im.reshape(g, 1, SSM_STATE),
            jnp.broadcast_to(log_dt.reshape(g, 1, 1), (g, 1, SSM_STATE)),
            b_re.transpose(0, 2, 1), b_im.transpose(0, 2, 1), c_re, c_im)


def _s5_fwd(proj, params, d_skip, w_glu, b_glu):
    ops = _ssm_operators(params)
    y_core = _ssm_core_fwd(proj, ops)
    yg = _s5_act(y_core, proj, d_skip)
    tg = _mm(yg, w_glu, name="s5_glu")
    return _s5_gate(yg, tg, b_glu, proj), (ops, y_core, yg, tg)


def _s5_bwd(proj, res, dp, params, d_skip, w_glu, b_glu):
    ops, y_core, yg, tg = res
    dtg, dyg1, dz, db_glu = _s5_gate_bwd(dp, yg, tg, b_glu, proj)
    dyg2 = _mm(dtg, w_glu, tb=True, name="s5_glu_dx")
    dw_glu = _mm(yg, dtg, ta=True, out_dtype=BF16, name="s5_glu_dw")
    dy_core, du_skip, dd = _s5_act_bwd(dyg1, dyg2, y_core, proj, d_skip)
    outs = _ssm_core_bwd(proj, dy_core, ops)
    dproj = _s5_dproj(du_skip, outs[0], dz)
    glr, gli, gdt, gbtr, gbti, gcr, gci = _ssm_operators_bwd(params, outs[1:])
    g = SSM_GROUPS
    grads = (glr.reshape(g, SSM_STATE), gli.reshape(g, SSM_STATE), gdt[:, 0, 0].reshape(1, g),
             gbtr.transpose(0, 2, 1), gbti.transpose(0, 2, 1), gcr, gci)
    return dproj, grads, dd, dw_glu, db_glu


def _perm_uq(w):
    w3 = w.reshape(w.shape[0], MLA_HEADS, MLA_NOPE + MLA_ROPE)
    return jnp.concatenate([w3[:, :, :64].reshape(-1, 1024), w3[:, :, 64:80].reshape(-1, 256),
                            w3[:, :, 80:96].reshape(-1, 256)], axis=1)


def _unperm_uq(w):
    r = w.shape[0]
    return jnp.concatenate([w[:, :1024].reshape(r, 16, 64), w[:, 1024:1280].reshape(r, 16, 16),
                            w[:, 1280:1536].reshape(r, 16, 16)], axis=2).reshape(r, 1536)


def _perm_ukv(w):
    w3 = w.reshape(w.shape[0], MLA_HEADS, 128)
    return jnp.concatenate([w3[:, :, :64].reshape(-1, 1024), w3[:, :, 64:].reshape(-1, 1024)], axis=1)


def _unperm_ukv(w):
    r = w.shape[0]
    return jnp.concatenate([w[:, :1024].reshape(r, 16, 64), w[:, 1024:].reshape(r, 16, 64)], axis=2).reshape(r, 2048)


def _mla_fwd(proj, gq, gkv, wuq, wukv, cos, sin):
    cqn, ckvn = _mla_norms(proj, gq, gkv)
    qf = _mm(cqn, wuq, name="mla_uq")
    kvf = _mm(ckvn, wukv, name="mla_ukv")
    qh, kh = _mla_pack(qf, kvf, proj, cos, sin)
    o, pout, lse = _mla_attn_fwd(qh, kh, kvf, proj)
    return pout, (cqn, ckvn, kvf, qh, kh, o, lse)


def _mla_bwd(proj, res, dpout, gq, gkv, wuq, wukv, cos, sin):
    cqn, ckvn, kvf, qh, kh, o, lse = res
    dqh, dkh, dv, dz = _mla_attn_bwd(qh, kh, kvf, proj, o, dpout, lse)
    dqf, dkvf, dkr = _mla_unpack(dqh, dkh, dv, cos, sin)
    dcqn = _mm(dqf, wuq, tb=True, name="mla_uq_dx")
    dwuq = _mm(cqn, dqf, ta=True, out_dtype=BF16, name="mla_uq_dw")
    dckvn = _mm(dkvf, wukv, tb=True, name="mla_ukv_dx")
    dwukv = _mm(ckvn, dkvf, ta=True, out_dtype=BF16, name="mla_ukv_dw")
    dproj, dgq, dgkv = _mla_norms_bwd(proj, dcqn, dckvn, dkr, dz, gq, gkv)
    return dproj, dgq, dgkv, dwuq, dwukv


_ANY = pl.BlockSpec(memory_space=pl.ANY)


def _chip_peers():
    x, y, c = lax.axis_index("x"), lax.axis_index("y"), lax.axis_index("c")
    return 2 * x + y, c, [(1 - x, y), (x, 1 - y), (1 - x, 1 - y)]


def _exchange_chips(src_of, dst_of, send_sems, recv_sems, loc_sem):
    me, c, peers = _chip_peers()
    loc = pltpu.make_async_copy(src_of(me), dst_of(me), loc_sem)
    loc.start()

    def copy(k, px, py, src_chip, dst_chip):
        return pltpu.make_async_remote_copy(src_ref=src_of(src_chip), dst_ref=dst_of(dst_chip), send_sem=send_sems.at[k],
                                            recv_sem=recv_sems.at[k], device_id=(px, py, c), device_id_type=MESH)

    sends = [copy(k, px, py, 2 * px + py, me) for k, (px, py) in enumerate(peers)]
    for cp in sends:
        cp.start()
    for k, (px, py) in enumerate(peers):
        copy(k, px, py, me, 2 * px + py).wait_recv()
    for cp in sends:
        cp.wait_send()
    loc.wait()


_EXCH_SCRATCH = [pltpu.SemaphoreType.DMA((3,)), pltpu.SemaphoreType.DMA((3,)), pltpu.SemaphoreType.DMA(())]


def _all_gather_chips(xs, name):
    def body(x_ref, o_ref, send_sems, recv_sems, loc_sem):
        _exchange_chips(lambda s: x_ref, lambda s: o_ref.at[s], send_sems, recv_sems, loc_sem)

    return pl.pallas_call(body, in_specs=[_ANY], out_specs=_ANY, out_shape=S((N_CHIPS,) + xs.shape, xs.dtype),
                          scratch_shapes=_EXCH_SCRATCH, name=name)(xs)


def _sibling_swap(ps, name):
    n = len(ps)

    def body(*refs):
        p_refs, q_refs, send_sems, recv_sems = refs[:n], refs[n:2 * n], refs[2 * n], refs[2 * n + 1]
        x, y, c = lax.axis_index("x"), lax.axis_index("y"), lax.axis_index("c")
        cps = [pltpu.make_async_remote_copy(src_ref=p_refs[i], dst_ref=q_refs[i], send_sem=send_sems.at[i],
                                            recv_sem=recv_sems.at[i], device_id=(x, y, 1 - c), device_id_type=MESH)
               for i in range(n)]
        for cp in cps:
            cp.start()
        for cp in cps:
            cp.wait()

    return pl.pallas_call(body, in_specs=[_ANY] * n, out_specs=[_ANY] * n, out_shape=[S(p.shape, p.dtype) for p in ps],
                          scratch_shapes=[pltpu.SemaphoreType.DMA((n,)), pltpu.SemaphoreType.DMA((n,))], name=name)(*ps)


def _quarter_spec(tr, r, c, dim):
    if dim is None:
        return pl.BlockSpec((1, tr, c), lambda i, me: (me[0], i, 0))
    if dim == 0:
        return pl.BlockSpec((tr, c), lambda i, me: (me[0] * (r // tr) + i, 0))
    return pl.BlockSpec((tr, c), lambda i, me: (i, me[0]))


def _whole_shape(r, c, dim):
    return (N_CHIPS, r, c) if dim is None else ((N_CHIPS * r, c) if dim == 0 else (r, N_CHIPS * c))


def _row_tile(r):
    return 256 if r % 256 == 0 else (128 if r % 128 == 0 else r)


def _cast_place(me, x, dim, dtype, name):
    r, c = x.shape
    tr = _row_tile(r)

    def body(me_ref, x_ref, o_ref):
        o_ref[...] = x_ref[...].astype(dtype).reshape(o_ref.shape)

    return pl.pallas_call(
        body, out_shape=S(_whole_shape(r, c, dim), dtype), name=name,
        grid_spec=pltpu.PrefetchScalarGridSpec(num_scalar_prefetch=1, grid=(r // tr,),
                                               in_specs=[pl.BlockSpec((tr, c), lambda i, me: (i, 0))],
                                               out_specs=_quarter_spec(tr, r, c, dim)),
        compiler_params=_cp(("parallel",)))(me, x)


def _sum_parts(me, g, land, dim, name):
    _, r, c = land.shape
    tr = _row_tile(r)

    def body(me_ref, g_ref, l_ref, o_ref):
        own = g_ref[...].astype(F32).reshape(tr, c)
        o_ref[...] = ((own + l_ref[0].astype(F32)) + l_ref[1].astype(F32)) + l_ref[2].astype(F32)

    return pl.pallas_call(
        body, out_shape=S((r, c), F32), name=name,
        grid_spec=pltpu.PrefetchScalarGridSpec(
            num_scalar_prefetch=1, grid=(r // tr,),
            in_specs=[_quarter_spec(tr, r, c, dim), pl.BlockSpec((3, tr, c), lambda i, me: (0, i, 0))],
            out_specs=pl.BlockSpec((tr, c), lambda i, me: (i, 0))),
        compiler_params=_cp(("parallel",)))(me, g, land)


_HBM = pl.BlockSpec(memory_space=pltpu.HBM)
_SEM = pl.BlockSpec(memory_space=pltpu.SEMAPHORE)
_EFFECT = pltpu.SideEffectType.DATAFLOW_SIDE_EFFECTING


def _quarter(ref, s, dim):
    n = ref.shape[dim] // N_CHIPS
    start = pl.multiple_of(s * n, n)
    return ref.at[pl.ds(start, n), :] if dim == 0 else ref.at[:, pl.ds(start, n)]


def _part(ref, s, dim):
    return ref.at[s] if dim is None else _quarter(ref, s, dim)


def _gather_plan(i, dim):
    return dict(src=i, dst=i, send_src=lambda ref, me, peer, k: _part(ref, me, dim),
                send_dst=lambda ref, me, peer, k: _part(ref, me, dim), recv_dst=lambda ref, me, peer, k: _part(ref, peer, dim))


def _scatter_plan(i, j, dim):
    return dict(src=i, dst=j, send_src=lambda ref, me, peer, k: _part(ref, peer, dim),
                send_dst=lambda ref, me, peer, k: ref.at[k], recv_dst=lambda ref, me, peer, k: ref.at[k])


def _hbm(a):
    return pltpu.with_memory_space_constraint(a, pltpu.HBM)


def _remote_copies(refs, send_sems, recv_sems, plans):
    me, c, peers = _chip_peers()
    sends, recvs = [], []
    for i, p in enumerate(plans):
        for k, (px, py) in enumerate(peers):
            peer = 2 * px + py
            sems = dict(send_sem=send_sems.at[3 * i + k], recv_sem=recv_sems.at[3 * i + k], device_id=(px, py, c),
                        device_id_type=MESH)
            src = p['send_src'](refs[p['src']], me, peer, k)
            sends.append(pltpu.make_async_remote_copy(src_ref=src, dst_ref=p['send_dst'](refs[p['dst']], me, peer, k), **sems))
            recvs.append(pltpu.make_async_remote_copy(src_ref=src, dst_ref=p['recv_dst'](refs[p['dst']], me, peer, k), **sems))
    return sends, recvs


def _exchange_start(arrays, plans, name):
    n, m = len(arrays), len(plans)

    def body(*refs):
        send_sems, recv_sems, token = refs[n], refs[n + 1], refs[-1]
        sends, _ = _remote_copies(refs[:n], send_sems, recv_sems, plans)
        for cp in sends:
            cp.start()
        token[...] = jnp.zeros_like(token)

    outs = pl.pallas_call(
        body, name=name,
        out_shape=(pltpu.SemaphoreType.DMA((3 * m,)), pltpu.SemaphoreType.DMA((3 * m,)),
                   *[pltpu.HBM(a.shape, a.dtype) for a in arrays], S((8, 128), F32)),
        in_specs=[_HBM] * n, out_specs=(_SEM, _SEM, *[_HBM] * n, pl.BlockSpec(memory_space=pltpu.VMEM)),
        input_output_aliases={i: 2 + i for i in range(n)},
        compiler_params=pltpu.CompilerParams(has_side_effects=_EFFECT))(*[_hbm(a) for a in arrays])
    return outs[0], outs[1], outs[2:2 + n], outs[-1]


def _exchange_wait(started, plans, after, name):
    send_sems, recv_sems, arrays, _ = started
    n = len(arrays)

    def body(*refs):
        sends, recvs = _remote_copies(refs[:n], refs[n], refs[n + 1], plans)
        for cp in sends:
            cp.wait_send()
        for cp in recvs:
            cp.wait_recv()

    return pl.pallas_call(
        body, name=name, out_shape=tuple(pltpu.HBM(a.shape, a.dtype) for a in arrays),
        in_specs=[_HBM] * n + [_SEM, _SEM, _ANY], out_specs=tuple([_HBM] * n),
        input_output_aliases={i: i for i in range(n)},
        compiler_params=pltpu.CompilerParams(has_side_effects=_EFFECT))(*arrays, send_sems, recv_sems, after)


def _add2(p, q):
    r, c = p.shape
    tr = 128 if r % 128 == 0 else r

    def body(p_ref, q_ref, o_ref):
        o_ref[...] = p_ref[...] + q_ref[...]

    return pl.pallas_call(body, grid=(r // tr,), in_specs=[_row_spec(tr, c), _row_spec(tr, c)], out_specs=_row_spec(tr, c),
                          out_shape=S((r, c), F32), name="add_cores", compiler_params=_cp(("parallel",)))(p, q)


def _adamw(w, g, m, v, name, g2=None):
    r, c = w.shape
    tr = 256 if r % 256 == 0 else (128 if r % 128 == 0 else r)
    c1 = 1.0 / (1.0 - ADAM_B1 ** ADAM_STEP)
    c2 = 1.0 / (1.0 - ADAM_B2 ** ADAM_STEP)
    gs = [g] if g2 is None else [g, g2]

    def body(*refs):
        w_ref, m_ref, v_ref = refs[0], refs[1], refs[2]
        g_refs = refs[3:3 + len(gs)]
        go_ref, d_ref, nm_ref, nv_ref = refs[3 + len(gs):]
        gv = g_refs[0][...]
        if len(gs) == 2:
            gv = gv + g_refs[1][...]
        go_ref[...] = gv
        nm = ADAM_B1 * m_ref[...] + (1.0 - ADAM_B1) * gv
        nv = ADAM_B2 * v_ref[...] + (1.0 - ADAM_B2) * (gv * gv)
        nm_ref[...] = nm
        nv_ref[...] = nv
        d_ref[...] = -ADAM_LR * ((nm * c1) / (jnp.sqrt(nv * c2) + ADAM_EPS) + ADAM_WD * w_ref[...])

    return pl.pallas_call(body, grid=(r // tr,), in_specs=[_row_spec(tr, c)] * (3 + len(gs)),
                          out_specs=[_row_spec(tr, c)] * 4, out_shape=[S((r, c), F32)] * 4, name=name,
                          compiler_params=_cp(("parallel",)))(w, m, v, *gs)


_WEIGHTS = ['pre_norm', 'post_norm', 'rel_bias', 'a_w_in', 'a_lam_re', 'a_lam_im', 'a_log_dt', 'a_b_re', 'a_b_im',
            'a_c_re', 'a_c_im', 'a_d', 'a_w_glu', 'a_b_glu', 'a_w_out', 'b_w_in', 'b_sinks', 'b_w_out', 'c_w_in',
            'c_q_norm', 'c_kv_norm', 'c_w_uq', 'c_w_ukv', 'c_w_out', 'd_w_in', 'd_ln_g', 'd_ln_b', 'd_w_s', 'd_b_s',
            'd_w_out']
_BIG = [('a_w_in', 1024, 2048, 1), ('a_w_glu', 1024, 1024, 0), ('a_w_out', 1024, 1024, 0), ('b_w_in', 1024, 2304, 1),
        ('b_w_out', 1024, 1024, 0), ('c_w_in', 1024, 2080, 1), ('c_w_uq', 768, 1536, 1), ('c_w_ukv', 256, 2048, 1),
        ('c_w_out', 1024, 1024, 0), ('d_w_in', 1024, 3072, 1), ('d_w_out', 1024, 1024, 0)]
_SHARDED_VECS = [('c_q_norm', 768), ('c_kv_norm', 256), ('d_ln_g', 1024), ('d_ln_b', 1024)]
_REPLICATED = [n for n in _WEIGHTS if n not in [b[0] for b in _BIG] and n not in [s[0] for s in _SHARDED_VECS]]
_LANES = 1024
_SMALL_Q_ROWS = 136


def kernel(x, pre_norm, post_norm, rel_bias, a_w_in, a_lam_re, a_lam_im, a_log_dt, a_b_re, a_b_im, a_c_re, a_c_im, a_d, a_w_glu, a_b_glu, a_w_out, b_w_in, b_sinks, b_w_out, c_w_in, c_q_norm, c_kv_norm, c_w_uq, c_w_ukv, c_w_out, d_w_in, d_ln_g, d_ln_b, d_w_s, d_b_s, d_w_out, loss_target, m_pre_norm, m_post_norm, m_rel_bias, m_a_w_in, m_a_lam_re, m_a_lam_im, m_a_log_dt, m_a_b_re, m_a_b_im, m_a_c_re, m_a_c_im, m_a_d, m_a_w_glu, m_a_b_glu, m_a_w_out, m_b_w_in, m_b_sinks, m_b_w_out, m_c_w_in, m_c_q_norm, m_c_kv_norm, m_c_w_uq, m_c_w_ukv, m_c_w_out, m_d_w_in, m_d_ln_g, m_d_ln_b, m_d_w_s, m_d_b_s, m_d_w_out, v_pre_norm, v_post_norm, v_rel_bias, v_a_w_in, v_a_lam_re, v_a_lam_im, v_a_log_dt, v_a_b_re, v_a_b_im, v_a_c_re, v_a_c_im, v_a_d, v_a_w_glu, v_a_b_glu, v_a_w_out, v_b_w_in, v_b_sinks, v_b_w_out, v_c_w_in, v_c_q_norm, v_c_kv_norm, v_c_w_uq, v_c_w_ukv, v_c_w_out, v_d_w_in, v_d_ln_g, v_d_ln_b, v_d_w_s, v_d_b_s, v_d_w_out):
    arg = dict(locals())
    me = 2 * lax.axis_index("x") + lax.axis_index("y")
    xin = x[0]
    t = xin.shape[0]

    big = {n: (r, c, dim) for n, r, c, dim in _BIG}

    me1 = me.reshape(1).astype(jnp.int32)

    def gather_start(names, tag):
        wholes, plans = [], []
        for i, n in enumerate(names):
            if n in big:
                r, c, dim = big[n]
                src, dtype = arg[n][0], BF16
                if dim == 1 and (c // N_CHIPS) % 128:
                    dim = None
            else:
                src, dim, dtype = arg[n], None, F32
            wholes.append(_cast_place(me1, src, dim, dtype, "place_" + n))
            plans.append(_gather_plan(i, dim))
        return _exchange_start(wholes, plans, "gather_start_" + tag), plans

    def gather_wait(names, started, after, tag):
        st, plans = started
        out = {}
        for n, a in zip(names, _exchange_wait(st, plans, after, "gather_wait_" + tag)):
            if n in big:
                r, c, dim = big[n]
                out[n] = a if a.ndim == 2 else a.transpose(1, 0, 2).reshape(r, c)
            else:
                out[n] = a.reshape(1, -1)
        return out

    groups = {'a1': ['a_w_in'], 'a2': ['a_w_glu', 'a_w_out'], 'b': ['b_w_in', 'b_w_out'],
              'c': ['c_w_in', 'c_w_uq', 'c_w_ukv', 'c_w_out', 'c_q_norm', 'c_kv_norm'],
              'd': ['d_w_in', 'd_w_out', 'd_ln_g', 'd_ln_b']}
    started = {tag: gather_start(names, tag) for tag, names in groups.items()}
    tokens = [st[0][3] for st in started.values()]
    wt = gather_wait(groups['a1'], started['a1'], xin, 'a1')
    cos, sin = _rope_tables(t)
    bucket = jnp.asarray(_t5_bucket_table())
    s5p = _s5_params(a_lam_re[0], a_lam_im[0], a_log_dt[0], a_b_re[0], a_b_im[0], a_c_re[0], a_c_im[0])
    bst = d_b_s[0].T

    x0 = xin
    h0 = _prenorm(x0, pre_norm[0:1], "pre0", after=tokens)
    proj0 = _mm(h0, wt['a_w_in'], name="a_in")
    wt.update(gather_wait(groups['a2'], started['a2'], proj0, 'a2'))
    p0, res0 = _s5_fwd(proj0, s5p, a_d, wt['a_w_glu'], a_b_glu)
    y0 = _mm(p0, wt['a_w_out'], name="a_out")
    x1 = _postnorm(x0, y0, post_norm[0:1], "post0")

    wt.update(gather_wait(groups['b'], started['b'], x1, 'b'))
    h1 = _prenorm(x1, pre_norm[1:2], "pre1")
    proj1 = _mm(h1, wt['b_w_in'], name="b_in")
    bias = _swa_bias(rel_bias, bucket)
    p1 = _swa_fwd(proj1, bias, b_sinks)
    y1 = _mm(p1, wt['b_w_out'], name="b_out")
    x2 = _postnorm(x1, y1, post_norm[1:2], "post1")

    wt.update(gather_wait(groups['c'], started['c'], x2, 'c'))
    w_c_in = jnp.concatenate([wt['c_w_in'][:, :1056], jnp.zeros((1024, 96), BF16), wt['c_w_in'][:, 1056:],
                              jnp.zeros((1024, MLA_PROJ - MLA_Z1), BF16)], axis=1)
    w_uq = _perm_uq(wt['c_w_uq'])
    w_ukv = _perm_ukv(wt['c_w_ukv'])
    h2 = _prenorm(x2, pre_norm[2:3], "pre2")
    proj2 = _mm(h2, w_c_in, name="c_in")
    p2, res2 = _mla_fwd(proj2, wt['c_q_norm'], wt['c_kv_norm'], w_uq, w_ukv, cos, sin)
    y2 = _mm(p2, wt['c_w_out'], name="c_out")
    x3 = _postnorm(x2, y2, post_norm[2:3], "post2")

    wt.update(gather_wait(groups['d'], started['d'], x3, 'd'))
    vec = wt
    h3 = _prenorm(x3, pre_norm[3:4], "pre3")
    proj3 = _mm(h3, wt['d_w_in'], name="d_in")
    p3 = _sgu_fwd(proj3, vec['d_ln_g'], vec['d_ln_b'], d_w_s[0], bst)
    y3 = _mm(p3, wt['d_w_out'], name="d_out")
    x4 = _postnorm(x3, y3, post_norm[3:4], "post3")

    loss_part, dx = _loss_head(x4, loss_target[0])
    loss = lax.psum(loss_part[0, 0], ("x", "y", "c"))

    gr = {}
    d_pre, d_post = [None] * 4, [None] * 4
    scattered = []

    def scatter_start(items, tag):
        srcs = [a for _, a, _ in items]
        n = len(items)
        plans = [_scatter_plan(i, n + i, dim) for i, (_, _, dim) in enumerate(items)]
        lands = []
        for _, a, dim in items:
            part = a.shape[1:] if dim is None else tuple(s // N_CHIPS if i == dim else s for i, s in enumerate(a.shape))
            lands.append(lax.empty((3,) + part, a.dtype))
        st = _exchange_start(srcs + lands, plans, "scatter_start_" + tag)
        scattered.append((st, plans, items, tag))
        return [st[3]]

    def stacked(g):
        r, c = g.shape
        return g.reshape(r, N_CHIPS, c // N_CHIPS).transpose(1, 0, 2)

    dy, d_post[3] = _postnorm_bwd(dx, y3, post_norm[3:4], "post3_bwd")
    dp = _mm(dy, wt['d_w_out'], tb=True, name="d_out_dx")
    g_out = _mm(p3, dy, ta=True, out_dtype=BF16, name="d_out_dw")
    dproj, gr['d_w_s'], dbst, gr['d_ln_g'], gr['d_ln_b'] = _sgu_bwd(proj3, dp, vec['d_ln_g'], vec['d_ln_b'], d_w_s[0], bst)
    gr['d_b_s'] = dbst.T
    dh = _mm(dproj, wt['d_w_in'], tb=True, name="d_in_dx")
    g_in = _mm(h3, dproj, ta=True, out_dtype=BF16, name="d_in_dw")
    tok = scatter_start([('d_w_in', g_in, 1), ('d_w_out', g_out, 0)], 'd')
    dx, d_pre[3] = _prenorm_bwd(dx, dh, x3, pre_norm[3:4], "pre3_bwd")

    dy, d_post[2] = _postnorm_bwd(dx, y2, post_norm[2:3], "post2_bwd", after=tok)
    dp = _mm(dy, wt['c_w_out'], tb=True, name="c_out_dx")
    g_out = _mm(p2, dy, ta=True, out_dtype=BF16, name="c_out_dw")
    dproj, gr['c_q_norm'], gr['c_kv_norm'], dwuq, dwukv = _mla_bwd(proj2, res2, dp, wt['c_q_norm'], wt['c_kv_norm'],
                                                                  w_uq, w_ukv, cos, sin)
    dh = _mm(dproj, w_c_in, tb=True, name="c_in_dx")
    dwc = _mm(h2, dproj, ta=True, out_dtype=BF16, name="c_in_dw")
    g_in = stacked(jnp.concatenate([dwc[:, :1056], dwc[:, MLA_Z0:MLA_Z1]], axis=1))
    tok = scatter_start([('c_w_in', g_in, None), ('c_w_uq', _unperm_uq(dwuq), 1), ('c_w_ukv', _unperm_ukv(dwukv), 1),
                         ('c_w_out', g_out, 0)], 'c')
    dx, d_pre[2] = _prenorm_bwd(dx, dh, x2, pre_norm[2:3], "pre2_bwd")

    dy, d_post[1] = _postnorm_bwd(dx, y1, post_norm[1:2], "post1_bwd", after=tok)
    dp = _mm(dy, wt['b_w_out'], tb=True, name="b_out_dx")
    g_out = _mm(p1, dy, ta=True, out_dtype=BF16, name="b_out_dw")
    dqz, dkv, dss, dsk = _swa_bwd(proj1, dp, bias, b_sinks)
    gr['rel_bias'], gr['b_sinks'] = _swa_dbias(dss, dsk, bucket)
    dproj = jnp.concatenate([dqz[:, :1024], dkv.astype(BF16), dqz[:, 1024:]], axis=1)
    dh = _mm(dproj, wt['b_w_in'], tb=True, name="b_in_dx")
    g_in = stacked(_mm(h1, dproj, ta=True, out_dtype=BF16, name="b_in_dw"))
    tok = scatter_start([('b_w_in', g_in, None), ('b_w_out', g_out, 0)], 'b')
    dx, d_pre[1] = _prenorm_bwd(dx, dh, x1, pre_norm[1:2], "pre1_bwd")

    dy, d_post[0] = _postnorm_bwd(dx, y0, post_norm[0:1], "post0_bwd", after=tok)
    dp = _mm(dy, wt['a_w_out'], tb=True, name="a_out_dx")
    g_out = _mm(p0, dy, ta=True, out_dtype=BF16, name="a_out_dw")
    dproj, s5g, gr['a_d'], g_glu, gr['a_b_glu'] = _s5_bwd(proj0, res0, dp, s5p, a_d, wt['a_w_glu'], a_b_glu)
    for n, g in zip(['a_lam_re', 'a_lam_im', 'a_log_dt', 'a_b_re', 'a_b_im', 'a_c_re', 'a_c_im'], s5g):
        gr[n] = g
    scatter_start([('a_w_glu', g_glu, 0), ('a_w_out', g_out, 0)], 'a2')
    g_in = _mm(h0, dproj, ta=True, out_dtype=BF16, name="a_in_dw")
    scatter_start([('a_w_in', g_in, 1)], 'a1')
    dh = _mm(dproj, wt['a_w_in'], tb=True, name="a_in_dx")
    dx, d_pre[0] = _prenorm_bwd(dx, dh, x0, pre_norm[0:1], "pre0_bwd")
    gr['pre_norm'] = jnp.concatenate(d_pre, axis=0)
    gr['post_norm'] = jnp.concatenate(d_post, axis=0)
    small_names = _REPLICATED + [n for n, _ in _SHARDED_VECS]
    small = jnp.concatenate([gr[n].reshape(-1) for n in small_names])
    small = jnp.pad(small, (0, N_CHIPS * _SMALL_Q_ROWS * _LANES - small.shape[0])).reshape(N_CHIPS, _SMALL_Q_ROWS, _LANES)
    scatter_start([('small', small, None)], 'a0')

    partial = {}
    for st, plans, items, tag in scattered:
        done = _exchange_wait(st, plans, dx, "scatter_wait_" + tag)
        for i, (n, _, dim) in enumerate(items):
            partial[n] = _sum_parts(me1, done[i], done[len(items) + i], dim, "sum_parts_" + n)
    order = [n for n, _, _, _ in _BIG] + ['small']
    other = dict(zip(order, _sibling_swap([partial[n] for n in order], "swap_cores")))
    small_q = _add2(partial['small'], other['small'])
    small_all = _all_gather_chips(small_q, "gather_small_grads").reshape(-1)
    grads = {}
    off = 0
    for n in _REPLICATED:
        sz = math.prod(arg[n].shape)
        grads[n] = small_all[off:off + sz].reshape(arg[n].shape)
        off += sz
    for n, sz in _SHARDED_VECS:
        q = sz // N_CHIPS
        grads[n] = lax.dynamic_slice(small_all, (off + me * q,), (q,)).reshape(arg[n].shape)
        off += sz

    delta, new_m, new_v = {}, {}, {}
    for n, _, _, _ in _BIG:
        shp = arg[n].shape
        two = (shp[1], shp[2])
        outs = _adamw(arg[n].reshape(two), partial[n], arg['m_' + n].reshape(two), arg['v_' + n].reshape(two),
                      "adamw_" + n, g2=other[n])
        grads[n], delta[n], new_m[n], new_v[n] = [o.reshape(shp) for o in outs]

    for n in small_names:
        shp = arg[n].shape
        two = (math.prod(shp[:-1]), shp[-1])
        outs = _adamw(arg[n].reshape(two), grads[n].reshape(two), arg['m_' + n].reshape(two), arg['v_' + n].reshape(two),
                      "adamw_" + n)
        delta[n], new_m[n], new_v[n] = [o.reshape(shp) for o in outs[1:]]

    return (loss, dx.reshape(x.shape), *[grads[n] for n in _WEIGHTS], *[delta[n] for n in _WEIGHTS],
            *[new_m[n] for n in _WEIGHTS], *[new_v[n] for n in _WEIGHTS])
```

```python
import functools
import math

import numpy as np
import jax
import jax.numpy as jnp
from jax import lax
from jax.experimental import pallas as pl
from jax.experimental.pallas import tpu as pltpu

F32 = jnp.float32
BF16 = jnp.bfloat16
S = jax.ShapeDtypeStruct
MESH = pl.DeviceIdType.MESH

D_MODEL = 1024
EPS = 1e-6
NEG_INF = -1e30
N_CHIPS = 4

SSM_GROUPS = 64
SSM_GROUP = 16
SSM_STATE = 64
SSM_CHUNK = 16
SSM_BLOCK = SSM_CHUNK * SSM_GROUP

HEAD_DIM = 64
SWA_HEADS = 16
SWA_GROUP = 8
WINDOW = 128
REL_BUCKETS = 32
REL_MAX_DIST = 128

MLA_HEADS = 16
MLA_NOPE = 64
MLA_ROPE = 32
MLA_Q_RANK = 768
MLA_KV_RANK = 256
MLA_SCALE = (MLA_NOPE + MLA_ROPE) ** -0.5
MLA_BQ = 128
MLA_BK = 256
ROPE_BASE = 10000.0

SGU_CHUNK = 128
SGU_GROUPS = 16
SGU_GDIM = 64

ADAM_LR = 0.001
ADAM_B1 = 0.9
ADAM_B2 = 0.999
ADAM_EPS = 1e-08
ADAM_WD = 0.01
ADAM_STEP = 10

VMEM_BIG = 56 * 1024 * 1024


def _cp(sem, vmem=None):
    return pltpu.CompilerParams(dimension_semantics=sem, vmem_limit_bytes=vmem)


def _row_spec(tr, c):
    return pl.BlockSpec((tr, c), lambda i: (i, 0))


def _full_spec(shape):
    nd = len(shape)
    return pl.BlockSpec(shape, lambda *_: (0,) * nd)


def _gelu(x):
    c = 0.7978845608028654
    return 0.5 * x * (1.0 + jnp.tanh(c * (x + 0.044715 * x * x * x)))


def _gelu_grad(x):
    c = 0.7978845608028654
    t = jnp.tanh(c * (x + 0.044715 * x * x * x))
    return 0.5 * (1.0 + t) + 0.5 * x * (1.0 - t * t) * c * (1.0 + 3.0 * 0.044715 * x * x)


def _silu(z):
    return z * jax.nn.sigmoid(z)


def _silu_grad(z):
    s = jax.nn.sigmoid(z)
    return s * (1.0 + z * (1.0 - s))


def _dot(a, b):
    return lax.dot_general(a, b, (((1,), (0,)), ((), ())), preferred_element_type=F32)


def _dot_nt(a, b):
    return lax.dot_general(a, b, (((1,), (1,)), ((), ())), preferred_element_type=F32)


def _dot_tn(a, b):
    return lax.dot_general(a, b, (((0,), (0,)), ((), ())), preferred_element_type=F32)


def _pick(n, cap):
    best = 0
    for d in range(128, min(n, cap) + 1, 128):
        if n % d == 0:
            best = d
    if best < 256:
        return n
    return best


def _mm(a, b, *, ta=False, tb=False, out_dtype=F32, name, after=()):
    m, k = (a.shape[1], a.shape[0]) if ta else a.shape
    n = b.shape[0] if tb else b.shape[1]
    assert (b.shape[1] if tb else b.shape[0]) == k
    tm, tn, tk = _pick(m, 1024), _pick(n, 1024), _pick(k, 1024)
    if tn > 1024:
        tm = _pick(m, 256)
    if tk > 1024:
        tm, tn = _pick(m, 256), _pick(n, 256)
    nk = k // tk
    dims = (((0 if ta else 1,), (1 if tb else 0,)), ((), ()))

    def body(a_ref, b_ref, *rest):
        o_ref, acc_ref = rest[-2], rest[-1]
        kk = pl.program_id(2)

        @pl.when(kk == 0)
        def _():
            acc_ref[...] = jnp.zeros_like(acc_ref)

        acc_ref[...] += lax.dot_general(a_ref[...].astype(BF16), b_ref[...].astype(BF16), dims,
                                        preferred_element_type=F32)

        @pl.when(kk == nk - 1)
        def _():
            o_ref[...] = acc_ref[...].astype(out_dtype)

    a_spec = pl.BlockSpec((tk, tm), lambda i, j, kk: (kk, i)) if ta else pl.BlockSpec((tm, tk), lambda i, j, kk: (i, kk))
    b_spec = pl.BlockSpec((tn, tk), lambda i, j, kk: (j, kk)) if tb else pl.BlockSpec((tk, tn), lambda i, j, kk: (kk, j))
    return pl.pallas_call(
        body, grid=(m // tm, n // tn, nk), in_specs=[a_spec, b_spec] + [pl.BlockSpec(memory_space=pl.ANY)] * len(after),
        out_specs=pl.BlockSpec((tm, tn), lambda i, j, kk: (i, j)), out_shape=S((m, n), out_dtype),
        scratch_shapes=[pltpu.VMEM((tm, tn), F32)], name=name,
        compiler_params=_cp(("parallel", "parallel", "arbitrary"), VMEM_BIG))(a, b, *after)


def _prenorm(x, g, name, after=()):
    t, d = x.shape
    tr = 256

    def body(x_ref, g_ref, *rest):
        o_ref = rest[-1]
        xv = x_ref[...]
        r = lax.rsqrt(jnp.mean(xv * xv, axis=-1, keepdims=True) + EPS)
        o_ref[...] = (xv * r * g_ref[...]).astype(BF16)

    return pl.pallas_call(body, grid=(t // tr,),
                          in_specs=[_row_spec(tr, d), _full_spec((1, d))] + [pl.BlockSpec(memory_space=pl.ANY)] * len(after),
                          out_specs=_row_spec(tr, d), out_shape=S((t, d), BF16), name=name,
                          compiler_params=_cp(("parallel",)))(x, g, *after)


def _postnorm(x, y, g, name):
    t, d = x.shape
    tr = 256

    def body(x_ref, y_ref, g_ref, o_ref):
        yv = y_ref[...]
        r = lax.rsqrt(jnp.mean(yv * yv, axis=-1, keepdims=True) + EPS)
        o_ref[...] = x_ref[...] + yv * r * g_ref[...]

    return pl.pallas_call(body, grid=(t // tr,), in_specs=[_row_spec(tr, d), _row_spec(tr, d), _full_spec((1, d))],
                          out_specs=_row_spec(tr, d), out_shape=S((t, d), F32), name=name,
                          compiler_params=_cp(("parallel",)))(x, y, g)


def _rms_bwd_rows(dout, xin, g):
    r = lax.rsqrt(jnp.mean(xin * xin, axis=-1, keepdims=True) + EPS)
    xh = xin * r
    gd = dout * g
    dx = r * (gd - xh * jnp.mean(gd * xh, axis=-1, keepdims=True))
    return dx, jnp.sum(dout * xh, axis=0, keepdims=True)


def _postnorm_bwd(dxn, y, g, name, after=()):
    t, d = y.shape
    tr = 256

    def body(dx_ref, y_ref, g_ref, *rest):
        dy_ref, dg_ref = rest[-2], rest[-1]

        @pl.when(pl.program_id(0) == 0)
        def _():
            dg_ref[...] = jnp.zeros_like(dg_ref)

        dy, dg = _rms_bwd_rows(dx_ref[...], y_ref[...], g_ref[...])
        dy_ref[...] = dy.astype(BF16)
        dg_ref[...] += dg

    return pl.pallas_call(body, grid=(t // tr,),
                          in_specs=[_row_spec(tr, d), _row_spec(tr, d), _full_spec((1, d))]
                          + [pl.BlockSpec(memory_space=pl.ANY)] * len(after),
                          out_specs=[_row_spec(tr, d), _full_spec((1, d))],
                          out_shape=[S((t, d), BF16), S((1, d), F32)], name=name,
                          compiler_params=_cp(("arbitrary",)))(dxn, y, g, *after)


def _prenorm_bwd(dxn, dh, x, g, name):
    t, d = x.shape
    tr = 256

    def body(dxn_ref, dh_ref, x_ref, g_ref, dx_ref, dg_ref):
        @pl.when(pl.program_id(0) == 0)
        def _():
            dg_ref[...] = jnp.zeros_like(dg_ref)

        dx, dg = _rms_bwd_rows(dh_ref[...], x_ref[...], g_ref[...])
        dx_ref[...] = dxn_ref[...] + dx
        dg_ref[...] += dg

    return pl.pallas_call(body, grid=(t // tr,),
                          in_specs=[_row_spec(tr, d), _row_spec(tr, d), _row_spec(tr, d), _full_spec((1, d))],
                          out_specs=[_row_spec(tr, d), _full_spec((1, d))],
                          out_shape=[S((t, d), F32), S((1, d), F32)], name=name,
                          compiler_params=_cp(("arbitrary",)))(dxn, dh, x, g)


def _loss_head(y, target):
    t, d = y.shape
    tr = 256

    def body(y_ref, t_ref, l_ref, dy_ref):
        @pl.when(pl.program_id(0) == 0)
        def _():
            l_ref[...] = jnp.zeros_like(l_ref)

        e = y_ref[...] - t_ref[...]
        dy_ref[...] = e * (1.0 / d)
        l_ref[...] += 0.5 * jnp.sum(jnp.mean(e * e, axis=-1, keepdims=True), axis=0, keepdims=True)

    return pl.pallas_call(body, grid=(t // tr,), in_specs=[_row_spec(tr, d), _row_spec(tr, d)],
                          out_specs=[_full_spec((1, 1)), _row_spec(tr, d)],
                          out_shape=[S((1, 1), F32), S((t, d), F32)], name="loss_head",
                          compiler_params=_cp(("arbitrary",)))(y, target)


def _sgu_parts(proj, lg, lb, ws_ref, bst):
    a = proj[:, 0:1024]
    bb = proj[:, 1024:2048]
    z = proj[:, 2048:3072]
    u = _gelu(a)
    vp = _gelu(bb)
    mu = jnp.mean(vp, axis=-1, keepdims=True)
    xc = vp - mu
    rstd = lax.rsqrt(jnp.mean(xc * xc, axis=-1, keepdims=True) + EPS)
    xh = xc * rstd
    vn = xh * lg + lb
    row = lax.broadcasted_iota(jnp.int32, (SGU_CHUNK, SGU_CHUNK), 0)
    col = lax.broadcasted_iota(jnp.int32, (SGU_CHUNK, SGU_CHUNK), 1)
    tril = row >= col
    ws = [jnp.where(tril, ws_ref[g], 0.0).astype(BF16) for g in range(SGU_GROUPS)]
    s = jnp.concatenate(
        [_dot(ws[g], vn[:, g * 64:(g + 1) * 64].astype(BF16)) + bst[:, g:g + 1] for g in range(SGU_GROUPS)], axis=1)
    return a, bb, z, u, vp, rstd, xh, vn, ws, tril, s


def _sgu_fwd(proj, lg, lb, w_s, bst):
    t = proj.shape[0]
    tr = SGU_CHUNK

    def body(p_ref, lg_ref, lb_ref, ws_ref, bst_ref, o_ref):
        _, _, z, u, _, _, _, _, _, _, s = _sgu_parts(p_ref[...], lg_ref[...], lb_ref[...], ws_ref, bst_ref[...])
        o_ref[...] = (u * s * _silu(z)).astype(BF16)

    return pl.pallas_call(
        body, grid=(t // tr,),
        in_specs=[_row_spec(tr, 3072), _full_spec((1, 1024)), _full_spec((1, 1024)), _full_spec((16, 128, 128)),
                  _full_spec((128, 16))],
        out_specs=_row_spec(tr, 1024), out_shape=S((t, 1024), BF16), name="sgu_fwd",
        compiler_params=_cp(("parallel",)))(proj, lg, lb, w_s, bst)


def _sgu_bwd(proj, dp, lg, lb, w_s, bst):
    t = proj.shape[0]
    tr = SGU_CHUNK

    def body(p_ref, dp_ref, lg_ref, lb_ref, ws_ref, bst_ref, dproj_ref, dws_ref, dbst_ref, dlg_ref, dlb_ref):
        @pl.when(pl.program_id(0) == 0)
        def _():
            dws_ref[...] = jnp.zeros_like(dws_ref)
            dbst_ref[...] = jnp.zeros_like(dbst_ref)
            dlg_ref[...] = jnp.zeros_like(dlg_ref)
            dlb_ref[...] = jnp.zeros_like(dlb_ref)

        lgv = lg_ref[...]
        a, bb, z, u, vp, rstd, xh, vn, ws, tril, s = _sgu_parts(p_ref[...], lgv, lb_ref[...], ws_ref, bst_ref[...])
        dpv = dp_ref[...].astype(F32)
        sz = _silu(z)
        du = dpv * s * sz
        ds = dpv * u * sz
        dz = dpv * u * s * _silu_grad(z)
        dvn_parts, dbs_cols = [], []
        for g in range(SGU_GROUPS):
            dsg = ds[:, g * 64:(g + 1) * 64]
            dsg_b = dsg.astype(BF16)
            dvn_parts.append(_dot_tn(ws[g], dsg_b))
            dws_ref[g] += jnp.where(tril, _dot_nt(dsg_b, vn[:, g * 64:(g + 1) * 64].astype(BF16)), 0.0)
            dbs_cols.append(jnp.sum(dsg, axis=1, keepdims=True))
        dvn = jnp.concatenate(dvn_parts, axis=1)
        dbst_ref[...] += jnp.concatenate(dbs_cols, axis=1)
        dlg_ref[...] += jnp.sum(dvn * xh, axis=0, keepdims=True)
        dlb_ref[...] += jnp.sum(dvn, axis=0, keepdims=True)
        dxh = dvn * lgv
        dvp = rstd * (dxh - jnp.mean(dxh, axis=-1, keepdims=True) - xh * jnp.mean(dxh * xh, axis=-1, keepdims=True))
        dproj_ref[:, 0:1024] = (du * _gelu_grad(a)).astype(BF16)
        dproj_ref[:, 1024:2048] = (dvp * _gelu_grad(bb)).astype(BF16)
        dproj_ref[:, 2048:3072] = dz.astype(BF16)

    return pl.pallas_call(
        body, grid=(t // tr,),
        in_specs=[_row_spec(tr, 3072), _row_spec(tr, 1024), _full_spec((1, 1024)), _full_spec((1, 1024)),
                  _full_spec((16, 128, 128)), _full_spec((128, 16))],
        out_specs=[_row_spec(tr, 3072), _full_spec((16, 128, 128)), _full_spec((128, 16)), _full_spec((1, 1024)),
                   _full_spec((1, 1024))],
        out_shape=[S((t, 3072), BF16), S((16, 128, 128), F32), S((128, 16), F32), S((1, 1024), F32),
                   S((1, 1024), F32)],
        name="sgu_bwd", compiler_params=_cp(("arbitrary",)))(proj, dp, lg, lb, w_s, bst)


def _t5_bucket_table():
    qi = np.arange(WINDOW)[:, None]
    kj = np.arange(2 * WINDOW)[None, :]
    dist = np.maximum(qi + WINDOW - kj, 0)
    max_exact = REL_BUCKETS // 2
    dist_f = np.maximum(dist, 1).astype(np.float32)
    large = max_exact + (np.log(dist_f / np.float32(max_exact)) / np.float32(math.log(REL_MAX_DIST / max_exact))
                         * np.float32(REL_BUCKETS - max_exact)).astype(np.int32)
    large = np.minimum(large, REL_BUCKETS - 1)
    return np.where(dist < max_exact, dist, large).astype(np.int32)


def _swa_bias(rel_bias, bucket):
    def body(rb_ref, bk_ref, o_ref):
        bk = bk_ref[...]
        for h in range(SWA_HEADS):
            acc = jnp.zeros((WINDOW, 2 * WINDOW), F32)
            for b in range(REL_BUCKETS):
                acc = jnp.where(bk == b, rb_ref[b, h], acc)
            o_ref[h] = acc

    return pl.pallas_call(
        body, in_specs=[pl.BlockSpec(memory_space=pltpu.SMEM), pl.BlockSpec(memory_space=pltpu.VMEM)],
        out_specs=pl.BlockSpec(memory_space=pltpu.VMEM), out_shape=S((SWA_HEADS, WINDOW, 2 * WINDOW), F32),
        name="swa_bias")(rel_bias, bucket)


def _swa_dbias(ds_sum, dsink_rows, bucket):
    def body(ds_ref, dsk_ref, bk_ref, drb_ref, dsink_ref):
        bk = bk_ref[...]
        r = lax.broadcasted_iota(jnp.int32, (REL_BUCKETS, SWA_HEADS), 0)
        c = lax.broadcasted_iota(jnp.int32, (REL_BUCKETS, SWA_HEADS), 1)
        out = jnp.zeros((REL_BUCKETS, SWA_HEADS), F32)
        for b in range(REL_BUCKETS):
            m = bk == b
            for h in range(SWA_HEADS):
                val = jnp.sum(jnp.where(m, ds_ref[h], 0.0))
                out = jnp.where((r == b) & (c == h), val, out)
        drb_ref[...] = out
        dsink_ref[...] = jnp.sum(dsk_ref[...], axis=0, keepdims=True)

    return pl.pallas_call(
        body, out_shape=[S((REL_BUCKETS, SWA_HEADS), F32), S((1, SWA_HEADS), F32)], name="swa_dbias",
    )(ds_sum, dsink_rows, bucket)


def _swa_band(cur_ref, prev_ref):
    kband = jnp.concatenate([prev_ref[:, 0:128], cur_ref[:, 1024:1152]], axis=0).astype(BF16)
    vband = jnp.concatenate([prev_ref[:, 128:256], cur_ref[:, 1152:1280]], axis=0).astype(BF16)
    return kband, vband


def _swa_valid(n):
    qi = lax.broadcasted_iota(jnp.int32, (WINDOW, 2 * WINDOW), 0)
    kj = lax.broadcasted_iota(jnp.int32, (WINDOW, 2 * WINDOW), 1)
    dist = qi + WINDOW - kj
    return (dist >= 0) & (dist < WINDOW) & (n * WINDOW + kj - WINDOW >= 0)


def _swa_probs(qh, kh, bias_h, sink, valid):
    s = _dot_nt(qh, kh) * (HEAD_DIM ** -0.5) + bias_h
    s = jnp.where(valid, s, NEG_INF)
    m = jnp.maximum(jnp.max(s, axis=-1, keepdims=True), sink)
    e = jnp.exp(s - m)
    es = jnp.exp(sink - m)
    inv = 1.0 / (jnp.sum(e, axis=-1, keepdims=True) + es)
    return e * inv, es * inv


def _swa_in_specs():
    return [_row_spec(WINDOW, 2304),
            pl.BlockSpec((WINDOW, 256), lambda n: (jnp.maximum(n - 1, 0), 4)),
            _full_spec((SWA_HEADS, WINDOW, 2 * WINDOW)),
            pl.BlockSpec(memory_space=pltpu.SMEM)]


def _swa_fwd(proj, bias, sinks):
    t = proj.shape[0]

    def body(cur_ref, prev_ref, bias_ref, sink_ref, o_ref):
        n = pl.program_id(0)
        kband, vband = _swa_band(cur_ref, prev_ref)
        valid = _swa_valid(n)
        outs = []
        for h in range(SWA_HEADS):
            j = h // SWA_GROUP
            qh = cur_ref[:, h * 64:(h + 1) * 64].astype(BF16)
            p, _ = _swa_probs(qh, kband[:, j * 64:(j + 1) * 64], bias_ref[h], sink_ref[0, h], valid)
            outs.append(_dot(p.astype(BF16), vband[:, j * 64:(j + 1) * 64]))
        o = jnp.concatenate(outs, axis=1)
        o_ref[...] = (o * _silu(cur_ref[:, 1280:2304])).astype(BF16)

    return pl.pallas_call(body, grid=(t // WINDOW,), in_specs=_swa_in_specs(), out_specs=_row_spec(WINDOW, 1024),
                          out_shape=S((t, 1024), BF16), name="swa_fwd",
                          compiler_params=_cp(("parallel",)))(proj, proj, bias, sinks)


def _swa_bwd(proj, dp, bias, sinks):
    t = proj.shape[0]
    nb = t // WINDOW

    def body(cur_ref, prev_ref, bias_ref, sink_ref, dp_ref, dqz_ref, dkv_ref, dss_ref, dsk_ref):
        n = pl.program_id(0)

        @pl.when(n == 0)
        def _():
            dkv_ref[...] = jnp.zeros_like(dkv_ref)
            dss_ref[...] = jnp.zeros_like(dss_ref)
            dsk_ref[...] = jnp.zeros_like(dsk_ref)

        kband, vband = _swa_band(cur_ref, prev_ref)
        valid = _swa_valid(n)
        z = cur_ref[:, 1280:2304]
        dpv = dp_ref[...].astype(F32)
        do = dpv * _silu(z)
        outs, dqs, dsinks = [], [], []
        dk = [jnp.zeros((2 * WINDOW, 64), F32) for _ in range(2)]
        dv = [jnp.zeros((2 * WINDOW, 64), F32) for _ in range(2)]
        for h in range(SWA_HEADS):
            j = h // SWA_GROUP
            qh = cur_ref[:, h * 64:(h + 1) * 64].astype(BF16)
            kh = kband[:, j * 64:(j + 1) * 64]
            vh = vband[:, j * 64:(j + 1) * 64]
            p, ps = _swa_probs(qh, kh, bias_ref[h], sink_ref[0, h], valid)
            pb = p.astype(BF16)
            outs.append(_dot(pb, vh))
            doh = do[:, h * 64:(h + 1) * 64].astype(BF16)
            dpr = _dot_nt(doh, vh)
            dlt = jnp.sum(p * dpr, axis=-1, keepdims=True)
            ds = p * (dpr - dlt)
            dsinks.append(-ps * dlt)
            dss_ref[h] += ds
            dsb = (ds * (HEAD_DIM ** -0.5)).astype(BF16)
            dqs.append(_dot(dsb, kh))
            dk[j] = dk[j] + _dot_tn(dsb, qh)
            dv[j] = dv[j] + _dot_tn(pb, doh)
        o = jnp.concatenate(outs, axis=1)
        dqz_ref[:, 0:1024] = jnp.concatenate(dqs, axis=1).astype(BF16)
        dqz_ref[:, 1024:2048] = (dpv * o * _silu_grad(z)).astype(BF16)
        dsk_ref[...] += jnp.concatenate(dsinks, axis=1)
        dband = jnp.concatenate([dk[0], dk[1], dv[0], dv[1]], axis=1)
        prow = pl.multiple_of(jnp.maximum(n - 1, 0) * WINDOW, WINDOW)
        dkv_ref[pl.ds(prow, WINDOW), :] += dband[0:WINDOW]
        crow = pl.multiple_of(n * WINDOW, WINDOW)
        dkv_ref[pl.ds(crow, WINDOW), :] += dband[WINDOW:]

    return pl.pallas_call(
        body, grid=(nb,), in_specs=_swa_in_specs() + [_row_spec(WINDOW, 1024)],
        out_specs=[_row_spec(WINDOW, 2048), _full_spec((t, 256)), _full_spec((SWA_HEADS, WINDOW, 2 * WINDOW)),
                   _full_spec((WINDOW, SWA_HEADS))],
        out_shape=[S((t, 2048), BF16), S((t, 256), F32), S((SWA_HEADS, WINDOW, 2 * WINDOW), F32),
                   S((WINDOW, SWA_HEADS), F32)],
        name="swa_bwd", compiler_params=_cp(("arbitrary",)))(proj, proj, bias, sinks, dp)


MLA_Z0 = MLA_Q_RANK + MLA_KV_RANK + 128
MLA_Z1 = MLA_Z0 + 1024
MLA_PROJ = MLA_Z1 + 128


def _rope_tables(t):
    inv = ROPE_BASE ** (-jnp.arange(0, MLA_ROPE, 2, dtype=F32) / MLA_ROPE)
    ang = jnp.arange(t, dtype=F32)[:, None] * inv[None, :]
    return jnp.tile(jnp.cos(ang), (1, MLA_HEADS)), jnp.tile(jnp.sin(ang), (1, MLA_HEADS))


def _mla_norms(proj, gq, gkv):
    t = proj.shape[0]
    tr = 256

    def body(p_ref, gq_ref, gkv_ref, q_ref, kv_ref):
        for lo, hi, g_ref, o_ref in ((0, 768, gq_ref, q_ref), (768, 1024, gkv_ref, kv_ref)):
            xv = p_ref[:, lo:hi]
            r = lax.rsqrt(jnp.mean(xv * xv, axis=-1, keepdims=True) + EPS)
            o_ref[...] = (xv * r * g_ref[...]).astype(BF16)

    return pl.pallas_call(
        body, grid=(t // tr,), in_specs=[_row_spec(tr, 1024), _full_spec((1, 768)), _full_spec((1, 256))],
        out_specs=[_row_spec(tr, 768), _row_spec(tr, 256)], out_shape=[S((t, 768), BF16), S((t, 256), BF16)],
        name="mla_norms", compiler_params=_cp(("parallel",)))(proj, gq, gkv)


def _mla_norms_bwd(proj, dcqn, dckvn, dkr, dz, gq, gkv):
    t = proj.shape[0]
    tr = 256

    def body(p_ref, dq_ref, dkv_ref, dkr_ref, dz_ref, gq_ref, gkv_ref, dproj_ref, dgq_ref, dgkv_ref):
        @pl.when(pl.program_id(0) == 0)
        def _():
            dgq_ref[...] = jnp.zeros_like(dgq_ref)
            dgkv_ref[...] = jnp.zeros_like(dgkv_ref)

        dx, dg = _rms_bwd_rows(dq_ref[...], p_ref[:, 0:768], gq_ref[...])
        dproj_ref[:, 0:768] = dx.astype(BF16)
        dgq_ref[...] += dg
        dx, dg = _rms_bwd_rows(dkv_ref[...], p_ref[:, 768:1024], gkv_ref[...])
        dproj_ref[:, 768:1024] = dx.astype(BF16)
        dgkv_ref[...] += dg
        dproj_ref[:, 1024:MLA_Z0] = dkr_ref[...].astype(BF16)
        dproj_ref[:, MLA_Z0:MLA_Z1] = dz_ref[...]
        dproj_ref[:, MLA_Z1:MLA_PROJ] = jnp.zeros((tr, MLA_PROJ - MLA_Z1), BF16)

    return pl.pallas_call(
        body, grid=(t // tr,),
        in_specs=[_row_spec(tr, 1024), _row_spec(tr, 768), _row_spec(tr, 256), _row_spec(tr, 128), _row_spec(tr, 1024),
                  _full_spec((1, 768)), _full_spec((1, 256))],
        out_specs=[_row_spec(tr, MLA_PROJ), _full_spec((1, 768)), _full_spec((1, 256))],
        out_shape=[S((t, MLA_PROJ), BF16), S((1, 768), F32), S((1, 256), F32)],
        name="mla_norms_bwd", compiler_params=_cp(("arbitrary",)))(proj, dcqn, dckvn, dkr, dz, gq, gkv)


def _mla_pack(qf, kvf, proj, cos, sin):
    t = qf.shape[0]
    tr = 256

    def body(q_ref, kn_ref, kr_ref, cos_ref, sin_ref, qh_ref, kh_ref):
        cs, sn = cos_ref[...], sin_ref[...]
        x1, x2 = q_ref[:, 1024:1280], q_ref[:, 1280:1536]
        r1 = x1 * cs - x2 * sn
        r2 = x2 * cs + x1 * sn
        c16, s16 = cs[:, 0:16], sn[:, 0:16]
        k1, k2 = kr_ref[:, 0:16], kr_ref[:, 16:32]
        kr1 = k1 * c16 - k2 * s16
        kr2 = k2 * c16 + k1 * s16
        zpad = jnp.zeros((tr, 32), F32)
        for h in range(MLA_HEADS):
            qh_ref[h] = jnp.concatenate([q_ref[:, h * 64:(h + 1) * 64], r1[:, h * 16:(h + 1) * 16],
                                         r2[:, h * 16:(h + 1) * 16], zpad], axis=1).astype(BF16)
            kh_ref[h] = jnp.concatenate([kn_ref[:, h * 64:(h + 1) * 64], kr1, kr2, zpad], axis=1).astype(BF16)

    hspec = pl.BlockSpec((MLA_HEADS, tr, 128), lambda i: (0, i, 0))
    return pl.pallas_call(
        body, grid=(t // tr,),
        in_specs=[_row_spec(tr, 1536), _row_spec(tr, 1024), pl.BlockSpec((tr, 128), lambda i: (i, 8)),
                  _row_spec(tr, 256), _row_spec(tr, 256)],
        out_specs=[hspec, hspec], out_shape=[S((MLA_HEADS, t, 128), BF16)] * 2,
        name="mla_pack", compiler_params=_cp(("parallel",)))(qf, kvf, proj, cos, sin)


def _mla_unpack(dqh, dkh, dv, cos, sin):
    t = dqh.shape[1]
    tr = 256

    def body(dqh_ref, dkh_ref, dv_ref, cos_ref, sin_ref, dq_ref, dkv_ref, dkr_ref):
        cs, sn = cos_ref[...], sin_ref[...]
        dqn = jnp.concatenate([dqh_ref[h, :, 0:64] for h in range(MLA_HEADS)], axis=1)
        dr1 = jnp.concatenate([dqh_ref[h, :, 64:80] for h in range(MLA_HEADS)], axis=1)
        dr2 = jnp.concatenate([dqh_ref[h, :, 80:96] for h in range(MLA_HEADS)], axis=1)
        dq_ref[:, 0:1024] = dqn.astype(BF16)
        dq_ref[:, 1024:1280] = (dr1 * cs + dr2 * sn).astype(BF16)
        dq_ref[:, 1280:1536] = (dr2 * cs - dr1 * sn).astype(BF16)
        dkv_ref[:, 0:1024] = jnp.concatenate([dkh_ref[h, :, 0:64] for h in range(MLA_HEADS)], axis=1).astype(BF16)
        dkv_ref[:, 1024:2048] = dv_ref[...].astype(BF16)
        d1 = dkh_ref[0, :, 64:80]
        d2 = dkh_ref[0, :, 80:96]
        for h in range(1, MLA_HEADS):
            d1 = d1 + dkh_ref[h, :, 64:80]
            d2 = d2 + dkh_ref[h, :, 80:96]
        c16, s16 = cs[:, 0:16], sn[:, 0:16]
        dkr_ref[...] = jnp.concatenate([d1 * c16 + d2 * s16, d2 * c16 - d1 * s16, jnp.zeros((tr, 96), F32)], axis=1)

    hspec = pl.BlockSpec((MLA_HEADS, tr, 128), lambda i: (0, i, 0))
    return pl.pallas_call(
        body, grid=(t // tr,), in_specs=[hspec, hspec, _row_spec(tr, 1024), _row_spec(tr, 256), _row_spec(tr, 256)],
        out_specs=[_row_spec(tr, 1536), _row_spec(tr, 2048), _row_spec(tr, 128)],
        out_shape=[S((t, 1536), BF16), S((t, 2048), BF16), S((t, 128), F32)],
        name="mla_unpack", compiler_params=_cp(("parallel",)))(dqh, dkh, dv, cos, sin)


def _mla_attn_fwd(qh, kh, kvf, proj):
    t = qh.shape[1]
    b, bk = MLA_BQ, MLA_BK

    def body(q_ref, k_ref, v_ref, z_ref, o_ref, p_ref, lse_ref):
        r0 = pl.program_id(1) * b
        nfull = r0 // bk
        diff = (lax.broadcasted_iota(jnp.int32, (b, bk), 1) - lax.broadcasted_iota(jnp.int32, (b, bk), 0))
        qs = [q_ref[0], q_ref[1]]

        def block(c0, carry, masked):
            new = []
            for a in range(2):
                m, l, acc = carry[a]
                k = k_ref[a, pl.ds(c0, bk), :]
                v = v_ref[pl.ds(c0, bk), a * 64:(a + 1) * 64].astype(BF16)
                s = _dot_nt(qs[a], k) * MLA_SCALE
                if masked:
                    s = jnp.where(diff <= r0 - c0, s, NEG_INF)
                mn = jnp.maximum(m, jnp.max(s, axis=-1, keepdims=True))
                alpha = jnp.exp(m - mn)
                p = jnp.exp(s - mn)
                new.append((mn, alpha * l + jnp.sum(p, axis=-1, keepdims=True), alpha * acc + _dot(p.astype(BF16), v)))
            return tuple(new)

        init = (jnp.full((b, 1), NEG_INF, F32), jnp.zeros((b, 1), F32), jnp.zeros((b, 64), F32))
        carry = lax.fori_loop(0, nfull, lambda c, cr: block(pl.multiple_of(c * bk, bk), cr, False), (init, init))
        carry = block(pl.multiple_of(nfull * bk, bk), carry, True)
        outs = [acc / l for _, l, acc in carry]
        lses = [jnp.broadcast_to(m + jnp.log(l), (b, 64)) for m, l, _ in carry]
        o = jnp.concatenate(outs, axis=1)
        o_ref[...] = o
        p_ref[...] = (o * _silu(z_ref[...])).astype(BF16)
        lse_ref[0] = jnp.concatenate(lses, axis=1)

    return pl.pallas_call(
        body, grid=(MLA_HEADS // 2, t // b),
        in_specs=[pl.BlockSpec((2, b, 128), lambda i, r: (i, r, 0)), pl.BlockSpec((2, t, 128), lambda i, r: (i, 0, 0)),
                  pl.BlockSpec((t, 128), lambda i, r: (0, 8 + i)), pl.BlockSpec((b, 128), lambda i, r: (r, 9 + i))],
        out_specs=[pl.BlockSpec((b, 128), lambda i, r: (r, i)), pl.BlockSpec((b, 128), lambda i, r: (r, i)),
                   pl.BlockSpec((1, b, 128), lambda i, r: (i, r, 0))],
        out_shape=[S((t, 1024), F32), S((t, 1024), BF16), S((MLA_HEADS // 2, t, 128), F32)],
        name="mla_attn_fwd", compiler_params=_cp(("parallel", "parallel")))(qh, kh, kvf, proj)


def _mla_attn_bwd(qh, kh, kvf, proj, o, dpout, lse):
    t = qh.shape[1]
    b, bk = MLA_BQ, MLA_BK
    nb = t // b

    def body(q_ref, k_ref, v_ref, z_ref, o_ref, dp_ref, lse_ref, dq_ref, dk_ref, dv_ref, dz_ref):
        dk_ref[...] = jnp.zeros_like(dk_ref)
        dv_ref[...] = jnp.zeros_like(dv_ref)
        diff = (lax.broadcasted_iota(jnp.int32, (b, bk), 1) - lax.broadcasted_iota(jnp.int32, (b, bk), 0))
        lanes = [slice(0, 64), slice(64, 128)]

        def rbody(r, carry):
            r0 = pl.multiple_of(r * b, b)
            rows = pl.ds(r0, b)
            qs, dobs, deltas, lses = [], [], [], []
            for a in range(2):
                zz = z_ref[rows, lanes[a]]
                oo = o_ref[rows, lanes[a]]
                dpo = dp_ref[rows, lanes[a]].astype(F32)
                do = dpo * _silu(zz)
                dz_ref[rows, lanes[a]] = (dpo * oo * _silu_grad(zz)).astype(BF16)
                qs.append(q_ref[a, rows, :])
                dobs.append(do.astype(BF16))
                deltas.append(jnp.sum(do * oo, axis=-1, keepdims=True))
                lses.append(lse_ref[0, rows, a * 64:a * 64 + 1])

            def block(c0, dqs, masked):
                cols = pl.ds(c0, bk)
                out = []
                for a in range(2):
                    k = k_ref[a, cols, :]
                    v = v_ref[cols, lanes[a]].astype(BF16)
                    s = _dot_nt(qs[a], k) * MLA_SCALE
                    if masked:
                        s = jnp.where(diff <= r0 - c0, s, NEG_INF)
                    p = jnp.exp(s - lses[a])
                    ds = p * (_dot_nt(dobs[a], v) - deltas[a]) * MLA_SCALE
                    dsb = ds.astype(BF16)
                    dk_ref[a, cols, :] += _dot_tn(dsb, qs[a])
                    dv_ref[cols, lanes[a]] += _dot_tn(p.astype(BF16), dobs[a])
                    out.append(dqs[a] + _dot(dsb, k))
                return tuple(out)

            zero = jnp.zeros((b, 128), F32)
            nfull = r0 // bk
            dqs = lax.fori_loop(0, nfull, lambda c, d: block(pl.multiple_of(c * bk, bk), d, False), (zero, zero))
            dqs = block(pl.multiple_of(nfull * bk, bk), dqs, True)
            dq_ref[0, rows, :] = dqs[0]
            dq_ref[1, rows, :] = dqs[1]
            return carry

        lax.fori_loop(0, nb, rbody, 0)

    pair3 = pl.BlockSpec((2, t, 128), lambda i: (i, 0, 0))
    return pl.pallas_call(
        body, grid=(MLA_HEADS // 2,),
        in_specs=[pair3, pair3, pl.BlockSpec((t, 128), lambda i: (0, 8 + i)), pl.BlockSpec((t, 128), lambda i: (0, 9 + i)),
                  pl.BlockSpec((t, 128), lambda i: (0, i)), pl.BlockSpec((t, 128), lambda i: (0, i)),
                  pl.BlockSpec((1, t, 128), lambda i: (i, 0, 0))],
        out_specs=[pair3, pair3, pl.BlockSpec((t, 128), lambda i: (0, i)), pl.BlockSpec((t, 128), lambda i: (0, i))],
        out_shape=[S((MLA_HEADS, t, 128), F32), S((MLA_HEADS, t, 128), F32), S((t, 1024), F32), S((t, 1024), BF16)],
        name="mla_attn_bwd", compiler_params=_cp(("parallel",), VMEM_BIG))(qh, kh, kvf, proj, o, dpout, lse)


def _dot_nt_hi(a, b):
    return lax.dot_general(a, b, (((1,), (1,)), ((), ())), preferred_element_type=F32,
                           precision=lax.Precision.HIGHEST)


def _ssm_gen(lr, li, ldt, btr, bti, cr, ci):
    n = SSM_BLOCK
    dt = jnp.exp(ldt)
    x = lr * dt
    w = li * dt
    mag = jnp.exp(x)
    ab_re = mag * jnp.cos(w)
    ab_im = mag * jnp.sin(w)
    den = lr * lr + li * li
    nr = ab_re - 1.0
    f_re = (nr * lr + ab_im * li) / den
    f_im = (ab_im * lr - nr * li) / den
    bb_re = f_re * btr - f_im * bti
    bb_im = f_re * bti + f_im * btr
    inv_mag = jnp.exp(-x)
    inv_re, inv_im = inv_mag * jnp.cos(w), -inv_mag * jnp.sin(w)
    one, zero = jnp.ones_like(x), jnp.zeros_like(x)
    pos, neg = [(one, zero)], [(one, zero)]
    for _ in range(SSM_CHUNK):
        pr, pi = pos[-1]
        pos.append((pr * ab_re - pi * ab_im, pr * ab_im + pi * ab_re))
        pr, pi = neg[-1]
        neg.append((pr * inv_re - pi * inv_im, pr * inv_im + pi * inv_re))

    def stack(table, idx):
        return (jnp.concatenate([jnp.broadcast_to(table[idx(t)][0], (SSM_GROUP, SSM_STATE)) for t in range(SSM_CHUNK)], axis=0),
                jnp.concatenate([jnp.broadcast_to(table[idx(t)][1], (SSM_GROUP, SSM_STATE)) for t in range(SSM_CHUNK)], axis=0))

    def rep(v):
        return jnp.concatenate([v] * SSM_CHUNK, axis=0)

    ct_re, ct_im, bt_re, bt_im = rep(cr), rep(ci), rep(bb_re), rep(bb_im)
    p_re, p_im = stack(pos, lambda t: t)
    l_re = ct_re * p_re - ct_im * p_im
    l_im = ct_re * p_im + ct_im * p_re
    n_re, n_im = stack(neg, lambda t: t)
    r_re = bt_re * n_re - bt_im * n_im
    r_im = bt_re * n_im + bt_im * n_re
    rk = lax.broadcasted_iota(jnp.int32, (n, n), 0) // SSM_GROUP
    ct = lax.broadcasted_iota(jnp.int32, (n, n), 1) // SSM_GROUP
    mt = jnp.where(ct >= rk, _dot_nt_hi(r_re, l_re) - _dot_nt_hi(r_im, l_im), 0.0)
    e_re, e_im = stack(pos, lambda t: SSM_CHUNK - 1 - t)
    pin_re = bt_re * e_re - bt_im * e_im
    pin_im = bt_re * e_im + bt_im * e_re
    q_re, q_im = stack(pos, lambda t: t + 1)
    qt_re = ct_re * q_re - ct_im * q_im
    qt_im = -(ct_re * q_im + ct_im * q_re)
    ad_re, ad_im = pos[SSM_CHUNK]
    return mt, pin_re, pin_im, qt_re, qt_im, ad_re, ad_im


_SSM_GEN_OUT = [(SSM_BLOCK, SSM_BLOCK)] + [(SSM_BLOCK, SSM_STATE)] * 4 + [(1, SSM_STATE)] * 2
_SSM_GEN_IN = [(1, SSM_STATE)] * 3 + [(SSM_GROUP, SSM_STATE)] * 4


def _gspec(shape2):
    return pl.BlockSpec((1,) + shape2, lambda g: (g, 0, 0))


def _ssm_operators(params):
    def body(*refs):
        ins, outs = refs[:7], refs[7:]
        for o_ref, val in zip(outs, _ssm_gen(*[r[0] for r in ins])):
            o_ref[0] = val

    return pl.pallas_call(
        body, grid=(SSM_GROUPS,), in_specs=[_gspec(s) for s in _SSM_GEN_IN], out_specs=[_gspec(s) for s in _SSM_GEN_OUT],
        out_shape=[S((SSM_GROUPS,) + s, F32) for s in _SSM_GEN_OUT], name="ssm_operators",
        compiler_params=_cp(("parallel",)))(*params)


def _ssm_operators_bwd(params, cots, after=()):
    def body(*refs):
        ins, cts, outs = refs[:7], refs[7:14], refs[14 + len(after):]
        _, vjp = jax.vjp(_ssm_gen, *[r[0] for r in ins])
        grads = list(vjp(tuple(r[0] for r in cts)))
        grads[2] = jnp.broadcast_to(jnp.sum(grads[2], axis=-1, keepdims=True), (1, SSM_STATE))
        for o_ref, val in zip(outs, grads):
            o_ref[0] = val

    return pl.pallas_call(
        body, grid=(SSM_GROUPS,),
        in_specs=[_gspec(s) for s in _SSM_GEN_IN] + [_gspec(s) for s in _SSM_GEN_OUT]
        + [pl.BlockSpec(memory_space=pl.ANY)] * len(after),
        out_specs=[_gspec(s) for s in _SSM_GEN_IN], out_shape=[S((SSM_GROUPS,) + s, F32) for s in _SSM_GEN_IN],
        name="ssm_operators_bwd", compiler_params=_cp(("parallel",)))(*params, *cots, *after)


def _shift_rows(x, sh, row, up):
    n = x.shape[0]
    if up:
        return jnp.where(row < n - sh, pltpu.roll(x, n - sh, axis=0), 0.0)
    return jnp.where(row >= sh, pltpu.roll(x, sh, axis=0), 0.0)


def _carry_scan(sr, si, ar, ai, row, up):
    n = sr.shape[0]
    sh = 1
    while sh < n:
        tr, ti = _shift_rows(sr, sh, row, up), _shift_rows(si, sh, row, up)
        sr, si = sr + ar * tr - ai * ti, si + ar * ti + ai * tr
        ar, ai = ar * ar - ai * ai, 2.0 * ar * ai
        sh *= 2
    return sr, si


def _ssm_states(u, p_re, p_im, ar, ai, row):
    lre = _dot(u, p_re.astype(BF16))
    lim = _dot(u, p_im.astype(BF16))
    sr, si = _carry_scan(lre, lim, ar, ai, row, False)
    return _shift_rows(sr, 1, row, False), _shift_rows(si, 1, row, False)


SSM_GPB = 128 // SSM_GROUP


def _step_rows(ref, nc):
    return [ref[pl.ds(k, nc, stride=SSM_CHUNK), :].astype(BF16) for k in range(SSM_CHUNK)]


def _group_of(rows, j):
    return jnp.concatenate([r[:, j * SSM_GROUP:(j + 1) * SSM_GROUP] for r in rows], axis=1)


def _store_groups(o_ref, per_group, nc):
    for k in range(SSM_CHUNK):
        o_ref[pl.ds(k, nc, stride=SSM_CHUNK), :] = jnp.concatenate(
            [y[:, k * SSM_GROUP:(k + 1) * SSM_GROUP] for y in per_group], axis=1)


def _bspec(shape2):
    return pl.BlockSpec((SSM_GPB,) + shape2, lambda j: (j, 0, 0))


def _ssm_core_fwd(proj, ops):
    t = proj.shape[0]
    nc = t // SSM_CHUNK

    def body(u_ref, mt_ref, pr_ref, pi_ref, qr_ref, qi_ref, ar_ref, ai_ref, y_ref):
        rows = _step_rows(u_ref, nc)
        row = lax.broadcasted_iota(jnp.int32, (nc, SSM_STATE), 0)
        ys = []
        for j in range(SSM_GPB):
            u = _group_of(rows, j)
            s_re, s_im = _ssm_states(u, pr_ref[j], pi_ref[j], ar_ref[j], ai_ref[j], row)
            ys.append(_dot(u, mt_ref[j].astype(BF16)) + _dot_nt(s_re.astype(BF16), qr_ref[j].astype(BF16))
                      + _dot_nt(s_im.astype(BF16), qi_ref[j].astype(BF16)))
        _store_groups(y_ref, ys, nc)

    blk = pl.BlockSpec((t, 128), lambda j: (0, j))
    return pl.pallas_call(
        body, grid=(SSM_GROUPS // SSM_GPB,), in_specs=[blk] + [_bspec(s) for s in _SSM_GEN_OUT], out_specs=blk,
        out_shape=S((t, 1024), F32), name="ssm_core_fwd", compiler_params=_cp(("parallel",)))(proj, *ops)


def _ssm_core_bwd(proj, dy, ops):
    t = proj.shape[0]
    nc = t // SSM_CHUNK

    def body(u_ref, dy_ref, mt_ref, pr_ref, pi_ref, qr_ref, qi_ref, ar_ref, ai_ref,
             du_ref, dmt_ref, dpr_ref, dpi_ref, dqr_ref, dqi_ref, dar_ref, dai_ref):
        u_rows = _step_rows(u_ref, nc)
        dy_rows = _step_rows(dy_ref, nc)
        row = lax.broadcasted_iota(jnp.int32, (nc, SSM_STATE), 0)
        dus = []
        for j in range(SSM_GPB):
            u = _group_of(u_rows, j)
            dyj = _group_of(dy_rows, j)
            ar, ai = ar_ref[j], ai_ref[j]
            s_re, s_im = _ssm_states(u, pr_ref[j], pi_ref[j], ar, ai, row)
            g_re = _shift_rows(_dot(dyj, qr_ref[j].astype(BF16)), 1, row, True)
            g_im = _shift_rows(_dot(dyj, qi_ref[j].astype(BF16)), 1, row, True)
            l_re, l_im = _carry_scan(g_re, g_im, ar, -ai, row, True)
            lrb, lib = l_re.astype(BF16), l_im.astype(BF16)
            dus.append(_dot_nt(dyj, mt_ref[j].astype(BF16)) + _dot_nt(lrb, pr_ref[j].astype(BF16))
                       + _dot_nt(lib, pi_ref[j].astype(BF16)))
            dmt_ref[j] = _dot_tn(u, dyj)
            dpr_ref[j] = _dot_tn(u, lrb)
            dpi_ref[j] = _dot_tn(u, lib)
            dqr_ref[j] = _dot_tn(dyj, s_re.astype(BF16))
            dqi_ref[j] = _dot_tn(dyj, s_im.astype(BF16))
            dar_ref[j] = jnp.sum(l_re * s_re + l_im * s_im, axis=0, keepdims=True)
            dai_ref[j] = jnp.sum(l_im * s_re - l_re * s_im, axis=0, keepdims=True)
        _store_groups(du_ref, dus, nc)

    blk = pl.BlockSpec((t, 128), lambda j: (0, j))
    return pl.pallas_call(
        body, grid=(SSM_GROUPS // SSM_GPB,), in_specs=[blk, blk] + [_bspec(s) for s in _SSM_GEN_OUT],
        out_specs=[blk] + [_bspec(s) for s in _SSM_GEN_OUT],
        out_shape=[S((t, 1024), F32)] + [S((SSM_GROUPS,) + s, F32) for s in _SSM_GEN_OUT],
        name="ssm_core_bwd", compiler_params=_cp(("parallel",)))(proj, dy, *ops)


def _s5_act(y_core, proj, d_skip):
    t = y_core.shape[0]
    tr = 256

    def body(y_ref, u_ref, d_ref, o_ref):
        o_ref[...] = _gelu(y_ref[...] + d_ref[...] * u_ref[...])

    return pl.pallas_call(body, grid=(t // tr,), in_specs=[_row_spec(tr, 1024), _row_spec(tr, 1024), _full_spec((1, 1024))],
                          out_specs=_row_spec(tr, 1024), out_shape=S((t, 1024), F32), name="s5_act",
                          compiler_params=_cp(("parallel",)))(y_core, proj, d_skip)


def _s5_gate(yg, tg, b_glu, proj):
    t = yg.shape[0]
    tr = 256

    def body(y_ref, t_ref, b_ref, z_ref, o_ref):
        o_ref[...] = (y_ref[...] * jax.nn.sigmoid(t_ref[...] + b_ref[...]) * _silu(z_ref[...])).astype(BF16)

    return pl.pallas_call(
        body, grid=(t // tr,),
        in_specs=[_row_spec(tr, 1024), _row_spec(tr, 1024), _full_spec((1, 1024)), pl.BlockSpec((tr, 1024), lambda i: (i, 1))],
        out_specs=_row_spec(tr, 1024), out_shape=S((t, 1024), BF16), name="s5_gate",
        compiler_params=_cp(("parallel",)))(yg, tg, b_glu, proj)


def _s5_gate_bwd(dp, yg, tg, b_glu, proj):
    t = yg.shape[0]
    tr = 256

    def body(dp_ref, y_ref, t_ref, b_ref, z_ref, dt_ref, dy_ref, dz_ref, db_ref):
        @pl.when(pl.program_id(0) == 0)
        def _():
            db_ref[...] = jnp.zeros_like(db_ref)

        z = z_ref[...]
        yv = y_ref[...]
        sg = jax.nn.sigmoid(t_ref[...] + b_ref[...])
        dpv = dp_ref[...].astype(F32)
        do = dpv * _silu(z)
        dz_ref[...] = (dpv * yv * sg * _silu_grad(z)).astype(BF16)
        dy_ref[...] = do * sg
        dtv = do * yv * sg * (1.0 - sg)
        dt_ref[...] = dtv.astype(BF16)
        db_ref[...] += jnp.sum(dtv, axis=0, keepdims=True)

    return pl.pallas_call(
        body, grid=(t // tr,),
        in_specs=[_row_spec(tr, 1024), _row_spec(tr, 1024), _row_spec(tr, 1024), _full_spec((1, 1024)),
                  pl.BlockSpec((tr, 1024), lambda i: (i, 1))],
        out_specs=[_row_spec(tr, 1024), _row_spec(tr, 1024), _row_spec(tr, 1024), _full_spec((1, 1024))],
        out_shape=[S((t, 1024), BF16), S((t, 1024), F32), S((t, 1024), BF16), S((1, 1024), F32)],
        name="s5_gate_bwd", compiler_params=_cp(("arbitrary",)))(dp, yg, tg, b_glu, proj)


def _s5_act_bwd(dyg1, dyg2, y_core, proj, d_skip):
    t = y_core.shape[0]
    tr = 256

    def body(a_ref, b_ref, y_ref, u_ref, d_ref, dy_ref, du_ref, dd_ref):
        @pl.when(pl.program_id(0) == 0)
        def _():
            dd_ref[...] = jnp.zeros_like(dd_ref)

        u = u_ref[...]
        dv = (a_ref[...] + b_ref[...]) * _gelu_grad(y_ref[...] + d_ref[...] * u)
        dy_ref[...] = dv
        du_ref[...] = dv * d_ref[...]
        dd_ref[...] += jnp.sum(dv * u, axis=0, keepdims=True)

    return pl.pallas_call(
        body, grid=(t // tr,),
        in_specs=[_row_spec(tr, 1024)] * 4 + [_full_spec((1, 1024))],
        out_specs=[_row_spec(tr, 1024), _row_spec(tr, 1024), _full_spec((1, 1024))],
        out_shape=[S((t, 1024), F32), S((t, 1024), F32), S((1, 1024), F32)],
        name="s5_act_bwd", compiler_params=_cp(("arbitrary",)))(dyg1, dyg2, y_core, proj, d_skip)


def _s5_dproj(du_skip, du_core, dz):
    t = du_skip.shape[0]
    tr = 256

    def body(a_ref, b_ref, z_ref, o_ref):
        o_ref[:, 0:1024] = (a_ref[...] + b_ref[...]).astype(BF16)
        o_ref[:, 1024:2048] = z_ref[...]

    return pl.pallas_call(body, grid=(t // tr,), in_specs=[_row_spec(tr, 1024)] * 3, out_specs=_row_spec(tr, 2048),
                          out_shape=S((t, 2048), BF16), name="s5_dproj",
                          compiler_params=_cp(("parallel",)))(du_skip, du_core, dz)


def _s5_params(lam_re, lam_im, log_dt, b_re, b_im, c_re, c_im):
    g = SSM_GROUPS
    return (lam_re.reshape(g, 1, SSM_STATE), lam_im.reshape(g, 1, SSM_STATE),
            jnp.broadcast_to(log_dt.reshape(g, 1, 1), (g, 1, SSM_STATE)),
            b_re.transpose(0, 2, 1), b_im.transpose(0, 2, 1), c_re, c_im)


def _s5_fwd(proj, params, d_skip, w_glu, b_glu):
    ops = _ssm_operators(params)
    y_core = _ssm_core_fwd(proj, ops)
    yg = _s5_act(y_core, proj, d_skip)
    tg = _mm(yg, w_glu, name="s5_glu")
    return _s5_gate(yg, tg, b_glu, proj), (ops, y_core, yg, tg)


def _s5_bwd(proj, res, dp, params, d_skip, w_glu, b_glu):
    ops, y_core, yg, tg = res
    dtg, dyg1, dz, db_glu = _s5_gate_bwd(dp, yg, tg, b_glu, proj)
    dyg2 = _mm(dtg, w_glu, tb=True, name="s5_glu_dx")
    dw_glu = _mm(yg, dtg, ta=True, out_dtype=BF16, name="s5_glu_dw")
    dy_core, du_skip, dd = _s5_act_bwd(dyg1, dyg2, y_core, proj, d_skip)
    outs = _ssm_core_bwd(proj, dy_core, ops)
    dproj = _s5_dproj(du_skip, outs[0], dz)
    return dproj, outs[1:], dd, dw_glu, db_glu


def _s5_param_grads(params, cots, after=()):
    glr, gli, gdt, gbtr, gbti, gcr, gci = _ssm_operators_bwd(params, cots, after)
    g = SSM_GROUPS
    return (glr.reshape(g, SSM_STATE), gli.reshape(g, SSM_STATE), gdt[:, 0, 0].reshape(1, g),
            gbtr.transpose(0, 2, 1), gbti.transpose(0, 2, 1), gcr, gci)


def _perm_uq(w):
    w3 = w.reshape(w.shape[0], MLA_HEADS, MLA_NOPE + MLA_ROPE)
    return jnp.concatenate([w3[:, :, :64].reshape(-1, 1024), w3[:, :, 64:80].reshape(-1, 256),
                            w3[:, :, 80:96].reshape(-1, 256)], axis=1)


def _unperm_uq(w):
    r = w.shape[0]
    return jnp.concatenate([w[:, :1024].reshape(r, 16, 64), w[:, 1024:1280].reshape(r, 16, 16),
                            w[:, 1280:1536].reshape(r, 16, 16)], axis=2).reshape(r, 1536)


def _perm_ukv(w):
    w3 = w.reshape(w.shape[0], MLA_HEADS, 128)
    return jnp.concatenate([w3[:, :, :64].reshape(-1, 1024), w3[:, :, 64:].reshape(-1, 1024)], axis=1)


def _unperm_ukv(w):
    r = w.shape[0]
    return jnp.concatenate([w[:, :1024].reshape(r, 16, 64), w[:, 1024:].reshape(r, 16, 64)], axis=2).reshape(r, 2048)


def _mla_fwd(proj, gq, gkv, wuq, wukv, cos, sin):
    cqn, ckvn = _mla_norms(proj, gq, gkv)
    qf = _mm(cqn, wuq, name="mla_uq")
    kvf = _mm(ckvn, wukv, name="mla_ukv")
    qh, kh = _mla_pack(qf, kvf, proj, cos, sin)
    o, pout, lse = _mla_attn_fwd(qh, kh, kvf, proj)
    return pout, (cqn, ckvn, kvf, qh, kh, o, lse)


def _mla_bwd(proj, res, dpout, gq, gkv, wuq, wukv, cos, sin):
    cqn, ckvn, kvf, qh, kh, o, lse = res
    dqh, dkh, dv, dz = _mla_attn_bwd(qh, kh, kvf, proj, o, dpout, lse)
    dqf, dkvf, dkr = _mla_unpack(dqh, dkh, dv, cos, sin)
    dcqn = _mm(dqf, wuq, tb=True, name="mla_uq_dx")
    dwuq = _mm(cqn, dqf, ta=True, out_dtype=BF16, name="mla_uq_dw")
    dckvn = _mm(dkvf, wukv, tb=True, name="mla_ukv_dx")
    dwukv = _mm(ckvn, dkvf, ta=True, out_dtype=BF16, name="mla_ukv_dw")
    dproj, dgq, dgkv = _mla_norms_bwd(proj, dcqn, dckvn, dkr, dz, gq, gkv)
    return dproj, dgq, dgkv, dwuq, dwukv


_ANY = pl.BlockSpec(memory_space=pl.ANY)


def _chip_peers():
    x, y, c = lax.axis_index("x"), lax.axis_index("y"), lax.axis_index("c")
    return 2 * x + y, c, [(1 - x, y), (x, 1 - y), (1 - x, 1 - y)]


def _exchange_chips(src_of, dst_of, send_sems, recv_sems, loc_sem):
    me, c, peers = _chip_peers()
    loc = pltpu.make_async_copy(src_of(me), dst_of(me), loc_sem)
    loc.start()

    def copy(k, px, py, src_chip, dst_chip):
        return pltpu.make_async_remote_copy(src_ref=src_of(src_chip), dst_ref=dst_of(dst_chip), send_sem=send_sems.at[k],
                                            recv_sem=recv_sems.at[k], device_id=(px, py, c), device_id_type=MESH)

    sends = [copy(k, px, py, 2 * px + py, me) for k, (px, py) in enumerate(peers)]
    for cp in sends:
        cp.start()
    for k, (px, py) in enumerate(peers):
        copy(k, px, py, me, 2 * px + py).wait_recv()
    for cp in sends:
        cp.wait_send()
    loc.wait()


_EXCH_SCRATCH = [pltpu.SemaphoreType.DMA((3,)), pltpu.SemaphoreType.DMA((3,)), pltpu.SemaphoreType.DMA(())]


def _all_gather_chips(xs, name):
    def body(x_ref, o_ref, send_sems, recv_sems, loc_sem):
        _exchange_chips(lambda s: x_ref, lambda s: o_ref.at[s], send_sems, recv_sems, loc_sem)

    return pl.pallas_call(body, in_specs=[_ANY], out_specs=_ANY, out_shape=S((N_CHIPS,) + xs.shape, xs.dtype),
                          scratch_shapes=_EXCH_SCRATCH, name=name)(xs)


def _sibling_swap(ps, name):
    n = len(ps)

    def body(*refs):
        p_refs, q_refs, send_sems, recv_sems = refs[:n], refs[n:2 * n], refs[2 * n], refs[2 * n + 1]
        x, y, c = lax.axis_index("x"), lax.axis_index("y"), lax.axis_index("c")
        cps = [pltpu.make_async_remote_copy(src_ref=p_refs[i], dst_ref=q_refs[i], send_sem=send_sems.at[i],
                                            recv_sem=recv_sems.at[i], device_id=(x, y, 1 - c), device_id_type=MESH)
               for i in range(n)]
        for cp in cps:
            cp.start()
        for cp in cps:
            cp.wait()

    return pl.pallas_call(body, in_specs=[_ANY] * n, out_specs=[_ANY] * n, out_shape=[S(p.shape, p.dtype) for p in ps],
                          scratch_shapes=[pltpu.SemaphoreType.DMA((n,)), pltpu.SemaphoreType.DMA((n,))], name=name)(*ps)


def _quarter_spec(tr, r, c, dim):
    if dim is None:
        return pl.BlockSpec((1, tr, c), lambda i, me: (me[0], i, 0))
    if dim == 0:
        return pl.BlockSpec((tr, c), lambda i, me: (me[0] * (r // tr) + i, 0))
    return pl.BlockSpec((tr, c), lambda i, me: (i, me[0]))


def _whole_shape(r, c, dim):
    return (N_CHIPS, r, c) if dim is None else ((N_CHIPS * r, c) if dim == 0 else (r, N_CHIPS * c))


def _row_tile(r):
    return 256 if r % 256 == 0 else (128 if r % 128 == 0 else r)


def _cast_place(me, x, dim, dtype, name):
    r, c = x.shape
    tr = _row_tile(r)

    def body(me_ref, x_ref, o_ref):
        o_ref[...] = x_ref[...].astype(dtype).reshape(o_ref.shape)

    return pl.pallas_call(
        body, out_shape=S(_whole_shape(r, c, dim), dtype), name=name,
        grid_spec=pltpu.PrefetchScalarGridSpec(num_scalar_prefetch=1, grid=(r // tr,),
                                               in_specs=[pl.BlockSpec((tr, c), lambda i, me: (i, 0))],
                                               out_specs=_quarter_spec(tr, r, c, dim)),
        compiler_params=_cp(("parallel",)))(me, x)


def _sum_parts(me, g, land, dim, name):
    _, r, c = land.shape
    tr = _row_tile(r)

    def body(me_ref, g_ref, l_ref, o_ref):
        own = g_ref[...].astype(F32).reshape(tr, c)
        o_ref[...] = ((own + l_ref[0].astype(F32)) + l_ref[1].astype(F32)) + l_ref[2].astype(F32)

    return pl.pallas_call(
        body, out_shape=S((r, c), F32), name=name,
        grid_spec=pltpu.PrefetchScalarGridSpec(
            num_scalar_prefetch=1, grid=(r // tr,),
            in_specs=[_quarter_spec(tr, r, c, dim), pl.BlockSpec((3, tr, c), lambda i, me: (0, i, 0))],
            out_specs=pl.BlockSpec((tr, c), lambda i, me: (i, 0))),
        compiler_params=_cp(("parallel",)))(me, g, land)


_HBM = pl.BlockSpec(memory_space=pltpu.HBM)
_SEM = pl.BlockSpec(memory_space=pltpu.SEMAPHORE)
_EFFECT = pltpu.SideEffectType.DATAFLOW_SIDE_EFFECTING


def _quarter(ref, s, dim):
    n = ref.shape[dim] // N_CHIPS
    start = pl.multiple_of(s * n, n)
    return ref.at[pl.ds(start, n), :] if dim == 0 else ref.at[:, pl.ds(start, n)]


def _part(ref, s, dim):
    return ref.at[s] if dim is None else _quarter(ref, s, dim)


def _gather_plan(i, dim):
    return dict(src=i, dst=i, send_src=lambda ref, me, peer, k: _part(ref, me, dim),
                send_dst=lambda ref, me, peer, k: _part(ref, me, dim), recv_dst=lambda ref, me, peer, k: _part(ref, peer, dim))


def _scatter_plan(i, j, dim):
    return dict(src=i, dst=j, send_src=lambda ref, me, peer, k: _part(ref, peer, dim),
                send_dst=lambda ref, me, peer, k: ref.at[k], recv_dst=lambda ref, me, peer, k: ref.at[k])


def _hbm(a):
    return pltpu.with_memory_space_constraint(a, pltpu.HBM)


def _remote_copies(refs, send_sems, recv_sems, plans):
    me, c, peers = _chip_peers()
    sends, recvs = [], []
    for i, p in enumerate(plans):
        for k, (px, py) in enumerate(peers):
            peer = 2 * px + py
            sems = dict(send_sem=send_sems.at[3 * i + k], recv_sem=recv_sems.at[3 * i + k], device_id=(px, py, c),
                        device_id_type=MESH)
            src = p['send_src'](refs[p['src']], me, peer, k)
            sends.append(pltpu.make_async_remote_copy(src_ref=src, dst_ref=p['send_dst'](refs[p['dst']], me, peer, k), **sems))
            recvs.append(pltpu.make_async_remote_copy(src_ref=src, dst_ref=p['recv_dst'](refs[p['dst']], me, peer, k), **sems))
    return sends, recvs


def _exchange_start(arrays, plans, name, after=()):
    n, m, extra = len(arrays), len(plans), len(after)

    def body(*refs):
        send_sems, recv_sems, token = refs[n + extra], refs[n + extra + 1], refs[-1]
        sends, _ = _remote_copies(refs[:n], send_sems, recv_sems, plans)
        for cp in sends:
            cp.start()
        token[...] = jnp.zeros_like(token)

    outs = pl.pallas_call(
        body, name=name,
        out_shape=(pltpu.SemaphoreType.DMA((3 * m,)), pltpu.SemaphoreType.DMA((3 * m,)),
                   *[pltpu.HBM(a.shape, a.dtype) for a in arrays], S((8, 128), F32)),
        in_specs=[_HBM] * n + [_ANY] * extra, out_specs=(_SEM, _SEM, *[_HBM] * n, pl.BlockSpec(memory_space=pltpu.VMEM)),
        input_output_aliases={i: 2 + i for i in range(n)},
        compiler_params=pltpu.CompilerParams(has_side_effects=_EFFECT))(*[_hbm(a) for a in arrays], *after)
    return outs[0], outs[1], outs[2:2 + n], outs[-1]


def _exchange_wait(started, plans, after, name):
    send_sems, recv_sems, arrays, _ = started
    n = len(arrays)

    def body(*refs):
        sends, recvs = _remote_copies(refs[:n], refs[n], refs[n + 1], plans)
        for cp in sends:
            cp.wait_send()
        for cp in recvs:
            cp.wait_recv()

    return pl.pallas_call(
        body, name=name, out_shape=tuple(pltpu.HBM(a.shape, a.dtype) for a in arrays),
        in_specs=[_HBM] * n + [_SEM, _SEM, _ANY], out_specs=tuple([_HBM] * n),
        input_output_aliases={i: i for i in range(n)},
        compiler_params=pltpu.CompilerParams(has_side_effects=_EFFECT))(*arrays, send_sems, recv_sems, after)


def _add2(p, q):
    r, c = p.shape
    tr = 128 if r % 128 == 0 else r

    def body(p_ref, q_ref, o_ref):
        o_ref[...] = p_ref[...] + q_ref[...]

    return pl.pallas_call(body, grid=(r // tr,), in_specs=[_row_spec(tr, c), _row_spec(tr, c)], out_specs=_row_spec(tr, c),
                          out_shape=S((r, c), F32), name="add_cores", compiler_params=_cp(("parallel",)))(p, q)


def _adamw(w, g, m, v, name, g2=None):
    r, c = w.shape
    tr = 256 if r % 256 == 0 else (128 if r % 128 == 0 else r)
    c1 = 1.0 / (1.0 - ADAM_B1 ** ADAM_STEP)
    c2 = 1.0 / (1.0 - ADAM_B2 ** ADAM_STEP)
    gs = [g] if g2 is None else [g, g2]

    def body(*refs):
        w_ref, m_ref, v_ref = refs[0], refs[1], refs[2]
        g_refs = refs[3:3 + len(gs)]
        go_ref, d_ref, nm_ref, nv_ref = refs[3 + len(gs):]
        gv = g_refs[0][...]
        if len(gs) == 2:
            gv = gv + g_refs[1][...]
        go_ref[...] = gv
        nm = ADAM_B1 * m_ref[...] + (1.0 - ADAM_B1) * gv
        nv = ADAM_B2 * v_ref[...] + (1.0 - ADAM_B2) * (gv * gv)
        nm_ref[...] = nm
        nv_ref[...] = nv
        d_ref[...] = -ADAM_LR * ((nm * c1) / (jnp.sqrt(nv * c2) + ADAM_EPS) + ADAM_WD * w_ref[...])

    return pl.pallas_call(body, grid=(r // tr,), in_specs=[_row_spec(tr, c)] * (3 + len(gs)),
                          out_specs=[_row_spec(tr, c)] * 4, out_shape=[S((r, c), F32)] * 4, name=name,
                          compiler_params=_cp(("parallel",)))(w, m, v, *gs)


_WEIGHTS = ['pre_norm', 'post_norm', 'rel_bias', 'a_w_in', 'a_lam_re', 'a_lam_im', 'a_log_dt', 'a_b_re', 'a_b_im',
            'a_c_re', 'a_c_im', 'a_d', 'a_w_glu', 'a_b_glu', 'a_w_out', 'b_w_in', 'b_sinks', 'b_w_out', 'c_w_in',
            'c_q_norm', 'c_kv_norm', 'c_w_uq', 'c_w_ukv', 'c_w_out', 'd_w_in', 'd_ln_g', 'd_ln_b', 'd_w_s', 'd_b_s',
            'd_w_out']
_BIG = [('a_w_in', 1024, 2048, 1), ('a_w_glu', 1024, 1024, 0), ('a_w_out', 1024, 1024, 0), ('b_w_in', 1024, 2304, 1),
        ('b_w_out', 1024, 1024, 0), ('c_w_in', 1024, 2080, 1), ('c_w_uq', 768, 1536, 1), ('c_w_ukv', 256, 2048, 1),
        ('c_w_out', 1024, 1024, 0), ('d_w_in', 1024, 3072, 1), ('d_w_out', 1024, 1024, 0)]
_SHARDED_VECS = [('c_q_norm', 768), ('c_kv_norm', 256), ('d_ln_g', 1024), ('d_ln_b', 1024)]
_REPLICATED = [n for n in _WEIGHTS if n not in [b[0] for b in _BIG] and n not in [s[0] for s in _SHARDED_VECS]]
_LANES = 1024
_SMALL_Q_ROWS = 136


def kernel(x, pre_norm, post_norm, rel_bias, a_w_in, a_lam_re, a_lam_im, a_log_dt, a_b_re, a_b_im, a_c_re, a_c_im, a_d, a_w_glu, a_b_glu, a_w_out, b_w_in, b_sinks, b_w_out, c_w_in, c_q_norm, c_kv_norm, c_w_uq, c_w_ukv, c_w_out, d_w_in, d_ln_g, d_ln_b, d_w_s, d_b_s, d_w_out, loss_target, m_pre_norm, m_post_norm, m_rel_bias, m_a_w_in, m_a_lam_re, m_a_lam_im, m_a_log_dt, m_a_b_re, m_a_b_im, m_a_c_re, m_a_c_im, m_a_d, m_a_w_glu, m_a_b_glu, m_a_w_out, m_b_w_in, m_b_sinks, m_b_w_out, m_c_w_in, m_c_q_norm, m_c_kv_norm, m_c_w_uq, m_c_w_ukv, m_c_w_out, m_d_w_in, m_d_ln_g, m_d_ln_b, m_d_w_s, m_d_b_s, m_d_w_out, v_pre_norm, v_post_norm, v_rel_bias, v_a_w_in, v_a_lam_re, v_a_lam_im, v_a_log_dt, v_a_b_re, v_a_b_im, v_a_c_re, v_a_c_im, v_a_d, v_a_w_glu, v_a_b_glu, v_a_w_out, v_b_w_in, v_b_sinks, v_b_w_out, v_c_w_in, v_c_q_norm, v_c_kv_norm, v_c_w_uq, v_c_w_ukv, v_c_w_out, v_d_w_in, v_d_ln_g, v_d_ln_b, v_d_w_s, v_d_b_s, v_d_w_out):
    arg = dict(locals())
    me = 2 * lax.axis_index("x") + lax.axis_index("y")
    xin = x[0]
    t = xin.shape[0]

    big = {n: (r, c, dim) for n, r, c, dim in _BIG}

    me1 = me.reshape(1).astype(jnp.int32)

    def gather_start(names, tag, after):
        wholes, plans = [], []
        for i, n in enumerate(names):
            if n in big:
                r, c, dim = big[n]
                src, dtype = arg[n][0], BF16
                if dim == 1 and (c // N_CHIPS) % 128:
                    dim = None
            else:
                src, dim, dtype = arg[n], None, F32
            wholes.append(_cast_place(me1, src, dim, dtype, "place_" + n))
            plans.append(_gather_plan(i, dim))
        return _exchange_start(wholes, plans, "gather_start_" + tag, after=after), plans

    def gather_wait(names, started, after, tag):
        st, plans = started
        out = {}
        for n, a in zip(names, _exchange_wait(st, plans, after, "gather_wait_" + tag)):
            if n in big:
                r, c, dim = big[n]
                out[n] = a if a.ndim == 2 else a.transpose(1, 0, 2).reshape(r, c)
            else:
                out[n] = a.reshape(1, -1)
        return out

    groups = {'a1': ['a_w_in'], 'a2': ['a_w_glu', 'a_w_out'], 'b': ['b_w_in', 'b_w_out'],
              'c': ['c_w_in', 'c_w_uq', 'c_w_ukv', 'c_w_out', 'c_q_norm', 'c_kv_norm'],
              'd': ['d_w_in', 'd_w_out', 'd_ln_g', 'd_ln_b']}
    started, tokens = {}, []
    for tag, names in groups.items():
        started[tag] = gather_start(names, tag, tokens)
        tokens = [started[tag][0][3]]
    wt = gather_wait(groups['a1'], started['a1'], xin, 'a1')
    cos, sin = _rope_tables(t)
    bucket = jnp.asarray(_t5_bucket_table())
    s5p = _s5_params(a_lam_re[0], a_lam_im[0], a_log_dt[0], a_b_re[0], a_b_im[0], a_c_re[0], a_c_im[0])
    bst = d_b_s[0].T

    x0 = xin
    h0 = _prenorm(x0, pre_norm[0:1], "pre0", after=tokens)
    proj0 = _mm(h0, wt['a_w_in'], name="a_in")
    wt.update(gather_wait(groups['a2'], started['a2'], proj0, 'a2'))
    p0, res0 = _s5_fwd(proj0, s5p, a_d, wt['a_w_glu'], a_b_glu)
    y0 = _mm(p0, wt['a_w_out'], name="a_out")
    x1 = _postnorm(x0, y0, post_norm[0:1], "post0")

    wt.update(gather_wait(groups['b'], started['b'], x1, 'b'))
    h1 = _prenorm(x1, pre_norm[1:2], "pre1")
    proj1 = _mm(h1, wt['b_w_in'], name="b_in")
    bias = _swa_bias(rel_bias, bucket)
    p1 = _swa_fwd(proj1, bias, b_sinks)
    y1 = _mm(p1, wt['b_w_out'], name="b_out")
    x2 = _postnorm(x1, y1, post_norm[1:2], "post1")

    wt.update(gather_wait(groups['c'], started['c'], x2, 'c'))
    w_c_in = jnp.concatenate([wt['c_w_in'][:, :1056], jnp.zeros((1024, 96), BF16), wt['c_w_in'][:, 1056:],
                              jnp.zeros((1024, MLA_PROJ - MLA_Z1), BF16)], axis=1)
    w_uq = _perm_uq(wt['c_w_uq'])
    w_ukv = _perm_ukv(wt['c_w_ukv'])
    h2 = _prenorm(x2, pre_norm[2:3], "pre2")
    proj2 = _mm(h2, w_c_in, name="c_in")
    p2, res2 = _mla_fwd(proj2, wt['c_q_norm'], wt['c_kv_norm'], w_uq, w_ukv, cos, sin)
    y2 = _mm(p2, wt['c_w_out'], name="c_out")
    x3 = _postnorm(x2, y2, post_norm[2:3], "post2")

    wt.update(gather_wait(groups['d'], started['d'], x3, 'd'))
    vec = wt
    h3 = _prenorm(x3, pre_norm[3:4], "pre3")
    proj3 = _mm(h3, wt['d_w_in'], name="d_in")
    p3 = _sgu_fwd(proj3, vec['d_ln_g'], vec['d_ln_b'], d_w_s[0], bst)
    y3 = _mm(p3, wt['d_w_out'], name="d_out")
    x4 = _postnorm(x3, y3, post_norm[3:4], "post3")

    loss_part, dx = _loss_head(x4, loss_target[0])
    loss = lax.psum(loss_part[0, 0], ("x", "y", "c"))

    gr = {}
    d_pre, d_post = [None] * 4, [None] * 4
    scattered = []

    def scatter_start(items, tag):
        srcs = [a for _, a, _ in items]
        n = len(items)
        plans = [_scatter_plan(i, n + i, dim) for i, (_, _, dim) in enumerate(items)]
        lands = []
        for _, a, dim in items:
            part = a.shape[1:] if dim is None else tuple(s // N_CHIPS if i == dim else s for i, s in enumerate(a.shape))
            lands.append(lax.empty((3,) + part, a.dtype))
        st = _exchange_start(srcs + lands, plans, "scatter_start_" + tag)
        scattered.append((st, plans, items, tag))
        return [st[3]]

    def stacked(g):
        r, c = g.shape
        return g.reshape(r, N_CHIPS, c // N_CHIPS).transpose(1, 0, 2)

    dy, d_post[3] = _postnorm_bwd(dx, y3, post_norm[3:4], "post3_bwd")
    dp = _mm(dy, wt['d_w_out'], tb=True, name="d_out_dx")
    g_out = _mm(p3, dy, ta=True, out_dtype=BF16, name="d_out_dw")
    dproj, gr['d_w_s'], dbst, gr['d_ln_g'], gr['d_ln_b'] = _sgu_bwd(proj3, dp, vec['d_ln_g'], vec['d_ln_b'], d_w_s[0], bst)
    gr['d_b_s'] = dbst.T
    dh = _mm(dproj, wt['d_w_in'], tb=True, name="d_in_dx")
    g_in = _mm(h3, dproj, ta=True, out_dtype=BF16, name="d_in_dw")
    tok = scatter_start([('d_w_in', g_in, 1), ('d_w_out', g_out, 0)], 'd')
    dx, d_pre[3] = _prenorm_bwd(dx, dh, x3, pre_norm[3:4], "pre3_bwd")

    dy, d_post[2] = _postnorm_bwd(dx, y2, post_norm[2:3], "post2_bwd", after=tok)
    dp = _mm(dy, wt['c_w_out'], tb=True, name="c_out_dx")
    g_out = _mm(p2, dy, ta=True, out_dtype=BF16, name="c_out_dw")
    dproj, gr['c_q_norm'], gr['c_kv_norm'], dwuq, dwukv = _mla_bwd(proj2, res2, dp, wt['c_q_norm'], wt['c_kv_norm'],
                                                                  w_uq, w_ukv, cos, sin)
    dh = _mm(dproj, w_c_in, tb=True, name="c_in_dx")
    dwc = _mm(h2, dproj, ta=True, out_dtype=BF16, name="c_in_dw")
    g_in = stacked(jnp.concatenate([dwc[:, :1056], dwc[:, MLA_Z0:MLA_Z1]], axis=1))
    tok = scatter_start([('c_w_in', g_in, None), ('c_w_uq', _unperm_uq(dwuq), 1), ('c_w_ukv', _unperm_ukv(dwukv), 1),
                         ('c_w_out', g_out, 0)], 'c')
    dx, d_pre[2] = _prenorm_bwd(dx, dh, x2, pre_norm[2:3], "pre2_bwd")

    dy, d_post[1] = _postnorm_bwd(dx, y1, post_norm[1:2], "post1_bwd", after=tok)
    dp = _mm(dy, wt['b_w_out'], tb=True, name="b_out_dx")
    g_out = _mm(p1, dy, ta=True, out_dtype=BF16, name="b_out_dw")
    dqz, dkv, dss, dsk = _swa_bwd(proj1, dp, bias, b_sinks)
    gr['rel_bias'], gr['b_sinks'] = _swa_dbias(dss, dsk, bucket)
    dproj = jnp.concatenate([dqz[:, :1024], dkv.astype(BF16), dqz[:, 1024:]], axis=1)
    dh = _mm(dproj, wt['b_w_in'], tb=True, name="b_in_dx")
    g_in = stacked(_mm(h1, dproj, ta=True, out_dtype=BF16, name="b_in_dw"))
    tok = scatter_start([('b_w_in', g_in, None), ('b_w_out', g_out, 0)], 'b')
    dx, d_pre[1] = _prenorm_bwd(dx, dh, x1, pre_norm[1:2], "pre1_bwd")

    dy, d_post[0] = _postnorm_bwd(dx, y0, post_norm[0:1], "post0_bwd", after=tok)
    dp = _mm(dy, wt['a_w_out'], tb=True, name="a_out_dx")
    g_out = _mm(p0, dy, ta=True, out_dtype=BF16, name="a_out_dw")
    dproj, s5cots, gr['a_d'], g_glu, gr['a_b_glu'] = _s5_bwd(proj0, res0, dp, s5p, a_d, wt['a_w_glu'], a_b_glu)
    tok = scatter_start([('a_w_glu', g_glu, 0), ('a_w_out', g_out, 0)], 'a2')
    g_in = _mm(h0, dproj, ta=True, out_dtype=BF16, name="a_in_dw", after=tok)
    tok = scatter_start([('a_w_in', g_in, 1)], 'a1')
    dh = _mm(dproj, wt['a_w_in'], tb=True, name="a_in_dx", after=tok)
    dx, d_pre[0] = _prenorm_bwd(dx, dh, x0, pre_norm[0:1], "pre0_bwd")
    s5g = _s5_param_grads(s5p, s5cots, after=tok)
    for n, g in zip(['a_lam_re', 'a_lam_im', 'a_log_dt', 'a_b_re', 'a_b_im', 'a_c_re', 'a_c_im'], s5g):
        gr[n] = g
    gr['pre_norm'] = jnp.concatenate(d_pre, axis=0)
    gr['post_norm'] = jnp.concatenate(d_post, axis=0)
    small_names = _REPLICATED + [n for n, _ in _SHARDED_VECS]
    small = jnp.concatenate([gr[n].reshape(-1) for n in small_names])
    small = jnp.pad(small, (0, N_CHIPS * _SMALL_Q_ROWS * _LANES - small.shape[0])).reshape(N_CHIPS, _SMALL_Q_ROWS, _LANES)
    scatter_start([('small', small, None)], 'a0')

    partial = {}
    for st, plans, items, tag in scattered:
        done = _exchange_wait(st, plans, small, "scatter_wait_" + tag)
        for i, (n, _, dim) in enumerate(items):
            partial[n] = _sum_parts(me1, done[i], done[len(items) + i], dim, "sum_parts_" + n)
    order = [n for n, _, _, _ in _BIG] + ['small']
    other = dict(zip(order, _sibling_swap([partial[n] for n in order], "swap_cores")))
    small_q = _add2(partial['small'], other['small'])
    small_all = _all_gather_chips(small_q, "gather_small_grads").reshape(-1)
    grads = {}
    off = 0
    for n in _REPLICATED:
        sz = math.prod(arg[n].shape)
        grads[n] = small_all[off:off + sz].reshape(arg[n].shape)
        off += sz
    for n, sz in _SHARDED_VECS:
        q = sz // N_CHIPS
        grads[n] = lax.dynamic_slice(small_all, (off + me * q,), (q,)).reshape(arg[n].shape)
        off += sz

    delta, new_m, new_v = {}, {}, {}
    for n, _, _, _ in _BIG:
        shp = arg[n].shape
        two = (shp[1], shp[2])
        outs = _adamw(arg[n].reshape(two), partial[n], arg['m_' + n].reshape(two), arg['v_' + n].reshape(two),
                      "adamw_" + n, g2=other[n])
        grads[n], delta[n], new_m[n], new_v[n] = [o.reshape(shp) for o in outs]

    for n in small_names:
        shp = arg[n].shape
        two = (math.prod(shp[:-1]), shp[-1])
        outs = _adamw(arg[n].reshape(two), grads[n].reshape(two), arg['m_' + n].reshape(two), arg['v_' + n].reshape(two),
                      "adamw_" + n)
        delta[n], new_m[n], new_v[n] = [o.reshape(shp) for o in outs[1:]]

    return (loss, dx.reshape(x.shape), *[grads[n] for n in _WEIGHTS], *[delta[n] for n in _WEIGHTS],
            *[new_m[n] for n in _WEIGHTS], *[new_v[n] for n in _WEIGHTS])
```

```python
import functools
import math

import numpy as np
import jax
import jax.numpy as jnp
from jax import lax
from jax.experimental import pallas as pl
from jax.experimental.pallas import tpu as pltpu

F32 = jnp.float32
BF16 = jnp.bfloat16
S = jax.ShapeDtypeStruct
MESH = pl.DeviceIdType.MESH

D_MODEL = 1024
EPS = 1e-6
NEG_INF = -1e30
N_CHIPS = 4

SSM_GROUPS = 64
SSM_GROUP = 16
SSM_STATE = 64
SSM_CHUNK = 16
SSM_BLOCK = SSM_CHUNK * SSM_GROUP

HEAD_DIM = 64
SWA_HEADS = 16
SWA_GROUP = 8
WINDOW = 128
REL_BUCKETS = 32
REL_MAX_DIST = 128

MLA_HEADS = 16
MLA_NOPE = 64
MLA_ROPE = 32
MLA_Q_RANK = 768
MLA_KV_RANK = 256
MLA_SCALE = (MLA_NOPE + MLA_ROPE) ** -0.5
MLA_BQ = 512
MLA_BK = 256
ROPE_BASE = 10000.0

SGU_CHUNK = 128
SGU_GROUPS = 16
SGU_GDIM = 64

ADAM_LR = 0.001
ADAM_B1 = 0.9
ADAM_B2 = 0.999
ADAM_EPS = 1e-08
ADAM_WD = 0.01
ADAM_STEP = 10

VMEM_BIG = 56 * 1024 * 1024


def _cp(sem, vmem=None):
    return pltpu.CompilerParams(dimension_semantics=sem, vmem_limit_bytes=vmem)


def _row_spec(tr, c):
    return pl.BlockSpec((tr, c), lambda i: (i, 0))


def _full_spec(shape):
    nd = len(shape)
    return pl.BlockSpec(shape, lambda *_: (0,) * nd)


def _gelu(x):
    c = 0.7978845608028654
    return 0.5 * x * (1.0 + jnp.tanh(c * (x + 0.044715 * x * x * x)))


def _gelu_grad(x):
    c = 0.7978845608028654
    t = jnp.tanh(c * (x + 0.044715 * x * x * x))
    return 0.5 * (1.0 + t) + 0.5 * x * (1.0 - t * t) * c * (1.0 + 3.0 * 0.044715 * x * x)


def _silu(z):
    return z * jax.nn.sigmoid(z)


def _silu_grad(z):
    s = jax.nn.sigmoid(z)
    return s * (1.0 + z * (1.0 - s))


def _dot(a, b):
    return lax.dot_general(a, b, (((1,), (0,)), ((), ())), preferred_element_type=F32)


def _dot_nt(a, b):
    return lax.dot_general(a, b, (((1,), (1,)), ((), ())), preferred_element_type=F32)


def _dot_tn(a, b):
    return lax.dot_general(a, b, (((0,), (0,)), ((), ())), preferred_element_type=F32)


def _pick(n, cap):
    best = 0
    for d in range(128, min(n, cap) + 1, 128):
        if n % d == 0:
            best = d
    if best < 256:
        return n
    return best


def _mm(a, b, *, ta=False, tb=False, out_dtype=F32, name, after=()):
    m, k = (a.shape[1], a.shape[0]) if ta else a.shape
    n = b.shape[0] if tb else b.shape[1]
    assert (b.shape[1] if tb else b.shape[0]) == k
    tm, tn, tk = _pick(m, 1024), _pick(n, 1024), _pick(k, 1024)
    if tn > 1024:
        tm = _pick(m, 256)
    if tk > 1024:
        tm, tn = _pick(m, 256), _pick(n, 256)
    nk = k // tk
    dims = (((0 if ta else 1,), (1 if tb else 0,)), ((), ()))

    def body(a_ref, b_ref, *rest):
        o_ref, acc_ref = rest[-2], rest[-1]
        kk = pl.program_id(2)

        @pl.when(kk == 0)
        def _():
            acc_ref[...] = jnp.zeros_like(acc_ref)

        acc_ref[...] += lax.dot_general(a_ref[...].astype(BF16), b_ref[...].astype(BF16), dims,
                                        preferred_element_type=F32)

        @pl.when(kk == nk - 1)
        def _():
            o_ref[...] = acc_ref[...].astype(out_dtype)

    a_spec = pl.BlockSpec((tk, tm), lambda i, j, kk: (kk, i)) if ta else pl.BlockSpec((tm, tk), lambda i, j, kk: (i, kk))
    b_spec = pl.BlockSpec((tn, tk), lambda i, j, kk: (j, kk)) if tb else pl.BlockSpec((tk, tn), lambda i, j, kk: (kk, j))
    return pl.pallas_call(
        body, grid=(m // tm, n // tn, nk), in_specs=[a_spec, b_spec] + [pl.BlockSpec(memory_space=pl.ANY)] * len(after),
        out_specs=pl.BlockSpec((tm, tn), lambda i, j, kk: (i, j)), out_shape=S((m, n), out_dtype),
        scratch_shapes=[pltpu.VMEM((tm, tn), F32)], name=name,
        compiler_params=_cp(("parallel", "parallel", "arbitrary"), VMEM_BIG))(a, b, *after)


def _prenorm(x, g, name, after=()):
    t, d = x.shape
    tr = 256

    def body(x_ref, g_ref, *rest):
        o_ref = rest[-1]
        xv = x_ref[...]
        r = lax.rsqrt(jnp.mean(xv * xv, axis=-1, keepdims=True) + EPS)
        o_ref[...] = (xv * r * g_ref[...]).astype(BF16)

    return pl.pallas_call(body, grid=(t // tr,),
                          in_specs=[_row_spec(tr, d), _full_spec((1, d))] + [pl.BlockSpec(memory_space=pl.ANY)] * len(after),
                          out_specs=_row_spec(tr, d), out_shape=S((t, d), BF16), name=name,
                          compiler_params=_cp(("parallel",)))(x, g, *after)


def _postnorm(x, y, g, name):
    t, d = x.shape
    tr = 256

    def body(x_ref, y_ref, g_ref, o_ref):
        yv = y_ref[...]
        r = lax.rsqrt(jnp.mean(yv * yv, axis=-1, keepdims=True) + EPS)
        o_ref[...] = x_ref[...] + yv * r * g_ref[...]

    return pl.pallas_call(body, grid=(t // tr,), in_specs=[_row_spec(tr, d), _row_spec(tr, d), _full_spec((1, d))],
                          out_specs=_row_spec(tr, d), out_shape=S((t, d), F32), name=name,
                          compiler_params=_cp(("parallel",)))(x, y, g)


def _rms_bwd_rows(dout, xin, g):
    r = lax.rsqrt(jnp.mean(xin * xin, axis=-1, keepdims=True) + EPS)
    xh = xin * r
    gd = dout * g
    dx = r * (gd - xh * jnp.mean(gd * xh, axis=-1, keepdims=True))
    return dx, jnp.sum(dout * xh, axis=0, keepdims=True)


def _postnorm_bwd(dxn, y, g, name, after=()):
    t, d = y.shape
    tr = 256

    def body(dx_ref, y_ref, g_ref, *rest):
        dy_ref, dg_ref = rest[-2], rest[-1]

        @pl.when(pl.program_id(0) == 0)
        def _():
            dg_ref[...] = jnp.zeros_like(dg_ref)

        dy, dg = _rms_bwd_rows(dx_ref[...], y_ref[...], g_ref[...])
        dy_ref[...] = dy.astype(BF16)
        dg_ref[...] += dg

    return pl.pallas_call(body, grid=(t // tr,),
                          in_specs=[_row_spec(tr, d), _row_spec(tr, d), _full_spec((1, d))]
                          + [pl.BlockSpec(memory_space=pl.ANY)] * len(after),
                          out_specs=[_row_spec(tr, d), _full_spec((1, d))],
                          out_shape=[S((t, d), BF16), S((1, d), F32)], name=name,
                          compiler_params=_cp(("arbitrary",)))(dxn, y, g, *after)


def _prenorm_bwd(dxn, dh, x, g, name):
    t, d = x.shape
    tr = 256

    def body(dxn_ref, dh_ref, x_ref, g_ref, dx_ref, dg_ref):
        @pl.when(pl.program_id(0) == 0)
        def _():
            dg_ref[...] = jnp.zeros_like(dg_ref)

        dx, dg = _rms_bwd_rows(dh_ref[...], x_ref[...], g_ref[...])
        dx_ref[...] = dxn_ref[...] + dx
        dg_ref[...] += dg

    return pl.pallas_call(body, grid=(t // tr,),
                          in_specs=[_row_spec(tr, d), _row_spec(tr, d), _row_spec(tr, d), _full_spec((1, d))],
                          out_specs=[_row_spec(tr, d), _full_spec((1, d))],
                          out_shape=[S((t, d), F32), S((1, d), F32)], name=name,
                          compiler_params=_cp(("arbitrary",)))(dxn, dh, x, g)


def _loss_head(y, target):
    t, d = y.shape
    tr = 256

    def body(y_ref, t_ref, l_ref, dy_ref):
        @pl.when(pl.program_id(0) == 0)
        def _():
            l_ref[...] = jnp.zeros_like(l_ref)

        e = y_ref[...] - t_ref[...]
        dy_ref[...] = e * (1.0 / d)
        l_ref[...] += 0.5 * jnp.sum(jnp.mean(e * e, axis=-1, keepdims=True), axis=0, keepdims=True)

    return pl.pallas_call(body, grid=(t // tr,), in_specs=[_row_spec(tr, d), _row_spec(tr, d)],
                          out_specs=[_full_spec((1, 1)), _row_spec(tr, d)],
                          out_shape=[S((1, 1), F32), S((t, d), F32)], name="loss_head",
                          compiler_params=_cp(("arbitrary",)))(y, target)


def _sgu_parts(proj, lg, lb, ws_ref, bst):
    a = proj[:, 0:1024]
    bb = proj[:, 1024:2048]
    z = proj[:, 2048:3072]
    u = _gelu(a)
    vp = _gelu(bb)
    mu = jnp.mean(vp, axis=-1, keepdims=True)
    xc = vp - mu
    rstd = lax.rsqrt(jnp.mean(xc * xc, axis=-1, keepdims=True) + EPS)
    xh = xc * rstd
    vn = xh * lg + lb
    row = lax.broadcasted_iota(jnp.int32, (SGU_CHUNK, SGU_CHUNK), 0)
    col = lax.broadcasted_iota(jnp.int32, (SGU_CHUNK, SGU_CHUNK), 1)
    tril = row >= col
    ws = [jnp.where(tril, ws_ref[g], 0.0).astype(BF16) for g in range(SGU_GROUPS)]
    s = jnp.concatenate(
        [_dot(ws[g], vn[:, g * 64:(g + 1) * 64].astype(BF16)) + bst[:, g:g + 1] for g in range(SGU_GROUPS)], axis=1)
    return a, bb, z, u, vp, rstd, xh, vn, ws, tril, s


def _sgu_fwd(proj, lg, lb, w_s, bst):
    t = proj.shape[0]
    tr = SGU_CHUNK

    def body(p_ref, lg_ref, lb_ref, ws_ref, bst_ref, o_ref):
        _, _, z, u, _, _, _, _, _, _, s = _sgu_parts(p_ref[...], lg_ref[...], lb_ref[...], ws_ref, bst_ref[...])
        o_ref[...] = (u * s * _silu(z)).astype(BF16)

    return pl.pallas_call(
        body, grid=(t // tr,),
        in_specs=[_row_spec(tr, 3072), _full_spec((1, 1024)), _full_spec((1, 1024)), _full_spec((16, 128, 128)),
                  _full_spec((128, 16))],
        out_specs=_row_spec(tr, 1024), out_shape=S((t, 1024), BF16), name="sgu_fwd",
        compiler_params=_cp(("parallel",)))(proj, lg, lb, w_s, bst)


def _sgu_bwd(proj, dp, lg, lb, w_s, bst):
    t = proj.shape[0]
    tr = SGU_CHUNK

    def body(p_ref, dp_ref, lg_ref, lb_ref, ws_ref, bst_ref, dproj_ref, dws_ref, dbst_ref, dlg_ref, dlb_ref):
        @pl.when(pl.program_id(0) == 0)
        def _():
            dws_ref[...] = jnp.zeros_like(dws_ref)
            dbst_ref[...] = jnp.zeros_like(dbst_ref)
            dlg_ref[...] = jnp.zeros_like(dlg_ref)
            dlb_ref[...] = jnp.zeros_like(dlb_ref)

        lgv = lg_ref[...]
        a, bb, z, u, vp, rstd, xh, vn, ws, tril, s = _sgu_parts(p_ref[...], lgv, lb_ref[...], ws_ref, bst_ref[...])
        dpv = dp_ref[...].astype(F32)
        sz = _silu(z)
        du = dpv * s * sz
        ds = dpv * u * sz
        dz = dpv * u * s * _silu_grad(z)
        dvn_parts, dbs_cols = [], []
        for g in range(SGU_GROUPS):
            dsg = ds[:, g * 64:(g + 1) * 64]
            dsg_b = dsg.astype(BF16)
            dvn_parts.append(_dot_tn(ws[g], dsg_b))
            dws_ref[g] += jnp.where(tril, _dot_nt(dsg_b, vn[:, g * 64:(g + 1) * 64].astype(BF16)), 0.0)
            dbs_cols.append(jnp.sum(dsg, axis=1, keepdims=True))
        dvn = jnp.concatenate(dvn_parts, axis=1)
        dbst_ref[...] += jnp.concatenate(dbs_cols, axis=1)
        dlg_ref[...] += jnp.sum(dvn * xh, axis=0, keepdims=True)
        dlb_ref[...] += jnp.sum(dvn, axis=0, keepdims=True)
        dxh = dvn * lgv
        dvp = rstd * (dxh - jnp.mean(dxh, axis=-1, keepdims=True) - xh * jnp.mean(dxh * xh, axis=-1, keepdims=True))
        dproj_ref[:, 0:1024] = (du * _gelu_grad(a)).astype(BF16)
        dproj_ref[:, 1024:2048] = (dvp * _gelu_grad(bb)).astype(BF16)
        dproj_ref[:, 2048:3072] = dz.astype(BF16)

    return pl.pallas_call(
        body, grid=(t // tr,),
        in_specs=[_row_spec(tr, 3072), _row_spec(tr, 1024), _full_spec((1, 1024)), _full_spec((1, 1024)),
                  _full_spec((16, 128, 128)), _full_spec((128, 16))],
        out_specs=[_row_spec(tr, 3072), _full_spec((16, 128, 128)), _full_spec((128, 16)), _full_spec((1, 1024)),
                   _full_spec((1, 1024))],
        out_shape=[S((t, 3072), BF16), S((16, 128, 128), F32), S((128, 16), F32), S((1, 1024), F32),
                   S((1, 1024), F32)],
        name="sgu_bwd", compiler_params=_cp(("arbitrary",)))(proj, dp, lg, lb, w_s, bst)


def _t5_bucket_table():
    qi = np.arange(WINDOW)[:, None]
    kj = np.arange(2 * WINDOW)[None, :]
    dist = np.maximum(qi + WINDOW - kj, 0)
    max_exact = REL_BUCKETS // 2
    dist_f = np.maximum(dist, 1).astype(np.float32)
    large = max_exact + (np.log(dist_f / np.float32(max_exact)) / np.float32(math.log(REL_MAX_DIST / max_exact))
                         * np.float32(REL_BUCKETS - max_exact)).astype(np.int32)
    large = np.minimum(large, REL_BUCKETS - 1)
    return np.where(dist < max_exact, dist, large).astype(np.int32)


def _swa_bias(rel_bias, bucket):
    def body(rb_ref, bk_ref, o_ref):
        bk = bk_ref[...]
        for h in range(SWA_HEADS):
            acc = jnp.zeros((WINDOW, 2 * WINDOW), F32)
            for b in range(REL_BUCKETS):
                acc = jnp.where(bk == b, rb_ref[b, h], acc)
            o_ref[h] = acc

    return pl.pallas_call(
        body, in_specs=[pl.BlockSpec(memory_space=pltpu.SMEM), pl.BlockSpec(memory_space=pltpu.VMEM)],
        out_specs=pl.BlockSpec(memory_space=pltpu.VMEM), out_shape=S((SWA_HEADS, WINDOW, 2 * WINDOW), F32),
        name="swa_bias")(rel_bias, bucket)


def _swa_dbias(ds_sum, dsink_rows, bucket):
    def body(ds_ref, dsk_ref, bk_ref, drb_ref, dsink_ref):
        bk = bk_ref[...]
        r = lax.broadcasted_iota(jnp.int32, (REL_BUCKETS, SWA_HEADS), 0)
        c = lax.broadcasted_iota(jnp.int32, (REL_BUCKETS, SWA_HEADS), 1)
        out = jnp.zeros((REL_BUCKETS, SWA_HEADS), F32)
        for b in range(REL_BUCKETS):
            m = bk == b
            for h in range(SWA_HEADS):
                val = jnp.sum(jnp.where(m, ds_ref[h], 0.0))
                out = jnp.where((r == b) & (c == h), val, out)
        drb_ref[...] = out
        dsink_ref[...] = jnp.sum(dsk_ref[...], axis=0, keepdims=True)

    return pl.pallas_call(
        body, out_shape=[S((REL_BUCKETS, SWA_HEADS), F32), S((1, SWA_HEADS), F32)], name="swa_dbias",
    )(ds_sum, dsink_rows, bucket)


def _swa_band(cur_ref, prev_ref):
    kband = jnp.concatenate([prev_ref[:, 0:128], cur_ref[:, 1024:1152]], axis=0).astype(BF16)
    vband = jnp.concatenate([prev_ref[:, 128:256], cur_ref[:, 1152:1280]], axis=0).astype(BF16)
    return kband, vband


def _swa_valid(n):
    qi = lax.broadcasted_iota(jnp.int32, (WINDOW, 2 * WINDOW), 0)
    kj = lax.broadcasted_iota(jnp.int32, (WINDOW, 2 * WINDOW), 1)
    dist = qi + WINDOW - kj
    return (dist >= 0) & (dist < WINDOW) & (n * WINDOW + kj - WINDOW >= 0)


def _swa_probs(qh, kh, bias_h, sink, valid):
    s = _dot_nt(qh, kh) * (HEAD_DIM ** -0.5) + bias_h
    s = jnp.where(valid, s, NEG_INF)
    m = jnp.maximum(jnp.max(s, axis=-1, keepdims=True), sink)
    e = jnp.exp(s - m)
    es = jnp.exp(sink - m)
    inv = 1.0 / (jnp.sum(e, axis=-1, keepdims=True) + es)
    return e * inv, es * inv


def _swa_in_specs():
    return [_row_spec(WINDOW, 2304),
            pl.BlockSpec((WINDOW, 256), lambda n: (jnp.maximum(n - 1, 0), 4)),
            _full_spec((SWA_HEADS, WINDOW, 2 * WINDOW)),
            pl.BlockSpec(memory_space=pltpu.SMEM)]


def _swa_fwd(proj, bias, sinks):
    t = proj.shape[0]

    def body(cur_ref, prev_ref, bias_ref, sink_ref, o_ref):
        n = pl.program_id(0)
        kband, vband = _swa_band(cur_ref, prev_ref)
        valid = _swa_valid(n)
        outs = []
        for h in range(SWA_HEADS):
            j = h // SWA_GROUP
            qh = cur_ref[:, h * 64:(h + 1) * 64].astype(BF16)
            p, _ = _swa_probs(qh, kband[:, j * 64:(j + 1) * 64], bias_ref[h], sink_ref[0, h], valid)
            outs.append(_dot(p.astype(BF16), vband[:, j * 64:(j + 1) * 64]))
        o = jnp.concatenate(outs, axis=1)
        o_ref[...] = (o * _silu(cur_ref[:, 1280:2304])).astype(BF16)

    return pl.pallas_call(body, grid=(t // WINDOW,), in_specs=_swa_in_specs(), out_specs=_row_spec(WINDOW, 1024),
                          out_shape=S((t, 1024), BF16), name="swa_fwd",
                          compiler_params=_cp(("parallel",)))(proj, proj, bias, sinks)


def _swa_bwd(proj, dp, bias, sinks):
    t = proj.shape[0]
    nb = t // WINDOW

    def body(cur_ref, prev_ref, bias_ref, sink_ref, dp_ref, dqz_ref, dkv_ref, dss_ref, dsk_ref):
        n = pl.program_id(0)

        @pl.when(n == 0)
        def _():
            dkv_ref[...] = jnp.zeros_like(dkv_ref)
            dss_ref[...] = jnp.zeros_like(dss_ref)
            dsk_ref[...] = jnp.zeros_like(dsk_ref)

        kband, vband = _swa_band(cur_ref, prev_ref)
        valid = _swa_valid(n)
        z = cur_ref[:, 1280:2304]
        dpv = dp_ref[...].astype(F32)
        do = dpv * _silu(z)
        outs, dqs, dsinks = [], [], []
        dk = [jnp.zeros((2 * WINDOW, 64), F32) for _ in range(2)]
        dv = [jnp.zeros((2 * WINDOW, 64), F32) for _ in range(2)]
        for h in range(SWA_HEADS):
            j = h // SWA_GROUP
            qh = cur_ref[:, h * 64:(h + 1) * 64].astype(BF16)
            kh = kband[:, j * 64:(j + 1) * 64]
            vh = vband[:, j * 64:(j + 1) * 64]
            p, ps = _swa_probs(qh, kh, bias_ref[h], sink_ref[0, h], valid)
            pb = p.astype(BF16)
            outs.append(_dot(pb, vh))
            doh = do[:, h * 64:(h + 1) * 64].astype(BF16)
            dpr = _dot_nt(doh, vh)
            dlt = jnp.sum(p * dpr, axis=-1, keepdims=True)
            ds = p * (dpr - dlt)
            dsinks.append(-ps * dlt)
            dss_ref[h] += ds
            dsb = (ds * (HEAD_DIM ** -0.5)).astype(BF16)
            dqs.append(_dot(dsb, kh))
            dk[j] = dk[j] + _dot_tn(dsb, qh)
            dv[j] = dv[j] + _dot_tn(pb, doh)
        o = jnp.concatenate(outs, axis=1)
        dqz_ref[:, 0:1024] = jnp.concatenate(dqs, axis=1).astype(BF16)
        dqz_ref[:, 1024:2048] = (dpv * o * _silu_grad(z)).astype(BF16)
        dsk_ref[...] += jnp.concatenate(dsinks, axis=1)
        dband = jnp.concatenate([dk[0], dk[1], dv[0], dv[1]], axis=1)
        prow = pl.multiple_of(jnp.maximum(n - 1, 0) * WINDOW, WINDOW)
        dkv_ref[pl.ds(prow, WINDOW), :] += dband[0:WINDOW]
        crow = pl.multiple_of(n * WINDOW, WINDOW)
        dkv_ref[pl.ds(crow, WINDOW), :] += dband[WINDOW:]

    return pl.pallas_call(
        body, grid=(nb,), in_specs=_swa_in_specs() + [_row_spec(WINDOW, 1024)],
        out_specs=[_row_spec(WINDOW, 2048), _full_spec((t, 256)), _full_spec((SWA_HEADS, WINDOW, 2 * WINDOW)),
                   _full_spec((WINDOW, SWA_HEADS))],
        out_shape=[S((t, 2048), BF16), S((t, 256), F32), S((SWA_HEADS, WINDOW, 2 * WINDOW), F32),
                   S((WINDOW, SWA_HEADS), F32)],
        name="swa_bwd", compiler_params=_cp(("arbitrary",)))(proj, proj, bias, sinks, dp)


MLA_Z0 = MLA_Q_RANK + MLA_KV_RANK + 128
MLA_Z1 = MLA_Z0 + 1024
MLA_PROJ = MLA_Z1 + 128


def _rope_tables(t):
    inv = ROPE_BASE ** (-jnp.arange(0, MLA_ROPE, 2, dtype=F32) / MLA_ROPE)
    ang = jnp.arange(t, dtype=F32)[:, None] * inv[None, :]
    return jnp.tile(jnp.cos(ang), (1, MLA_HEADS)), jnp.tile(jnp.sin(ang), (1, MLA_HEADS))


def _mla_norms(proj, gq, gkv):
    t = proj.shape[0]
    tr = 256

    def body(p_ref, gq_ref, gkv_ref, q_ref, kv_ref):
        for lo, hi, g_ref, o_ref in ((0, 768, gq_ref, q_ref), (768, 1024, gkv_ref, kv_ref)):
            xv = p_ref[:, lo:hi]
            r = lax.rsqrt(jnp.mean(xv * xv, axis=-1, keepdims=True) + EPS)
            o_ref[...] = (xv * r * g_ref[...]).astype(BF16)

    return pl.pallas_call(
        body, grid=(t // tr,), in_specs=[_row_spec(tr, 1024), _full_spec((1, 768)), _full_spec((1, 256))],
        out_specs=[_row_spec(tr, 768), _row_spec(tr, 256)], out_shape=[S((t, 768), BF16), S((t, 256), BF16)],
        name="mla_norms", compiler_params=_cp(("parallel",)))(proj, gq, gkv)


def _mla_norms_bwd(proj, dcqn, dckvn, dkr, dz, gq, gkv):
    t = proj.shape[0]
    tr = 256

    def body(p_ref, dq_ref, dkv_ref, dkr_ref, dz_ref, gq_ref, gkv_ref, dproj_ref, dgq_ref, dgkv_ref):
        @pl.when(pl.program_id(0) == 0)
        def _():
            dgq_ref[...] = jnp.zeros_like(dgq_ref)
            dgkv_ref[...] = jnp.zeros_like(dgkv_ref)

        dx, dg = _rms_bwd_rows(dq_ref[...], p_ref[:, 0:768], gq_ref[...])
        dproj_ref[:, 0:768] = dx.astype(BF16)
        dgq_ref[...] += dg
        dx, dg = _rms_bwd_rows(dkv_ref[...], p_ref[:, 768:1024], gkv_ref[...])
        dproj_ref[:, 768:1024] = dx.astype(BF16)
        dgkv_ref[...] += dg
        dproj_ref[:, 1024:MLA_Z0] = dkr_ref[...].astype(BF16)
        dproj_ref[:, MLA_Z0:MLA_Z1] = dz_ref[...]
        dproj_ref[:, MLA_Z1:MLA_PROJ] = jnp.zeros((tr, MLA_PROJ - MLA_Z1), BF16)

    return pl.pallas_call(
        body, grid=(t // tr,),
        in_specs=[_row_spec(tr, 1024), _row_spec(tr, 768), _row_spec(tr, 256), _row_spec(tr, 128), _row_spec(tr, 1024),
                  _full_spec((1, 768)), _full_spec((1, 256))],
        out_specs=[_row_spec(tr, MLA_PROJ), _full_spec((1, 768)), _full_spec((1, 256))],
        out_shape=[S((t, MLA_PROJ), BF16), S((1, 768), F32), S((1, 256), F32)],
        name="mla_norms_bwd", compiler_params=_cp(("arbitrary",)))(proj, dcqn, dckvn, dkr, dz, gq, gkv)


def _mla_pack(qf, kvf, proj, cos, sin):
    t = qf.shape[0]
    tr = 256

    def body(q_ref, kn_ref, kr_ref, cos_ref, sin_ref, qh_ref, kh_ref):
        cs, sn = cos_ref[...], sin_ref[...]
        x1, x2 = q_ref[:, 1024:1280], q_ref[:, 1280:1536]
        r1 = x1 * cs - x2 * sn
        r2 = x2 * cs + x1 * sn
        c16, s16 = cs[:, 0:16], sn[:, 0:16]
        k1, k2 = kr_ref[:, 0:16], kr_ref[:, 16:32]
        kr1 = k1 * c16 - k2 * s16
        kr2 = k2 * c16 + k1 * s16
        zpad = jnp.zeros((tr, 32), F32)
        for h in range(MLA_HEADS):
            qh_ref[h] = jnp.concatenate([q_ref[:, h * 64:(h + 1) * 64], r1[:, h * 16:(h + 1) * 16],
                                         r2[:, h * 16:(h + 1) * 16], zpad], axis=1).astype(BF16)
            kh_ref[h] = jnp.concatenate([kn_ref[:, h * 64:(h + 1) * 64], kr1, kr2, zpad], axis=1).astype(BF16)

    hspec = pl.BlockSpec((MLA_HEADS, tr, 128), lambda i: (0, i, 0))
    return pl.pallas_call(
        body, grid=(t // tr,),
        in_specs=[_row_spec(tr, 1536), _row_spec(tr, 1024), pl.BlockSpec((tr, 128), lambda i: (i, 8)),
                  _row_spec(tr, 256), _row_spec(tr, 256)],
        out_specs=[hspec, hspec], out_shape=[S((MLA_HEADS, t, 128), BF16)] * 2,
        name="mla_pack", compiler_params=_cp(("parallel",)))(qf, kvf, proj, cos, sin)


def _mla_unpack(dqh, dkh, dv, cos, sin):
    t = dqh.shape[1]
    tr = 256

    def body(dqh_ref, dkh_ref, dv_ref, cos_ref, sin_ref, dq_ref, dkv_ref, dkr_ref):
        cs, sn = cos_ref[...], sin_ref[...]
        dqn = jnp.concatenate([dqh_ref[h, :, 0:64] for h in range(MLA_HEADS)], axis=1)
        dr1 = jnp.concatenate([dqh_ref[h, :, 64:80] for h in range(MLA_HEADS)], axis=1)
        dr2 = jnp.concatenate([dqh_ref[h, :, 80:96] for h in range(MLA_HEADS)], axis=1)
        dq_ref[:, 0:1024] = dqn.astype(BF16)
        dq_ref[:, 1024:1280] = (dr1 * cs + dr2 * sn).astype(BF16)
        dq_ref[:, 1280:1536] = (dr2 * cs - dr1 * sn).astype(BF16)
        dkv_ref[:, 0:1024] = jnp.concatenate([dkh_ref[h, :, 0:64] for h in range(MLA_HEADS)], axis=1).astype(BF16)
        dkv_ref[:, 1024:2048] = dv_ref[...].astype(BF16)
        d1 = dkh_ref[0, :, 64:80]
        d2 = dkh_ref[0, :, 80:96]
        for h in range(1, MLA_HEADS):
            d1 = d1 + dkh_ref[h, :, 64:80]
            d2 = d2 + dkh_ref[h, :, 80:96]
        c16, s16 = cs[:, 0:16], sn[:, 0:16]
        dkr_ref[...] = jnp.concatenate([d1 * c16 + d2 * s16, d2 * c16 - d1 * s16, jnp.zeros((tr, 96), F32)], axis=1)

    hspec = pl.BlockSpec((MLA_HEADS, tr, 128), lambda i: (0, i, 0))
    return pl.pallas_call(
        body, grid=(t // tr,), in_specs=[hspec, hspec, _row_spec(tr, 1024), _row_spec(tr, 256), _row_spec(tr, 256)],
        out_specs=[_row_spec(tr, 1536), _row_spec(tr, 2048), _row_spec(tr, 128)],
        out_shape=[S((t, 1536), BF16), S((t, 2048), BF16), S((t, 128), F32)],
        name="mla_unpack", compiler_params=_cp(("parallel",)))(dqh, dkh, dv, cos, sin)


def _mla_attn_fwd(qh, kh, kvf, proj):
    t = qh.shape[1]
    b, bk = MLA_BQ, MLA_BK

    def body(q_ref, k_ref, v_ref, z_ref, o_ref, p_ref, lse_ref):
        r0 = pl.program_id(1) * b
        nfull = r0 // bk
        diff = (lax.broadcasted_iota(jnp.int32, (b, bk), 1) - lax.broadcasted_iota(jnp.int32, (b, bk), 0))
        qs = [q_ref[0], q_ref[1]]

        def block(c0, carry, masked):
            new = []
            for a in range(2):
                m, l, acc = carry[a]
                k = k_ref[a, pl.ds(c0, bk), :]
                v = v_ref[pl.ds(c0, bk), a * 64:(a + 1) * 64].astype(BF16)
                s = _dot_nt(qs[a], k) * MLA_SCALE
                if masked:
                    s = jnp.where(diff <= r0 - c0, s, NEG_INF)
                mn = jnp.maximum(m, jnp.max(s, axis=-1, keepdims=True))
                alpha = jnp.exp(m - mn)
                p = jnp.exp(s - mn)
                new.append((mn, alpha * l + jnp.sum(p, axis=-1, keepdims=True), alpha * acc + _dot(p.astype(BF16), v)))
            return tuple(new)

        init = (jnp.full((b, 1), NEG_INF, F32), jnp.zeros((b, 1), F32), jnp.zeros((b, 64), F32))
        carry = lax.fori_loop(0, nfull, lambda c, cr: block(pl.multiple_of(c * bk, bk), cr, False), (init, init))
        for i in range(max(1, b // bk)):
            carry = block(pl.multiple_of((nfull + i) * bk, bk), carry, True)
        outs = [acc / l for _, l, acc in carry]
        lses = [jnp.broadcast_to(m + jnp.log(l), (b, 64)) for m, l, _ in carry]
        o = jnp.concatenate(outs, axis=1)
        o_ref[...] = o
        p_ref[...] = (o * _silu(z_ref[...])).astype(BF16)
        lse_ref[0] = jnp.concatenate(lses, axis=1)

    return pl.pallas_call(
        body, grid=(MLA_HEADS // 2, t // b),
        in_specs=[pl.BlockSpec((2, b, 128), lambda i, r: (i, r, 0)), pl.BlockSpec((2, t, 128), lambda i, r: (i, 0, 0)),
                  pl.BlockSpec((t, 128), lambda i, r: (0, 8 + i)), pl.BlockSpec((b, 128), lambda i, r: (r, 9 + i))],
        out_specs=[pl.BlockSpec((b, 128), lambda i, r: (r, i)), pl.BlockSpec((b, 128), lambda i, r: (r, i)),
                   pl.BlockSpec((1, b, 128), lambda i, r: (i, r, 0))],
        out_shape=[S((t, 1024), F32), S((t, 1024), BF16), S((MLA_HEADS // 2, t, 128), F32)],
        name="mla_attn_fwd", compiler_params=_cp(("parallel", "parallel")))(qh, kh, kvf, proj)


def _mla_attn_bwd(qh, kh, kvf, proj, o, dpout, lse):
    t = qh.shape[1]
    b, bk = MLA_BQ, MLA_BK
    nb = t // b

    def body(q_ref, k_ref, v_ref, z_ref, o_ref, dp_ref, lse_ref, dq_ref, dk_ref, dv_ref, dz_ref):
        dk_ref[...] = jnp.zeros_like(dk_ref)
        dv_ref[...] = jnp.zeros_like(dv_ref)
        diff = (lax.broadcasted_iota(jnp.int32, (b, bk), 1) - lax.broadcasted_iota(jnp.int32, (b, bk), 0))
        lanes = [slice(0, 64), slice(64, 128)]

        def rbody(r, carry):
            r0 = pl.multiple_of(r * b, b)
            rows = pl.ds(r0, b)
            qs, dobs, deltas, lses = [], [], [], []
            for a in range(2):
                zz = z_ref[rows, lanes[a]]
                oo = o_ref[rows, lanes[a]]
                dpo = dp_ref[rows, lanes[a]].astype(F32)
                do = dpo * _silu(zz)
                dz_ref[rows, lanes[a]] = (dpo * oo * _silu_grad(zz)).astype(BF16)
                qs.append(q_ref[a, rows, :])
                dobs.append(do.astype(BF16))
                deltas.append(jnp.sum(do * oo, axis=-1, keepdims=True))
                lses.append(lse_ref[0, rows, a * 64:a * 64 + 1])

            def block(c0, dqs, masked):
                cols = pl.ds(c0, bk)
                out = []
                for a in range(2):
                    k = k_ref[a, cols, :]
                    v = v_ref[cols, lanes[a]].astype(BF16)
                    s = _dot_nt(qs[a], k) * MLA_SCALE
                    if masked:
                        s = jnp.where(diff <= r0 - c0, s, NEG_INF)
                    p = jnp.exp(s - lses[a])
                    ds = p * (_dot_nt(dobs[a], v) - deltas[a]) * MLA_SCALE
                    dsb = ds.astype(BF16)
                    dk_ref[a, cols, :] += _dot_tn(dsb, qs[a])
                    dv_ref[cols, lanes[a]] += _dot_tn(p.astype(BF16), dobs[a])
                    out.append(dqs[a] + _dot(dsb, k))
                return tuple(out)

            zero = jnp.zeros((b, 128), F32)
            nfull = r0 // bk
            dqs = lax.fori_loop(0, nfull, lambda c, d: block(pl.multiple_of(c * bk, bk), d, False), (zero, zero))
            for i in range(max(1, b // bk)):
                dqs = block(pl.multiple_of((nfull + i) * bk, bk), dqs, True)
            dq_ref[0, rows, :] = dqs[0]
            dq_ref[1, rows, :] = dqs[1]
            return carry

        lax.fori_loop(0, nb, rbody, 0)

    pair3 = pl.BlockSpec((2, t, 128), lambda i: (i, 0, 0))
    return pl.pallas_call(
        body, grid=(MLA_HEADS // 2,),
        in_specs=[pair3, pair3, pl.BlockSpec((t, 128), lambda i: (0, 8 + i)), pl.BlockSpec((t, 128), lambda i: (0, 9 + i)),
                  pl.BlockSpec((t, 128), lambda i: (0, i)), pl.BlockSpec((t, 128), lambda i: (0, i)),
                  pl.BlockSpec((1, t, 128), lambda i: (i, 0, 0))],
        out_specs=[pair3, pair3, pl.BlockSpec((t, 128), lambda i: (0, i)), pl.BlockSpec((t, 128), lambda i: (0, i))],
        out_shape=[S((MLA_HEADS, t, 128), F32), S((MLA_HEADS, t, 128), F32), S((t, 1024), F32), S((t, 1024), BF16)],
        name="mla_attn_bwd", compiler_params=_cp(("parallel",), VMEM_BIG))(qh, kh, kvf, proj, o, dpout, lse)


def _dot_nt_hi(a, b):
    return lax.dot_general(a, b, (((1,), (1,)), ((), ())), preferred_element_type=F32,
                           precision=lax.Precision.HIGHEST)


def _ssm_gen(lr, li, ldt, btr, bti, cr, ci):
    n = SSM_BLOCK
    dt = jnp.exp(ldt)
    x = lr * dt
    w = li * dt
    mag = jnp.exp(x)
    ab_re = mag * jnp.cos(w)
    ab_im = mag * jnp.sin(w)
    den = lr * lr + li * li
    nr = ab_re - 1.0
    f_re = (nr * lr + ab_im * li) / den
    f_im = (ab_im * lr - nr * li) / den
    bb_re = f_re * btr - f_im * bti
    bb_im = f_re * bti + f_im * btr
    inv_mag = jnp.exp(-x)
    inv_re, inv_im = inv_mag * jnp.cos(w), -inv_mag * jnp.sin(w)
    one, zero = jnp.ones_like(x), jnp.zeros_like(x)
    pos, neg = [(one, zero)], [(one, zero)]
    for _ in range(SSM_CHUNK):
        pr, pi = pos[-1]
        pos.append((pr * ab_re - pi * ab_im, pr * ab_im + pi * ab_re))
        pr, pi = neg[-1]
        neg.append((pr * inv_re - pi * inv_im, pr * inv_im + pi * inv_re))

    def stack(table, idx):
        return (jnp.concatenate([jnp.broadcast_to(table[idx(t)][0], (SSM_GROUP, SSM_STATE)) for t in range(SSM_CHUNK)], axis=0),
                jnp.concatenate([jnp.broadcast_to(table[idx(t)][1], (SSM_GROUP, SSM_STATE)) for t in range(SSM_CHUNK)], axis=0))

    def rep(v):
        return jnp.concatenate([v] * SSM_CHUNK, axis=0)

    ct_re, ct_im, bt_re, bt_im = rep(cr), rep(ci), rep(bb_re), rep(bb_im)
    p_re, p_im = stack(pos, lambda t: t)
    l_re = ct_re * p_re - ct_im * p_im
    l_im = ct_re * p_im + ct_im * p_re
    n_re, n_im = stack(neg, lambda t: t)
    r_re = bt_re * n_re - bt_im * n_im
    r_im = bt_re * n_im + bt_im * n_re
    rk = lax.broadcasted_iota(jnp.int32, (n, n), 0) // SSM_GROUP
    ct = lax.broadcasted_iota(jnp.int32, (n, n), 1) // SSM_GROUP
    mt = jnp.where(ct >= rk, _dot_nt_hi(r_re, l_re) - _dot_nt_hi(r_im, l_im), 0.0)
    e_re, e_im = stack(pos, lambda t: SSM_CHUNK - 1 - t)
    pin_re = bt_re * e_re - bt_im * e_im
    pin_im = bt_re * e_im + bt_im * e_re
    q_re, q_im = stack(pos, lambda t: t + 1)
    qt_re = ct_re * q_re - ct_im * q_im
    qt_im = -(ct_re * q_im + ct_im * q_re)
    ad_re, ad_im = pos[SSM_CHUNK]
    return mt, pin_re, pin_im, qt_re, qt_im, ad_re, ad_im


_SSM_GEN_OUT = [(SSM_BLOCK, SSM_BLOCK)] + [(SSM_BLOCK, SSM_STATE)] * 4 + [(1, SSM_STATE)] * 2
_SSM_GEN_IN = [(1, SSM_STATE)] * 3 + [(SSM_GROUP, SSM_STATE)] * 4


def _gspec(shape2):
    return pl.BlockSpec((1,) + shape2, lambda g: (g, 0, 0))


def _ssm_operators(params):
    def body(*refs):
        ins, outs = refs[:7], refs[7:]
        for o_ref, val in zip(outs, _ssm_gen(*[r[0] for r in ins])):
            o_ref[0] = val

    return pl.pallas_call(
        body, grid=(SSM_GROUPS,), in_specs=[_gspec(s) for s in _SSM_GEN_IN], out_specs=[_gspec(s) for s in _SSM_GEN_OUT],
        out_shape=[S((SSM_GROUPS,) + s, F32) for s in _SSM_GEN_OUT], name="ssm_operators",
        compiler_params=_cp(("parallel",)))(*params)


def _ssm_operators_bwd(params, cots, after=()):
    def body(*refs):
        ins, cts, outs = refs[:7], refs[7:14], refs[14 + len(after):]
        _, vjp = jax.vjp(_ssm_gen, *[r[0] for r in ins])
        grads = list(vjp(tuple(r[0] for r in cts)))
        grads[2] = jnp.broadcast_to(jnp.sum(grads[2], axis=-1, keepdims=True), (1, SSM_STATE))
        for o_ref, val in zip(outs, grads):
            o_ref[0] = val

    return pl.pallas_call(
        body, grid=(SSM_GROUPS,),
        in_specs=[_gspec(s) for s in _SSM_GEN_IN] + [_gspec(s) for s in _SSM_GEN_OUT]
        + [pl.BlockSpec(memory_space=pl.ANY)] * len(after),
        out_specs=[_gspec(s) for s in _SSM_GEN_IN], out_shape=[S((SSM_GROUPS,) + s, F32) for s in _SSM_GEN_IN],
        name="ssm_operators_bwd", compiler_params=_cp(("parallel",)))(*params, *cots, *after)


def _shift_rows(x, sh, row, up):
    n = x.shape[0]
    if up:
        return jnp.where(row < n - sh, pltpu.roll(x, n - sh, axis=0), 0.0)
    return jnp.where(row >= sh, pltpu.roll(x, sh, axis=0), 0.0)


def _carry_scan(sr, si, ar, ai, row, up):
    n = sr.shape[0]
    sh = 1
    while sh < n:
        tr, ti = _shift_rows(sr, sh, row, up), _shift_rows(si, sh, row, up)
        sr, si = sr + ar * tr - ai * ti, si + ar * ti + ai * tr
        ar, ai = ar * ar - ai * ai, 2.0 * ar * ai
        sh *= 2
    return sr, si


def _ssm_states(u, p_re, p_im, ar, ai, row):
    lre = _dot(u, p_re.astype(BF16))
    lim = _dot(u, p_im.astype(BF16))
    sr, si = _carry_scan(lre, lim, ar, ai, row, False)
    return _shift_rows(sr, 1, row, False), _shift_rows(si, 1, row, False)


SSM_GPB = 128 // SSM_GROUP


def _step_rows(ref, nc):
    return [ref[pl.ds(k, nc, stride=SSM_CHUNK), :].astype(BF16) for k in range(SSM_CHUNK)]


def _group_of(rows, j):
    return jnp.concatenate([r[:, j * SSM_GROUP:(j + 1) * SSM_GROUP] for r in rows], axis=1)


def _store_groups(o_ref, per_group, nc):
    for k in range(SSM_CHUNK):
        o_ref[pl.ds(k, nc, stride=SSM_CHUNK), :] = jnp.concatenate(
            [y[:, k * SSM_GROUP:(k + 1) * SSM_GROUP] for y in per_group], axis=1)


def _bspec(shape2):
    return pl.BlockSpec((SSM_GPB,) + shape2, lambda j: (j, 0, 0))


def _ssm_core_fwd(proj, ops):
    t = proj.shape[0]
    nc = t // SSM_CHUNK

    def body(u_ref, mt_ref, pr_ref, pi_ref, qr_ref, qi_ref, ar_ref, ai_ref, y_ref):
        rows = _step_rows(u_ref, nc)
        row = lax.broadcasted_iota(jnp.int32, (nc, SSM_STATE), 0)
        ys = []
        for j in range(SSM_GPB):
            u = _group_of(rows, j)
            s_re, s_im = _ssm_states(u, pr_ref[j], pi_ref[j], ar_ref[j], ai_ref[j], row)
            ys.append(_dot(u, mt_ref[j].astype(BF16)) + _dot_nt(s_re.astype(BF16), qr_ref[j].astype(BF16))
                      + _dot_nt(s_im.astype(BF16), qi_ref[j].astype(BF16)))
        _store_groups(y_ref, ys, nc)

    blk = pl.BlockSpec((t, 128), lambda j: (0, j))
    return pl.pallas_call(
        body, grid=(SSM_GROUPS // SSM_GPB,), in_specs=[blk] + [_bspec(s) for s in _SSM_GEN_OUT], out_specs=blk,
        out_shape=S((t, 1024), F32), name="ssm_core_fwd", compiler_params=_cp(("parallel",)))(proj, *ops)


def _ssm_core_bwd(proj, dy, ops):
    t = proj.shape[0]
    nc = t // SSM_CHUNK

    def body(u_ref, dy_ref, mt_ref, pr_ref, pi_ref, qr_ref, qi_ref, ar_ref, ai_ref,
             du_ref, dmt_ref, dpr_ref, dpi_ref, dqr_ref, dqi_ref, dar_ref, dai_ref):
        u_rows = _step_rows(u_ref, nc)
        dy_rows = _step_rows(dy_ref, nc)
        row = lax.broadcasted_iota(jnp.int32, (nc, SSM_STATE), 0)
        dus = []
        for j in range(SSM_GPB):
            u = _group_of(u_rows, j)
            dyj = _group_of(dy_rows, j)
            ar, ai = ar_ref[j], ai_ref[j]
            s_re, s_im = _ssm_states(u, pr_ref[j], pi_ref[j], ar, ai, row)
            g_re = _shift_rows(_dot(dyj, qr_ref[j].astype(BF16)), 1, row, True)
            g_im = _shift_rows(_dot(dyj, qi_ref[j].astype(BF16)), 1, row, True)
            l_re, l_im = _carry_scan(g_re, g_im, ar, -ai, row, True)
            lrb, lib = l_re.astype(BF16), l_im.astype(BF16)
            dus.append(_dot_nt(dyj, mt_ref[j].astype(BF16)) + _dot_nt(lrb, pr_ref[j].astype(BF16))
                       + _dot_nt(lib, pi_ref[j].astype(BF16)))
            dmt_ref[j] = _dot_tn(u, dyj)
            dpr_ref[j] = _dot_tn(u, lrb)
            dpi_ref[j] = _dot_tn(u, lib)
            dqr_ref[j] = _dot_tn(dyj, s_re.astype(BF16))
            dqi_ref[j] = _dot_tn(dyj, s_im.astype(BF16))
            dar_ref[j] = jnp.sum(l_re * s_re + l_im * s_im, axis=0, keepdims=True)
            dai_ref[j] = jnp.sum(l_im * s_re - l_re * s_im, axis=0, keepdims=True)
        _store_groups(du_ref, dus, nc)

    blk = pl.BlockSpec((t, 128), lambda j: (0, j))
    return pl.pallas_call(
        body, grid=(SSM_GROUPS // SSM_GPB,), in_specs=[blk, blk] + [_bspec(s) for s in _SSM_GEN_OUT],
        out_specs=[blk] + [_bspec(s) for s in _SSM_GEN_OUT],
        out_shape=[S((t, 1024), F32)] + [S((SSM_GROUPS,) + s, F32) for s in _SSM_GEN_OUT],
        name="ssm_core_bwd", compiler_params=_cp(("parallel",)))(proj, dy, *ops)


def _s5_act(y_core, proj, d_skip):
    t = y_core.shape[0]
    tr = 256

    def body(y_ref, u_ref, d_ref, o_ref):
        o_ref[...] = _gelu(y_ref[...] + d_ref[...] * u_ref[...])

    return pl.pallas_call(body, grid=(t // tr,), in_specs=[_row_spec(tr, 1024), _row_spec(tr, 1024), _full_spec((1, 1024))],
                          out_specs=_row_spec(tr, 1024), out_shape=S((t, 1024), F32), name="s5_act",
                          compiler_params=_cp(("parallel",)))(y_core, proj, d_skip)


def _s5_gate(yg, tg, b_glu, proj):
    t = yg.shape[0]
    tr = 256

    def body(y_ref, t_ref, b_ref, z_ref, o_ref):
        o_ref[...] = (y_ref[...] * jax.nn.sigmoid(t_ref[...] + b_ref[...]) * _silu(z_ref[...])).astype(BF16)

    return pl.pallas_call(
        body, grid=(t // tr,),
        in_specs=[_row_spec(tr, 1024), _row_spec(tr, 1024), _full_spec((1, 1024)), pl.BlockSpec((tr, 1024), lambda i: (i, 1))],
        out_specs=_row_spec(tr, 1024), out_shape=S((t, 1024), BF16), name="s5_gate",
        compiler_params=_cp(("parallel",)))(yg, tg, b_glu, proj)


def _s5_gate_bwd(dp, yg, tg, b_glu, proj):
    t = yg.shape[0]
    tr = 256

    def body(dp_ref, y_ref, t_ref, b_ref, z_ref, dt_ref, dy_ref, dz_ref, db_ref):
        @pl.when(pl.program_id(0) == 0)
        def _():
            db_ref[...] = jnp.zeros_like(db_ref)

        z = z_ref[...]
        yv = y_ref[...]
        sg = jax.nn.sigmoid(t_ref[...] + b_ref[...])
        dpv = dp_ref[...].astype(F32)
        do = dpv * _silu(z)
        dz_ref[...] = (dpv * yv * sg * _silu_grad(z)).astype(BF16)
        dy_ref[...] = do * sg
        dtv = do * yv * sg * (1.0 - sg)
        dt_ref[...] = dtv.astype(BF16)
        db_ref[...] += jnp.sum(dtv, axis=0, keepdims=True)

    return pl.pallas_call(
        body, grid=(t // tr,),
        in_specs=[_row_spec(tr, 1024), _row_spec(tr, 1024), _row_spec(tr, 1024), _full_spec((1, 1024)),
                  pl.BlockSpec((tr, 1024), lambda i: (i, 1))],
        out_specs=[_row_spec(tr, 1024), _row_spec(tr, 1024), _row_spec(tr, 1024), _full_spec((1, 1024))],
        out_shape=[S((t, 1024), BF16), S((t, 1024), F32), S((t, 1024), BF16), S((1, 1024), F32)],
        name="s5_gate_bwd", compiler_params=_cp(("arbitrary",)))(dp, yg, tg, b_glu, proj)


def _s5_act_bwd(dyg1, dyg2, y_core, proj, d_skip):
    t = y_core.shape[0]
    tr = 256

    def body(a_ref, b_ref, y_ref, u_ref, d_ref, dy_ref, du_ref, dd_ref):
        @pl.when(pl.program_id(0) == 0)
        def _():
            dd_ref[...] = jnp.zeros_like(dd_ref)

        u = u_ref[...]
        dv = (a_ref[...] + b_ref[...]) * _gelu_grad(y_ref[...] + d_ref[...] * u)
        dy_ref[...] = dv
        du_ref[...] = dv * d_ref[...]
        dd_ref[...] += jnp.sum(dv * u, axis=0, keepdims=True)

    return pl.pallas_call(
        body, grid=(t // tr,),
        in_specs=[_row_spec(tr, 1024)] * 4 + [_full_spec((1, 1024))],
        out_specs=[_row_spec(tr, 1024), _row_spec(tr, 1024), _full_spec((1, 1024))],
        out_shape=[S((t, 1024), F32), S((t, 1024), F32), S((1, 1024), F32)],
        name="s5_act_bwd", compiler_params=_cp(("arbitrary",)))(dyg1, dyg2, y_core, proj, d_skip)


def _s5_dproj(du_skip, du_core, dz):
    t = du_skip.shape[0]
    tr = 256

    def body(a_ref, b_ref, z_ref, o_ref):
        o_ref[:, 0:1024] = (a_ref[...] + b_ref[...]).astype(BF16)
        o_ref[:, 1024:2048] = z_ref[...]

    return pl.pallas_call(body, grid=(t // tr,), in_specs=[_row_spec(tr, 1024)] * 3, out_specs=_row_spec(tr, 2048),
                          out_shape=S((t, 2048), BF16), name="s5_dproj",
                          compiler_params=_cp(("parallel",)))(du_skip, du_core, dz)


def _s5_params(lam_re, lam_im, log_dt, b_re, b_im, c_re, c_im):
    g = SSM_GROUPS
    return (lam_re.reshape(g, 1, SSM_STATE), lam_im.reshape(g, 1, SSM_STATE),
            jnp.broadcast_to(log_dt.reshape(g, 1, 1), (g, 1, SSM_STATE)),
            b_re.transpose(0, 2, 1), b_im.transpose(0, 2, 1), c_re, c_im)


def _s5_fwd(proj, params, d_skip, w_glu, b_glu):
    ops = _ssm_operators(params)
    y_core = _ssm_core_fwd(proj, ops)
    yg = _s5_act(y_core, proj, d_skip)
    tg = _mm(yg, w_glu, name="s5_glu")
    return _s5_gate(yg, tg, b_glu, proj), (ops, y_core, yg, tg)


def _s5_bwd(proj, res, dp, params, d_skip, w_glu, b_glu):
    ops, y_core, yg, tg = res
    dtg, dyg1, dz, db_glu = _s5_gate_bwd(dp, yg, tg, b_glu, proj)
    dyg2 = _mm(dtg, w_glu, tb=True, name="s5_glu_dx")
    dw_glu = _mm(yg, dtg, ta=True, out_dtype=BF16, name="s5_glu_dw")
    dy_core, du_skip, dd = _s5_act_bwd(dyg1, dyg2, y_core, proj, d_skip)
    outs = _ssm_core_bwd(proj, dy_core, ops)
    dproj = _s5_dproj(du_skip, outs[0], dz)
    return dproj, outs[1:], dd, dw_glu, db_glu


def _s5_param_grads(params, cots, after=()):
    glr, gli, gdt, gbtr, gbti, gcr, gci = _ssm_operators_bwd(params, cots, after)
    g = SSM_GROUPS
    return (glr.reshape(g, SSM_STATE), gli.reshape(g, SSM_STATE), gdt[:, 0, 0].reshape(1, g),
            gbtr.transpose(0, 2, 1), gbti.transpose(0, 2, 1), gcr, gci)


def _perm_uq(w):
    w3 = w.reshape(w.shape[0], MLA_HEADS, MLA_NOPE + MLA_ROPE)
    return jnp.concatenate([w3[:, :, :64].reshape(-1, 1024), w3[:, :, 64:80].reshape(-1, 256),
                            w3[:, :, 80:96].reshape(-1, 256)], axis=1)


def _unperm_uq(w):
    r = w.shape[0]
    return jnp.concatenate([w[:, :1024].reshape(r, 16, 64), w[:, 1024:1280].reshape(r, 16, 16),
                            w[:, 1280:1536].reshape(r, 16, 16)], axis=2).reshape(r, 1536)


def _perm_ukv(w):
    w3 = w.reshape(w.shape[0], MLA_HEADS, 128)
    return jnp.concatenate([w3[:, :, :64].reshape(-1, 1024), w3[:, :, 64:].reshape(-1, 1024)], axis=1)


def _unperm_ukv(w):
    r = w.shape[0]
    return jnp.concatenate([w[:, :1024].reshape(r, 16, 64), w[:, 1024:].reshape(r, 16, 64)], axis=2).reshape(r, 2048)


def _mla_fwd(proj, gq, gkv, wuq, wukv, cos, sin):
    cqn, ckvn = _mla_norms(proj, gq, gkv)
    qf = _mm(cqn, wuq, name="mla_uq")
    kvf = _mm(ckvn, wukv, name="mla_ukv")
    qh, kh = _mla_pack(qf, kvf, proj, cos, sin)
    o, pout, lse = _mla_attn_fwd(qh, kh, kvf, proj)
    return pout, (cqn, ckvn, kvf, qh, kh, o, lse)


def _mla_bwd(proj, res, dpout, gq, gkv, wuq, wukv, cos, sin):
    cqn, ckvn, kvf, qh, kh, o, lse = res
    dqh, dkh, dv, dz = _mla_attn_bwd(qh, kh, kvf, proj, o, dpout, lse)
    dqf, dkvf, dkr = _mla_unpack(dqh, dkh, dv, cos, sin)
    dcqn = _mm(dqf, wuq, tb=True, name="mla_uq_dx")
    dwuq = _mm(cqn, dqf, ta=True, out_dtype=BF16, name="mla_uq_dw")
    dckvn = _mm(dkvf, wukv, tb=True, name="mla_ukv_dx")
    dwukv = _mm(ckvn, dkvf, ta=True, out_dtype=BF16, name="mla_ukv_dw")
    dproj, dgq, dgkv = _mla_norms_bwd(proj, dcqn, dckvn, dkr, dz, gq, gkv)
    return dproj, dgq, dgkv, dwuq, dwukv


_ANY = pl.BlockSpec(memory_space=pl.ANY)


def _chip_peers():
    x, y, c = lax.axis_index("x"), lax.axis_index("y"), lax.axis_index("c")
    return 2 * x + y, c, [(1 - x, y), (x, 1 - y), (1 - x, 1 - y)]


def _exchange_chips(src_of, dst_of, send_sems, recv_sems, loc_sem):
    me, c, peers = _chip_peers()
    loc = pltpu.make_async_copy(src_of(me), dst_of(me), loc_sem)
    loc.start()

    def copy(k, px, py, src_chip, dst_chip):
        return pltpu.make_async_remote_copy(src_ref=src_of(src_chip), dst_ref=dst_of(dst_chip), send_sem=send_sems.at[k],
                                            recv_sem=recv_sems.at[k], device_id=(px, py, c), device_id_type=MESH)

    sends = [copy(k, px, py, 2 * px + py, me) for k, (px, py) in enumerate(peers)]
    for cp in sends:
        cp.start()
    for k, (px, py) in enumerate(peers):
        copy(k, px, py, me, 2 * px + py).wait_recv()
    for cp in sends:
        cp.wait_send()
    loc.wait()


_EXCH_SCRATCH = [pltpu.SemaphoreType.DMA((3,)), pltpu.SemaphoreType.DMA((3,)), pltpu.SemaphoreType.DMA(())]


def _all_gather_chips(xs, name):
    def body(x_ref, o_ref, send_sems, recv_sems, loc_sem):
        _exchange_chips(lambda s: x_ref, lambda s: o_ref.at[s], send_sems, recv_sems, loc_sem)

    return pl.pallas_call(body, in_specs=[_ANY], out_specs=_ANY, out_shape=S((N_CHIPS,) + xs.shape, xs.dtype),
                          scratch_shapes=_EXCH_SCRATCH, name=name)(xs)


def _sibling_swap(ps, name):
    n = len(ps)

    def body(*refs):
        p_refs, q_refs, send_sems, recv_sems = refs[:n], refs[n:2 * n], refs[2 * n], refs[2 * n + 1]
        x, y, c = lax.axis_index("x"), lax.axis_index("y"), lax.axis_index("c")
        cps = [pltpu.make_async_remote_copy(src_ref=p_refs[i], dst_ref=q_refs[i], send_sem=send_sems.at[i],
                                            recv_sem=recv_sems.at[i], device_id=(x, y, 1 - c), device_id_type=MESH)
               for i in range(n)]
        for cp in cps:
            cp.start()
        for cp in cps:
            cp.wait()

    return pl.pallas_call(body, in_specs=[_ANY] * n, out_specs=[_ANY] * n, out_shape=[S(p.shape, p.dtype) for p in ps],
                          scratch_shapes=[pltpu.SemaphoreType.DMA((n,)), pltpu.SemaphoreType.DMA((n,))], name=name)(*ps)


def _quarter_spec(tr, r, c, dim):
    if dim is None:
        return pl.BlockSpec((1, tr, c), lambda i, me: (me[0], i, 0))
    if dim == 0:
        return pl.BlockSpec((tr, c), lambda i, me: (me[0] * (r // tr) + i, 0))
    return pl.BlockSpec((tr, c), lambda i, me: (i, me[0]))


def _whole_shape(r, c, dim):
    return (N_CHIPS, r, c) if dim is None else ((N_CHIPS * r, c) if dim == 0 else (r, N_CHIPS * c))


def _row_tile(r):
    return 256 if r % 256 == 0 else (128 if r % 128 == 0 else r)


def _cast_place(me, x, dim, dtype, name):
    r, c = x.shape
    tr = _row_tile(r)

    def body(me_ref, x_ref, o_ref):
        o_ref[...] = x_ref[...].astype(dtype).reshape(o_ref.shape)

    return pl.pallas_call(
        body, out_shape=S(_whole_shape(r, c, dim), dtype), name=name,
        grid_spec=pltpu.PrefetchScalarGridSpec(num_scalar_prefetch=1, grid=(r // tr,),
                                               in_specs=[pl.BlockSpec((tr, c), lambda i, me: (i, 0))],
                                               out_specs=_quarter_spec(tr, r, c, dim)),
        compiler_params=_cp(("parallel",)))(me, x)


def _sum_parts(me, g, land, dim, name):
    _, r, c = land.shape
    tr = _row_tile(r)

    def body(me_ref, g_ref, l_ref, o_ref):
        own = g_ref[...].astype(F32).reshape(tr, c)
        o_ref[...] = ((own + l_ref[0].astype(F32)) + l_ref[1].astype(F32)) + l_ref[2].astype(F32)

    return pl.pallas_call(
        body, out_shape=S((r, c), F32), name=name,
        grid_spec=pltpu.PrefetchScalarGridSpec(
            num_scalar_prefetch=1, grid=(r // tr,),
            in_specs=[_quarter_spec(tr, r, c, dim), pl.BlockSpec((3, tr, c), lambda i, me: (0, i, 0))],
            out_specs=pl.BlockSpec((tr, c), lambda i, me: (i, 0))),
        compiler_params=_cp(("parallel",)))(me, g, land)


_HBM = pl.BlockSpec(memory_space=pltpu.HBM)
_SEM = pl.BlockSpec(memory_space=pltpu.SEMAPHORE)
_EFFECT = pltpu.SideEffectType.DATAFLOW_SIDE_EFFECTING


def _quarter(ref, s, dim):
    n = ref.shape[dim] // N_CHIPS
    start = pl.multiple_of(s * n, n)
    return ref.at[pl.ds(start, n), :] if dim == 0 else ref.at[:, pl.ds(start, n)]


def _part(ref, s, dim):
    return ref.at[s] if dim is None else _quarter(ref, s, dim)


def _gather_plan(i, dim):
    return dict(src=i, dst=i, send_src=lambda ref, me, peer, k: _part(ref, me, dim),
                send_dst=lambda ref, me, peer, k: _part(ref, me, dim), recv_dst=lambda ref, me, peer, k: _part(ref, peer, dim))


def _scatter_plan(i, j, dim):
    return dict(src=i, dst=j, send_src=lambda ref, me, peer, k: _part(ref, peer, dim),
                send_dst=lambda ref, me, peer, k: ref.at[k], recv_dst=lambda ref, me, peer, k: ref.at[k])


def _hbm(a):
    return pltpu.with_memory_space_constraint(a, pltpu.HBM)


def _remote_copies(refs, send_sems, recv_sems, plans):
    me, c, peers = _chip_peers()
    sends, recvs = [], []
    for i, p in enumerate(plans):
        for k, (px, py) in enumerate(peers):
            peer = 2 * px + py
            sems = dict(send_sem=send_sems.at[3 * i + k], recv_sem=recv_sems.at[3 * i + k], device_id=(px, py, c),
                        device_id_type=MESH)
            src = p['send_src'](refs[p['src']], me, peer, k)
            sends.append(pltpu.make_async_remote_copy(src_ref=src, dst_ref=p['send_dst'](refs[p['dst']], me, peer, k), **sems))
            recvs.append(pltpu.make_async_remote_copy(src_ref=src, dst_ref=p['recv_dst'](refs[p['dst']], me, peer, k), **sems))
    return sends, recvs


def _exchange_start(arrays, plans, name, after=()):
    n, m, extra = len(arrays), len(plans), len(after)

    def body(*refs):
        send_sems, recv_sems, token = refs[n + extra], refs[n + extra + 1], refs[-1]
        sends, _ = _remote_copies(refs[:n], send_sems, recv_sems, plans)
        for cp in sends:
            cp.start()
        token[...] = jnp.zeros_like(token)

    outs = pl.pallas_call(
        body, name=name,
        out_shape=(pltpu.SemaphoreType.DMA((3 * m,)), pltpu.SemaphoreType.DMA((3 * m,)),
                   *[pltpu.HBM(a.shape, a.dtype) for a in arrays], S((8, 128), F32)),
        in_specs=[_HBM] * n + [_ANY] * extra, out_specs=(_SEM, _SEM, *[_HBM] * n, pl.BlockSpec(memory_space=pltpu.VMEM)),
        input_output_aliases={i: 2 + i for i in range(n)},
        compiler_params=pltpu.CompilerParams(has_side_effects=_EFFECT))(*[_hbm(a) for a in arrays], *after)
    return outs[0], outs[1], outs[2:2 + n], outs[-1]


def _exchange_wait(started, plans, after, name):
    send_sems, recv_sems, arrays, _ = started
    n = len(arrays)

    def body(*refs):
        sends, recvs = _remote_copies(refs[:n], refs[n], refs[n + 1], plans)
        for cp in sends:
            cp.wait_send()
        for cp in recvs:
            cp.wait_recv()

    return pl.pallas_call(
        body, name=name, out_shape=tuple(pltpu.HBM(a.shape, a.dtype) for a in arrays),
        in_specs=[_HBM] * n + [_SEM, _SEM, _ANY], out_specs=tuple([_HBM] * n),
        input_output_aliases={i: i for i in range(n)},
        compiler_params=pltpu.CompilerParams(has_side_effects=_EFFECT))(*arrays, send_sems, recv_sems, after)


def _add2(p, q):
    r, c = p.shape
    tr = 128 if r % 128 == 0 else r

    def body(p_ref, q_ref, o_ref):
        o_ref[...] = p_ref[...] + q_ref[...]

    return pl.pallas_call(body, grid=(r // tr,), in_specs=[_row_spec(tr, c), _row_spec(tr, c)], out_specs=_row_spec(tr, c),
                          out_shape=S((r, c), F32), name="add_cores", compiler_params=_cp(("parallel",)))(p, q)


def _adamw(w, g, m, v, name, g2=None):
    r, c = w.shape
    tr = 256 if r % 256 == 0 else (128 if r % 128 == 0 else r)
    c1 = 1.0 / (1.0 - ADAM_B1 ** ADAM_STEP)
    c2 = 1.0 / (1.0 - ADAM_B2 ** ADAM_STEP)
    gs = [g] if g2 is None else [g, g2]

    def body(*refs):
        w_ref, m_ref, v_ref = refs[0], refs[1], refs[2]
        g_refs = refs[3:3 + len(gs)]
        go_ref, d_ref, nm_ref, nv_ref = refs[3 + len(gs):]
        gv = g_refs[0][...]
        if len(gs) == 2:
            gv = gv + g_refs[1][...]
        go_ref[...] = gv
        nm = ADAM_B1 * m_ref[...] + (1.0 - ADAM_B1) * gv
        nv = ADAM_B2 * v_ref[...] + (1.0 - ADAM_B2) * (gv * gv)
        nm_ref[...] = nm
        nv_ref[...] = nv
        d_ref[...] = -ADAM_LR * ((nm * c1) / (jnp.sqrt(nv * c2) + ADAM_EPS) + ADAM_WD * w_ref[...])

    return pl.pallas_call(body, grid=(r // tr,), in_specs=[_row_spec(tr, c)] * (3 + len(gs)),
                          out_specs=[_row_spec(tr, c)] * 4, out_shape=[S((r, c), F32)] * 4, name=name,
                          compiler_params=_cp(("parallel",)))(w, m, v, *gs)


_WEIGHTS = ['pre_norm', 'post_norm', 'rel_bias', 'a_w_in', 'a_lam_re', 'a_lam_im', 'a_log_dt', 'a_b_re', 'a_b_im',
            'a_c_re', 'a_c_im', 'a_d', 'a_w_glu', 'a_b_glu', 'a_w_out', 'b_w_in', 'b_sinks', 'b_w_out', 'c_w_in',
            'c_q_norm', 'c_kv_norm', 'c_w_uq', 'c_w_ukv', 'c_w_out', 'd_w_in', 'd_ln_g', 'd_ln_b', 'd_w_s', 'd_b_s',
            'd_w_out']
_BIG = [('a_w_in', 1024, 2048, 1), ('a_w_glu', 1024, 1024, 0), ('a_w_out', 1024, 1024, 0), ('b_w_in', 1024, 2304, 1),
        ('b_w_out', 1024, 1024, 0), ('c_w_in', 1024, 2080, 1), ('c_w_uq', 768, 1536, 1), ('c_w_ukv', 256, 2048, 1),
        ('c_w_out', 1024, 1024, 0), ('d_w_in', 1024, 3072, 1), ('d_w_out', 1024, 1024, 0)]
_SHARDED_VECS = [('c_q_norm', 768), ('c_kv_norm', 256), ('d_ln_g', 1024), ('d_ln_b', 1024)]
_REPLICATED = [n for n in _WEIGHTS if n not in [b[0] for b in _BIG] and n not in [s[0] for s in _SHARDED_VECS]]
_LANES = 1024
_SMALL_Q_ROWS = 136


def kernel(x, pre_norm, post_norm, rel_bias, a_w_in, a_lam_re, a_lam_im, a_log_dt, a_b_re, a_b_im, a_c_re, a_c_im, a_d, a_w_glu, a_b_glu, a_w_out, b_w_in, b_sinks, b_w_out, c_w_in, c_q_norm, c_kv_norm, c_w_uq, c_w_ukv, c_w_out, d_w_in, d_ln_g, d_ln_b, d_w_s, d_b_s, d_w_out, loss_target, m_pre_norm, m_post_norm, m_rel_bias, m_a_w_in, m_a_lam_re, m_a_lam_im, m_a_log_dt, m_a_b_re, m_a_b_im, m_a_c_re, m_a_c_im, m_a_d, m_a_w_glu, m_a_b_glu, m_a_w_out, m_b_w_in, m_b_sinks, m_b_w_out, m_c_w_in, m_c_q_norm, m_c_kv_norm, m_c_w_uq, m_c_w_ukv, m_c_w_out, m_d_w_in, m_d_ln_g, m_d_ln_b, m_d_w_s, m_d_b_s, m_d_w_out, v_pre_norm, v_post_norm, v_rel_bias, v_a_w_in, v_a_lam_re, v_a_lam_im, v_a_log_dt, v_a_b_re, v_a_b_im, v_a_c_re, v_a_c_im, v_a_d, v_a_w_glu, v_a_b_glu, v_a_w_out, v_b_w_in, v_b_sinks, v_b_w_out, v_c_w_in, v_c_q_norm, v_c_kv_norm, v_c_w_uq, v_c_w_ukv, v_c_w_out, v_d_w_in, v_d_ln_g, v_d_ln_b, v_d_w_s, v_d_b_s, v_d_w_out):
    arg = dict(locals())
    me = 2 * lax.axis_index("x") + lax.axis_index("y")
    xin = x[0]
    t = xin.shape[0]

    big = {n: (r, c, dim) for n, r, c, dim in _BIG}

    me1 = me.reshape(1).astype(jnp.int32)

    def gather_start(names, tag, after):
        wholes, plans = [], []
        for i, n in enumerate(names):
            if n in big:
                r, c, dim = big[n]
                src, dtype = arg[n][0], BF16
                if dim == 1 and (c // N_CHIPS) % 128:
                    dim = None
            else:
                src, dim, dtype = arg[n], None, F32
            wholes.append(_cast_place(me1, src, dim, dtype, "place_" + n))
            plans.append(_gather_plan(i, dim))
        return _exchange_start(wholes, plans, "gather_start_" + tag, after=after), plans

    def gather_wait(names, started, after, tag):
        st, plans = started
        out = {}
        for n, a in zip(names, _exchange_wait(st, plans, after, "gather_wait_" + tag)):
            if n in big:
                r, c, dim = big[n]
                out[n] = a if a.ndim == 2 else a.transpose(1, 0, 2).reshape(r, c)
            else:
                out[n] = a.reshape(1, -1)
        return out

    groups = {'a1': ['a_w_in'], 'a2': ['a_w_glu', 'a_w_out'], 'b': ['b_w_in', 'b_w_out'],
              'c': ['c_w_in', 'c_w_uq', 'c_w_ukv', 'c_w_out', 'c_q_norm', 'c_kv_norm'],
              'd': ['d_w_in', 'd_w_out', 'd_ln_g', 'd_ln_b']}
    started, tokens = {}, []
    for tag, names in groups.items():
        started[tag] = gather_start(names, tag, tokens)
        tokens = [started[tag][0][3]]
    wt = gather_wait(groups['a1'], started['a1'], xin, 'a1')
    cos, sin = _rope_tables(t)
    bucket = jnp.asarray(_t5_bucket_table())
    s5p = _s5_params(a_lam_re[0], a_lam_im[0], a_log_dt[0], a_b_re[0], a_b_im[0], a_c_re[0], a_c_im[0])
    bst = d_b_s[0].T

    x0 = xin
    h0 = _prenorm(x0, pre_norm[0:1], "pre0", after=tokens)
    proj0 = _mm(h0, wt['a_w_in'], name="a_in")
    wt.update(gather_wait(groups['a2'], started['a2'], proj0, 'a2'))
    p0, res0 = _s5_fwd(proj0, s5p, a_d, wt['a_w_glu'], a_b_glu)
    y0 = _mm(p0, wt['a_w_out'], name="a_out")
    x1 = _postnorm(x0, y0, post_norm[0:1], "post0")

    wt.update(gather_wait(groups['b'], started['b'], x1, 'b'))
    h1 = _prenorm(x1, pre_norm[1:2], "pre1")
    proj1 = _mm(h1, wt['b_w_in'], name="b_in")
    bias = _swa_bias(rel_bias, bucket)
    p1 = _swa_fwd(proj1, bias, b_sinks)
    y1 = _mm(p1, wt['b_w_out'], name="b_out")
    x2 = _postnorm(x1, y1, post_norm[1:2], "post1")

    wt.update(gather_wait(groups['c'], started['c'], x2, 'c'))
    w_c_in = jnp.concatenate([wt['c_w_in'][:, :1056], jnp.zeros((1024, 96), BF16), wt['c_w_in'][:, 1056:],
                              jnp.zeros((1024, MLA_PROJ - MLA_Z1), BF16)], axis=1)
    w_uq = _perm_uq(wt['c_w_uq'])
    w_ukv = _perm_ukv(wt['c_w_ukv'])
    h2 = _prenorm(x2, pre_norm[2:3], "pre2")
    proj2 = _mm(h2, w_c_in, name="c_in")
    p2, res2 = _mla_fwd(proj2, wt['c_q_norm'], wt['c_kv_norm'], w_uq, w_ukv, cos, sin)
    y2 = _mm(p2, wt['c_w_out'], name="c_out")
    x3 = _postnorm(x2, y2, post_norm[2:3], "post2")

    wt.update(gather_wait(groups['d'], started['d'], x3, 'd'))
    vec = wt
    h3 = _prenorm(x3, pre_norm[3:4], "pre3")
    proj3 = _mm(h3, wt['d_w_in'], name="d_in")
    p3 = _sgu_fwd(proj3, vec['d_ln_g'], vec['d_ln_b'], d_w_s[0], bst)
    y3 = _mm(p3, wt['d_w_out'], name="d_out")
    x4 = _postnorm(x3, y3, post_norm[3:4], "post3")

    loss_part, dx = _loss_head(x4, loss_target[0])
    loss = lax.psum(loss_part[0, 0], ("x", "y", "c"))

    gr = {}
    d_pre, d_post = [None] * 4, [None] * 4
    scattered = []

    def scatter_start(items, tag):
        srcs = [a for _, a, _ in items]
        n = len(items)
        plans = [_scatter_plan(i, n + i, dim) for i, (_, _, dim) in enumerate(items)]
        lands = []
        for _, a, dim in items:
            part = a.shape[1:] if dim is None else tuple(s // N_CHIPS if i == dim else s for i, s in enumerate(a.shape))
            lands.append(lax.empty((3,) + part, a.dtype))
        st = _exchange_start(srcs + lands, plans, "scatter_start_" + tag)
        scattered.append((st, plans, items, tag))
        return [st[3]]

    def stacked(g):
        r, c = g.shape
        return g.reshape(r, N_CHIPS, c // N_CHIPS).transpose(1, 0, 2)

    dy, d_post[3] = _postnorm_bwd(dx, y3, post_norm[3:4], "post3_bwd")
    dp = _mm(dy, wt['d_w_out'], tb=True, name="d_out_dx")
    g_out = _mm(p3, dy, ta=True, out_dtype=BF16, name="d_out_dw")
    dproj, gr['d_w_s'], dbst, gr['d_ln_g'], gr['d_ln_b'] = _sgu_bwd(proj3, dp, vec['d_ln_g'], vec['d_ln_b'], d_w_s[0], bst)
    gr['d_b_s'] = dbst.T
    dh = _mm(dproj, wt['d_w_in'], tb=True, name="d_in_dx")
    g_in = _mm(h3, dproj, ta=True, out_dtype=BF16, name="d_in_dw")
    tok = scatter_start([('d_w_in', g_in, 1), ('d_w_out', g_out, 0)], 'd')
    dx, d_pre[3] = _prenorm_bwd(dx, dh, x3, pre_norm[3:4], "pre3_bwd")

    dy, d_post[2] = _postnorm_bwd(dx, y2, post_norm[2:3], "post2_bwd", after=tok)
    dp = _mm(dy, wt['c_w_out'], tb=True, name="c_out_dx")
    g_out = _mm(p2, dy, ta=True, out_dtype=BF16, name="c_out_dw")
    dproj, gr['c_q_norm'], gr['c_kv_norm'], dwuq, dwukv = _mla_bwd(proj2, res2, dp, wt['c_q_norm'], wt['c_kv_norm'],
                                                                  w_uq, w_ukv, cos, sin)
    dh = _mm(dproj, w_c_in, tb=True, name="c_in_dx")
    dwc = _mm(h2, dproj, ta=True, out_dtype=BF16, name="c_in_dw")
    g_in = stacked(jnp.concatenate([dwc[:, :1056], dwc[:, MLA_Z0:MLA_Z1]], axis=1))
    tok = scatter_start([('c_w_in', g_in, None), ('c_w_uq', _unperm_uq(dwuq), 1), ('c_w_ukv', _unperm_ukv(dwukv), 1),
                         ('c_w_out', g_out, 0)], 'c')
    dx, d_pre[2] = _prenorm_bwd(dx, dh, x2, pre_norm[2:3], "pre2_bwd")

    dy, d_post[1] = _postnorm_bwd(dx, y1, post_norm[1:2], "post1_bwd", after=tok)
    dp = _mm(dy, wt['b_w_out'], tb=True, name="b_out_dx")
    g_out = _mm(p1, dy, ta=True, out_dtype=BF16, name="b_out_dw")
    dqz, dkv, dss, dsk = _swa_bwd(proj1, dp, bias, b_sinks)
    gr['rel_bias'], gr['b_sinks'] = _swa_dbias(dss, dsk, bucket)
    dproj = jnp.concatenate([dqz[:, :1024], dkv.astype(BF16), dqz[:, 1024:]], axis=1)
    dh = _mm(dproj, wt['b_w_in'], tb=True, name="b_in_dx")
    g_in = stacked(_mm(h1, dproj, ta=True, out_dtype=BF16, name="b_in_dw"))
    tok = scatter_start([('b_w_in', g_in, None), ('b_w_out', g_out, 0)], 'b')
    dx, d_pre[1] = _prenorm_bwd(dx, dh, x1, pre_norm[1:2], "pre1_bwd")

    dy, d_post[0] = _postnorm_bwd(dx, y0, post_norm[0:1], "post0_bwd", after=tok)
    dp = _mm(dy, wt['a_w_out'], tb=True, name="a_out_dx")
    g_out = _mm(p0, dy, ta=True, out_dtype=BF16, name="a_out_dw")
    dproj, s5cots, gr['a_d'], g_glu, gr['a_b_glu'] = _s5_bwd(proj0, res0, dp, s5p, a_d, wt['a_w_glu'], a_b_glu)
    tok = scatter_start([('a_w_glu', g_glu, 0), ('a_w_out', g_out, 0)], 'a2')
    g_in = _mm(h0, dproj, ta=True, out_dtype=BF16, name="a_in_dw", after=tok)
    tok = scatter_start([('a_w_in', g_in, 1)], 'a1')
    dh = _mm(dproj, wt['a_w_in'], tb=True, name="a_in_dx", after=tok)
    dx, d_pre[0] = _prenorm_bwd(dx, dh, x0, pre_norm[0:1], "pre0_bwd")
    s5g = _s5_param_grads(s5p, s5cots, after=tok)
    for n, g in zip(['a_lam_re', 'a_lam_im', 'a_log_dt', 'a_b_re', 'a_b_im', 'a_c_re', 'a_c_im'], s5g):
        gr[n] = g
    gr['pre_norm'] = jnp.concatenate(d_pre, axis=0)
    gr['post_norm'] = jnp.concatenate(d_post, axis=0)
    small_names = _REPLICATED + [n for n, _ in _SHARDED_VECS]
    small = jnp.concatenate([gr[n].reshape(-1) for n in small_names])
    small = jnp.pad(small, (0, N_CHIPS * _SMALL_Q_ROWS * _LANES - small.shape[0])).reshape(N_CHIPS, _SMALL_Q_ROWS, _LANES)
    scatter_start([('small', small, None)], 'a0')

    partial = {}
    for st, plans, items, tag in scattered:
        done = _exchange_wait(st, plans, small, "scatter_wait_" + tag)
        for i, (n, _, dim) in enumerate(items):
            partial[n] = _sum_parts(me1, done[i], done[len(items) + i], dim, "sum_parts_" + n)
    order = [n for n, _, _, _ in _BIG] + ['small']
    other = dict(zip(order, _sibling_swap([partial[n] for n in order], "swap_cores")))
    small_q = _add2(partial['small'], other['small'])
    small_all = _all_gather_chips(small_q, "gather_small_grads").reshape(-1)
    grads = {}
    off = 0
    for n in _REPLICATED:
        sz = math.prod(arg[n].shape)
        grads[n] = small_all[off:off + sz].reshape(arg[n].shape)
        off += sz
    for n, sz in _SHARDED_VECS:
        q = sz // N_CHIPS
        grads[n] = lax.dynamic_slice(small_all, (off + me * q,), (q,)).reshape(arg[n].shape)
        off += sz

    delta, new_m, new_v = {}, {}, {}
    for n, _, _, _ in _BIG:
        shp = arg[n].shape
        two = (shp[1], shp[2])
        outs = _adamw(arg[n].reshape(two), partial[n], arg['m_' + n].reshape(two), arg['v_' + n].reshape(two),
                      "adamw_" + n, g2=other[n])
        grads[n], delta[n], new_m[n], new_v[n] = [o.reshape(shp) for o in outs]

    for n in small_names:
        shp = arg[n].shape
        two = (math.prod(shp[:-1]), shp[-1])
        outs = _adamw(arg[n].reshape(two), grads[n].reshape(two), arg['m_' + n].reshape(two), arg['v_' + n].reshape(two),
                      "adamw_" + n)
        delta[n], new_m[n], new_v[n] = [o.reshape(shp) for o in outs[1:]]

    return (loss, dx.reshape(x.shape), *[grads[n] for n in _WEIGHTS], *[delta[n] for n in _WEIGHTS],
            *[new_m[n] for n in _WEIGHTS], *[new_v[n] for n in _WEIGHTS])
```

```python
import functools
import math

import numpy as np
import jax
import jax.numpy as jnp
from jax import lax
from jax.experimental import pallas as pl
from jax.experimental.pallas import tpu as pltpu

F32 = jnp.float32
BF16 = jnp.bfloat16
S = jax.ShapeDtypeStruct
MESH = pl.DeviceIdType.MESH

D_MODEL = 1024
EPS = 1e-6
NEG_INF = -1e30
N_CHIPS = 4

SSM_GROUPS = 64
SSM_GROUP = 16
SSM_STATE = 64
SSM_CHUNK = 16
SSM_BLOCK = SSM_CHUNK * SSM_GROUP

HEAD_DIM = 64
SWA_HEADS = 16
SWA_GROUP = 8
WINDOW = 128
REL_BUCKETS = 32
REL_MAX_DIST = 128

MLA_HEADS = 16
MLA_NOPE = 64
MLA_ROPE = 32
MLA_Q_RANK = 768
MLA_KV_RANK = 256
MLA_SCALE = (MLA_NOPE + MLA_ROPE) ** -0.5
MLA_BQ = 512
MLA_BK = 512
ROPE_BASE = 10000.0

SGU_CHUNK = 128
SGU_GROUPS = 16
SGU_GDIM = 64

ADAM_LR = 0.001
ADAM_B1 = 0.9
ADAM_B2 = 0.999
ADAM_EPS = 1e-08
ADAM_WD = 0.01
ADAM_STEP = 10

VMEM_BIG = 56 * 1024 * 1024


def _cp(sem, vmem=None):
    return pltpu.CompilerParams(dimension_semantics=sem, vmem_limit_bytes=vmem)


def _row_spec(tr, c):
    return pl.BlockSpec((tr, c), lambda i: (i, 0))


def _full_spec(shape):
    nd = len(shape)
    return pl.BlockSpec(shape, lambda *_: (0,) * nd)


def _gelu(x):
    c = 0.7978845608028654
    return 0.5 * x * (1.0 + jnp.tanh(c * (x + 0.044715 * x * x * x)))


def _gelu_grad(x):
    c = 0.7978845608028654
    t = jnp.tanh(c * (x + 0.044715 * x * x * x))
    return 0.5 * (1.0 + t) + 0.5 * x * (1.0 - t * t) * c * (1.0 + 3.0 * 0.044715 * x * x)


def _silu(z):
    return z * jax.nn.sigmoid(z)


def _silu_grad(z):
    s = jax.nn.sigmoid(z)
    return s * (1.0 + z * (1.0 - s))


def _dot(a, b):
    return lax.dot_general(a, b, (((1,), (0,)), ((), ())), preferred_element_type=F32)


def _dot_nt(a, b):
    return lax.dot_general(a, b, (((1,), (1,)), ((), ())), preferred_element_type=F32)


def _dot_tn(a, b):
    return lax.dot_general(a, b, (((0,), (0,)), ((), ())), preferred_element_type=F32)


def _pick(n, cap):
    best = 0
    for d in range(128, min(n, cap) + 1, 128):
        if n % d == 0:
            best = d
    if best < 256:
        return n
    return best


def _mm(a, b, *, ta=False, tb=False, out_dtype=F32, name, after=()):
    m, k = (a.shape[1], a.shape[0]) if ta else a.shape
    n = b.shape[0] if tb else b.shape[1]
    assert (b.shape[1] if tb else b.shape[0]) == k
    tm, tn, tk = _pick(m, 1024), _pick(n, 1024), _pick(k, 1024)
    if tn > 1024:
        tm = _pick(m, 256)
    if tk > 1024:
        tm, tn = _pick(m, 256), _pick(n, 256)
    nk = k // tk
    dims = (((0 if ta else 1,), (1 if tb else 0,)), ((), ()))

    def body(a_ref, b_ref, *rest):
        o_ref, acc_ref = rest[-2], rest[-1]
        kk = pl.program_id(2)

        @pl.when(kk == 0)
        def _():
            acc_ref[...] = jnp.zeros_like(acc_ref)

        acc_ref[...] += lax.dot_general(a_ref[...].astype(BF16), b_ref[...].astype(BF16), dims,
                                        preferred_element_type=F32)

        @pl.when(kk == nk - 1)
        def _():
            o_ref[...] = acc_ref[...].astype(out_dtype)

    a_spec = pl.BlockSpec((tk, tm), lambda i, j, kk: (kk, i)) if ta else pl.BlockSpec((tm, tk), lambda i, j, kk: (i, kk))
    b_spec = pl.BlockSpec((tn, tk), lambda i, j, kk: (j, kk)) if tb else pl.BlockSpec((tk, tn), lambda i, j, kk: (kk, j))
    return pl.pallas_call(
        body, grid=(m // tm, n // tn, nk), in_specs=[a_spec, b_spec] + [pl.BlockSpec(memory_space=pl.ANY)] * len(after),
        out_specs=pl.BlockSpec((tm, tn), lambda i, j, kk: (i, j)), out_shape=S((m, n), out_dtype),
        scratch_shapes=[pltpu.VMEM((tm, tn), F32)], name=name,
        compiler_params=_cp(("parallel", "parallel", "arbitrary"), VMEM_BIG))(a, b, *after)


def _prenorm(x, g, name, after=()):
    t, d = x.shape
    tr = 256

    def body(x_ref, g_ref, *rest):
        o_ref = rest[-1]
        xv = x_ref[...]
        r = lax.rsqrt(jnp.mean(xv * xv, axis=-1, keepdims=True) + EPS)
        o_ref[...] = (xv * r * g_ref[...]).astype(BF16)

    return pl.pallas_call(body, grid=(t // tr,),
                          in_specs=[_row_spec(tr, d), _full_spec((1, d))] + [pl.BlockSpec(memory_space=pl.ANY)] * len(after),
                          out_specs=_row_spec(tr, d), out_shape=S((t, d), BF16), name=name,
                          compiler_params=_cp(("parallel",)))(x, g, *after)


def _postnorm(x, y, g, name):
    t, d = x.shape
    tr = 256

    def body(x_ref, y_ref, g_ref, o_ref):
        yv = y_ref[...]
        r = lax.rsqrt(jnp.mean(yv * yv, axis=-1, keepdims=True) + EPS)
        o_ref[...] = x_ref[...] + yv * r * g_ref[...]

    return pl.pallas_call(body, grid=(t // tr,), in_specs=[_row_spec(tr, d), _row_spec(tr, d), _full_spec((1, d))],
                          out_specs=_row_spec(tr, d), out_shape=S((t, d), F32), name=name,
                          compiler_params=_cp(("parallel",)))(x, y, g)


def _rms_bwd_rows(dout, xin, g):
    r = lax.rsqrt(jnp.mean(xin * xin, axis=-1, keepdims=True) + EPS)
    xh = xin * r
    gd = dout * g
    dx = r * (gd - xh * jnp.mean(gd * xh, axis=-1, keepdims=True))
    return dx, jnp.sum(dout * xh, axis=0, keepdims=True)


def _postnorm_bwd(dxn, y, g, name, after=()):
    t, d = y.shape
    tr = 256

    def body(dx_ref, y_ref, g_ref, *rest):
        dy_ref, dg_ref = rest[-2], rest[-1]

        @pl.when(pl.program_id(0) == 0)
        def _():
            dg_ref[...] = jnp.zeros_like(dg_ref)

        dy, dg = _rms_bwd_rows(dx_ref[...], y_ref[...], g_ref[...])
        dy_ref[...] = dy.astype(BF16)
        dg_ref[...] += dg

    return pl.pallas_call(body, grid=(t // tr,),
                          in_specs=[_row_spec(tr, d), _row_spec(tr, d), _full_spec((1, d))]
                          + [pl.BlockSpec(memory_space=pl.ANY)] * len(after),
                          out_specs=[_row_spec(tr, d), _full_spec((1, d))],
                          out_shape=[S((t, d), BF16), S((1, d), F32)], name=name,
                          compiler_params=_cp(("arbitrary",)))(dxn, y, g, *after)


def _prenorm_bwd(dxn, dh, x, g, name):
    t, d = x.shape
    tr = 256

    def body(dxn_ref, dh_ref, x_ref, g_ref, dx_ref, dg_ref):
        @pl.when(pl.program_id(0) == 0)
        def _():
            dg_ref[...] = jnp.zeros_like(dg_ref)

        dx, dg = _rms_bwd_rows(dh_ref[...], x_ref[...], g_ref[...])
        dx_ref[...] = dxn_ref[...] + dx
        dg_ref[...] += dg

    return pl.pallas_call(body, grid=(t // tr,),
                          in_specs=[_row_spec(tr, d), _row_spec(tr, d), _row_spec(tr, d), _full_spec((1, d))],
                          out_specs=[_row_spec(tr, d), _full_spec((1, d))],
                          out_shape=[S((t, d), F32), S((1, d), F32)], name=name,
                          compiler_params=_cp(("arbitrary",)))(dxn, dh, x, g)


def _loss_head(y, target):
    t, d = y.shape
    tr = 256

    def body(y_ref, t_ref, l_ref, dy_ref):
        @pl.when(pl.program_id(0) == 0)
        def _():
            l_ref[...] = jnp.zeros_like(l_ref)

        e = y_ref[...] - t_ref[...]
        dy_ref[...] = e * (1.0 / d)
        l_ref[...] += 0.5 * jnp.sum(jnp.mean(e * e, axis=-1, keepdims=True), axis=0, keepdims=True)

    return pl.pallas_call(body, grid=(t // tr,), in_specs=[_row_spec(tr, d), _row_spec(tr, d)],
                          out_specs=[_full_spec((1, 1)), _row_spec(tr, d)],
                          out_shape=[S((1, 1), F32), S((t, d), F32)], name="loss_head",
                          compiler_params=_cp(("arbitrary",)))(y, target)


def _sgu_parts(proj, lg, lb, ws_ref, bst):
    a = proj[:, 0:1024]
    bb = proj[:, 1024:2048]
    z = proj[:, 2048:3072]
    u = _gelu(a)
    vp = _gelu(bb)
    mu = jnp.mean(vp, axis=-1, keepdims=True)
    xc = vp - mu
    rstd = lax.rsqrt(jnp.mean(xc * xc, axis=-1, keepdims=True) + EPS)
    xh = xc * rstd
    vn = xh * lg + lb
    row = lax.broadcasted_iota(jnp.int32, (SGU_CHUNK, SGU_CHUNK), 0)
    col = lax.broadcasted_iota(jnp.int32, (SGU_CHUNK, SGU_CHUNK), 1)
    tril = row >= col
    ws = [jnp.where(tril, ws_ref[g], 0.0).astype(BF16) for g in range(SGU_GROUPS)]
    s = jnp.concatenate(
        [_dot(ws[g], vn[:, g * 64:(g + 1) * 64].astype(BF16)) + bst[:, g:g + 1] for g in range(SGU_GROUPS)], axis=1)
    return a, bb, z, u, vp, rstd, xh, vn, ws, tril, s


def _sgu_fwd(proj, lg, lb, w_s, bst):
    t = proj.shape[0]
    tr = SGU_CHUNK

    def body(p_ref, lg_ref, lb_ref, ws_ref, bst_ref, o_ref):
        _, _, z, u, _, _, _, _, _, _, s = _sgu_parts(p_ref[...], lg_ref[...], lb_ref[...], ws_ref, bst_ref[...])
        o_ref[...] = (u * s * _silu(z)).astype(BF16)

    return pl.pallas_call(
        body, grid=(t // tr,),
        in_specs=[_row_spec(tr, 3072), _full_spec((1, 1024)), _full_spec((1, 1024)), _full_spec((16, 128, 128)),
                  _full_spec((128, 16))],
        out_specs=_row_spec(tr, 1024), out_shape=S((t, 1024), BF16), name="sgu_fwd",
        compiler_params=_cp(("parallel",)))(proj, lg, lb, w_s, bst)


def _sgu_bwd(proj, dp, lg, lb, w_s, bst):
    t = proj.shape[0]
    tr = SGU_CHUNK

    def body(p_ref, dp_ref, lg_ref, lb_ref, ws_ref, bst_ref, dproj_ref, dws_ref, dbst_ref, dlg_ref, dlb_ref):
        @pl.when(pl.program_id(0) == 0)
        def _():
            dws_ref[...] = jnp.zeros_like(dws_ref)
            dbst_ref[...] = jnp.zeros_like(dbst_ref)
            dlg_ref[...] = jnp.zeros_like(dlg_ref)
            dlb_ref[...] = jnp.zeros_like(dlb_ref)

        lgv = lg_ref[...]
        a, bb, z, u, vp, rstd, xh, vn, ws, tril, s = _sgu_parts(p_ref[...], lgv, lb_ref[...], ws_ref, bst_ref[...])
        dpv = dp_ref[...].astype(F32)
        sz = _silu(z)
        du = dpv * s * sz
        ds = dpv * u * sz
        dz = dpv * u * s * _silu_grad(z)
        dvn_parts, dbs_cols = [], []
        for g in range(SGU_GROUPS):
            dsg = ds[:, g * 64:(g + 1) * 64]
            dsg_b = dsg.astype(BF16)
            dvn_parts.append(_dot_tn(ws[g], dsg_b))
            dws_ref[g] += jnp.where(tril, _dot_nt(dsg_b, vn[:, g * 64:(g + 1) * 64].astype(BF16)), 0.0)
            dbs_cols.append(jnp.sum(dsg, axis=1, keepdims=True))
        dvn = jnp.concatenate(dvn_parts, axis=1)
        dbst_ref[...] += jnp.concatenate(dbs_cols, axis=1)
        dlg_ref[...] += jnp.sum(dvn * xh, axis=0, keepdims=True)
        dlb_ref[...] += jnp.sum(dvn, axis=0, keepdims=True)
        dxh = dvn * lgv
        dvp = rstd * (dxh - jnp.mean(dxh, axis=-1, keepdims=True) - xh * jnp.mean(dxh * xh, axis=-1, keepdims=True))
        dproj_ref[:, 0:1024] = (du * _gelu_grad(a)).astype(BF16)
        dproj_ref[:, 1024:2048] = (dvp * _gelu_grad(bb)).astype(BF16)
        dproj_ref[:, 2048:3072] = dz.astype(BF16)

    return pl.pallas_call(
        body, grid=(t // tr,),
        in_specs=[_row_spec(tr, 3072), _row_spec(tr, 1024), _full_spec((1, 1024)), _full_spec((1, 1024)),
                  _full_spec((16, 128, 128)), _full_spec((128, 16))],
        out_specs=[_row_spec(tr, 3072), _full_spec((16, 128, 128)), _full_spec((128, 16)), _full_spec((1, 1024)),
                   _full_spec((1, 1024))],
        out_shape=[S((t, 3072), BF16), S((16, 128, 128), F32), S((128, 16), F32), S((1, 1024), F32),
                   S((1, 1024), F32)],
        name="sgu_bwd", compiler_params=_cp(("arbitrary",)))(proj, dp, lg, lb, w_s, bst)


def _t5_bucket_table():
    qi = np.arange(WINDOW)[:, None]
    kj = np.arange(2 * WINDOW)[None, :]
    dist = np.maximum(qi + WINDOW - kj, 0)
    max_exact = REL_BUCKETS // 2
    dist_f = np.maximum(dist, 1).astype(np.float32)
    large = max_exact + (np.log(dist_f / np.float32(max_exact)) / np.float32(math.log(REL_MAX_DIST / max_exact))
                         * np.float32(REL_BUCKETS - max_exact)).astype(np.int32)
    large = np.minimum(large, REL_BUCKETS - 1)
    return np.where(dist < max_exact, dist, large).astype(np.int32)


def _swa_bias(rel_bias, bucket):
    def body(rb_ref, bk_ref, o_ref):
        bk = bk_ref[...]
        for h in range(SWA_HEADS):
            acc = jnp.zeros((WINDOW, 2 * WINDOW), F32)
            for b in range(REL_BUCKETS):
                acc = jnp.where(bk == b, rb_ref[b, h], acc)
            o_ref[h] = acc

    return pl.pallas_call(
        body, in_specs=[pl.BlockSpec(memory_space=pltpu.SMEM), pl.BlockSpec(memory_space=pltpu.VMEM)],
        out_specs=pl.BlockSpec(memory_space=pltpu.VMEM), out_shape=S((SWA_HEADS, WINDOW, 2 * WINDOW), F32),
        name="swa_bias")(rel_bias, bucket)


def _swa_dbias(ds_sum, dsink_rows, bucket):
    def body(ds_ref, dsk_ref, bk_ref, drb_ref, dsink_ref):
        bk = bk_ref[...]
        r = lax.broadcasted_iota(jnp.int32, (REL_BUCKETS, SWA_HEADS), 0)
        c = lax.broadcasted_iota(jnp.int32, (REL_BUCKETS, SWA_HEADS), 1)
        out = jnp.zeros((REL_BUCKETS, SWA_HEADS), F32)
        for b in range(REL_BUCKETS):
            m = bk == b
            for h in range(SWA_HEADS):
                val = jnp.sum(jnp.where(m, ds_ref[h], 0.0))
                out = jnp.where((r == b) & (c == h), val, out)
        drb_ref[...] = out
        dsink_ref[...] = jnp.sum(dsk_ref[...], axis=0, keepdims=True)

    return pl.pallas_call(
        body, out_shape=[S((REL_BUCKETS, SWA_HEADS), F32), S((1, SWA_HEADS), F32)], name="swa_dbias",
    )(ds_sum, dsink_rows, bucket)


def _swa_band(cur_ref, prev_ref):
    kband = jnp.concatenate([prev_ref[:, 0:128], cur_ref[:, 1024:1152]], axis=0).astype(BF16)
    vband = jnp.concatenate([prev_ref[:, 128:256], cur_ref[:, 1152:1280]], axis=0).astype(BF16)
    return kband, vband


def _swa_valid(n):
    qi = lax.broadcasted_iota(jnp.int32, (WINDOW, 2 * WINDOW), 0)
    kj = lax.broadcasted_iota(jnp.int32, (WINDOW, 2 * WINDOW), 1)
    dist = qi + WINDOW - kj
    return (dist >= 0) & (dist < WINDOW) & (n * WINDOW + kj - WINDOW >= 0)


def _swa_probs(qh, kh, bias_h, sink, valid):
    s = _dot_nt(qh, kh) * (HEAD_DIM ** -0.5) + bias_h
    s = jnp.where(valid, s, NEG_INF)
    m = jnp.maximum(jnp.max(s, axis=-1, keepdims=True), sink)
    e = jnp.exp(s - m)
    es = jnp.exp(sink - m)
    inv = 1.0 / (jnp.sum(e, axis=-1, keepdims=True) + es)
    return e * inv, es * inv


def _swa_in_specs():
    return [_row_spec(WINDOW, 2304),
            pl.BlockSpec((WINDOW, 256), lambda n: (jnp.maximum(n - 1, 0), 4)),
            _full_spec((SWA_HEADS, WINDOW, 2 * WINDOW)),
            pl.BlockSpec(memory_space=pltpu.SMEM)]


def _swa_fwd(proj, bias, sinks):
    t = proj.shape[0]

    def body(cur_ref, prev_ref, bias_ref, sink_ref, o_ref):
        n = pl.program_id(0)
        kband, vband = _swa_band(cur_ref, prev_ref)
        valid = _swa_valid(n)
        outs = []
        for h in range(SWA_HEADS):
            j = h // SWA_GROUP
            qh = cur_ref[:, h * 64:(h + 1) * 64].astype(BF16)
            p, _ = _swa_probs(qh, kband[:, j * 64:(j + 1) * 64], bias_ref[h], sink_ref[0, h], valid)
            outs.append(_dot(p.astype(BF16), vband[:, j * 64:(j + 1) * 64]))
        o = jnp.concatenate(outs, axis=1)
        o_ref[...] = (o * _silu(cur_ref[:, 1280:2304])).astype(BF16)

    return pl.pallas_call(body, grid=(t // WINDOW,), in_specs=_swa_in_specs(), out_specs=_row_spec(WINDOW, 1024),
                          out_shape=S((t, 1024), BF16), name="swa_fwd",
                          compiler_params=_cp(("parallel",)))(proj, proj, bias, sinks)


def _swa_bwd(proj, dp, bias, sinks):
    t = proj.shape[0]
    nb = t // WINDOW

    def body(cur_ref, prev_ref, bias_ref, sink_ref, dp_ref, dqz_ref, dkv_ref, dss_ref, dsk_ref):
        n = pl.program_id(0)

        @pl.when(n == 0)
        def _():
            dkv_ref[...] = jnp.zeros_like(dkv_ref)
            dss_ref[...] = jnp.zeros_like(dss_ref)
            dsk_ref[...] = jnp.zeros_like(dsk_ref)

        kband, vband = _swa_band(cur_ref, prev_ref)
        valid = _swa_valid(n)
        z = cur_ref[:, 1280:2304]
        dpv = dp_ref[...].astype(F32)
        do = dpv * _silu(z)
        outs, dqs, dsinks = [], [], []
        dk = [jnp.zeros((2 * WINDOW, 64), F32) for _ in range(2)]
        dv = [jnp.zeros((2 * WINDOW, 64), F32) for _ in range(2)]
        for h in range(SWA_HEADS):
            j = h // SWA_GROUP
            qh = cur_ref[:, h * 64:(h + 1) * 64].astype(BF16)
            kh = kband[:, j * 64:(j + 1) * 64]
            vh = vband[:, j * 64:(j + 1) * 64]
            p, ps = _swa_probs(qh, kh, bias_ref[h], sink_ref[0, h], valid)
            pb = p.astype(BF16)
            outs.append(_dot(pb, vh))
            doh = do[:, h * 64:(h + 1) * 64].astype(BF16)
            dpr = _dot_nt(doh, vh)
            dlt = jnp.sum(p * dpr, axis=-1, keepdims=True)
            ds = p * (dpr - dlt)
            dsinks.append(-ps * dlt)
            dss_ref[h] += ds
            dsb = (ds * (HEAD_DIM ** -0.5)).astype(BF16)
            dqs.append(_dot(dsb, kh))
            dk[j] = dk[j] + _dot_tn(dsb, qh)
            dv[j] = dv[j] + _dot_tn(pb, doh)
        o = jnp.concatenate(outs, axis=1)
        dqz_ref[:, 0:1024] = jnp.concatenate(dqs, axis=1).astype(BF16)
        dqz_ref[:, 1024:2048] = (dpv * o * _silu_grad(z)).astype(BF16)
        dsk_ref[...] += jnp.concatenate(dsinks, axis=1)
        dband = jnp.concatenate([dk[0], dk[1], dv[0], dv[1]], axis=1)
        prow = pl.multiple_of(jnp.maximum(n - 1, 0) * WINDOW, WINDOW)
        dkv_ref[pl.ds(prow, WINDOW), :] += dband[0:WINDOW]
        crow = pl.multiple_of(n * WINDOW, WINDOW)
        dkv_ref[pl.ds(crow, WINDOW), :] += dband[WINDOW:]

    return pl.pallas_call(
        body, grid=(nb,), in_specs=_swa_in_specs() + [_row_spec(WINDOW, 1024)],
        out_specs=[_row_spec(WINDOW, 2048), _full_spec((t, 256)), _full_spec((SWA_HEADS, WINDOW, 2 * WINDOW)),
                   _full_spec((WINDOW, SWA_HEADS))],
        out_shape=[S((t, 2048), BF16), S((t, 256), F32), S((SWA_HEADS, WINDOW, 2 * WINDOW), F32),
                   S((WINDOW, SWA_HEADS), F32)],
        name="swa_bwd", compiler_params=_cp(("arbitrary",)))(proj, proj, bias, sinks, dp)


MLA_Z0 = MLA_Q_RANK + MLA_KV_RANK + 128
MLA_Z1 = MLA_Z0 + 1024
MLA_PROJ = MLA_Z1 + 128


def _rope_tables(t):
    inv = ROPE_BASE ** (-jnp.arange(0, MLA_ROPE, 2, dtype=F32) / MLA_ROPE)
    ang = jnp.arange(t, dtype=F32)[:, None] * inv[None, :]
    return jnp.tile(jnp.cos(ang), (1, MLA_HEADS)), jnp.tile(jnp.sin(ang), (1, MLA_HEADS))


def _mla_norms(proj, gq, gkv):
    t = proj.shape[0]
    tr = 256

    def body(p_ref, gq_ref, gkv_ref, q_ref, kv_ref):
        for lo, hi, g_ref, o_ref in ((0, 768, gq_ref, q_ref), (768, 1024, gkv_ref, kv_ref)):
            xv = p_ref[:, lo:hi]
            r = lax.rsqrt(jnp.mean(xv * xv, axis=-1, keepdims=True) + EPS)
            o_ref[...] = (xv * r * g_ref[...]).astype(BF16)

    return pl.pallas_call(
        body, grid=(t // tr,), in_specs=[_row_spec(tr, 1024), _full_spec((1, 768)), _full_spec((1, 256))],
        out_specs=[_row_spec(tr, 768), _row_spec(tr, 256)], out_shape=[S((t, 768), BF16), S((t, 256), BF16)],
        name="mla_norms", compiler_params=_cp(("parallel",)))(proj, gq, gkv)


def _mla_norms_bwd(proj, dcqn, dckvn, dkr, dz, gq, gkv):
    t = proj.shape[0]
    tr = 256

    def body(p_ref, dq_ref, dkv_ref, dkr_ref, dz_ref, gq_ref, gkv_ref, dproj_ref, dgq_ref, dgkv_ref):
        @pl.when(pl.program_id(0) == 0)
        def _():
            dgq_ref[...] = jnp.zeros_like(dgq_ref)
            dgkv_ref[...] = jnp.zeros_like(dgkv_ref)

        dx, dg = _rms_bwd_rows(dq_ref[...], p_ref[:, 0:768], gq_ref[...])
        dproj_ref[:, 0:768] = dx.astype(BF16)
        dgq_ref[...] += dg
        dx, dg = _rms_bwd_rows(dkv_ref[...], p_ref[:, 768:1024], gkv_ref[...])
        dproj_ref[:, 768:1024] = dx.astype(BF16)
        dgkv_ref[...] += dg
        dproj_ref[:, 1024:MLA_Z0] = dkr_ref[...].astype(BF16)
        dproj_ref[:, MLA_Z0:MLA_Z1] = dz_ref[...]
        dproj_ref[:, MLA_Z1:MLA_PROJ] = jnp.zeros((tr, MLA_PROJ - MLA_Z1), BF16)

    return pl.pallas_call(
        body, grid=(t // tr,),
        in_specs=[_row_spec(tr, 1024), _row_spec(tr, 768), _row_spec(tr, 256), _row_spec(tr, 128), _row_spec(tr, 1024),
                  _full_spec((1, 768)), _full_spec((1, 256))],
        out_specs=[_row_spec(tr, MLA_PROJ), _full_spec((1, 768)), _full_spec((1, 256))],
        out_shape=[S((t, MLA_PROJ), BF16), S((1, 768), F32), S((1, 256), F32)],
        name="mla_norms_bwd", compiler_params=_cp(("arbitrary",)))(proj, dcqn, dckvn, dkr, dz, gq, gkv)


def _mla_pack(qf, kvf, proj, cos, sin):
    t = qf.shape[0]
    tr = 256

    def body(q_ref, kn_ref, kr_ref, cos_ref, sin_ref, qh_ref, kh_ref):
        cs, sn = cos_ref[...], sin_ref[...]
        x1, x2 = q_ref[:, 1024:1280], q_ref[:, 1280:1536]
        r1 = x1 * cs - x2 * sn
        r2 = x2 * cs + x1 * sn
        c16, s16 = cs[:, 0:16], sn[:, 0:16]
        k1, k2 = kr_ref[:, 0:16], kr_ref[:, 16:32]
        kr1 = k1 * c16 - k2 * s16
        kr2 = k2 * c16 + k1 * s16
        zpad = jnp.zeros((tr, 32), F32)
        for h in range(MLA_HEADS):
            qh_ref[h] = jnp.concatenate([q_ref[:, h * 64:(h + 1) * 64], r1[:, h * 16:(h + 1) * 16],
                                         r2[:, h * 16:(h + 1) * 16], zpad], axis=1).astype(BF16)
            kh_ref[h] = jnp.concatenate([kn_ref[:, h * 64:(h + 1) * 64], kr1, kr2, zpad], axis=1).astype(BF16)

    hspec = pl.BlockSpec((MLA_HEADS, tr, 128), lambda i: (0, i, 0))
    return pl.pallas_call(
        body, grid=(t // tr,),
        in_specs=[_row_spec(tr, 1536), _row_spec(tr, 1024), pl.BlockSpec((tr, 128), lambda i: (i, 8)),
                  _row_spec(tr, 256), _row_spec(tr, 256)],
        out_specs=[hspec, hspec], out_shape=[S((MLA_HEADS, t, 128), BF16)] * 2,
        name="mla_pack", compiler_params=_cp(("parallel",)))(qf, kvf, proj, cos, sin)


def _mla_unpack(dqh, dkh, dv, cos, sin):
    t = dqh.shape[1]
    tr = 256

    def body(dqh_ref, dkh_ref, dv_ref, cos_ref, sin_ref, dq_ref, dkv_ref, dkr_ref):
        cs, sn = cos_ref[...], sin_ref[...]
        dqn = jnp.concatenate([dqh_ref[h, :, 0:64] for h in range(MLA_HEADS)], axis=1)
        dr1 = jnp.concatenate([dqh_ref[h, :, 64:80] for h in range(MLA_HEADS)], axis=1)
        dr2 = jnp.concatenate([dqh_ref[h, :, 80:96] for h in range(MLA_HEADS)], axis=1)
        dq_ref[:, 0:1024] = dqn.astype(BF16)
        dq_ref[:, 1024:1280] = (dr1 * cs + dr2 * sn).astype(BF16)
        dq_ref[:, 1280:1536] = (dr2 * cs - dr1 * sn).astype(BF16)
        dkv_ref[:, 0:1024] = jnp.concatenate([dkh_ref[h, :, 0:64] for h in range(MLA_HEADS)], axis=1).astype(BF16)
        dkv_ref[:, 1024:2048] = dv_ref[...].astype(BF16)
        d1 = dkh_ref[0, :, 64:80]
        d2 = dkh_ref[0, :, 80:96]
        for h in range(1, MLA_HEADS):
            d1 = d1 + dkh_ref[h, :, 64:80]
            d2 = d2 + dkh_ref[h, :, 80:96]
        c16, s16 = cs[:, 0:16], sn[:, 0:16]
        dkr_ref[...] = jnp.concatenate([d1 * c16 + d2 * s16, d2 * c16 - d1 * s16, jnp.zeros((tr, 96), F32)], axis=1)

    hspec = pl.BlockSpec((MLA_HEADS, tr, 128), lambda i: (0, i, 0))
    return pl.pallas_call(
        body, grid=(t // tr,), in_specs=[hspec, hspec, _row_spec(tr, 1024), _row_spec(tr, 256), _row_spec(tr, 256)],
        out_specs=[_row_spec(tr, 1536), _row_spec(tr, 2048), _row_spec(tr, 128)],
        out_shape=[S((t, 1536), BF16), S((t, 2048), BF16), S((t, 128), F32)],
        name="mla_unpack", compiler_params=_cp(("parallel",)))(dqh, dkh, dv, cos, sin)


def _mla_attn_fwd(qh, kh, kvf, proj):
    t = qh.shape[1]
    b, bk = MLA_BQ, MLA_BK

    def body(q_ref, k_ref, v_ref, z_ref, o_ref, p_ref, lse_ref):
        r0 = pl.program_id(1) * b
        nfull = r0 // bk
        diff = (lax.broadcasted_iota(jnp.int32, (b, bk), 1) - lax.broadcasted_iota(jnp.int32, (b, bk), 0))
        qs = [q_ref[0], q_ref[1]]

        def block(c0, carry, masked):
            new = []
            for a in range(2):
                m, l, acc = carry[a]
                k = k_ref[a, pl.ds(c0, bk), :]
                v = v_ref[pl.ds(c0, bk), a * 64:(a + 1) * 64].astype(BF16)
                s = _dot_nt(qs[a], k) * MLA_SCALE
                if masked:
                    s = jnp.where(diff <= r0 - c0, s, NEG_INF)
                mn = jnp.maximum(m, jnp.max(s, axis=-1, keepdims=True))
                alpha = jnp.exp(m - mn)
                p = jnp.exp(s - mn)
                new.append((mn, alpha * l + jnp.sum(p, axis=-1, keepdims=True), alpha * acc + _dot(p.astype(BF16), v)))
            return tuple(new)

        init = (jnp.full((b, 1), NEG_INF, F32), jnp.zeros((b, 1), F32), jnp.zeros((b, 64), F32))
        carry = lax.fori_loop(0, nfull, lambda c, cr: block(pl.multiple_of(c * bk, bk), cr, False), (init, init))
        for i in range(max(1, b // bk)):
            carry = block(pl.multiple_of((nfull + i) * bk, bk), carry, True)
        outs = [acc / l for _, l, acc in carry]
        lses = [jnp.broadcast_to(m + jnp.log(l), (b, 64)) for m, l, _ in carry]
        o = jnp.concatenate(outs, axis=1)
        o_ref[...] = o
        p_ref[...] = (o * _silu(z_ref[...])).astype(BF16)
        lse_ref[0] = jnp.concatenate(lses, axis=1)

    return pl.pallas_call(
        body, grid=(MLA_HEADS // 2, t // b),
        in_specs=[pl.BlockSpec((2, b, 128), lambda i, r: (i, r, 0)), pl.BlockSpec((2, t, 128), lambda i, r: (i, 0, 0)),
                  pl.BlockSpec((t, 128), lambda i, r: (0, 8 + i)), pl.BlockSpec((b, 128), lambda i, r: (r, 9 + i))],
        out_specs=[pl.BlockSpec((b, 128), lambda i, r: (r, i)), pl.BlockSpec((b, 128), lambda i, r: (r, i)),
                   pl.BlockSpec((1, b, 128), lambda i, r: (i, r, 0))],
        out_shape=[S((t, 1024), F32), S((t, 1024), BF16), S((MLA_HEADS // 2, t, 128), F32)],
        name="mla_attn_fwd", compiler_params=_cp(("parallel", "parallel")))(qh, kh, kvf, proj)


def _mla_attn_bwd(qh, kh, kvf, proj, o, dpout, lse):
    t = qh.shape[1]
    b, bk = MLA_BQ, MLA_BK
    nb = t // b

    def body(q_ref, k_ref, v_ref, z_ref, o_ref, dp_ref, lse_ref, dq_ref, dk_ref, dv_ref, dz_ref):
        dk_ref[...] = jnp.zeros_like(dk_ref)
        dv_ref[...] = jnp.zeros_like(dv_ref)
        diff = (lax.broadcasted_iota(jnp.int32, (b, bk), 1) - lax.broadcasted_iota(jnp.int32, (b, bk), 0))
        lanes = [slice(0, 64), slice(64, 128)]

        def rbody(r, carry):
            r0 = pl.multiple_of(r * b, b)
            rows = pl.ds(r0, b)
            qs, dobs, deltas, lses = [], [], [], []
            for a in range(2):
                zz = z_ref[rows, lanes[a]]
                oo = o_ref[rows, lanes[a]]
                dpo = dp_ref[rows, lanes[a]].astype(F32)
                do = dpo * _silu(zz)
                dz_ref[rows, lanes[a]] = (dpo * oo * _silu_grad(zz)).astype(BF16)
                qs.append(q_ref[a, rows, :])
                dobs.append(do.astype(BF16))
                deltas.append(jnp.sum(do * oo, axis=-1, keepdims=True))
                lses.append(lse_ref[0, rows, a * 64:a * 64 + 1])

            def block(c0, dqs, masked):
                cols = pl.ds(c0, bk)
                out = []
                for a in range(2):
                    k = k_ref[a, cols, :]
                    v = v_ref[cols, lanes[a]].astype(BF16)
                    s = _dot_nt(qs[a], k) * MLA_SCALE
                    if masked:
                        s = jnp.where(diff <= r0 - c0, s, NEG_INF)
                    p = jnp.exp(s - lses[a])
                    ds = p * (_dot_nt(dobs[a], v) - deltas[a]) * MLA_SCALE
                    dsb = ds.astype(BF16)
                    dk_ref[a, cols, :] += _dot_tn(dsb, qs[a])
                    dv_ref[cols, lanes[a]] += _dot_tn(p.astype(BF16), dobs[a])
                    out.append(dqs[a] + _dot(dsb, k))
                return tuple(out)

            zero = jnp.zeros((b, 128), F32)
            nfull = r0 // bk
            dqs = lax.fori_loop(0, nfull, lambda c, d: block(pl.multiple_of(c * bk, bk), d, False), (zero, zero))
            for i in range(max(1, b // bk)):
                dqs = block(pl.multiple_of((nfull + i) * bk, bk), dqs, True)
            dq_ref[0, rows, :] = dqs[0]
            dq_ref[1, rows, :] = dqs[1]
            return carry

        lax.fori_loop(0, nb, rbody, 0)

    pair3 = pl.BlockSpec((2, t, 128), lambda i: (i, 0, 0))
    return pl.pallas_call(
        body, grid=(MLA_HEADS // 2,),
        in_specs=[pair3, pair3, pl.BlockSpec((t, 128), lambda i: (0, 8 + i)), pl.BlockSpec((t, 128), lambda i: (0, 9 + i)),
                  pl.BlockSpec((t, 128), lambda i: (0, i)), pl.BlockSpec((t, 128), lambda i: (0, i)),
                  pl.BlockSpec((1, t, 128), lambda i: (i, 0, 0))],
        out_specs=[pair3, pair3, pl.BlockSpec((t, 128), lambda i: (0, i)), pl.BlockSpec((t, 128), lambda i: (0, i))],
        out_shape=[S((MLA_HEADS, t, 128), F32), S((MLA_HEADS, t, 128), F32), S((t, 1024), F32), S((t, 1024), BF16)],
        name="mla_attn_bwd", compiler_params=_cp(("parallel",), VMEM_BIG))(qh, kh, kvf, proj, o, dpout, lse)


def _dot_nt_hi(a, b):
    return lax.dot_general(a, b, (((1,), (1,)), ((), ())), preferred_element_type=F32,
                           precision=lax.Precision.HIGHEST)


def _ssm_gen(lr, li, ldt, btr, bti, cr, ci):
    n = SSM_BLOCK
    dt = jnp.exp(ldt)
    x = lr * dt
    w = li * dt
    mag = jnp.exp(x)
    ab_re = mag * jnp.cos(w)
    ab_im = mag * jnp.sin(w)
    den = lr * lr + li * li
    nr = ab_re - 1.0
    f_re = (nr * lr + ab_im * li) / den
    f_im = (ab_im * lr - nr * li) / den
    bb_re = f_re * btr - f_im * bti
    bb_im = f_re * bti + f_im * btr
    inv_mag = jnp.exp(-x)
    inv_re, inv_im = inv_mag * jnp.cos(w), -inv_mag * jnp.sin(w)
    one, zero = jnp.ones_like(x), jnp.zeros_like(x)
    pos, neg = [(one, zero)], [(one, zero)]
    for _ in range(SSM_CHUNK):
        pr, pi = pos[-1]
        pos.append((pr * ab_re - pi * ab_im, pr * ab_im + pi * ab_re))
        pr, pi = neg[-1]
        neg.append((pr * inv_re - pi * inv_im, pr * inv_im + pi * inv_re))

    def stack(table, idx):
        return (jnp.concatenate([jnp.broadcast_to(table[idx(t)][0], (SSM_GROUP, SSM_STATE)) for t in range(SSM_CHUNK)], axis=0),
                jnp.concatenate([jnp.broadcast_to(table[idx(t)][1], (SSM_GROUP, SSM_STATE)) for t in range(SSM_CHUNK)], axis=0))

    def rep(v):
        return jnp.concatenate([v] * SSM_CHUNK, axis=0)

    ct_re, ct_im, bt_re, bt_im = rep(cr), rep(ci), rep(bb_re), rep(bb_im)
    p_re, p_im = stack(pos, lambda t: t)
    l_re = ct_re * p_re - ct_im * p_im
    l_im = ct_re * p_im + ct_im * p_re
    n_re, n_im = stack(neg, lambda t: t)
    r_re = bt_re * n_re - bt_im * n_im
    r_im = bt_re * n_im + bt_im * n_re
    rk = lax.broadcasted_iota(jnp.int32, (n, n), 0) // SSM_GROUP
    ct = lax.broadcasted_iota(jnp.int32, (n, n), 1) // SSM_GROUP
    mt = jnp.where(ct >= rk, _dot_nt_hi(r_re, l_re) - _dot_nt_hi(r_im, l_im), 0.0)
    e_re, e_im = stack(pos, lambda t: SSM_CHUNK - 1 - t)
    pin_re = bt_re * e_re - bt_im * e_im
    pin_im = bt_re * e_im + bt_im * e_re
    q_re, q_im = stack(pos, lambda t: t + 1)
    qt_re = ct_re * q_re - ct_im * q_im
    qt_im = -(ct_re * q_im + ct_im * q_re)
    ad_re, ad_im = pos[SSM_CHUNK]
    return mt, pin_re, pin_im, qt_re, qt_im, ad_re, ad_im


_SSM_GEN_OUT = [(SSM_BLOCK, SSM_BLOCK)] + [(SSM_BLOCK, SSM_STATE)] * 4 + [(1, SSM_STATE)] * 2
_SSM_GEN_IN = [(1, SSM_STATE)] * 3 + [(SSM_GROUP, SSM_STATE)] * 4


def _gspec(shape2):
    return pl.BlockSpec((1,) + shape2, lambda g: (g, 0, 0))


def _ssm_operators(params):
    def body(*refs):
        ins, outs = refs[:7], refs[7:]
        for o_ref, val in zip(outs, _ssm_gen(*[r[0] for r in ins])):
            o_ref[0] = val

    return pl.pallas_call(
        body, grid=(SSM_GROUPS,), in_specs=[_gspec(s) for s in _SSM_GEN_IN], out_specs=[_gspec(s) for s in _SSM_GEN_OUT],
        out_shape=[S((SSM_GROUPS,) + s, F32) for s in _SSM_GEN_OUT], name="ssm_operators",
        compiler_params=_cp(("parallel",)))(*params)


def _ssm_operators_bwd(params, cots, after=()):
    def body(*refs):
        ins, cts, outs = refs[:7], refs[7:14], refs[14 + len(after):]
        _, vjp = jax.vjp(_ssm_gen, *[r[0] for r in ins])
        grads = list(vjp(tuple(r[0] for r in cts)))
        grads[2] = jnp.broadcast_to(jnp.sum(grads[2], axis=-1, keepdims=True), (1, SSM_STATE))
        for o_ref, val in zip(outs, grads):
            o_ref[0] = val

    return pl.pallas_call(
        body, grid=(SSM_GROUPS,),
        in_specs=[_gspec(s) for s in _SSM_GEN_IN] + [_gspec(s) for s in _SSM_GEN_OUT]
        + [pl.BlockSpec(memory_space=pl.ANY)] * len(after),
        out_specs=[_gspec(s) for s in _SSM_GEN_IN], out_shape=[S((SSM_GROUPS,) + s, F32) for s in _SSM_GEN_IN],
        name="ssm_operators_bwd", compiler_params=_cp(("parallel",)))(*params, *cots, *after)


def _shift_rows(x, sh, row, up):
    n = x.shape[0]
    if up:
        return jnp.where(row < n - sh, pltpu.roll(x, n - sh, axis=0), 0.0)
    return jnp.where(row >= sh, pltpu.roll(x, sh, axis=0), 0.0)


def _carry_scan(sr, si, ar, ai, row, up):
    n = sr.shape[0]
    sh = 1
    while sh < n:
        tr, ti = _shift_rows(sr, sh, row, up), _shift_rows(si, sh, row, up)
        sr, si = sr + ar * tr - ai * ti, si + ar * ti + ai * tr
        ar, ai = ar * ar - ai * ai, 2.0 * ar * ai
        sh *= 2
    return sr, si


def _ssm_states(u, p_re, p_im, ar, ai, row):
    lre = _dot(u, p_re.astype(BF16))
    lim = _dot(u, p_im.astype(BF16))
    sr, si = _carry_scan(lre, lim, ar, ai, row, False)
    return _shift_rows(sr, 1, row, False), _shift_rows(si, 1, row, False)


SSM_GPB = 128 // SSM_GROUP


def _step_rows(ref, nc):
    return [ref[pl.ds(k, nc, stride=SSM_CHUNK), :].astype(BF16) for k in range(SSM_CHUNK)]


def _group_of(rows, j):
    return jnp.concatenate([r[:, j * SSM_GROUP:(j + 1) * SSM_GROUP] for r in rows], axis=1)


def _store_groups(o_ref, per_group, nc):
    for k in range(SSM_CHUNK):
        o_ref[pl.ds(k, nc, stride=SSM_CHUNK), :] = jnp.concatenate(
            [y[:, k * SSM_GROUP:(k + 1) * SSM_GROUP] for y in per_group], axis=1)


def _bspec(shape2):
    return pl.BlockSpec((SSM_GPB,) + shape2, lambda j: (j, 0, 0))


def _ssm_core_fwd(proj, ops):
    t = proj.shape[0]
    nc = t // SSM_CHUNK

    def body(u_ref, mt_ref, pr_ref, pi_ref, qr_ref, qi_ref, ar_ref, ai_ref, y_ref):
        rows = _step_rows(u_ref, nc)
        row = lax.broadcasted_iota(jnp.int32, (nc, SSM_STATE), 0)
        ys = []
        for j in range(SSM_GPB):
            u = _group_of(rows, j)
            s_re, s_im = _ssm_states(u, pr_ref[j], pi_ref[j], ar_ref[j], ai_ref[j], row)
            ys.append(_dot(u, mt_ref[j].astype(BF16)) + _dot_nt(s_re.astype(BF16), qr_ref[j].astype(BF16))
                      + _dot_nt(s_im.astype(BF16), qi_ref[j].astype(BF16)))
        _store_groups(y_ref, ys, nc)

    blk = pl.BlockSpec((t, 128), lambda j: (0, j))
    return pl.pallas_call(
        body, grid=(SSM_GROUPS // SSM_GPB,), in_specs=[blk] + [_bspec(s) for s in _SSM_GEN_OUT], out_specs=blk,
        out_shape=S((t, 1024), F32), name="ssm_core_fwd", compiler_params=_cp(("parallel",)))(proj, *ops)


def _ssm_core_bwd(proj, dy, ops):
    t = proj.shape[0]
    nc = t // SSM_CHUNK

    def body(u_ref, dy_ref, mt_ref, pr_ref, pi_ref, qr_ref, qi_ref, ar_ref, ai_ref,
             du_ref, dmt_ref, dpr_ref, dpi_ref, dqr_ref, dqi_ref, dar_ref, dai_ref):
        u_rows = _step_rows(u_ref, nc)
        dy_rows = _step_rows(dy_ref, nc)
        row = lax.broadcasted_iota(jnp.int32, (nc, SSM_STATE), 0)
        dus = []
        for j in range(SSM_GPB):
            u = _group_of(u_rows, j)
            dyj = _group_of(dy_rows, j)
            ar, ai = ar_ref[j], ai_ref[j]
            s_re, s_im = _ssm_states(u, pr_ref[j], pi_ref[j], ar, ai, row)
            g_re = _shift_rows(_dot(dyj, qr_ref[j].astype(BF16)), 1, row, True)
            g_im = _shift_rows(_dot(dyj, qi_ref[j].astype(BF16)), 1, row, True)
            l_re, l_im = _carry_scan(g_re, g_im, ar, -ai, row, True)
            lrb, lib = l_re.astype(BF16), l_im.astype(BF16)
            dus.append(_dot_nt(dyj, mt_ref[j].astype(BF16)) + _dot_nt(lrb, pr_ref[j].astype(BF16))
                       + _dot_nt(lib, pi_ref[j].astype(BF16)))
            dmt_ref[j] = _dot_tn(u, dyj)
            dpr_ref[j] = _dot_tn(u, lrb)
            dpi_ref[j] = _dot_tn(u, lib)
            dqr_ref[j] = _dot_tn(dyj, s_re.astype(BF16))
            dqi_ref[j] = _dot_tn(dyj, s_im.astype(BF16))
            dar_ref[j] = jnp.sum(l_re * s_re + l_im * s_im, axis=0, keepdims=True)
            dai_ref[j] = jnp.sum(l_im * s_re - l_re * s_im, axis=0, keepdims=True)
        _store_groups(du_ref, dus, nc)

    blk = pl.BlockSpec((t, 128), lambda j: (0, j))
    return pl.pallas_call(
        body, grid=(SSM_GROUPS // SSM_GPB,), in_specs=[blk, blk] + [_bspec(s) for s in _SSM_GEN_OUT],
        out_specs=[blk] + [_bspec(s) for s in _SSM_GEN_OUT],
        out_shape=[S((t, 1024), F32)] + [S((SSM_GROUPS,) + s, F32) for s in _SSM_GEN_OUT],
        name="ssm_core_bwd", compiler_params=_cp(("parallel",)))(proj, dy, *ops)


def _s5_act(y_core, proj, d_skip):
    t = y_core.shape[0]
    tr = 256

    def body(y_ref, u_ref, d_ref, o_ref):
        o_ref[...] = _gelu(y_ref[...] + d_ref[...] * u_ref[...])

    return pl.pallas_call(body, grid=(t // tr,), in_specs=[_row_spec(tr, 1024), _row_spec(tr, 1024), _full_spec((1, 1024))],
                          out_specs=_row_spec(tr, 1024), out_shape=S((t, 1024), F32), name="s5_act",
                          compiler_params=_cp(("parallel",)))(y_core, proj, d_skip)


def _s5_gate(yg, tg, b_glu, proj):
    t = yg.shape[0]
    tr = 256

    def body(y_ref, t_ref, b_ref, z_ref, o_ref):
        o_ref[...] = (y_ref[...] * jax.nn.sigmoid(t_ref[...] + b_ref[...]) * _silu(z_ref[...])).astype(BF16)

    return pl.pallas_call(
        body, grid=(t // tr,),
        in_specs=[_row_spec(tr, 1024), _row_spec(tr, 1024), _full_spec((1, 1024)), pl.BlockSpec((tr, 1024), lambda i: (i, 1))],
        out_specs=_row_spec(tr, 1024), out_shape=S((t, 1024), BF16), name="s5_gate",
        compiler_params=_cp(("parallel",)))(yg, tg, b_glu, proj)


def _s5_gate_bwd(dp, yg, tg, b_glu, proj):
    t = yg.shape[0]
    tr = 256

    def body(dp_ref, y_ref, t_ref, b_ref, z_ref, dt_ref, dy_ref, dz_ref, db_ref):
        @pl.when(pl.program_id(0) == 0)
        def _():
            db_ref[...] = jnp.zeros_like(db_ref)

        z = z_ref[...]
        yv = y_ref[...]
        sg = jax.nn.sigmoid(t_ref[...] + b_ref[...])
        dpv = dp_ref[...].astype(F32)
        do = dpv * _silu(z)
        dz_ref[...] = (dpv * yv * sg * _silu_grad(z)).astype(BF16)
        dy_ref[...] = do * sg
        dtv = do * yv * sg * (1.0 - sg)
        dt_ref[...] = dtv.astype(BF16)
        db_ref[...] += jnp.sum(dtv, axis=0, keepdims=True)

    return pl.pallas_call(
        body, grid=(t // tr,),
        in_specs=[_row_spec(tr, 1024), _row_spec(tr, 1024), _row_spec(tr, 1024), _full_spec((1, 1024)),
                  pl.BlockSpec((tr, 1024), lambda i: (i, 1))],
        out_specs=[_row_spec(tr, 1024), _row_spec(tr, 1024), _row_spec(tr, 1024), _full_spec((1, 1024))],
        out_shape=[S((t, 1024), BF16), S((t, 1024), F32), S((t, 1024), BF16), S((1, 1024), F32)],
        name="s5_gate_bwd", compiler_params=_cp(("arbitrary",)))(dp, yg, tg, b_glu, proj)


def _s5_act_bwd(dyg1, dyg2, y_core, proj, d_skip):
    t = y_core.shape[0]
    tr = 256

    def body(a_ref, b_ref, y_ref, u_ref, d_ref, dy_ref, du_ref, dd_ref):
        @pl.when(pl.program_id(0) == 0)
        def _():
            dd_ref[...] = jnp.zeros_like(dd_ref)

        u = u_ref[...]
        dv = (a_ref[...] + b_ref[...]) * _gelu_grad(y_ref[...] + d_ref[...] * u)
        dy_ref[...] = dv
        du_ref[...] = dv * d_ref[...]
        dd_ref[...] += jnp.sum(dv * u, axis=0, keepdims=True)

    return pl.pallas_call(
        body, grid=(t // tr,),
        in_specs=[_row_spec(tr, 1024)] * 4 + [_full_spec((1, 1024))],
        out_specs=[_row_spec(tr, 1024), _row_spec(tr, 1024), _full_spec((1, 1024))],
        out_shape=[S((t, 1024), F32), S((t, 1024), F32), S((1, 1024), F32)],
        name="s5_act_bwd", compiler_params=_cp(("arbitrary",)))(dyg1, dyg2, y_core, proj, d_skip)


def _s5_dproj(du_skip, du_core, dz):
    t = du_skip.shape[0]
    tr = 256

    def body(a_ref, b_ref, z_ref, o_ref):
        o_ref[:, 0:1024] = (a_ref[...] + b_ref[...]).astype(BF16)
        o_ref[:, 1024:2048] = z_ref[...]

    return pl.pallas_call(body, grid=(t // tr,), in_specs=[_row_spec(tr, 1024)] * 3, out_specs=_row_spec(tr, 2048),
                          out_shape=S((t, 2048), BF16), name="s5_dproj",
                          compiler_params=_cp(("parallel",)))(du_skip, du_core, dz)


def _s5_params(lam_re, lam_im, log_dt, b_re, b_im, c_re, c_im):
    g = SSM_GROUPS
    return (lam_re.reshape(g, 1, SSM_STATE), lam_im.reshape(g, 1, SSM_STATE),
            jnp.broadcast_to(log_dt.reshape(g, 1, 1), (g, 1, SSM_STATE)),
            b_re.transpose(0, 2, 1), b_im.transpose(0, 2, 1), c_re, c_im)


def _s5_fwd(proj, params, d_skip, w_glu, b_glu):
    ops = _ssm_operators(params)
    y_core = _ssm_core_fwd(proj, ops)
    yg = _s5_act(y_core, proj, d_skip)
    tg = _mm(yg, w_glu, name="s5_glu")
    return _s5_gate(yg, tg, b_glu, proj), (ops, y_core, yg, tg)


def _s5_bwd(proj, res, dp, params, d_skip, w_glu, b_glu):
    ops, y_core, yg, tg = res
    dtg, dyg1, dz, db_glu = _s5_gate_bwd(dp, yg, tg, b_glu, proj)
    dyg2 = _mm(dtg, w_glu, tb=True, name="s5_glu_dx")
    dw_glu = _mm(yg, dtg, ta=True, out_dtype=BF16, name="s5_glu_dw")
    dy_core, du_skip, dd = _s5_act_bwd(dyg1, dyg2, y_core, proj, d_skip)
    outs = _ssm_core_bwd(proj, dy_core, ops)
    dproj = _s5_dproj(du_skip, outs[0], dz)
    return dproj, outs[1:], dd, dw_glu, db_glu


def _s5_param_grads(params, cots, after=()):
    glr, gli, gdt, gbtr, gbti, gcr, gci = _ssm_operators_bwd(params, cots, after)
    g = SSM_GROUPS
    return (glr.reshape(g, SSM_STATE), gli.reshape(g, SSM_STATE), gdt[:, 0, 0].reshape(1, g),
            gbtr.transpose(0, 2, 1), gbti.transpose(0, 2, 1), gcr, gci)


def _perm_uq(w):
    w3 = w.reshape(w.shape[0], MLA_HEADS, MLA_NOPE + MLA_ROPE)
    return jnp.concatenate([w3[:, :, :64].reshape(-1, 1024), w3[:, :, 64:80].reshape(-1, 256),
                            w3[:, :, 80:96].reshape(-1, 256)], axis=1)


def _unperm_uq(w):
    r = w.shape[0]
    return jnp.concatenate([w[:, :1024].reshape(r, 16, 64), w[:, 1024:1280].reshape(r, 16, 16),
                            w[:, 1280:1536].reshape(r, 16, 16)], axis=2).reshape(r, 1536)


def _perm_ukv(w):
    w3 = w.reshape(w.shape[0], MLA_HEADS, 128)
    return jnp.concatenate([w3[:, :, :64].reshape(-1, 1024), w3[:, :, 64:].reshape(-1, 1024)], axis=1)


def _unperm_ukv(w):
    r = w.shape[0]
    return jnp.concatenate([w[:, :1024].reshape(r, 16, 64), w[:, 1024:].reshape(r, 16, 64)], axis=2).reshape(r, 2048)


def _mla_fwd(proj, gq, gkv, wuq, wukv, cos, sin):
    cqn, ckvn = _mla_norms(proj, gq, gkv)
    qf = _mm(cqn, wuq, name="mla_uq")
    kvf = _mm(ckvn, wukv, name="mla_ukv")
    qh, kh = _mla_pack(qf, kvf, proj, cos, sin)
    o, pout, lse = _mla_attn_fwd(qh, kh, kvf, proj)
    return pout, (cqn, ckvn, kvf, qh, kh, o, lse)


def _mla_bwd(proj, res, dpout, gq, gkv, wuq, wukv, cos, sin):
    cqn, ckvn, kvf, qh, kh, o, lse = res
    dqh, dkh, dv, dz = _mla_attn_bwd(qh, kh, kvf, proj, o, dpout, lse)
    dqf, dkvf, dkr = _mla_unpack(dqh, dkh, dv, cos, sin)
    dcqn = _mm(dqf, wuq, tb=True, name="mla_uq_dx")
    dwuq = _mm(cqn, dqf, ta=True, out_dtype=BF16, name="mla_uq_dw")
    dckvn = _mm(dkvf, wukv, tb=True, name="mla_ukv_dx")
    dwukv = _mm(ckvn, dkvf, ta=True, out_dtype=BF16, name="mla_ukv_dw")
    dproj, dgq, dgkv = _mla_norms_bwd(proj, dcqn, dckvn, dkr, dz, gq, gkv)
    return dproj, dgq, dgkv, dwuq, dwukv


_ANY = pl.BlockSpec(memory_space=pl.ANY)


def _chip_peers():
    x, y, c = lax.axis_index("x"), lax.axis_index("y"), lax.axis_index("c")
    return 2 * x + y, c, [(1 - x, y), (x, 1 - y), (1 - x, 1 - y)]


def _exchange_chips(src_of, dst_of, send_sems, recv_sems, loc_sem):
    me, c, peers = _chip_peers()
    loc = pltpu.make_async_copy(src_of(me), dst_of(me), loc_sem)
    loc.start()

    def copy(k, px, py, src_chip, dst_chip):
        return pltpu.make_async_remote_copy(src_ref=src_of(src_chip), dst_ref=dst_of(dst_chip), send_sem=send_sems.at[k],
                                            recv_sem=recv_sems.at[k], device_id=(px, py, c), device_id_type=MESH)

    sends = [copy(k, px, py, 2 * px + py, me) for k, (px, py) in enumerate(peers)]
    for cp in sends:
        cp.start()
    for k, (px, py) in enumerate(peers):
        copy(k, px, py, me, 2 * px + py).wait_recv()
    for cp in sends:
        cp.wait_send()
    loc.wait()


_EXCH_SCRATCH = [pltpu.SemaphoreType.DMA((3,)), pltpu.SemaphoreType.DMA((3,)), pltpu.SemaphoreType.DMA(())]


def _all_gather_chips(xs, name):
    def body(x_ref, o_ref, send_sems, recv_sems, loc_sem):
        _exchange_chips(lambda s: x_ref, lambda s: o_ref.at[s], send_sems, recv_sems, loc_sem)

    return pl.pallas_call(body, in_specs=[_ANY], out_specs=_ANY, out_shape=S((N_CHIPS,) + xs.shape, xs.dtype),
                          scratch_shapes=_EXCH_SCRATCH, name=name)(xs)


def _sibling_swap(ps, name):
    n = len(ps)

    def body(*refs):
        p_refs, q_refs, send_sems, recv_sems = refs[:n], refs[n:2 * n], refs[2 * n], refs[2 * n + 1]
        x, y, c = lax.axis_index("x"), lax.axis_index("y"), lax.axis_index("c")
        cps = [pltpu.make_async_remote_copy(src_ref=p_refs[i], dst_ref=q_refs[i], send_sem=send_sems.at[i],
                                            recv_sem=recv_sems.at[i], device_id=(x, y, 1 - c), device_id_type=MESH)
               for i in range(n)]
        for cp in cps:
            cp.start()
        for cp in cps:
            cp.wait()

    return pl.pallas_call(body, in_specs=[_ANY] * n, out_specs=[_ANY] * n, out_shape=[S(p.shape, p.dtype) for p in ps],
                          scratch_shapes=[pltpu.SemaphoreType.DMA((n,)), pltpu.SemaphoreType.DMA((n,))], name=name)(*ps)


def _quarter_spec(tr, r, c, dim):
    if dim is None:
        return pl.BlockSpec((1, tr, c), lambda i, me: (me[0], i, 0))
    if dim == 0:
        return pl.BlockSpec((tr, c), lambda i, me: (me[0] * (r // tr) + i, 0))
    return pl.BlockSpec((tr, c), lambda i, me: (i, me[0]))


def _whole_shape(r, c, dim):
    return (N_CHIPS, r, c) if dim is None else ((N_CHIPS * r, c) if dim == 0 else (r, N_CHIPS * c))


def _row_tile(r):
    return 256 if r % 256 == 0 else (128 if r % 128 == 0 else r)


def _cast_place(me, x, dim, dtype, name):
    r, c = x.shape
    tr = _row_tile(r)

    def body(me_ref, x_ref, o_ref):
        o_ref[...] = x_ref[...].astype(dtype).reshape(o_ref.shape)

    return pl.pallas_call(
        body, out_shape=S(_whole_shape(r, c, dim), dtype), name=name,
        grid_spec=pltpu.PrefetchScalarGridSpec(num_scalar_prefetch=1, grid=(r // tr,),
                                               in_specs=[pl.BlockSpec((tr, c), lambda i, me: (i, 0))],
                                               out_specs=_quarter_spec(tr, r, c, dim)),
        compiler_params=_cp(("parallel",)))(me, x)


def _sum_parts(me, g, land, dim, name):
    _, r, c = land.shape
    tr = _row_tile(r)

    def body(me_ref, g_ref, l_ref, o_ref):
        own = g_ref[...].astype(F32).reshape(tr, c)
        o_ref[...] = ((own + l_ref[0].astype(F32)) + l_ref[1].astype(F32)) + l_ref[2].astype(F32)

    return pl.pallas_call(
        body, out_shape=S((r, c), F32), name=name,
        grid_spec=pltpu.PrefetchScalarGridSpec(
            num_scalar_prefetch=1, grid=(r // tr,),
            in_specs=[_quarter_spec(tr, r, c, dim), pl.BlockSpec((3, tr, c), lambda i, me: (0, i, 0))],
            out_specs=pl.BlockSpec((tr, c), lambda i, me: (i, 0))),
        compiler_params=_cp(("parallel",)))(me, g, land)


_HBM = pl.BlockSpec(memory_space=pltpu.HBM)
_SEM = pl.BlockSpec(memory_space=pltpu.SEMAPHORE)
_EFFECT = pltpu.SideEffectType.DATAFLOW_SIDE_EFFECTING


def _quarter(ref, s, dim):
    n = ref.shape[dim] // N_CHIPS
    start = pl.multiple_of(s * n, n)
    return ref.at[pl.ds(start, n), :] if dim == 0 else ref.at[:, pl.ds(start, n)]


def _part(ref, s, dim):
    return ref.at[s] if dim is None else _quarter(ref, s, dim)


def _gather_plan(i, dim):
    return dict(src=i, dst=i, send_src=lambda ref, me, peer, k: _part(ref, me, dim),
                send_dst=lambda ref, me, peer, k: _part(ref, me, dim), recv_dst=lambda ref, me, peer, k: _part(ref, peer, dim))


def _scatter_plan(i, j, dim):
    return dict(src=i, dst=j, send_src=lambda ref, me, peer, k: _part(ref, peer, dim),
                send_dst=lambda ref, me, peer, k: ref.at[k], recv_dst=lambda ref, me, peer, k: ref.at[k])


def _hbm(a):
    return pltpu.with_memory_space_constraint(a, pltpu.HBM)


def _remote_copies(refs, send_sems, recv_sems, plans):
    me, c, peers = _chip_peers()
    sends, recvs = [], []
    for i, p in enumerate(plans):
        for k, (px, py) in enumerate(peers):
            peer = 2 * px + py
            sems = dict(send_sem=send_sems.at[3 * i + k], recv_sem=recv_sems.at[3 * i + k], device_id=(px, py, c),
                        device_id_type=MESH)
            src = p['send_src'](refs[p['src']], me, peer, k)
            sends.append(pltpu.make_async_remote_copy(src_ref=src, dst_ref=p['send_dst'](refs[p['dst']], me, peer, k), **sems))
            recvs.append(pltpu.make_async_remote_copy(src_ref=src, dst_ref=p['recv_dst'](refs[p['dst']], me, peer, k), **sems))
    return sends, recvs


def _exchange_start(arrays, plans, name, after=()):
    n, m, extra = len(arrays), len(plans), len(after)

    def body(*refs):
        send_sems, recv_sems, token = refs[n + extra], refs[n + extra + 1], refs[-1]
        sends, _ = _remote_copies(refs[:n], send_sems, recv_sems, plans)
        for cp in sends:
            cp.start()
        token[...] = jnp.zeros_like(token)

    outs = pl.pallas_call(
        body, name=name,
        out_shape=(pltpu.SemaphoreType.DMA((3 * m,)), pltpu.SemaphoreType.DMA((3 * m,)),
                   *[pltpu.HBM(a.shape, a.dtype) for a in arrays], S((8, 128), F32)),
        in_specs=[_HBM] * n + [_ANY] * extra, out_specs=(_SEM, _SEM, *[_HBM] * n, pl.BlockSpec(memory_space=pltpu.VMEM)),
        input_output_aliases={i: 2 + i for i in range(n)},
        compiler_params=pltpu.CompilerParams(has_side_effects=_EFFECT))(*[_hbm(a) for a in arrays], *after)
    return outs[0], outs[1], outs[2:2 + n], outs[-1]


def _exchange_wait(started, plans, after, name):
    send_sems, recv_sems, arrays, _ = started
    n = len(arrays)

    def body(*refs):
        sends, recvs = _remote_copies(refs[:n], refs[n], refs[n + 1], plans)
        for cp in sends:
            cp.wait_send()
        for cp in recvs:
            cp.wait_recv()

    return pl.pallas_call(
        body, name=name, out_shape=tuple(pltpu.HBM(a.shape, a.dtype) for a in arrays),
        in_specs=[_HBM] * n + [_SEM, _SEM, _ANY], out_specs=tuple([_HBM] * n),
        input_output_aliases={i: i for i in range(n)},
        compiler_params=pltpu.CompilerParams(has_side_effects=_EFFECT))(*arrays, send_sems, recv_sems, after)


def _add2(p, q):
    r, c = p.shape
    tr = 128 if r % 128 == 0 else r

    def body(p_ref, q_ref, o_ref):
        o_ref[...] = p_ref[...] + q_ref[...]

    return pl.pallas_call(body, grid=(r // tr,), in_specs=[_row_spec(tr, c), _row_spec(tr, c)], out_specs=_row_spec(tr, c),
                          out_shape=S((r, c), F32), name="add_cores", compiler_params=_cp(("parallel",)))(p, q)


def _adamw(w, g, m, v, name, g2=None):
    r, c = w.shape
    tr = 256 if r % 256 == 0 else (128 if r % 128 == 0 else r)
    c1 = 1.0 / (1.0 - ADAM_B1 ** ADAM_STEP)
    c2 = 1.0 / (1.0 - ADAM_B2 ** ADAM_STEP)
    gs = [g] if g2 is None else [g, g2]

    def body(*refs):
        w_ref, m_ref, v_ref = refs[0], refs[1], refs[2]
        g_refs = refs[3:3 + len(gs)]
        go_ref, d_ref, nm_ref, nv_ref = refs[3 + len(gs):]
        gv = g_refs[0][...]
        if len(gs) == 2:
            gv = gv + g_refs[1][...]
        go_ref[...] = gv
        nm = ADAM_B1 * m_ref[...] + (1.0 - ADAM_B1) * gv
        nv = ADAM_B2 * v_ref[...] + (1.0 - ADAM_B2) * (gv * gv)
        nm_ref[...] = nm
        nv_ref[...] = nv
        d_ref[...] = -ADAM_LR * ((nm * c1) / (jnp.sqrt(nv * c2) + ADAM_EPS) + ADAM_WD * w_ref[...])

    return pl.pallas_call(body, grid=(r // tr,), in_specs=[_row_spec(tr, c)] * (3 + len(gs)),
                          out_specs=[_row_spec(tr, c)] * 4, out_shape=[S((r, c), F32)] * 4, name=name,
                          compiler_params=_cp(("parallel",)))(w, m, v, *gs)


_WEIGHTS = ['pre_norm', 'post_norm', 'rel_bias', 'a_w_in', 'a_lam_re', 'a_lam_im', 'a_log_dt', 'a_b_re', 'a_b_im',
            'a_c_re', 'a_c_im', 'a_d', 'a_w_glu', 'a_b_glu', 'a_w_out', 'b_w_in', 'b_sinks', 'b_w_out', 'c_w_in',
            'c_q_norm', 'c_kv_norm', 'c_w_uq', 'c_w_ukv', 'c_w_out', 'd_w_in', 'd_ln_g', 'd_ln_b', 'd_w_s', 'd_b_s',
            'd_w_out']
_BIG = [('a_w_in', 1024, 2048, 1), ('a_w_glu', 1024, 1024, 0), ('a_w_out', 1024, 1024, 0), ('b_w_in', 1024, 2304, 1),
        ('b_w_out', 1024, 1024, 0), ('c_w_in', 1024, 2080, 1), ('c_w_uq', 768, 1536, 1), ('c_w_ukv', 256, 2048, 1),
        ('c_w_out', 1024, 1024, 0), ('d_w_in', 1024, 3072, 1), ('d_w_out', 1024, 1024, 0)]
_SHARDED_VECS = [('c_q_norm', 768), ('c_kv_norm', 256), ('d_ln_g', 1024), ('d_ln_b', 1024)]
_REPLICATED = [n for n in _WEIGHTS if n not in [b[0] for b in _BIG] and n not in [s[0] for s in _SHARDED_VECS]]
_LANES = 1024
_SMALL_Q_ROWS = 136


def kernel(x, pre_norm, post_norm, rel_bias, a_w_in, a_lam_re, a_lam_im, a_log_dt, a_b_re, a_b_im, a_c_re, a_c_im, a_d, a_w_glu, a_b_glu, a_w_out, b_w_in, b_sinks, b_w_out, c_w_in, c_q_norm, c_kv_norm, c_w_uq, c_w_ukv, c_w_out, d_w_in, d_ln_g, d_ln_b, d_w_s, d_b_s, d_w_out, loss_target, m_pre_norm, m_post_norm, m_rel_bias, m_a_w_in, m_a_lam_re, m_a_lam_im, m_a_log_dt, m_a_b_re, m_a_b_im, m_a_c_re, m_a_c_im, m_a_d, m_a_w_glu, m_a_b_glu, m_a_w_out, m_b_w_in, m_b_sinks, m_b_w_out, m_c_w_in, m_c_q_norm, m_c_kv_norm, m_c_w_uq, m_c_w_ukv, m_c_w_out, m_d_w_in, m_d_ln_g, m_d_ln_b, m_d_w_s, m_d_b_s, m_d_w_out, v_pre_norm, v_post_norm, v_rel_bias, v_a_w_in, v_a_lam_re, v_a_lam_im, v_a_log_dt, v_a_b_re, v_a_b_im, v_a_c_re, v_a_c_im, v_a_d, v_a_w_glu, v_a_b_glu, v_a_w_out, v_b_w_in, v_b_sinks, v_b_w_out, v_c_w_in, v_c_q_norm, v_c_kv_norm, v_c_w_uq, v_c_w_ukv, v_c_w_out, v_d_w_in, v_d_ln_g, v_d_ln_b, v_d_w_s, v_d_b_s, v_d_w_out):
    arg = dict(locals())
    me = 2 * lax.axis_index("x") + lax.axis_index("y")
    xin = x[0]
    t = xin.shape[0]

    big = {n: (r, c, dim) for n, r, c, dim in _BIG}

    me1 = me.reshape(1).astype(jnp.int32)

    def gather_start(names, tag, after):
        wholes, plans = [], []
        for i, n in enumerate(names):
            if n in big:
                r, c, dim = big[n]
                src, dtype = arg[n][0], BF16
                if dim == 1 and (c // N_CHIPS) % 128:
                    dim = None
            else:
                src, dim, dtype = arg[n], None, F32
            wholes.append(_cast_place(me1, src, dim, dtype, "place_" + n))
            plans.append(_gather_plan(i, dim))
        return _exchange_start(wholes, plans, "gather_start_" + tag, after=after), plans

    def gather_wait(names, started, after, tag):
        st, plans = started
        out = {}
        for n, a in zip(names, _exchange_wait(st, plans, after, "gather_wait_" + tag)):
            if n in big:
                r, c, dim = big[n]
                out[n] = a if a.ndim == 2 else a.transpose(1, 0, 2).reshape(r, c)
            else:
                out[n] = a.reshape(1, -1)
        return out

    groups = {'a1': ['a_w_in'], 'a2': ['a_w_glu', 'a_w_out'], 'b': ['b_w_in', 'b_w_out'],
              'c': ['c_w_in', 'c_w_uq', 'c_w_ukv', 'c_w_out', 'c_q_norm', 'c_kv_norm'],
              'd': ['d_w_in', 'd_w_out', 'd_ln_g', 'd_ln_b']}
    started, tokens = {}, []
    for tag, names in groups.items():
        started[tag] = gather_start(names, tag, tokens)
        tokens = [started[tag][0][3]]
    wt = gather_wait(groups['a1'], started['a1'], xin, 'a1')
    cos, sin = _rope_tables(t)
    bucket = jnp.asarray(_t5_bucket_table())
    s5p = _s5_params(a_lam_re[0], a_lam_im[0], a_log_dt[0], a_b_re[0], a_b_im[0], a_c_re[0], a_c_im[0])
    bst = d_b_s[0].T

    x0 = xin
    h0 = _prenorm(x0, pre_norm[0:1], "pre0", after=tokens)
    proj0 = _mm(h0, wt['a_w_in'], name="a_in")
    wt.update(gather_wait(groups['a2'], started['a2'], proj0, 'a2'))
    p0, res0 = _s5_fwd(proj0, s5p, a_d, wt['a_w_glu'], a_b_glu)
    y0 = _mm(p0, wt['a_w_out'], name="a_out")
    x1 = _postnorm(x0, y0, post_norm[0:1], "post0")

    wt.update(gather_wait(groups['b'], started['b'], x1, 'b'))
    h1 = _prenorm(x1, pre_norm[1:2], "pre1")
    proj1 = _mm(h1, wt['b_w_in'], name="b_in")
    bias = _swa_bias(rel_bias, bucket)
    p1 = _swa_fwd(proj1, bias, b_sinks)
    y1 = _mm(p1, wt['b_w_out'], name="b_out")
    x2 = _postnorm(x1, y1, post_norm[1:2], "post1")

    wt.update(gather_wait(groups['c'], started['c'], x2, 'c'))
    w_c_in = jnp.concatenate([wt['c_w_in'][:, :1056], jnp.zeros((1024, 96), BF16), wt['c_w_in'][:, 1056:],
                              jnp.zeros((1024, MLA_PROJ - MLA_Z1), BF16)], axis=1)
    w_uq = _perm_uq(wt['c_w_uq'])
    w_ukv = _perm_ukv(wt['c_w_ukv'])
    h2 = _prenorm(x2, pre_norm[2:3], "pre2")
    proj2 = _mm(h2, w_c_in, name="c_in")
    p2, res2 = _mla_fwd(proj2, wt['c_q_norm'], wt['c_kv_norm'], w_uq, w_ukv, cos, sin)
    y2 = _mm(p2, wt['c_w_out'], name="c_out")
    x3 = _postnorm(x2, y2, post_norm[2:3], "post2")

    wt.update(gather_wait(groups['d'], started['d'], x3, 'd'))
    vec = wt
    h3 = _prenorm(x3, pre_norm[3:4], "pre3")
    proj3 = _mm(h3, wt['d_w_in'], name="d_in")
    p3 = _sgu_fwd(proj3, vec['d_ln_g'], vec['d_ln_b'], d_w_s[0], bst)
    y3 = _mm(p3, wt['d_w_out'], name="d_out")
    x4 = _postnorm(x3, y3, post_norm[3:4], "post3")

    loss_part, dx = _loss_head(x4, loss_target[0])
    loss = lax.psum(loss_part[0, 0], ("x", "y", "c"))

    gr = {}
    d_pre, d_post = [None] * 4, [None] * 4
    scattered = []

    def scatter_start(items, tag):
        srcs = [a for _, a, _ in items]
        n = len(items)
        plans = [_scatter_plan(i, n + i, dim) for i, (_, _, dim) in enumerate(items)]
        lands = []
        for _, a, dim in items:
            part = a.shape[1:] if dim is None else tuple(s // N_CHIPS if i == dim else s for i, s in enumerate(a.shape))
            lands.append(lax.empty((3,) + part, a.dtype))
        st = _exchange_start(srcs + lands, plans, "scatter_start_" + tag)
        scattered.append((st, plans, items, tag))
        return [st[3]]

    def stacked(g):
        r, c = g.shape
        return g.reshape(r, N_CHIPS, c // N_CHIPS).transpose(1, 0, 2)

    dy, d_post[3] = _postnorm_bwd(dx, y3, post_norm[3:4], "post3_bwd")
    dp = _mm(dy, wt['d_w_out'], tb=True, name="d_out_dx")
    g_out = _mm(p3, dy, ta=True, out_dtype=BF16, name="d_out_dw")
    dproj, gr['d_w_s'], dbst, gr['d_ln_g'], gr['d_ln_b'] = _sgu_bwd(proj3, dp, vec['d_ln_g'], vec['d_ln_b'], d_w_s[0], bst)
    gr['d_b_s'] = dbst.T
    dh = _mm(dproj, wt['d_w_in'], tb=True, name="d_in_dx")
    g_in = _mm(h3, dproj, ta=True, out_dtype=BF16, name="d_in_dw")
    tok = scatter_start([('d_w_in', g_in, 1), ('d_w_out', g_out, 0)], 'd')
    dx, d_pre[3] = _prenorm_bwd(dx, dh, x3, pre_norm[3:4], "pre3_bwd")

    dy, d_post[2] = _postnorm_bwd(dx, y2, post_norm[2:3], "post2_bwd", after=tok)
    dp = _mm(dy, wt['c_w_out'], tb=True, name="c_out_dx")
    g_out = _mm(p2, dy, ta=True, out_dtype=BF16, name="c_out_dw")
    dproj, gr['c_q_norm'], gr['c_kv_norm'], dwuq, dwukv = _mla_bwd(proj2, res2, dp, wt['c_q_norm'], wt['c_kv_norm'],
                                                                  w_uq, w_ukv, cos, sin)
    dh = _mm(dproj, w_c_in, tb=True, name="c_in_dx")
    dwc = _mm(h2, dproj, ta=True, out_dtype=BF16, name="c_in_dw")
    g_in = stacked(jnp.concatenate([dwc[:, :1056], dwc[:, MLA_Z0:MLA_Z1]], axis=1))
    tok = scatter_start([('c_w_in', g_in, None), ('c_w_uq', _unperm_uq(dwuq), 1), ('c_w_ukv', _unperm_ukv(dwukv), 1),
                         ('c_w_out', g_out, 0)], 'c')
    dx, d_pre[2] = _prenorm_bwd(dx, dh, x2, pre_norm[2:3], "pre2_bwd")

    dy, d_post[1] = _postnorm_bwd(dx, y1, post_norm[1:2], "post1_bwd", after=tok)
    dp = _mm(dy, wt['b_w_out'], tb=True, name="b_out_dx")
    g_out = _mm(p1, dy, ta=True, out_dtype=BF16, name="b_out_dw")
    dqz, dkv, dss, dsk = _swa_bwd(proj1, dp, bias, b_sinks)
    gr['rel_bias'], gr['b_sinks'] = _swa_dbias(dss, dsk, bucket)
    dproj = jnp.concatenate([dqz[:, :1024], dkv.astype(BF16), dqz[:, 1024:]], axis=1)
    dh = _mm(dproj, wt['b_w_in'], tb=True, name="b_in_dx")
    g_in = stacked(_mm(h1, dproj, ta=True, out_dtype=BF16, name="b_in_dw"))
    tok = scatter_start([('b_w_in', g_in, None), ('b_w_out', g_out, 0)], 'b')
    dx, d_pre[1] = _prenorm_bwd(dx, dh, x1, pre_norm[1:2], "pre1_bwd")

    dy, d_post[0] = _postnorm_bwd(dx, y0, post_norm[0:1], "post0_bwd", after=tok)
    dp = _mm(dy, wt['a_w_out'], tb=True, name="a_out_dx")
    g_out = _mm(p0, dy, ta=True, out_dtype=BF16, name="a_out_dw")
    dproj, s5cots, gr['a_d'], g_glu, gr['a_b_glu'] = _s5_bwd(proj0, res0, dp, s5p, a_d, wt['a_w_glu'], a_b_glu)
    tok = scatter_start([('a_w_glu', g_glu, 0), ('a_w_out', g_out, 0)], 'a2')
    g_in = _mm(h0, dproj, ta=True, out_dtype=BF16, name="a_in_dw", after=tok)
    tok = scatter_start([('a_w_in', g_in, 1)], 'a1')
    dh = _mm(dproj, wt['a_w_in'], tb=True, name="a_in_dx", after=tok)
    dx, d_pre[0] = _prenorm_bwd(dx, dh, x0, pre_norm[0:1], "pre0_bwd")
    s5g = _s5_param_grads(s5p, s5cots, after=tok)
    for n, g in zip(['a_lam_re', 'a_lam_im', 'a_log_dt', 'a_b_re', 'a_b_im', 'a_c_re', 'a_c_im'], s5g):
        gr[n] = g
    gr['pre_norm'] = jnp.concatenate(d_pre, axis=0)
    gr['post_norm'] = jnp.concatenate(d_post, axis=0)
    small_names = _REPLICATED + [n for n, _ in _SHARDED_VECS]
    small = jnp.concatenate([gr[n].reshape(-1) for n in small_names])
    small = jnp.pad(small, (0, N_CHIPS * _SMALL_Q_ROWS * _LANES - small.shape[0])).reshape(N_CHIPS, _SMALL_Q_ROWS, _LANES)
    scatter_start([('small', small, None)], 'a0')

    partial = {}
    for st, plans, items, tag in scattered:
        done = _exchange_wait(st, plans, small, "scatter_wait_" + tag)
        for i, (n, _, dim) in enumerate(items):
            partial[n] = _sum_parts(me1, done[i], done[len(items) + i], dim, "sum_parts_" + n)
    order = [n for n, _, _, _ in _BIG] + ['small']
    other = dict(zip(order, _sibling_swap([partial[n] for n in order], "swap_cores")))
    small_q = _add2(partial['small'], other['small'])
    small_all = _all_gather_chips(small_q, "gather_small_grads").reshape(-1)
    grads = {}
    off = 0
    for n in _REPLICATED:
        sz = math.prod(arg[n].shape)
        grads[n] = small_all[off:off + sz].reshape(arg[n].shape)
        off += sz
    for n, sz in _SHARDED_VECS:
        q = sz // N_CHIPS
        grads[n] = lax.dynamic_slice(small_all, (off + me * q,), (q,)).reshape(arg[n].shape)
        off += sz

    delta, new_m, new_v = {}, {}, {}
    for n, _, _, _ in _BIG:
        shp = arg[n].shape
        two = (shp[1], shp[2])
        outs = _adamw(arg[n].reshape(two), partial[n], arg['m_' + n].reshape(two), arg['v_' + n].reshape(two),
                      "adamw_" + n, g2=other[n])
        grads[n], delta[n], new_m[n], new_v[n] = [o.reshape(shp) for o in outs]

    for n in small_names:
        shp = arg[n].shape
        two = (math.prod(shp[:-1]), shp[-1])
        outs = _adamw(arg[n].reshape(two), grads[n].reshape(two), arg['m_' + n].reshape(two), arg['v_' + n].reshape(two),
                      "adamw_" + n)
        delta[n], new_m[n], new_v[n] = [o.reshape(shp) for o in outs[1:]]

    return (loss, dx.reshape(x.shape), *[grads[n] for n in _WEIGHTS], *[delta[n] for n in _WEIGHTS],
            *[new_m[n] for n in _WEIGHTS], *[new_v[n] for n in _WEIGHTS])
```

```python
import functools
import math

import numpy as np
import jax
import jax.numpy as jnp
from jax import lax
from jax.experimental import pallas as pl
from jax.experimental.pallas import tpu as pltpu

F32 = jnp.float32
BF16 = jnp.bfloat16
S = jax.ShapeDtypeStruct
MESH = pl.DeviceIdType.MESH

D_MODEL = 1024
EPS = 1e-6
NEG_INF = -1e30
N_CHIPS = 4

SSM_GROUPS = 64
SSM_GROUP = 16
SSM_STATE = 64
SSM_CHUNK = 16
SSM_BLOCK = SSM_CHUNK * SSM_GROUP

HEAD_DIM = 64
SWA_HEADS = 16
SWA_GROUP = 8
WINDOW = 128
REL_BUCKETS = 32
REL_MAX_DIST = 128

MLA_HEADS = 16
MLA_NOPE = 64
MLA_ROPE = 32
MLA_Q_RANK = 768
MLA_KV_RANK = 256
MLA_SCALE = (MLA_NOPE + MLA_ROPE) ** -0.5
MLA_BQ = 512
MLA_BK = 512
ROPE_BASE = 10000.0

SGU_CHUNK = 128
SGU_GROUPS = 16
SGU_GDIM = 64

ADAM_LR = 0.001
ADAM_B1 = 0.9
ADAM_B2 = 0.999
ADAM_EPS = 1e-08
ADAM_WD = 0.01
ADAM_STEP = 10

VMEM_BIG = 56 * 1024 * 1024


def _cp(sem, vmem=None):
    return pltpu.CompilerParams(dimension_semantics=sem, vmem_limit_bytes=vmem)


def _row_spec(tr, c):
    return pl.BlockSpec((tr, c), lambda i: (i, 0))


def _full_spec(shape):
    nd = len(shape)
    return pl.BlockSpec(shape, lambda *_: (0,) * nd)


def _gelu(x):
    c = 0.7978845608028654
    return 0.5 * x * (1.0 + jnp.tanh(c * (x + 0.044715 * x * x * x)))


def _gelu_grad(x):
    c = 0.7978845608028654
    t = jnp.tanh(c * (x + 0.044715 * x * x * x))
    return 0.5 * (1.0 + t) + 0.5 * x * (1.0 - t * t) * c * (1.0 + 3.0 * 0.044715 * x * x)


def _silu(z):
    return z * jax.nn.sigmoid(z)


def _silu_grad(z):
    s = jax.nn.sigmoid(z)
    return s * (1.0 + z * (1.0 - s))


def _dot(a, b):
    return lax.dot_general(a, b, (((1,), (0,)), ((), ())), preferred_element_type=F32)


def _dot_nt(a, b):
    return lax.dot_general(a, b, (((1,), (1,)), ((), ())), preferred_element_type=F32)


def _dot_tn(a, b):
    return lax.dot_general(a, b, (((0,), (0,)), ((), ())), preferred_element_type=F32)


def _pick(n, cap):
    best = 0
    for d in range(128, min(n, cap) + 1, 128):
        if n % d == 0:
            best = d
    if best < 256:
        return n
    return best


def _mm(a, b, *, ta=False, tb=False, out_dtype=F32, name, after=()):
    m, k = (a.shape[1], a.shape[0]) if ta else a.shape
    n = b.shape[0] if tb else b.shape[1]
    assert (b.shape[1] if tb else b.shape[0]) == k
    tm, tn, tk = _pick(m, 1024), _pick(n, 1024), _pick(k, 1024)
    if tn > 1024:
        tm = _pick(m, 256)
    if tk > 1024:
        tm, tn = _pick(m, 256), _pick(n, 256)
    nk = k // tk
    dims = (((0 if ta else 1,), (1 if tb else 0,)), ((), ()))

    def body(a_ref, b_ref, *rest):
        o_ref, acc_ref = rest[-2], rest[-1]
        kk = pl.program_id(2)

        @pl.when(kk == 0)
        def _():
            acc_ref[...] = jnp.zeros_like(acc_ref)

        acc_ref[...] += lax.dot_general(a_ref[...].astype(BF16), b_ref[...].astype(BF16), dims,
                                        preferred_element_type=F32)

        @pl.when(kk == nk - 1)
        def _():
            o_ref[...] = acc_ref[...].astype(out_dtype)

    a_spec = pl.BlockSpec((tk, tm), lambda i, j, kk: (kk, i)) if ta else pl.BlockSpec((tm, tk), lambda i, j, kk: (i, kk))
    b_spec = pl.BlockSpec((tn, tk), lambda i, j, kk: (j, kk)) if tb else pl.BlockSpec((tk, tn), lambda i, j, kk: (kk, j))
    return pl.pallas_call(
        body, grid=(m // tm, n // tn, nk), in_specs=[a_spec, b_spec] + [pl.BlockSpec(memory_space=pl.ANY)] * len(after),
        out_specs=pl.BlockSpec((tm, tn), lambda i, j, kk: (i, j)), out_shape=S((m, n), out_dtype),
        scratch_shapes=[pltpu.VMEM((tm, tn), F32)], name=name,
        compiler_params=_cp(("parallel", "parallel", "arbitrary"), VMEM_BIG))(a, b, *after)


def _prenorm(x, g, name, after=()):
    t, d = x.shape
    tr = 256

    def body(x_ref, g_ref, *rest):
        o_ref = rest[-1]
        xv = x_ref[...]
        r = lax.rsqrt(jnp.mean(xv * xv, axis=-1, keepdims=True) + EPS)
        o_ref[...] = (xv * r * g_ref[...]).astype(BF16)

    return pl.pallas_call(body, grid=(t // tr,),
                          in_specs=[_row_spec(tr, d), _full_spec((1, d))] + [pl.BlockSpec(memory_space=pl.ANY)] * len(after),
                          out_specs=_row_spec(tr, d), out_shape=S((t, d), BF16), name=name,
                          compiler_params=_cp(("parallel",)))(x, g, *after)


def _postnorm(x, y, g, name):
    t, d = x.shape
    tr = 256

    def body(x_ref, y_ref, g_ref, o_ref):
        yv = y_ref[...]
        r = lax.rsqrt(jnp.mean(yv * yv, axis=-1, keepdims=True) + EPS)
        o_ref[...] = x_ref[...] + yv * r * g_ref[...]

    return pl.pallas_call(body, grid=(t // tr,), in_specs=[_row_spec(tr, d), _row_spec(tr, d), _full_spec((1, d))],
                          out_specs=_row_spec(tr, d), out_shape=S((t, d), F32), name=name,
                          compiler_params=_cp(("parallel",)))(x, y, g)


def _rms_bwd_rows(dout, xin, g):
    r = lax.rsqrt(jnp.mean(xin * xin, axis=-1, keepdims=True) + EPS)
    xh = xin * r
    gd = dout * g
    dx = r * (gd - xh * jnp.mean(gd * xh, axis=-1, keepdims=True))
    return dx, jnp.sum(dout * xh, axis=0, keepdims=True)


def _postnorm_bwd(dxn, y, g, name, after=()):
    t, d = y.shape
    tr = 256

    def body(dx_ref, y_ref, g_ref, *rest):
        dy_ref, dg_ref = rest[-2], rest[-1]

        @pl.when(pl.program_id(0) == 0)
        def _():
            dg_ref[...] = jnp.zeros_like(dg_ref)

        dy, dg = _rms_bwd_rows(dx_ref[...], y_ref[...], g_ref[...])
        dy_ref[...] = dy.astype(BF16)
        dg_ref[...] += dg

    return pl.pallas_call(body, grid=(t // tr,),
                          in_specs=[_row_spec(tr, d), _row_spec(tr, d), _full_spec((1, d))]
                          + [pl.BlockSpec(memory_space=pl.ANY)] * len(after),
                          out_specs=[_row_spec(tr, d), _full_spec((1, d))],
                          out_shape=[S((t, d), BF16), S((1, d), F32)], name=name,
                          compiler_params=_cp(("arbitrary",)))(dxn, y, g, *after)


def _prenorm_bwd(dxn, dh, x, g, name):
    t, d = x.shape
    tr = 256

    def body(dxn_ref, dh_ref, x_ref, g_ref, dx_ref, dg_ref):
        @pl.when(pl.program_id(0) == 0)
        def _():
            dg_ref[...] = jnp.zeros_like(dg_ref)

        dx, dg = _rms_bwd_rows(dh_ref[...], x_ref[...], g_ref[...])
        dx_ref[...] = dxn_ref[...] + dx
        dg_ref[...] += dg

    return pl.pallas_call(body, grid=(t // tr,),
                          in_specs=[_row_spec(tr, d), _row_spec(tr, d), _row_spec(tr, d), _full_spec((1, d))],
                          out_specs=[_row_spec(tr, d), _full_spec((1, d))],
                          out_shape=[S((t, d), F32), S((1, d), F32)], name=name,
                          compiler_params=_cp(("arbitrary",)))(dxn, dh, x, g)


def _loss_head(y, target):
    t, d = y.shape
    tr = 256

    def body(y_ref, t_ref, l_ref, dy_ref):
        @pl.when(pl.program_id(0) == 0)
        def _():
            l_ref[...] = jnp.zeros_like(l_ref)

        e = y_ref[...] - t_ref[...]
        dy_ref[...] = e * (1.0 / d)
        l_ref[...] += 0.5 * jnp.sum(jnp.mean(e * e, axis=-1, keepdims=True), axis=0, keepdims=True)

    return pl.pallas_call(body, grid=(t // tr,), in_specs=[_row_spec(tr, d), _row_spec(tr, d)],
                          out_specs=[_full_spec((1, 1)), _row_spec(tr, d)],
                          out_shape=[S((1, 1), F32), S((t, d), F32)], name="loss_head",
                          compiler_params=_cp(("arbitrary",)))(y, target)


def _sgu_parts(proj, lg, lb, ws_ref, bst):
    a = proj[:, 0:1024]
    bb = proj[:, 1024:2048]
    z = proj[:, 2048:3072]
    u = _gelu(a)
    vp = _gelu(bb)
    mu = jnp.mean(vp, axis=-1, keepdims=True)
    xc = vp - mu
    rstd = lax.rsqrt(jnp.mean(xc * xc, axis=-1, keepdims=True) + EPS)
    xh = xc * rstd
    vn = xh * lg + lb
    row = lax.broadcasted_iota(jnp.int32, (SGU_CHUNK, SGU_CHUNK), 0)
    col = lax.broadcasted_iota(jnp.int32, (SGU_CHUNK, SGU_CHUNK), 1)
    tril = row >= col
    ws = [jnp.where(tril, ws_ref[g], 0.0).astype(BF16) for g in range(SGU_GROUPS)]
    s = jnp.concatenate(
        [_dot(ws[g], vn[:, g * 64:(g + 1) * 64].astype(BF16)) + bst[:, g:g + 1] for g in range(SGU_GROUPS)], axis=1)
    return a, bb, z, u, vp, rstd, xh, vn, ws, tril, s


def _sgu_fwd(proj, lg, lb, w_s, bst):
    t = proj.shape[0]
    tr = SGU_CHUNK

    def body(p_ref, lg_ref, lb_ref, ws_ref, bst_ref, o_ref):
        _, _, z, u, _, _, _, _, _, _, s = _sgu_parts(p_ref[...], lg_ref[...], lb_ref[...], ws_ref, bst_ref[...])
        o_ref[...] = (u * s * _silu(z)).astype(BF16)

    return pl.pallas_call(
        body, grid=(t // tr,),
        in_specs=[_row_spec(tr, 3072), _full_spec((1, 1024)), _full_spec((1, 1024)), _full_spec((16, 128, 128)),
                  _full_spec((128, 16))],
        out_specs=_row_spec(tr, 1024), out_shape=S((t, 1024), BF16), name="sgu_fwd",
        compiler_params=_cp(("parallel",)))(proj, lg, lb, w_s, bst)


def _sgu_bwd(proj, dp, lg, lb, w_s, bst):
    t = proj.shape[0]
    tr = SGU_CHUNK

    def body(p_ref, dp_ref, lg_ref, lb_ref, ws_ref, bst_ref, dproj_ref, dws_ref, dbst_ref, dlg_ref, dlb_ref):
        @pl.when(pl.program_id(0) == 0)
        def _():
            dws_ref[...] = jnp.zeros_like(dws_ref)
            dbst_ref[...] = jnp.zeros_like(dbst_ref)
            dlg_ref[...] = jnp.zeros_like(dlg_ref)
            dlb_ref[...] = jnp.zeros_like(dlb_ref)

        lgv = lg_ref[...]
        a, bb, z, u, vp, rstd, xh, vn, ws, tril, s = _sgu_parts(p_ref[...], lgv, lb_ref[...], ws_ref, bst_ref[...])
        dpv = dp_ref[...].astype(F32)
        sz = _silu(z)
        du = dpv * s * sz
        ds = dpv * u * sz
        dz = dpv * u * s * _silu_grad(z)
        dvn_parts, dbs_cols = [], []
        for g in range(SGU_GROUPS):
            dsg = ds[:, g * 64:(g + 1) * 64]
            dsg_b = dsg.astype(BF16)
            dvn_parts.append(_dot_tn(ws[g], dsg_b))
            dws_ref[g] += jnp.where(tril, _dot_nt(dsg_b, vn[:, g * 64:(g + 1) * 64].astype(BF16)), 0.0)
            dbs_cols.append(jnp.sum(dsg, axis=1, keepdims=True))
        dvn = jnp.concatenate(dvn_parts, axis=1)
        dbst_ref[...] += jnp.concatenate(dbs_cols, axis=1)
        dlg_ref[...] += jnp.sum(dvn * xh, axis=0, keepdims=True)
        dlb_ref[...] += jnp.sum(dvn, axis=0, keepdims=True)
        dxh = dvn * lgv
        dvp = rstd * (dxh - jnp.mean(dxh, axis=-1, keepdims=True) - xh * jnp.mean(dxh * xh, axis=-1, keepdims=True))
        dproj_ref[:, 0:1024] = (du * _gelu_grad(a)).astype(BF16)
        dproj_ref[:, 1024:2048] = (dvp * _gelu_grad(bb)).astype(BF16)
        dproj_ref[:, 2048:3072] = dz.astype(BF16)

    return pl.pallas_call(
        body, grid=(t // tr,),
        in_specs=[_row_spec(tr, 3072), _row_spec(tr, 1024), _full_spec((1, 1024)), _full_spec((1, 1024)),
                  _full_spec((16, 128, 128)), _full_spec((128, 16))],
        out_specs=[_row_spec(tr, 3072), _full_spec((16, 128, 128)), _full_spec((128, 16)), _full_spec((1, 1024)),
                   _full_spec((1, 1024))],
        out_shape=[S((t, 3072), BF16), S((16, 128, 128), F32), S((128, 16), F32), S((1, 1024), F32),
                   S((1, 1024), F32)],
        name="sgu_bwd", compiler_params=_cp(("arbitrary",)))(proj, dp, lg, lb, w_s, bst)


def _t5_bucket_table():
    qi = np.arange(WINDOW)[:, None]
    kj = np.arange(2 * WINDOW)[None, :]
    dist = np.maximum(qi + WINDOW - kj, 0)
    max_exact = REL_BUCKETS // 2
    dist_f = np.maximum(dist, 1).astype(np.float32)
    large = max_exact + (np.log(dist_f / np.float32(max_exact)) / np.float32(math.log(REL_MAX_DIST / max_exact))
                         * np.float32(REL_BUCKETS - max_exact)).astype(np.int32)
    large = np.minimum(large, REL_BUCKETS - 1)
    return np.where(dist < max_exact, dist, large).astype(np.int32)


def _swa_bias(rel_bias, bucket):
    def body(rb_ref, bk_ref, o_ref):
        bk = bk_ref[...]
        for h in range(SWA_HEADS):
            acc = jnp.zeros((WINDOW, 2 * WINDOW), F32)
            for b in range(REL_BUCKETS):
                acc = jnp.where(bk == b, rb_ref[b, h], acc)
            o_ref[h] = acc

    return pl.pallas_call(
        body, in_specs=[pl.BlockSpec(memory_space=pltpu.SMEM), pl.BlockSpec(memory_space=pltpu.VMEM)],
        out_specs=pl.BlockSpec(memory_space=pltpu.VMEM), out_shape=S((SWA_HEADS, WINDOW, 2 * WINDOW), F32),
        name="swa_bias")(rel_bias, bucket)


def _swa_dbias(ds_sum, dsink_rows, bucket):
    def body(ds_ref, dsk_ref, bk_ref, drb_ref, dsink_ref):
        bk = bk_ref[...]
        r = lax.broadcasted_iota(jnp.int32, (REL_BUCKETS, SWA_HEADS), 0)
        c = lax.broadcasted_iota(jnp.int32, (REL_BUCKETS, SWA_HEADS), 1)
        out = jnp.zeros((REL_BUCKETS, SWA_HEADS), F32)
        for b in range(REL_BUCKETS):
            m = bk == b
            for h in range(SWA_HEADS):
                val = jnp.sum(jnp.where(m, ds_ref[h], 0.0))
                out = jnp.where((r == b) & (c == h), val, out)
        drb_ref[...] = out
        dsink_ref[...] = jnp.sum(dsk_ref[...], axis=0, keepdims=True)

    return pl.pallas_call(
        body, out_shape=[S((REL_BUCKETS, SWA_HEADS), F32), S((1, SWA_HEADS), F32)], name="swa_dbias",
    )(ds_sum, dsink_rows, bucket)


def _swa_band(cur_ref, prev_ref):
    kband = jnp.concatenate([prev_ref[:, 0:128], cur_ref[:, 1024:1152]], axis=0).astype(BF16)
    vband = jnp.concatenate([prev_ref[:, 128:256], cur_ref[:, 1152:1280]], axis=0).astype(BF16)
    return kband, vband


SWA_ROWS = SWA_GROUP * WINDOW


def _swa_valid(n):
    qi = lax.broadcasted_iota(jnp.int32, (SWA_ROWS, 2 * WINDOW), 0) & (WINDOW - 1)
    kj = lax.broadcasted_iota(jnp.int32, (SWA_ROWS, 2 * WINDOW), 1)
    dist = qi + WINDOW - kj
    return (dist >= 0) & (dist < WINDOW) & (n * WINDOW + kj - WINDOW >= 0)


def _swa_stack(x, j):
    return jnp.concatenate([x[:, h * 64:(h + 1) * 64] for h in range(j * SWA_GROUP, (j + 1) * SWA_GROUP)], axis=0)


def _swa_unstack(parts):
    return jnp.concatenate([p[i * WINDOW:(i + 1) * WINDOW] for p in parts for i in range(SWA_GROUP)], axis=1)


def _swa_sink_col(sink_ref, j):
    return jnp.concatenate([jnp.full((WINDOW, 1), sink_ref[0, h], F32)
                            for h in range(j * SWA_GROUP, (j + 1) * SWA_GROUP)], axis=0)


def _swa_probs(qh, kh, bias_h, sink, valid):
    s = _dot_nt(qh, kh) * (HEAD_DIM ** -0.5) + bias_h
    s = jnp.where(valid, s, NEG_INF)
    m = jnp.maximum(jnp.max(s, axis=-1, keepdims=True), sink)
    e = jnp.exp(s - m)
    es = jnp.exp(sink - m)
    inv = 1.0 / (jnp.sum(e, axis=-1, keepdims=True) + es)
    return e * inv, es * inv


def _swa_in_specs():
    return [_row_spec(WINDOW, 2304),
            pl.BlockSpec((WINDOW, 256), lambda n: (jnp.maximum(n - 1, 0), 4)),
            _full_spec((SWA_HEADS, WINDOW, 2 * WINDOW)),
            pl.BlockSpec(memory_space=pltpu.SMEM)]


def _swa_fwd(proj, bias, sinks):
    t = proj.shape[0]

    def body(cur_ref, prev_ref, bias_ref, sink_ref, o_ref):
        n = pl.program_id(0)
        kband, vband = _swa_band(cur_ref, prev_ref)
        valid = _swa_valid(n)
        q = cur_ref[:, 0:1024].astype(BF16)
        outs = []
        for j in range(SWA_HEADS // SWA_GROUP):
            bias_j = bias_ref[j * SWA_GROUP:(j + 1) * SWA_GROUP].reshape(SWA_ROWS, 2 * WINDOW)
            p, _ = _swa_probs(_swa_stack(q, j), kband[:, j * 64:(j + 1) * 64], bias_j, _swa_sink_col(sink_ref, j), valid)
            outs.append(_dot(p.astype(BF16), vband[:, j * 64:(j + 1) * 64]))
        o_ref[...] = (_swa_unstack(outs) * _silu(cur_ref[:, 1280:2304])).astype(BF16)

    return pl.pallas_call(body, grid=(t // WINDOW,), in_specs=_swa_in_specs(), out_specs=_row_spec(WINDOW, 1024),
                          out_shape=S((t, 1024), BF16), name="swa_fwd",
                          compiler_params=_cp(("parallel",)))(proj, proj, bias, sinks)


def _swa_bwd(proj, dp, bias, sinks):
    t = proj.shape[0]
    nb = t // WINDOW

    def body(cur_ref, prev_ref, bias_ref, sink_ref, dp_ref, dqz_ref, dkv_ref, dss_ref, dsk_ref):
        n = pl.program_id(0)

        @pl.when(n == 0)
        def _():
            dkv_ref[...] = jnp.zeros_like(dkv_ref)
            dss_ref[...] = jnp.zeros_like(dss_ref)
            dsk_ref[...] = jnp.zeros_like(dsk_ref)

        kband, vband = _swa_band(cur_ref, prev_ref)
        valid = _swa_valid(n)
        z = cur_ref[:, 1280:2304]
        dpv = dp_ref[...].astype(F32)
        do = dpv * _silu(z)
        q = cur_ref[:, 0:1024].astype(BF16)
        dob = do.astype(BF16)
        outs, dqs, dsinks, dk, dv = [], [], [], [], []
        for j in range(SWA_HEADS // SWA_GROUP):
            heads = slice(j * SWA_GROUP, (j + 1) * SWA_GROUP)
            qs, dos = _swa_stack(q, j), _swa_stack(dob, j)
            kh = kband[:, j * 64:(j + 1) * 64]
            vh = vband[:, j * 64:(j + 1) * 64]
            p, ps = _swa_probs(qs, kh, bias_ref[heads].reshape(SWA_ROWS, 2 * WINDOW), _swa_sink_col(sink_ref, j), valid)
            pb = p.astype(BF16)
            outs.append(_dot(pb, vh))
            dpr = _dot_nt(dos, vh)
            dlt = jnp.sum(p * dpr, axis=-1, keepdims=True)
            ds = p * (dpr - dlt)
            dsk = -ps * dlt
            dsinks += [dsk[i * WINDOW:(i + 1) * WINDOW] for i in range(SWA_GROUP)]
            dss_ref[heads] += ds.reshape(SWA_GROUP, WINDOW, 2 * WINDOW)
            dsb = (ds * (HEAD_DIM ** -0.5)).astype(BF16)
            dqs.append(_dot(dsb, kh))
            dk.append(_dot_tn(dsb, qs))
            dv.append(_dot_tn(pb, dos))
        dqz_ref[:, 0:1024] = _swa_unstack(dqs).astype(BF16)
        dqz_ref[:, 1024:2048] = (dpv * _swa_unstack(outs) * _silu_grad(z)).astype(BF16)
        dsk_ref[...] += jnp.concatenate(dsinks, axis=1)
        dband = jnp.concatenate([dk[0], dk[1], dv[0], dv[1]], axis=1)
        prow = pl.multiple_of(jnp.maximum(n - 1, 0) * WINDOW, WINDOW)
        dkv_ref[pl.ds(prow, WINDOW), :] += dband[0:WINDOW]
        crow = pl.multiple_of(n * WINDOW, WINDOW)
        dkv_ref[pl.ds(crow, WINDOW), :] += dband[WINDOW:]

    return pl.pallas_call(
        body, grid=(nb,), in_specs=_swa_in_specs() + [_row_spec(WINDOW, 1024)],
        out_specs=[_row_spec(WINDOW, 2048), _full_spec((t, 256)), _full_spec((SWA_HEADS, WINDOW, 2 * WINDOW)),
                   _full_spec((WINDOW, SWA_HEADS))],
        out_shape=[S((t, 2048), BF16), S((t, 256), F32), S((SWA_HEADS, WINDOW, 2 * WINDOW), F32),
                   S((WINDOW, SWA_HEADS), F32)],
        name="swa_bwd", compiler_params=_cp(("arbitrary",)))(proj, proj, bias, sinks, dp)


MLA_Z0 = MLA_Q_RANK + MLA_KV_RANK + 128
MLA_Z1 = MLA_Z0 + 1024
MLA_PROJ = MLA_Z1 + 128


def _rope_tables(t):
    inv = ROPE_BASE ** (-jnp.arange(0, MLA_ROPE, 2, dtype=F32) / MLA_ROPE)
    ang = jnp.arange(t, dtype=F32)[:, None] * inv[None, :]
    return jnp.tile(jnp.cos(ang), (1, MLA_HEADS)), jnp.tile(jnp.sin(ang), (1, MLA_HEADS))


def _mla_norms(proj, gq, gkv):
    t = proj.shape[0]
    tr = 256

    def body(p_ref, gq_ref, gkv_ref, q_ref, kv_ref):
        for lo, hi, g_ref, o_ref in ((0, 768, gq_ref, q_ref), (768, 1024, gkv_ref, kv_ref)):
            xv = p_ref[:, lo:hi]
            r = lax.rsqrt(jnp.mean(xv * xv, axis=-1, keepdims=True) + EPS)
            o_ref[...] = (xv * r * g_ref[...]).astype(BF16)

    return pl.pallas_call(
        body, grid=(t // tr,), in_specs=[_row_spec(tr, 1024), _full_spec((1, 768)), _full_spec((1, 256))],
        out_specs=[_row_spec(tr, 768), _row_spec(tr, 256)], out_shape=[S((t, 768), BF16), S((t, 256), BF16)],
        name="mla_norms", compiler_params=_cp(("parallel",)))(proj, gq, gkv)


def _mla_norms_bwd(proj, dcqn, dckvn, dkr, dz, gq, gkv):
    t = proj.shape[0]
    tr = 256

    def body(p_ref, dq_ref, dkv_ref, dkr_ref, dz_ref, gq_ref, gkv_ref, dproj_ref, dgq_ref, dgkv_ref):
        @pl.when(pl.program_id(0) == 0)
        def _():
            dgq_ref[...] = jnp.zeros_like(dgq_ref)
            dgkv_ref[...] = jnp.zeros_like(dgkv_ref)

        dx, dg = _rms_bwd_rows(dq_ref[...], p_ref[:, 0:768], gq_ref[...])
        dproj_ref[:, 0:768] = dx.astype(BF16)
        dgq_ref[...] += dg
        dx, dg = _rms_bwd_rows(dkv_ref[...], p_ref[:, 768:1024], gkv_ref[...])
        dproj_ref[:, 768:1024] = dx.astype(BF16)
        dgkv_ref[...] += dg
        dproj_ref[:, 1024:MLA_Z0] = dkr_ref[...].astype(BF16)
        dproj_ref[:, MLA_Z0:MLA_Z1] = dz_ref[...]
        dproj_ref[:, MLA_Z1:MLA_PROJ] = jnp.zeros((tr, MLA_PROJ - MLA_Z1), BF16)

    return pl.pallas_call(
        body, grid=(t // tr,),
        in_specs=[_row_spec(tr, 1024), _row_spec(tr, 768), _row_spec(tr, 256), _row_spec(tr, 128), _row_spec(tr, 1024),
                  _full_spec((1, 768)), _full_spec((1, 256))],
        out_specs=[_row_spec(tr, MLA_PROJ), _full_spec((1, 768)), _full_spec((1, 256))],
        out_shape=[S((t, MLA_PROJ), BF16), S((1, 768), F32), S((1, 256), F32)],
        name="mla_norms_bwd", compiler_params=_cp(("arbitrary",)))(proj, dcqn, dckvn, dkr, dz, gq, gkv)


def _mla_pack(qf, kvf, proj, cos, sin):
    t = qf.shape[0]
    tr = 256

    def body(q_ref, kn_ref, kr_ref, cos_ref, sin_ref, qh_ref, kh_ref):
        cs, sn = cos_ref[...], sin_ref[...]
        x1, x2 = q_ref[:, 1024:1280], q_ref[:, 1280:1536]
        r1 = x1 * cs - x2 * sn
        r2 = x2 * cs + x1 * sn
        c16, s16 = cs[:, 0:16], sn[:, 0:16]
        k1, k2 = kr_ref[:, 0:16], kr_ref[:, 16:32]
        kr1 = k1 * c16 - k2 * s16
        kr2 = k2 * c16 + k1 * s16
        zpad = jnp.zeros((tr, 32), F32)
        for h in range(MLA_HEADS):
            qh_ref[h] = jnp.concatenate([q_ref[:, h * 64:(h + 1) * 64], r1[:, h * 16:(h + 1) * 16],
                                         r2[:, h * 16:(h + 1) * 16], zpad], axis=1).astype(BF16)
            kh_ref[h] = jnp.concatenate([kn_ref[:, h * 64:(h + 1) * 64], kr1, kr2, zpad], axis=1).astype(BF16)

    hspec = pl.BlockSpec((MLA_HEADS, tr, 128), lambda i: (0, i, 0))
    return pl.pallas_call(
        body, grid=(t // tr,),
        in_specs=[_row_spec(tr, 1536), _row_spec(tr, 1024), pl.BlockSpec((tr, 128), lambda i: (i, 8)),
                  _row_spec(tr, 256), _row_spec(tr, 256)],
        out_specs=[hspec, hspec], out_shape=[S((MLA_HEADS, t, 128), BF16)] * 2,
        name="mla_pack", compiler_params=_cp(("parallel",)))(qf, kvf, proj, cos, sin)


def _mla_unpack(dqh, dkh, dv, cos, sin):
    t = dqh.shape[1]
    tr = 256

    def body(dqh_ref, dkh_ref, dv_ref, cos_ref, sin_ref, dq_ref, dkv_ref, dkr_ref):
        cs, sn = cos_ref[...], sin_ref[...]
        dqn = jnp.concatenate([dqh_ref[h, :, 0:64] for h in range(MLA_HEADS)], axis=1)
        dr1 = jnp.concatenate([dqh_ref[h, :, 64:80] for h in range(MLA_HEADS)], axis=1)
        dr2 = jnp.concatenate([dqh_ref[h, :, 80:96] for h in range(MLA_HEADS)], axis=1)
        dq_ref[:, 0:1024] = dqn.astype(BF16)
        dq_ref[:, 1024:1280] = (dr1 * cs + dr2 * sn).astype(BF16)
        dq_ref[:, 1280:1536] = (dr2 * cs - dr1 * sn).astype(BF16)
        dkv_ref[:, 0:1024] = jnp.concatenate([dkh_ref[h, :, 0:64] for h in range(MLA_HEADS)], axis=1).astype(BF16)
        dkv_ref[:, 1024:2048] = dv_ref[...].astype(BF16)
        d1 = dkh_ref[0, :, 64:80]
        d2 = dkh_ref[0, :, 80:96]
        for h in range(1, MLA_HEADS):
            d1 = d1 + dkh_ref[h, :, 64:80]
            d2 = d2 + dkh_ref[h, :, 80:96]
        c16, s16 = cs[:, 0:16], sn[:, 0:16]
        dkr_ref[...] = jnp.concatenate([d1 * c16 + d2 * s16, d2 * c16 - d1 * s16, jnp.zeros((tr, 96), F32)], axis=1)

    hspec = pl.BlockSpec((MLA_HEADS, tr, 128), lambda i: (0, i, 0))
    return pl.pallas_call(
        body, grid=(t // tr,), in_specs=[hspec, hspec, _row_spec(tr, 1024), _row_spec(tr, 256), _row_spec(tr, 256)],
        out_specs=[_row_spec(tr, 1536), _row_spec(tr, 2048), _row_spec(tr, 128)],
        out_shape=[S((t, 1536), BF16), S((t, 2048), BF16), S((t, 128), F32)],
        name="mla_unpack", compiler_params=_cp(("parallel",)))(dqh, dkh, dv, cos, sin)


def _mla_attn_fwd(qh, kh, kvf, proj):
    t = qh.shape[1]
    b, bk = MLA_BQ, MLA_BK

    def body(q_ref, k_ref, v_ref, z_ref, o_ref, p_ref, lse_ref):
        r0 = pl.program_id(1) * b
        nfull = r0 // bk
        diff = (lax.broadcasted_iota(jnp.int32, (b, bk), 1) - lax.broadcasted_iota(jnp.int32, (b, bk), 0))
        qs = [q_ref[0], q_ref[1]]

        def block(c0, carry, masked):
            new = []
            for a in range(2):
                m, l, acc = carry[a]
                k = k_ref[a, pl.ds(c0, bk), :]
                v = v_ref[pl.ds(c0, bk), a * 64:(a + 1) * 64].astype(BF16)
                s = _dot_nt(qs[a], k) * MLA_SCALE
                if masked:
                    s = jnp.where(diff <= r0 - c0, s, NEG_INF)
                mn = jnp.maximum(m, jnp.max(s, axis=-1, keepdims=True))
                alpha = jnp.exp(m - mn)
                p = jnp.exp(s - mn)
                new.append((mn, alpha * l + jnp.sum(p, axis=-1, keepdims=True), alpha * acc + _dot(p.astype(BF16), v)))
            return tuple(new)

        init = (jnp.full((b, 1), NEG_INF, F32), jnp.zeros((b, 1), F32), jnp.zeros((b, 64), F32))
        carry = lax.fori_loop(0, nfull, lambda c, cr: block(pl.multiple_of(c * bk, bk), cr, False), (init, init))
        for i in range(max(1, b // bk)):
            carry = block(pl.multiple_of((nfull + i) * bk, bk), carry, True)
        outs = [acc / l for _, l, acc in carry]
        lses = [jnp.broadcast_to(m + jnp.log(l), (b, 64)) for m, l, _ in carry]
        o = jnp.concatenate(outs, axis=1)
        o_ref[...] = o
        p_ref[...] = (o * _silu(z_ref[...])).astype(BF16)
        lse_ref[0] = jnp.concatenate(lses, axis=1)

    return pl.pallas_call(
        body, grid=(MLA_HEADS // 2, t // b),
        in_specs=[pl.BlockSpec((2, b, 128), lambda i, r: (i, r, 0)), pl.BlockSpec((2, t, 128), lambda i, r: (i, 0, 0)),
                  pl.BlockSpec((t, 128), lambda i, r: (0, 8 + i)), pl.BlockSpec((b, 128), lambda i, r: (r, 9 + i))],
        out_specs=[pl.BlockSpec((b, 128), lambda i, r: (r, i)), pl.BlockSpec((b, 128), lambda i, r: (r, i)),
                   pl.BlockSpec((1, b, 128), lambda i, r: (i, r, 0))],
        out_shape=[S((t, 1024), F32), S((t, 1024), BF16), S((MLA_HEADS // 2, t, 128), F32)],
        name="mla_attn_fwd", compiler_params=_cp(("parallel", "parallel")))(qh, kh, kvf, proj)


def _mla_attn_bwd(qh, kh, kvf, proj, o, dpout, lse):
    t = qh.shape[1]
    b, bk = MLA_BQ, MLA_BK
    nb = t // b

    def body(q_ref, k_ref, v_ref, z_ref, o_ref, dp_ref, lse_ref, dq_ref, dk_ref, dv_ref, dz_ref):
        dk_ref[...] = jnp.zeros_like(dk_ref)
        dv_ref[...] = jnp.zeros_like(dv_ref)
        diff = (lax.broadcasted_iota(jnp.int32, (b, bk), 1) - lax.broadcasted_iota(jnp.int32, (b, bk), 0))
        lanes = [slice(0, 64), slice(64, 128)]

        def rbody(r, carry):
            r0 = pl.multiple_of(r * b, b)
            rows = pl.ds(r0, b)
            qs, dobs, deltas, lses = [], [], [], []
            for a in range(2):
                zz = z_ref[rows, lanes[a]]
                oo = o_ref[rows, lanes[a]]
                dpo = dp_ref[rows, lanes[a]].astype(F32)
                do = dpo * _silu(zz)
                dz_ref[rows, lanes[a]] = (dpo * oo * _silu_grad(zz)).astype(BF16)
                qs.append(q_ref[a, rows, :])
                dobs.append(do.astype(BF16))
                deltas.append(jnp.sum(do * oo, axis=-1, keepdims=True))
                lses.append(lse_ref[0, rows, a * 64:a * 64 + 1])

            def block(c0, dqs, masked):
                cols = pl.ds(c0, bk)
                out = []
                for a in range(2):
                    k = k_ref[a, cols, :]
                    v = v_ref[cols, lanes[a]].astype(BF16)
                    s = _dot_nt(qs[a], k) * MLA_SCALE
                    if masked:
                        s = jnp.where(diff <= r0 - c0, s, NEG_INF)
                    p = jnp.exp(s - lses[a])
                    ds = p * (_dot_nt(dobs[a], v) - deltas[a]) * MLA_SCALE
                    dsb = ds.astype(BF16)
                    dk_ref[a, cols, :] += _dot_tn(dsb, qs[a])
                    dv_ref[cols, lanes[a]] += _dot_tn(p.astype(BF16), dobs[a])
                    out.append(dqs[a] + _dot(dsb, k))
                return tuple(out)

            zero = jnp.zeros((b, 128), F32)
            nfull = r0 // bk
            dqs = lax.fori_loop(0, nfull, lambda c, d: block(pl.multiple_of(c * bk, bk), d, False), (zero, zero))
            for i in range(max(1, b // bk)):
                dqs = block(pl.multiple_of((nfull + i) * bk, bk), dqs, True)
            dq_ref[0, rows, :] = dqs[0]
            dq_ref[1, rows, :] = dqs[1]
            return carry

        lax.fori_loop(0, nb, rbody, 0)

    pair3 = pl.BlockSpec((2, t, 128), lambda i: (i, 0, 0))
    return pl.pallas_call(
        body, grid=(MLA_HEADS // 2,),
        in_specs=[pair3, pair3, pl.BlockSpec((t, 128), lambda i: (0, 8 + i)), pl.BlockSpec((t, 128), lambda i: (0, 9 + i)),
                  pl.BlockSpec((t, 128), lambda i: (0, i)), pl.BlockSpec((t, 128), lambda i: (0, i)),
                  pl.BlockSpec((1, t, 128), lambda i: (i, 0, 0))],
        out_specs=[pair3, pair3, pl.BlockSpec((t, 128), lambda i: (0, i)), pl.BlockSpec((t, 128), lambda i: (0, i))],
        out_shape=[S((MLA_HEADS, t, 128), F32), S((MLA_HEADS, t, 128), F32), S((t, 1024), F32), S((t, 1024), BF16)],
        name="mla_attn_bwd", compiler_params=_cp(("parallel",), VMEM_BIG))(qh, kh, kvf, proj, o, dpout, lse)


def _dot_nt_hi(a, b):
    return lax.dot_general(a, b, (((1,), (1,)), ((), ())), preferred_element_type=F32,
                           precision=lax.Precision.HIGHEST)


def _ssm_gen(lr, li, ldt, btr, bti, cr, ci):
    n = SSM_BLOCK
    dt = jnp.exp(ldt)
    x = lr * dt
    w = li * dt
    mag = jnp.exp(x)
    ab_re = mag * jnp.cos(w)
    ab_im = mag * jnp.sin(w)
    den = lr * lr + li * li
    nr = ab_re - 1.0
    f_re = (nr * lr + ab_im * li) / den
    f_im = (ab_im * lr - nr * li) / den
    bb_re = f_re * btr - f_im * bti
    bb_im = f_re * bti + f_im * btr
    inv_mag = jnp.exp(-x)
    inv_re, inv_im = inv_mag * jnp.cos(w), -inv_mag * jnp.sin(w)
    one, zero = jnp.ones_like(x), jnp.zeros_like(x)
    pos, neg = [(one, zero)], [(one, zero)]
    for _ in range(SSM_CHUNK):
        pr, pi = pos[-1]
        pos.append((pr * ab_re - pi * ab_im, pr * ab_im + pi * ab_re))
        pr, pi = neg[-1]
        neg.append((pr * inv_re - pi * inv_im, pr * inv_im + pi * inv_re))

    def stack(table, idx):
        return (jnp.concatenate([jnp.broadcast_to(table[idx(t)][0], (SSM_GROUP, SSM_STATE)) for t in range(SSM_CHUNK)], axis=0),
                jnp.concatenate([jnp.broadcast_to(table[idx(t)][1], (SSM_GROUP, SSM_STATE)) for t in range(SSM_CHUNK)], axis=0))

    def rep(v):
        return jnp.concatenate([v] * SSM_CHUNK, axis=0)

    ct_re, ct_im, bt_re, bt_im = rep(cr), rep(ci), rep(bb_re), rep(bb_im)
    p_re, p_im = stack(pos, lambda t: t)
    l_re = ct_re * p_re - ct_im * p_im
    l_im = ct_re * p_im + ct_im * p_re
    n_re, n_im = stack(neg, lambda t: t)
    r_re = bt_re * n_re - bt_im * n_im
    r_im = bt_re * n_im + bt_im * n_re
    rk = lax.broadcasted_iota(jnp.int32, (n, n), 0) // SSM_GROUP
    ct = lax.broadcasted_iota(jnp.int32, (n, n), 1) // SSM_GROUP
    mt = jnp.where(ct >= rk, _dot_nt_hi(r_re, l_re) - _dot_nt_hi(r_im, l_im), 0.0)
    e_re, e_im = stack(pos, lambda t: SSM_CHUNK - 1 - t)
    pin_re = bt_re * e_re - bt_im * e_im
    pin_im = bt_re * e_im + bt_im * e_re
    q_re, q_im = stack(pos, lambda t: t + 1)
    qt_re = ct_re * q_re - ct_im * q_im
    qt_im = -(ct_re * q_im + ct_im * q_re)
    ad_re, ad_im = pos[SSM_CHUNK]
    return mt, pin_re, pin_im, qt_re, qt_im, ad_re, ad_im


_SSM_GEN_OUT = [(SSM_BLOCK, SSM_BLOCK)] + [(SSM_BLOCK, SSM_STATE)] * 4 + [(1, SSM_STATE)] * 2
_SSM_GEN_IN = [(1, SSM_STATE)] * 3 + [(SSM_GROUP, SSM_STATE)] * 4


def _gspec(shape2):
    return pl.BlockSpec((1,) + shape2, lambda g: (g, 0, 0))


def _ssm_operators(params):
    def body(*refs):
        ins, outs = refs[:7], refs[7:]
        for o_ref, val in zip(outs, _ssm_gen(*[r[0] for r in ins])):
            o_ref[0] = val

    return pl.pallas_call(
        body, grid=(SSM_GROUPS,), in_specs=[_gspec(s) for s in _SSM_GEN_IN], out_specs=[_gspec(s) for s in _SSM_GEN_OUT],
        out_shape=[S((SSM_GROUPS,) + s, F32) for s in _SSM_GEN_OUT], name="ssm_operators",
        compiler_params=_cp(("parallel",)))(*params)


def _ssm_operators_bwd(params, cots, after=()):
    def body(*refs):
        ins, cts, outs = refs[:7], refs[7:14], refs[14 + len(after):]
        _, vjp = jax.vjp(_ssm_gen, *[r[0] for r in ins])
        grads = list(vjp(tuple(r[0] for r in cts)))
        grads[2] = jnp.broadcast_to(jnp.sum(grads[2], axis=-1, keepdims=True), (1, SSM_STATE))
        for o_ref, val in zip(outs, grads):
            o_ref[0] = val

    return pl.pallas_call(
        body, grid=(SSM_GROUPS,),
        in_specs=[_gspec(s) for s in _SSM_GEN_IN] + [_gspec(s) for s in _SSM_GEN_OUT]
        + [pl.BlockSpec(memory_space=pl.ANY)] * len(after),
        out_specs=[_gspec(s) for s in _SSM_GEN_IN], out_shape=[S((SSM_GROUPS,) + s, F32) for s in _SSM_GEN_IN],
        name="ssm_operators_bwd", compiler_params=_cp(("parallel",)))(*params, *cots, *after)


def _shift_rows(x, sh, row, up):
    n = x.shape[0]
    if up:
        return jnp.where(row < n - sh, pltpu.roll(x, n - sh, axis=0), 0.0)
    return jnp.where(row >= sh, pltpu.roll(x, sh, axis=0), 0.0)


def _carry_scan(sr, si, ar, ai, row, up):
    n = sr.shape[0]
    sh = 1
    while sh < n:
        tr, ti = _shift_rows(sr, sh, row, up), _shift_rows(si, sh, row, up)
        sr, si = sr + ar * tr - ai * ti, si + ar * ti + ai * tr
        ar, ai = ar * ar - ai * ai, 2.0 * ar * ai
        sh *= 2
    return sr, si


def _ssm_states(u, p_re, p_im, ar, ai, row):
    lre = _dot(u, p_re.astype(BF16))
    lim = _dot(u, p_im.astype(BF16))
    sr, si = _carry_scan(lre, lim, ar, ai, row, False)
    return _shift_rows(sr, 1, row, False), _shift_rows(si, 1, row, False)


SSM_GPB = 128 // SSM_GROUP


def _step_rows(ref, nc):
    return [ref[pl.ds(k, nc, stride=SSM_CHUNK), :].astype(BF16) for k in range(SSM_CHUNK)]


def _group_of(rows, j):
    return jnp.concatenate([r[:, j * SSM_GROUP:(j + 1) * SSM_GROUP] for r in rows], axis=1)


def _store_groups(o_ref, per_group, nc):
    for k in range(SSM_CHUNK):
        o_ref[pl.ds(k, nc, stride=SSM_CHUNK), :] = jnp.concatenate(
            [y[:, k * SSM_GROUP:(k + 1) * SSM_GROUP] for y in per_group], axis=1)


def _bspec(shape2):
    return pl.BlockSpec((SSM_GPB,) + shape2, lambda j: (j, 0, 0))


def _ssm_core_fwd(proj, ops):
    t = proj.shape[0]
    nc = t // SSM_CHUNK

    def body(u_ref, mt_ref, pr_ref, pi_ref, qr_ref, qi_ref, ar_ref, ai_ref, y_ref):
        rows = _step_rows(u_ref, nc)
        row = lax.broadcasted_iota(jnp.int32, (nc, SSM_STATE), 0)
        ys = []
        for j in range(SSM_GPB):
            u = _group_of(rows, j)
            s_re, s_im = _ssm_states(u, pr_ref[j], pi_ref[j], ar_ref[j], ai_ref[j], row)
            ys.append(_dot(u, mt_ref[j].astype(BF16)) + _dot_nt(s_re.astype(BF16), qr_ref[j].astype(BF16))
                      + _dot_nt(s_im.astype(BF16), qi_ref[j].astype(BF16)))
        _store_groups(y_ref, ys, nc)

    blk = pl.BlockSpec((t, 128), lambda j: (0, j))
    return pl.pallas_call(
        body, grid=(SSM_GROUPS // SSM_GPB,), in_specs=[blk] + [_bspec(s) for s in _SSM_GEN_OUT], out_specs=blk,
        out_shape=S((t, 1024), F32), name="ssm_core_fwd", compiler_params=_cp(("parallel",)))(proj, *ops)


def _ssm_core_bwd(proj, dy, ops):
    t = proj.shape[0]
    nc = t // SSM_CHUNK

    def body(u_ref, dy_ref, mt_ref, pr_ref, pi_ref, qr_ref, qi_ref, ar_ref, ai_ref,
             du_ref, dmt_ref, dpr_ref, dpi_ref, dqr_ref, dqi_ref, dar_ref, dai_ref):
        u_rows = _step_rows(u_ref, nc)
        dy_rows = _step_rows(dy_ref, nc)
        row = lax.broadcasted_iota(jnp.int32, (nc, SSM_STATE), 0)
        dus = []
        for j in range(SSM_GPB):
            u = _group_of(u_rows, j)
            dyj = _group_of(dy_rows, j)
            ar, ai = ar_ref[j], ai_ref[j]
            s_re, s_im = _ssm_states(u, pr_ref[j], pi_ref[j], ar, ai, row)
            g_re = _shift_rows(_dot(dyj, qr_ref[j].astype(BF16)), 1, row, True)
            g_im = _shift_rows(_dot(dyj, qi_ref[j].astype(BF16)), 1, row, True)
            l_re, l_im = _carry_scan(g_re, g_im, ar, -ai, row, True)
            lrb, lib = l_re.astype(BF16), l_im.astype(BF16)
            dus.append(_dot_nt(dyj, mt_ref[j].astype(BF16)) + _dot_nt(lrb, pr_ref[j].astype(BF16))
                       + _dot_nt(lib, pi_ref[j].astype(BF16)))
            dmt_ref[j] = _dot_tn(u, dyj)
            dpr_ref[j] = _dot_tn(u, lrb)
            dpi_ref[j] = _dot_tn(u, lib)
            dqr_ref[j] = _dot_tn(dyj, s_re.astype(BF16))
            dqi_ref[j] = _dot_tn(dyj, s_im.astype(BF16))
            dar_ref[j] = jnp.sum(l_re * s_re + l_im * s_im, axis=0, keepdims=True)
            dai_ref[j] = jnp.sum(l_im * s_re - l_re * s_im, axis=0, keepdims=True)
        _store_groups(du_ref, dus, nc)

    blk = pl.BlockSpec((t, 128), lambda j: (0, j))
    return pl.pallas_call(
        body, grid=(SSM_GROUPS // SSM_GPB,), in_specs=[blk, blk] + [_bspec(s) for s in _SSM_GEN_OUT],
        out_specs=[blk] + [_bspec(s) for s in _SSM_GEN_OUT],
        out_shape=[S((t, 1024), F32)] + [S((SSM_GROUPS,) + s, F32) for s in _SSM_GEN_OUT],
        name="ssm_core_bwd", compiler_params=_cp(("parallel",)))(proj, dy, *ops)


def _s5_act(y_core, proj, d_skip):
    t = y_core.shape[0]
    tr = 256

    def body(y_ref, u_ref, d_ref, o_ref):
        o_ref[...] = _gelu(y_ref[...] + d_ref[...] * u_ref[...])

    return pl.pallas_call(body, grid=(t // tr,), in_specs=[_row_spec(tr, 1024), _row_spec(tr, 1024), _full_spec((1, 1024))],
                          out_specs=_row_spec(tr, 1024), out_shape=S((t, 1024), F32), name="s5_act",
                          compiler_params=_cp(("parallel",)))(y_core, proj, d_skip)


def _s5_gate(yg, tg, b_glu, proj):
    t = yg.shape[0]
    tr = 256

    def body(y_ref, t_ref, b_ref, z_ref, o_ref):
        o_ref[...] = (y_ref[...] * jax.nn.sigmoid(t_ref[...] + b_ref[...]) * _silu(z_ref[...])).astype(BF16)

    return pl.pallas_call(
        body, grid=(t // tr,),
        in_specs=[_row_spec(tr, 1024), _row_spec(tr, 1024), _full_spec((1, 1024)), pl.BlockSpec((tr, 1024), lambda i: (i, 1))],
        out_specs=_row_spec(tr, 1024), out_shape=S((t, 1024), BF16), name="s5_gate",
        compiler_params=_cp(("parallel",)))(yg, tg, b_glu, proj)


def _s5_gate_bwd(dp, yg, tg, b_glu, proj):
    t = yg.shape[0]
    tr = 256

    def body(dp_ref, y_ref, t_ref, b_ref, z_ref, dt_ref, dy_ref, dz_ref, db_ref):
        @pl.when(pl.program_id(0) == 0)
        def _():
            db_ref[...] = jnp.zeros_like(db_ref)

        z = z_ref[...]
        yv = y_ref[...]
        sg = jax.nn.sigmoid(t_ref[...] + b_ref[...])
        dpv = dp_ref[...].astype(F32)
        do = dpv * _silu(z)
        dz_ref[...] = (dpv * yv * sg * _silu_grad(z)).astype(BF16)
        dy_ref[...] = do * sg
        dtv = do * yv * sg * (1.0 - sg)
        dt_ref[...] = dtv.astype(BF16)
        db_ref[...] += jnp.sum(dtv, axis=0, keepdims=True)

    return pl.pallas_call(
        body, grid=(t // tr,),
        in_specs=[_row_spec(tr, 1024), _row_spec(tr, 1024), _row_spec(tr, 1024), _full_spec((1, 1024)),
                  pl.BlockSpec((tr, 1024), lambda i: (i, 1))],
        out_specs=[_row_spec(tr, 1024), _row_spec(tr, 1024), _row_spec(tr, 1024), _full_spec((1, 1024))],
        out_shape=[S((t, 1024), BF16), S((t, 1024), F32), S((t, 1024), BF16), S((1, 1024), F32)],
        name="s5_gate_bwd", compiler_params=_cp(("arbitrary",)))(dp, yg, tg, b_glu, proj)


def _s5_act_bwd(dyg1, dyg2, y_core, proj, d_skip):
    t = y_core.shape[0]
    tr = 256

    def body(a_ref, b_ref, y_ref, u_ref, d_ref, dy_ref, du_ref, dd_ref):
        @pl.when(pl.program_id(0) == 0)
        def _():
            dd_ref[...] = jnp.zeros_like(dd_ref)

        u = u_ref[...]
        dv = (a_ref[...] + b_ref[...]) * _gelu_grad(y_ref[...] + d_ref[...] * u)
        dy_ref[...] = dv
        du_ref[...] = dv * d_ref[...]
        dd_ref[...] += jnp.sum(dv * u, axis=0, keepdims=True)

    return pl.pallas_call(
        body, grid=(t // tr,),
        in_specs=[_row_spec(tr, 1024)] * 4 + [_full_spec((1, 1024))],
        out_specs=[_row_spec(tr, 1024), _row_spec(tr, 1024), _full_spec((1, 1024))],
        out_shape=[S((t, 1024), F32), S((t, 1024), F32), S((1, 1024), F32)],
        name="s5_act_bwd", compiler_params=_cp(("arbitrary",)))(dyg1, dyg2, y_core, proj, d_skip)


def _s5_dproj(du_skip, du_core, dz):
    t = du_skip.shape[0]
    tr = 256

    def body(a_ref, b_ref, z_ref, o_ref):
        o_ref[:, 0:1024] = (a_ref[...] + b_ref[...]).astype(BF16)
        o_ref[:, 1024:2048] = z_ref[...]

    return pl.pallas_call(body, grid=(t // tr,), in_specs=[_row_spec(tr, 1024)] * 3, out_specs=_row_spec(tr, 2048),
                          out_shape=S((t, 2048), BF16), name="s5_dproj",
                          compiler_params=_cp(("parallel",)))(du_skip, du_core, dz)


def _s5_params(lam_re, lam_im, log_dt, b_re, b_im, c_re, c_im):
    g = SSM_GROUPS
    return (lam_re.reshape(g, 1, SSM_STATE), lam_im.reshape(g, 1, SSM_STATE),
            jnp.broadcast_to(log_dt.reshape(g, 1, 1), (g, 1, SSM_STATE)),
            b_re.transpose(0, 2, 1), b_im.transpose(0, 2, 1), c_re, c_im)


def _s5_fwd(proj, params, d_skip, w_glu, b_glu):
    ops = _ssm_operators(params)
    y_core = _ssm_core_fwd(proj, ops)
    yg = _s5_act(y_core, proj, d_skip)
    tg = _mm(yg, w_glu, name="s5_glu")
    return _s5_gate(yg, tg, b_glu, proj), (ops, y_core, yg, tg)


def _s5_bwd(proj, res, dp, params, d_skip, w_glu, b_glu):
    ops, y_core, yg, tg = res
    dtg, dyg1, dz, db_glu = _s5_gate_bwd(dp, yg, tg, b_glu, proj)
    dyg2 = _mm(dtg, w_glu, tb=True, name="s5_glu_dx")
    dw_glu = _mm(yg, dtg, ta=True, out_dtype=BF16, name="s5_glu_dw")
    dy_core, du_skip, dd = _s5_act_bwd(dyg1, dyg2, y_core, proj, d_skip)
    outs = _ssm_core_bwd(proj, dy_core, ops)
    dproj = _s5_dproj(du_skip, outs[0], dz)
    return dproj, outs[1:], dd, dw_glu, db_glu


def _s5_param_grads(params, cots, after=()):
    glr, gli, gdt, gbtr, gbti, gcr, gci = _ssm_operators_bwd(params, cots, after)
    g = SSM_GROUPS
    return (glr.reshape(g, SSM_STATE), gli.reshape(g, SSM_STATE), gdt[:, 0, 0].reshape(1, g),
            gbtr.transpose(0, 2, 1), gbti.transpose(0, 2, 1), gcr, gci)


def _perm_uq(w):
    w3 = w.reshape(w.shape[0], MLA_HEADS, MLA_NOPE + MLA_ROPE)
    return jnp.concatenate([w3[:, :, :64].reshape(-1, 1024), w3[:, :, 64:80].reshape(-1, 256),
                            w3[:, :, 80:96].reshape(-1, 256)], axis=1)


def _unperm_uq(w):
    r = w.shape[0]
    return jnp.concatenate([w[:, :1024].reshape(r, 16, 64), w[:, 1024:1280].reshape(r, 16, 16),
                            w[:, 1280:1536].reshape(r, 16, 16)], axis=2).reshape(r, 1536)


def _perm_ukv(w):
    w3 = w.reshape(w.shape[0], MLA_HEADS, 128)
    return jnp.concatenate([w3[:, :, :64].reshape(-1, 1024), w3[:, :, 64:].reshape(-1, 1024)], axis=1)


def _unperm_ukv(w):
    r = w.shape[0]
    return jnp.concatenate([w[:, :1024].reshape(r, 16, 64), w[:, 1024:].reshape(r, 16, 64)], axis=2).reshape(r, 2048)


def _mla_fwd(proj, gq, gkv, wuq, wukv, cos, sin):
    cqn, ckvn = _mla_norms(proj, gq, gkv)
    qf = _mm(cqn, wuq, name="mla_uq")
    kvf = _mm(ckvn, wukv, name="mla_ukv")
    qh, kh = _mla_pack(qf, kvf, proj, cos, sin)
    o, pout, lse = _mla_attn_fwd(qh, kh, kvf, proj)
    return pout, (cqn, ckvn, kvf, qh, kh, o, lse)


def _mla_bwd(proj, res, dpout, gq, gkv, wuq, wukv, cos, sin):
    cqn, ckvn, kvf, qh, kh, o, lse = res
    dqh, dkh, dv, dz = _mla_attn_bwd(qh, kh, kvf, proj, o, dpout, lse)
    dqf, dkvf, dkr = _mla_unpack(dqh, dkh, dv, cos, sin)
    dcqn = _mm(dqf, wuq, tb=True, name="mla_uq_dx")
    dwuq = _mm(cqn, dqf, ta=True, out_dtype=BF16, name="mla_uq_dw")
    dckvn = _mm(dkvf, wukv, tb=True, name="mla_ukv_dx")
    dwukv = _mm(ckvn, dkvf, ta=True, out_dtype=BF16, name="mla_ukv_dw")
    dproj, dgq, dgkv = _mla_norms_bwd(proj, dcqn, dckvn, dkr, dz, gq, gkv)
    return dproj, dgq, dgkv, dwuq, dwukv


_ANY = pl.BlockSpec(memory_space=pl.ANY)


def _chip_peers():
    x, y, c = lax.axis_index("x"), lax.axis_index("y"), lax.axis_index("c")
    return 2 * x + y, c, [(1 - x, y), (x, 1 - y), (1 - x, 1 - y)]


def _exchange_chips(src_of, dst_of, send_sems, recv_sems, loc_sem):
    me, c, peers = _chip_peers()
    loc = pltpu.make_async_copy(src_of(me), dst_of(me), loc_sem)
    loc.start()

    def copy(k, px, py, src_chip, dst_chip):
        return pltpu.make_async_remote_copy(src_ref=src_of(src_chip), dst_ref=dst_of(dst_chip), send_sem=send_sems.at[k],
                                            recv_sem=recv_sems.at[k], device_id=(px, py, c), device_id_type=MESH)

    sends = [copy(k, px, py, 2 * px + py, me) for k, (px, py) in enumerate(peers)]
    for cp in sends:
        cp.start()
    for k, (px, py) in enumerate(peers):
        copy(k, px, py, me, 2 * px + py).wait_recv()
    for cp in sends:
        cp.wait_send()
    loc.wait()


_EXCH_SCRATCH = [pltpu.SemaphoreType.DMA((3,)), pltpu.SemaphoreType.DMA((3,)), pltpu.SemaphoreType.DMA(())]


def _all_gather_chips(xs, name):
    def body(x_ref, o_ref, send_sems, recv_sems, loc_sem):
        _exchange_chips(lambda s: x_ref, lambda s: o_ref.at[s], send_sems, recv_sems, loc_sem)

    return pl.pallas_call(body, in_specs=[_ANY], out_specs=_ANY, out_shape=S((N_CHIPS,) + xs.shape, xs.dtype),
                          scratch_shapes=_EXCH_SCRATCH, name=name)(xs)


def _sibling_swap(ps, name):
    n = len(ps)

    def body(*refs):
        p_refs, q_refs, send_sems, recv_sems = refs[:n], refs[n:2 * n], refs[2 * n], refs[2 * n + 1]
        x, y, c = lax.axis_index("x"), lax.axis_index("y"), lax.axis_index("c")
        cps = [pltpu.make_async_remote_copy(src_ref=p_refs[i], dst_ref=q_refs[i], send_sem=send_sems.at[i],
                                            recv_sem=recv_sems.at[i], device_id=(x, y, 1 - c), device_id_type=MESH)
               for i in range(n)]
        for cp in cps:
            cp.start()
        for cp in cps:
            cp.wait()

    return pl.pallas_call(body, in_specs=[_ANY] * n, out_specs=[_ANY] * n, out_shape=[S(p.shape, p.dtype) for p in ps],
                          scratch_shapes=[pltpu.SemaphoreType.DMA((n,)), pltpu.SemaphoreType.DMA((n,))], name=name)(*ps)


def _quarter_spec(tr, r, c, dim):
    if dim is None:
        return pl.BlockSpec((1, tr, c), lambda i, me: (me[0], i, 0))
    if dim == 0:
        return pl.BlockSpec((tr, c), lambda i, me: (me[0] * (r // tr) + i, 0))
    return pl.BlockSpec((tr, c), lambda i, me: (i, me[0]))


def _whole_shape(r, c, dim):
    return (N_CHIPS, r, c) if dim is None else ((N_CHIPS * r, c) if dim == 0 else (r, N_CHIPS * c))


def _row_tile(r):
    return 256 if r % 256 == 0 else (128 if r % 128 == 0 else r)


def _cast_place(me, x, dim, dtype, name):
    r, c = x.shape
    tr = _row_tile(r)

    def body(me_ref, x_ref, o_ref):
        o_ref[...] = x_ref[...].astype(dtype).reshape(o_ref.shape)

    return pl.pallas_call(
        body, out_shape=S(_whole_shape(r, c, dim), dtype), name=name,
        grid_spec=pltpu.PrefetchScalarGridSpec(num_scalar_prefetch=1, grid=(r // tr,),
                                               in_specs=[pl.BlockSpec((tr, c), lambda i, me: (i, 0))],
                                               out_specs=_quarter_spec(tr, r, c, dim)),
        compiler_params=_cp(("parallel",)))(me, x)


def _sum_parts(me, g, land, dim, name):
    _, r, c = land.shape
    tr = _row_tile(r)

    def body(me_ref, g_ref, l_ref, o_ref):
        own = g_ref[...].astype(F32).reshape(tr, c)
        o_ref[...] = ((own + l_ref[0].astype(F32)) + l_ref[1].astype(F32)) + l_ref[2].astype(F32)

    return pl.pallas_call(
        body, out_shape=S((r, c), F32), name=name,
        grid_spec=pltpu.PrefetchScalarGridSpec(
            num_scalar_prefetch=1, grid=(r // tr,),
            in_specs=[_quarter_spec(tr, r, c, dim), pl.BlockSpec((3, tr, c), lambda i, me: (0, i, 0))],
            out_specs=pl.BlockSpec((tr, c), lambda i, me: (i, 0))),
        compiler_params=_cp(("parallel",)))(me, g, land)


_HBM = pl.BlockSpec(memory_space=pltpu.HBM)
_SEM = pl.BlockSpec(memory_space=pltpu.SEMAPHORE)
_EFFECT = pltpu.SideEffectType.DATAFLOW_SIDE_EFFECTING


def _quarter(ref, s, dim):
    n = ref.shape[dim] // N_CHIPS
    start = pl.multiple_of(s * n, n)
    return ref.at[pl.ds(start, n), :] if dim == 0 else ref.at[:, pl.ds(start, n)]


def _part(ref, s, dim):
    return ref.at[s] if dim is None else _quarter(ref, s, dim)


def _gather_plan(i, dim):
    return dict(src=i, dst=i, send_src=lambda ref, me, peer, k: _part(ref, me, dim),
                send_dst=lambda ref, me, peer, k: _part(ref, me, dim), recv_dst=lambda ref, me, peer, k: _part(ref, peer, dim))


def _scatter_plan(i, j, dim):
    return dict(src=i, dst=j, send_src=lambda ref, me, peer, k: _part(ref, peer, dim),
                send_dst=lambda ref, me, peer, k: ref.at[k], recv_dst=lambda ref, me, peer, k: ref.at[k])


def _hbm(a):
    return pltpu.with_memory_space_constraint(a, pltpu.HBM)


def _remote_copies(refs, send_sems, recv_sems, plans):
    me, c, peers = _chip_peers()
    sends, recvs = [], []
    for i, p in enumerate(plans):
        for k, (px, py) in enumerate(peers):
            peer = 2 * px + py
            sems = dict(send_sem=send_sems.at[3 * i + k], recv_sem=recv_sems.at[3 * i + k], device_id=(px, py, c),
                        device_id_type=MESH)
            src = p['send_src'](refs[p['src']], me, peer, k)
            sends.append(pltpu.make_async_remote_copy(src_ref=src, dst_ref=p['send_dst'](refs[p['dst']], me, peer, k), **sems))
            recvs.append(pltpu.make_async_remote_copy(src_ref=src, dst_ref=p['recv_dst'](refs[p['dst']], me, peer, k), **sems))
    return sends, recvs


def _exchange_start(arrays, plans, name, after=()):
    n, m, extra = len(arrays), len(plans), len(after)

    def body(*refs):
        send_sems, recv_sems, token = refs[n + extra], refs[n + extra + 1], refs[-1]
        sends, _ = _remote_copies(refs[:n], send_sems, recv_sems, plans)
        for cp in sends:
            cp.start()
        token[...] = jnp.zeros_like(token)

    outs = pl.pallas_call(
        body, name=name,
        out_shape=(pltpu.SemaphoreType.DMA((3 * m,)), pltpu.SemaphoreType.DMA((3 * m,)),
                   *[pltpu.HBM(a.shape, a.dtype) for a in arrays], S((8, 128), F32)),
        in_specs=[_HBM] * n + [_ANY] * extra, out_specs=(_SEM, _SEM, *[_HBM] * n, pl.BlockSpec(memory_space=pltpu.VMEM)),
        input_output_aliases={i: 2 + i for i in range(n)},
        compiler_params=pltpu.CompilerParams(has_side_effects=_EFFECT))(*[_hbm(a) for a in arrays], *after)
    return outs[0], outs[1], outs[2:2 + n], outs[-1]


def _exchange_wait(started, plans, after, name):
    send_sems, recv_sems, arrays, _ = started
    n = len(arrays)

    def body(*refs):
        sends, recvs = _remote_copies(refs[:n], refs[n], refs[n + 1], plans)
        for cp in sends:
            cp.wait_send()
        for cp in recvs:
            cp.wait_recv()

    return pl.pallas_call(
        body, name=name, out_shape=tuple(pltpu.HBM(a.shape, a.dtype) for a in arrays),
        in_specs=[_HBM] * n + [_SEM, _SEM, _ANY], out_specs=tuple([_HBM] * n),
        input_output_aliases={i: i for i in range(n)},
        compiler_params=pltpu.CompilerParams(has_side_effects=_EFFECT))(*arrays, send_sems, recv_sems, after)


def _add2(p, q):
    r, c = p.shape
    tr = 128 if r % 128 == 0 else r

    def body(p_ref, q_ref, o_ref):
        o_ref[...] = p_ref[...] + q_ref[...]

    return pl.pallas_call(body, grid=(r // tr,), in_specs=[_row_spec(tr, c), _row_spec(tr, c)], out_specs=_row_spec(tr, c),
                          out_shape=S((r, c), F32), name="add_cores", compiler_params=_cp(("parallel",)))(p, q)


def _adamw(w, g, m, v, name, g2=None):
    r, c = w.shape
    tr = 256 if r % 256 == 0 else (128 if r % 128 == 0 else r)
    c1 = 1.0 / (1.0 - ADAM_B1 ** ADAM_STEP)
    c2 = 1.0 / (1.0 - ADAM_B2 ** ADAM_STEP)
    gs = [g] if g2 is None else [g, g2]

    def body(*refs):
        w_ref, m_ref, v_ref = refs[0], refs[1], refs[2]
        g_refs = refs[3:3 + len(gs)]
        go_ref, d_ref, nm_ref, nv_ref = refs[3 + len(gs):]
        gv = g_refs[0][...]
        if len(gs) == 2:
            gv = gv + g_refs[1][...]
        go_ref[...] = gv
        nm = ADAM_B1 * m_ref[...] + (1.0 - ADAM_B1) * gv
        nv = ADAM_B2 * v_ref[...] + (1.0 - ADAM_B2) * (gv * gv)
        nm_ref[...] = nm
        nv_ref[...] = nv
        d_ref[...] = -ADAM_LR * ((nm * c1) / (jnp.sqrt(nv * c2) + ADAM_EPS) + ADAM_WD * w_ref[...])

    return pl.pallas_call(body, grid=(r // tr,), in_specs=[_row_spec(tr, c)] * (3 + len(gs)),
                          out_specs=[_row_spec(tr, c)] * 4, out_shape=[S((r, c), F32)] * 4, name=name,
                          compiler_params=_cp(("parallel",)))(w, m, v, *gs)


_WEIGHTS = ['pre_norm', 'post_norm', 'rel_bias', 'a_w_in', 'a_lam_re', 'a_lam_im', 'a_log_dt', 'a_b_re', 'a_b_im',
            'a_c_re', 'a_c_im', 'a_d', 'a_w_glu', 'a_b_glu', 'a_w_out', 'b_w_in', 'b_sinks', 'b_w_out', 'c_w_in',
            'c_q_norm', 'c_kv_norm', 'c_w_uq', 'c_w_ukv', 'c_w_out', 'd_w_in', 'd_ln_g', 'd_ln_b', 'd_w_s', 'd_b_s',
            'd_w_out']
_BIG = [('a_w_in', 1024, 2048, 1), ('a_w_glu', 1024, 1024, 0), ('a_w_out', 1024, 1024, 0), ('b_w_in', 1024, 2304, 1),
        ('b_w_out', 1024, 1024, 0), ('c_w_in', 1024, 2080, 1), ('c_w_uq', 768, 1536, 1), ('c_w_ukv', 256, 2048, 1),
        ('c_w_out', 1024, 1024, 0), ('d_w_in', 1024, 3072, 1), ('d_w_out', 1024, 1024, 0)]
_SHARDED_VECS = [('c_q_norm', 768), ('c_kv_norm', 256), ('d_ln_g', 1024), ('d_ln_b', 1024)]
_REPLICATED = [n for n in _WEIGHTS if n not in [b[0] for b in _BIG] and n not in [s[0] for s in _SHARDED_VECS]]
_LANES = 1024
_SMALL_Q_ROWS = 136


def kernel(x, pre_norm, post_norm, rel_bias, a_w_in, a_lam_re, a_lam_im, a_log_dt, a_b_re, a_b_im, a_c_re, a_c_im, a_d, a_w_glu, a_b_glu, a_w_out, b_w_in, b_sinks, b_w_out, c_w_in, c_q_norm, c_kv_norm, c_w_uq, c_w_ukv, c_w_out, d_w_in, d_ln_g, d_ln_b, d_w_s, d_b_s, d_w_out, loss_target, m_pre_norm, m_post_norm, m_rel_bias, m_a_w_in, m_a_lam_re, m_a_lam_im, m_a_log_dt, m_a_b_re, m_a_b_im, m_a_c_re, m_a_c_im, m_a_d, m_a_w_glu, m_a_b_glu, m_a_w_out, m_b_w_in, m_b_sinks, m_b_w_out, m_c_w_in, m_c_q_norm, m_c_kv_norm, m_c_w_uq, m_c_w_ukv, m_c_w_out, m_d_w_in, m_d_ln_g, m_d_ln_b, m_d_w_s, m_d_b_s, m_d_w_out, v_pre_norm, v_post_norm, v_rel_bias, v_a_w_in, v_a_lam_re, v_a_lam_im, v_a_log_dt, v_a_b_re, v_a_b_im, v_a_c_re, v_a_c_im, v_a_d, v_a_w_glu, v_a_b_glu, v_a_w_out, v_b_w_in, v_b_sinks, v_b_w_out, v_c_w_in, v_c_q_norm, v_c_kv_norm, v_c_w_uq, v_c_w_ukv, v_c_w_out, v_d_w_in, v_d_ln_g, v_d_ln_b, v_d_w_s, v_d_b_s, v_d_w_out):
    arg = dict(locals())
    me = 2 * lax.axis_index("x") + lax.axis_index("y")
    xin = x[0]
    t = xin.shape[0]

    big = {n: (r, c, dim) for n, r, c, dim in _BIG}

    me1 = me.reshape(1).astype(jnp.int32)

    def gather_start(names, tag, after):
        wholes, plans = [], []
        for i, n in enumerate(names):
            if n in big:
                r, c, dim = big[n]
                src, dtype = arg[n][0], BF16
                if dim == 1 and (c // N_CHIPS) % 128:
                    dim = None
            else:
                src, dim, dtype = arg[n], None, F32
            wholes.append(_cast_place(me1, src, dim, dtype, "place_" + n))
            plans.append(_gather_plan(i, dim))
        return _exchange_start(wholes, plans, "gather_start_" + tag, after=after), plans

    def gather_wait(names, started, after, tag):
        st, plans = started
        out = {}
        for n, a in zip(names, _exchange_wait(st, plans, after, "gather_wait_" + tag)):
            if n in big:
                r, c, dim = big[n]
                out[n] = a if a.ndim == 2 else a.transpose(1, 0, 2).reshape(r, c)
            else:
                out[n] = a.reshape(1, -1)
        return out

    groups = {'a1': ['a_w_in'], 'a2': ['a_w_glu', 'a_w_out'], 'b': ['b_w_in', 'b_w_out'],
              'c': ['c_w_in', 'c_w_uq', 'c_w_ukv', 'c_w_out', 'c_q_norm', 'c_kv_norm'],
              'd': ['d_w_in', 'd_w_out', 'd_ln_g', 'd_ln_b']}
    started, tokens = {}, []
    for tag, names in groups.items():
        started[tag] = gather_start(names, tag, tokens)
        tokens = [started[tag][0][3]]
    wt = gather_wait(groups['a1'], started['a1'], xin, 'a1')
    cos, sin = _rope_tables(t)
    bucket = jnp.asarray(_t5_bucket_table())
    s5p = _s5_params(a_lam_re[0], a_lam_im[0], a_log_dt[0], a_b_re[0], a_b_im[0], a_c_re[0], a_c_im[0])
    bst = d_b_s[0].T

    x0 = xin
    h0 = _prenorm(x0, pre_norm[0:1], "pre0", after=tokens)
    proj0 = _mm(h0, wt['a_w_in'], name="a_in")
    wt.update(gather_wait(groups['a2'], started['a2'], proj0, 'a2'))
    p0, res0 = _s5_fwd(proj0, s5p, a_d, wt['a_w_glu'], a_b_glu)
    y0 = _mm(p0, wt['a_w_out'], name="a_out")
    x1 = _postnorm(x0, y0, post_norm[0:1], "post0")

    wt.update(gather_wait(groups['b'], started['b'], x1, 'b'))
    h1 = _prenorm(x1, pre_norm[1:2], "pre1")
    proj1 = _mm(h1, wt['b_w_in'], name="b_in")
    bias = _swa_bias(rel_bias, bucket)
    p1 = _swa_fwd(proj1, bias, b_sinks)
    y1 = _mm(p1, wt['b_w_out'], name="b_out")
    x2 = _postnorm(x1, y1, post_norm[1:2], "post1")

    wt.update(gather_wait(groups['c'], started['c'], x2, 'c'))
    w_c_in = jnp.concatenate([wt['c_w_in'][:, :1056], jnp.zeros((1024, 96), BF16), wt['c_w_in'][:, 1056:],
                              jnp.zeros((1024, MLA_PROJ - MLA_Z1), BF16)], axis=1)
    w_uq = _perm_uq(wt['c_w_uq'])
    w_ukv = _perm_ukv(wt['c_w_ukv'])
    h2 = _prenorm(x2, pre_norm[2:3], "pre2")
    proj2 = _mm(h2, w_c_in, name="c_in")
    p2, res2 = _mla_fwd(proj2, wt['c_q_norm'], wt['c_kv_norm'], w_uq, w_ukv, cos, sin)
    y2 = _mm(p2, wt['c_w_out'], name="c_out")
    x3 = _postnorm(x2, y2, post_norm[2:3], "post2")

    wt.update(gather_wait(groups['d'], started['d'], x3, 'd'))
    vec = wt
    h3 = _prenorm(x3, pre_norm[3:4], "pre3")
    proj3 = _mm(h3, wt['d_w_in'], name="d_in")
    p3 = _sgu_fwd(proj3, vec['d_ln_g'], vec['d_ln_b'], d_w_s[0], bst)
    y3 = _mm(p3, wt['d_w_out'], name="d_out")
    x4 = _postnorm(x3, y3, post_norm[3:4], "post3")

    loss_part, dx = _loss_head(x4, loss_target[0])
    loss = lax.psum(loss_part[0, 0], ("x", "y", "c"))

    gr = {}
    d_pre, d_post = [None] * 4, [None] * 4
    scattered = []

    def scatter_start(items, tag):
        srcs = [a for _, a, _ in items]
        n = len(items)
        plans = [_scatter_plan(i, n + i, dim) for i, (_, _, dim) in enumerate(items)]
        lands = []
        for _, a, dim in items:
            part = a.shape[1:] if dim is None else tuple(s // N_CHIPS if i == dim else s for i, s in enumerate(a.shape))
            lands.append(lax.empty((3,) + part, a.dtype))
        st = _exchange_start(srcs + lands, plans, "scatter_start_" + tag)
        scattered.append((st, plans, items, tag))
        return [st[3]]

    def stacked(g):
        r, c = g.shape
        return g.reshape(r, N_CHIPS, c // N_CHIPS).transpose(1, 0, 2)

    dy, d_post[3] = _postnorm_bwd(dx, y3, post_norm[3:4], "post3_bwd")
    dp = _mm(dy, wt['d_w_out'], tb=True, name="d_out_dx")
    g_out = _mm(p3, dy, ta=True, out_dtype=BF16, name="d_out_dw")
    dproj, gr['d_w_s'], dbst, gr['d_ln_g'], gr['d_ln_b'] = _sgu_bwd(proj3, dp, vec['d_ln_g'], vec['d_ln_b'], d_w_s[0], bst)
    gr['d_b_s'] = dbst.T
    dh = _mm(dproj, wt['d_w_in'], tb=True, name="d_in_dx")
    g_in = _mm(h3, dproj, ta=True, out_dtype=BF16, name="d_in_dw")
    tok = scatter_start([('d_w_in', g_in, 1), ('d_w_out', g_out, 0)], 'd')
    dx, d_pre[3] = _prenorm_bwd(dx, dh, x3, pre_norm[3:4], "pre3_bwd")

    dy, d_post[2] = _postnorm_bwd(dx, y2, post_norm[2:3], "post2_bwd", after=tok)
    dp = _mm(dy, wt['c_w_out'], tb=True, name="c_out_dx")
    g_out = _mm(p2, dy, ta=True, out_dtype=BF16, name="c_out_dw")
    dproj, gr['c_q_norm'], gr['c_kv_norm'], dwuq, dwukv = _mla_bwd(proj2, res2, dp, wt['c_q_norm'], wt['c_kv_norm'],
                                                                  w_uq, w_ukv, cos, sin)
    dh = _mm(dproj, w_c_in, tb=True, name="c_in_dx")
    dwc = _mm(h2, dproj, ta=True, out_dtype=BF16, name="c_in_dw")
    g_in = stacked(jnp.concatenate([dwc[:, :1056], dwc[:, MLA_Z0:MLA_Z1]], axis=1))
    tok = scatter_start([('c_w_in', g_in, None), ('c_w_uq', _unperm_uq(dwuq), 1), ('c_w_ukv', _unperm_ukv(dwukv), 1),
                         ('c_w_out', g_out, 0)], 'c')
    dx, d_pre[2] = _prenorm_bwd(dx, dh, x2, pre_norm[2:3], "pre2_bwd")

    dy, d_post[1] = _postnorm_bwd(dx, y1, post_norm[1:2], "post1_bwd", after=tok)
    dp = _mm(dy, wt['b_w_out'], tb=True, name="b_out_dx")
    g_out = _mm(p1, dy, ta=True, out_dtype=BF16, name="b_out_dw")
    dqz, dkv, dss, dsk = _swa_bwd(proj1, dp, bias, b_sinks)
    gr['rel_bias'], gr['b_sinks'] = _swa_dbias(dss, dsk, bucket)
    dproj = jnp.concatenate([dqz[:, :1024], dkv.astype(BF16), dqz[:, 1024:]], axis=1)
    dh = _mm(dproj, wt['b_w_in'], tb=True, name="b_in_dx")
    g_in = stacked(_mm(h1, dproj, ta=True, out_dtype=BF16, name="b_in_dw"))
    tok = scatter_start([('b_w_in', g_in, None), ('b_w_out', g_out, 0)], 'b')
    dx, d_pre[1] = _prenorm_bwd(dx, dh, x1, pre_norm[1:2], "pre1_bwd")

    dy, d_post[0] = _postnorm_bwd(dx, y0, post_norm[0:1], "post0_bwd", after=tok)
    dp = _mm(dy, wt['a_w_out'], tb=True, name="a_out_dx")
    g_out = _mm(p0, dy, ta=True, out_dtype=BF16, name="a_out_dw")
    dproj, s5cots, gr['a_d'], g_glu, gr['a_b_glu'] = _s5_bwd(proj0, res0, dp, s5p, a_d, wt['a_w_glu'], a_b_glu)
    tok = scatter_start([('a_w_glu', g_glu, 0), ('a_w_out', g_out, 0)], 'a2')
    g_in = _mm(h0, dproj, ta=True, out_dtype=BF16, name="a_in_dw", after=tok)
    tok = scatter_start([('a_w_in', g_in, 1)], 'a1')
    dh = _mm(dproj, wt['a_w_in'], tb=True, name="a_in_dx", after=tok)
    dx, d_pre[0] = _prenorm_bwd(dx, dh, x0, pre_norm[0:1], "pre0_bwd")
    s5g = _s5_param_grads(s5p, s5cots, after=tok)
    for n, g in zip(['a_lam_re', 'a_lam_im', 'a_log_dt', 'a_b_re', 'a_b_im', 'a_c_re', 'a_c_im'], s5g):
        gr[n] = g
    gr['pre_norm'] = jnp.concatenate(d_pre, axis=0)
    gr['post_norm'] = jnp.concatenate(d_post, axis=0)
    small_names = _REPLICATED + [n for n, _ in _SHARDED_VECS]
    small = jnp.concatenate([gr[n].reshape(-1) for n in small_names])
    small = jnp.pad(small, (0, N_CHIPS * _SMALL_Q_ROWS * _LANES - small.shape[0])).reshape(N_CHIPS, _SMALL_Q_ROWS, _LANES)
    scatter_start([('small', small, None)], 'a0')

    partial = {}
    for st, plans, items, tag in scattered:
        done = _exchange_wait(st, plans, small, "scatter_wait_" + tag)
        for i, (n, _, dim) in enumerate(items):
            partial[n] = _sum_parts(me1, done[i], done[len(items) + i], dim, "sum_parts_" + n)
    order = [n for n, _, _, _ in _BIG] + ['small']
    other = dict(zip(order, _sibling_swap([partial[n] for n in order], "swap_cores")))
    small_q = _add2(partial['small'], other['small'])
    small_all = _all_gather_chips(small_q, "gather_small_grads").reshape(-1)
    grads = {}
    off = 0
    for n in _REPLICATED:
        sz = math.prod(arg[n].shape)
        grads[n] = small_all[off:off + sz].reshape(arg[n].shape)
        off += sz
    for n, sz in _SHARDED_VECS:
        q = sz // N_CHIPS
        grads[n] = lax.dynamic_slice(small_all, (off + me * q,), (q,)).reshape(arg[n].shape)
        off += sz

    delta, new_m, new_v = {}, {}, {}
    for n, _, _, _ in _BIG:
        shp = arg[n].shape
        two = (shp[1], shp[2])
        outs = _adamw(arg[n].reshape(two), partial[n], arg['m_' + n].reshape(two), arg['v_' + n].reshape(two),
                      "adamw_" + n, g2=other[n])
        grads[n], delta[n], new_m[n], new_v[n] = [o.reshape(shp) for o in outs]

    for n in small_names:
        shp = arg[n].shape
        two = (math.prod(shp[:-1]), shp[-1])
        outs = _adamw(arg[n].reshape(two), grads[n].reshape(two), arg['m_' + n].reshape(two), arg['v_' + n].reshape(two),
                      "adamw_" + n)
        delta[n], new_m[n], new_v[n] = [o.reshape(shp) for o in outs[1:]]

    return (loss, dx.reshape(x.shape), *[grads[n] for n in _WEIGHTS], *[delta[n] for n in _WEIGHTS],
            *[new_m[n] for n in _WEIGHTS], *[new_v[n] for n in _WEIGHTS])
```

```python
import functools
import math

import numpy as np
import jax
import jax.numpy as jnp
from jax import lax
from jax.experimental import pallas as pl
from jax.experimental.pallas import tpu as pltpu

F32 = jnp.float32
BF16 = jnp.bfloat16
S = jax.ShapeDtypeStruct
MESH = pl.DeviceIdType.MESH

D_MODEL = 1024
EPS = 1e-6
NEG_INF = -1e30
N_CHIPS = 4

SSM_GROUPS = 64
SSM_GROUP = 16
SSM_STATE = 64
SSM_CHUNK = 16
SSM_BLOCK = SSM_CHUNK * SSM_GROUP

HEAD_DIM = 64
SWA_HEADS = 16
SWA_GROUP = 8
WINDOW = 128
REL_BUCKETS = 32
REL_MAX_DIST = 128

MLA_HEADS = 16
MLA_NOPE = 64
MLA_ROPE = 32
MLA_Q_RANK = 768
MLA_KV_RANK = 256
MLA_SCALE = (MLA_NOPE + MLA_ROPE) ** -0.5
MLA_BQ = 512
MLA_BK = 512
ROPE_BASE = 10000.0

SGU_CHUNK = 128
SGU_GROUPS = 16
SGU_GDIM = 64

ADAM_LR = 0.001
ADAM_B1 = 0.9
ADAM_B2 = 0.999
ADAM_EPS = 1e-08
ADAM_WD = 0.01
ADAM_STEP = 10

VMEM_BIG = 56 * 1024 * 1024


def _cp(sem, vmem=None):
    return pltpu.CompilerParams(dimension_semantics=sem, vmem_limit_bytes=vmem)


def _row_spec(tr, c):
    return pl.BlockSpec((tr, c), lambda i: (i, 0))


def _full_spec(shape):
    nd = len(shape)
    return pl.BlockSpec(shape, lambda *_: (0,) * nd)


def _gelu(x):
    c = 0.7978845608028654
    return 0.5 * x * (1.0 + jnp.tanh(c * (x + 0.044715 * x * x * x)))


def _gelu_grad(x):
    c = 0.7978845608028654
    t = jnp.tanh(c * (x + 0.044715 * x * x * x))
    return 0.5 * (1.0 + t) + 0.5 * x * (1.0 - t * t) * c * (1.0 + 3.0 * 0.044715 * x * x)


def _silu(z):
    return z * jax.nn.sigmoid(z)


def _silu_grad(z):
    s = jax.nn.sigmoid(z)
    return s * (1.0 + z * (1.0 - s))


def _dot(a, b):
    return lax.dot_general(a, b, (((1,), (0,)), ((), ())), preferred_element_type=F32)


def _dot_nt(a, b):
    return lax.dot_general(a, b, (((1,), (1,)), ((), ())), preferred_element_type=F32)


def _dot_tn(a, b):
    return lax.dot_general(a, b, (((0,), (0,)), ((), ())), preferred_element_type=F32)


def _pick(n, cap):
    best = 0
    for d in range(128, min(n, cap) + 1, 128):
        if n % d == 0:
            best = d
    if best < 256:
        return n
    return best


def _mm(a, b, *, ta=False, tb=False, out_dtype=F32, name, after=()):
    m, k = (a.shape[1], a.shape[0]) if ta else a.shape
    n = b.shape[0] if tb else b.shape[1]
    assert (b.shape[1] if tb else b.shape[0]) == k
    tm, tn, tk = _pick(m, 1024), _pick(n, 1024), _pick(k, 1024)
    if tn > 1024:
        tm = _pick(m, 256)
    if tk > 1024:
        tm, tn = _pick(m, 256), _pick(n, 256)
    nk = k // tk
    dims = (((0 if ta else 1,), (1 if tb else 0,)), ((), ()))

    def body(a_ref, b_ref, *rest):
        o_ref, acc_ref = rest[-2], rest[-1]
        kk = pl.program_id(2)

        @pl.when(kk == 0)
        def _():
            acc_ref[...] = jnp.zeros_like(acc_ref)

        acc_ref[...] += lax.dot_general(a_ref[...].astype(BF16), b_ref[...].astype(BF16), dims,
                                        preferred_element_type=F32)

        @pl.when(kk == nk - 1)
        def _():
            o_ref[...] = acc_ref[...].astype(out_dtype)

    a_spec = pl.BlockSpec((tk, tm), lambda i, j, kk: (kk, i)) if ta else pl.BlockSpec((tm, tk), lambda i, j, kk: (i, kk))
    b_spec = pl.BlockSpec((tn, tk), lambda i, j, kk: (j, kk)) if tb else pl.BlockSpec((tk, tn), lambda i, j, kk: (kk, j))
    return pl.pallas_call(
        body, grid=(m // tm, n // tn, nk), in_specs=[a_spec, b_spec] + [pl.BlockSpec(memory_space=pl.ANY)] * len(after),
        out_specs=pl.BlockSpec((tm, tn), lambda i, j, kk: (i, j)), out_shape=S((m, n), out_dtype),
        scratch_shapes=[pltpu.VMEM((tm, tn), F32)], name=name,
        compiler_params=_cp(("parallel", "parallel", "arbitrary"), VMEM_BIG))(a, b, *after)


def _prenorm(x, g, name, after=()):
    t, d = x.shape
    tr = 256

    def body(x_ref, g_ref, *rest):
        o_ref = rest[-1]
        xv = x_ref[...]
        r = lax.rsqrt(jnp.mean(xv * xv, axis=-1, keepdims=True) + EPS)
        o_ref[...] = (xv * r * g_ref[...]).astype(BF16)

    return pl.pallas_call(body, grid=(t // tr,),
                          in_specs=[_row_spec(tr, d), _full_spec((1, d))] + [pl.BlockSpec(memory_space=pl.ANY)] * len(after),
                          out_specs=_row_spec(tr, d), out_shape=S((t, d), BF16), name=name,
                          compiler_params=_cp(("parallel",)))(x, g, *after)


def _postnorm(x, y, g, name):
    t, d = x.shape
    tr = 256

    def body(x_ref, y_ref, g_ref, o_ref):
        yv = y_ref[...]
        r = lax.rsqrt(jnp.mean(yv * yv, axis=-1, keepdims=True) + EPS)
        o_ref[...] = x_ref[...] + yv * r * g_ref[...]

    return pl.pallas_call(body, grid=(t // tr,), in_specs=[_row_spec(tr, d), _row_spec(tr, d), _full_spec((1, d))],
                          out_specs=_row_spec(tr, d), out_shape=S((t, d), F32), name=name,
                          compiler_params=_cp(("parallel",)))(x, y, g)


def _rms_bwd_rows(dout, xin, g):
    r = lax.rsqrt(jnp.mean(xin * xin, axis=-1, keepdims=True) + EPS)
    xh = xin * r
    gd = dout * g
    dx = r * (gd - xh * jnp.mean(gd * xh, axis=-1, keepdims=True))
    return dx, jnp.sum(dout * xh, axis=0, keepdims=True)


def _postnorm_bwd(dxn, y, g, name, after=()):
    t, d = y.shape
    tr = 256

    def body(dx_ref, y_ref, g_ref, *rest):
        dy_ref, dg_ref = rest[-2], rest[-1]

        @pl.when(pl.program_id(0) == 0)
        def _():
            dg_ref[...] = jnp.zeros_like(dg_ref)

        dy, dg = _rms_bwd_rows(dx_ref[...], y_ref[...], g_ref[...])
        dy_ref[...] = dy.astype(BF16)
        dg_ref[...] += dg

    return pl.pallas_call(body, grid=(t // tr,),
                          in_specs=[_row_spec(tr, d), _row_spec(tr, d), _full_spec((1, d))]
                          + [pl.BlockSpec(memory_space=pl.ANY)] * len(after),
                          out_specs=[_row_spec(tr, d), _full_spec((1, d))],
                          out_shape=[S((t, d), BF16), S((1, d), F32)], name=name,
                          compiler_params=_cp(("arbitrary",)))(dxn, y, g, *after)


def _prenorm_bwd(dxn, dh, x, g, name):
    t, d = x.shape
    tr = 256

    def body(dxn_ref, dh_ref, x_ref, g_ref, dx_ref, dg_ref):
        @pl.when(pl.program_id(0) == 0)
        def _():
            dg_ref[...] = jnp.zeros_like(dg_ref)

        dx, dg = _rms_bwd_rows(dh_ref[...], x_ref[...], g_ref[...])
        dx_ref[...] = dxn_ref[...] + dx
        dg_ref[...] += dg

    return pl.pallas_call(body, grid=(t // tr,),
                          in_specs=[_row_spec(tr, d), _row_spec(tr, d), _row_spec(tr, d), _full_spec((1, d))],
                          out_specs=[_row_spec(tr, d), _full_spec((1, d))],
                          out_shape=[S((t, d), F32), S((1, d), F32)], name=name,
                          compiler_params=_cp(("arbitrary",)))(dxn, dh, x, g)


def _loss_head(y, target):
    t, d = y.shape
    tr = 256

    def body(y_ref, t_ref, l_ref, dy_ref):
        @pl.when(pl.program_id(0) == 0)
        def _():
            l_ref[...] = jnp.zeros_like(l_ref)

        e = y_ref[...] - t_ref[...]
        dy_ref[...] = e * (1.0 / d)
        l_ref[...] += 0.5 * jnp.sum(jnp.mean(e * e, axis=-1, keepdims=True), axis=0, keepdims=True)

    return pl.pallas_call(body, grid=(t // tr,), in_specs=[_row_spec(tr, d), _row_spec(tr, d)],
                          out_specs=[_full_spec((1, 1)), _row_spec(tr, d)],
                          out_shape=[S((1, 1), F32), S((t, d), F32)], name="loss_head",
                          compiler_params=_cp(("arbitrary",)))(y, target)


def _sgu_parts(proj, lg, lb, ws_ref, bst):
    a = proj[:, 0:1024]
    bb = proj[:, 1024:2048]
    z = proj[:, 2048:3072]
    u = _gelu(a)
    vp = _gelu(bb)
    mu = jnp.mean(vp, axis=-1, keepdims=True)
    xc = vp - mu
    rstd = lax.rsqrt(jnp.mean(xc * xc, axis=-1, keepdims=True) + EPS)
    xh = xc * rstd
    vn = xh * lg + lb
    row = lax.broadcasted_iota(jnp.int32, (SGU_CHUNK, SGU_CHUNK), 0)
    col = lax.broadcasted_iota(jnp.int32, (SGU_CHUNK, SGU_CHUNK), 1)
    tril = row >= col
    ws = [jnp.where(tril, ws_ref[g], 0.0).astype(BF16) for g in range(SGU_GROUPS)]
    s = jnp.concatenate(
        [_dot(ws[g], vn[:, g * 64:(g + 1) * 64].astype(BF16)) + bst[:, g:g + 1] for g in range(SGU_GROUPS)], axis=1)
    return a, bb, z, u, vp, rstd, xh, vn, ws, tril, s


def _sgu_fwd(proj, lg, lb, w_s, bst):
    t = proj.shape[0]
    tr = SGU_CHUNK

    def body(p_ref, lg_ref, lb_ref, ws_ref, bst_ref, o_ref):
        _, _, z, u, _, _, _, _, _, _, s = _sgu_parts(p_ref[...], lg_ref[...], lb_ref[...], ws_ref, bst_ref[...])
        o_ref[...] = (u * s * _silu(z)).astype(BF16)

    return pl.pallas_call(
        body, grid=(t // tr,),
        in_specs=[_row_spec(tr, 3072), _full_spec((1, 1024)), _full_spec((1, 1024)), _full_spec((16, 128, 128)),
                  _full_spec((128, 16))],
        out_specs=_row_spec(tr, 1024), out_shape=S((t, 1024), BF16), name="sgu_fwd",
        compiler_params=_cp(("parallel",)))(proj, lg, lb, w_s, bst)


def _sgu_bwd(proj, dp, lg, lb, w_s, bst):
    t = proj.shape[0]
    tr = SGU_CHUNK

    def body(p_ref, dp_ref, lg_ref, lb_ref, ws_ref, bst_ref, dproj_ref, dws_ref, dbst_ref, dlg_ref, dlb_ref):
        @pl.when(pl.program_id(0) == 0)
        def _():
            dws_ref[...] = jnp.zeros_like(dws_ref)
            dbst_ref[...] = jnp.zeros_like(dbst_ref)
            dlg_ref[...] = jnp.zeros_like(dlg_ref)
            dlb_ref[...] = jnp.zeros_like(dlb_ref)

        lgv = lg_ref[...]
        a, bb, z, u, vp, rstd, xh, vn, ws, tril, s = _sgu_parts(p_ref[...], lgv, lb_ref[...], ws_ref, bst_ref[...])
        dpv = dp_ref[...].astype(F32)
        sz = _silu(z)
        du = dpv * s * sz
        ds = dpv * u * sz
        dz = dpv * u * s * _silu_grad(z)
        dvn_parts, dbs_cols = [], []
        for g in range(SGU_GROUPS):
            dsg = ds[:, g * 64:(g + 1) * 64]
            dsg_b = dsg.astype(BF16)
            dvn_parts.append(_dot_tn(ws[g], dsg_b))
            dws_ref[g] += jnp.where(tril, _dot_nt(dsg_b, vn[:, g * 64:(g + 1) * 64].astype(BF16)), 0.0)
            dbs_cols.append(jnp.sum(dsg, axis=1, keepdims=True))
        dvn = jnp.concatenate(dvn_parts, axis=1)
        dbst_ref[...] += jnp.concatenate(dbs_cols, axis=1)
        dlg_ref[...] += jnp.sum(dvn * xh, axis=0, keepdims=True)
        dlb_ref[...] += jnp.sum(dvn, axis=0, keepdims=True)
        dxh = dvn * lgv
        dvp = rstd * (dxh - jnp.mean(dxh, axis=-1, keepdims=True) - xh * jnp.mean(dxh * xh, axis=-1, keepdims=True))
        dproj_ref[:, 0:1024] = (du * _gelu_grad(a)).astype(BF16)
        dproj_ref[:, 1024:2048] = (dvp * _gelu_grad(bb)).astype(BF16)
        dproj_ref[:, 2048:3072] = dz.astype(BF16)

    return pl.pallas_call(
        body, grid=(t // tr,),
        in_specs=[_row_spec(tr, 3072), _row_spec(tr, 1024), _full_spec((1, 1024)), _full_spec((1, 1024)),
                  _full_spec((16, 128, 128)), _full_spec((128, 16))],
        out_specs=[_row_spec(tr, 3072), _full_spec((16, 128, 128)), _full_spec((128, 16)), _full_spec((1, 1024)),
                   _full_spec((1, 1024))],
        out_shape=[S((t, 3072), BF16), S((16, 128, 128), F32), S((128, 16), F32), S((1, 1024), F32),
                   S((1, 1024), F32)],
        name="sgu_bwd", compiler_params=_cp(("arbitrary",)))(proj, dp, lg, lb, w_s, bst)


def _t5_bucket_table():
    qi = np.arange(WINDOW)[:, None]
    kj = np.arange(2 * WINDOW)[None, :]
    dist = np.maximum(qi + WINDOW - kj, 0)
    max_exact = REL_BUCKETS // 2
    dist_f = np.maximum(dist, 1).astype(np.float32)
    large = max_exact + (np.log(dist_f / np.float32(max_exact)) / np.float32(math.log(REL_MAX_DIST / max_exact))
                         * np.float32(REL_BUCKETS - max_exact)).astype(np.int32)
    large = np.minimum(large, REL_BUCKETS - 1)
    return np.where(dist < max_exact, dist, large).astype(np.int32)


def _swa_bias(rel_bias, bucket):
    def body(rb_ref, bk_ref, o_ref):
        bk = bk_ref[...]
        for h in range(SWA_HEADS):
            acc = jnp.zeros((WINDOW, 2 * WINDOW), F32)
            for b in range(REL_BUCKETS):
                acc = jnp.where(bk == b, rb_ref[b, h], acc)
            o_ref[h] = acc

    return pl.pallas_call(
        body, in_specs=[pl.BlockSpec(memory_space=pltpu.SMEM), pl.BlockSpec(memory_space=pltpu.VMEM)],
        out_specs=pl.BlockSpec(memory_space=pltpu.VMEM), out_shape=S((SWA_HEADS, WINDOW, 2 * WINDOW), F32),
        name="swa_bias")(rel_bias, bucket)


def _swa_dbias(ds_sum, dsink_rows, bucket):
    def body(ds_ref, dsk_ref, bk_ref, drb_ref, dsink_ref):
        bk = bk_ref[...]
        r = lax.broadcasted_iota(jnp.int32, (REL_BUCKETS, SWA_HEADS), 0)
        c = lax.broadcasted_iota(jnp.int32, (REL_BUCKETS, SWA_HEADS), 1)
        out = jnp.zeros((REL_BUCKETS, SWA_HEADS), F32)
        for b in range(REL_BUCKETS):
            m = bk == b
            for h in range(SWA_HEADS):
                val = jnp.sum(jnp.where(m, ds_ref[h], 0.0))
                out = jnp.where((r == b) & (c == h), val, out)
        drb_ref[...] = out
        dsink_ref[...] = jnp.sum(dsk_ref[...], axis=0, keepdims=True)

    return pl.pallas_call(
        body, out_shape=[S((REL_BUCKETS, SWA_HEADS), F32), S((1, SWA_HEADS), F32)], name="swa_dbias",
    )(ds_sum, dsink_rows, bucket)


def _swa_band(cur_ref, prev_ref):
    kband = jnp.concatenate([prev_ref[:, 0:128], cur_ref[:, 1024:1152]], axis=0).astype(BF16)
    vband = jnp.concatenate([prev_ref[:, 128:256], cur_ref[:, 1152:1280]], axis=0).astype(BF16)
    return kband, vband


SWA_ROWS = SWA_GROUP * WINDOW


def _swa_valid(n, rows=SWA_ROWS):
    qi = lax.broadcasted_iota(jnp.int32, (rows, 2 * WINDOW), 0) & (WINDOW - 1)
    kj = lax.broadcasted_iota(jnp.int32, (rows, 2 * WINDOW), 1)
    dist = qi + WINDOW - kj
    return (dist >= 0) & (dist < WINDOW) & (n * WINDOW + kj - WINDOW >= 0)


def _swa_stack(x, j):
    return jnp.concatenate([x[:, h * 64:(h + 1) * 64] for h in range(j * SWA_GROUP, (j + 1) * SWA_GROUP)], axis=0)


def _swa_unstack(parts):
    return jnp.concatenate([p[i * WINDOW:(i + 1) * WINDOW] for p in parts for i in range(SWA_GROUP)], axis=1)


def _swa_sink_col(sink_ref, j):
    return jnp.concatenate([jnp.full((WINDOW, 1), sink_ref[0, h], F32)
                            for h in range(j * SWA_GROUP, (j + 1) * SWA_GROUP)], axis=0)


def _swa_probs(qh, kh, bias_h, sink, valid):
    s = _dot_nt(qh, kh) * (HEAD_DIM ** -0.5) + bias_h
    s = jnp.where(valid, s, NEG_INF)
    m = jnp.maximum(jnp.max(s, axis=-1, keepdims=True), sink)
    e = jnp.exp(s - m)
    es = jnp.exp(sink - m)
    inv = 1.0 / (jnp.sum(e, axis=-1, keepdims=True) + es)
    return e * inv, es * inv


def _swa_in_specs():
    return [_row_spec(WINDOW, 2304),
            pl.BlockSpec((WINDOW, 256), lambda n: (jnp.maximum(n - 1, 0), 4)),
            _full_spec((SWA_HEADS, WINDOW, 2 * WINDOW)),
            pl.BlockSpec(memory_space=pltpu.SMEM)]


def _swa_fwd(proj, bias, sinks):
    t = proj.shape[0]

    def body(cur_ref, prev_ref, bias_ref, sink_ref, o_ref):
        n = pl.program_id(0)
        kband, vband = _swa_band(cur_ref, prev_ref)
        valid = _swa_valid(n, WINDOW)
        outs = []
        for h in range(SWA_HEADS):
            j = h // SWA_GROUP
            qh = cur_ref[:, h * 64:(h + 1) * 64].astype(BF16)
            p, _ = _swa_probs(qh, kband[:, j * 64:(j + 1) * 64], bias_ref[h], sink_ref[0, h], valid)
            outs.append(_dot(p.astype(BF16), vband[:, j * 64:(j + 1) * 64]))
        o_ref[...] = (jnp.concatenate(outs, axis=1) * _silu(cur_ref[:, 1280:2304])).astype(BF16)

    return pl.pallas_call(body, grid=(t // WINDOW,), in_specs=_swa_in_specs(), out_specs=_row_spec(WINDOW, 1024),
                          out_shape=S((t, 1024), BF16), name="swa_fwd",
                          compiler_params=_cp(("parallel",)))(proj, proj, bias, sinks)


def _swa_bwd(proj, dp, bias, sinks):
    t = proj.shape[0]
    nb = t // WINDOW

    def body(cur_ref, prev_ref, bias_ref, sink_ref, dp_ref, dqz_ref, dkv_ref, dss_ref, dsk_ref):
        n = pl.program_id(0)

        @pl.when(n == 0)
        def _():
            dkv_ref[...] = jnp.zeros_like(dkv_ref)
            dss_ref[...] = jnp.zeros_like(dss_ref)
            dsk_ref[...] = jnp.zeros_like(dsk_ref)

        kband, vband = _swa_band(cur_ref, prev_ref)
        valid = _swa_valid(n)
        z = cur_ref[:, 1280:2304]
        dpv = dp_ref[...].astype(F32)
        do = dpv * _silu(z)
        q = cur_ref[:, 0:1024].astype(BF16)
        dob = do.astype(BF16)
        outs, dqs, dsinks, dk, dv = [], [], [], [], []
        for j in range(SWA_HEADS // SWA_GROUP):
            heads = slice(j * SWA_GROUP, (j + 1) * SWA_GROUP)
            qs, dos = _swa_stack(q, j), _swa_stack(dob, j)
            kh = kband[:, j * 64:(j + 1) * 64]
            vh = vband[:, j * 64:(j + 1) * 64]
            p, ps = _swa_probs(qs, kh, bias_ref[heads].reshape(SWA_ROWS, 2 * WINDOW), _swa_sink_col(sink_ref, j), valid)
            pb = p.astype(BF16)
            outs.append(_dot(pb, vh))
            dpr = _dot_nt(dos, vh)
            dlt = jnp.sum(p * dpr, axis=-1, keepdims=True)
            ds = p * (dpr - dlt)
            dsk = -ps * dlt
            dsinks += [dsk[i * WINDOW:(i + 1) * WINDOW] for i in range(SWA_GROUP)]
            dss_ref[heads] += ds.reshape(SWA_GROUP, WINDOW, 2 * WINDOW)
            dsb = (ds * (HEAD_DIM ** -0.5)).astype(BF16)
            dqs.append(_dot(dsb, kh))
            dk.append(_dot_tn(dsb, qs))
            dv.append(_dot_tn(pb, dos))
        dqz_ref[:, 0:1024] = _swa_unstack(dqs).astype(BF16)
        dqz_ref[:, 1024:2048] = (dpv * _swa_unstack(outs) * _silu_grad(z)).astype(BF16)
        dsk_ref[...] += jnp.concatenate(dsinks, axis=1)
        dband = jnp.concatenate([dk[0], dk[1], dv[0], dv[1]], axis=1)
        prow = pl.multiple_of(jnp.maximum(n - 1, 0) * WINDOW, WINDOW)
        dkv_ref[pl.ds(prow, WINDOW), :] += dband[0:WINDOW]
        crow = pl.multiple_of(n * WINDOW, WINDOW)
        dkv_ref[pl.ds(crow, WINDOW), :] += dband[WINDOW:]

    return pl.pallas_call(
        body, grid=(nb,), in_specs=_swa_in_specs() + [_row_spec(WINDOW, 1024)],
        out_specs=[_row_spec(WINDOW, 2048), _full_spec((t, 256)), _full_spec((SWA_HEADS, WINDOW, 2 * WINDOW)),
                   _full_spec((WINDOW, SWA_HEADS))],
        out_shape=[S((t, 2048), BF16), S((t, 256), F32), S((SWA_HEADS, WINDOW, 2 * WINDOW), F32),
                   S((WINDOW, SWA_HEADS), F32)],
        name="swa_bwd", compiler_params=_cp(("arbitrary",)))(proj, proj, bias, sinks, dp)


MLA_Z0 = MLA_Q_RANK + MLA_KV_RANK + 128
MLA_Z1 = MLA_Z0 + 1024
MLA_PROJ = MLA_Z1 + 128


def _rope_tables(t):
    inv = ROPE_BASE ** (-jnp.arange(0, MLA_ROPE, 2, dtype=F32) / MLA_ROPE)
    ang = jnp.arange(t, dtype=F32)[:, None] * inv[None, :]
    return jnp.tile(jnp.cos(ang), (1, MLA_HEADS)), jnp.tile(jnp.sin(ang), (1, MLA_HEADS))


def _mla_norms(proj, gq, gkv):
    t = proj.shape[0]
    tr = 256

    def body(p_ref, gq_ref, gkv_ref, q_ref, kv_ref):
        for lo, hi, g_ref, o_ref in ((0, 768, gq_ref, q_ref), (768, 1024, gkv_ref, kv_ref)):
            xv = p_ref[:, lo:hi]
            r = lax.rsqrt(jnp.mean(xv * xv, axis=-1, keepdims=True) + EPS)
            o_ref[...] = (xv * r * g_ref[...]).astype(BF16)

    return pl.pallas_call(
        body, grid=(t // tr,), in_specs=[_row_spec(tr, 1024), _full_spec((1, 768)), _full_spec((1, 256))],
        out_specs=[_row_spec(tr, 768), _row_spec(tr, 256)], out_shape=[S((t, 768), BF16), S((t, 256), BF16)],
        name="mla_norms", compiler_params=_cp(("parallel",)))(proj, gq, gkv)


def _mla_norms_bwd(proj, dcqn, dckvn, dkr, dz, gq, gkv):
    t = proj.shape[0]
    tr = 256

    def body(p_ref, dq_ref, dkv_ref, dkr_ref, dz_ref, gq_ref, gkv_ref, dproj_ref, dgq_ref, dgkv_ref):
        @pl.when(pl.program_id(0) == 0)
        def _():
            dgq_ref[...] = jnp.zeros_like(dgq_ref)
            dgkv_ref[...] = jnp.zeros_like(dgkv_ref)

        dx, dg = _rms_bwd_rows(dq_ref[...], p_ref[:, 0:768], gq_ref[...])
        dproj_ref[:, 0:768] = dx.astype(BF16)
        dgq_ref[...] += dg
        dx, dg = _rms_bwd_rows(dkv_ref[...], p_ref[:, 768:1024], gkv_ref[...])
        dproj_ref[:, 768:1024] = dx.astype(BF16)
        dgkv_ref[...] += dg
        dproj_ref[:, 1024:MLA_Z0] = dkr_ref[...].astype(BF16)
        dproj_ref[:, MLA_Z0:MLA_Z1] = dz_ref[...]
        dproj_ref[:, MLA_Z1:MLA_PROJ] = jnp.zeros((tr, MLA_PROJ - MLA_Z1), BF16)

    return pl.pallas_call(
        body, grid=(t // tr,),
        in_specs=[_row_spec(tr, 1024), _row_spec(tr, 768), _row_spec(tr, 256), _row_spec(tr, 128), _row_spec(tr, 1024),
                  _full_spec((1, 768)), _full_spec((1, 256))],
        out_specs=[_row_spec(tr, MLA_PROJ), _full_spec((1, 768)), _full_spec((1, 256))],
        out_shape=[S((t, MLA_PROJ), BF16), S((1, 768), F32), S((1, 256), F32)],
        name="mla_norms_bwd", compiler_params=_cp(("arbitrary",)))(proj, dcqn, dckvn, dkr, dz, gq, gkv)


def _mla_pack(qf, kvf, proj, cos, sin):
    t = qf.shape[0]
    tr = 256

    def body(q_ref, kn_ref, kr_ref, cos_ref, sin_ref, qh_ref, kh_ref):
        cs, sn = cos_ref[...], sin_ref[...]
        x1, x2 = q_ref[:, 1024:1280], q_ref[:, 1280:1536]
        r1 = x1 * cs - x2 * sn
        r2 = x2 * cs + x1 * sn
        c16, s16 = cs[:, 0:16], sn[:, 0:16]
        k1, k2 = kr_ref[:, 0:16], kr_ref[:, 16:32]
        kr1 = k1 * c16 - k2 * s16
        kr2 = k2 * c16 + k1 * s16
        zpad = jnp.zeros((tr, 32), F32)
        for h in range(MLA_HEADS):
            qh_ref[h] = jnp.concatenate([q_ref[:, h * 64:(h + 1) * 64], r1[:, h * 16:(h + 1) * 16],
                                         r2[:, h * 16:(h + 1) * 16], zpad], axis=1).astype(BF16)
            kh_ref[h] = jnp.concatenate([kn_ref[:, h * 64:(h + 1) * 64], kr1, kr2, zpad], axis=1).astype(BF16)

    hspec = pl.BlockSpec((MLA_HEADS, tr, 128), lambda i: (0, i, 0))
    return pl.pallas_call(
        body, grid=(t // tr,),
        in_specs=[_row_spec(tr, 1536), _row_spec(tr, 1024), pl.BlockSpec((tr, 128), lambda i: (i, 8)),
                  _row_spec(tr, 256), _row_spec(tr, 256)],
        out_specs=[hspec, hspec], out_shape=[S((MLA_HEADS, t, 128), BF16)] * 2,
        name="mla_pack", compiler_params=_cp(("parallel",)))(qf, kvf, proj, cos, sin)


def _mla_unpack(dqh, dkh, dv, cos, sin):
    t = dqh.shape[1]
    tr = 256

    def body(dqh_ref, dkh_ref, dv_ref, cos_ref, sin_ref, dq_ref, dkv_ref, dkr_ref):
        cs, sn = cos_ref[...], sin_ref[...]
        dqn = jnp.concatenate([dqh_ref[h, :, 0:64] for h in range(MLA_HEADS)], axis=1)
        dr1 = jnp.concatenate([dqh_ref[h, :, 64:80] for h in range(MLA_HEADS)], axis=1)
        dr2 = jnp.concatenate([dqh_ref[h, :, 80:96] for h in range(MLA_HEADS)], axis=1)
        dq_ref[:, 0:1024] = dqn.astype(BF16)
        dq_ref[:, 1024:1280] = (dr1 * cs + dr2 * sn).astype(BF16)
        dq_ref[:, 1280:1536] = (dr2 * cs - dr1 * sn).astype(BF16)
        dkv_ref[:, 0:1024] = jnp.concatenate([dkh_ref[h, :, 0:64] for h in range(MLA_HEADS)], axis=1).astype(BF16)
        dkv_ref[:, 1024:2048] = dv_ref[...].astype(BF16)
        d1 = dkh_ref[0, :, 64:80]
        d2 = dkh_ref[0, :, 80:96]
        for h in range(1, MLA_HEADS):
            d1 = d1 + dkh_ref[h, :, 64:80]
            d2 = d2 + dkh_ref[h, :, 80:96]
        c16, s16 = cs[:, 0:16], sn[:, 0:16]
        dkr_ref[...] = jnp.concatenate([d1 * c16 + d2 * s16, d2 * c16 - d1 * s16, jnp.zeros((tr, 96), F32)], axis=1)

    hspec = pl.BlockSpec((MLA_HEADS, tr, 128), lambda i: (0, i, 0))
    return pl.pallas_call(
        body, grid=(t // tr,), in_specs=[hspec, hspec, _row_spec(tr, 1024), _row_spec(tr, 256), _row_spec(tr, 256)],
        out_specs=[_row_spec(tr, 1536), _row_spec(tr, 2048), _row_spec(tr, 128)],
        out_shape=[S((t, 1536), BF16), S((t, 2048), BF16), S((t, 128), F32)],
        name="mla_unpack", compiler_params=_cp(("parallel",)))(dqh, dkh, dv, cos, sin)


def _mla_attn_fwd(qh, kh, kvf, proj):
    t = qh.shape[1]
    b, bk = MLA_BQ, MLA_BK

    def body(q_ref, k_ref, v_ref, z_ref, o_ref, p_ref, lse_ref):
        r0 = pl.program_id(1) * b
        nfull = r0 // bk
        diff = (lax.broadcasted_iota(jnp.int32, (b, bk), 1) - lax.broadcasted_iota(jnp.int32, (b, bk), 0))
        qs = [q_ref[0], q_ref[1]]

        def block(c0, carry, masked):
            new = []
            for a in range(2):
                m, l, acc = carry[a]
                k = k_ref[a, pl.ds(c0, bk), :]
                v = v_ref[pl.ds(c0, bk), a * 64:(a + 1) * 64].astype(BF16)
                s = _dot_nt(qs[a], k) * MLA_SCALE
                if masked:
                    s = jnp.where(diff <= r0 - c0, s, NEG_INF)
                mn = jnp.maximum(m, jnp.max(s, axis=-1, keepdims=True))
                alpha = jnp.exp(m - mn)
                p = jnp.exp(s - mn)
                new.append((mn, alpha * l + jnp.sum(p, axis=-1, keepdims=True), alpha * acc + _dot(p.astype(BF16), v)))
            return tuple(new)

        init = (jnp.full((b, 1), NEG_INF, F32), jnp.zeros((b, 1), F32), jnp.zeros((b, 64), F32))
        carry = lax.fori_loop(0, nfull, lambda c, cr: block(pl.multiple_of(c * bk, bk), cr, False), (init, init))
        for i in range(max(1, b // bk)):
            carry = block(pl.multiple_of((nfull + i) * bk, bk), carry, True)
        outs = [acc / l for _, l, acc in carry]
        lses = [jnp.broadcast_to(m + jnp.log(l), (b, 64)) for m, l, _ in carry]
        o = jnp.concatenate(outs, axis=1)
        o_ref[...] = o
        p_ref[...] = (o * _silu(z_ref[...])).astype(BF16)
        lse_ref[0] = jnp.concatenate(lses, axis=1)

    return pl.pallas_call(
        body, grid=(MLA_HEADS // 2, t // b),
        in_specs=[pl.BlockSpec((2, b, 128), lambda i, r: (i, r, 0)), pl.BlockSpec((2, t, 128), lambda i, r: (i, 0, 0)),
                  pl.BlockSpec((t, 128), lambda i, r: (0, 8 + i)), pl.BlockSpec((b, 128), lambda i, r: (r, 9 + i))],
        out_specs=[pl.BlockSpec((b, 128), lambda i, r: (r, i)), pl.BlockSpec((b, 128), lambda i, r: (r, i)),
                   pl.BlockSpec((1, b, 128), lambda i, r: (i, r, 0))],
        out_shape=[S((t, 1024), F32), S((t, 1024), BF16), S((MLA_HEADS // 2, t, 128), F32)],
        name="mla_attn_fwd", compiler_params=_cp(("parallel", "parallel")))(qh, kh, kvf, proj)


def _mla_attn_bwd(qh, kh, kvf, proj, o, dpout, lse):
    t = qh.shape[1]
    b, bk = MLA_BQ, MLA_BK
    nb = t // b

    def body(q_ref, k_ref, v_ref, z_ref, o_ref, dp_ref, lse_ref, dq_ref, dk_ref, dv_ref, dz_ref):
        dk_ref[...] = jnp.zeros_like(dk_ref)
        dv_ref[...] = jnp.zeros_like(dv_ref)
        diff = (lax.broadcasted_iota(jnp.int32, (b, bk), 1) - lax.broadcasted_iota(jnp.int32, (b, bk), 0))
        lanes = [slice(0, 64), slice(64, 128)]

        def rbody(r, carry):
            r0 = pl.multiple_of(r * b, b)
            rows = pl.ds(r0, b)
            qs, dobs, deltas, lses = [], [], [], []
            for a in range(2):
                zz = z_ref[rows, lanes[a]]
                oo = o_ref[rows, lanes[a]]
                dpo = dp_ref[rows, lanes[a]].astype(F32)
                do = dpo * _silu(zz)
                dz_ref[rows, lanes[a]] = (dpo * oo * _silu_grad(zz)).astype(BF16)
                qs.append(q_ref[a, rows, :])
                dobs.append(do.astype(BF16))
                deltas.append(jnp.sum(do * oo, axis=-1, keepdims=True))
                lses.append(lse_ref[0, rows, a * 64:a * 64 + 1])

            def block(c0, dqs, masked):
                cols = pl.ds(c0, bk)
                out = []
                for a in range(2):
                    k = k_ref[a, cols, :]
                    v = v_ref[cols, lanes[a]].astype(BF16)
                    s = _dot_nt(qs[a], k) * MLA_SCALE
                    if masked:
                        s = jnp.where(diff <= r0 - c0, s, NEG_INF)
                    p = jnp.exp(s - lses[a])
                    ds = p * (_dot_nt(dobs[a], v) - deltas[a]) * MLA_SCALE
                    dsb = ds.astype(BF16)
                    dk_ref[a, cols, :] += _dot_tn(dsb, qs[a])
                    dv_ref[cols, lanes[a]] += _dot_tn(p.astype(BF16), dobs[a])
                    out.append(dqs[a] + _dot(dsb, k))
                return tuple(out)

            zero = jnp.zeros((b, 128), F32)
            nfull = r0 // bk
            dqs = lax.fori_loop(0, nfull, lambda c, d: block(pl.multiple_of(c * bk, bk), d, False), (zero, zero))
            for i in range(max(1, b // bk)):
                dqs = block(pl.multiple_of((nfull + i) * bk, bk), dqs, True)
            dq_ref[0, rows, :] = dqs[0]
            dq_ref[1, rows, :] = dqs[1]
            return carry

        lax.fori_loop(0, nb, rbody, 0)

    pair3 = pl.BlockSpec((2, t, 128), lambda i: (i, 0, 0))
    return pl.pallas_call(
        body, grid=(MLA_HEADS // 2,),
        in_specs=[pair3, pair3, pl.BlockSpec((t, 128), lambda i: (0, 8 + i)), pl.BlockSpec((t, 128), lambda i: (0, 9 + i)),
                  pl.BlockSpec((t, 128), lambda i: (0, i)), pl.BlockSpec((t, 128), lambda i: (0, i)),
                  pl.BlockSpec((1, t, 128), lambda i: (i, 0, 0))],
        out_specs=[pair3, pair3, pl.BlockSpec((t, 128), lambda i: (0, i)), pl.BlockSpec((t, 128), lambda i: (0, i))],
        out_shape=[S((MLA_HEADS, t, 128), F32), S((MLA_HEADS, t, 128), F32), S((t, 1024), F32), S((t, 1024), BF16)],
        name="mla_attn_bwd", compiler_params=_cp(("parallel",), VMEM_BIG))(qh, kh, kvf, proj, o, dpout, lse)


def _dot_nt_hi(a, b):
    return lax.dot_general(a, b, (((1,), (1,)), ((), ())), preferred_element_type=F32,
                           precision=lax.Precision.HIGHEST)


SSM_PACK = 2


def _ssm_gen(lr, li, ldt, btr, bti, cr, ci):
    n = SSM_BLOCK
    lanes = SSM_PACK * SSM_STATE
    dt = jnp.exp(ldt)
    x = lr * dt
    w = li * dt
    mag = jnp.exp(x)
    ab_re = mag * jnp.cos(w)
    ab_im = mag * jnp.sin(w)
    den = lr * lr + li * li
    nr = ab_re - 1.0
    f_re = (nr * lr + ab_im * li) / den
    f_im = (ab_im * lr - nr * li) / den
    bb_re = f_re * btr - f_im * bti
    bb_im = f_re * bti + f_im * btr
    inv_mag = jnp.exp(-x)
    inv_re, inv_im = inv_mag * jnp.cos(w), -inv_mag * jnp.sin(w)
    one, zero = jnp.ones_like(x), jnp.zeros_like(x)
    pos, neg = [(one, zero)], [(one, zero)]
    for _ in range(SSM_CHUNK):
        pr, pi = pos[-1]
        pos.append((pr * ab_re - pi * ab_im, pr * ab_im + pi * ab_re))
        pr, pi = neg[-1]
        neg.append((pr * inv_re - pi * inv_im, pr * inv_im + pi * inv_re))

    def stack(table, idx):
        return (jnp.concatenate([jnp.broadcast_to(table[idx(t)][0], (SSM_GROUP, lanes)) for t in range(SSM_CHUNK)], axis=0),
                jnp.concatenate([jnp.broadcast_to(table[idx(t)][1], (SSM_GROUP, lanes)) for t in range(SSM_CHUNK)], axis=0))

    def rep(v):
        return jnp.concatenate([v] * SSM_CHUNK, axis=0)

    ct_re, ct_im, bt_re, bt_im = rep(cr), rep(ci), rep(bb_re), rep(bb_im)
    p_re, p_im = stack(pos, lambda t: t)
    l_re = ct_re * p_re - ct_im * p_im
    l_im = ct_re * p_im + ct_im * p_re
    n_re, n_im = stack(neg, lambda t: t)
    r_re = bt_re * n_re - bt_im * n_im
    r_im = bt_re * n_im + bt_im * n_re
    rk = lax.broadcasted_iota(jnp.int32, (n, n), 0) // SSM_GROUP
    ct = lax.broadcasted_iota(jnp.int32, (n, n), 1) // SSM_GROUP
    mts = []
    for g in range(SSM_PACK):
        sl = slice(g * SSM_STATE, (g + 1) * SSM_STATE)
        mts.append(jnp.where(ct >= rk, _dot_nt_hi(r_re[:, sl], l_re[:, sl]) - _dot_nt_hi(r_im[:, sl], l_im[:, sl]), 0.0))
    e_re, e_im = stack(pos, lambda t: SSM_CHUNK - 1 - t)
    pin_re = bt_re * e_re - bt_im * e_im
    pin_im = bt_re * e_im + bt_im * e_re
    q_re, q_im = stack(pos, lambda t: t + 1)
    qt_re = ct_re * q_re - ct_im * q_im
    qt_im = -(ct_re * q_im + ct_im * q_re)
    ad_re, ad_im = pos[SSM_CHUNK]
    return tuple(mts), pin_re, pin_im, qt_re, qt_im, ad_re, ad_im


_SSM_GEN_OUT = [(SSM_BLOCK, SSM_BLOCK)] + [(SSM_BLOCK, SSM_STATE)] * 4 + [(1, SSM_STATE)] * 2
_SSM_GEN_IN = [(1, SSM_STATE)] * 3 + [(SSM_GROUP, SSM_STATE)] * 4


def _gspec(shape2):
    return pl.BlockSpec((SSM_PACK,) + shape2, lambda g: (g, 0, 0))


def _pack_lanes(ref):
    return jnp.concatenate([ref[g] for g in range(SSM_PACK)], axis=1)


def _unpack_lanes(ref, val):
    for g in range(SSM_PACK):
        ref[g] = val[:, g * SSM_STATE:(g + 1) * SSM_STATE]


def _ssm_operators(params):
    def body(*refs):
        ins, outs = refs[:7], refs[7:]
        vals = _ssm_gen(*[_pack_lanes(r) for r in ins])
        for g in range(SSM_PACK):
            outs[0][g] = vals[0][g]
        for o_ref, val in zip(outs[1:], vals[1:]):
            _unpack_lanes(o_ref, val)

    return pl.pallas_call(
        body, grid=(SSM_GROUPS // SSM_PACK,), in_specs=[_gspec(s) for s in _SSM_GEN_IN],
        out_specs=[_gspec(s) for s in _SSM_GEN_OUT],
        out_shape=[S((SSM_GROUPS,) + s, F32) for s in _SSM_GEN_OUT], name="ssm_operators",
        compiler_params=_cp(("parallel",)))(*params)


def _ssm_operators_bwd(params, cots, after=()):
    def body(*refs):
        ins, cts, outs = refs[:7], refs[7:14], refs[14 + len(after):]
        _, vjp = jax.vjp(_ssm_gen, *[_pack_lanes(r) for r in ins])
        grads = list(vjp((tuple(cts[0][g] for g in range(SSM_PACK)),) + tuple(_pack_lanes(r) for r in cts[1:])))
        for o_ref, val in zip(outs, grads):
            _unpack_lanes(o_ref, val)
        for g in range(SSM_PACK):
            dldt = grads[2][:, g * SSM_STATE:(g + 1) * SSM_STATE]
            outs[2][g] = jnp.broadcast_to(jnp.sum(dldt, axis=-1, keepdims=True), (1, SSM_STATE))

    return pl.pallas_call(
        body, grid=(SSM_GROUPS // SSM_PACK,),
        in_specs=[_gspec(s) for s in _SSM_GEN_IN] + [_gspec(s) for s in _SSM_GEN_OUT]
        + [pl.BlockSpec(memory_space=pl.ANY)] * len(after),
        out_specs=[_gspec(s) for s in _SSM_GEN_IN], out_shape=[S((SSM_GROUPS,) + s, F32) for s in _SSM_GEN_IN],
        name="ssm_operators_bwd", compiler_params=_cp(("parallel",)))(*params, *cots, *after)


def _shift_rows(x, sh, row, up):
    n = x.shape[0]
    if up:
        return jnp.where(row < n - sh, pltpu.roll(x, n - sh, axis=0), 0.0)
    return jnp.where(row >= sh, pltpu.roll(x, sh, axis=0), 0.0)


def _carry_scan(sr, si, ar, ai, row, up):
    n = sr.shape[0]
    sh = 1
    while sh < n:
        tr, ti = _shift_rows(sr, sh, row, up), _shift_rows(si, sh, row, up)
        sr, si = sr + ar * tr - ai * ti, si + ar * ti + ai * tr
        ar, ai = ar * ar - ai * ai, 2.0 * ar * ai
        sh *= 2
    return sr, si


def _ssm_states(u, p_re, p_im, ar, ai, row):
    lre = _dot(u, p_re.astype(BF16))
    lim = _dot(u, p_im.astype(BF16))
    sr, si = _carry_scan(lre, lim, ar, ai, row, False)
    return _shift_rows(sr, 1, row, False), _shift_rows(si, 1, row, False)


SSM_GPB = 128 // SSM_GROUP


def _step_rows(ref, nc):
    return [ref[pl.ds(k, nc, stride=SSM_CHUNK), :].astype(BF16) for k in range(SSM_CHUNK)]


def _group_of(rows, j):
    return jnp.concatenate([r[:, j * SSM_GROUP:(j + 1) * SSM_GROUP] for r in rows], axis=1)


def _store_groups(o_ref, per_group, nc):
    for k in range(SSM_CHUNK):
        o_ref[pl.ds(k, nc, stride=SSM_CHUNK), :] = jnp.concatenate(
            [y[:, k * SSM_GROUP:(k + 1) * SSM_GROUP] for y in per_group], axis=1)


def _bspec(shape2):
    return pl.BlockSpec((SSM_GPB,) + shape2, lambda j: (j, 0, 0))


def _ssm_core_fwd(proj, ops):
    t = proj.shape[0]
    nc = t // SSM_CHUNK

    def body(u_ref, mt_ref, pr_ref, pi_ref, qr_ref, qi_ref, ar_ref, ai_ref, y_ref):
        rows = _step_rows(u_ref, nc)
        row = lax.broadcasted_iota(jnp.int32, (nc, SSM_STATE), 0)
        ys = []
        for j in range(SSM_GPB):
            u = _group_of(rows, j)
            s_re, s_im = _ssm_states(u, pr_ref[j], pi_ref[j], ar_ref[j], ai_ref[j], row)
            ys.append(_dot(u, mt_ref[j].astype(BF16)) + _dot_nt(s_re.astype(BF16), qr_ref[j].astype(BF16))
                      + _dot_nt(s_im.astype(BF16), qi_ref[j].astype(BF16)))
        _store_groups(y_ref, ys, nc)

    blk = pl.BlockSpec((t, 128), lambda j: (0, j))
    return pl.pallas_call(
        body, grid=(SSM_GROUPS // SSM_GPB,), in_specs=[blk] + [_bspec(s) for s in _SSM_GEN_OUT], out_specs=blk,
        out_shape=S((t, 1024), F32), name="ssm_core_fwd", compiler_params=_cp(("parallel",)))(proj, *ops)


def _ssm_core_bwd(proj, dy, ops):
    t = proj.shape[0]
    nc = t // SSM_CHUNK

    def body(u_ref, dy_ref, mt_ref, pr_ref, pi_ref, qr_ref, qi_ref, ar_ref, ai_ref,
             du_ref, dmt_ref, dpr_ref, dpi_ref, dqr_ref, dqi_ref, dar_ref, dai_ref):
        u_rows = _step_rows(u_ref, nc)
        dy_rows = _step_rows(dy_ref, nc)
        row = lax.broadcasted_iota(jnp.int32, (nc, SSM_STATE), 0)
        dus = []
        for j in range(SSM_GPB):
            u = _group_of(u_rows, j)
            dyj = _group_of(dy_rows, j)
            ar, ai = ar_ref[j], ai_ref[j]
            s_re, s_im = _ssm_states(u, pr_ref[j], pi_ref[j], ar, ai, row)
            g_re = _shift_rows(_dot(dyj, qr_ref[j].astype(BF16)), 1, row, True)
            g_im = _shift_rows(_dot(dyj, qi_ref[j].astype(BF16)), 1, row, True)
            l_re, l_im = _carry_scan(g_re, g_im, ar, -ai, row, True)
            lrb, lib = l_re.astype(BF16), l_im.astype(BF16)
            dus.append(_dot_nt(dyj, mt_ref[j].astype(BF16)) + _dot_nt(lrb, pr_ref[j].astype(BF16))
                       + _dot_nt(lib, pi_ref[j].astype(BF16)))
            dmt_ref[j] = _dot_tn(u, dyj)
            dpr_ref[j] = _dot_tn(u, lrb)
            dpi_ref[j] = _dot_tn(u, lib)
            dqr_ref[j] = _dot_tn(dyj, s_re.astype(BF16))
            dqi_ref[j] = _dot_tn(dyj, s_im.astype(BF16))
            dar_ref[j] = jnp.sum(l_re * s_re + l_im * s_im, axis=0, keepdims=True)
            dai_ref[j] = jnp.sum(l_im * s_re - l_re * s_im, axis=0, keepdims=True)
        _store_groups(du_ref, dus, nc)

    blk = pl.BlockSpec((t, 128), lambda j: (0, j))
    return pl.pallas_call(
        body, grid=(SSM_GROUPS // SSM_GPB,), in_specs=[blk, blk] + [_bspec(s) for s in _SSM_GEN_OUT],
        out_specs=[blk] + [_bspec(s) for s in _SSM_GEN_OUT],
        out_shape=[S((t, 1024), F32)] + [S((SSM_GROUPS,) + s, F32) for s in _SSM_GEN_OUT],
        name="ssm_core_bwd", compiler_params=_cp(("parallel",)))(proj, dy, *ops)


def _s5_act(y_core, proj, d_skip):
    t = y_core.shape[0]
    tr = 256

    def body(y_ref, u_ref, d_ref, o_ref):
        o_ref[...] = _gelu(y_ref[...] + d_ref[...] * u_ref[...])

    return pl.pallas_call(body, grid=(t // tr,), in_specs=[_row_spec(tr, 1024), _row_spec(tr, 1024), _full_spec((1, 1024))],
                          out_specs=_row_spec(tr, 1024), out_shape=S((t, 1024), F32), name="s5_act",
                          compiler_params=_cp(("parallel",)))(y_core, proj, d_skip)


def _s5_gate(yg, tg, b_glu, proj):
    t = yg.shape[0]
    tr = 256

    def body(y_ref, t_ref, b_ref, z_ref, o_ref):
        o_ref[...] = (y_ref[...] * jax.nn.sigmoid(t_ref[...] + b_ref[...]) * _silu(z_ref[...])).astype(BF16)

    return pl.pallas_call(
        body, grid=(t // tr,),
        in_specs=[_row_spec(tr, 1024), _row_spec(tr, 1024), _full_spec((1, 1024)), pl.BlockSpec((tr, 1024), lambda i: (i, 1))],
        out_specs=_row_spec(tr, 1024), out_shape=S((t, 1024), BF16), name="s5_gate",
        compiler_params=_cp(("parallel",)))(yg, tg, b_glu, proj)


def _s5_gate_bwd(dp, yg, tg, b_glu, proj):
    t = yg.shape[0]
    tr = 256

    def body(dp_ref, y_ref, t_ref, b_ref, z_ref, dt_ref, dy_ref, dz_ref, db_ref):
        @pl.when(pl.program_id(0) == 0)
        def _():
            db_ref[...] = jnp.zeros_like(db_ref)

        z = z_ref[...]
        yv = y_ref[...]
        sg = jax.nn.sigmoid(t_ref[...] + b_ref[...])
        dpv = dp_ref[...].astype(F32)
        do = dpv * _silu(z)
        dz_ref[...] = (dpv * yv * sg * _silu_grad(z)).astype(BF16)
        dy_ref[...] = do * sg
        dtv = do * yv * sg * (1.0 - sg)
        dt_ref[...] = dtv.astype(BF16)
        db_ref[...] += jnp.sum(dtv, axis=0, keepdims=True)

    return pl.pallas_call(
        body, grid=(t // tr,),
        in_specs=[_row_spec(tr, 1024), _row_spec(tr, 1024), _row_spec(tr, 1024), _full_spec((1, 1024)),
                  pl.BlockSpec((tr, 1024), lambda i: (i, 1))],
        out_specs=[_row_spec(tr, 1024), _row_spec(tr, 1024), _row_spec(tr, 1024), _full_spec((1, 1024))],
        out_shape=[S((t, 1024), BF16), S((t, 1024), F32), S((t, 1024), BF16), S((1, 1024), F32)],
        name="s5_gate_bwd", compiler_params=_cp(("arbitrary",)))(dp, yg, tg, b_glu, proj)


def _s5_act_bwd(dyg1, dyg2, y_core, proj, d_skip):
    t = y_core.shape[0]
    tr = 256

    def body(a_ref, b_ref, y_ref, u_ref, d_ref, dy_ref, du_ref, dd_ref):
        @pl.when(pl.program_id(0) == 0)
        def _():
            dd_ref[...] = jnp.zeros_like(dd_ref)

        u = u_ref[...]
        dv = (a_ref[...] + b_ref[...]) * _gelu_grad(y_ref[...] + d_ref[...] * u)
        dy_ref[...] = dv
        du_ref[...] = dv * d_ref[...]
        dd_ref[...] += jnp.sum(dv * u, axis=0, keepdims=True)

    return pl.pallas_call(
        body, grid=(t // tr,),
        in_specs=[_row_spec(tr, 1024)] * 4 + [_full_spec((1, 1024))],
        out_specs=[_row_spec(tr, 1024), _row_spec(tr, 1024), _full_spec((1, 1024))],
        out_shape=[S((t, 1024), F32), S((t, 1024), F32), S((1, 1024), F32)],
        name="s5_act_bwd", compiler_params=_cp(("arbitrary",)))(dyg1, dyg2, y_core, proj, d_skip)


def _s5_dproj(du_skip, du_core, dz):
    t = du_skip.shape[0]
    tr = 256

    def body(a_ref, b_ref, z_ref, o_ref):
        o_ref[:, 0:1024] = (a_ref[...] + b_ref[...]).astype(BF16)
        o_ref[:, 1024:2048] = z_ref[...]

    return pl.pallas_call(body, grid=(t // tr,), in_specs=[_row_spec(tr, 1024)] * 3, out_specs=_row_spec(tr, 2048),
                          out_shape=S((t, 2048), BF16), name="s5_dproj",
                          compiler_params=_cp(("parallel",)))(du_skip, du_core, dz)


def _s5_params(lam_re, lam_im, log_dt, b_re, b_im, c_re, c_im):
    g = SSM_GROUPS
    return (lam_re.reshape(g, 1, SSM_STATE), lam_im.reshape(g, 1, SSM_STATE),
            jnp.broadcast_to(log_dt.reshape(g, 1, 1), (g, 1, SSM_STATE)),
            b_re.transpose(0, 2, 1), b_im.transpose(0, 2, 1), c_re, c_im)


def _s5_fwd(proj, params, d_skip, w_glu, b_glu):
    ops = _ssm_operators(params)
    y_core = _ssm_core_fwd(proj, ops)
    yg = _s5_act(y_core, proj, d_skip)
    tg = _mm(yg, w_glu, name="s5_glu")
    return _s5_gate(yg, tg, b_glu, proj), (ops, y_core, yg, tg)


def _s5_bwd(proj, res, dp, params, d_skip, w_glu, b_glu):
    ops, y_core, yg, tg = res
    dtg, dyg1, dz, db_glu = _s5_gate_bwd(dp, yg, tg, b_glu, proj)
    dyg2 = _mm(dtg, w_glu, tb=True, name="s5_glu_dx")
    dw_glu = _mm(yg, dtg, ta=True, out_dtype=BF16, name="s5_glu_dw")
    dy_core, du_skip, dd = _s5_act_bwd(dyg1, dyg2, y_core, proj, d_skip)
    outs = _ssm_core_bwd(proj, dy_core, ops)
    dproj = _s5_dproj(du_skip, outs[0], dz)
    return dproj, outs[1:], dd, dw_glu, db_glu


def _s5_param_grads(params, cots, after=()):
    glr, gli, gdt, gbtr, gbti, gcr, gci = _ssm_operators_bwd(params, cots, after)
    g = SSM_GROUPS
    return (glr.reshape(g, SSM_STATE), gli.reshape(g, SSM_STATE), gdt[:, 0, 0].reshape(1, g),
            gbtr.transpose(0, 2, 1), gbti.transpose(0, 2, 1), gcr, gci)


def _perm_uq(w):
    w3 = w.reshape(w.shape[0], MLA_HEADS, MLA_NOPE + MLA_ROPE)
    return jnp.concatenate([w3[:, :, :64].reshape(-1, 1024), w3[:, :, 64:80].reshape(-1, 256),
                            w3[:, :, 80:96].reshape(-1, 256)], axis=1)


def _unperm_uq(w):
    r = w.shape[0]
    return jnp.concatenate([w[:, :1024].reshape(r, 16, 64), w[:, 1024:1280].reshape(r, 16, 16),
                            w[:, 1280:1536].reshape(r, 16, 16)], axis=2).reshape(r, 1536)


def _perm_ukv(w):
    w3 = w.reshape(w.shape[0], MLA_HEADS, 128)
    return jnp.concatenate([w3[:, :, :64].reshape(-1, 1024), w3[:, :, 64:].reshape(-1, 1024)], axis=1)


def _unperm_ukv(w):
    r = w.shape[0]
    return jnp.concatenate([w[:, :1024].reshape(r, 16, 64), w[:, 1024:].reshape(r, 16, 64)], axis=2).reshape(r, 2048)


def _mla_fwd(proj, gq, gkv, wuq, wukv, cos, sin):
    cqn, ckvn = _mla_norms(proj, gq, gkv)
    qf = _mm(cqn, wuq, name="mla_uq")
    kvf = _mm(ckvn, wukv, name="mla_ukv")
    qh, kh = _mla_pack(qf, kvf, proj, cos, sin)
    o, pout, lse = _mla_attn_fwd(qh, kh, kvf, proj)
    return pout, (cqn, ckvn, kvf, qh, kh, o, lse)


def _mla_bwd(proj, res, dpout, gq, gkv, wuq, wukv, cos, sin):
    cqn, ckvn, kvf, qh, kh, o, lse = res
    dqh, dkh, dv, dz = _mla_attn_bwd(qh, kh, kvf, proj, o, dpout, lse)
    dqf, dkvf, dkr = _mla_unpack(dqh, dkh, dv, cos, sin)
    dcqn = _mm(dqf, wuq, tb=True, name="mla_uq_dx")
    dwuq = _mm(cqn, dqf, ta=True, out_dtype=BF16, name="mla_uq_dw")
    dckvn = _mm(dkvf, wukv, tb=True, name="mla_ukv_dx")
    dwukv = _mm(ckvn, dkvf, ta=True, out_dtype=BF16, name="mla_ukv_dw")
    dproj, dgq, dgkv = _mla_norms_bwd(proj, dcqn, dckvn, dkr, dz, gq, gkv)
    return dproj, dgq, dgkv, dwuq, dwukv


_ANY = pl.BlockSpec(memory_space=pl.ANY)


def _chip_peers():
    x, y, c = lax.axis_index("x"), lax.axis_index("y"), lax.axis_index("c")
    return 2 * x + y, c, [(1 - x, y), (x, 1 - y), (1 - x, 1 - y)]


def _exchange_chips(src_of, dst_of, send_sems, recv_sems, loc_sem):
    me, c, peers = _chip_peers()
    loc = pltpu.make_async_copy(src_of(me), dst_of(me), loc_sem)
    loc.start()

    def copy(k, px, py, src_chip, dst_chip):
        return pltpu.make_async_remote_copy(src_ref=src_of(src_chip), dst_ref=dst_of(dst_chip), send_sem=send_sems.at[k],
                                            recv_sem=recv_sems.at[k], device_id=(px, py, c), device_id_type=MESH)

    sends = [copy(k, px, py, 2 * px + py, me) for k, (px, py) in enumerate(peers)]
    for cp in sends:
        cp.start()
    for k, (px, py) in enumerate(peers):
        copy(k, px, py, me, 2 * px + py).wait_recv()
    for cp in sends:
        cp.wait_send()
    loc.wait()


_EXCH_SCRATCH = [pltpu.SemaphoreType.DMA((3,)), pltpu.SemaphoreType.DMA((3,)), pltpu.SemaphoreType.DMA(())]


def _all_gather_chips(xs, name):
    def body(x_ref, o_ref, send_sems, recv_sems, loc_sem):
        _exchange_chips(lambda s: x_ref, lambda s: o_ref.at[s], send_sems, recv_sems, loc_sem)

    return pl.pallas_call(body, in_specs=[_ANY], out_specs=_ANY, out_shape=S((N_CHIPS,) + xs.shape, xs.dtype),
                          scratch_shapes=_EXCH_SCRATCH, name=name)(xs)


def _sibling_swap(ps, name):
    n = len(ps)

    def body(*refs):
        p_refs, q_refs, send_sems, recv_sems = refs[:n], refs[n:2 * n], refs[2 * n], refs[2 * n + 1]
        x, y, c = lax.axis_index("x"), lax.axis_index("y"), lax.axis_index("c")
        cps = [pltpu.make_async_remote_copy(src_ref=p_refs[i], dst_ref=q_refs[i], send_sem=send_sems.at[i],
                                            recv_sem=recv_sems.at[i], device_id=(x, y, 1 - c), device_id_type=MESH)
               for i in range(n)]
        for cp in cps:
            cp.start()
        for cp in cps:
            cp.wait()

    return pl.pallas_call(body, in_specs=[_ANY] * n, out_specs=[_ANY] * n, out_shape=[S(p.shape, p.dtype) for p in ps],
                          scratch_shapes=[pltpu.SemaphoreType.DMA((n,)), pltpu.SemaphoreType.DMA((n,))], name=name)(*ps)


def _quarter_spec(tr, r, c, dim):
    if dim is None:
        return pl.BlockSpec((1, tr, c), lambda i, me: (me[0], i, 0))
    if dim == 0:
        return pl.BlockSpec((tr, c), lambda i, me: (me[0] * (r // tr) + i, 0))
    return pl.BlockSpec((tr, c), lambda i, me: (i, me[0]))


def _whole_shape(r, c, dim):
    return (N_CHIPS, r, c) if dim is None else ((N_CHIPS * r, c) if dim == 0 else (r, N_CHIPS * c))


def _row_tile(r):
    return 256 if r % 256 == 0 else (128 if r % 128 == 0 else r)


def _cast_place(me, x, dim, dtype, name):
    r, c = x.shape
    tr = _row_tile(r)

    def body(me_ref, x_ref, o_ref):
        o_ref[...] = x_ref[...].astype(dtype).reshape(o_ref.shape)

    return pl.pallas_call(
        body, out_shape=S(_whole_shape(r, c, dim), dtype), name=name,
        grid_spec=pltpu.PrefetchScalarGridSpec(num_scalar_prefetch=1, grid=(r // tr,),
                                               in_specs=[pl.BlockSpec((tr, c), lambda i, me: (i, 0))],
                                               out_specs=_quarter_spec(tr, r, c, dim)),
        compiler_params=_cp(("parallel",)))(me, x)


def _sum_parts(me, g, land, dim, name):
    _, r, c = land.shape
    tr = _row_tile(r)

    def body(me_ref, g_ref, l_ref, o_ref):
        own = g_ref[...].astype(F32).reshape(tr, c)
        o_ref[...] = ((own + l_ref[0].astype(F32)) + l_ref[1].astype(F32)) + l_ref[2].astype(F32)

    return pl.pallas_call(
        body, out_shape=S((r, c), F32), name=name,
        grid_spec=pltpu.PrefetchScalarGridSpec(
            num_scalar_prefetch=1, grid=(r // tr,),
            in_specs=[_quarter_spec(tr, r, c, dim), pl.BlockSpec((3, tr, c), lambda i, me: (0, i, 0))],
            out_specs=pl.BlockSpec((tr, c), lambda i, me: (i, 0))),
        compiler_params=_cp(("parallel",)))(me, g, land)


_HBM = pl.BlockSpec(memory_space=pltpu.HBM)
_SEM = pl.BlockSpec(memory_space=pltpu.SEMAPHORE)
_EFFECT = pltpu.SideEffectType.DATAFLOW_SIDE_EFFECTING


def _quarter(ref, s, dim):
    n = ref.shape[dim] // N_CHIPS
    start = pl.multiple_of(s * n, n)
    return ref.at[pl.ds(start, n), :] if dim == 0 else ref.at[:, pl.ds(start, n)]


def _part(ref, s, dim):
    return ref.at[s] if dim is None else _quarter(ref, s, dim)


def _gather_plan(i, dim):
    return dict(src=i, dst=i, send_src=lambda ref, me, peer, k: _part(ref, me, dim),
                send_dst=lambda ref, me, peer, k: _part(ref, me, dim), recv_dst=lambda ref, me, peer, k: _part(ref, peer, dim))


def _scatter_plan(i, j, dim):
    return dict(src=i, dst=j, send_src=lambda ref, me, peer, k: _part(ref, peer, dim),
                send_dst=lambda ref, me, peer, k: ref.at[k], recv_dst=lambda ref, me, peer, k: ref.at[k])


def _hbm(a):
    return pltpu.with_memory_space_constraint(a, pltpu.HBM)


def _remote_copies(refs, send_sems, recv_sems, plans):
    me, c, peers = _chip_peers()
    sends, recvs = [], []
    for i, p in enumerate(plans):
        for k, (px, py) in enumerate(peers):
            peer = 2 * px + py
            sems = dict(send_sem=send_sems.at[3 * i + k], recv_sem=recv_sems.at[3 * i + k], device_id=(px, py, c),
                        device_id_type=MESH)
            src = p['send_src'](refs[p['src']], me, peer, k)
            sends.append(pltpu.make_async_remote_copy(src_ref=src, dst_ref=p['send_dst'](refs[p['dst']], me, peer, k), **sems))
            recvs.append(pltpu.make_async_remote_copy(src_ref=src, dst_ref=p['recv_dst'](refs[p['dst']], me, peer, k), **sems))
    return sends, recvs


def _exchange_start(arrays, plans, name, after=()):
    n, m, extra = len(arrays), len(plans), len(after)

    def body(*refs):
        send_sems, recv_sems, token = refs[n + extra], refs[n + extra + 1], refs[-1]
        sends, _ = _remote_copies(refs[:n], send_sems, recv_sems, plans)
        for cp in sends:
            cp.start()
        token[...] = jnp.zeros_like(token)

    outs = pl.pallas_call(
        body, name=name,
        out_shape=(pltpu.SemaphoreType.DMA((3 * m,)), pltpu.SemaphoreType.DMA((3 * m,)),
                   *[pltpu.HBM(a.shape, a.dtype) for a in arrays], S((8, 128), F32)),
        in_specs=[_HBM] * n + [_ANY] * extra, out_specs=(_SEM, _SEM, *[_HBM] * n, pl.BlockSpec(memory_space=pltpu.VMEM)),
        input_output_aliases={i: 2 + i for i in range(n)},
        compiler_params=pltpu.CompilerParams(has_side_effects=_EFFECT))(*[_hbm(a) for a in arrays], *after)
    return outs[0], outs[1], outs[2:2 + n], outs[-1]


def _exchange_wait(started, plans, after, name):
    send_sems, recv_sems, arrays, _ = started
    n = len(arrays)

    def body(*refs):
        sends, recvs = _remote_copies(refs[:n], refs[n], refs[n + 1], plans)
        for cp in sends:
            cp.wait_send()
        for cp in recvs:
            cp.wait_recv()

    return pl.pallas_call(
        body, name=name, out_shape=tuple(pltpu.HBM(a.shape, a.dtype) for a in arrays),
        in_specs=[_HBM] * n + [_SEM, _SEM, _ANY], out_specs=tuple([_HBM] * n),
        input_output_aliases={i: i for i in range(n)},
        compiler_params=pltpu.CompilerParams(has_side_effects=_EFFECT))(*arrays, send_sems, recv_sems, after)


def _add2(p, q):
    r, c = p.shape
    tr = 128 if r % 128 == 0 else r

    def body(p_ref, q_ref, o_ref):
        o_ref[...] = p_ref[...] + q_ref[...]

    return pl.pallas_call(body, grid=(r // tr,), in_specs=[_row_spec(tr, c), _row_spec(tr, c)], out_specs=_row_spec(tr, c),
                          out_shape=S((r, c), F32), name="add_cores", compiler_params=_cp(("parallel",)))(p, q)


def _adamw(w, g, m, v, name, g2=None):
    r, c = w.shape
    tr = 256 if r % 256 == 0 else (128 if r % 128 == 0 else r)
    c1 = 1.0 / (1.0 - ADAM_B1 ** ADAM_STEP)
    c2 = 1.0 / (1.0 - ADAM_B2 ** ADAM_STEP)
    gs = [g] if g2 is None else [g, g2]

    def body(*refs):
        w_ref, m_ref, v_ref = refs[0], refs[1], refs[2]
        g_refs = refs[3:3 + len(gs)]
        go_ref, d_ref, nm_ref, nv_ref = refs[3 + len(gs):]
        gv = g_refs[0][...]
        if len(gs) == 2:
            gv = gv + g_refs[1][...]
        go_ref[...] = gv
        nm = ADAM_B1 * m_ref[...] + (1.0 - ADAM_B1) * gv
        nv = ADAM_B2 * v_ref[...] + (1.0 - ADAM_B2) * (gv * gv)
        nm_ref[...] = nm
        nv_ref[...] = nv
        d_ref[...] = -ADAM_LR * ((nm * c1) / (jnp.sqrt(nv * c2) + ADAM_EPS) + ADAM_WD * w_ref[...])

    return pl.pallas_call(body, grid=(r // tr,), in_specs=[_row_spec(tr, c)] * (3 + len(gs)),
                          out_specs=[_row_spec(tr, c)] * 4, out_shape=[S((r, c), F32)] * 4, name=name,
                          compiler_params=_cp(("parallel",)))(w, m, v, *gs)


_WEIGHTS = ['pre_norm', 'post_norm', 'rel_bias', 'a_w_in', 'a_lam_re', 'a_lam_im', 'a_log_dt', 'a_b_re', 'a_b_im',
            'a_c_re', 'a_c_im', 'a_d', 'a_w_glu', 'a_b_glu', 'a_w_out', 'b_w_in', 'b_sinks', 'b_w_out', 'c_w_in',
            'c_q_norm', 'c_kv_norm', 'c_w_uq', 'c_w_ukv', 'c_w_out', 'd_w_in', 'd_ln_g', 'd_ln_b', 'd_w_s', 'd_b_s',
            'd_w_out']
_BIG = [('a_w_in', 1024, 2048, 1), ('a_w_glu', 1024, 1024, 0), ('a_w_out', 1024, 1024, 0), ('b_w_in', 1024, 2304, 1),
        ('b_w_out', 1024, 1024, 0), ('c_w_in', 1024, 2080, 1), ('c_w_uq', 768, 1536, 1), ('c_w_ukv', 256, 2048, 1),
        ('c_w_out', 1024, 1024, 0), ('d_w_in', 1024, 3072, 1), ('d_w_out', 1024, 1024, 0)]
_SHARDED_VECS = [('c_q_norm', 768), ('c_kv_norm', 256), ('d_ln_g', 1024), ('d_ln_b', 1024)]
_REPLICATED = [n for n in _WEIGHTS if n not in [b[0] for b in _BIG] and n not in [s[0] for s in _SHARDED_VECS]]
_LANES = 1024
_SMALL_Q_ROWS = 136


def kernel(x, pre_norm, post_norm, rel_bias, a_w_in, a_lam_re, a_lam_im, a_log_dt, a_b_re, a_b_im, a_c_re, a_c_im, a_d, a_w_glu, a_b_glu, a_w_out, b_w_in, b_sinks, b_w_out, c_w_in, c_q_norm, c_kv_norm, c_w_uq, c_w_ukv, c_w_out, d_w_in, d_ln_g, d_ln_b, d_w_s, d_b_s, d_w_out, loss_target, m_pre_norm, m_post_norm, m_rel_bias, m_a_w_in, m_a_lam_re, m_a_lam_im, m_a_log_dt, m_a_b_re, m_a_b_im, m_a_c_re, m_a_c_im, m_a_d, m_a_w_glu, m_a_b_glu, m_a_w_out, m_b_w_in, m_b_sinks, m_b_w_out, m_c_w_in, m_c_q_norm, m_c_kv_norm, m_c_w_uq, m_c_w_ukv, m_c_w_out, m_d_w_in, m_d_ln_g, m_d_ln_b, m_d_w_s, m_d_b_s, m_d_w_out, v_pre_norm, v_post_norm, v_rel_bias, v_a_w_in, v_a_lam_re, v_a_lam_im, v_a_log_dt, v_a_b_re, v_a_b_im, v_a_c_re, v_a_c_im, v_a_d, v_a_w_glu, v_a_b_glu, v_a_w_out, v_b_w_in, v_b_sinks, v_b_w_out, v_c_w_in, v_c_q_norm, v_c_kv_norm, v_c_w_uq, v_c_w_ukv, v_c_w_out, v_d_w_in, v_d_ln_g, v_d_ln_b, v_d_w_s, v_d_b_s, v_d_w_out):
    arg = dict(locals())
    me = 2 * lax.axis_index("x") + lax.axis_index("y")
    xin = x[0]
    t = xin.shape[0]

    big = {n: (r, c, dim) for n, r, c, dim in _BIG}

    me1 = me.reshape(1).astype(jnp.int32)

    def gather_start(names, tag, after):
        wholes, plans = [], []
        for i, n in enumerate(names):
            if n in big:
                r, c, dim = big[n]
                src, dtype = arg[n][0], BF16
                if dim == 1 and (c // N_CHIPS) % 128:
                    dim = None
            else:
                src, dim, dtype = arg[n], None, F32
            wholes.append(_cast_place(me1, src, dim, dtype, "place_" + n))
            plans.append(_gather_plan(i, dim))
        return _exchange_start(wholes, plans, "gather_start_" + tag, after=after), plans

    def gather_wait(names, started, after, tag):
        st, plans = started
        out = {}
        for n, a in zip(names, _exchange_wait(st, plans, after, "gather_wait_" + tag)):
            if n in big:
                r, c, dim = big[n]
                out[n] = a if a.ndim == 2 else a.transpose(1, 0, 2).reshape(r, c)
            else:
                out[n] = a.reshape(1, -1)
        return out

    groups = {'a1': ['a_w_in'], 'a2': ['a_w_glu', 'a_w_out'], 'b': ['b_w_in', 'b_w_out'],
              'c': ['c_w_in', 'c_w_uq', 'c_w_ukv', 'c_w_out', 'c_q_norm', 'c_kv_norm'],
              'd': ['d_w_in', 'd_w_out', 'd_ln_g', 'd_ln_b']}
    started, tokens = {}, []
    for tag, names in groups.items():
        started[tag] = gather_start(names, tag, tokens)
        tokens = [started[tag][0][3]]
    wt = gather_wait(groups['a1'], started['a1'], xin, 'a1')
    cos, sin = _rope_tables(t)
    bucket = jnp.asarray(_t5_bucket_table())
    s5p = _s5_params(a_lam_re[0], a_lam_im[0], a_log_dt[0], a_b_re[0], a_b_im[0], a_c_re[0], a_c_im[0])
    bst = d_b_s[0].T

    x0 = xin
    h0 = _prenorm(x0, pre_norm[0:1], "pre0", after=tokens)
    proj0 = _mm(h0, wt['a_w_in'], name="a_in")
    wt.update(gather_wait(groups['a2'], started['a2'], proj0, 'a2'))
    p0, res0 = _s5_fwd(proj0, s5p, a_d, wt['a_w_glu'], a_b_glu)
    y0 = _mm(p0, wt['a_w_out'], name="a_out")
    x1 = _postnorm(x0, y0, post_norm[0:1], "post0")

    wt.update(gather_wait(groups['b'], started['b'], x1, 'b'))
    h1 = _prenorm(x1, pre_norm[1:2], "pre1")
    proj1 = _mm(h1, wt['b_w_in'], name="b_in")
    bias = _swa_bias(rel_bias, bucket)
    p1 = _swa_fwd(proj1, bias, b_sinks)
    y1 = _mm(p1, wt['b_w_out'], name="b_out")
    x2 = _postnorm(x1, y1, post_norm[1:2], "post1")

    wt.update(gather_wait(groups['c'], started['c'], x2, 'c'))
    w_c_in = jnp.concatenate([wt['c_w_in'][:, :1056], jnp.zeros((1024, 96), BF16), wt['c_w_in'][:, 1056:],
                              jnp.zeros((1024, MLA_PROJ - MLA_Z1), BF16)], axis=1)
    w_uq = _perm_uq(wt['c_w_uq'])
    w_ukv = _perm_ukv(wt['c_w_ukv'])
    h2 = _prenorm(x2, pre_norm[2:3], "pre2")
    proj2 = _mm(h2, w_c_in, name="c_in")
    p2, res2 = _mla_fwd(proj2, wt['c_q_norm'], wt['c_kv_norm'], w_uq, w_ukv, cos, sin)
    y2 = _mm(p2, wt['c_w_out'], name="c_out")
    x3 = _postnorm(x2, y2, post_norm[2:3], "post2")

    wt.update(gather_wait(groups['d'], started['d'], x3, 'd'))
    vec = wt
    h3 = _prenorm(x3, pre_norm[3:4], "pre3")
    proj3 = _mm(h3, wt['d_w_in'], name="d_in")
    p3 = _sgu_fwd(proj3, vec['d_ln_g'], vec['d_ln_b'], d_w_s[0], bst)
    y3 = _mm(p3, wt['d_w_out'], name="d_out")
    x4 = _postnorm(x3, y3, post_norm[3:4], "post3")

    loss_part, dx = _loss_head(x4, loss_target[0])
    loss = lax.psum(loss_part[0, 0], ("x", "y", "c"))

    gr = {}
    d_pre, d_post = [None] * 4, [None] * 4
    scattered = []

    def scatter_start(items, tag):
        srcs = [a for _, a, _ in items]
        n = len(items)
        plans = [_scatter_plan(i, n + i, dim) for i, (_, _, dim) in enumerate(items)]
        lands = []
        for _, a, dim in items:
            part = a.shape[1:] if dim is None else tuple(s // N_CHIPS if i == dim else s for i, s in enumerate(a.shape))
            lands.append(lax.empty((3,) + part, a.dtype))
        st = _exchange_start(srcs + lands, plans, "scatter_start_" + tag)
        scattered.append((st, plans, items, tag))
        return [st[3]]

    def stacked(g):
        r, c = g.shape
        return g.reshape(r, N_CHIPS, c // N_CHIPS).transpose(1, 0, 2)

    dy, d_post[3] = _postnorm_bwd(dx, y3, post_norm[3:4], "post3_bwd")
    dp = _mm(dy, wt['d_w_out'], tb=True, name="d_out_dx")
    g_out = _mm(p3, dy, ta=True, out_dtype=BF16, name="d_out_dw")
    dproj, gr['d_w_s'], dbst, gr['d_ln_g'], gr['d_ln_b'] = _sgu_bwd(proj3, dp, vec['d_ln_g'], vec['d_ln_b'], d_w_s[0], bst)
    gr['d_b_s'] = dbst.T
    dh = _mm(dproj, wt['d_w_in'], tb=True, name="d_in_dx")
    g_in = _mm(h3, dproj, ta=True, out_dtype=BF16, name="d_in_dw")
    tok = scatter_start([('d_w_in', g_in, 1), ('d_w_out', g_out, 0)], 'd')
    dx, d_pre[3] = _prenorm_bwd(dx, dh, x3, pre_norm[3:4], "pre3_bwd")

    dy, d_post[2] = _postnorm_bwd(dx, y2, post_norm[2:3], "post2_bwd", after=tok)
    dp = _mm(dy, wt['c_w_out'], tb=True, name="c_out_dx")
    g_out = _mm(p2, dy, ta=True, out_dtype=BF16, name="c_out_dw")
    dproj, gr['c_q_norm'], gr['c_kv_norm'], dwuq, dwukv = _mla_bwd(proj2, res2, dp, wt['c_q_norm'], wt['c_kv_norm'],
                                                                  w_uq, w_ukv, cos, sin)
    dh = _mm(dproj, w_c_in, tb=True, name="c_in_dx")
    dwc = _mm(h2, dproj, ta=True, out_dtype=BF16, name="c_in_dw")
    g_in = stacked(jnp.concatenate([dwc[:, :1056], dwc[:, MLA_Z0:MLA_Z1]], axis=1))
    tok = scatter_start([('c_w_in', g_in, None), ('c_w_uq', _unperm_uq(dwuq), 1), ('c_w_ukv', _unperm_ukv(dwukv), 1),
                         ('c_w_out', g_out, 0)], 'c')
    dx, d_pre[2] = _prenorm_bwd(dx, dh, x2, pre_norm[2:3], "pre2_bwd")

    dy, d_post[1] = _postnorm_bwd(dx, y1, post_norm[1:2], "post1_bwd", after=tok)
    dp = _mm(dy, wt['b_w_out'], tb=True, name="b_out_dx")
    g_out = _mm(p1, dy, ta=True, out_dtype=BF16, name="b_out_dw")
    dqz, dkv, dss, dsk = _swa_bwd(proj1, dp, bias, b_sinks)
    gr['rel_bias'], gr['b_sinks'] = _swa_dbias(dss, dsk, bucket)
    dproj = jnp.concatenate([dqz[:, :1024], dkv.astype(BF16), dqz[:, 1024:]], axis=1)
    dh = _mm(dproj, wt['b_w_in'], tb=True, name="b_in_dx")
    g_in = stacked(_mm(h1, dproj, ta=True, out_dtype=BF16, name="b_in_dw"))
    tok = scatter_start([('b_w_in', g_in, None), ('b_w_out', g_out, 0)], 'b')
    dx, d_pre[1] = _prenorm_bwd(dx, dh, x1, pre_norm[1:2], "pre1_bwd")

    dy, d_post[0] = _postnorm_bwd(dx, y0, post_norm[0:1], "post0_bwd", after=tok)
    dp = _mm(dy, wt['a_w_out'], tb=True, name="a_out_dx")
    g_out = _mm(p0, dy, ta=True, out_dtype=BF16, name="a_out_dw")
    dproj, s5cots, gr['a_d'], g_glu, gr['a_b_glu'] = _s5_bwd(proj0, res0, dp, s5p, a_d, wt['a_w_glu'], a_b_glu)
    tok = scatter_start([('a_w_glu', g_glu, 0), ('a_w_out', g_out, 0)], 'a2')
    g_in = _mm(h0, dproj, ta=True, out_dtype=BF16, name="a_in_dw", after=tok)
    tok = scatter_start([('a_w_in', g_in, 1)], 'a1')
    dh = _mm(dproj, wt['a_w_in'], tb=True, name="a_in_dx", after=tok)
    dx, d_pre[0] = _prenorm_bwd(dx, dh, x0, pre_norm[0:1], "pre0_bwd")
    s5g = _s5_param_grads(s5p, s5cots, after=tok)
    for n, g in zip(['a_lam_re', 'a_lam_im', 'a_log_dt', 'a_b_re', 'a_b_im', 'a_c_re', 'a_c_im'], s5g):
        gr[n] = g
    gr['pre_norm'] = jnp.concatenate(d_pre, axis=0)
    gr['post_norm'] = jnp.concatenate(d_post, axis=0)
    small_names = _REPLICATED + [n for n, _ in _SHARDED_VECS]
    small = jnp.concatenate([gr[n].reshape(-1) for n in small_names])
    small = jnp.pad(small, (0, N_CHIPS * _SMALL_Q_ROWS * _LANES - small.shape[0])).reshape(N_CHIPS, _SMALL_Q_ROWS, _LANES)
    scatter_start([('small', small, None)], 'a0')

    partial = {}
    for st, plans, items, tag in scattered:
        done = _exchange_wait(st, plans, small, "scatter_wait_" + tag)
        for i, (n, _, dim) in enumerate(items):
            partial[n] = _sum_parts(me1, done[i], done[len(items) + i], dim, "sum_parts_" + n)
    order = [n for n, _, _, _ in _BIG] + ['small']
    other = dict(zip(order, _sibling_swap([partial[n] for n in order], "swap_cores")))
    small_q = _add2(partial['small'], other['small'])
    small_all = _all_gather_chips(small_q, "gather_small_grads").reshape(-1)
    grads = {}
    off = 0
    for n in _REPLICATED:
        sz = math.prod(arg[n].shape)
        grads[n] = small_all[off:off + sz].reshape(arg[n].shape)
        off += sz
    for n, sz in _SHARDED_VECS:
        q = sz // N_CHIPS
        grads[n] = lax.dynamic_slice(small_all, (off + me * q,), (q,)).reshape(arg[n].shape)
        off += sz

    delta, new_m, new_v = {}, {}, {}
    for n, _, _, _ in _BIG:
        shp = arg[n].shape
        two = (shp[1], shp[2])
        outs = _adamw(arg[n].reshape(two), partial[n], arg['m_' + n].reshape(two), arg['v_' + n].reshape(two),
                      "adamw_" + n, g2=other[n])
        grads[n], delta[n], new_m[n], new_v[n] = [o.reshape(shp) for o in outs]

    for n in small_names:
        shp = arg[n].shape
        two = (math.prod(shp[:-1]), shp[-1])
        outs = _adamw(arg[n].reshape(two), grads[n].reshape(two), arg['m_' + n].reshape(two), arg['v_' + n].reshape(two),
                      "adamw_" + n)
        delta[n], new_m[n], new_v[n] = [o.reshape(shp) for o in outs[1:]]

    return (loss, dx.reshape(x.shape), *[grads[n] for n in _WEIGHTS], *[delta[n] for n in _WEIGHTS],
            *[new_m[n] for n in _WEIGHTS], *[new_v[n] for n in _WEIGHTS])
```

```python
import functools
import math

import numpy as np
import jax
import jax.numpy as jnp
from jax import lax
from jax.experimental import pallas as pl
from jax.experimental.pallas import tpu as pltpu

F32 = jnp.float32
BF16 = jnp.bfloat16
S = jax.ShapeDtypeStruct
MESH = pl.DeviceIdType.MESH

D_MODEL = 1024
EPS = 1e-6
NEG_INF = -1e30
N_CHIPS = 4

SSM_GROUPS = 64
SSM_GROUP = 16
SSM_STATE = 64
SSM_CHUNK = 16
SSM_BLOCK = SSM_CHUNK * SSM_GROUP

HEAD_DIM = 64
SWA_HEADS = 16
SWA_GROUP = 8
WINDOW = 128
REL_BUCKETS = 32
REL_MAX_DIST = 128

MLA_HEADS = 16
MLA_NOPE = 64
MLA_ROPE = 32
MLA_Q_RANK = 768
MLA_KV_RANK = 256
MLA_SCALE = (MLA_NOPE + MLA_ROPE) ** -0.5
MLA_BQ = 512
MLA_BK = 512
ROPE_BASE = 10000.0

SGU_CHUNK = 128
SGU_GROUPS = 16
SGU_GDIM = 64

ADAM_LR = 0.001
ADAM_B1 = 0.9
ADAM_B2 = 0.999
ADAM_EPS = 1e-08
ADAM_WD = 0.01
ADAM_STEP = 10

VMEM_BIG = 56 * 1024 * 1024


def _cp(sem, vmem=None):
    return pltpu.CompilerParams(dimension_semantics=sem, vmem_limit_bytes=vmem)


def _row_spec(tr, c):
    return pl.BlockSpec((tr, c), lambda i: (i, 0))


def _full_spec(shape):
    nd = len(shape)
    return pl.BlockSpec(shape, lambda *_: (0,) * nd)


def _gelu(x):
    c = 0.7978845608028654
    return 0.5 * x * (1.0 + jnp.tanh(c * (x + 0.044715 * x * x * x)))


def _gelu_grad(x):
    c = 0.7978845608028654
    t = jnp.tanh(c * (x + 0.044715 * x * x * x))
    return 0.5 * (1.0 + t) + 0.5 * x * (1.0 - t * t) * c * (1.0 + 3.0 * 0.044715 * x * x)


def _silu(z):
    return z * jax.nn.sigmoid(z)


def _silu_grad(z):
    s = jax.nn.sigmoid(z)
    return s * (1.0 + z * (1.0 - s))


def _dot(a, b):
    return lax.dot_general(a, b, (((1,), (0,)), ((), ())), preferred_element_type=F32)


def _dot_nt(a, b):
    return lax.dot_general(a, b, (((1,), (1,)), ((), ())), preferred_element_type=F32)


def _dot_tn(a, b):
    return lax.dot_general(a, b, (((0,), (0,)), ((), ())), preferred_element_type=F32)


def _pick(n, cap):
    best = 0
    for d in range(128, min(n, cap) + 1, 128):
        if n % d == 0:
            best = d
    if best < 256:
        return n
    return best


def _mm(a, b, *, ta=False, tb=False, out_dtype=F32, name, after=()):
    m, k = (a.shape[1], a.shape[0]) if ta else a.shape
    n = b.shape[0] if tb else b.shape[1]
    assert (b.shape[1] if tb else b.shape[0]) == k
    tm, tn, tk = _pick(m, 1024), _pick(n, 1024), _pick(k, 1024)
    if tn > 1024:
        tm = _pick(m, 256)
    if tk > 1024:
        tm, tn = _pick(m, 256), _pick(n, 256)
    nk = k // tk
    dims = (((0 if ta else 1,), (1 if tb else 0,)), ((), ()))

    def body(a_ref, b_ref, *rest):
        o_ref, acc_ref = rest[-2], rest[-1]
        kk = pl.program_id(2)

        @pl.when(kk == 0)
        def _():
            acc_ref[...] = jnp.zeros_like(acc_ref)

        acc_ref[...] += lax.dot_general(a_ref[...].astype(BF16), b_ref[...].astype(BF16), dims,
                                        preferred_element_type=F32)

        @pl.when(kk == nk - 1)
        def _():
            o_ref[...] = acc_ref[...].astype(out_dtype)

    a_spec = pl.BlockSpec((tk, tm), lambda i, j, kk: (kk, i)) if ta else pl.BlockSpec((tm, tk), lambda i, j, kk: (i, kk))
    b_spec = pl.BlockSpec((tn, tk), lambda i, j, kk: (j, kk)) if tb else pl.BlockSpec((tk, tn), lambda i, j, kk: (kk, j))
    return pl.pallas_call(
        body, grid=(m // tm, n // tn, nk), in_specs=[a_spec, b_spec] + [pl.BlockSpec(memory_space=pl.ANY)] * len(after),
        out_specs=pl.BlockSpec((tm, tn), lambda i, j, kk: (i, j)), out_shape=S((m, n), out_dtype),
        scratch_shapes=[pltpu.VMEM((tm, tn), F32)], name=name,
        compiler_params=_cp(("parallel", "parallel", "arbitrary"), VMEM_BIG))(a, b, *after)


def _prenorm(x, g, name, after=()):
    t, d = x.shape
    tr = 256

    def body(x_ref, g_ref, *rest):
        o_ref = rest[-1]
        xv = x_ref[...]
        r = lax.rsqrt(jnp.mean(xv * xv, axis=-1, keepdims=True) + EPS)
        o_ref[...] = (xv * r * g_ref[...]).astype(BF16)

    return pl.pallas_call(body, grid=(t // tr,),
                          in_specs=[_row_spec(tr, d), _full_spec((1, d))] + [pl.BlockSpec(memory_space=pl.ANY)] * len(after),
                          out_specs=_row_spec(tr, d), out_shape=S((t, d), BF16), name=name,
                          compiler_params=_cp(("parallel",)))(x, g, *after)


def _postnorm(x, y, g, name):
    t, d = x.shape
    tr = 256

    def body(x_ref, y_ref, g_ref, o_ref):
        yv = y_ref[...]
        r = lax.rsqrt(jnp.mean(yv * yv, axis=-1, keepdims=True) + EPS)
        o_ref[...] = x_ref[...] + yv * r * g_ref[...]

    return pl.pallas_call(body, grid=(t // tr,), in_specs=[_row_spec(tr, d), _row_spec(tr, d), _full_spec((1, d))],
                          out_specs=_row_spec(tr, d), out_shape=S((t, d), F32), name=name,
                          compiler_params=_cp(("parallel",)))(x, y, g)


def _rms_bwd_rows(dout, xin, g):
    r = lax.rsqrt(jnp.mean(xin * xin, axis=-1, keepdims=True) + EPS)
    xh = xin * r
    gd = dout * g
    dx = r * (gd - xh * jnp.mean(gd * xh, axis=-1, keepdims=True))
    return dx, jnp.sum(dout * xh, axis=0, keepdims=True)


def _postnorm_bwd(dxn, y, g, name, after=()):
    t, d = y.shape
    tr = 256

    def body(dx_ref, y_ref, g_ref, *rest):
        dy_ref, dg_ref = rest[-2], rest[-1]

        @pl.when(pl.program_id(0) == 0)
        def _():
            dg_ref[...] = jnp.zeros_like(dg_ref)

        dy, dg = _rms_bwd_rows(dx_ref[...], y_ref[...], g_ref[...])
        dy_ref[...] = dy.astype(BF16)
        dg_ref[...] += dg

    return pl.pallas_call(body, grid=(t // tr,),
                          in_specs=[_row_spec(tr, d), _row_spec(tr, d), _full_spec((1, d))]
                          + [pl.BlockSpec(memory_space=pl.ANY)] * len(after),
                          out_specs=[_row_spec(tr, d), _full_spec((1, d))],
                          out_shape=[S((t, d), BF16), S((1, d), F32)], name=name,
                          compiler_params=_cp(("arbitrary",)))(dxn, y, g, *after)


def _prenorm_bwd(dxn, dh, x, g, name):
    t, d = x.shape
    tr = 256

    def body(dxn_ref, dh_ref, x_ref, g_ref, dx_ref, dg_ref):
        @pl.when(pl.program_id(0) == 0)
        def _():
            dg_ref[...] = jnp.zeros_like(dg_ref)

        dx, dg = _rms_bwd_rows(dh_ref[...], x_ref[...], g_ref[...])
        dx_ref[...] = dxn_ref[...] + dx
        dg_ref[...] += dg

    return pl.pallas_call(body, grid=(t // tr,),
                          in_specs=[_row_spec(tr, d), _row_spec(tr, d), _row_spec(tr, d), _full_spec((1, d))],
                          out_specs=[_row_spec(tr, d), _full_spec((1, d))],
                          out_shape=[S((t, d), F32), S((1, d), F32)], name=name,
                          compiler_params=_cp(("arbitrary",)))(dxn, dh, x, g)


def _loss_head(y, target):
    t, d = y.shape
    tr = 256

    def body(y_ref, t_ref, l_ref, dy_ref):
        @pl.when(pl.program_id(0) == 0)
        def _():
            l_ref[...] = jnp.zeros_like(l_ref)

        e = y_ref[...] - t_ref[...]
        dy_ref[...] = e * (1.0 / d)
        l_ref[...] += 0.5 * jnp.sum(jnp.mean(e * e, axis=-1, keepdims=True), axis=0, keepdims=True)

    return pl.pallas_call(body, grid=(t // tr,), in_specs=[_row_spec(tr, d), _row_spec(tr, d)],
                          out_specs=[_full_spec((1, 1)), _row_spec(tr, d)],
                          out_shape=[S((1, 1), F32), S((t, d), F32)], name="loss_head",
                          compiler_params=_cp(("arbitrary",)))(y, target)


def _sgu_parts(proj, lg, lb, ws_ref, bst):
    a = proj[:, 0:1024]
    bb = proj[:, 1024:2048]
    z = proj[:, 2048:3072]
    u = _gelu(a)
    vp = _gelu(bb)
    mu = jnp.mean(vp, axis=-1, keepdims=True)
    xc = vp - mu
    rstd = lax.rsqrt(jnp.mean(xc * xc, axis=-1, keepdims=True) + EPS)
    xh = xc * rstd
    vn = xh * lg + lb
    row = lax.broadcasted_iota(jnp.int32, (SGU_CHUNK, SGU_CHUNK), 0)
    col = lax.broadcasted_iota(jnp.int32, (SGU_CHUNK, SGU_CHUNK), 1)
    tril = row >= col
    ws = [jnp.where(tril, ws_ref[g], 0.0).astype(BF16) for g in range(SGU_GROUPS)]
    s = jnp.concatenate(
        [_dot(ws[g], vn[:, g * 64:(g + 1) * 64].astype(BF16)) + bst[:, g:g + 1] for g in range(SGU_GROUPS)], axis=1)
    return a, bb, z, u, vp, rstd, xh, vn, ws, tril, s


def _sgu_fwd(proj, lg, lb, w_s, bst):
    t = proj.shape[0]
    tr = SGU_CHUNK

    def body(p_ref, lg_ref, lb_ref, ws_ref, bst_ref, o_ref):
        _, _, z, u, _, _, _, _, _, _, s = _sgu_parts(p_ref[...], lg_ref[...], lb_ref[...], ws_ref, bst_ref[...])
        o_ref[...] = (u * s * _silu(z)).astype(BF16)

    return pl.pallas_call(
        body, grid=(t // tr,),
        in_specs=[_row_spec(tr, 3072), _full_spec((1, 1024)), _full_spec((1, 1024)), _full_spec((16, 128, 128)),
                  _full_spec((128, 16))],
        out_specs=_row_spec(tr, 1024), out_shape=S((t, 1024), BF16), name="sgu_fwd",
        compiler_params=_cp(("parallel",)))(proj, lg, lb, w_s, bst)


def _sgu_bwd(proj, dp, lg, lb, w_s, bst):
    t = proj.shape[0]
    tr = SGU_CHUNK

    def body(p_ref, dp_ref, lg_ref, lb_ref, ws_ref, bst_ref, dproj_ref, dws_ref, dbst_ref, dlg_ref, dlb_ref):
        @pl.when(pl.program_id(0) == 0)
        def _():
            dws_ref[...] = jnp.zeros_like(dws_ref)
            dbst_ref[...] = jnp.zeros_like(dbst_ref)
            dlg_ref[...] = jnp.zeros_like(dlg_ref)
            dlb_ref[...] = jnp.zeros_like(dlb_ref)

        lgv = lg_ref[...]
        a, bb, z, u, vp, rstd, xh, vn, ws, tril, s = _sgu_parts(p_ref[...], lgv, lb_ref[...], ws_ref, bst_ref[...])
        dpv = dp_ref[...].astype(F32)
        sz = _silu(z)
        du = dpv * s * sz
        ds = dpv * u * sz
        dz = dpv * u * s * _silu_grad(z)
        dvn_parts, dbs_cols = [], []
        for g in range(SGU_GROUPS):
            dsg = ds[:, g * 64:(g + 1) * 64]
            dsg_b = dsg.astype(BF16)
            dvn_parts.append(_dot_tn(ws[g], dsg_b))
            dws_ref[g] += jnp.where(tril, _dot_nt(dsg_b, vn[:, g * 64:(g + 1) * 64].astype(BF16)), 0.0)
            dbs_cols.append(jnp.sum(dsg, axis=1, keepdims=True))
        dvn = jnp.concatenate(dvn_parts, axis=1)
        dbst_ref[...] += jnp.concatenate(dbs_cols, axis=1)
        dlg_ref[...] += jnp.sum(dvn * xh, axis=0, keepdims=True)
        dlb_ref[...] += jnp.sum(dvn, axis=0, keepdims=True)
        dxh = dvn * lgv
        dvp = rstd * (dxh - jnp.mean(dxh, axis=-1, keepdims=True) - xh * jnp.mean(dxh * xh, axis=-1, keepdims=True))
        dproj_ref[:, 0:1024] = (du * _gelu_grad(a)).astype(BF16)
        dproj_ref[:, 1024:2048] = (dvp * _gelu_grad(bb)).astype(BF16)
        dproj_ref[:, 2048:3072] = dz.astype(BF16)

    return pl.pallas_call(
        body, grid=(t // tr,),
        in_specs=[_row_spec(tr, 3072), _row_spec(tr, 1024), _full_spec((1, 1024)), _full_spec((1, 1024)),
                  _full_spec((16, 128, 128)), _full_spec((128, 16))],
        out_specs=[_row_spec(tr, 3072), _full_spec((16, 128, 128)), _full_spec((128, 16)), _full_spec((1, 1024)),
                   _full_spec((1, 1024))],
        out_shape=[S((t, 3072), BF16), S((16, 128, 128), F32), S((128, 16), F32), S((1, 1024), F32),
                   S((1, 1024), F32)],
        name="sgu_bwd", compiler_params=_cp(("arbitrary",)))(proj, dp, lg, lb, w_s, bst)


def _t5_bucket_table():
    qi = np.arange(WINDOW)[:, None]
    kj = np.arange(2 * WINDOW)[None, :]
    dist = np.maximum(qi + WINDOW - kj, 0)
    max_exact = REL_BUCKETS // 2
    dist_f = np.maximum(dist, 1).astype(np.float32)
    large = max_exact + (np.log(dist_f / np.float32(max_exact)) / np.float32(math.log(REL_MAX_DIST / max_exact))
                         * np.float32(REL_BUCKETS - max_exact)).astype(np.int32)
    large = np.minimum(large, REL_BUCKETS - 1)
    return np.where(dist < max_exact, dist, large).astype(np.int32)


def _swa_bias(rel_bias, bucket):
    def body(rb_ref, bk_ref, o_ref):
        bk = bk_ref[...]
        for h in range(SWA_HEADS):
            acc = jnp.zeros((WINDOW, 2 * WINDOW), F32)
            for b in range(REL_BUCKETS):
                acc = jnp.where(bk == b, rb_ref[b, h], acc)
            o_ref[h] = acc

    return pl.pallas_call(
        body, in_specs=[pl.BlockSpec(memory_space=pltpu.SMEM), pl.BlockSpec(memory_space=pltpu.VMEM)],
        out_specs=pl.BlockSpec(memory_space=pltpu.VMEM), out_shape=S((SWA_HEADS, WINDOW, 2 * WINDOW), F32),
        name="swa_bias")(rel_bias, bucket)


def _swa_dbias(ds_sum, dsink_rows, bucket):
    def body(ds_ref, dsk_ref, bk_ref, drb_ref, dsink_ref):
        bk = bk_ref[...]
        r = lax.broadcasted_iota(jnp.int32, (REL_BUCKETS, SWA_HEADS), 0)
        c = lax.broadcasted_iota(jnp.int32, (REL_BUCKETS, SWA_HEADS), 1)
        out = jnp.zeros((REL_BUCKETS, SWA_HEADS), F32)
        for b in range(REL_BUCKETS):
            m = bk == b
            for h in range(SWA_HEADS):
                val = jnp.sum(jnp.where(m, ds_ref[h], 0.0))
                out = jnp.where((r == b) & (c == h), val, out)
        drb_ref[...] = out
        dsink_ref[...] = jnp.sum(dsk_ref[...], axis=0, keepdims=True)

    return pl.pallas_call(
        body, out_shape=[S((REL_BUCKETS, SWA_HEADS), F32), S((1, SWA_HEADS), F32)], name="swa_dbias",
    )(ds_sum, dsink_rows, bucket)


def _swa_band(cur_ref, prev_ref):
    kband = jnp.concatenate([prev_ref[:, 0:128], cur_ref[:, 1024:1152]], axis=0).astype(BF16)
    vband = jnp.concatenate([prev_ref[:, 128:256], cur_ref[:, 1152:1280]], axis=0).astype(BF16)
    return kband, vband


SWA_ROWS = SWA_GROUP * WINDOW


def _swa_valid(n, rows=SWA_ROWS):
    qi = lax.broadcasted_iota(jnp.int32, (rows, 2 * WINDOW), 0) & (WINDOW - 1)
    kj = lax.broadcasted_iota(jnp.int32, (rows, 2 * WINDOW), 1)
    dist = qi + WINDOW - kj
    return (dist >= 0) & (dist < WINDOW) & (n * WINDOW + kj - WINDOW >= 0)


def _swa_stack(x, j):
    return jnp.concatenate([x[:, h * 64:(h + 1) * 64] for h in range(j * SWA_GROUP, (j + 1) * SWA_GROUP)], axis=0)


def _swa_unstack(parts):
    return jnp.concatenate([p[i * WINDOW:(i + 1) * WINDOW] for p in parts for i in range(SWA_GROUP)], axis=1)


def _swa_sink_col(sink_ref, j):
    return jnp.concatenate([jnp.full((WINDOW, 1), sink_ref[0, h], F32)
                            for h in range(j * SWA_GROUP, (j + 1) * SWA_GROUP)], axis=0)


def _swa_probs(qh, kh, bias_h, sink, valid):
    s = _dot_nt(qh, kh) * (HEAD_DIM ** -0.5) + bias_h
    s = jnp.where(valid, s, NEG_INF)
    m = jnp.maximum(jnp.max(s, axis=-1, keepdims=True), sink)
    e = jnp.exp(s - m)
    es = jnp.exp(sink - m)
    inv = 1.0 / (jnp.sum(e, axis=-1, keepdims=True) + es)
    return e * inv, es * inv


def _swa_in_specs():
    return [_row_spec(WINDOW, 2304),
            pl.BlockSpec((WINDOW, 256), lambda n: (jnp.maximum(n - 1, 0), 4)),
            _full_spec((SWA_HEADS, WINDOW, 2 * WINDOW)),
            pl.BlockSpec(memory_space=pltpu.SMEM)]


def _swa_fwd(proj, bias, sinks):
    t = proj.shape[0]

    def body(cur_ref, prev_ref, bias_ref, sink_ref, o_ref):
        n = pl.program_id(0)
        kband, vband = _swa_band(cur_ref, prev_ref)
        valid = _swa_valid(n, WINDOW)
        outs = []
        for h in range(SWA_HEADS):
            j = h // SWA_GROUP
            qh = cur_ref[:, h * 64:(h + 1) * 64].astype(BF16)
            p, _ = _swa_probs(qh, kband[:, j * 64:(j + 1) * 64], bias_ref[h], sink_ref[0, h], valid)
            outs.append(_dot(p.astype(BF16), vband[:, j * 64:(j + 1) * 64]))
        o_ref[...] = (jnp.concatenate(outs, axis=1) * _silu(cur_ref[:, 1280:2304])).astype(BF16)

    return pl.pallas_call(body, grid=(t // WINDOW,), in_specs=_swa_in_specs(), out_specs=_row_spec(WINDOW, 1024),
                          out_shape=S((t, 1024), BF16), name="swa_fwd",
                          compiler_params=_cp(("parallel",)))(proj, proj, bias, sinks)


def _swa_bwd(proj, dp, bias, sinks):
    t = proj.shape[0]
    nb = t // WINDOW

    def body(cur_ref, prev_ref, bias_ref, sink_ref, dp_ref, dqz_ref, dkv_ref, dss_ref, dsk_ref):
        n = pl.program_id(0)

        @pl.when(n == 0)
        def _():
            dkv_ref[...] = jnp.zeros_like(dkv_ref)
            dss_ref[...] = jnp.zeros_like(dss_ref)
            dsk_ref[...] = jnp.zeros_like(dsk_ref)

        kband, vband = _swa_band(cur_ref, prev_ref)
        valid = _swa_valid(n)
        z = cur_ref[:, 1280:2304]
        dpv = dp_ref[...].astype(F32)
        do = dpv * _silu(z)
        q = cur_ref[:, 0:1024].astype(BF16)
        dob = do.astype(BF16)
        outs, dqs, dsinks, dk, dv = [], [], [], [], []
        for j in range(SWA_HEADS // SWA_GROUP):
            heads = slice(j * SWA_GROUP, (j + 1) * SWA_GROUP)
            qs, dos = _swa_stack(q, j), _swa_stack(dob, j)
            kh = kband[:, j * 64:(j + 1) * 64]
            vh = vband[:, j * 64:(j + 1) * 64]
            p, ps = _swa_probs(qs, kh, bias_ref[heads].reshape(SWA_ROWS, 2 * WINDOW), _swa_sink_col(sink_ref, j), valid)
            pb = p.astype(BF16)
            outs.append(_dot(pb, vh))
            dpr = _dot_nt(dos, vh)
            dlt = jnp.sum(p * dpr, axis=-1, keepdims=True)
            ds = p * (dpr - dlt)
            dsk = -ps * dlt
            dsinks += [dsk[i * WINDOW:(i + 1) * WINDOW] for i in range(SWA_GROUP)]
            dss_ref[heads] += ds.reshape(SWA_GROUP, WINDOW, 2 * WINDOW)
            dsb = (ds * (HEAD_DIM ** -0.5)).astype(BF16)
            dqs.append(_dot(dsb, kh))
            dk.append(_dot_tn(dsb, qs))
            dv.append(_dot_tn(pb, dos))
        dqz_ref[:, 0:1024] = _swa_unstack(dqs).astype(BF16)
        dqz_ref[:, 1024:2048] = (dpv * _swa_unstack(outs) * _silu_grad(z)).astype(BF16)
        dsk_ref[...] += jnp.concatenate(dsinks, axis=1)
        dband = jnp.concatenate([dk[0], dk[1], dv[0], dv[1]], axis=1)
        prow = pl.multiple_of(jnp.maximum(n - 1, 0) * WINDOW, WINDOW)
        dkv_ref[pl.ds(prow, WINDOW), :] += dband[0:WINDOW]
        crow = pl.multiple_of(n * WINDOW, WINDOW)
        dkv_ref[pl.ds(crow, WINDOW), :] += dband[WINDOW:]

    return pl.pallas_call(
        body, grid=(nb,), in_specs=_swa_in_specs() + [_row_spec(WINDOW, 1024)],
        out_specs=[_row_spec(WINDOW, 2048), _full_spec((t, 256)), _full_spec((SWA_HEADS, WINDOW, 2 * WINDOW)),
                   _full_spec((WINDOW, SWA_HEADS))],
        out_shape=[S((t, 2048), BF16), S((t, 256), F32), S((SWA_HEADS, WINDOW, 2 * WINDOW), F32),
                   S((WINDOW, SWA_HEADS), F32)],
        name="swa_bwd", compiler_params=_cp(("arbitrary",)))(proj, proj, bias, sinks, dp)


MLA_Z0 = MLA_Q_RANK + MLA_KV_RANK + 128
MLA_Z1 = MLA_Z0 + 1024
MLA_PROJ = MLA_Z1 + 128


def _rope_tables(t):
    inv = ROPE_BASE ** (-jnp.arange(0, MLA_ROPE, 2, dtype=F32) / MLA_ROPE)
    ang = jnp.arange(t, dtype=F32)[:, None] * inv[None, :]
    return jnp.tile(jnp.cos(ang), (1, MLA_HEADS)), jnp.tile(jnp.sin(ang), (1, MLA_HEADS))


def _mla_norms(proj, gq, gkv):
    t = proj.shape[0]
    tr = 256

    def body(p_ref, gq_ref, gkv_ref, q_ref, kv_ref):
        for lo, hi, g_ref, o_ref in ((0, 768, gq_ref, q_ref), (768, 1024, gkv_ref, kv_ref)):
            xv = p_ref[:, lo:hi]
            r = lax.rsqrt(jnp.mean(xv * xv, axis=-1, keepdims=True) + EPS)
            o_ref[...] = (xv * r * g_ref[...]).astype(BF16)

    return pl.pallas_call(
        body, grid=(t // tr,), in_specs=[_row_spec(tr, 1024), _full_spec((1, 768)), _full_spec((1, 256))],
        out_specs=[_row_spec(tr, 768), _row_spec(tr, 256)], out_shape=[S((t, 768), BF16), S((t, 256), BF16)],
        name="mla_norms", compiler_params=_cp(("parallel",)))(proj, gq, gkv)


def _mla_norms_bwd(proj, dcqn, dckvn, dkr, dz, gq, gkv):
    t = proj.shape[0]
    tr = 256

    def body(p_ref, dq_ref, dkv_ref, dkr_ref, dz_ref, gq_ref, gkv_ref, dproj_ref, dgq_ref, dgkv_ref):
        @pl.when(pl.program_id(0) == 0)
        def _():
            dgq_ref[...] = jnp.zeros_like(dgq_ref)
            dgkv_ref[...] = jnp.zeros_like(dgkv_ref)

        dx, dg = _rms_bwd_rows(dq_ref[...], p_ref[:, 0:768], gq_ref[...])
        dproj_ref[:, 0:768] = dx.astype(BF16)
        dgq_ref[...] += dg
        dx, dg = _rms_bwd_rows(dkv_ref[...], p_ref[:, 768:1024], gkv_ref[...])
        dproj_ref[:, 768:1024] = dx.astype(BF16)
        dgkv_ref[...] += dg
        dproj_ref[:, 1024:MLA_Z0] = dkr_ref[...].astype(BF16)
        dproj_ref[:, MLA_Z0:MLA_Z1] = dz_ref[...]
        dproj_ref[:, MLA_Z1:MLA_PROJ] = jnp.zeros((tr, MLA_PROJ - MLA_Z1), BF16)

    return pl.pallas_call(
        body, grid=(t // tr,),
        in_specs=[_row_spec(tr, 1024), _row_spec(tr, 768), _row_spec(tr, 256), _row_spec(tr, 128), _row_spec(tr, 1024),
                  _full_spec((1, 768)), _full_spec((1, 256))],
        out_specs=[_row_spec(tr, MLA_PROJ), _full_spec((1, 768)), _full_spec((1, 256))],
        out_shape=[S((t, MLA_PROJ), BF16), S((1, 768), F32), S((1, 256), F32)],
        name="mla_norms_bwd", compiler_params=_cp(("arbitrary",)))(proj, dcqn, dckvn, dkr, dz, gq, gkv)


def _mla_pack(qf, kvf, proj, cos, sin):
    t = qf.shape[0]
    tr = 256

    def body(q_ref, kn_ref, kr_ref, cos_ref, sin_ref, qh_ref, kh_ref):
        cs, sn = cos_ref[...], sin_ref[...]
        x1, x2 = q_ref[:, 1024:1280], q_ref[:, 1280:1536]
        r1 = x1 * cs - x2 * sn
        r2 = x2 * cs + x1 * sn
        c16, s16 = cs[:, 0:16], sn[:, 0:16]
        k1, k2 = kr_ref[:, 0:16], kr_ref[:, 16:32]
        kr1 = k1 * c16 - k2 * s16
        kr2 = k2 * c16 + k1 * s16
        zpad = jnp.zeros((tr, 32), F32)
        for h in range(MLA_HEADS):
            qh_ref[h] = jnp.concatenate([q_ref[:, h * 64:(h + 1) * 64], r1[:, h * 16:(h + 1) * 16],
                                         r2[:, h * 16:(h + 1) * 16], zpad], axis=1).astype(BF16)
            kh_ref[h] = jnp.concatenate([kn_ref[:, h * 64:(h + 1) * 64], kr1, kr2, zpad], axis=1).astype(BF16)

    hspec = pl.BlockSpec((MLA_HEADS, tr, 128), lambda i: (0, i, 0))
    return pl.pallas_call(
        body, grid=(t // tr,),
        in_specs=[_row_spec(tr, 1536), _row_spec(tr, 1024), pl.BlockSpec((tr, 128), lambda i: (i, 8)),
                  _row_spec(tr, 256), _row_spec(tr, 256)],
        out_specs=[hspec, hspec], out_shape=[S((MLA_HEADS, t, 128), BF16)] * 2,
        name="mla_pack", compiler_params=_cp(("parallel",)))(qf, kvf, proj, cos, sin)


def _mla_unpack(dqh, dkh, dv, cos, sin):
    t = dqh.shape[1]
    tr = 256

    def body(dqh_ref, dkh_ref, dv_ref, cos_ref, sin_ref, dq_ref, dkv_ref, dkr_ref):
        cs, sn = cos_ref[...], sin_ref[...]
        dqn = jnp.concatenate([dqh_ref[h, :, 0:64] for h in range(MLA_HEADS)], axis=1)
        dr1 = jnp.concatenate([dqh_ref[h, :, 64:80] for h in range(MLA_HEADS)], axis=1)
        dr2 = jnp.concatenate([dqh_ref[h, :, 80:96] for h in range(MLA_HEADS)], axis=1)
        dq_ref[:, 0:1024] = dqn.astype(BF16)
        dq_ref[:, 1024:1280] = (dr1 * cs + dr2 * sn).astype(BF16)
        dq_ref[:, 1280:1536] = (dr2 * cs - dr1 * sn).astype(BF16)
        dkv_ref[:, 0:1024] = jnp.concatenate([dkh_ref[h, :, 0:64] for h in range(MLA_HEADS)], axis=1).astype(BF16)
        dkv_ref[:, 1024:2048] = dv_ref[...].astype(BF16)
        d1 = dkh_ref[0, :, 64:80]
        d2 = dkh_ref[0, :, 80:96]
        for h in range(1, MLA_HEADS):
            d1 = d1 + dkh_ref[h, :, 64:80]
            d2 = d2 + dkh_ref[h, :, 80:96]
        c16, s16 = cs[:, 0:16], sn[:, 0:16]
        dkr_ref[...] = jnp.concatenate([d1 * c16 + d2 * s16, d2 * c16 - d1 * s16, jnp.zeros((tr, 96), F32)], axis=1)

    hspec = pl.BlockSpec((MLA_HEADS, tr, 128), lambda i: (0, i, 0))
    return pl.pallas_call(
        body, grid=(t // tr,), in_specs=[hspec, hspec, _row_spec(tr, 1024), _row_spec(tr, 256), _row_spec(tr, 256)],
        out_specs=[_row_spec(tr, 1536), _row_spec(tr, 2048), _row_spec(tr, 128)],
        out_shape=[S((t, 1536), BF16), S((t, 2048), BF16), S((t, 128), F32)],
        name="mla_unpack", compiler_params=_cp(("parallel",)))(dqh, dkh, dv, cos, sin)


def _mla_attn_fwd(qh, kh, kvf, proj):
    t = qh.shape[1]
    b, bk = MLA_BQ, MLA_BK

    def body(q_ref, k_ref, v_ref, z_ref, o_ref, p_ref, lse_ref):
        r0 = pl.program_id(1) * b
        nfull = r0 // bk
        diff = (lax.broadcasted_iota(jnp.int32, (b, bk), 1) - lax.broadcasted_iota(jnp.int32, (b, bk), 0))
        qs = [q_ref[0], q_ref[1]]

        def block(c0, carry, masked):
            new = []
            for a in range(2):
                m, l, acc = carry[a]
                k = k_ref[a, pl.ds(c0, bk), :]
                v = v_ref[pl.ds(c0, bk), a * 64:(a + 1) * 64].astype(BF16)
                s = _dot_nt(qs[a], k) * MLA_SCALE
                if masked:
                    s = jnp.where(diff <= r0 - c0, s, NEG_INF)
                mn = jnp.maximum(m, jnp.max(s, axis=-1, keepdims=True))
                alpha = jnp.exp(m - mn)
                p = jnp.exp(s - mn)
                new.append((mn, alpha * l + jnp.sum(p, axis=-1, keepdims=True), alpha * acc + _dot(p.astype(BF16), v)))
            return tuple(new)

        init = (jnp.full((b, 1), NEG_INF, F32), jnp.zeros((b, 1), F32), jnp.zeros((b, 64), F32))
        carry = lax.fori_loop(0, nfull, lambda c, cr: block(pl.multiple_of(c * bk, bk), cr, False), (init, init))
        for i in range(max(1, b // bk)):
            carry = block(pl.multiple_of((nfull + i) * bk, bk), carry, True)
        outs = [acc / l for _, l, acc in carry]
        lses = [jnp.broadcast_to(m + jnp.log(l), (b, 64)) for m, l, _ in carry]
        o = jnp.concatenate(outs, axis=1)
        o_ref[...] = o
        p_ref[...] = (o * _silu(z_ref[...])).astype(BF16)
        lse_ref[0] = jnp.concatenate(lses, axis=1)

    return pl.pallas_call(
        body, grid=(MLA_HEADS // 2, t // b),
        in_specs=[pl.BlockSpec((2, b, 128), lambda i, r: (i, r, 0)), pl.BlockSpec((2, t, 128), lambda i, r: (i, 0, 0)),
                  pl.BlockSpec((t, 128), lambda i, r: (0, 8 + i)), pl.BlockSpec((b, 128), lambda i, r: (r, 9 + i))],
        out_specs=[pl.BlockSpec((b, 128), lambda i, r: (r, i)), pl.BlockSpec((b, 128), lambda i, r: (r, i)),
                   pl.BlockSpec((1, b, 128), lambda i, r: (i, r, 0))],
        out_shape=[S((t, 1024), F32), S((t, 1024), BF16), S((MLA_HEADS // 2, t, 128), F32)],
        name="mla_attn_fwd", compiler_params=_cp(("parallel", "parallel")))(qh, kh, kvf, proj)


def _mla_attn_bwd(qh, kh, kvf, proj, o, dpout, lse):
    t = qh.shape[1]
    b, bk = MLA_BQ, MLA_BK
    nb = t // b

    def body(q_ref, k_ref, v_ref, z_ref, o_ref, dp_ref, lse_ref, dq_ref, dk_ref, dv_ref, dz_ref):
        dk_ref[...] = jnp.zeros_like(dk_ref)
        dv_ref[...] = jnp.zeros_like(dv_ref)
        diff = (lax.broadcasted_iota(jnp.int32, (b, bk), 1) - lax.broadcasted_iota(jnp.int32, (b, bk), 0))
        lanes = [slice(0, 64), slice(64, 128)]

        def rbody(r, carry):
            r0 = pl.multiple_of(r * b, b)
            rows = pl.ds(r0, b)
            qs, dobs, deltas, lses = [], [], [], []
            for a in range(2):
                zz = z_ref[rows, lanes[a]]
                oo = o_ref[rows, lanes[a]]
                dpo = dp_ref[rows, lanes[a]].astype(F32)
                do = dpo * _silu(zz)
                dz_ref[rows, lanes[a]] = (dpo * oo * _silu_grad(zz)).astype(BF16)
                qs.append(q_ref[a, rows, :])
                dobs.append(do.astype(BF16))
                deltas.append(jnp.sum(do * oo, axis=-1, keepdims=True))
                lses.append(lse_ref[0, rows, a * 64:a * 64 + 1])

            def block(c0, dqs, masked):
                cols = pl.ds(c0, bk)
                out = []
                for a in range(2):
                    k = k_ref[a, cols, :]
                    v = v_ref[cols, lanes[a]].astype(BF16)
                    s = _dot_nt(qs[a], k) * MLA_SCALE
                    if masked:
                        s = jnp.where(diff <= r0 - c0, s, NEG_INF)
                    p = jnp.exp(s - lses[a])
                    ds = p * (_dot_nt(dobs[a], v) - deltas[a]) * MLA_SCALE
                    dsb = ds.astype(BF16)
                    dk_ref[a, cols, :] += _dot_tn(dsb, qs[a])
                    dv_ref[cols, lanes[a]] += _dot_tn(p.astype(BF16), dobs[a])
                    out.append(dqs[a] + _dot(dsb, k))
                return tuple(out)

            zero = jnp.zeros((b, 128), F32)
            nfull = r0 // bk
            dqs = lax.fori_loop(0, nfull, lambda c, d: block(pl.multiple_of(c * bk, bk), d, False), (zero, zero))
            for i in range(max(1, b // bk)):
                dqs = block(pl.multiple_of((nfull + i) * bk, bk), dqs, True)
            dq_ref[0, rows, :] = dqs[0]
            dq_ref[1, rows, :] = dqs[1]
            return carry

        lax.fori_loop(0, nb, rbody, 0)

    pair3 = pl.BlockSpec((2, t, 128), lambda i: (i, 0, 0))
    return pl.pallas_call(
        body, grid=(MLA_HEADS // 2,),
        in_specs=[pair3, pair3, pl.BlockSpec((t, 128), lambda i: (0, 8 + i)), pl.BlockSpec((t, 128), lambda i: (0, 9 + i)),
                  pl.BlockSpec((t, 128), lambda i: (0, i)), pl.BlockSpec((t, 128), lambda i: (0, i)),
                  pl.BlockSpec((1, t, 128), lambda i: (i, 0, 0))],
        out_specs=[pair3, pair3, pl.BlockSpec((t, 128), lambda i: (0, i)), pl.BlockSpec((t, 128), lambda i: (0, i))],
        out_shape=[S((MLA_HEADS, t, 128), F32), S((MLA_HEADS, t, 128), F32), S((t, 1024), F32), S((t, 1024), BF16)],
        name="mla_attn_bwd", compiler_params=_cp(("parallel",), VMEM_BIG))(qh, kh, kvf, proj, o, dpout, lse)


def _dot_nt_hi(a, b):
    return lax.dot_general(a, b, (((1,), (1,)), ((), ())), preferred_element_type=F32,
                           precision=lax.Precision.HIGHEST)


SSM_PACK = 2


def _ssm_gen(lr, li, ldt, btr, bti, cr, ci):
    n = SSM_BLOCK
    lanes = SSM_PACK * SSM_STATE
    dt = jnp.exp(ldt)
    x = lr * dt
    w = li * dt
    mag = jnp.exp(x)
    ab_re = mag * jnp.cos(w)
    ab_im = mag * jnp.sin(w)
    den = lr * lr + li * li
    nr = ab_re - 1.0
    f_re = (nr * lr + ab_im * li) / den
    f_im = (ab_im * lr - nr * li) / den
    bb_re = f_re * btr - f_im * bti
    bb_im = f_re * bti + f_im * btr
    inv_mag = jnp.exp(-x)
    inv_re, inv_im = inv_mag * jnp.cos(w), -inv_mag * jnp.sin(w)
    one, zero = jnp.ones_like(x), jnp.zeros_like(x)
    pos, neg = [(one, zero)], [(one, zero)]
    for _ in range(SSM_CHUNK):
        pr, pi = pos[-1]
        pos.append((pr * ab_re - pi * ab_im, pr * ab_im + pi * ab_re))
        pr, pi = neg[-1]
        neg.append((pr * inv_re - pi * inv_im, pr * inv_im + pi * inv_re))

    def stack(table, idx):
        return (jnp.concatenate([jnp.broadcast_to(table[idx(t)][0], (SSM_GROUP, lanes)) for t in range(SSM_CHUNK)], axis=0),
                jnp.concatenate([jnp.broadcast_to(table[idx(t)][1], (SSM_GROUP, lanes)) for t in range(SSM_CHUNK)], axis=0))

    def rep(v):
        return jnp.concatenate([v] * SSM_CHUNK, axis=0)

    ct_re, ct_im, bt_re, bt_im = rep(cr), rep(ci), rep(bb_re), rep(bb_im)
    p_re, p_im = stack(pos, lambda t: t)
    l_re = ct_re * p_re - ct_im * p_im
    l_im = ct_re * p_im + ct_im * p_re
    n_re, n_im = stack(neg, lambda t: t)
    r_re = bt_re * n_re - bt_im * n_im
    r_im = bt_re * n_im + bt_im * n_re
    rk = lax.broadcasted_iota(jnp.int32, (n, n), 0) // SSM_GROUP
    ct = lax.broadcasted_iota(jnp.int32, (n, n), 1) // SSM_GROUP
    mts = []
    for g in range(SSM_PACK):
        sl = slice(g * SSM_STATE, (g + 1) * SSM_STATE)
        mts.append(jnp.where(ct >= rk, _dot_nt_hi(r_re[:, sl], l_re[:, sl]) - _dot_nt_hi(r_im[:, sl], l_im[:, sl]), 0.0))
    e_re, e_im = stack(pos, lambda t: SSM_CHUNK - 1 - t)
    pin_re = bt_re * e_re - bt_im * e_im
    pin_im = bt_re * e_im + bt_im * e_re
    q_re, q_im = stack(pos, lambda t: t + 1)
    qt_re = ct_re * q_re - ct_im * q_im
    qt_im = -(ct_re * q_im + ct_im * q_re)
    ad_re, ad_im = pos[SSM_CHUNK]
    return tuple(mts), pin_re, pin_im, qt_re, qt_im, ad_re, ad_im


_SSM_GEN_OUT = [(SSM_BLOCK, SSM_BLOCK)] + [(SSM_BLOCK, SSM_STATE)] * 4 + [(1, SSM_STATE)] * 2
_SSM_GEN_IN = [(1, SSM_STATE)] * 3 + [(SSM_GROUP, SSM_STATE)] * 4


def _gspec(shape2):
    return pl.BlockSpec((SSM_PACK,) + shape2, lambda g: (g, 0, 0))


def _pack_lanes(ref):
    return jnp.concatenate([ref[g] for g in range(SSM_PACK)], axis=1)


def _unpack_lanes(ref, val):
    for g in range(SSM_PACK):
        ref[g] = val[:, g * SSM_STATE:(g + 1) * SSM_STATE]


def _ssm_operators(params):
    def body(*refs):
        ins, outs = refs[:7], refs[7:]
        vals = _ssm_gen(*[_pack_lanes(r) for r in ins])
        for g in range(SSM_PACK):
            outs[0][g] = vals[0][g]
        for o_ref, val in zip(outs[1:], vals[1:]):
            _unpack_lanes(o_ref, val)

    return pl.pallas_call(
        body, grid=(SSM_GROUPS // SSM_PACK,), in_specs=[_gspec(s) for s in _SSM_GEN_IN],
        out_specs=[_gspec(s) for s in _SSM_GEN_OUT],
        out_shape=[S((SSM_GROUPS,) + s, F32) for s in _SSM_GEN_OUT], name="ssm_operators",
        compiler_params=_cp(("parallel",)))(*params)


def _ssm_operators_bwd(params, cots, after=()):
    def body(*refs):
        ins, cts, outs = refs[:7], refs[7:14], refs[14 + len(after):]
        _, vjp = jax.vjp(_ssm_gen, *[_pack_lanes(r) for r in ins])
        grads = list(vjp((tuple(cts[0][g] for g in range(SSM_PACK)),) + tuple(_pack_lanes(r) for r in cts[1:])))
        for o_ref, val in zip(outs, grads):
            _unpack_lanes(o_ref, val)
        for g in range(SSM_PACK):
            dldt = grads[2][:, g * SSM_STATE:(g + 1) * SSM_STATE]
            outs[2][g] = jnp.broadcast_to(jnp.sum(dldt, axis=-1, keepdims=True), (1, SSM_STATE))

    return pl.pallas_call(
        body, grid=(SSM_GROUPS // SSM_PACK,),
        in_specs=[_gspec(s) for s in _SSM_GEN_IN] + [_gspec(s) for s in _SSM_GEN_OUT]
        + [pl.BlockSpec(memory_space=pl.ANY)] * len(after),
        out_specs=[_gspec(s) for s in _SSM_GEN_IN], out_shape=[S((SSM_GROUPS,) + s, F32) for s in _SSM_GEN_IN],
        name="ssm_operators_bwd", compiler_params=_cp(("parallel",)))(*params, *cots, *after)


def _shift_rows(x, sh, row, up):
    n = x.shape[0]
    if up:
        return jnp.where(row < n - sh, pltpu.roll(x, n - sh, axis=0), 0.0)
    return jnp.where(row >= sh, pltpu.roll(x, sh, axis=0), 0.0)


def _carry_scan(sr, si, ar, ai, row, up):
    n = sr.shape[0]
    sh = 1
    while sh < n:
        tr, ti = _shift_rows(sr, sh, row, up), _shift_rows(si, sh, row, up)
        sr, si = sr + ar * tr - ai * ti, si + ar * ti + ai * tr
        ar, ai = ar * ar - ai * ai, 2.0 * ar * ai
        sh *= 2
    return sr, si


def _ssm_states(u, p_re, p_im, ar, ai, row):
    lre = _dot(u, p_re.astype(BF16))
    lim = _dot(u, p_im.astype(BF16))
    sr, si = _carry_scan(lre, lim, ar, ai, row, False)
    return _shift_rows(sr, 1, row, False), _shift_rows(si, 1, row, False)


SSM_GPB = 128 // SSM_GROUP


def _step_rows(ref, nc):
    return [ref[pl.ds(k, nc, stride=SSM_CHUNK), :].astype(BF16) for k in range(SSM_CHUNK)]


def _group_of(rows, j):
    return jnp.concatenate([r[:, j * SSM_GROUP:(j + 1) * SSM_GROUP] for r in rows], axis=1)


def _store_groups(o_ref, per_group, nc):
    for k in range(SSM_CHUNK):
        o_ref[pl.ds(k, nc, stride=SSM_CHUNK), :] = jnp.concatenate(
            [y[:, k * SSM_GROUP:(k + 1) * SSM_GROUP] for y in per_group], axis=1)


def _bspec(shape2):
    return pl.BlockSpec((SSM_GPB,) + shape2, lambda j: (j, 0, 0))


def _ssm_core_fwd(proj, ops):
    t = proj.shape[0]
    nc = t // SSM_CHUNK

    def body(u_ref, mt_ref, pr_ref, pi_ref, qr_ref, qi_ref, ar_ref, ai_ref, y_ref):
        rows = _step_rows(u_ref, nc)
        row = lax.broadcasted_iota(jnp.int32, (nc, SSM_STATE), 0)
        ys = []
        for j in range(SSM_GPB):
            u = _group_of(rows, j)
            s_re, s_im = _ssm_states(u, pr_ref[j], pi_ref[j], ar_ref[j], ai_ref[j], row)
            ys.append(_dot(u, mt_ref[j].astype(BF16)) + _dot_nt(s_re.astype(BF16), qr_ref[j].astype(BF16))
                      + _dot_nt(s_im.astype(BF16), qi_ref[j].astype(BF16)))
        _store_groups(y_ref, ys, nc)

    blk = pl.BlockSpec((t, 128), lambda j: (0, j))
    return pl.pallas_call(
        body, grid=(SSM_GROUPS // SSM_GPB,), in_specs=[blk] + [_bspec(s) for s in _SSM_GEN_OUT], out_specs=blk,
        out_shape=S((t, 1024), F32), name="ssm_core_fwd", compiler_params=_cp(("parallel",)))(proj, *ops)


def _ssm_core_bwd(proj, dy, ops):
    t = proj.shape[0]
    nc = t // SSM_CHUNK

    def body(u_ref, dy_ref, mt_ref, pr_ref, pi_ref, qr_ref, qi_ref, ar_ref, ai_ref,
             du_ref, dmt_ref, dpr_ref, dpi_ref, dqr_ref, dqi_ref, dar_ref, dai_ref):
        u_rows = _step_rows(u_ref, nc)
        dy_rows = _step_rows(dy_ref, nc)
        row = lax.broadcasted_iota(jnp.int32, (nc, SSM_STATE), 0)
        dus = []
        for j in range(SSM_GPB):
            u = _group_of(u_rows, j)
            dyj = _group_of(dy_rows, j)
            ar, ai = ar_ref[j], ai_ref[j]
            s_re, s_im = _ssm_states(u, pr_ref[j], pi_ref[j], ar, ai, row)
            g_re = _shift_rows(_dot(dyj, qr_ref[j].astype(BF16)), 1, row, True)
            g_im = _shift_rows(_dot(dyj, qi_ref[j].astype(BF16)), 1, row, True)
            l_re, l_im = _carry_scan(g_re, g_im, ar, -ai, row, True)
            lrb, lib = l_re.astype(BF16), l_im.astype(BF16)
            dus.append(_dot_nt(dyj, mt_ref[j].astype(BF16)) + _dot_nt(lrb, pr_ref[j].astype(BF16))
                       + _dot_nt(lib, pi_ref[j].astype(BF16)))
            dmt_ref[j] = _dot_tn(u, dyj)
            dpr_ref[j] = _dot_tn(u, lrb)
            dpi_ref[j] = _dot_tn(u, lib)
            dqr_ref[j] = _dot_tn(dyj, s_re.astype(BF16))
            dqi_ref[j] = _dot_tn(dyj, s_im.astype(BF16))
            dar_ref[j] = jnp.sum(l_re * s_re + l_im * s_im, axis=0, keepdims=True)
            dai_ref[j] = jnp.sum(l_im * s_re - l_re * s_im, axis=0, keepdims=True)
        _store_groups(du_ref, dus, nc)

    blk = pl.BlockSpec((t, 128), lambda j: (0, j))
    return pl.pallas_call(
        body, grid=(SSM_GROUPS // SSM_GPB,), in_specs=[blk, blk] + [_bspec(s) for s in _SSM_GEN_OUT],
        out_specs=[blk] + [_bspec(s) for s in _SSM_GEN_OUT],
        out_shape=[S((t, 1024), F32)] + [S((SSM_GROUPS,) + s, F32) for s in _SSM_GEN_OUT],
        name="ssm_core_bwd", compiler_params=_cp(("parallel",)))(proj, dy, *ops)


def _s5_act(y_core, proj, d_skip):
    t = y_core.shape[0]
    tr = 256

    def body(y_ref, u_ref, d_ref, o_ref):
        o_ref[...] = _gelu(y_ref[...] + d_ref[...] * u_ref[...])

    return pl.pallas_call(body, grid=(t // tr,), in_specs=[_row_spec(tr, 1024), _row_spec(tr, 1024), _full_spec((1, 1024))],
                          out_specs=_row_spec(tr, 1024), out_shape=S((t, 1024), F32), name="s5_act",
                          compiler_params=_cp(("parallel",)))(y_core, proj, d_skip)


def _s5_gate(yg, tg, b_glu, proj):
    t = yg.shape[0]
    tr = 256

    def body(y_ref, t_ref, b_ref, z_ref, o_ref):
        o_ref[...] = (y_ref[...] * jax.nn.sigmoid(t_ref[...] + b_ref[...]) * _silu(z_ref[...])).astype(BF16)

    return pl.pallas_call(
        body, grid=(t // tr,),
        in_specs=[_row_spec(tr, 1024), _row_spec(tr, 1024), _full_spec((1, 1024)), pl.BlockSpec((tr, 1024), lambda i: (i, 1))],
        out_specs=_row_spec(tr, 1024), out_shape=S((t, 1024), BF16), name="s5_gate",
        compiler_params=_cp(("parallel",)))(yg, tg, b_glu, proj)


def _s5_gate_bwd(dp, yg, tg, b_glu, proj):
    t = yg.shape[0]
    tr = 256

    def body(dp_ref, y_ref, t_ref, b_ref, z_ref, dt_ref, dy_ref, dz_ref, db_ref):
        @pl.when(pl.program_id(0) == 0)
        def _():
            db_ref[...] = jnp.zeros_like(db_ref)

        z = z_ref[...]
        yv = y_ref[...]
        sg = jax.nn.sigmoid(t_ref[...] + b_ref[...])
        dpv = dp_ref[...].astype(F32)
        do = dpv * _silu(z)
        dz_ref[...] = (dpv * yv * sg * _silu_grad(z)).astype(BF16)
        dy_ref[...] = do * sg
        dtv = do * yv * sg * (1.0 - sg)
        dt_ref[...] = dtv.astype(BF16)
        db_ref[...] += jnp.sum(dtv, axis=0, keepdims=True)

    return pl.pallas_call(
        body, grid=(t // tr,),
        in_specs=[_row_spec(tr, 1024), _row_spec(tr, 1024), _row_spec(tr, 1024), _full_spec((1, 1024)),
                  pl.BlockSpec((tr, 1024), lambda i: (i, 1))],
        out_specs=[_row_spec(tr, 1024), _row_spec(tr, 1024), _row_spec(tr, 1024), _full_spec((1, 1024))],
        out_shape=[S((t, 1024), BF16), S((t, 1024), F32), S((t, 1024), BF16), S((1, 1024), F32)],
        name="s5_gate_bwd", compiler_params=_cp(("arbitrary",)))(dp, yg, tg, b_glu, proj)


def _s5_act_bwd(dyg1, dyg2, y_core, proj, d_skip):
    t = y_core.shape[0]
    tr = 256

    def body(a_ref, b_ref, y_ref, u_ref, d_ref, dy_ref, du_ref, dd_ref):
        @pl.when(pl.program_id(0) == 0)
        def _():
            dd_ref[...] = jnp.zeros_like(dd_ref)

        u = u_ref[...]
        dv = (a_ref[...] + b_ref[...]) * _gelu_grad(y_ref[...] + d_ref[...] * u)
        dy_ref[...] = dv
        du_ref[...] = dv * d_ref[...]
        dd_ref[...] += jnp.sum(dv * u, axis=0, keepdims=True)

    return pl.pallas_call(
        body, grid=(t // tr,),
        in_specs=[_row_spec(tr, 1024)] * 4 + [_full_spec((1, 1024))],
        out_specs=[_row_spec(tr, 1024), _row_spec(tr, 1024), _full_spec((1, 1024))],
        out_shape=[S((t, 1024), F32), S((t, 1024), F32), S((1, 1024), F32)],
        name="s5_act_bwd", compiler_params=_cp(("arbitrary",)))(dyg1, dyg2, y_core, proj, d_skip)


def _s5_dproj(du_skip, du_core, dz):
    t = du_skip.shape[0]
    tr = 256

    def body(a_ref, b_ref, z_ref, o_ref):
        o_ref[:, 0:1024] = (a_ref[...] + b_ref[...]).astype(BF16)
        o_ref[:, 1024:2048] = z_ref[...]

    return pl.pallas_call(body, grid=(t // tr,), in_specs=[_row_spec(tr, 1024)] * 3, out_specs=_row_spec(tr, 2048),
                          out_shape=S((t, 2048), BF16), name="s5_dproj",
                          compiler_params=_cp(("parallel",)))(du_skip, du_core, dz)


def _s5_params(lam_re, lam_im, log_dt, b_re, b_im, c_re, c_im):
    g = SSM_GROUPS
    return (lam_re.reshape(g, 1, SSM_STATE), lam_im.reshape(g, 1, SSM_STATE),
            jnp.broadcast_to(log_dt.reshape(g, 1, 1), (g, 1, SSM_STATE)),
            b_re.transpose(0, 2, 1), b_im.transpose(0, 2, 1), c_re, c_im)


def _s5_fwd(proj, params, d_skip, w_glu, b_glu):
    ops = _ssm_operators(params)
    y_core = _ssm_core_fwd(proj, ops)
    yg = _s5_act(y_core, proj, d_skip)
    tg = _mm(yg, w_glu, name="s5_glu")
    return _s5_gate(yg, tg, b_glu, proj), (ops, y_core, yg, tg)


def _s5_bwd(proj, res, dp, params, d_skip, w_glu, b_glu):
    ops, y_core, yg, tg = res
    dtg, dyg1, dz, db_glu = _s5_gate_bwd(dp, yg, tg, b_glu, proj)
    dyg2 = _mm(dtg, w_glu, tb=True, name="s5_glu_dx")
    dw_glu = _mm(yg, dtg, ta=True, out_dtype=BF16, name="s5_glu_dw")
    dy_core, du_skip, dd = _s5_act_bwd(dyg1, dyg2, y_core, proj, d_skip)
    outs = _ssm_core_bwd(proj, dy_core, ops)
    dproj = _s5_dproj(du_skip, outs[0], dz)
    return dproj, outs[1:], dd, dw_glu, db_glu


def _s5_param_grads(params, cots, after=()):
    glr, gli, gdt, gbtr, gbti, gcr, gci = _ssm_operators_bwd(params, cots, after)
    g = SSM_GROUPS
    return (glr.reshape(g, SSM_STATE), gli.reshape(g, SSM_STATE), gdt[:, 0, 0].reshape(1, g),
            gbtr.transpose(0, 2, 1), gbti.transpose(0, 2, 1), gcr, gci)


def _perm_uq(w):
    w3 = w.reshape(w.shape[0], MLA_HEADS, MLA_NOPE + MLA_ROPE)
    return jnp.concatenate([w3[:, :, :64].reshape(-1, 1024), w3[:, :, 64:80].reshape(-1, 256),
                            w3[:, :, 80:96].reshape(-1, 256)], axis=1)


def _unperm_uq(w):
    r = w.shape[0]
    return jnp.concatenate([w[:, :1024].reshape(r, 16, 64), w[:, 1024:1280].reshape(r, 16, 16),
                            w[:, 1280:1536].reshape(r, 16, 16)], axis=2).reshape(r, 1536)


def _perm_ukv(w):
    w3 = w.reshape(w.shape[0], MLA_HEADS, 128)
    return jnp.concatenate([w3[:, :, :64].reshape(-1, 1024), w3[:, :, 64:].reshape(-1, 1024)], axis=1)


def _unperm_ukv(w):
    r = w.shape[0]
    return jnp.concatenate([w[:, :1024].reshape(r, 16, 64), w[:, 1024:].reshape(r, 16, 64)], axis=2).reshape(r, 2048)


def _mla_fwd(proj, gq, gkv, wuq, wukv, cos, sin):
    cqn, ckvn = _mla_norms(proj, gq, gkv)
    qf = _mm(cqn, wuq, name="mla_uq")
    kvf = _mm(ckvn, wukv, name="mla_ukv")
    qh, kh = _mla_pack(qf, kvf, proj, cos, sin)
    o, pout, lse = _mla_attn_fwd(qh, kh, kvf, proj)
    return pout, (cqn, ckvn, kvf, qh, kh, o, lse)


def _mla_bwd(proj, res, dpout, gq, gkv, wuq, wukv, cos, sin):
    cqn, ckvn, kvf, qh, kh, o, lse = res
    dqh, dkh, dv, dz = _mla_attn_bwd(qh, kh, kvf, proj, o, dpout, lse)
    dqf, dkvf, dkr = _mla_unpack(dqh, dkh, dv, cos, sin)
    dcqn = _mm(dqf, wuq, tb=True, name="mla_uq_dx")
    dwuq = _mm(cqn, dqf, ta=True, out_dtype=BF16, name="mla_uq_dw")
    dckvn = _mm(dkvf, wukv, tb=True, name="mla_ukv_dx")
    dwukv = _mm(ckvn, dkvf, ta=True, out_dtype=BF16, name="mla_ukv_dw")
    dproj, dgq, dgkv = _mla_norms_bwd(proj, dcqn, dckvn, dkr, dz, gq, gkv)
    return dproj, dgq, dgkv, dwuq, dwukv


_ANY = pl.BlockSpec(memory_space=pl.ANY)


def _chip_peers():
    x, y, c = lax.axis_index("x"), lax.axis_index("y"), lax.axis_index("c")
    return 2 * x + y, c, [(1 - x, y), (x, 1 - y), (1 - x, 1 - y)]


def _sibling_swap(ps, name):
    n = len(ps)

    def body(*refs):
        p_refs, q_refs, send_sems, recv_sems = refs[:n], refs[n:2 * n], refs[2 * n], refs[2 * n + 1]
        x, y, c = lax.axis_index("x"), lax.axis_index("y"), lax.axis_index("c")
        cps = [pltpu.make_async_remote_copy(src_ref=p_refs[i], dst_ref=q_refs[i], send_sem=send_sems.at[i],
                                            recv_sem=recv_sems.at[i], device_id=(x, y, 1 - c), device_id_type=MESH)
               for i in range(n)]
        for cp in cps:
            cp.start()
        for cp in cps:
            cp.wait()

    return pl.pallas_call(body, in_specs=[_ANY] * n, out_specs=[_ANY] * n, out_shape=[S(p.shape, p.dtype) for p in ps],
                          scratch_shapes=[pltpu.SemaphoreType.DMA((n,)), pltpu.SemaphoreType.DMA((n,))], name=name)(*ps)


def _quarter_spec(tr, r, c, dim):
    if dim is None:
        return pl.BlockSpec((1, tr, c), lambda i, me: (me[0], i, 0))
    if dim == 0:
        return pl.BlockSpec((tr, c), lambda i, me: (me[0] * (r // tr) + i, 0))
    return pl.BlockSpec((tr, c), lambda i, me: (i, me[0]))


def _whole_shape(r, c, dim):
    return (N_CHIPS, r, c) if dim is None else ((N_CHIPS * r, c) if dim == 0 else (r, N_CHIPS * c))


def _row_tile(r):
    return 256 if r % 256 == 0 else (128 if r % 128 == 0 else r)


def _cast_place(me, x, dim, dtype, name):
    r, c = x.shape
    tr = _row_tile(r)

    def body(me_ref, x_ref, o_ref):
        o_ref[...] = x_ref[...].astype(dtype).reshape(o_ref.shape)

    return pl.pallas_call(
        body, out_shape=S(_whole_shape(r, c, dim), dtype), name=name,
        grid_spec=pltpu.PrefetchScalarGridSpec(num_scalar_prefetch=1, grid=(r // tr,),
                                               in_specs=[pl.BlockSpec((tr, c), lambda i, me: (i, 0))],
                                               out_specs=_quarter_spec(tr, r, c, dim)),
        compiler_params=_cp(("parallel",)))(me, x)


def _sum_parts(me, g, land, dim, name):
    _, r, c = land.shape
    tr = _row_tile(r)

    def body(me_ref, g_ref, l_ref, o_ref):
        own = g_ref[...].astype(F32).reshape(tr, c)
        o_ref[...] = ((own + l_ref[0].astype(F32)) + l_ref[1].astype(F32)) + l_ref[2].astype(F32)

    return pl.pallas_call(
        body, out_shape=S((r, c), F32), name=name,
        grid_spec=pltpu.PrefetchScalarGridSpec(
            num_scalar_prefetch=1, grid=(r // tr,),
            in_specs=[_quarter_spec(tr, r, c, dim), pl.BlockSpec((3, tr, c), lambda i, me: (0, i, 0))],
            out_specs=pl.BlockSpec((tr, c), lambda i, me: (i, 0))),
        compiler_params=_cp(("parallel",)))(me, g, land)


_HBM = pl.BlockSpec(memory_space=pltpu.HBM)
_SEM = pl.BlockSpec(memory_space=pltpu.SEMAPHORE)
_EFFECT = pltpu.SideEffectType.DATAFLOW_SIDE_EFFECTING


def _quarter(ref, s, dim):
    n = ref.shape[dim] // N_CHIPS
    start = pl.multiple_of(s * n, n)
    return ref.at[pl.ds(start, n), :] if dim == 0 else ref.at[:, pl.ds(start, n)]


def _part(ref, s, dim):
    return ref.at[s] if dim is None else _quarter(ref, s, dim)


def _gather_plan(i, dim):
    return dict(src=i, dst=i, send_src=lambda ref, me, peer, k: _part(ref, me, dim),
                send_dst=lambda ref, me, peer, k: _part(ref, me, dim), recv_dst=lambda ref, me, peer, k: _part(ref, peer, dim))


def _scatter_plan(i, j, dim):
    return dict(src=i, dst=j, send_src=lambda ref, me, peer, k: _part(ref, peer, dim),
                send_dst=lambda ref, me, peer, k: ref.at[k], recv_dst=lambda ref, me, peer, k: ref.at[k])


def _hbm(a):
    return pltpu.with_memory_space_constraint(a, pltpu.HBM)


def _remote_copies(refs, send_sems, recv_sems, plans):
    me, c, peers = _chip_peers()
    sends, recvs = [], []
    for i, p in enumerate(plans):
        for k, (px, py) in enumerate(peers):
            peer = 2 * px + py
            sems = dict(send_sem=send_sems.at[3 * i + k], recv_sem=recv_sems.at[3 * i + k], device_id=(px, py, c),
                        device_id_type=MESH)
            src = p['send_src'](refs[p['src']], me, peer, k)
            sends.append(pltpu.make_async_remote_copy(src_ref=src, dst_ref=p['send_dst'](refs[p['dst']], me, peer, k), **sems))
            recvs.append(pltpu.make_async_remote_copy(src_ref=src, dst_ref=p['recv_dst'](refs[p['dst']], me, peer, k), **sems))
    return sends, recvs


def _exchange_start(arrays, plans, name, after=()):
    n, m, extra = len(arrays), len(plans), len(after)

    def body(*refs):
        send_sems, recv_sems, token = refs[n + extra], refs[n + extra + 1], refs[-1]
        sends, _ = _remote_copies(refs[:n], send_sems, recv_sems, plans)
        for cp in sends:
            cp.start()
        token[...] = jnp.zeros_like(token)

    outs = pl.pallas_call(
        body, name=name,
        out_shape=(pltpu.SemaphoreType.DMA((3 * m,)), pltpu.SemaphoreType.DMA((3 * m,)),
                   *[pltpu.HBM(a.shape, a.dtype) for a in arrays], S((8, 128), F32)),
        in_specs=[_HBM] * n + [_ANY] * extra, out_specs=(_SEM, _SEM, *[_HBM] * n, pl.BlockSpec(memory_space=pltpu.VMEM)),
        input_output_aliases={i: 2 + i for i in range(n)},
        compiler_params=pltpu.CompilerParams(has_side_effects=_EFFECT))(*[_hbm(a) for a in arrays], *after)
    return outs[0], outs[1], outs[2:2 + n], outs[-1]


def _exchange_wait(started, plans, after, name):
    send_sems, recv_sems, arrays, _ = started
    n = len(arrays)

    def body(*refs):
        sends, recvs = _remote_copies(refs[:n], refs[n], refs[n + 1], plans)
        for cp in sends:
            cp.wait_send()
        for cp in recvs:
            cp.wait_recv()

    return pl.pallas_call(
        body, name=name, out_shape=tuple(pltpu.HBM(a.shape, a.dtype) for a in arrays),
        in_specs=[_HBM] * n + [_SEM, _SEM, _ANY], out_specs=tuple([_HBM] * n),
        input_output_aliases={i: i for i in range(n)},
        compiler_params=pltpu.CompilerParams(has_side_effects=_EFFECT))(*arrays, send_sems, recv_sems, after)


def _add2(p, q):
    r, c = p.shape
    tr = 128 if r % 128 == 0 else r

    def body(p_ref, q_ref, o_ref):
        o_ref[...] = p_ref[...] + q_ref[...]

    return pl.pallas_call(body, grid=(r // tr,), in_specs=[_row_spec(tr, c), _row_spec(tr, c)], out_specs=_row_spec(tr, c),
                          out_shape=S((r, c), F32), name="add_cores", compiler_params=_cp(("parallel",)))(p, q)


def _adamw(w, g, m, v, name, g2=None, after=()):
    r, c = w.shape
    tr = 256 if r % 256 == 0 else (128 if r % 128 == 0 else r)
    c1 = 1.0 / (1.0 - ADAM_B1 ** ADAM_STEP)
    c2 = 1.0 / (1.0 - ADAM_B2 ** ADAM_STEP)
    gs = [g] if g2 is None else [g, g2]

    def body(*refs):
        w_ref, m_ref, v_ref = refs[0], refs[1], refs[2]
        g_refs = refs[3:3 + len(gs)]
        go_ref, d_ref, nm_ref, nv_ref = refs[3 + len(gs) + len(after):]
        gv = g_refs[0][...]
        if len(gs) == 2:
            gv = gv + g_refs[1][...]
        go_ref[...] = gv
        nm = ADAM_B1 * m_ref[...] + (1.0 - ADAM_B1) * gv
        nv = ADAM_B2 * v_ref[...] + (1.0 - ADAM_B2) * (gv * gv)
        nm_ref[...] = nm
        nv_ref[...] = nv
        d_ref[...] = -ADAM_LR * ((nm * c1) / (jnp.sqrt(nv * c2) + ADAM_EPS) + ADAM_WD * w_ref[...])

    return pl.pallas_call(body, grid=(r // tr,),
                          in_specs=[_row_spec(tr, c)] * (3 + len(gs)) + [pl.BlockSpec(memory_space=pl.ANY)] * len(after),
                          out_specs=[_row_spec(tr, c)] * 4, out_shape=[S((r, c), F32)] * 4, name=name,
                          compiler_params=_cp(("parallel",)))(w, m, v, *gs, *after)


_WEIGHTS = ['pre_norm', 'post_norm', 'rel_bias', 'a_w_in', 'a_lam_re', 'a_lam_im', 'a_log_dt', 'a_b_re', 'a_b_im',
            'a_c_re', 'a_c_im', 'a_d', 'a_w_glu', 'a_b_glu', 'a_w_out', 'b_w_in', 'b_sinks', 'b_w_out', 'c_w_in',
            'c_q_norm', 'c_kv_norm', 'c_w_uq', 'c_w_ukv', 'c_w_out', 'd_w_in', 'd_ln_g', 'd_ln_b', 'd_w_s', 'd_b_s',
            'd_w_out']
_BIG = [('a_w_in', 1024, 2048, 1), ('a_w_glu', 1024, 1024, 0), ('a_w_out', 1024, 1024, 0), ('b_w_in', 1024, 2304, 1),
        ('b_w_out', 1024, 1024, 0), ('c_w_in', 1024, 2080, 1), ('c_w_uq', 768, 1536, 1), ('c_w_ukv', 256, 2048, 1),
        ('c_w_out', 1024, 1024, 0), ('d_w_in', 1024, 3072, 1), ('d_w_out', 1024, 1024, 0)]
_SHARDED_VECS = [('c_q_norm', 768), ('c_kv_norm', 256), ('d_ln_g', 1024), ('d_ln_b', 1024)]
_REPLICATED = [n for n in _WEIGHTS if n not in [b[0] for b in _BIG] and n not in [s[0] for s in _SHARDED_VECS]]
_LANES = 1024
_SMALL_Q_ROWS = 136


def kernel(x, pre_norm, post_norm, rel_bias, a_w_in, a_lam_re, a_lam_im, a_log_dt, a_b_re, a_b_im, a_c_re, a_c_im, a_d, a_w_glu, a_b_glu, a_w_out, b_w_in, b_sinks, b_w_out, c_w_in, c_q_norm, c_kv_norm, c_w_uq, c_w_ukv, c_w_out, d_w_in, d_ln_g, d_ln_b, d_w_s, d_b_s, d_w_out, loss_target, m_pre_norm, m_post_norm, m_rel_bias, m_a_w_in, m_a_lam_re, m_a_lam_im, m_a_log_dt, m_a_b_re, m_a_b_im, m_a_c_re, m_a_c_im, m_a_d, m_a_w_glu, m_a_b_glu, m_a_w_out, m_b_w_in, m_b_sinks, m_b_w_out, m_c_w_in, m_c_q_norm, m_c_kv_norm, m_c_w_uq, m_c_w_ukv, m_c_w_out, m_d_w_in, m_d_ln_g, m_d_ln_b, m_d_w_s, m_d_b_s, m_d_w_out, v_pre_norm, v_post_norm, v_rel_bias, v_a_w_in, v_a_lam_re, v_a_lam_im, v_a_log_dt, v_a_b_re, v_a_b_im, v_a_c_re, v_a_c_im, v_a_d, v_a_w_glu, v_a_b_glu, v_a_w_out, v_b_w_in, v_b_sinks, v_b_w_out, v_c_w_in, v_c_q_norm, v_c_kv_norm, v_c_w_uq, v_c_w_ukv, v_c_w_out, v_d_w_in, v_d_ln_g, v_d_ln_b, v_d_w_s, v_d_b_s, v_d_w_out):
    arg = dict(locals())
    me = 2 * lax.axis_index("x") + lax.axis_index("y")
    xin = x[0]
    t = xin.shape[0]

    big = {n: (r, c, dim) for n, r, c, dim in _BIG}

    me1 = me.reshape(1).astype(jnp.int32)

    def gather_start(names, tag, after):
        wholes, plans = [], []
        for i, n in enumerate(names):
            if n in big:
                r, c, dim = big[n]
                src, dtype = arg[n][0], BF16
                if dim == 1 and (c // N_CHIPS) % 128:
                    dim = None
            else:
                src, dim, dtype = arg[n], None, F32
            wholes.append(_cast_place(me1, src, dim, dtype, "place_" + n))
            plans.append(_gather_plan(i, dim))
        return _exchange_start(wholes, plans, "gather_start_" + tag, after=after), plans

    def gather_wait(names, started, after, tag):
        st, plans = started
        out = {}
        for n, a in zip(names, _exchange_wait(st, plans, after, "gather_wait_" + tag)):
            if n in big:
                r, c, dim = big[n]
                out[n] = a if a.ndim == 2 else a.transpose(1, 0, 2).reshape(r, c)
            else:
                out[n] = a.reshape(1, -1)
        return out

    groups = {'a1': ['a_w_in'], 'a2': ['a_w_glu', 'a_w_out'], 'b': ['b_w_in', 'b_w_out'],
              'c': ['c_w_in', 'c_w_uq', 'c_w_ukv', 'c_w_out', 'c_q_norm', 'c_kv_norm'],
              'd': ['d_w_in', 'd_w_out', 'd_ln_g', 'd_ln_b']}
    started, tokens = {}, []
    for tag, names in groups.items():
        started[tag] = gather_start(names, tag, tokens)
        tokens = [started[tag][0][3]]
    wt = gather_wait(groups['a1'], started['a1'], xin, 'a1')
    cos, sin = _rope_tables(t)
    bucket = jnp.asarray(_t5_bucket_table())
    s5p = _s5_params(a_lam_re[0], a_lam_im[0], a_log_dt[0], a_b_re[0], a_b_im[0], a_c_re[0], a_c_im[0])
    bst = d_b_s[0].T

    x0 = xin
    h0 = _prenorm(x0, pre_norm[0:1], "pre0", after=tokens)
    proj0 = _mm(h0, wt['a_w_in'], name="a_in")
    wt.update(gather_wait(groups['a2'], started['a2'], proj0, 'a2'))
    p0, res0 = _s5_fwd(proj0, s5p, a_d, wt['a_w_glu'], a_b_glu)
    y0 = _mm(p0, wt['a_w_out'], name="a_out")
    x1 = _postnorm(x0, y0, post_norm[0:1], "post0")

    wt.update(gather_wait(groups['b'], started['b'], x1, 'b'))
    h1 = _prenorm(x1, pre_norm[1:2], "pre1")
    proj1 = _mm(h1, wt['b_w_in'], name="b_in")
    bias = _swa_bias(rel_bias, bucket)
    p1 = _swa_fwd(proj1, bias, b_sinks)
    y1 = _mm(p1, wt['b_w_out'], name="b_out")
    x2 = _postnorm(x1, y1, post_norm[1:2], "post1")

    wt.update(gather_wait(groups['c'], started['c'], x2, 'c'))
    w_c_in = jnp.concatenate([wt['c_w_in'][:, :1056], jnp.zeros((1024, 96), BF16), wt['c_w_in'][:, 1056:],
                              jnp.zeros((1024, MLA_PROJ - MLA_Z1), BF16)], axis=1)
    w_uq = _perm_uq(wt['c_w_uq'])
    w_ukv = _perm_ukv(wt['c_w_ukv'])
    h2 = _prenorm(x2, pre_norm[2:3], "pre2")
    proj2 = _mm(h2, w_c_in, name="c_in")
    p2, res2 = _mla_fwd(proj2, wt['c_q_norm'], wt['c_kv_norm'], w_uq, w_ukv, cos, sin)
    y2 = _mm(p2, wt['c_w_out'], name="c_out")
    x3 = _postnorm(x2, y2, post_norm[2:3], "post2")

    wt.update(gather_wait(groups['d'], started['d'], x3, 'd'))
    vec = wt
    h3 = _prenorm(x3, pre_norm[3:4], "pre3")
    proj3 = _mm(h3, wt['d_w_in'], name="d_in")
    p3 = _sgu_fwd(proj3, vec['d_ln_g'], vec['d_ln_b'], d_w_s[0], bst)
    y3 = _mm(p3, wt['d_w_out'], name="d_out")
    x4 = _postnorm(x3, y3, post_norm[3:4], "post3")

    loss_part, dx = _loss_head(x4, loss_target[0])
    loss = lax.psum(loss_part[0, 0], ("x", "y", "c"))

    gr = {}
    d_pre, d_post = [None] * 4, [None] * 4
    scattered = []

    def scatter_start(items, tag):
        srcs = [a for _, a, _ in items]
        n = len(items)
        plans = [_scatter_plan(i, n + i, dim) for i, (_, _, dim) in enumerate(items)]
        lands = []
        for _, a, dim in items:
            part = a.shape[1:] if dim is None else tuple(s // N_CHIPS if i == dim else s for i, s in enumerate(a.shape))
            lands.append(lax.empty((3,) + part, a.dtype))
        st = _exchange_start(srcs + lands, plans, "scatter_start_" + tag)
        scattered.append((st, plans, items, tag))
        return [st[3]]

    def stacked(g):
        r, c = g.shape
        return g.reshape(r, N_CHIPS, c // N_CHIPS).transpose(1, 0, 2)

    dy, d_post[3] = _postnorm_bwd(dx, y3, post_norm[3:4], "post3_bwd")
    dp = _mm(dy, wt['d_w_out'], tb=True, name="d_out_dx")
    g_out = _mm(p3, dy, ta=True, out_dtype=BF16, name="d_out_dw")
    dproj, gr['d_w_s'], dbst, gr['d_ln_g'], gr['d_ln_b'] = _sgu_bwd(proj3, dp, vec['d_ln_g'], vec['d_ln_b'], d_w_s[0], bst)
    gr['d_b_s'] = dbst.T
    dh = _mm(dproj, wt['d_w_in'], tb=True, name="d_in_dx")
    g_in = _mm(h3, dproj, ta=True, out_dtype=BF16, name="d_in_dw")
    tok = scatter_start([('d_w_in', g_in, 1), ('d_w_out', g_out, 0)], 'd')
    dx, d_pre[3] = _prenorm_bwd(dx, dh, x3, pre_norm[3:4], "pre3_bwd")

    dy, d_post[2] = _postnorm_bwd(dx, y2, post_norm[2:3], "post2_bwd", after=tok)
    dp = _mm(dy, wt['c_w_out'], tb=True, name="c_out_dx")
    g_out = _mm(p2, dy, ta=True, out_dtype=BF16, name="c_out_dw")
    dproj, gr['c_q_norm'], gr['c_kv_norm'], dwuq, dwukv = _mla_bwd(proj2, res2, dp, wt['c_q_norm'], wt['c_kv_norm'],
                                                                  w_uq, w_ukv, cos, sin)
    dh = _mm(dproj, w_c_in, tb=True, name="c_in_dx")
    dwc = _mm(h2, dproj, ta=True, out_dtype=BF16, name="c_in_dw")
    g_in = stacked(jnp.concatenate([dwc[:, :1056], dwc[:, MLA_Z0:MLA_Z1]], axis=1))
    tok = scatter_start([('c_w_in', g_in, None), ('c_w_uq', _unperm_uq(dwuq), 1), ('c_w_ukv', _unperm_ukv(dwukv), 1),
                         ('c_w_out', g_out, 0)], 'c')
    dx, d_pre[2] = _prenorm_bwd(dx, dh, x2, pre_norm[2:3], "pre2_bwd")

    dy, d_post[1] = _postnorm_bwd(dx, y1, post_norm[1:2], "post1_bwd", after=tok)
    dp = _mm(dy, wt['b_w_out'], tb=True, name="b_out_dx")
    g_out = _mm(p1, dy, ta=True, out_dtype=BF16, name="b_out_dw")
    dqz, dkv, dss, dsk = _swa_bwd(proj1, dp, bias, b_sinks)
    gr['rel_bias'], gr['b_sinks'] = _swa_dbias(dss, dsk, bucket)
    dproj = jnp.concatenate([dqz[:, :1024], dkv.astype(BF16), dqz[:, 1024:]], axis=1)
    dh = _mm(dproj, wt['b_w_in'], tb=True, name="b_in_dx")
    g_in = stacked(_mm(h1, dproj, ta=True, out_dtype=BF16, name="b_in_dw"))
    tok = scatter_start([('b_w_in', g_in, None), ('b_w_out', g_out, 0)], 'b')
    dx, d_pre[1] = _prenorm_bwd(dx, dh, x1, pre_norm[1:2], "pre1_bwd")

    dy, d_post[0] = _postnorm_bwd(dx, y0, post_norm[0:1], "post0_bwd", after=tok)
    dp = _mm(dy, wt['a_w_out'], tb=True, name="a_out_dx")
    g_out = _mm(p0, dy, ta=True, out_dtype=BF16, name="a_out_dw")
    dproj, s5cots, gr['a_d'], g_glu, gr['a_b_glu'] = _s5_bwd(proj0, res0, dp, s5p, a_d, wt['a_w_glu'], a_b_glu)
    tok = scatter_start([('a_w_glu', g_glu, 0), ('a_w_out', g_out, 0)], 'a2')
    g_in = _mm(h0, dproj, ta=True, out_dtype=BF16, name="a_in_dw", after=tok)
    tok = scatter_start([('a_w_in', g_in, 1)], 'a1')
    dh = _mm(dproj, wt['a_w_in'], tb=True, name="a_in_dx", after=tok)
    dx, d_pre[0] = _prenorm_bwd(dx, dh, x0, pre_norm[0:1], "pre0_bwd")
    s5g = _s5_param_grads(s5p, s5cots, after=tok)
    for n, g in zip(['a_lam_re', 'a_lam_im', 'a_log_dt', 'a_b_re', 'a_b_im', 'a_c_re', 'a_c_im'], s5g):
        gr[n] = g
    gr['pre_norm'] = jnp.concatenate(d_pre, axis=0)
    gr['post_norm'] = jnp.concatenate(d_post, axis=0)
    small_names = _REPLICATED + [n for n, _ in _SHARDED_VECS]
    small = jnp.concatenate([gr[n].reshape(-1) for n in small_names])
    small = jnp.pad(small, (0, N_CHIPS * _SMALL_Q_ROWS * _LANES - small.shape[0])).reshape(N_CHIPS, _SMALL_Q_ROWS, _LANES)
    scatter_start([('small', small, None)], 'a0')

    partial = {}
    for st, plans, items, tag in scattered[:-1]:
        done = _exchange_wait(st, plans, small, "scatter_wait_" + tag)
        for i, (n, _, dim) in enumerate(items):
            partial[n] = _sum_parts(me1, done[i], done[len(items) + i], dim, "sum_parts_" + n)
    big_names = [n for n, _, _, _ in _BIG]
    other = dict(zip(big_names, _sibling_swap([partial[n] for n in big_names], "swap_cores")))
    st, plans, _, tag = scattered[-1]
    done = _exchange_wait(st, plans, other[big_names[-1]], "scatter_wait_" + tag)
    p_small = _sum_parts(me1, done[0], done[1], None, "sum_parts_small")
    small_q = _add2(p_small, _sibling_swap([p_small], "swap_cores_small")[0])
    gplan = [_gather_plan(0, None)]
    gathering = _exchange_start([_cast_place(me1, small_q, None, F32, "place_small_grads")], gplan, "gather_small_start")

    delta, new_m, new_v, grads = {}, {}, {}, {}
    for n in big_names:
        shp = arg[n].shape
        two = (shp[1], shp[2])
        outs = _adamw(arg[n].reshape(two), partial[n], arg['m_' + n].reshape(two), arg['v_' + n].reshape(two),
                      "adamw_" + n, g2=other[n], after=[gathering[3]])
        grads[n], delta[n], new_m[n], new_v[n] = [o.reshape(shp) for o in outs]
    small_all = _exchange_wait(gathering, gplan, new_v[big_names[-1]], "gather_small_wait")[0].reshape(-1)
    off = 0
    for n in _REPLICATED:
        sz = math.prod(arg[n].shape)
        grads[n] = small_all[off:off + sz].reshape(arg[n].shape)
        off += sz
    for n, sz in _SHARDED_VECS:
        q = sz // N_CHIPS
        grads[n] = lax.dynamic_slice(small_all, (off + me * q,), (q,)).reshape(arg[n].shape)
        off += sz

    for n in small_names:
        shp = arg[n].shape
        two = (math.prod(shp[:-1]), shp[-1])
        outs = _adamw(arg[n].reshape(two), grads[n].reshape(two), arg['m_' + n].reshape(two), arg['v_' + n].reshape(two),
                      "adamw_" + n)
        delta[n], new_m[n], new_v[n] = [o.reshape(shp) for o in outs[1:]]

    return (loss, dx.reshape(x.shape), *[grads[n] for n in _WEIGHTS], *[delta[n] for n in _WEIGHTS],
            *[new_m[n] for n in _WEIGHTS], *[new_v[n] for n in _WEIGHTS])
```

```python
import functools
import math

import numpy as np
import jax
import jax.numpy as jnp
from jax import lax
from jax.experimental import pallas as pl
from jax.experimental.pallas import tpu as pltpu

F32 = jnp.float32
BF16 = jnp.bfloat16
S = jax.ShapeDtypeStruct
MESH = pl.DeviceIdType.MESH

D_MODEL = 1024
EPS = 1e-6
NEG_INF = -1e30
N_CHIPS = 4

SSM_GROUPS = 64
SSM_GROUP = 16
SSM_STATE = 64
SSM_CHUNK = 16
SSM_BLOCK = SSM_CHUNK * SSM_GROUP

HEAD_DIM = 64
SWA_HEADS = 16
SWA_GROUP = 8
WINDOW = 128
REL_BUCKETS = 32
REL_MAX_DIST = 128

MLA_HEADS = 16
MLA_NOPE = 64
MLA_ROPE = 32
MLA_Q_RANK = 768
MLA_KV_RANK = 256
MLA_SCALE = (MLA_NOPE + MLA_ROPE) ** -0.5
MLA_BQ = 512
MLA_BK = 512
ROPE_BASE = 10000.0

SGU_CHUNK = 128
SGU_GROUPS = 16
SGU_GDIM = 64

ADAM_LR = 0.001
ADAM_B1 = 0.9
ADAM_B2 = 0.999
ADAM_EPS = 1e-08
ADAM_WD = 0.01
ADAM_STEP = 10

VMEM_BIG = 56 * 1024 * 1024


def _cp(sem, vmem=None):
    return pltpu.CompilerParams(dimension_semantics=sem, vmem_limit_bytes=vmem)


def _row_spec(tr, c):
    return pl.BlockSpec((tr, c), lambda i: (i, 0))


def _full_spec(shape):
    nd = len(shape)
    return pl.BlockSpec(shape, lambda *_: (0,) * nd)


def _gelu(x):
    c = 0.7978845608028654
    return 0.5 * x * (1.0 + jnp.tanh(c * (x + 0.044715 * x * x * x)))


def _gelu_grad(x):
    c = 0.7978845608028654
    t = jnp.tanh(c * (x + 0.044715 * x * x * x))
    return 0.5 * (1.0 + t) + 0.5 * x * (1.0 - t * t) * c * (1.0 + 3.0 * 0.044715 * x * x)


def _silu(z):
    return z * jax.nn.sigmoid(z)


def _silu_grad(z):
    s = jax.nn.sigmoid(z)
    return s * (1.0 + z * (1.0 - s))


def _dot(a, b):
    return lax.dot_general(a, b, (((1,), (0,)), ((), ())), preferred_element_type=F32)


def _dot_nt(a, b):
    return lax.dot_general(a, b, (((1,), (1,)), ((), ())), preferred_element_type=F32)


def _dot_tn(a, b):
    return lax.dot_general(a, b, (((0,), (0,)), ((), ())), preferred_element_type=F32)


def _pick(n, cap):
    best = 0
    for d in range(128, min(n, cap) + 1, 128):
        if n % d == 0:
            best = d
    if best < 256:
        return n
    return best


def _mm(a, b, *, ta=False, tb=False, out_dtype=F32, name, after=()):
    m, k = (a.shape[1], a.shape[0]) if ta else a.shape
    n = b.shape[0] if tb else b.shape[1]
    assert (b.shape[1] if tb else b.shape[0]) == k
    tm, tn, tk = _pick(m, 1024), _pick(n, 1024), _pick(k, 1024)
    if tn > 1024:
        tm = _pick(m, 256)
    if tk > 1024:
        tm, tn = _pick(m, 256), _pick(n, 256)
    nk = k // tk
    dims = (((0 if ta else 1,), (1 if tb else 0,)), ((), ()))

    def body(a_ref, b_ref, *rest):
        o_ref, acc_ref = rest[-2], rest[-1]
        kk = pl.program_id(2)

        @pl.when(kk == 0)
        def _():
            acc_ref[...] = jnp.zeros_like(acc_ref)

        acc_ref[...] += lax.dot_general(a_ref[...].astype(BF16), b_ref[...].astype(BF16), dims,
                                        preferred_element_type=F32)

        @pl.when(kk == nk - 1)
        def _():
            o_ref[...] = acc_ref[...].astype(out_dtype)

    a_spec = pl.BlockSpec((tk, tm), lambda i, j, kk: (kk, i)) if ta else pl.BlockSpec((tm, tk), lambda i, j, kk: (i, kk))
    b_spec = pl.BlockSpec((tn, tk), lambda i, j, kk: (j, kk)) if tb else pl.BlockSpec((tk, tn), lambda i, j, kk: (kk, j))
    return pl.pallas_call(
        body, grid=(m // tm, n // tn, nk), in_specs=[a_spec, b_spec] + [pl.BlockSpec(memory_space=pl.ANY)] * len(after),
        out_specs=pl.BlockSpec((tm, tn), lambda i, j, kk: (i, j)), out_shape=S((m, n), out_dtype),
        scratch_shapes=[pltpu.VMEM((tm, tn), F32)], name=name,
        compiler_params=_cp(("parallel", "parallel", "arbitrary"), VMEM_BIG))(a, b, *after)


def _prenorm(x, g, name, after=()):
    t, d = x.shape
    tr = 256

    def body(x_ref, g_ref, *rest):
        o_ref = rest[-1]
        xv = x_ref[...]
        r = lax.rsqrt(jnp.mean(xv * xv, axis=-1, keepdims=True) + EPS)
        o_ref[...] = (xv * r * g_ref[...]).astype(BF16)

    return pl.pallas_call(body, grid=(t // tr,),
                          in_specs=[_row_spec(tr, d), _full_spec((1, d))] + [pl.BlockSpec(memory_space=pl.ANY)] * len(after),
                          out_specs=_row_spec(tr, d), out_shape=S((t, d), BF16), name=name,
                          compiler_params=_cp(("parallel",)))(x, g, *after)


def _rms_bwd_rows(dout, xin, g):
    r = lax.rsqrt(jnp.mean(xin * xin, axis=-1, keepdims=True) + EPS)
    xh = xin * r
    gd = dout * g
    dx = r * (gd - xh * jnp.mean(gd * xh, axis=-1, keepdims=True))
    return dx, jnp.sum(dout * xh, axis=0, keepdims=True)


def _postnorm_bwd(dxn, y, g, name, after=()):
    t, d = y.shape
    tr = 256

    def body(dx_ref, y_ref, g_ref, *rest):
        dy_ref, dg_ref = rest[-2], rest[-1]

        @pl.when(pl.program_id(0) == 0)
        def _():
            dg_ref[...] = jnp.zeros_like(dg_ref)

        dy, dg = _rms_bwd_rows(dx_ref[...], y_ref[...], g_ref[...])
        dy_ref[...] = dy.astype(BF16)
        dg_ref[...] += dg

    return pl.pallas_call(body, grid=(t // tr,),
                          in_specs=[_row_spec(tr, d), _row_spec(tr, d), _full_spec((1, d))]
                          + [pl.BlockSpec(memory_space=pl.ANY)] * len(after),
                          out_specs=[_row_spec(tr, d), _full_spec((1, d))],
                          out_shape=[S((t, d), BF16), S((1, d), F32)], name=name,
                          compiler_params=_cp(("arbitrary",)))(dxn, y, g, *after)


def _prenorm_bwd(dxn, dh, x, g, name):
    t, d = x.shape
    tr = 256

    def body(dxn_ref, dh_ref, x_ref, g_ref, dx_ref, dg_ref):
        @pl.when(pl.program_id(0) == 0)
        def _():
            dg_ref[...] = jnp.zeros_like(dg_ref)

        dx, dg = _rms_bwd_rows(dh_ref[...], x_ref[...], g_ref[...])
        dx_ref[...] = dxn_ref[...] + dx
        dg_ref[...] += dg

    return pl.pallas_call(body, grid=(t // tr,),
                          in_specs=[_row_spec(tr, d), _row_spec(tr, d), _row_spec(tr, d), _full_spec((1, d))],
                          out_specs=[_row_spec(tr, d), _full_spec((1, d))],
                          out_shape=[S((t, d), F32), S((1, d), F32)], name=name,
                          compiler_params=_cp(("arbitrary",)))(dxn, dh, x, g)


def _postnorm_prenorm(x, y, gpost, gpre, name):
    t, d = x.shape
    tr = 256

    def body(x_ref, y_ref, gp_ref, gn_ref, xn_ref, h_ref):
        yv = y_ref[...]
        r = lax.rsqrt(jnp.mean(yv * yv, axis=-1, keepdims=True) + EPS)
        xn = x_ref[...] + yv * r * gp_ref[...]
        xn_ref[...] = xn
        r2 = lax.rsqrt(jnp.mean(xn * xn, axis=-1, keepdims=True) + EPS)
        h_ref[...] = (xn * r2 * gn_ref[...]).astype(BF16)

    return pl.pallas_call(body, grid=(t // tr,),
                          in_specs=[_row_spec(tr, d), _row_spec(tr, d), _full_spec((1, d)), _full_spec((1, d))],
                          out_specs=[_row_spec(tr, d), _row_spec(tr, d)], out_shape=[S((t, d), F32), S((t, d), BF16)],
                          name=name, compiler_params=_cp(("parallel",)))(x, y, gpost, gpre)


def _norm_bwd_pair(dxn, dh, x, gpre, yprev, gpost, name, after=()):
    t, d = x.shape
    tr = 256

    def body(dxn_ref, dh_ref, x_ref, gn_ref, y_ref, gp_ref, *rest):
        dx_ref, dgn_ref, dy_ref, dgp_ref = rest[-4:]

        @pl.when(pl.program_id(0) == 0)
        def _():
            dgn_ref[...] = jnp.zeros_like(dgn_ref)
            dgp_ref[...] = jnp.zeros_like(dgp_ref)

        dxa, dgn = _rms_bwd_rows(dh_ref[...], x_ref[...], gn_ref[...])
        dx = dxn_ref[...] + dxa
        dx_ref[...] = dx
        dgn_ref[...] += dgn
        dy, dgp = _rms_bwd_rows(dx, y_ref[...], gp_ref[...])
        dy_ref[...] = dy.astype(BF16)
        dgp_ref[...] += dgp

    rows, vec = _row_spec(tr, d), _full_spec((1, d))
    return pl.pallas_call(body, grid=(t // tr,),
                          in_specs=[rows, rows, rows, vec, rows, vec] + [pl.BlockSpec(memory_space=pl.ANY)] * len(after),
                          out_specs=[rows, vec, rows, vec],
                          out_shape=[S((t, d), F32), S((1, d), F32), S((t, d), BF16), S((1, d), F32)], name=name,
                          compiler_params=_cp(("arbitrary",)))(dxn, dh, x, gpre, yprev, gpost, *after)


def _postnorm_loss(x, y, gpost, target):
    t, d = x.shape
    tr = 256

    def body(x_ref, y_ref, g_ref, t_ref, l_ref, dy_ref):
        @pl.when(pl.program_id(0) == 0)
        def _():
            l_ref[...] = jnp.zeros_like(l_ref)

        yv = y_ref[...]
        r = lax.rsqrt(jnp.mean(yv * yv, axis=-1, keepdims=True) + EPS)
        e = (x_ref[...] + yv * r * g_ref[...]) - t_ref[...]
        dy_ref[...] = e * (1.0 / d)
        l_ref[...] += 0.5 * jnp.sum(jnp.mean(e * e, axis=-1, keepdims=True), axis=0, keepdims=True)

    return pl.pallas_call(body, grid=(t // tr,),
                          in_specs=[_row_spec(tr, d), _row_spec(tr, d), _full_spec((1, d)), _row_spec(tr, d)],
                          out_specs=[_full_spec((1, 1)), _row_spec(tr, d)],
                          out_shape=[S((1, 1), F32), S((t, d), F32)], name="postnorm_loss",
                          compiler_params=_cp(("arbitrary",)))(x, y, gpost, target)


def _sgu_parts(proj, lg, lb, ws_ref, bst):
    a = proj[:, 0:1024]
    bb = proj[:, 1024:2048]
    z = proj[:, 2048:3072]
    u = _gelu(a)
    vp = _gelu(bb)
    mu = jnp.mean(vp, axis=-1, keepdims=True)
    xc = vp - mu
    rstd = lax.rsqrt(jnp.mean(xc * xc, axis=-1, keepdims=True) + EPS)
    xh = xc * rstd
    vn = xh * lg + lb
    row = lax.broadcasted_iota(jnp.int32, (SGU_CHUNK, SGU_CHUNK), 0)
    col = lax.broadcasted_iota(jnp.int32, (SGU_CHUNK, SGU_CHUNK), 1)
    tril = row >= col
    ws = [jnp.where(tril, ws_ref[g], 0.0).astype(BF16) for g in range(SGU_GROUPS)]
    s = jnp.concatenate(
        [_dot(ws[g], vn[:, g * 64:(g + 1) * 64].astype(BF16)) + bst[:, g:g + 1] for g in range(SGU_GROUPS)], axis=1)
    return a, bb, z, u, vp, rstd, xh, vn, ws, tril, s


def _sgu_fwd(proj, lg, lb, w_s, bst):
    t = proj.shape[0]
    tr = SGU_CHUNK

    def body(p_ref, lg_ref, lb_ref, ws_ref, bst_ref, o_ref):
        _, _, z, u, _, _, _, _, _, _, s = _sgu_parts(p_ref[...], lg_ref[...], lb_ref[...], ws_ref, bst_ref[...])
        o_ref[...] = (u * s * _silu(z)).astype(BF16)

    return pl.pallas_call(
        body, grid=(t // tr,),
        in_specs=[_row_spec(tr, 3072), _full_spec((1, 1024)), _full_spec((1, 1024)), _full_spec((16, 128, 128)),
                  _full_spec((128, 16))],
        out_specs=_row_spec(tr, 1024), out_shape=S((t, 1024), BF16), name="sgu_fwd",
        compiler_params=_cp(("parallel",)))(proj, lg, lb, w_s, bst)


def _sgu_bwd(proj, dp, lg, lb, w_s, bst):
    t = proj.shape[0]
    tr = SGU_CHUNK

    def body(p_ref, dp_ref, lg_ref, lb_ref, ws_ref, bst_ref, dproj_ref, dws_ref, dbst_ref, dlg_ref, dlb_ref):
        @pl.when(pl.program_id(0) == 0)
        def _():
            dws_ref[...] = jnp.zeros_like(dws_ref)
            dbst_ref[...] = jnp.zeros_like(dbst_ref)
            dlg_ref[...] = jnp.zeros_like(dlg_ref)
            dlb_ref[...] = jnp.zeros_like(dlb_ref)

        lgv = lg_ref[...]
        a, bb, z, u, vp, rstd, xh, vn, ws, tril, s = _sgu_parts(p_ref[...], lgv, lb_ref[...], ws_ref, bst_ref[...])
        dpv = dp_ref[...].astype(F32)
        sz = _silu(z)
        du = dpv * s * sz
        ds = dpv * u * sz
        dz = dpv * u * s * _silu_grad(z)
        dvn_parts, dbs_cols = [], []
        for g in range(SGU_GROUPS):
            dsg = ds[:, g * 64:(g + 1) * 64]
            dsg_b = dsg.astype(BF16)
            dvn_parts.append(_dot_tn(ws[g], dsg_b))
            dws_ref[g] += jnp.where(tril, _dot_nt(dsg_b, vn[:, g * 64:(g + 1) * 64].astype(BF16)), 0.0)
            dbs_cols.append(jnp.sum(dsg, axis=1, keepdims=True))
        dvn = jnp.concatenate(dvn_parts, axis=1)
        dbst_ref[...] += jnp.concatenate(dbs_cols, axis=1)
        dlg_ref[...] += jnp.sum(dvn * xh, axis=0, keepdims=True)
        dlb_ref[...] += jnp.sum(dvn, axis=0, keepdims=True)
        dxh = dvn * lgv
        dvp = rstd * (dxh - jnp.mean(dxh, axis=-1, keepdims=True) - xh * jnp.mean(dxh * xh, axis=-1, keepdims=True))
        dproj_ref[:, 0:1024] = (du * _gelu_grad(a)).astype(BF16)
        dproj_ref[:, 1024:2048] = (dvp * _gelu_grad(bb)).astype(BF16)
        dproj_ref[:, 2048:3072] = dz.astype(BF16)

    return pl.pallas_call(
        body, grid=(t // tr,),
        in_specs=[_row_spec(tr, 3072), _row_spec(tr, 1024), _full_spec((1, 1024)), _full_spec((1, 1024)),
                  _full_spec((16, 128, 128)), _full_spec((128, 16))],
        out_specs=[_row_spec(tr, 3072), _full_spec((16, 128, 128)), _full_spec((128, 16)), _full_spec((1, 1024)),
                   _full_spec((1, 1024))],
        out_shape=[S((t, 3072), BF16), S((16, 128, 128), F32), S((128, 16), F32), S((1, 1024), F32),
                   S((1, 1024), F32)],
        name="sgu_bwd", compiler_params=_cp(("arbitrary",)))(proj, dp, lg, lb, w_s, bst)


def _t5_bucket_table():
    qi = np.arange(WINDOW)[:, None]
    kj = np.arange(2 * WINDOW)[None, :]
    dist = np.maximum(qi + WINDOW - kj, 0)
    max_exact = REL_BUCKETS // 2
    dist_f = np.maximum(dist, 1).astype(np.float32)
    large = max_exact + (np.log(dist_f / np.float32(max_exact)) / np.float32(math.log(REL_MAX_DIST / max_exact))
                         * np.float32(REL_BUCKETS - max_exact)).astype(np.int32)
    large = np.minimum(large, REL_BUCKETS - 1)
    return np.where(dist < max_exact, dist, large).astype(np.int32)


def _swa_bias(rel_bias, bucket):
    def body(rb_ref, bk_ref, o_ref):
        bk = bk_ref[...]
        for h in range(SWA_HEADS):
            acc = jnp.zeros((WINDOW, 2 * WINDOW), F32)
            for b in range(REL_BUCKETS):
                acc = jnp.where(bk == b, rb_ref[b, h], acc)
            o_ref[h] = acc

    return pl.pallas_call(
        body, in_specs=[pl.BlockSpec(memory_space=pltpu.SMEM), pl.BlockSpec(memory_space=pltpu.VMEM)],
        out_specs=pl.BlockSpec(memory_space=pltpu.VMEM), out_shape=S((SWA_HEADS, WINDOW, 2 * WINDOW), F32),
        name="swa_bias")(rel_bias, bucket)


def _swa_dbias(ds_sum, dsink_rows, bucket):
    def body(ds_ref, dsk_ref, bk_ref, drb_ref, dsink_ref):
        bk = bk_ref[...]
        r = lax.broadcasted_iota(jnp.int32, (REL_BUCKETS, SWA_HEADS), 0)
        c = lax.broadcasted_iota(jnp.int32, (REL_BUCKETS, SWA_HEADS), 1)
        out = jnp.zeros((REL_BUCKETS, SWA_HEADS), F32)
        for b in range(REL_BUCKETS):
            m = bk == b
            for h in range(SWA_HEADS):
                val = jnp.sum(jnp.where(m, ds_ref[h], 0.0))
                out = jnp.where((r == b) & (c == h), val, out)
        drb_ref[...] = out
        dsink_ref[...] = jnp.sum(dsk_ref[...], axis=0, keepdims=True)

    return pl.pallas_call(
        body, out_shape=[S((REL_BUCKETS, SWA_HEADS), F32), S((1, SWA_HEADS), F32)], name="swa_dbias",
    )(ds_sum, dsink_rows, bucket)


def _swa_band(cur_ref, prev_ref):
    kband = jnp.concatenate([prev_ref[:, 0:128], cur_ref[:, 1024:1152]], axis=0).astype(BF16)
    vband = jnp.concatenate([prev_ref[:, 128:256], cur_ref[:, 1152:1280]], axis=0).astype(BF16)
    return kband, vband


SWA_ROWS = SWA_GROUP * WINDOW


def _swa_valid(n, rows=SWA_ROWS):
    qi = lax.broadcasted_iota(jnp.int32, (rows, 2 * WINDOW), 0) & (WINDOW - 1)
    kj = lax.broadcasted_iota(jnp.int32, (rows, 2 * WINDOW), 1)
    dist = qi + WINDOW - kj
    return (dist >= 0) & (dist < WINDOW) & (n * WINDOW + kj - WINDOW >= 0)


def _swa_stack(x, j):
    return jnp.concatenate([x[:, h * 64:(h + 1) * 64] for h in range(j * SWA_GROUP, (j + 1) * SWA_GROUP)], axis=0)


def _swa_unstack(parts):
    return jnp.concatenate([p[i * WINDOW:(i + 1) * WINDOW] for p in parts for i in range(SWA_GROUP)], axis=1)


def _swa_sink_col(sink_ref, j):
    return jnp.concatenate([jnp.full((WINDOW, 1), sink_ref[0, h], F32)
                            for h in range(j * SWA_GROUP, (j + 1) * SWA_GROUP)], axis=0)


def _swa_probs(qh, kh, bias_h, sink, valid):
    s = _dot_nt(qh, kh) * (HEAD_DIM ** -0.5) + bias_h
    s = jnp.where(valid, s, NEG_INF)
    m = jnp.maximum(jnp.max(s, axis=-1, keepdims=True), sink)
    e = jnp.exp(s - m)
    es = jnp.exp(sink - m)
    inv = 1.0 / (jnp.sum(e, axis=-1, keepdims=True) + es)
    return e * inv, es * inv


def _swa_in_specs():
    return [_row_spec(WINDOW, 2304),
            pl.BlockSpec((WINDOW, 256), lambda n: (jnp.maximum(n - 1, 0), 4)),
            _full_spec((SWA_HEADS, WINDOW, 2 * WINDOW)),
            pl.BlockSpec(memory_space=pltpu.SMEM)]


def _swa_fwd(proj, bias, sinks):
    t = proj.shape[0]

    def body(cur_ref, prev_ref, bias_ref, sink_ref, o_ref):
        n = pl.program_id(0)
        kband, vband = _swa_band(cur_ref, prev_ref)
        valid = _swa_valid(n, WINDOW)
        outs = []
        for h in range(SWA_HEADS):
            j = h // SWA_GROUP
            qh = cur_ref[:, h * 64:(h + 1) * 64].astype(BF16)
            p, _ = _swa_probs(qh, kband[:, j * 64:(j + 1) * 64], bias_ref[h], sink_ref[0, h], valid)
            outs.append(_dot(p.astype(BF16), vband[:, j * 64:(j + 1) * 64]))
        o_ref[...] = (jnp.concatenate(outs, axis=1) * _silu(cur_ref[:, 1280:2304])).astype(BF16)

    return pl.pallas_call(body, grid=(t // WINDOW,), in_specs=_swa_in_specs(), out_specs=_row_spec(WINDOW, 1024),
                          out_shape=S((t, 1024), BF16), name="swa_fwd",
                          compiler_params=_cp(("parallel",)))(proj, proj, bias, sinks)


def _swa_bwd(proj, dp, bias, sinks):
    t = proj.shape[0]
    nb = t // WINDOW

    def body(cur_ref, prev_ref, bias_ref, sink_ref, dp_ref, dqz_ref, dkv_ref, dss_ref, dsk_ref):
        n = pl.program_id(0)

        @pl.when(n == 0)
        def _():
            dkv_ref[...] = jnp.zeros_like(dkv_ref)
            dss_ref[...] = jnp.zeros_like(dss_ref)
            dsk_ref[...] = jnp.zeros_like(dsk_ref)

        kband, vband = _swa_band(cur_ref, prev_ref)
        valid = _swa_valid(n)
        z = cur_ref[:, 1280:2304]
        dpv = dp_ref[...].astype(F32)
        do = dpv * _silu(z)
        q = cur_ref[:, 0:1024].astype(BF16)
        dob = do.astype(BF16)
        outs, dqs, dsinks, dk, dv = [], [], [], [], []
        for j in range(SWA_HEADS // SWA_GROUP):
            heads = slice(j * SWA_GROUP, (j + 1) * SWA_GROUP)
            qs, dos = _swa_stack(q, j), _swa_stack(dob, j)
            kh = kband[:, j * 64:(j + 1) * 64]
            vh = vband[:, j * 64:(j + 1) * 64]
            p, ps = _swa_probs(qs, kh, bias_ref[heads].reshape(SWA_ROWS, 2 * WINDOW), _swa_sink_col(sink_ref, j), valid)
            pb = p.astype(BF16)
            outs.append(_dot(pb, vh))
            dpr = _dot_nt(dos, vh)
            dlt = jnp.sum(p * dpr, axis=-1, keepdims=True)
            ds = p * (dpr - dlt)
            dsk = -ps * dlt
            dsinks += [dsk[i * WINDOW:(i + 1) * WINDOW] for i in range(SWA_GROUP)]
            dss_ref[heads] += ds.reshape(SWA_GROUP, WINDOW, 2 * WINDOW)
            dsb = (ds * (HEAD_DIM ** -0.5)).astype(BF16)
            dqs.append(_dot(dsb, kh))
            dk.append(_dot_tn(dsb, qs))
            dv.append(_dot_tn(pb, dos))
        dqz_ref[:, 0:1024] = _swa_unstack(dqs).astype(BF16)
        dqz_ref[:, 1024:2048] = (dpv * _swa_unstack(outs) * _silu_grad(z)).astype(BF16)
        dsk_ref[...] += jnp.concatenate(dsinks, axis=1)
        dband = jnp.concatenate([dk[0], dk[1], dv[0], dv[1]], axis=1)
        prow = pl.multiple_of(jnp.maximum(n - 1, 0) * WINDOW, WINDOW)
        dkv_ref[pl.ds(prow, WINDOW), :] += dband[0:WINDOW]
        crow = pl.multiple_of(n * WINDOW, WINDOW)
        dkv_ref[pl.ds(crow, WINDOW), :] += dband[WINDOW:]

    return pl.pallas_call(
        body, grid=(nb,), in_specs=_swa_in_specs() + [_row_spec(WINDOW, 1024)],
        out_specs=[_row_spec(WINDOW, 2048), _full_spec((t, 256)), _full_spec((SWA_HEADS, WINDOW, 2 * WINDOW)),
                   _full_spec((WINDOW, SWA_HEADS))],
        out_shape=[S((t, 2048), BF16), S((t, 256), F32), S((SWA_HEADS, WINDOW, 2 * WINDOW), F32),
                   S((WINDOW, SWA_HEADS), F32)],
        name="swa_bwd", compiler_params=_cp(("arbitrary",)))(proj, proj, bias, sinks, dp)


MLA_Z0 = MLA_Q_RANK + MLA_KV_RANK + 128
MLA_Z1 = MLA_Z0 + 1024
MLA_PROJ = MLA_Z1 + 128


def _rope_tables(t):
    inv = ROPE_BASE ** (-jnp.arange(0, MLA_ROPE, 2, dtype=F32) / MLA_ROPE)
    ang = jnp.arange(t, dtype=F32)[:, None] * inv[None, :]
    return jnp.tile(jnp.cos(ang), (1, MLA_HEADS)), jnp.tile(jnp.sin(ang), (1, MLA_HEADS))


def _mla_norms(proj, gq, gkv):
    t = proj.shape[0]
    tr = 256

    def body(p_ref, gq_ref, gkv_ref, q_ref, kv_ref):
        for lo, hi, g_ref, o_ref in ((0, 768, gq_ref, q_ref), (768, 1024, gkv_ref, kv_ref)):
            xv = p_ref[:, lo:hi]
            r = lax.rsqrt(jnp.mean(xv * xv, axis=-1, keepdims=True) + EPS)
            o_ref[...] = (xv * r * g_ref[...]).astype(BF16)

    return pl.pallas_call(
        body, grid=(t // tr,), in_specs=[_row_spec(tr, 1024), _full_spec((1, 768)), _full_spec((1, 256))],
        out_specs=[_row_spec(tr, 768), _row_spec(tr, 256)], out_shape=[S((t, 768), BF16), S((t, 256), BF16)],
        name="mla_norms", compiler_params=_cp(("parallel",)))(proj, gq, gkv)


def _mla_norms_bwd(proj, dcqn, dckvn, dkr, dz, gq, gkv):
    t = proj.shape[0]
    tr = 256

    def body(p_ref, dq_ref, dkv_ref, dkr_ref, dz_ref, gq_ref, gkv_ref, dproj_ref, dgq_ref, dgkv_ref):
        @pl.when(pl.program_id(0) == 0)
        def _():
            dgq_ref[...] = jnp.zeros_like(dgq_ref)
            dgkv_ref[...] = jnp.zeros_like(dgkv_ref)

        dx, dg = _rms_bwd_rows(dq_ref[...], p_ref[:, 0:768], gq_ref[...])
        dproj_ref[:, 0:768] = dx.astype(BF16)
        dgq_ref[...] += dg
        dx, dg = _rms_bwd_rows(dkv_ref[...], p_ref[:, 768:1024], gkv_ref[...])
        dproj_ref[:, 768:1024] = dx.astype(BF16)
        dgkv_ref[...] += dg
        dproj_ref[:, 1024:MLA_Z0] = dkr_ref[...].astype(BF16)
        dproj_ref[:, MLA_Z0:MLA_Z1] = dz_ref[...]
        dproj_ref[:, MLA_Z1:MLA_PROJ] = jnp.zeros((tr, MLA_PROJ - MLA_Z1), BF16)

    return pl.pallas_call(
        body, grid=(t // tr,),
        in_specs=[_row_spec(tr, 1024), _row_spec(tr, 768), _row_spec(tr, 256), _row_spec(tr, 128), _row_spec(tr, 1024),
                  _full_spec((1, 768)), _full_spec((1, 256))],
        out_specs=[_row_spec(tr, MLA_PROJ), _full_spec((1, 768)), _full_spec((1, 256))],
        out_shape=[S((t, MLA_PROJ), BF16), S((1, 768), F32), S((1, 256), F32)],
        name="mla_norms_bwd", compiler_params=_cp(("arbitrary",)))(proj, dcqn, dckvn, dkr, dz, gq, gkv)


def _mla_pack(qf, kvf, proj, cos, sin):
    t = qf.shape[0]
    tr = 256

    def body(q_ref, kn_ref, kr_ref, cos_ref, sin_ref, qh_ref, kh_ref):
        cs, sn = cos_ref[...], sin_ref[...]
        x1, x2 = q_ref[:, 1024:1280], q_ref[:, 1280:1536]
        r1 = x1 * cs - x2 * sn
        r2 = x2 * cs + x1 * sn
        c16, s16 = cs[:, 0:16], sn[:, 0:16]
        k1, k2 = kr_ref[:, 0:16], kr_ref[:, 16:32]
        kr1 = k1 * c16 - k2 * s16
        kr2 = k2 * c16 + k1 * s16
        zpad = jnp.zeros((tr, 32), F32)
        for h in range(MLA_HEADS):
            qh_ref[h] = jnp.concatenate([q_ref[:, h * 64:(h + 1) * 64], r1[:, h * 16:(h + 1) * 16],
                                         r2[:, h * 16:(h + 1) * 16], zpad], axis=1).astype(BF16)
            kh_ref[h] = jnp.concatenate([kn_ref[:, h * 64:(h + 1) * 64], kr1, kr2, zpad], axis=1).astype(BF16)

    hspec = pl.BlockSpec((MLA_HEADS, tr, 128), lambda i: (0, i, 0))
    return pl.pallas_call(
        body, grid=(t // tr,),
        in_specs=[_row_spec(tr, 1536), _row_spec(tr, 1024), pl.BlockSpec((tr, 128), lambda i: (i, 8)),
                  _row_spec(tr, 256), _row_spec(tr, 256)],
        out_specs=[hspec, hspec], out_shape=[S((MLA_HEADS, t, 128), BF16)] * 2,
        name="mla_pack", compiler_params=_cp(("parallel",)))(qf, kvf, proj, cos, sin)


def _mla_unpack(dqh, dkh, dv, cos, sin):
    t = dqh.shape[1]
    tr = 256

    def body(dqh_ref, dkh_ref, dv_ref, cos_ref, sin_ref, dq_ref, dkv_ref, dkr_ref):
        cs, sn = cos_ref[...], sin_ref[...]
        dqn = jnp.concatenate([dqh_ref[h, :, 0:64] for h in range(MLA_HEADS)], axis=1)
        dr1 = jnp.concatenate([dqh_ref[h, :, 64:80] for h in range(MLA_HEADS)], axis=1)
        dr2 = jnp.concatenate([dqh_ref[h, :, 80:96] for h in range(MLA_HEADS)], axis=1)
        dq_ref[:, 0:1024] = dqn.astype(BF16)
        dq_ref[:, 1024:1280] = (dr1 * cs + dr2 * sn).astype(BF16)
        dq_ref[:, 1280:1536] = (dr2 * cs - dr1 * sn).astype(BF16)
        dkv_ref[:, 0:1024] = jnp.concatenate([dkh_ref[h, :, 0:64] for h in range(MLA_HEADS)], axis=1).astype(BF16)
        dkv_ref[:, 1024:2048] = dv_ref[...].astype(BF16)
        d1 = dkh_ref[0, :, 64:80]
        d2 = dkh_ref[0, :, 80:96]
        for h in range(1, MLA_HEADS):
            d1 = d1 + dkh_ref[h, :, 64:80]
            d2 = d2 + dkh_ref[h, :, 80:96]
        c16, s16 = cs[:, 0:16], sn[:, 0:16]
        dkr_ref[...] = jnp.concatenate([d1 * c16 + d2 * s16, d2 * c16 - d1 * s16, jnp.zeros((tr, 96), F32)], axis=1)

    hspec = pl.BlockSpec((MLA_HEADS, tr, 128), lambda i: (0, i, 0))
    return pl.pallas_call(
        body, grid=(t // tr,), in_specs=[hspec, hspec, _row_spec(tr, 1024), _row_spec(tr, 256), _row_spec(tr, 256)],
        out_specs=[_row_spec(tr, 1536), _row_spec(tr, 2048), _row_spec(tr, 128)],
        out_shape=[S((t, 1536), BF16), S((t, 2048), BF16), S((t, 128), F32)],
        name="mla_unpack", compiler_params=_cp(("parallel",)))(dqh, dkh, dv, cos, sin)


def _mla_attn_fwd(qh, kh, kvf, proj):
    t = qh.shape[1]
    b, bk = MLA_BQ, MLA_BK

    def body(q_ref, k_ref, v_ref, z_ref, o_ref, p_ref, lse_ref):
        r0 = pl.program_id(1) * b
        nfull = r0 // bk
        diff = (lax.broadcasted_iota(jnp.int32, (b, bk), 1) - lax.broadcasted_iota(jnp.int32, (b, bk), 0))
        qs = [q_ref[0], q_ref[1]]

        def block(c0, carry, masked):
            new = []
            for a in range(2):
                m, l, acc = carry[a]
                k = k_ref[a, pl.ds(c0, bk), :]
                v = v_ref[pl.ds(c0, bk), a * 64:(a + 1) * 64].astype(BF16)
                s = _dot_nt(qs[a], k) * MLA_SCALE
                if masked:
                    s = jnp.where(diff <= r0 - c0, s, NEG_INF)
                mn = jnp.maximum(m, jnp.max(s, axis=-1, keepdims=True))
                alpha = jnp.exp(m - mn)
                p = jnp.exp(s - mn)
                new.append((mn, alpha * l + jnp.sum(p, axis=-1, keepdims=True), alpha * acc + _dot(p.astype(BF16), v)))
            return tuple(new)

        init = (jnp.full((b, 1), NEG_INF, F32), jnp.zeros((b, 1), F32), jnp.zeros((b, 64), F32))
        carry = lax.fori_loop(0, nfull, lambda c, cr: block(pl.multiple_of(c * bk, bk), cr, False), (init, init))
        for i in range(max(1, b // bk)):
            carry = block(pl.multiple_of((nfull + i) * bk, bk), carry, True)
        outs = [acc / l for _, l, acc in carry]
        lses = [jnp.broadcast_to(m + jnp.log(l), (b, 64)) for m, l, _ in carry]
        o = jnp.concatenate(outs, axis=1)
        o_ref[...] = o
        p_ref[...] = (o * _silu(z_ref[...])).astype(BF16)
        lse_ref[0] = jnp.concatenate(lses, axis=1)

    return pl.pallas_call(
        body, grid=(MLA_HEADS // 2, t // b),
        in_specs=[pl.BlockSpec((2, b, 128), lambda i, r: (i, r, 0)), pl.BlockSpec((2, t, 128), lambda i, r: (i, 0, 0)),
                  pl.BlockSpec((t, 128), lambda i, r: (0, 8 + i)), pl.BlockSpec((b, 128), lambda i, r: (r, 9 + i))],
        out_specs=[pl.BlockSpec((b, 128), lambda i, r: (r, i)), pl.BlockSpec((b, 128), lambda i, r: (r, i)),
                   pl.BlockSpec((1, b, 128), lambda i, r: (i, r, 0))],
        out_shape=[S((t, 1024), F32), S((t, 1024), BF16), S((MLA_HEADS // 2, t, 128), F32)],
        name="mla_attn_fwd", compiler_params=_cp(("parallel", "parallel")))(qh, kh, kvf, proj)


def _mla_attn_bwd(qh, kh, kvf, proj, o, dpout, lse):
    t = qh.shape[1]
    b, bk = MLA_BQ, MLA_BK
    nb = t // b

    def body(q_ref, k_ref, v_ref, z_ref, o_ref, dp_ref, lse_ref, dq_ref, dk_ref, dv_ref, dz_ref):
        dk_ref[...] = jnp.zeros_like(dk_ref)
        dv_ref[...] = jnp.zeros_like(dv_ref)
        diff = (lax.broadcasted_iota(jnp.int32, (b, bk), 1) - lax.broadcasted_iota(jnp.int32, (b, bk), 0))
        lanes = [slice(0, 64), slice(64, 128)]

        def rbody(r, carry):
            r0 = pl.multiple_of(r * b, b)
            rows = pl.ds(r0, b)
            qs, dobs, deltas, lses = [], [], [], []
            for a in range(2):
                zz = z_ref[rows, lanes[a]]
                oo = o_ref[rows, lanes[a]]
                dpo = dp_ref[rows, lanes[a]].astype(F32)
                do = dpo * _silu(zz)
                dz_ref[rows, lanes[a]] = (dpo * oo * _silu_grad(zz)).astype(BF16)
                qs.append(q_ref[a, rows, :])
                dobs.append(do.astype(BF16))
                deltas.append(jnp.sum(do * oo, axis=-1, keepdims=True))
                lses.append(lse_ref[0, rows, a * 64:a * 64 + 1])

            def block(c0, dqs, masked):
                cols = pl.ds(c0, bk)
                out = []
                for a in range(2):
                    k = k_ref[a, cols, :]
                    v = v_ref[cols, lanes[a]].astype(BF16)
                    s = _dot_nt(qs[a], k) * MLA_SCALE
                    if masked:
                        s = jnp.where(diff <= r0 - c0, s, NEG_INF)
                    p = jnp.exp(s - lses[a])
                    ds = p * (_dot_nt(dobs[a], v) - deltas[a]) * MLA_SCALE
                    dsb = ds.astype(BF16)
                    dk_ref[a, cols, :] += _dot_tn(dsb, qs[a])
                    dv_ref[cols, lanes[a]] += _dot_tn(p.astype(BF16), dobs[a])
                    out.append(dqs[a] + _dot(dsb, k))
                return tuple(out)

            zero = jnp.zeros((b, 128), F32)
            nfull = r0 // bk
            dqs = lax.fori_loop(0, nfull, lambda c, d: block(pl.multiple_of(c * bk, bk), d, False), (zero, zero))
            for i in range(max(1, b // bk)):
                dqs = block(pl.multiple_of((nfull + i) * bk, bk), dqs, True)
            dq_ref[0, rows, :] = dqs[0]
            dq_ref[1, rows, :] = dqs[1]
            return carry

        lax.fori_loop(0, nb, rbody, 0)

    pair3 = pl.BlockSpec((2, t, 128), lambda i: (i, 0, 0))
    return pl.pallas_call(
        body, grid=(MLA_HEADS // 2,),
        in_specs=[pair3, pair3, pl.BlockSpec((t, 128), lambda i: (0, 8 + i)), pl.BlockSpec((t, 128), lambda i: (0, 9 + i)),
                  pl.BlockSpec((t, 128), lambda i: (0, i)), pl.BlockSpec((t, 128), lambda i: (0, i)),
                  pl.BlockSpec((1, t, 128), lambda i: (i, 0, 0))],
        out_specs=[pair3, pair3, pl.BlockSpec((t, 128), lambda i: (0, i)), pl.BlockSpec((t, 128), lambda i: (0, i))],
        out_shape=[S((MLA_HEADS, t, 128), F32), S((MLA_HEADS, t, 128), F32), S((t, 1024), F32), S((t, 1024), BF16)],
        name="mla_attn_bwd", compiler_params=_cp(("parallel",), VMEM_BIG))(qh, kh, kvf, proj, o, dpout, lse)


def _dot_nt_hi(a, b):
    return lax.dot_general(a, b, (((1,), (1,)), ((), ())), preferred_element_type=F32,
                           precision=lax.Precision.HIGHEST)


SSM_PACK = 2


def _ssm_gen(lr, li, ldt, btr, bti, cr, ci):
    n = SSM_BLOCK
    lanes = SSM_PACK * SSM_STATE
    dt = jnp.exp(ldt)
    x = lr * dt
    w = li * dt
    mag = jnp.exp(x)
    ab_re = mag * jnp.cos(w)
    ab_im = mag * jnp.sin(w)
    den = lr * lr + li * li
    nr = ab_re - 1.0
    f_re = (nr * lr + ab_im * li) / den
    f_im = (ab_im * lr - nr * li) / den
    bb_re = f_re * btr - f_im * bti
    bb_im = f_re * bti + f_im * btr
    inv_mag = jnp.exp(-x)
    inv_re, inv_im = inv_mag * jnp.cos(w), -inv_mag * jnp.sin(w)
    one, zero = jnp.ones_like(x), jnp.zeros_like(x)
    pos, neg = [(one, zero)], [(one, zero)]
    for _ in range(SSM_CHUNK):
        pr, pi = pos[-1]
        pos.append((pr * ab_re - pi * ab_im, pr * ab_im + pi * ab_re))
        pr, pi = neg[-1]
        neg.append((pr * inv_re - pi * inv_im, pr * inv_im + pi * inv_re))

    def stack(table, idx):
        return (jnp.concatenate([jnp.broadcast_to(table[idx(t)][0], (SSM_GROUP, lanes)) for t in range(SSM_CHUNK)], axis=0),
                jnp.concatenate([jnp.broadcast_to(table[idx(t)][1], (SSM_GROUP, lanes)) for t in range(SSM_CHUNK)], axis=0))

    def rep(v):
        return jnp.concatenate([v] * SSM_CHUNK, axis=0)

    ct_re, ct_im, bt_re, bt_im = rep(cr), rep(ci), rep(bb_re), rep(bb_im)
    p_re, p_im = stack(pos, lambda t: t)
    l_re = ct_re * p_re - ct_im * p_im
    l_im = ct_re * p_im + ct_im * p_re
    n_re, n_im = stack(neg, lambda t: t)
    r_re = bt_re * n_re - bt_im * n_im
    r_im = bt_re * n_im + bt_im * n_re
    rk = lax.broadcasted_iota(jnp.int32, (n, n), 0) // SSM_GROUP
    ct = lax.broadcasted_iota(jnp.int32, (n, n), 1) // SSM_GROUP
    mts = []
    for g in range(SSM_PACK):
        sl = slice(g * SSM_STATE, (g + 1) * SSM_STATE)
        mts.append(jnp.where(ct >= rk, _dot_nt_hi(r_re[:, sl], l_re[:, sl]) - _dot_nt_hi(r_im[:, sl], l_im[:, sl]), 0.0))
    e_re, e_im = stack(pos, lambda t: SSM_CHUNK - 1 - t)
    pin_re = bt_re * e_re - bt_im * e_im
    pin_im = bt_re * e_im + bt_im * e_re
    q_re, q_im = stack(pos, lambda t: t + 1)
    qt_re = ct_re * q_re - ct_im * q_im
    qt_im = -(ct_re * q_im + ct_im * q_re)
    ad_re, ad_im = pos[SSM_CHUNK]
    return tuple(mts), pin_re, pin_im, qt_re, qt_im, ad_re, ad_im


_SSM_GEN_OUT = [(SSM_BLOCK, SSM_BLOCK)] + [(SSM_BLOCK, SSM_STATE)] * 4 + [(1, SSM_STATE)] * 2
_SSM_GEN_IN = [(1, SSM_STATE)] * 3 + [(SSM_GROUP, SSM_STATE)] * 4


def _gspec(shape2):
    return pl.BlockSpec((SSM_PACK,) + shape2, lambda g: (g, 0, 0))


def _pack_lanes(ref):
    return jnp.concatenate([ref[g] for g in range(SSM_PACK)], axis=1)


def _unpack_lanes(ref, val):
    for g in range(SSM_PACK):
        ref[g] = val[:, g * SSM_STATE:(g + 1) * SSM_STATE]


def _ssm_operators(params):
    def body(*refs):
        ins, outs = refs[:7], refs[7:]
        vals = _ssm_gen(*[_pack_lanes(r) for r in ins])
        for g in range(SSM_PACK):
            outs[0][g] = vals[0][g]
        for o_ref, val in zip(outs[1:], vals[1:]):
            _unpack_lanes(o_ref, val)

    return pl.pallas_call(
        body, grid=(SSM_GROUPS // SSM_PACK,), in_specs=[_gspec(s) for s in _SSM_GEN_IN],
        out_specs=[_gspec(s) for s in _SSM_GEN_OUT],
        out_shape=[S((SSM_GROUPS,) + s, F32) for s in _SSM_GEN_OUT], name="ssm_operators",
        compiler_params=_cp(("parallel",)))(*params)


def _ssm_operators_bwd(params, cots, after=()):
    def body(*refs):
        ins, cts, outs = refs[:7], refs[7:14], refs[14 + len(after):]
        _, vjp = jax.vjp(_ssm_gen, *[_pack_lanes(r) for r in ins])
        grads = list(vjp((tuple(cts[0][g] for g in range(SSM_PACK)),) + tuple(_pack_lanes(r) for r in cts[1:])))
        for o_ref, val in zip(outs, grads):
            _unpack_lanes(o_ref, val)
        for g in range(SSM_PACK):
            dldt = grads[2][:, g * SSM_STATE:(g + 1) * SSM_STATE]
            outs[2][g] = jnp.broadcast_to(jnp.sum(dldt, axis=-1, keepdims=True), (1, SSM_STATE))

    return pl.pallas_call(
        body, grid=(SSM_GROUPS // SSM_PACK,),
        in_specs=[_gspec(s) for s in _SSM_GEN_IN] + [_gspec(s) for s in _SSM_GEN_OUT]
        + [pl.BlockSpec(memory_space=pl.ANY)] * len(after),
        out_specs=[_gspec(s) for s in _SSM_GEN_IN], out_shape=[S((SSM_GROUPS,) + s, F32) for s in _SSM_GEN_IN],
        name="ssm_operators_bwd", compiler_params=_cp(("parallel",)))(*params, *cots, *after)


def _shift_rows(x, sh, row, up):
    n = x.shape[0]
    if up:
        return jnp.where(row < n - sh, pltpu.roll(x, n - sh, axis=0), 0.0)
    return jnp.where(row >= sh, pltpu.roll(x, sh, axis=0), 0.0)


def _carry_scan(sr, si, ar, ai, row, up):
    n = sr.shape[0]
    sh = 1
    while sh < n:
        tr, ti = _shift_rows(sr, sh, row, up), _shift_rows(si, sh, row, up)
        sr, si = sr + ar * tr - ai * ti, si + ar * ti + ai * tr
        ar, ai = ar * ar - ai * ai, 2.0 * ar * ai
        sh *= 2
    return sr, si


def _ssm_states(u, p_re, p_im, ar, ai, row):
    lre = _dot(u, p_re.astype(BF16))
    lim = _dot(u, p_im.astype(BF16))
    sr, si = _carry_scan(lre, lim, ar, ai, row, False)
    return _shift_rows(sr, 1, row, False), _shift_rows(si, 1, row, False)


SSM_GPB = 128 // SSM_GROUP


def _step_rows(ref, nc):
    return [ref[pl.ds(k, nc, stride=SSM_CHUNK), :].astype(BF16) for k in range(SSM_CHUNK)]


def _group_of(rows, j):
    return jnp.concatenate([r[:, j * SSM_GROUP:(j + 1) * SSM_GROUP] for r in rows], axis=1)


def _store_groups(o_ref, per_group, nc):
    for k in range(SSM_CHUNK):
        o_ref[pl.ds(k, nc, stride=SSM_CHUNK), :] = jnp.concatenate(
            [y[:, k * SSM_GROUP:(k + 1) * SSM_GROUP] for y in per_group], axis=1)


def _bspec(shape2):
    return pl.BlockSpec((SSM_GPB,) + shape2, lambda j: (j, 0, 0))


def _ssm_core_fwd(proj, ops):
    t = proj.shape[0]
    nc = t // SSM_CHUNK

    def body(u_ref, mt_ref, pr_ref, pi_ref, qr_ref, qi_ref, ar_ref, ai_ref, y_ref):
        rows = _step_rows(u_ref, nc)
        row = lax.broadcasted_iota(jnp.int32, (nc, SSM_STATE), 0)
        ys = []
        for j in range(SSM_GPB):
            u = _group_of(rows, j)
            s_re, s_im = _ssm_states(u, pr_ref[j], pi_ref[j], ar_ref[j], ai_ref[j], row)
            ys.append(_dot(u, mt_ref[j].astype(BF16)) + _dot_nt(s_re.astype(BF16), qr_ref[j].astype(BF16))
                      + _dot_nt(s_im.astype(BF16), qi_ref[j].astype(BF16)))
        _store_groups(y_ref, ys, nc)

    blk = pl.BlockSpec((t, 128), lambda j: (0, j))
    return pl.pallas_call(
        body, grid=(SSM_GROUPS // SSM_GPB,), in_specs=[blk] + [_bspec(s) for s in _SSM_GEN_OUT], out_specs=blk,
        out_shape=S((t, 1024), F32), name="ssm_core_fwd", compiler_params=_cp(("parallel",)))(proj, *ops)


def _ssm_core_bwd(proj, dy, ops):
    t = proj.shape[0]
    nc = t // SSM_CHUNK

    def body(u_ref, dy_ref, mt_ref, pr_ref, pi_ref, qr_ref, qi_ref, ar_ref, ai_ref,
             du_ref, dmt_ref, dpr_ref, dpi_ref, dqr_ref, dqi_ref, dar_ref, dai_ref):
        u_rows = _step_rows(u_ref, nc)
        dy_rows = _step_rows(dy_ref, nc)
        row = lax.broadcasted_iota(jnp.int32, (nc, SSM_STATE), 0)
        dus = []
        for j in range(SSM_GPB):
            u = _group_of(u_rows, j)
            dyj = _group_of(dy_rows, j)
            ar, ai = ar_ref[j], ai_ref[j]
            s_re, s_im = _ssm_states(u, pr_ref[j], pi_ref[j], ar, ai, row)
            g_re = _shift_rows(_dot(dyj, qr_ref[j].astype(BF16)), 1, row, True)
            g_im = _shift_rows(_dot(dyj, qi_ref[j].astype(BF16)), 1, row, True)
            l_re, l_im = _carry_scan(g_re, g_im, ar, -ai, row, True)
            lrb, lib = l_re.astype(BF16), l_im.astype(BF16)
            dus.append(_dot_nt(dyj, mt_ref[j].astype(BF16)) + _dot_nt(lrb, pr_ref[j].astype(BF16))
                       + _dot_nt(lib, pi_ref[j].astype(BF16)))
            dmt_ref[j] = _dot_tn(u, dyj)
            dpr_ref[j] = _dot_tn(u, lrb)
            dpi_ref[j] = _dot_tn(u, lib)
            dqr_ref[j] = _dot_tn(dyj, s_re.astype(BF16))
            dqi_ref[j] = _dot_tn(dyj, s_im.astype(BF16))
            dar_ref[j] = jnp.sum(l_re * s_re + l_im * s_im, axis=0, keepdims=True)
            dai_ref[j] = jnp.sum(l_im * s_re - l_re * s_im, axis=0, keepdims=True)
        _store_groups(du_ref, dus, nc)

    blk = pl.BlockSpec((t, 128), lambda j: (0, j))
    return pl.pallas_call(
        body, grid=(SSM_GROUPS // SSM_GPB,), in_specs=[blk, blk] + [_bspec(s) for s in _SSM_GEN_OUT],
        out_specs=[blk] + [_bspec(s) for s in _SSM_GEN_OUT],
        out_shape=[S((t, 1024), F32)] + [S((SSM_GROUPS,) + s, F32) for s in _SSM_GEN_OUT],
        name="ssm_core_bwd", compiler_params=_cp(("parallel",)))(proj, dy, *ops)


def _s5_act(y_core, proj, d_skip):
    t = y_core.shape[0]
    tr = 256

    def body(y_ref, u_ref, d_ref, o_ref):
        o_ref[...] = _gelu(y_ref[...] + d_ref[...] * u_ref[...])

    return pl.pallas_call(body, grid=(t // tr,), in_specs=[_row_spec(tr, 1024), _row_spec(tr, 1024), _full_spec((1, 1024))],
                          out_specs=_row_spec(tr, 1024), out_shape=S((t, 1024), F32), name="s5_act",
                          compiler_params=_cp(("parallel",)))(y_core, proj, d_skip)


def _s5_gate(yg, tg, b_glu, proj):
    t = yg.shape[0]
    tr = 256

    def body(y_ref, t_ref, b_ref, z_ref, o_ref):
        o_ref[...] = (y_ref[...] * jax.nn.sigmoid(t_ref[...] + b_ref[...]) * _silu(z_ref[...])).astype(BF16)

    return pl.pallas_call(
        body, grid=(t // tr,),
        in_specs=[_row_spec(tr, 1024), _row_spec(tr, 1024), _full_spec((1, 1024)), pl.BlockSpec((tr, 1024), lambda i: (i, 1))],
        out_specs=_row_spec(tr, 1024), out_shape=S((t, 1024), BF16), name="s5_gate",
        compiler_params=_cp(("parallel",)))(yg, tg, b_glu, proj)


def _s5_gate_bwd(dp, yg, tg, b_glu, proj):
    t = yg.shape[0]
    tr = 256

    def body(dp_ref, y_ref, t_ref, b_ref, z_ref, dt_ref, dy_ref, dz_ref, db_ref):
        @pl.when(pl.program_id(0) == 0)
        def _():
            db_ref[...] = jnp.zeros_like(db_ref)

        z = z_ref[...]
        yv = y_ref[...]
        sg = jax.nn.sigmoid(t_ref[...] + b_ref[...])
        dpv = dp_ref[...].astype(F32)
        do = dpv * _silu(z)
        dz_ref[...] = (dpv * yv * sg * _silu_grad(z)).astype(BF16)
        dy_ref[...] = do * sg
        dtv = do * yv * sg * (1.0 - sg)
        dt_ref[...] = dtv.astype(BF16)
        db_ref[...] += jnp.sum(dtv, axis=0, keepdims=True)

    return pl.pallas_call(
        body, grid=(t // tr,),
        in_specs=[_row_spec(tr, 1024), _row_spec(tr, 1024), _row_spec(tr, 1024), _full_spec((1, 1024)),
                  pl.BlockSpec((tr, 1024), lambda i: (i, 1))],
        out_specs=[_row_spec(tr, 1024), _row_spec(tr, 1024), _row_spec(tr, 1024), _full_spec((1, 1024))],
        out_shape=[S((t, 1024), BF16), S((t, 1024), F32), S((t, 1024), BF16), S((1, 1024), F32)],
        name="s5_gate_bwd", compiler_params=_cp(("arbitrary",)))(dp, yg, tg, b_glu, proj)


def _s5_act_bwd(dyg1, dyg2, y_core, proj, d_skip):
    t = y_core.shape[0]
    tr = 256

    def body(a_ref, b_ref, y_ref, u_ref, d_ref, dy_ref, du_ref, dd_ref):
        @pl.when(pl.program_id(0) == 0)
        def _():
            dd_ref[...] = jnp.zeros_like(dd_ref)

        u = u_ref[...]
        dv = (a_ref[...] + b_ref[...]) * _gelu_grad(y_ref[...] + d_ref[...] * u)
        dy_ref[...] = dv
        du_ref[...] = dv * d_ref[...]
        dd_ref[...] += jnp.sum(dv * u, axis=0, keepdims=True)

    return pl.pallas_call(
        body, grid=(t // tr,),
        in_specs=[_row_spec(tr, 1024)] * 4 + [_full_spec((1, 1024))],
        out_specs=[_row_spec(tr, 1024), _row_spec(tr, 1024), _full_spec((1, 1024))],
        out_shape=[S((t, 1024), F32), S((t, 1024), F32), S((1, 1024), F32)],
        name="s5_act_bwd", compiler_params=_cp(("arbitrary",)))(dyg1, dyg2, y_core, proj, d_skip)


def _s5_dproj(du_skip, du_core, dz):
    t = du_skip.shape[0]
    tr = 256

    def body(a_ref, b_ref, z_ref, o_ref):
        o_ref[:, 0:1024] = (a_ref[...] + b_ref[...]).astype(BF16)
        o_ref[:, 1024:2048] = z_ref[...]

    return pl.pallas_call(body, grid=(t // tr,), in_specs=[_row_spec(tr, 1024)] * 3, out_specs=_row_spec(tr, 2048),
                          out_shape=S((t, 2048), BF16), name="s5_dproj",
                          compiler_params=_cp(("parallel",)))(du_skip, du_core, dz)


def _s5_params(lam_re, lam_im, log_dt, b_re, b_im, c_re, c_im):
    g = SSM_GROUPS
    return (lam_re.reshape(g, 1, SSM_STATE), lam_im.reshape(g, 1, SSM_STATE),
            jnp.broadcast_to(log_dt.reshape(g, 1, 1), (g, 1, SSM_STATE)),
            b_re.transpose(0, 2, 1), b_im.transpose(0, 2, 1), c_re, c_im)


def _s5_fwd(proj, params, d_skip, w_glu_after, b_glu):
    ops = _ssm_operators(params)
    y_core = _ssm_core_fwd(proj, ops)
    yg = _s5_act(y_core, proj, d_skip)
    tg = _mm(yg, w_glu_after(yg), name="s5_glu")
    return _s5_gate(yg, tg, b_glu, proj), (ops, y_core, yg, tg)


def _s5_bwd(proj, res, dp, params, d_skip, w_glu, b_glu):
    ops, y_core, yg, tg = res
    dtg, dyg1, dz, db_glu = _s5_gate_bwd(dp, yg, tg, b_glu, proj)
    dyg2 = _mm(dtg, w_glu, tb=True, name="s5_glu_dx")
    dw_glu = _mm(yg, dtg, ta=True, out_dtype=BF16, name="s5_glu_dw")
    dy_core, du_skip, dd = _s5_act_bwd(dyg1, dyg2, y_core, proj, d_skip)
    outs = _ssm_core_bwd(proj, dy_core, ops)
    dproj = _s5_dproj(du_skip, outs[0], dz)
    return dproj, outs[1:], dd, dw_glu, db_glu


def _s5_param_grads(params, cots, after=()):
    glr, gli, gdt, gbtr, gbti, gcr, gci = _ssm_operators_bwd(params, cots, after)
    g = SSM_GROUPS
    return (glr.reshape(g, SSM_STATE), gli.reshape(g, SSM_STATE), gdt[:, 0, 0].reshape(1, g),
            gbtr.transpose(0, 2, 1), gbti.transpose(0, 2, 1), gcr, gci)


def _perm_uq(w):
    w3 = w.reshape(w.shape[0], MLA_HEADS, MLA_NOPE + MLA_ROPE)
    return jnp.concatenate([w3[:, :, :64].reshape(-1, 1024), w3[:, :, 64:80].reshape(-1, 256),
                            w3[:, :, 80:96].reshape(-1, 256)], axis=1)


def _unperm_uq(w):
    r = w.shape[0]
    return jnp.concatenate([w[:, :1024].reshape(r, 16, 64), w[:, 1024:1280].reshape(r, 16, 16),
                            w[:, 1280:1536].reshape(r, 16, 16)], axis=2).reshape(r, 1536)


def _perm_ukv(w):
    w3 = w.reshape(w.shape[0], MLA_HEADS, 128)
    return jnp.concatenate([w3[:, :, :64].reshape(-1, 1024), w3[:, :, 64:].reshape(-1, 1024)], axis=1)


def _unperm_ukv(w):
    r = w.shape[0]
    return jnp.concatenate([w[:, :1024].reshape(r, 16, 64), w[:, 1024:].reshape(r, 16, 64)], axis=2).reshape(r, 2048)


def _mla_fwd(proj, gq, gkv, wuq, wukv, cos, sin):
    cqn, ckvn = _mla_norms(proj, gq, gkv)
    qf = _mm(cqn, wuq, name="mla_uq")
    kvf = _mm(ckvn, wukv, name="mla_ukv")
    qh, kh = _mla_pack(qf, kvf, proj, cos, sin)
    o, pout, lse = _mla_attn_fwd(qh, kh, kvf, proj)
    return pout, (cqn, ckvn, kvf, qh, kh, o, lse)


def _mla_bwd(proj, res, dpout, gq, gkv, wuq, wukv, cos, sin):
    cqn, ckvn, kvf, qh, kh, o, lse = res
    dqh, dkh, dv, dz = _mla_attn_bwd(qh, kh, kvf, proj, o, dpout, lse)
    dqf, dkvf, dkr = _mla_unpack(dqh, dkh, dv, cos, sin)
    dcqn = _mm(dqf, wuq, tb=True, name="mla_uq_dx")
    dwuq = _mm(cqn, dqf, ta=True, out_dtype=BF16, name="mla_uq_dw")
    dckvn = _mm(dkvf, wukv, tb=True, name="mla_ukv_dx")
    dwukv = _mm(ckvn, dkvf, ta=True, out_dtype=BF16, name="mla_ukv_dw")
    dproj, dgq, dgkv = _mla_norms_bwd(proj, dcqn, dckvn, dkr, dz, gq, gkv)
    return dproj, dgq, dgkv, dwuq, dwukv


_ANY = pl.BlockSpec(memory_space=pl.ANY)


def _chip_peers():
    x, y, c = lax.axis_index("x"), lax.axis_index("y"), lax.axis_index("c")
    return 2 * x + y, c, [(1 - x, y), (x, 1 - y), (1 - x, 1 - y)]


def _sibling_swap(ps, name):
    n = len(ps)

    def body(*refs):
        p_refs, q_refs, send_sems, recv_sems = refs[:n], refs[n:2 * n], refs[2 * n], refs[2 * n + 1]
        x, y, c = lax.axis_index("x"), lax.axis_index("y"), lax.axis_index("c")
        cps = [pltpu.make_async_remote_copy(src_ref=p_refs[i], dst_ref=q_refs[i], send_sem=send_sems.at[i],
                                            recv_sem=recv_sems.at[i], device_id=(x, y, 1 - c), device_id_type=MESH)
               for i in range(n)]
        for cp in cps:
            cp.start()
        for cp in cps:
            cp.wait()

    return pl.pallas_call(body, in_specs=[_ANY] * n, out_specs=[_ANY] * n, out_shape=[S(p.shape, p.dtype) for p in ps],
                          scratch_shapes=[pltpu.SemaphoreType.DMA((n,)), pltpu.SemaphoreType.DMA((n,))], name=name)(*ps)


def _quarter_spec(tr, r, c, dim):
    if dim is None:
        return pl.BlockSpec((1, tr, c), lambda i, me: (me[0], i, 0))
    if dim == 0:
        return pl.BlockSpec((tr, c), lambda i, me: (me[0] * (r // tr) + i, 0))
    return pl.BlockSpec((tr, c), lambda i, me: (i, me[0]))


def _whole_shape(r, c, dim):
    return (N_CHIPS, r, c) if dim is None else ((N_CHIPS * r, c) if dim == 0 else (r, N_CHIPS * c))


def _row_tile(r):
    return 256 if r % 256 == 0 else (128 if r % 128 == 0 else r)


def _cast_place(me, x, dim, dtype, name):
    r, c = x.shape
    tr = _row_tile(r)

    def body(me_ref, x_ref, o_ref):
        o_ref[...] = x_ref[...].astype(dtype).reshape(o_ref.shape)

    return pl.pallas_call(
        body, out_shape=S(_whole_shape(r, c, dim), dtype), name=name,
        grid_spec=pltpu.PrefetchScalarGridSpec(num_scalar_prefetch=1, grid=(r // tr,),
                                               in_specs=[pl.BlockSpec((tr, c), lambda i, me: (i, 0))],
                                               out_specs=_quarter_spec(tr, r, c, dim)),
        compiler_params=_cp(("parallel",)))(me, x)


def _sum_parts(me, g, land, dim, name):
    _, r, c = land.shape
    tr = _row_tile(r)

    def body(me_ref, g_ref, l_ref, o_ref):
        own = g_ref[...].astype(F32).reshape(tr, c)
        o_ref[...] = ((own + l_ref[0].astype(F32)) + l_ref[1].astype(F32)) + l_ref[2].astype(F32)

    return pl.pallas_call(
        body, out_shape=S((r, c), F32), name=name,
        grid_spec=pltpu.PrefetchScalarGridSpec(
            num_scalar_prefetch=1, grid=(r // tr,),
            in_specs=[_quarter_spec(tr, r, c, dim), pl.BlockSpec((3, tr, c), lambda i, me: (0, i, 0))],
            out_specs=pl.BlockSpec((tr, c), lambda i, me: (i, 0))),
        compiler_params=_cp(("parallel",)))(me, g, land)


_HBM = pl.BlockSpec(memory_space=pltpu.HBM)
_SEM = pl.BlockSpec(memory_space=pltpu.SEMAPHORE)
_EFFECT = pltpu.SideEffectType.DATAFLOW_SIDE_EFFECTING


def _quarter(ref, s, dim):
    n = ref.shape[dim] // N_CHIPS
    start = pl.multiple_of(s * n, n)
    return ref.at[pl.ds(start, n), :] if dim == 0 else ref.at[:, pl.ds(start, n)]


def _part(ref, s, dim):
    return ref.at[s] if dim is None else _quarter(ref, s, dim)


def _gather_plan(i, dim):
    return dict(src=i, dst=i, send_src=lambda ref, me, peer, k: _part(ref, me, dim),
                send_dst=lambda ref, me, peer, k: _part(ref, me, dim), recv_dst=lambda ref, me, peer, k: _part(ref, peer, dim))


def _scatter_plan(i, j, dim):
    return dict(src=i, dst=j, send_src=lambda ref, me, peer, k: _part(ref, peer, dim),
                send_dst=lambda ref, me, peer, k: ref.at[k], recv_dst=lambda ref, me, peer, k: ref.at[k])


def _hbm(a):
    return pltpu.with_memory_space_constraint(a, pltpu.HBM)


def _remote_copies(refs, send_sems, recv_sems, plans):
    me, c, peers = _chip_peers()
    sends, recvs = [], []
    for i, p in enumerate(plans):
        for k, (px, py) in enumerate(peers):
            peer = 2 * px + py
            sems = dict(send_sem=send_sems.at[3 * i + k], recv_sem=recv_sems.at[3 * i + k], device_id=(px, py, c),
                        device_id_type=MESH)
            src = p['send_src'](refs[p['src']], me, peer, k)
            sends.append(pltpu.make_async_remote_copy(src_ref=src, dst_ref=p['send_dst'](refs[p['dst']], me, peer, k), **sems))
            recvs.append(pltpu.make_async_remote_copy(src_ref=src, dst_ref=p['recv_dst'](refs[p['dst']], me, peer, k), **sems))
    return sends, recvs


def _exchange_start(arrays, plans, name, after=()):
    n, m, extra = len(arrays), len(plans), len(after)

    def body(*refs):
        send_sems, recv_sems, token = refs[n + extra], refs[n + extra + 1], refs[-1]
        sends, _ = _remote_copies(refs[:n], send_sems, recv_sems, plans)
        for cp in sends:
            cp.start()
        token[...] = jnp.zeros_like(token)

    outs = pl.pallas_call(
        body, name=name,
        out_shape=(pltpu.SemaphoreType.DMA((3 * m,)), pltpu.SemaphoreType.DMA((3 * m,)),
                   *[pltpu.HBM(a.shape, a.dtype) for a in arrays], S((8, 128), F32)),
        in_specs=[_HBM] * n + [_ANY] * extra, out_specs=(_SEM, _SEM, *[_HBM] * n, pl.BlockSpec(memory_space=pltpu.VMEM)),
        input_output_aliases={i: 2 + i for i in range(n)},
        compiler_params=pltpu.CompilerParams(has_side_effects=_EFFECT))(*[_hbm(a) for a in arrays], *after)
    return outs[0], outs[1], outs[2:2 + n], outs[-1]


def _exchange_wait(started, plans, after, name):
    send_sems, recv_sems, arrays, _ = started
    n = len(arrays)

    def body(*refs):
        sends, recvs = _remote_copies(refs[:n], refs[n], refs[n + 1], plans)
        for cp in sends:
            cp.wait_send()
        for cp in recvs:
            cp.wait_recv()

    return pl.pallas_call(
        body, name=name, out_shape=tuple(pltpu.HBM(a.shape, a.dtype) for a in arrays),
        in_specs=[_HBM] * n + [_SEM, _SEM, _ANY], out_specs=tuple([_HBM] * n),
        input_output_aliases={i: i for i in range(n)},
        compiler_params=pltpu.CompilerParams(has_side_effects=_EFFECT))(*arrays, send_sems, recv_sems, after)


def _add2(p, q):
    r, c = p.shape
    tr = 128 if r % 128 == 0 else r

    def body(p_ref, q_ref, o_ref):
        o_ref[...] = p_ref[...] + q_ref[...]

    return pl.pallas_call(body, grid=(r // tr,), in_specs=[_row_spec(tr, c), _row_spec(tr, c)], out_specs=_row_spec(tr, c),
                          out_shape=S((r, c), F32), name="add_cores", compiler_params=_cp(("parallel",)))(p, q)


def _adamw(w, g, m, v, name, g2=None, after=()):
    r, c = w.shape
    tr = 256 if r % 256 == 0 else (128 if r % 128 == 0 else r)
    c1 = 1.0 / (1.0 - ADAM_B1 ** ADAM_STEP)
    c2 = 1.0 / (1.0 - ADAM_B2 ** ADAM_STEP)
    gs = [g] if g2 is None else [g, g2]

    def body(*refs):
        w_ref, m_ref, v_ref = refs[0], refs[1], refs[2]
        g_refs = refs[3:3 + len(gs)]
        go_ref, d_ref, nm_ref, nv_ref = refs[3 + len(gs) + len(after):]
        gv = g_refs[0][...]
        if len(gs) == 2:
            gv = gv + g_refs[1][...]
        go_ref[...] = gv
        nm = ADAM_B1 * m_ref[...] + (1.0 - ADAM_B1) * gv
        nv = ADAM_B2 * v_ref[...] + (1.0 - ADAM_B2) * (gv * gv)
        nm_ref[...] = nm
        nv_ref[...] = nv
        d_ref[...] = -ADAM_LR * ((nm * c1) / (jnp.sqrt(nv * c2) + ADAM_EPS) + ADAM_WD * w_ref[...])

    return pl.pallas_call(body, grid=(r // tr,),
                          in_specs=[_row_spec(tr, c)] * (3 + len(gs)) + [pl.BlockSpec(memory_space=pl.ANY)] * len(after),
                          out_specs=[_row_spec(tr, c)] * 4, out_shape=[S((r, c), F32)] * 4, name=name,
                          compiler_params=_cp(("parallel",)))(w, m, v, *gs, *after)


_WEIGHTS = ['pre_norm', 'post_norm', 'rel_bias', 'a_w_in', 'a_lam_re', 'a_lam_im', 'a_log_dt', 'a_b_re', 'a_b_im',
            'a_c_re', 'a_c_im', 'a_d', 'a_w_glu', 'a_b_glu', 'a_w_out', 'b_w_in', 'b_sinks', 'b_w_out', 'c_w_in',
            'c_q_norm', 'c_kv_norm', 'c_w_uq', 'c_w_ukv', 'c_w_out', 'd_w_in', 'd_ln_g', 'd_ln_b', 'd_w_s', 'd_b_s',
            'd_w_out']
_BIG = [('a_w_in', 1024, 2048, 1), ('a_w_glu', 1024, 1024, 0), ('a_w_out', 1024, 1024, 0), ('b_w_in', 1024, 2304, 1),
        ('b_w_out', 1024, 1024, 0), ('c_w_in', 1024, 2080, 1), ('c_w_uq', 768, 1536, 1), ('c_w_ukv', 256, 2048, 1),
        ('c_w_out', 1024, 1024, 0), ('d_w_in', 1024, 3072, 1), ('d_w_out', 1024, 1024, 0)]
_SHARDED_VECS = [('c_q_norm', 768), ('c_kv_norm', 256), ('d_ln_g', 1024), ('d_ln_b', 1024)]
_REPLICATED = [n for n in _WEIGHTS if n not in [b[0] for b in _BIG] and n not in [s[0] for s in _SHARDED_VECS]]
_LANES = 1024
_SMALL_Q_ROWS = 136


def kernel(x, pre_norm, post_norm, rel_bias, a_w_in, a_lam_re, a_lam_im, a_log_dt, a_b_re, a_b_im, a_c_re, a_c_im, a_d, a_w_glu, a_b_glu, a_w_out, b_w_in, b_sinks, b_w_out, c_w_in, c_q_norm, c_kv_norm, c_w_uq, c_w_ukv, c_w_out, d_w_in, d_ln_g, d_ln_b, d_w_s, d_b_s, d_w_out, loss_target, m_pre_norm, m_post_norm, m_rel_bias, m_a_w_in, m_a_lam_re, m_a_lam_im, m_a_log_dt, m_a_b_re, m_a_b_im, m_a_c_re, m_a_c_im, m_a_d, m_a_w_glu, m_a_b_glu, m_a_w_out, m_b_w_in, m_b_sinks, m_b_w_out, m_c_w_in, m_c_q_norm, m_c_kv_norm, m_c_w_uq, m_c_w_ukv, m_c_w_out, m_d_w_in, m_d_ln_g, m_d_ln_b, m_d_w_s, m_d_b_s, m_d_w_out, v_pre_norm, v_post_norm, v_rel_bias, v_a_w_in, v_a_lam_re, v_a_lam_im, v_a_log_dt, v_a_b_re, v_a_b_im, v_a_c_re, v_a_c_im, v_a_d, v_a_w_glu, v_a_b_glu, v_a_w_out, v_b_w_in, v_b_sinks, v_b_w_out, v_c_w_in, v_c_q_norm, v_c_kv_norm, v_c_w_uq, v_c_w_ukv, v_c_w_out, v_d_w_in, v_d_ln_g, v_d_ln_b, v_d_w_s, v_d_b_s, v_d_w_out):
    arg = dict(locals())
    me = 2 * lax.axis_index("x") + lax.axis_index("y")
    xin = x[0]
    t = xin.shape[0]

    big = {n: (r, c, dim) for n, r, c, dim in _BIG}

    me1 = me.reshape(1).astype(jnp.int32)

    def gather_start(names, tag, after):
        wholes, plans = [], []
        for i, n in enumerate(names):
            if n in big:
                r, c, dim = big[n]
                src, dtype = arg[n][0], BF16
                if dim == 1 and (c // N_CHIPS) % 128:
                    dim = None
            else:
                src, dim, dtype = arg[n], None, F32
            wholes.append(_cast_place(me1, src, dim, dtype, "place_" + n))
            plans.append(_gather_plan(i, dim))
        return _exchange_start(wholes, plans, "gather_start_" + tag, after=after), plans

    def gather_wait(names, started, after, tag):
        st, plans = started
        out = {}
        for n, a in zip(names, _exchange_wait(st, plans, after, "gather_wait_" + tag)):
            if n in big:
                r, c, dim = big[n]
                out[n] = a if a.ndim == 2 else a.transpose(1, 0, 2).reshape(r, c)
            else:
                out[n] = a.reshape(1, -1)
        return out

    groups = {'a1': ['a_w_in'], 'a2': ['a_w_glu', 'a_w_out'], 'b': ['b_w_in', 'b_w_out'],
              'c': ['c_w_in', 'c_w_uq', 'c_w_ukv', 'c_w_out', 'c_q_norm', 'c_kv_norm'],
              'd': ['d_w_in', 'd_w_out', 'd_ln_g', 'd_ln_b']}
    started, tokens = {}, []
    for tag, names in groups.items():
        started[tag] = gather_start(names, tag, tokens)
        tokens = [started[tag][0][3]]
    wt = gather_wait(groups['a1'], started['a1'], xin, 'a1')
    cos, sin = _rope_tables(t)
    bucket = jnp.asarray(_t5_bucket_table())
    s5p = _s5_params(a_lam_re[0], a_lam_im[0], a_log_dt[0], a_b_re[0], a_b_im[0], a_c_re[0], a_c_im[0])
    bst = d_b_s[0].T

    x0 = xin
    h0 = _prenorm(x0, pre_norm[0:1], "pre0", after=tokens)
    proj0 = _mm(h0, wt['a_w_in'], name="a_in")

    def w_glu_after(x):
        wt.update(gather_wait(groups['a2'], started['a2'], x, 'a2'))
        return wt['a_w_glu']

    p0, res0 = _s5_fwd(proj0, s5p, a_d, w_glu_after, a_b_glu)
    y0 = _mm(p0, wt['a_w_out'], name="a_out")
    x1, h1 = _postnorm_prenorm(x0, y0, post_norm[0:1], pre_norm[1:2], "post0_pre1")

    wt.update(gather_wait(groups['b'], started['b'], x1, 'b'))
    proj1 = _mm(h1, wt['b_w_in'], name="b_in")
    bias = _swa_bias(rel_bias, bucket)
    p1 = _swa_fwd(proj1, bias, b_sinks)
    y1 = _mm(p1, wt['b_w_out'], name="b_out")
    x2, h2 = _postnorm_prenorm(x1, y1, post_norm[1:2], pre_norm[2:3], "post1_pre2")

    wt.update(gather_wait(groups['c'], started['c'], x2, 'c'))
    w_c_in = jnp.concatenate([wt['c_w_in'][:, :1056], jnp.zeros((1024, 96), BF16), wt['c_w_in'][:, 1056:],
                              jnp.zeros((1024, MLA_PROJ - MLA_Z1), BF16)], axis=1)
    w_uq = _perm_uq(wt['c_w_uq'])
    w_ukv = _perm_ukv(wt['c_w_ukv'])
    proj2 = _mm(h2, w_c_in, name="c_in")
    p2, res2 = _mla_fwd(proj2, wt['c_q_norm'], wt['c_kv_norm'], w_uq, w_ukv, cos, sin)
    y2 = _mm(p2, wt['c_w_out'], name="c_out")
    x3, h3 = _postnorm_prenorm(x2, y2, post_norm[2:3], pre_norm[3:4], "post2_pre3")

    wt.update(gather_wait(groups['d'], started['d'], x3, 'd'))
    vec = wt
    proj3 = _mm(h3, wt['d_w_in'], name="d_in")
    p3 = _sgu_fwd(proj3, vec['d_ln_g'], vec['d_ln_b'], d_w_s[0], bst)
    y3 = _mm(p3, wt['d_w_out'], name="d_out")
    loss_part, dx = _postnorm_loss(x3, y3, post_norm[3:4], loss_target[0])
    loss = lax.psum(loss_part[0, 0], ("x", "y", "c"))

    gr = {}
    d_pre, d_post = [None] * 4, [None] * 4
    scattered = []

    def scatter_start(items, tag):
        srcs = [a for _, a, _ in items]
        n = len(items)
        plans = [_scatter_plan(i, n + i, dim) for i, (_, _, dim) in enumerate(items)]
        lands = []
        for _, a, dim in items:
            part = a.shape[1:] if dim is None else tuple(s // N_CHIPS if i == dim else s for i, s in enumerate(a.shape))
            lands.append(lax.empty((3,) + part, a.dtype))
        st = _exchange_start(srcs + lands, plans, "scatter_start_" + tag)
        scattered.append((st, plans, items, tag))
        return [st[3]]

    def stacked(g):
        r, c = g.shape
        return g.reshape(r, N_CHIPS, c // N_CHIPS).transpose(1, 0, 2)

    dy, d_post[3] = _postnorm_bwd(dx, y3, post_norm[3:4], "post3_bwd")
    dp = _mm(dy, wt['d_w_out'], tb=True, name="d_out_dx")
    g_out = _mm(p3, dy, ta=True, out_dtype=BF16, name="d_out_dw")
    dproj, gr['d_w_s'], dbst, gr['d_ln_g'], gr['d_ln_b'] = _sgu_bwd(proj3, dp, vec['d_ln_g'], vec['d_ln_b'], d_w_s[0], bst)
    gr['d_b_s'] = dbst.T
    dh = _mm(dproj, wt['d_w_in'], tb=True, name="d_in_dx")
    g_in = _mm(h3, dproj, ta=True, out_dtype=BF16, name="d_in_dw")
    tok = scatter_start([('d_w_in', g_in, 1), ('d_w_out', g_out, 0)], 'd')
    dx, d_pre[3], dy, d_post[2] = _norm_bwd_pair(dx, dh, x3, pre_norm[3:4], y2, post_norm[2:3], "pre3_post2_bwd", after=tok)

    dp = _mm(dy, wt['c_w_out'], tb=True, name="c_out_dx")
    g_out = _mm(p2, dy, ta=True, out_dtype=BF16, name="c_out_dw")
    dproj, gr['c_q_norm'], gr['c_kv_norm'], dwuq, dwukv = _mla_bwd(proj2, res2, dp, wt['c_q_norm'], wt['c_kv_norm'],
                                                                  w_uq, w_ukv, cos, sin)
    dh = _mm(dproj, w_c_in, tb=True, name="c_in_dx")
    dwc = _mm(h2, dproj, ta=True, out_dtype=BF16, name="c_in_dw")
    g_in = stacked(jnp.concatenate([dwc[:, :1056], dwc[:, MLA_Z0:MLA_Z1]], axis=1))
    tok = scatter_start([('c_w_in', g_in, None), ('c_w_uq', _unperm_uq(dwuq), 1), ('c_w_ukv', _unperm_ukv(dwukv), 1),
                         ('c_w_out', g_out, 0)], 'c')
    dx, d_pre[2], dy, d_post[1] = _norm_bwd_pair(dx, dh, x2, pre_norm[2:3], y1, post_norm[1:2], "pre2_post1_bwd", after=tok)

    dp = _mm(dy, wt['b_w_out'], tb=True, name="b_out_dx")
    g_out = _mm(p1, dy, ta=True, out_dtype=BF16, name="b_out_dw")
    dqz, dkv, dss, dsk = _swa_bwd(proj1, dp, bias, b_sinks)
    gr['rel_bias'], gr['b_sinks'] = _swa_dbias(dss, dsk, bucket)
    dproj = jnp.concatenate([dqz[:, :1024], dkv.astype(BF16), dqz[:, 1024:]], axis=1)
    dh = _mm(dproj, wt['b_w_in'], tb=True, name="b_in_dx")
    g_in = stacked(_mm(h1, dproj, ta=True, out_dtype=BF16, name="b_in_dw"))
    tok = scatter_start([('b_w_in', g_in, None), ('b_w_out', g_out, 0)], 'b')
    dx, d_pre[1], dy, d_post[0] = _norm_bwd_pair(dx, dh, x1, pre_norm[1:2], y0, post_norm[0:1], "pre1_post0_bwd", after=tok)

    dp = _mm(dy, wt['a_w_out'], tb=True, name="a_out_dx")
    g_out = _mm(p0, dy, ta=True, out_dtype=BF16, name="a_out_dw")
    dproj, s5cots, gr['a_d'], g_glu, gr['a_b_glu'] = _s5_bwd(proj0, res0, dp, s5p, a_d, wt['a_w_glu'], a_b_glu)
    tok = scatter_start([('a_w_glu', g_glu, 0), ('a_w_out', g_out, 0)], 'a2')
    g_in = _mm(h0, dproj, ta=True, out_dtype=BF16, name="a_in_dw", after=tok)
    tok = scatter_start([('a_w_in', g_in, 1)], 'a1')
    dh = _mm(dproj, wt['a_w_in'], tb=True, name="a_in_dx", after=tok)
    dx, d_pre[0] = _prenorm_bwd(dx, dh, x0, pre_norm[0:1], "pre0_bwd")
    s5g = _s5_param_grads(s5p, s5cots, after=tok)
    for n, g in zip(['a_lam_re', 'a_lam_im', 'a_log_dt', 'a_b_re', 'a_b_im', 'a_c_re', 'a_c_im'], s5g):
        gr[n] = g
    gr['pre_norm'] = jnp.concatenate(d_pre, axis=0)
    gr['post_norm'] = jnp.concatenate(d_post, axis=0)
    small_names = _REPLICATED + [n for n, _ in _SHARDED_VECS]
    small = jnp.concatenate([gr[n].reshape(-1) for n in small_names])
    small = jnp.pad(small, (0, N_CHIPS * _SMALL_Q_ROWS * _LANES - small.shape[0])).reshape(N_CHIPS, _SMALL_Q_ROWS, _LANES)
    tok = scatter_start([('small', small, None)], 'a0')

    partial = {}
    for st, plans, items, tag in scattered[:-1]:
        done = _exchange_wait(st, plans, tok[0], "scatter_wait_" + tag)
        for i, (n, _, dim) in enumerate(items):
            partial[n] = _sum_parts(me1, done[i], done[len(items) + i], dim, "sum_parts_" + n)
    big_names = [n for n, _, _, _ in _BIG]
    other = dict(zip(big_names, _sibling_swap([partial[n] for n in big_names], "swap_cores")))
    st, plans, _, tag = scattered[-1]
    done = _exchange_wait(st, plans, other[big_names[-1]], "scatter_wait_" + tag)
    p_small = _sum_parts(me1, done[0], done[1], None, "sum_parts_small")
    small_q = _add2(p_small, _sibling_swap([p_small], "swap_cores_small")[0])
    gplan = [_gather_plan(0, None)]
    gathering = _exchange_start([_cast_place(me1, small_q, None, F32, "place_small_grads")], gplan, "gather_small_start")

    delta, new_m, new_v, grads = {}, {}, {}, {}
    for n in big_names:
        shp = arg[n].shape
        two = (shp[1], shp[2])
        outs = _adamw(arg[n].reshape(two), partial[n], arg['m_' + n].reshape(two), arg['v_' + n].reshape(two),
                      "adamw_" + n, g2=other[n], after=[gathering[3]])
        grads[n], delta[n], new_m[n], new_v[n] = [o.reshape(shp) for o in outs]
    small_all = _exchange_wait(gathering, gplan, new_v[big_names[-1]], "gather_small_wait")[0].reshape(-1)
    off = 0
    for n in _REPLICATED:
        sz = math.prod(arg[n].shape)
        grads[n] = small_all[off:off + sz].reshape(arg[n].shape)
        off += sz
    for n, sz in _SHARDED_VECS:
        q = sz // N_CHIPS
        grads[n] = lax.dynamic_slice(small_all, (off + me * q,), (q,)).reshape(arg[n].shape)
        off += sz

    for n in small_names:
        shp = arg[n].shape
        two = (math.prod(shp[:-1]), shp[-1])
        outs = _adamw(arg[n].reshape(two), grads[n].reshape(two), arg['m_' + n].reshape(two), arg['v_' + n].reshape(two),
                      "adamw_" + n)
        delta[n], new_m[n], new_v[n] = [o.reshape(shp) for o in outs[1:]]

    return (loss, dx.reshape(x.shape), *[grads[n] for n in _WEIGHTS], *[delta[n] for n in _WEIGHTS],
            *[new_m[n] for n in _WEIGHTS], *[new_v[n] for n in _WEIGHTS])
```

```python
import functools
import math

import numpy as np
import jax
import jax.numpy as jnp
from jax import lax
from jax.experimental import pallas as pl
from jax.experimental.pallas import tpu as pltpu

F32 = jnp.float32
BF16 = jnp.bfloat16
S = jax.ShapeDtypeStruct
MESH = pl.DeviceIdType.MESH

D_MODEL = 1024
EPS = 1e-6
NEG_INF = -1e30
N_CHIPS = 4

SSM_GROUPS = 64
SSM_GROUP = 16
SSM_STATE = 64
SSM_CHUNK = 16
SSM_BLOCK = SSM_CHUNK * SSM_GROUP

HEAD_DIM = 64
SWA_HEADS = 16
SWA_GROUP = 8
WINDOW = 128
REL_BUCKETS = 32
REL_MAX_DIST = 128

MLA_HEADS = 16
MLA_NOPE = 64
MLA_ROPE = 32
MLA_Q_RANK = 768
MLA_KV_RANK = 256
MLA_SCALE = (MLA_NOPE + MLA_ROPE) ** -0.5
MLA_BQ = 512
MLA_BK = 512
ROPE_BASE = 10000.0

SGU_CHUNK = 128
SGU_GROUPS = 16
SGU_GDIM = 64

ADAM_LR = 0.001
ADAM_B1 = 0.9
ADAM_B2 = 0.999
ADAM_EPS = 1e-08
ADAM_WD = 0.01
ADAM_STEP = 10

VMEM_BIG = 56 * 1024 * 1024


def _cp(sem, vmem=None):
    return pltpu.CompilerParams(dimension_semantics=sem, vmem_limit_bytes=vmem)


def _row_spec(tr, c):
    return pl.BlockSpec((tr, c), lambda i: (i, 0))


def _full_spec(shape):
    nd = len(shape)
    return pl.BlockSpec(shape, lambda *_: (0,) * nd)


def _gelu(x):
    c = 0.7978845608028654
    return 0.5 * x * (1.0 + jnp.tanh(c * (x + 0.044715 * x * x * x)))


def _gelu_grad(x):
    c = 0.7978845608028654
    t = jnp.tanh(c * (x + 0.044715 * x * x * x))
    return 0.5 * (1.0 + t) + 0.5 * x * (1.0 - t * t) * c * (1.0 + 3.0 * 0.044715 * x * x)


def _silu(z):
    return z * jax.nn.sigmoid(z)


def _silu_grad(z):
    s = jax.nn.sigmoid(z)
    return s * (1.0 + z * (1.0 - s))


def _dot(a, b):
    return lax.dot_general(a, b, (((1,), (0,)), ((), ())), preferred_element_type=F32)


def _dot_nt(a, b):
    return lax.dot_general(a, b, (((1,), (1,)), ((), ())), preferred_element_type=F32)


def _dot_tn(a, b):
    return lax.dot_general(a, b, (((0,), (0,)), ((), ())), preferred_element_type=F32)


def _pick(n, cap):
    best = 0
    for d in range(128, min(n, cap) + 1, 128):
        if n % d == 0:
            best = d
    if best < 256:
        return n
    return best


def _mm(a, b, *, ta=False, tb=False, out_dtype=F32, name, after=()):
    m, k = (a.shape[1], a.shape[0]) if ta else a.shape
    n = b.shape[0] if tb else b.shape[1]
    assert (b.shape[1] if tb else b.shape[0]) == k
    tm, tn, tk = _pick(m, 1024), _pick(n, 1024), _pick(k, 1024)
    if tn > 1024:
        tm = _pick(m, 256)
    if tk > 1024:
        tm, tn = _pick(m, 256), _pick(n, 256)
    nk = k // tk
    dims = (((0 if ta else 1,), (1 if tb else 0,)), ((), ()))

    def body(a_ref, b_ref, *rest):
        o_ref, acc_ref = rest[-2], rest[-1]
        kk = pl.program_id(2)

        @pl.when(kk == 0)
        def _():
            acc_ref[...] = jnp.zeros_like(acc_ref)

        acc_ref[...] += lax.dot_general(a_ref[...].astype(BF16), b_ref[...].astype(BF16), dims,
                                        preferred_element_type=F32)

        @pl.when(kk == nk - 1)
        def _():
            o_ref[...] = acc_ref[...].astype(out_dtype)

    a_spec = pl.BlockSpec((tk, tm), lambda i, j, kk: (kk, i)) if ta else pl.BlockSpec((tm, tk), lambda i, j, kk: (i, kk))
    b_spec = pl.BlockSpec((tn, tk), lambda i, j, kk: (j, kk)) if tb else pl.BlockSpec((tk, tn), lambda i, j, kk: (kk, j))
    return pl.pallas_call(
        body, grid=(m // tm, n // tn, nk), in_specs=[a_spec, b_spec] + [pl.BlockSpec(memory_space=pl.ANY)] * len(after),
        out_specs=pl.BlockSpec((tm, tn), lambda i, j, kk: (i, j)), out_shape=S((m, n), out_dtype),
        scratch_shapes=[pltpu.VMEM((tm, tn), F32)], name=name,
        compiler_params=_cp(("parallel", "parallel", "arbitrary"), VMEM_BIG))(a, b, *after)


def _prenorm(x, g, name, after=()):
    t, d = x.shape
    tr = 256

    def body(x_ref, g_ref, *rest):
        o_ref = rest[-1]
        xv = x_ref[...]
        r = lax.rsqrt(jnp.mean(xv * xv, axis=-1, keepdims=True) + EPS)
        o_ref[...] = (xv * r * g_ref[...]).astype(BF16)

    return pl.pallas_call(body, grid=(t // tr,),
                          in_specs=[_row_spec(tr, d), _full_spec((1, d))] + [pl.BlockSpec(memory_space=pl.ANY)] * len(after),
                          out_specs=_row_spec(tr, d), out_shape=S((t, d), BF16), name=name,
                          compiler_params=_cp(("parallel",)))(x, g, *after)


def _rms_bwd_rows(dout, xin, g):
    r = lax.rsqrt(jnp.mean(xin * xin, axis=-1, keepdims=True) + EPS)
    xh = xin * r
    gd = dout * g
    dx = r * (gd - xh * jnp.mean(gd * xh, axis=-1, keepdims=True))
    return dx, jnp.sum(dout * xh, axis=0, keepdims=True)


def _postnorm_bwd(dxn, y, g, name, after=()):
    t, d = y.shape
    tr = 256

    def body(dx_ref, y_ref, g_ref, *rest):
        dy_ref, dg_ref = rest[-2], rest[-1]

        @pl.when(pl.program_id(0) == 0)
        def _():
            dg_ref[...] = jnp.zeros_like(dg_ref)

        dy, dg = _rms_bwd_rows(dx_ref[...], y_ref[...], g_ref[...])
        dy_ref[...] = dy.astype(BF16)
        dg_ref[...] += dg

    return pl.pallas_call(body, grid=(t // tr,),
                          in_specs=[_row_spec(tr, d), _row_spec(tr, d), _full_spec((1, d))]
                          + [pl.BlockSpec(memory_space=pl.ANY)] * len(after),
                          out_specs=[_row_spec(tr, d), _full_spec((1, d))],
                          out_shape=[S((t, d), BF16), S((1, d), F32)], name=name,
                          compiler_params=_cp(("arbitrary",)))(dxn, y, g, *after)


def _prenorm_bwd(dxn, dh, x, g, name):
    t, d = x.shape
    tr = 256

    def body(dxn_ref, dh_ref, x_ref, g_ref, dx_ref, dg_ref):
        @pl.when(pl.program_id(0) == 0)
        def _():
            dg_ref[...] = jnp.zeros_like(dg_ref)

        dx, dg = _rms_bwd_rows(dh_ref[...], x_ref[...], g_ref[...])
        dx_ref[...] = dxn_ref[...] + dx
        dg_ref[...] += dg

    return pl.pallas_call(body, grid=(t // tr,),
                          in_specs=[_row_spec(tr, d), _row_spec(tr, d), _row_spec(tr, d), _full_spec((1, d))],
                          out_specs=[_row_spec(tr, d), _full_spec((1, d))],
                          out_shape=[S((t, d), F32), S((1, d), F32)], name=name,
                          compiler_params=_cp(("arbitrary",)))(dxn, dh, x, g)


def _postnorm_prenorm(x, y, gpost, gpre, name):
    t, d = x.shape
    tr = 256

    def body(x_ref, y_ref, gp_ref, gn_ref, xn_ref, h_ref):
        yv = y_ref[...]
        r = lax.rsqrt(jnp.mean(yv * yv, axis=-1, keepdims=True) + EPS)
        xn = x_ref[...] + yv * r * gp_ref[...]
        xn_ref[...] = xn
        r2 = lax.rsqrt(jnp.mean(xn * xn, axis=-1, keepdims=True) + EPS)
        h_ref[...] = (xn * r2 * gn_ref[...]).astype(BF16)

    return pl.pallas_call(body, grid=(t // tr,),
                          in_specs=[_row_spec(tr, d), _row_spec(tr, d), _full_spec((1, d)), _full_spec((1, d))],
                          out_specs=[_row_spec(tr, d), _row_spec(tr, d)], out_shape=[S((t, d), F32), S((t, d), BF16)],
                          name=name, compiler_params=_cp(("parallel",)))(x, y, gpost, gpre)


def _norm_bwd_pair(dxn, dh, x, gpre, yprev, gpost, name, after=()):
    t, d = x.shape
    tr = 256

    def body(dxn_ref, dh_ref, x_ref, gn_ref, y_ref, gp_ref, *rest):
        dx_ref, dgn_ref, dy_ref, dgp_ref = rest[-4:]

        @pl.when(pl.program_id(0) == 0)
        def _():
            dgn_ref[...] = jnp.zeros_like(dgn_ref)
            dgp_ref[...] = jnp.zeros_like(dgp_ref)

        dxa, dgn = _rms_bwd_rows(dh_ref[...], x_ref[...], gn_ref[...])
        dx = dxn_ref[...] + dxa
        dx_ref[...] = dx
        dgn_ref[...] += dgn
        dy, dgp = _rms_bwd_rows(dx, y_ref[...], gp_ref[...])
        dy_ref[...] = dy.astype(BF16)
        dgp_ref[...] += dgp

    rows, vec = _row_spec(tr, d), _full_spec((1, d))
    return pl.pallas_call(body, grid=(t // tr,),
                          in_specs=[rows, rows, rows, vec, rows, vec] + [pl.BlockSpec(memory_space=pl.ANY)] * len(after),
                          out_specs=[rows, vec, rows, vec],
                          out_shape=[S((t, d), F32), S((1, d), F32), S((t, d), BF16), S((1, d), F32)], name=name,
                          compiler_params=_cp(("arbitrary",)))(dxn, dh, x, gpre, yprev, gpost, *after)


def _postnorm_loss(x, y, gpost, target):
    t, d = x.shape
    tr = 256

    def body(x_ref, y_ref, g_ref, t_ref, l_ref, dy_ref):
        @pl.when(pl.program_id(0) == 0)
        def _():
            l_ref[...] = jnp.zeros_like(l_ref)

        yv = y_ref[...]
        r = lax.rsqrt(jnp.mean(yv * yv, axis=-1, keepdims=True) + EPS)
        e = (x_ref[...] + yv * r * g_ref[...]) - t_ref[...]
        dy_ref[...] = e * (1.0 / d)
        l_ref[...] += 0.5 * jnp.sum(jnp.mean(e * e, axis=-1, keepdims=True), axis=0, keepdims=True)

    return pl.pallas_call(body, grid=(t // tr,),
                          in_specs=[_row_spec(tr, d), _row_spec(tr, d), _full_spec((1, d)), _row_spec(tr, d)],
                          out_specs=[_full_spec((1, 1)), _row_spec(tr, d)],
                          out_shape=[S((1, 1), F32), S((t, d), F32)], name="postnorm_loss",
                          compiler_params=_cp(("arbitrary",)))(x, y, gpost, target)


def _sgu_parts(proj, lg, lb, ws_ref, bst):
    a = proj[:, 0:1024]
    bb = proj[:, 1024:2048]
    z = proj[:, 2048:3072]
    u = _gelu(a)
    vp = _gelu(bb)
    mu = jnp.mean(vp, axis=-1, keepdims=True)
    xc = vp - mu
    rstd = lax.rsqrt(jnp.mean(xc * xc, axis=-1, keepdims=True) + EPS)
    xh = xc * rstd
    vn = xh * lg + lb
    row = lax.broadcasted_iota(jnp.int32, (SGU_CHUNK, SGU_CHUNK), 0)
    col = lax.broadcasted_iota(jnp.int32, (SGU_CHUNK, SGU_CHUNK), 1)
    tril = row >= col
    ws = [jnp.where(tril, ws_ref[g], 0.0).astype(BF16) for g in range(SGU_GROUPS)]
    s = jnp.concatenate(
        [_dot(ws[g], vn[:, g * 64:(g + 1) * 64].astype(BF16)) + bst[:, g:g + 1] for g in range(SGU_GROUPS)], axis=1)
    return a, bb, z, u, vp, rstd, xh, vn, ws, tril, s


def _sgu_fwd(proj, lg, lb, w_s, bst):
    t = proj.shape[0]
    tr = SGU_CHUNK

    def body(p_ref, lg_ref, lb_ref, ws_ref, bst_ref, o_ref):
        _, _, z, u, _, _, _, _, _, _, s = _sgu_parts(p_ref[...], lg_ref[...], lb_ref[...], ws_ref, bst_ref[...])
        o_ref[...] = (u * s * _silu(z)).astype(BF16)

    return pl.pallas_call(
        body, grid=(t // tr,),
        in_specs=[_row_spec(tr, 3072), _full_spec((1, 1024)), _full_spec((1, 1024)), _full_spec((16, 128, 128)),
                  _full_spec((128, 16))],
        out_specs=_row_spec(tr, 1024), out_shape=S((t, 1024), BF16), name="sgu_fwd",
        compiler_params=_cp(("parallel",)))(proj, lg, lb, w_s, bst)


def _sgu_bwd(proj, dp, lg, lb, w_s, bst):
    t = proj.shape[0]
    tr = SGU_CHUNK

    def body(p_ref, dp_ref, lg_ref, lb_ref, ws_ref, bst_ref, dproj_ref, dws_ref, dbst_ref, dlg_ref, dlb_ref):
        @pl.when(pl.program_id(0) == 0)
        def _():
            dws_ref[...] = jnp.zeros_like(dws_ref)
            dbst_ref[...] = jnp.zeros_like(dbst_ref)
            dlg_ref[...] = jnp.zeros_like(dlg_ref)
            dlb_ref[...] = jnp.zeros_like(dlb_ref)

        lgv = lg_ref[...]
        a, bb, z, u, vp, rstd, xh, vn, ws, tril, s = _sgu_parts(p_ref[...], lgv, lb_ref[...], ws_ref, bst_ref[...])
        dpv = dp_ref[...].astype(F32)
        sz = _silu(z)
        du = dpv * s * sz
        ds = dpv * u * sz
        dz = dpv * u * s * _silu_grad(z)
        dvn_parts, dbs_cols = [], []
        for g in range(SGU_GROUPS):
            dsg = ds[:, g * 64:(g + 1) * 64]
            dsg_b = dsg.astype(BF16)
            dvn_parts.append(_dot_tn(ws[g], dsg_b))
            dws_ref[g] += jnp.where(tril, _dot_nt(dsg_b, vn[:, g * 64:(g + 1) * 64].astype(BF16)), 0.0)
            dbs_cols.append(jnp.sum(dsg, axis=1, keepdims=True))
        dvn = jnp.concatenate(dvn_parts, axis=1)
        dbst_ref[...] += jnp.concatenate(dbs_cols, axis=1)
        dlg_ref[...] += jnp.sum(dvn * xh, axis=0, keepdims=True)
        dlb_ref[...] += jnp.sum(dvn, axis=0, keepdims=True)
        dxh = dvn * lgv
        dvp = rstd * (dxh - jnp.mean(dxh, axis=-1, keepdims=True) - xh * jnp.mean(dxh * xh, axis=-1, keepdims=True))
        dproj_ref[:, 0:1024] = (du * _gelu_grad(a)).astype(BF16)
        dproj_ref[:, 1024:2048] = (dvp * _gelu_grad(bb)).astype(BF16)
        dproj_ref[:, 2048:3072] = dz.astype(BF16)

    return pl.pallas_call(
        body, grid=(t // tr,),
        in_specs=[_row_spec(tr, 3072), _row_spec(tr, 1024), _full_spec((1, 1024)), _full_spec((1, 1024)),
                  _full_spec((16, 128, 128)), _full_spec((128, 16))],
        out_specs=[_row_spec(tr, 3072), _full_spec((16, 128, 128)), _full_spec((128, 16)), _full_spec((1, 1024)),
                   _full_spec((1, 1024))],
        out_shape=[S((t, 3072), BF16), S((16, 128, 128), F32), S((128, 16), F32), S((1, 1024), F32),
                   S((1, 1024), F32)],
        name="sgu_bwd", compiler_params=_cp(("arbitrary",)))(proj, dp, lg, lb, w_s, bst)


def _t5_bucket_table():
    qi = np.arange(WINDOW)[:, None]
    kj = np.arange(2 * WINDOW)[None, :]
    dist = np.maximum(qi + WINDOW - kj, 0)
    max_exact = REL_BUCKETS // 2
    dist_f = np.maximum(dist, 1).astype(np.float32)
    large = max_exact + (np.log(dist_f / np.float32(max_exact)) / np.float32(math.log(REL_MAX_DIST / max_exact))
                         * np.float32(REL_BUCKETS - max_exact)).astype(np.int32)
    large = np.minimum(large, REL_BUCKETS - 1)
    return np.where(dist < max_exact, dist, large).astype(np.int32)


def _swa_bias(rel_bias, bucket):
    def body(rb_ref, bk_ref, o_ref):
        bk = bk_ref[...]
        for h in range(SWA_HEADS):
            acc = jnp.zeros((WINDOW, 2 * WINDOW), F32)
            for b in range(REL_BUCKETS):
                acc = jnp.where(bk == b, rb_ref[b, h], acc)
            o_ref[h] = acc

    return pl.pallas_call(
        body, in_specs=[pl.BlockSpec(memory_space=pltpu.SMEM), pl.BlockSpec(memory_space=pltpu.VMEM)],
        out_specs=pl.BlockSpec(memory_space=pltpu.VMEM), out_shape=S((SWA_HEADS, WINDOW, 2 * WINDOW), F32),
        name="swa_bias")(rel_bias, bucket)


def _swa_dbias(ds_sum, dsink_rows, bucket):
    def body(ds_ref, dsk_ref, bk_ref, drb_ref, dsink_ref):
        bk = bk_ref[...]
        r = lax.broadcasted_iota(jnp.int32, (REL_BUCKETS, SWA_HEADS), 0)
        c = lax.broadcasted_iota(jnp.int32, (REL_BUCKETS, SWA_HEADS), 1)
        out = jnp.zeros((REL_BUCKETS, SWA_HEADS), F32)
        for b in range(REL_BUCKETS):
            m = bk == b
            for h in range(SWA_HEADS):
                val = jnp.sum(jnp.where(m, ds_ref[h], 0.0))
                out = jnp.where((r == b) & (c == h), val, out)
        drb_ref[...] = out
        dsink_ref[...] = jnp.sum(dsk_ref[...], axis=0, keepdims=True)

    return pl.pallas_call(
        body, out_shape=[S((REL_BUCKETS, SWA_HEADS), F32), S((1, SWA_HEADS), F32)], name="swa_dbias",
    )(ds_sum, dsink_rows, bucket)


def _swa_band(cur_ref, prev_ref):
    kband = jnp.concatenate([prev_ref[:, 0:128], cur_ref[:, 1024:1152]], axis=0).astype(BF16)
    vband = jnp.concatenate([prev_ref[:, 128:256], cur_ref[:, 1152:1280]], axis=0).astype(BF16)
    return kband, vband


SWA_ROWS = SWA_GROUP * WINDOW


def _swa_valid(n, rows=SWA_ROWS):
    qi = lax.broadcasted_iota(jnp.int32, (rows, 2 * WINDOW), 0) & (WINDOW - 1)
    kj = lax.broadcasted_iota(jnp.int32, (rows, 2 * WINDOW), 1)
    dist = qi + WINDOW - kj
    return (dist >= 0) & (dist < WINDOW) & (n * WINDOW + kj - WINDOW >= 0)


def _swa_stack(x, j):
    return jnp.concatenate([x[:, h * 64:(h + 1) * 64] for h in range(j * SWA_GROUP, (j + 1) * SWA_GROUP)], axis=0)


def _swa_unstack(parts):
    return jnp.concatenate([p[i * WINDOW:(i + 1) * WINDOW] for p in parts for i in range(SWA_GROUP)], axis=1)


def _swa_sink_col(sink_ref, j):
    return jnp.concatenate([jnp.full((WINDOW, 1), sink_ref[0, h], F32)
                            for h in range(j * SWA_GROUP, (j + 1) * SWA_GROUP)], axis=0)


def _swa_probs(qh, kh, bias_h, sink, valid):
    s = _dot_nt(qh, kh) * (HEAD_DIM ** -0.5) + bias_h
    s = jnp.where(valid, s, NEG_INF)
    m = jnp.maximum(jnp.max(s, axis=-1, keepdims=True), sink)
    e = jnp.exp(s - m)
    es = jnp.exp(sink - m)
    inv = 1.0 / (jnp.sum(e, axis=-1, keepdims=True) + es)
    return e * inv, es * inv


def _swa_in_specs():
    return [_row_spec(WINDOW, 2304),
            pl.BlockSpec((WINDOW, 256), lambda n: (jnp.maximum(n - 1, 0), 4)),
            _full_spec((SWA_HEADS, WINDOW, 2 * WINDOW)),
            pl.BlockSpec(memory_space=pltpu.SMEM)]


def _swa_fwd(proj, bias, sinks):
    t = proj.shape[0]

    def body(cur_ref, prev_ref, bias_ref, sink_ref, o_ref):
        n = pl.program_id(0)
        kband, vband = _swa_band(cur_ref, prev_ref)
        valid = _swa_valid(n, WINDOW)
        outs = []
        for h in range(SWA_HEADS):
            j = h // SWA_GROUP
            qh = cur_ref[:, h * 64:(h + 1) * 64].astype(BF16)
            p, _ = _swa_probs(qh, kband[:, j * 64:(j + 1) * 64], bias_ref[h], sink_ref[0, h], valid)
            outs.append(_dot(p.astype(BF16), vband[:, j * 64:(j + 1) * 64]))
        o_ref[...] = (jnp.concatenate(outs, axis=1) * _silu(cur_ref[:, 1280:2304])).astype(BF16)

    return pl.pallas_call(body, grid=(t // WINDOW,), in_specs=_swa_in_specs(), out_specs=_row_spec(WINDOW, 1024),
                          out_shape=S((t, 1024), BF16), name="swa_fwd",
                          compiler_params=_cp(("parallel",)))(proj, proj, bias, sinks)


def _swa_bwd(proj, dp, bias, sinks):
    t = proj.shape[0]
    nb = t // WINDOW

    def body(cur_ref, prev_ref, bias_ref, sink_ref, dp_ref, dqz_ref, dkv_ref, dss_ref, dsk_ref):
        n = pl.program_id(0)

        @pl.when(n == 0)
        def _():
            dkv_ref[...] = jnp.zeros_like(dkv_ref)
            dss_ref[...] = jnp.zeros_like(dss_ref)
            dsk_ref[...] = jnp.zeros_like(dsk_ref)

        kband, vband = _swa_band(cur_ref, prev_ref)
        valid = _swa_valid(n)
        z = cur_ref[:, 1280:2304]
        dpv = dp_ref[...].astype(F32)
        do = dpv * _silu(z)
        q = cur_ref[:, 0:1024].astype(BF16)
        dob = do.astype(BF16)
        outs, dqs, dsinks, dk, dv = [], [], [], [], []
        for j in range(SWA_HEADS // SWA_GROUP):
            heads = slice(j * SWA_GROUP, (j + 1) * SWA_GROUP)
            qs, dos = _swa_stack(q, j), _swa_stack(dob, j)
            kh = kband[:, j * 64:(j + 1) * 64]
            vh = vband[:, j * 64:(j + 1) * 64]
            p, ps = _swa_probs(qs, kh, bias_ref[heads].reshape(SWA_ROWS, 2 * WINDOW), _swa_sink_col(sink_ref, j), valid)
            pb = p.astype(BF16)
            outs.append(_dot(pb, vh))
            dpr = _dot_nt(dos, vh)
            dlt = jnp.sum(p * dpr, axis=-1, keepdims=True)
            ds = p * (dpr - dlt)
            dsk = -ps * dlt
            dsinks += [dsk[i * WINDOW:(i + 1) * WINDOW] for i in range(SWA_GROUP)]
            dss_ref[heads] += ds.reshape(SWA_GROUP, WINDOW, 2 * WINDOW)
            dsb = (ds * (HEAD_DIM ** -0.5)).astype(BF16)
            dqs.append(_dot(dsb, kh))
            dk.append(_dot_tn(dsb, qs))
            dv.append(_dot_tn(pb, dos))
        dqz_ref[:, 0:1024] = _swa_unstack(dqs).astype(BF16)
        dqz_ref[:, 1024:2048] = (dpv * _swa_unstack(outs) * _silu_grad(z)).astype(BF16)
        dsk_ref[...] += jnp.concatenate(dsinks, axis=1)
        dband = jnp.concatenate([dk[0], dk[1], dv[0], dv[1]], axis=1)
        prow = pl.multiple_of(jnp.maximum(n - 1, 0) * WINDOW, WINDOW)
        dkv_ref[pl.ds(prow, WINDOW), :] += dband[0:WINDOW]
        crow = pl.multiple_of(n * WINDOW, WINDOW)
        dkv_ref[pl.ds(crow, WINDOW), :] += dband[WINDOW:]

    return pl.pallas_call(
        body, grid=(nb,), in_specs=_swa_in_specs() + [_row_spec(WINDOW, 1024)],
        out_specs=[_row_spec(WINDOW, 2048), _full_spec((t, 256)), _full_spec((SWA_HEADS, WINDOW, 2 * WINDOW)),
                   _full_spec((WINDOW, SWA_HEADS))],
        out_shape=[S((t, 2048), BF16), S((t, 256), F32), S((SWA_HEADS, WINDOW, 2 * WINDOW), F32),
                   S((WINDOW, SWA_HEADS), F32)],
        name="swa_bwd", compiler_params=_cp(("arbitrary",)))(proj, proj, bias, sinks, dp)


MLA_Z0 = MLA_Q_RANK + MLA_KV_RANK + 128
MLA_Z1 = MLA_Z0 + 1024
MLA_PROJ = MLA_Z1 + 128


def _rope_tables(t):
    inv = ROPE_BASE ** (-jnp.arange(0, MLA_ROPE, 2, dtype=F32) / MLA_ROPE)
    ang = jnp.arange(t, dtype=F32)[:, None] * inv[None, :]
    return jnp.tile(jnp.cos(ang), (1, MLA_HEADS)), jnp.tile(jnp.sin(ang), (1, MLA_HEADS))


def _mla_norms(proj, gq, gkv):
    t = proj.shape[0]
    tr = 256

    def body(p_ref, gq_ref, gkv_ref, q_ref, kv_ref):
        for lo, hi, g_ref, o_ref in ((0, 768, gq_ref, q_ref), (768, 1024, gkv_ref, kv_ref)):
            xv = p_ref[:, lo:hi]
            r = lax.rsqrt(jnp.mean(xv * xv, axis=-1, keepdims=True) + EPS)
            o_ref[...] = (xv * r * g_ref[...]).astype(BF16)

    return pl.pallas_call(
        body, grid=(t // tr,), in_specs=[_row_spec(tr, 1024), _full_spec((1, 768)), _full_spec((1, 256))],
        out_specs=[_row_spec(tr, 768), _row_spec(tr, 256)], out_shape=[S((t, 768), BF16), S((t, 256), BF16)],
        name="mla_norms", compiler_params=_cp(("parallel",)))(proj, gq, gkv)


def _mla_norms_bwd(proj, dcqn, dckvn, dkr, dz, gq, gkv):
    t = proj.shape[0]
    tr = 256

    def body(p_ref, dq_ref, dkv_ref, dkr_ref, dz_ref, gq_ref, gkv_ref, dproj_ref, dgq_ref, dgkv_ref):
        @pl.when(pl.program_id(0) == 0)
        def _():
            dgq_ref[...] = jnp.zeros_like(dgq_ref)
            dgkv_ref[...] = jnp.zeros_like(dgkv_ref)

        dx, dg = _rms_bwd_rows(dq_ref[...], p_ref[:, 0:768], gq_ref[...])
        dproj_ref[:, 0:768] = dx.astype(BF16)
        dgq_ref[...] += dg
        dx, dg = _rms_bwd_rows(dkv_ref[...], p_ref[:, 768:1024], gkv_ref[...])
        dproj_ref[:, 768:1024] = dx.astype(BF16)
        dgkv_ref[...] += dg
        dproj_ref[:, 1024:MLA_Z0] = dkr_ref[...].astype(BF16)
        dproj_ref[:, MLA_Z0:MLA_Z1] = dz_ref[...]
        dproj_ref[:, MLA_Z1:MLA_PROJ] = jnp.zeros((tr, MLA_PROJ - MLA_Z1), BF16)

    return pl.pallas_call(
        body, grid=(t // tr,),
        in_specs=[_row_spec(tr, 1024), _row_spec(tr, 768), _row_spec(tr, 256), _row_spec(tr, 128), _row_spec(tr, 1024),
                  _full_spec((1, 768)), _full_spec((1, 256))],
        out_specs=[_row_spec(tr, MLA_PROJ), _full_spec((1, 768)), _full_spec((1, 256))],
        out_shape=[S((t, MLA_PROJ), BF16), S((1, 768), F32), S((1, 256), F32)],
        name="mla_norms_bwd", compiler_params=_cp(("arbitrary",)))(proj, dcqn, dckvn, dkr, dz, gq, gkv)


def _mla_pack(qf, kvf, proj, cos, sin):
    t = qf.shape[0]
    tr = 256

    def body(q_ref, kn_ref, kr_ref, cos_ref, sin_ref, qh_ref, kh_ref):
        cs, sn = cos_ref[...], sin_ref[...]
        x1, x2 = q_ref[:, 1024:1280], q_ref[:, 1280:1536]
        r1 = x1 * cs - x2 * sn
        r2 = x2 * cs + x1 * sn
        c16, s16 = cs[:, 0:16], sn[:, 0:16]
        k1, k2 = kr_ref[:, 0:16], kr_ref[:, 16:32]
        kr1 = k1 * c16 - k2 * s16
        kr2 = k2 * c16 + k1 * s16
        zpad = jnp.zeros((tr, 32), F32)
        for h in range(MLA_HEADS):
            qh_ref[h] = jnp.concatenate([q_ref[:, h * 64:(h + 1) * 64], r1[:, h * 16:(h + 1) * 16],
                                         r2[:, h * 16:(h + 1) * 16], zpad], axis=1).astype(BF16)
            kh_ref[h] = jnp.concatenate([kn_ref[:, h * 64:(h + 1) * 64], kr1, kr2, zpad], axis=1).astype(BF16)

    hspec = pl.BlockSpec((MLA_HEADS, tr, 128), lambda i: (0, i, 0))
    return pl.pallas_call(
        body, grid=(t // tr,),
        in_specs=[_row_spec(tr, 1536), _row_spec(tr, 1024), pl.BlockSpec((tr, 128), lambda i: (i, 8)),
                  _row_spec(tr, 256), _row_spec(tr, 256)],
        out_specs=[hspec, hspec], out_shape=[S((MLA_HEADS, t, 128), BF16)] * 2,
        name="mla_pack", compiler_params=_cp(("parallel",)))(qf, kvf, proj, cos, sin)


def _mla_unpack(dqh, dkh, dv, cos, sin):
    t = dqh.shape[1]
    tr = 256

    def body(dqh_ref, dkh_ref, dv_ref, cos_ref, sin_ref, dq_ref, dkv_ref, dkr_ref):
        cs, sn = cos_ref[...], sin_ref[...]
        dqn = jnp.concatenate([dqh_ref[h, :, 0:64] for h in range(MLA_HEADS)], axis=1)
        dr1 = jnp.concatenate([dqh_ref[h, :, 64:80] for h in range(MLA_HEADS)], axis=1)
        dr2 = jnp.concatenate([dqh_ref[h, :, 80:96] for h in range(MLA_HEADS)], axis=1)
        dq_ref[:, 0:1024] = dqn.astype(BF16)
        dq_ref[:, 1024:1280] = (dr1 * cs + dr2 * sn).astype(BF16)
        dq_ref[:, 1280:1536] = (dr2 * cs - dr1 * sn).astype(BF16)
        dkv_ref[:, 0:1024] = jnp.concatenate([dkh_ref[h, :, 0:64] for h in range(MLA_HEADS)], axis=1).astype(BF16)
        dkv_ref[:, 1024:2048] = dv_ref[...].astype(BF16)
        d1 = dkh_ref[0, :, 64:80]
        d2 = dkh_ref[0, :, 80:96]
        for h in range(1, MLA_HEADS):
            d1 = d1 + dkh_ref[h, :, 64:80]
            d2 = d2 + dkh_ref[h, :, 80:96]
        c16, s16 = cs[:, 0:16], sn[:, 0:16]
        dkr_ref[...] = jnp.concatenate([d1 * c16 + d2 * s16, d2 * c16 - d1 * s16, jnp.zeros((tr, 96), F32)], axis=1)

    hspec = pl.BlockSpec((MLA_HEADS, tr, 128), lambda i: (0, i, 0))
    return pl.pallas_call(
        body, grid=(t // tr,), in_specs=[hspec, hspec, _row_spec(tr, 1024), _row_spec(tr, 256), _row_spec(tr, 256)],
        out_specs=[_row_spec(tr, 1536), _row_spec(tr, 2048), _row_spec(tr, 128)],
        out_shape=[S((t, 1536), BF16), S((t, 2048), BF16), S((t, 128), F32)],
        name="mla_unpack", compiler_params=_cp(("parallel",)))(dqh, dkh, dv, cos, sin)


def _mla_attn_fwd(qh, kh, kvf, proj):
    t = qh.shape[1]
    b, bk = MLA_BQ, MLA_BK

    def body(q_ref, k_ref, v_ref, z_ref, o_ref, p_ref, lse_ref):
        r0 = pl.program_id(1) * b
        nfull = r0 // bk
        diff = (lax.broadcasted_iota(jnp.int32, (b, bk), 1) - lax.broadcasted_iota(jnp.int32, (b, bk), 0))
        qs = [q_ref[0], q_ref[1]]

        def block(c0, carry, masked):
            new = []
            for a in range(2):
                m, l, acc = carry[a]
                k = k_ref[a, pl.ds(c0, bk), :]
                v = v_ref[pl.ds(c0, bk), a * 64:(a + 1) * 64].astype(BF16)
                s = _dot_nt(qs[a], k) * MLA_SCALE
                if masked:
                    s = jnp.where(diff <= r0 - c0, s, NEG_INF)
                mn = jnp.maximum(m, jnp.max(s, axis=-1, keepdims=True))
                alpha = jnp.exp(m - mn)
                p = jnp.exp(s - mn)
                new.append((mn, alpha * l + jnp.sum(p, axis=-1, keepdims=True), alpha * acc + _dot(p.astype(BF16), v)))
            return tuple(new)

        init = (jnp.full((b, 1), NEG_INF, F32), jnp.zeros((b, 1), F32), jnp.zeros((b, 64), F32))
        carry = lax.fori_loop(0, nfull, lambda c, cr: block(pl.multiple_of(c * bk, bk), cr, False), (init, init))
        for i in range(max(1, b // bk)):
            carry = block(pl.multiple_of((nfull + i) * bk, bk), carry, True)
        outs = [acc / l for _, l, acc in carry]
        lses = [jnp.broadcast_to(m + jnp.log(l), (b, 64)) for m, l, _ in carry]
        o = jnp.concatenate(outs, axis=1)
        o_ref[...] = o
        p_ref[...] = (o * _silu(z_ref[...])).astype(BF16)
        lse_ref[0] = jnp.concatenate(lses, axis=1)

    return pl.pallas_call(
        body, grid=(MLA_HEADS // 2, t // b),
        in_specs=[pl.BlockSpec((2, b, 128), lambda i, r: (i, r, 0)), pl.BlockSpec((2, t, 128), lambda i, r: (i, 0, 0)),
                  pl.BlockSpec((t, 128), lambda i, r: (0, 8 + i)), pl.BlockSpec((b, 128), lambda i, r: (r, 9 + i))],
        out_specs=[pl.BlockSpec((b, 128), lambda i, r: (r, i)), pl.BlockSpec((b, 128), lambda i, r: (r, i)),
                   pl.BlockSpec((1, b, 128), lambda i, r: (i, r, 0))],
        out_shape=[S((t, 1024), F32), S((t, 1024), BF16), S((MLA_HEADS // 2, t, 128), F32)],
        name="mla_attn_fwd", compiler_params=_cp(("parallel", "parallel")))(qh, kh, kvf, proj)


def _mla_attn_bwd(qh, kh, kvf, proj, o, dpout, lse):
    t = qh.shape[1]
    b, bk = MLA_BQ, MLA_BK
    nb = t // b

    def body(q_ref, k_ref, v_ref, z_ref, o_ref, dp_ref, lse_ref, dq_ref, dk_ref, dv_ref, dz_ref):
        dk_ref[...] = jnp.zeros_like(dk_ref)
        dv_ref[...] = jnp.zeros_like(dv_ref)
        diff = (lax.broadcasted_iota(jnp.int32, (b, bk), 1) - lax.broadcasted_iota(jnp.int32, (b, bk), 0))
        lanes = [slice(0, 64), slice(64, 128)]

        def rbody(r, carry):
            r0 = pl.multiple_of(r * b, b)
            rows = pl.ds(r0, b)
            qs, dobs, deltas, lses = [], [], [], []
            for a in range(2):
                zz = z_ref[rows, lanes[a]]
                oo = o_ref[rows, lanes[a]]
                dpo = dp_ref[rows, lanes[a]].astype(F32)
                do = dpo * _silu(zz)
                dz_ref[rows, lanes[a]] = (dpo * oo * _silu_grad(zz)).astype(BF16)
                qs.append(q_ref[a, rows, :])
                dobs.append(do.astype(BF16))
                deltas.append(jnp.sum(do * oo, axis=-1, keepdims=True))
                lses.append(lse_ref[0, rows, a * 64:a * 64 + 1])

            def block(c0, dqs, masked):
                cols = pl.ds(c0, bk)
                out = []
                for a in range(2):
                    k = k_ref[a, cols, :]
                    v = v_ref[cols, lanes[a]].astype(BF16)
                    s = _dot_nt(qs[a], k) * MLA_SCALE
                    if masked:
                        s = jnp.where(diff <= r0 - c0, s, NEG_INF)
                    p = jnp.exp(s - lses[a])
                    ds = p * (_dot_nt(dobs[a], v) - deltas[a]) * MLA_SCALE
                    dsb = ds.astype(BF16)
                    dk_ref[a, cols, :] += _dot_tn(dsb, qs[a])
                    dv_ref[cols, lanes[a]] += _dot_tn(p.astype(BF16), dobs[a])
                    out.append(dqs[a] + _dot(dsb, k))
                return tuple(out)

            zero = jnp.zeros((b, 128), F32)
            nfull = r0 // bk
            dqs = lax.fori_loop(0, nfull, lambda c, d: block(pl.multiple_of(c * bk, bk), d, False), (zero, zero))
            for i in range(max(1, b // bk)):
                dqs = block(pl.multiple_of((nfull + i) * bk, bk), dqs, True)
            dq_ref[0, rows, :] = dqs[0]
            dq_ref[1, rows, :] = dqs[1]
            return carry

        lax.fori_loop(0, nb, rbody, 0)

    pair3 = pl.BlockSpec((2, t, 128), lambda i: (i, 0, 0))
    return pl.pallas_call(
        body, grid=(MLA_HEADS // 2,),
        in_specs=[pair3, pair3, pl.BlockSpec((t, 128), lambda i: (0, 8 + i)), pl.BlockSpec((t, 128), lambda i: (0, 9 + i)),
                  pl.BlockSpec((t, 128), lambda i: (0, i)), pl.BlockSpec((t, 128), lambda i: (0, i)),
                  pl.BlockSpec((1, t, 128), lambda i: (i, 0, 0))],
        out_specs=[pair3, pair3, pl.BlockSpec((t, 128), lambda i: (0, i)), pl.BlockSpec((t, 128), lambda i: (0, i))],
        out_shape=[S((MLA_HEADS, t, 128), F32), S((MLA_HEADS, t, 128), F32), S((t, 1024), F32), S((t, 1024), BF16)],
        name="mla_attn_bwd", compiler_params=_cp(("parallel",), VMEM_BIG))(qh, kh, kvf, proj, o, dpout, lse)


def _dot_nt_hi(a, b):
    return lax.dot_general(a, b, (((1,), (1,)), ((), ())), preferred_element_type=F32,
                           precision=lax.Precision.HIGHEST)


SSM_PACK = 2


def _ssm_gen(lr, li, ldt, btr, bti, cr, ci):
    n = SSM_BLOCK
    lanes = SSM_PACK * SSM_STATE
    dt = jnp.exp(ldt)
    x = lr * dt
    w = li * dt
    mag = jnp.exp(x)
    ab_re = mag * jnp.cos(w)
    ab_im = mag * jnp.sin(w)
    den = lr * lr + li * li
    nr = ab_re - 1.0
    f_re = (nr * lr + ab_im * li) / den
    f_im = (ab_im * lr - nr * li) / den
    bb_re = f_re * btr - f_im * bti
    bb_im = f_re * bti + f_im * btr
    inv_mag = jnp.exp(-x)
    inv_re, inv_im = inv_mag * jnp.cos(w), -inv_mag * jnp.sin(w)
    one, zero = jnp.ones_like(x), jnp.zeros_like(x)
    pos, neg = [(one, zero)], [(one, zero)]
    for _ in range(SSM_CHUNK):
        pr, pi = pos[-1]
        pos.append((pr * ab_re - pi * ab_im, pr * ab_im + pi * ab_re))
        pr, pi = neg[-1]
        neg.append((pr * inv_re - pi * inv_im, pr * inv_im + pi * inv_re))

    def stack(table, idx):
        return (jnp.concatenate([jnp.broadcast_to(table[idx(t)][0], (SSM_GROUP, lanes)) for t in range(SSM_CHUNK)], axis=0),
                jnp.concatenate([jnp.broadcast_to(table[idx(t)][1], (SSM_GROUP, lanes)) for t in range(SSM_CHUNK)], axis=0))

    def rep(v):
        return jnp.concatenate([v] * SSM_CHUNK, axis=0)

    ct_re, ct_im, bt_re, bt_im = rep(cr), rep(ci), rep(bb_re), rep(bb_im)
    p_re, p_im = stack(pos, lambda t: t)
    l_re = ct_re * p_re - ct_im * p_im
    l_im = ct_re * p_im + ct_im * p_re
    n_re, n_im = stack(neg, lambda t: t)
    r_re = bt_re * n_re - bt_im * n_im
    r_im = bt_re * n_im + bt_im * n_re
    rk = lax.broadcasted_iota(jnp.int32, (n, n), 0) // SSM_GROUP
    ct = lax.broadcasted_iota(jnp.int32, (n, n), 1) // SSM_GROUP
    mts = []
    for g in range(SSM_PACK):
        sl = slice(g * SSM_STATE, (g + 1) * SSM_STATE)
        mts.append(jnp.where(ct >= rk, _dot_nt_hi(r_re[:, sl], l_re[:, sl]) - _dot_nt_hi(r_im[:, sl], l_im[:, sl]), 0.0))
    e_re, e_im = stack(pos, lambda t: SSM_CHUNK - 1 - t)
    pin_re = bt_re * e_re - bt_im * e_im
    pin_im = bt_re * e_im + bt_im * e_re
    q_re, q_im = stack(pos, lambda t: t + 1)
    qt_re = ct_re * q_re - ct_im * q_im
    qt_im = -(ct_re * q_im + ct_im * q_re)
    ad_re, ad_im = pos[SSM_CHUNK]
    return tuple(mts), pin_re, pin_im, qt_re, qt_im, ad_re, ad_im


_SSM_GEN_OUT = [(SSM_BLOCK, SSM_BLOCK)] + [(SSM_BLOCK, SSM_STATE)] * 4 + [(1, SSM_STATE)] * 2
_SSM_GEN_IN = [(1, SSM_STATE)] * 3 + [(SSM_GROUP, SSM_STATE)] * 4


def _gspec(shape2):
    return pl.BlockSpec((SSM_PACK,) + shape2, lambda g: (g, 0, 0))


def _pack_lanes(ref):
    return jnp.concatenate([ref[g] for g in range(SSM_PACK)], axis=1)


def _unpack_lanes(ref, val):
    for g in range(SSM_PACK):
        ref[g] = val[:, g * SSM_STATE:(g + 1) * SSM_STATE]


def _ssm_operators(params):
    def body(*refs):
        ins, outs = refs[:7], refs[7:]
        vals = _ssm_gen(*[_pack_lanes(r) for r in ins])
        for g in range(SSM_PACK):
            outs[0][g] = vals[0][g]
        for o_ref, val in zip(outs[1:], vals[1:]):
            _unpack_lanes(o_ref, val)

    return pl.pallas_call(
        body, grid=(SSM_GROUPS // SSM_PACK,), in_specs=[_gspec(s) for s in _SSM_GEN_IN],
        out_specs=[_gspec(s) for s in _SSM_GEN_OUT],
        out_shape=[S((SSM_GROUPS,) + s, F32) for s in _SSM_GEN_OUT], name="ssm_operators",
        compiler_params=_cp(("parallel",)))(*params)


def _ssm_operators_bwd(params, cots, after=()):
    def body(*refs):
        ins, cts, outs = refs[:7], refs[7:14], refs[14 + len(after):]
        _, vjp = jax.vjp(_ssm_gen, *[_pack_lanes(r) for r in ins])
        grads = list(vjp((tuple(cts[0][g] for g in range(SSM_PACK)),) + tuple(_pack_lanes(r) for r in cts[1:])))
        for o_ref, val in zip(outs, grads):
            _unpack_lanes(o_ref, val)
        for g in range(SSM_PACK):
            dldt = grads[2][:, g * SSM_STATE:(g + 1) * SSM_STATE]
            outs[2][g] = jnp.broadcast_to(jnp.sum(dldt, axis=-1, keepdims=True), (1, SSM_STATE))

    return pl.pallas_call(
        body, grid=(SSM_GROUPS // SSM_PACK,),
        in_specs=[_gspec(s) for s in _SSM_GEN_IN] + [_gspec(s) for s in _SSM_GEN_OUT]
        + [pl.BlockSpec(memory_space=pl.ANY)] * len(after),
        out_specs=[_gspec(s) for s in _SSM_GEN_IN], out_shape=[S((SSM_GROUPS,) + s, F32) for s in _SSM_GEN_IN],
        name="ssm_operators_bwd", compiler_params=_cp(("parallel",)))(*params, *cots, *after)


def _shift_rows(x, sh, row, up):
    n = x.shape[0]
    if up:
        return jnp.where(row < n - sh, pltpu.roll(x, n - sh, axis=0), 0.0)
    return jnp.where(row >= sh, pltpu.roll(x, sh, axis=0), 0.0)


def _carry_scan(sr, si, ar, ai, row, up):
    n = sr.shape[0]
    sh = 1
    while sh < n:
        tr, ti = _shift_rows(sr, sh, row, up), _shift_rows(si, sh, row, up)
        sr, si = sr + ar * tr - ai * ti, si + ar * ti + ai * tr
        ar, ai = ar * ar - ai * ai, 2.0 * ar * ai
        sh *= 2
    return sr, si


def _ssm_states(u, p_re, p_im, ar, ai, row):
    lre = _dot(u, p_re.astype(BF16))
    lim = _dot(u, p_im.astype(BF16))
    sr, si = _carry_scan(lre, lim, ar, ai, row, False)
    return _shift_rows(sr, 1, row, False), _shift_rows(si, 1, row, False)


SSM_GPB = 128 // SSM_GROUP


def _step_rows(ref, nc):
    return [ref[pl.ds(k, nc, stride=SSM_CHUNK), :].astype(BF16) for k in range(SSM_CHUNK)]


def _group_of(rows, j):
    return jnp.concatenate([r[:, j * SSM_GROUP:(j + 1) * SSM_GROUP] for r in rows], axis=1)


def _store_groups(o_ref, per_group, nc):
    for k in range(SSM_CHUNK):
        o_ref[pl.ds(k, nc, stride=SSM_CHUNK), :] = jnp.concatenate(
            [y[:, k * SSM_GROUP:(k + 1) * SSM_GROUP] for y in per_group], axis=1)


def _bspec(shape2):
    return pl.BlockSpec((SSM_GPB,) + shape2, lambda j: (j, 0, 0))


def _ssm_core_fwd(proj, ops):
    t = proj.shape[0]
    nc = t // SSM_CHUNK

    def body(u_ref, mt_ref, pr_ref, pi_ref, qr_ref, qi_ref, ar_ref, ai_ref, y_ref):
        rows = _step_rows(u_ref, nc)
        row = lax.broadcasted_iota(jnp.int32, (nc, SSM_STATE), 0)
        ys = []
        for j in range(SSM_GPB):
            u = _group_of(rows, j)
            s_re, s_im = _ssm_states(u, pr_ref[j], pi_ref[j], ar_ref[j], ai_ref[j], row)
            ys.append(_dot(u, mt_ref[j].astype(BF16)) + _dot_nt(s_re.astype(BF16), qr_ref[j].astype(BF16))
                      + _dot_nt(s_im.astype(BF16), qi_ref[j].astype(BF16)))
        _store_groups(y_ref, ys, nc)

    blk = pl.BlockSpec((t, 128), lambda j: (0, j))
    return pl.pallas_call(
        body, grid=(SSM_GROUPS // SSM_GPB,), in_specs=[blk] + [_bspec(s) for s in _SSM_GEN_OUT], out_specs=blk,
        out_shape=S((t, 1024), F32), name="ssm_core_fwd", compiler_params=_cp(("parallel",)))(proj, *ops)


def _ssm_core_bwd(proj, dy, ops):
    t = proj.shape[0]
    nc = t // SSM_CHUNK

    def body(u_ref, dy_ref, mt_ref, pr_ref, pi_ref, qr_ref, qi_ref, ar_ref, ai_ref,
             du_ref, dmt_ref, dpr_ref, dpi_ref, dqr_ref, dqi_ref, dar_ref, dai_ref):
        u_rows = _step_rows(u_ref, nc)
        dy_rows = _step_rows(dy_ref, nc)
        row = lax.broadcasted_iota(jnp.int32, (nc, SSM_STATE), 0)
        dus = []
        for j in range(SSM_GPB):
            u = _group_of(u_rows, j)
            dyj = _group_of(dy_rows, j)
            ar, ai = ar_ref[j], ai_ref[j]
            s_re, s_im = _ssm_states(u, pr_ref[j], pi_ref[j], ar, ai, row)
            g_re = _shift_rows(_dot(dyj, qr_ref[j].astype(BF16)), 1, row, True)
            g_im = _shift_rows(_dot(dyj, qi_ref[j].astype(BF16)), 1, row, True)
            l_re, l_im = _carry_scan(g_re, g_im, ar, -ai, row, True)
            lrb, lib = l_re.astype(BF16), l_im.astype(BF16)
            dus.append(_dot_nt(dyj, mt_ref[j].astype(BF16)) + _dot_nt(lrb, pr_ref[j].astype(BF16))
                       + _dot_nt(lib, pi_ref[j].astype(BF16)))
            dmt_ref[j] = _dot_tn(u, dyj)
            dpr_ref[j] = _dot_tn(u, lrb)
            dpi_ref[j] = _dot_tn(u, lib)
            dqr_ref[j] = _dot_tn(dyj, s_re.astype(BF16))
            dqi_ref[j] = _dot_tn(dyj, s_im.astype(BF16))
            dar_ref[j] = jnp.sum(l_re * s_re + l_im * s_im, axis=0, keepdims=True)
            dai_ref[j] = jnp.sum(l_im * s_re - l_re * s_im, axis=0, keepdims=True)
        _store_groups(du_ref, dus, nc)

    blk = pl.BlockSpec((t, 128), lambda j: (0, j))
    return pl.pallas_call(
        body, grid=(SSM_GROUPS // SSM_GPB,), in_specs=[blk, blk] + [_bspec(s) for s in _SSM_GEN_OUT],
        out_specs=[blk] + [_bspec(s) for s in _SSM_GEN_OUT],
        out_shape=[S((t, 1024), F32)] + [S((SSM_GROUPS,) + s, F32) for s in _SSM_GEN_OUT],
        name="ssm_core_bwd", compiler_params=_cp(("parallel",)))(proj, dy, *ops)


def _s5_act(y_core, proj, d_skip):
    t = y_core.shape[0]
    tr = 256

    def body(y_ref, u_ref, d_ref, o_ref):
        o_ref[...] = _gelu(y_ref[...] + d_ref[...] * u_ref[...])

    return pl.pallas_call(body, grid=(t // tr,), in_specs=[_row_spec(tr, 1024), _row_spec(tr, 1024), _full_spec((1, 1024))],
                          out_specs=_row_spec(tr, 1024), out_shape=S((t, 1024), F32), name="s5_act",
                          compiler_params=_cp(("parallel",)))(y_core, proj, d_skip)


def _s5_gate(yg, tg, b_glu, proj):
    t = yg.shape[0]
    tr = 256

    def body(y_ref, t_ref, b_ref, z_ref, o_ref):
        o_ref[...] = (y_ref[...] * jax.nn.sigmoid(t_ref[...] + b_ref[...]) * _silu(z_ref[...])).astype(BF16)

    return pl.pallas_call(
        body, grid=(t // tr,),
        in_specs=[_row_spec(tr, 1024), _row_spec(tr, 1024), _full_spec((1, 1024)), pl.BlockSpec((tr, 1024), lambda i: (i, 1))],
        out_specs=_row_spec(tr, 1024), out_shape=S((t, 1024), BF16), name="s5_gate",
        compiler_params=_cp(("parallel",)))(yg, tg, b_glu, proj)


def _s5_gate_bwd(dp, yg, tg, b_glu, proj):
    t = yg.shape[0]
    tr = 256

    def body(dp_ref, y_ref, t_ref, b_ref, z_ref, dt_ref, dy_ref, dz_ref, db_ref):
        @pl.when(pl.program_id(0) == 0)
        def _():
            db_ref[...] = jnp.zeros_like(db_ref)

        z = z_ref[...]
        yv = y_ref[...]
        sg = jax.nn.sigmoid(t_ref[...] + b_ref[...])
        dpv = dp_ref[...].astype(F32)
        do = dpv * _silu(z)
        dz_ref[...] = (dpv * yv * sg * _silu_grad(z)).astype(BF16)
        dy_ref[...] = do * sg
        dtv = do * yv * sg * (1.0 - sg)
        dt_ref[...] = dtv.astype(BF16)
        db_ref[...] += jnp.sum(dtv, axis=0, keepdims=True)

    return pl.pallas_call(
        body, grid=(t // tr,),
        in_specs=[_row_spec(tr, 1024), _row_spec(tr, 1024), _row_spec(tr, 1024), _full_spec((1, 1024)),
                  pl.BlockSpec((tr, 1024), lambda i: (i, 1))],
        out_specs=[_row_spec(tr, 1024), _row_spec(tr, 1024), _row_spec(tr, 1024), _full_spec((1, 1024))],
        out_shape=[S((t, 1024), BF16), S((t, 1024), F32), S((t, 1024), BF16), S((1, 1024), F32)],
        name="s5_gate_bwd", compiler_params=_cp(("arbitrary",)))(dp, yg, tg, b_glu, proj)


def _s5_act_bwd(dyg1, dyg2, y_core, proj, d_skip):
    t = y_core.shape[0]
    tr = 256

    def body(a_ref, b_ref, y_ref, u_ref, d_ref, dy_ref, du_ref, dd_ref):
        @pl.when(pl.program_id(0) == 0)
        def _():
            dd_ref[...] = jnp.zeros_like(dd_ref)

        u = u_ref[...]
        dv = (a_ref[...] + b_ref[...]) * _gelu_grad(y_ref[...] + d_ref[...] * u)
        dy_ref[...] = dv
        du_ref[...] = dv * d_ref[...]
        dd_ref[...] += jnp.sum(dv * u, axis=0, keepdims=True)

    return pl.pallas_call(
        body, grid=(t // tr,),
        in_specs=[_row_spec(tr, 1024)] * 4 + [_full_spec((1, 1024))],
        out_specs=[_row_spec(tr, 1024), _row_spec(tr, 1024), _full_spec((1, 1024))],
        out_shape=[S((t, 1024), F32), S((t, 1024), F32), S((1, 1024), F32)],
        name="s5_act_bwd", compiler_params=_cp(("arbitrary",)))(dyg1, dyg2, y_core, proj, d_skip)


def _s5_dproj(du_skip, du_core, dz):
    t = du_skip.shape[0]
    tr = 256

    def body(a_ref, b_ref, z_ref, o_ref):
        o_ref[:, 0:1024] = (a_ref[...] + b_ref[...]).astype(BF16)
        o_ref[:, 1024:2048] = z_ref[...]

    return pl.pallas_call(body, grid=(t // tr,), in_specs=[_row_spec(tr, 1024)] * 3, out_specs=_row_spec(tr, 2048),
                          out_shape=S((t, 2048), BF16), name="s5_dproj",
                          compiler_params=_cp(("parallel",)))(du_skip, du_core, dz)


def _s5_params(lam_re, lam_im, log_dt, b_re, b_im, c_re, c_im):
    g = SSM_GROUPS
    return (lam_re.reshape(g, 1, SSM_STATE), lam_im.reshape(g, 1, SSM_STATE),
            jnp.broadcast_to(log_dt.reshape(g, 1, 1), (g, 1, SSM_STATE)),
            b_re.transpose(0, 2, 1), b_im.transpose(0, 2, 1), c_re, c_im)


def _s5_fwd(proj, params, d_skip, w_glu_after, b_glu):
    ops = _ssm_operators(params)
    y_core = _ssm_core_fwd(proj, ops)
    yg = _s5_act(y_core, proj, d_skip)
    tg = _mm(yg, w_glu_after(yg), name="s5_glu")
    return _s5_gate(yg, tg, b_glu, proj), (ops, y_core, yg, tg)


def _s5_bwd(proj, res, dp, params, d_skip, w_glu, b_glu):
    ops, y_core, yg, tg = res
    dtg, dyg1, dz, db_glu = _s5_gate_bwd(dp, yg, tg, b_glu, proj)
    dyg2 = _mm(dtg, w_glu, tb=True, name="s5_glu_dx")
    dw_glu = _mm(yg, dtg, ta=True, out_dtype=BF16, name="s5_glu_dw")
    dy_core, du_skip, dd = _s5_act_bwd(dyg1, dyg2, y_core, proj, d_skip)
    outs = _ssm_core_bwd(proj, dy_core, ops)
    dproj = _s5_dproj(du_skip, outs[0], dz)
    return dproj, outs[1:], dd, dw_glu, db_glu


def _s5_param_grads(params, cots, after=()):
    glr, gli, gdt, gbtr, gbti, gcr, gci = _ssm_operators_bwd(params, cots, after)
    g = SSM_GROUPS
    return (glr.reshape(g, SSM_STATE), gli.reshape(g, SSM_STATE), gdt[:, 0, 0].reshape(1, g), gbtr, gbti, gcr, gci)


def _perm_uq(w):
    w3 = w.reshape(w.shape[0], MLA_HEADS, MLA_NOPE + MLA_ROPE)
    return jnp.concatenate([w3[:, :, :64].reshape(-1, 1024), w3[:, :, 64:80].reshape(-1, 256),
                            w3[:, :, 80:96].reshape(-1, 256)], axis=1)


def _unperm_uq(w):
    r = w.shape[0]
    return jnp.concatenate([w[:, :1024].reshape(r, 16, 64), w[:, 1024:1280].reshape(r, 16, 16),
                            w[:, 1280:1536].reshape(r, 16, 16)], axis=2).reshape(r, 1536)


def _perm_ukv(w):
    w3 = w.reshape(w.shape[0], MLA_HEADS, 128)
    return jnp.concatenate([w3[:, :, :64].reshape(-1, 1024), w3[:, :, 64:].reshape(-1, 1024)], axis=1)


def _unperm_ukv(w):
    r = w.shape[0]
    return jnp.concatenate([w[:, :1024].reshape(r, 16, 64), w[:, 1024:].reshape(r, 16, 64)], axis=2).reshape(r, 2048)


def _mla_fwd(proj, gq, gkv, wuq, wukv, cos, sin):
    cqn, ckvn = _mla_norms(proj, gq, gkv)
    qf = _mm(cqn, wuq, name="mla_uq")
    kvf = _mm(ckvn, wukv, name="mla_ukv")
    qh, kh = _mla_pack(qf, kvf, proj, cos, sin)
    o, pout, lse = _mla_attn_fwd(qh, kh, kvf, proj)
    return pout, (cqn, ckvn, kvf, qh, kh, o, lse)


def _mla_bwd(proj, res, dpout, gq, gkv, wuq, wukv, cos, sin):
    cqn, ckvn, kvf, qh, kh, o, lse = res
    dqh, dkh, dv, dz = _mla_attn_bwd(qh, kh, kvf, proj, o, dpout, lse)
    dqf, dkvf, dkr = _mla_unpack(dqh, dkh, dv, cos, sin)
    dcqn = _mm(dqf, wuq, tb=True, name="mla_uq_dx")
    dwuq = _mm(cqn, dqf, ta=True, out_dtype=BF16, name="mla_uq_dw")
    dckvn = _mm(dkvf, wukv, tb=True, name="mla_ukv_dx")
    dwukv = _mm(ckvn, dkvf, ta=True, out_dtype=BF16, name="mla_ukv_dw")
    dproj, dgq, dgkv = _mla_norms_bwd(proj, dcqn, dckvn, dkr, dz, gq, gkv)
    return dproj, dgq, dgkv, dwuq, dwukv


_ANY = pl.BlockSpec(memory_space=pl.ANY)


def _chip_peers():
    x, y, c = lax.axis_index("x"), lax.axis_index("y"), lax.axis_index("c")
    return 2 * x + y, c, [(1 - x, y), (x, 1 - y), (1 - x, 1 - y)]


def _sibling_swap(ps, name):
    n = len(ps)

    def body(*refs):
        p_refs, q_refs, send_sems, recv_sems = refs[:n], refs[n:2 * n], refs[2 * n], refs[2 * n + 1]
        x, y, c = lax.axis_index("x"), lax.axis_index("y"), lax.axis_index("c")
        cps = [pltpu.make_async_remote_copy(src_ref=p_refs[i], dst_ref=q_refs[i], send_sem=send_sems.at[i],
                                            recv_sem=recv_sems.at[i], device_id=(x, y, 1 - c), device_id_type=MESH)
               for i in range(n)]
        for cp in cps:
            cp.start()
        for cp in cps:
            cp.wait()

    return pl.pallas_call(body, in_specs=[_ANY] * n, out_specs=[_ANY] * n, out_shape=[S(p.shape, p.dtype) for p in ps],
                          scratch_shapes=[pltpu.SemaphoreType.DMA((n,)), pltpu.SemaphoreType.DMA((n,))], name=name)(*ps)


def _quarter_spec(tr, r, c, dim):
    if dim is None:
        return pl.BlockSpec((1, tr, c), lambda i, me: (me[0], i, 0))
    if dim == 0:
        return pl.BlockSpec((tr, c), lambda i, me: (me[0] * (r // tr) + i, 0))
    return pl.BlockSpec((tr, c), lambda i, me: (i, me[0]))


def _whole_shape(r, c, dim):
    return (N_CHIPS, r, c) if dim is None else ((N_CHIPS * r, c) if dim == 0 else (r, N_CHIPS * c))


def _row_tile(r):
    return 256 if r % 256 == 0 else (128 if r % 128 == 0 else r)


def _cast_place(me, x, dim, dtype, name):
    r, c = x.shape
    tr = _row_tile(r)

    def body(me_ref, x_ref, o_ref):
        o_ref[...] = x_ref[...].astype(dtype).reshape(o_ref.shape)

    return pl.pallas_call(
        body, out_shape=S(_whole_shape(r, c, dim), dtype), name=name,
        grid_spec=pltpu.PrefetchScalarGridSpec(num_scalar_prefetch=1, grid=(r // tr,),
                                               in_specs=[pl.BlockSpec((tr, c), lambda i, me: (i, 0))],
                                               out_specs=_quarter_spec(tr, r, c, dim)),
        compiler_params=_cp(("parallel",)))(me, x)


def _sum_parts(me, g, land, dim, name):
    _, r, c = land.shape
    tr = _row_tile(r)

    def body(me_ref, g_ref, l_ref, o_ref):
        own = g_ref[...].astype(F32).reshape(tr, c)
        o_ref[...] = ((own + l_ref[0].astype(F32)) + l_ref[1].astype(F32)) + l_ref[2].astype(F32)

    return pl.pallas_call(
        body, out_shape=S((r, c), F32), name=name,
        grid_spec=pltpu.PrefetchScalarGridSpec(
            num_scalar_prefetch=1, grid=(r // tr,),
            in_specs=[_quarter_spec(tr, r, c, dim), pl.BlockSpec((3, tr, c), lambda i, me: (0, i, 0))],
            out_specs=pl.BlockSpec((tr, c), lambda i, me: (i, 0))),
        compiler_params=_cp(("parallel",)))(me, g, land)


_HBM = pl.BlockSpec(memory_space=pltpu.HBM)
_SEM = pl.BlockSpec(memory_space=pltpu.SEMAPHORE)
_EFFECT = pltpu.SideEffectType.DATAFLOW_SIDE_EFFECTING


def _quarter(ref, s, dim):
    n = ref.shape[dim] // N_CHIPS
    start = pl.multiple_of(s * n, n)
    return ref.at[pl.ds(start, n), :] if dim == 0 else ref.at[:, pl.ds(start, n)]


def _part(ref, s, dim):
    return ref.at[s] if dim is None else _quarter(ref, s, dim)


def _gather_plan(i, dim):
    return dict(src=i, dst=i, send_src=lambda ref, me, peer, k: _part(ref, me, dim),
                send_dst=lambda ref, me, peer, k: _part(ref, me, dim), recv_dst=lambda ref, me, peer, k: _part(ref, peer, dim))


def _scatter_plan(i, j, dim):
    return dict(src=i, dst=j, send_src=lambda ref, me, peer, k: _part(ref, peer, dim),
                send_dst=lambda ref, me, peer, k: ref.at[k], recv_dst=lambda ref, me, peer, k: ref.at[k])


def _hbm(a):
    return pltpu.with_memory_space_constraint(a, pltpu.HBM)


def _remote_copies(refs, send_sems, recv_sems, plans):
    me, c, peers = _chip_peers()
    sends, recvs = [], []
    for i, p in enumerate(plans):
        for k, (px, py) in enumerate(peers):
            peer = 2 * px + py
            sems = dict(send_sem=send_sems.at[3 * i + k], recv_sem=recv_sems.at[3 * i + k], device_id=(px, py, c),
                        device_id_type=MESH)
            src = p['send_src'](refs[p['src']], me, peer, k)
            sends.append(pltpu.make_async_remote_copy(src_ref=src, dst_ref=p['send_dst'](refs[p['dst']], me, peer, k), **sems))
            recvs.append(pltpu.make_async_remote_copy(src_ref=src, dst_ref=p['recv_dst'](refs[p['dst']], me, peer, k), **sems))
    return sends, recvs


def _exchange_start(arrays, plans, name, after=()):
    n, m, extra = len(arrays), len(plans), len(after)

    def body(*refs):
        send_sems, recv_sems, token = refs[n + extra], refs[n + extra + 1], refs[-1]
        sends, _ = _remote_copies(refs[:n], send_sems, recv_sems, plans)
        for cp in sends:
            cp.start()
        token[...] = jnp.zeros_like(token)

    outs = pl.pallas_call(
        body, name=name,
        out_shape=(pltpu.SemaphoreType.DMA((3 * m,)), pltpu.SemaphoreType.DMA((3 * m,)),
                   *[pltpu.HBM(a.shape, a.dtype) for a in arrays], S((8, 128), F32)),
        in_specs=[_HBM] * n + [_ANY] * extra, out_specs=(_SEM, _SEM, *[_HBM] * n, pl.BlockSpec(memory_space=pltpu.VMEM)),
        input_output_aliases={i: 2 + i for i in range(n)},
        compiler_params=pltpu.CompilerParams(has_side_effects=_EFFECT))(*[_hbm(a) for a in arrays], *after)
    return outs[0], outs[1], outs[2:2 + n], outs[-1]


def _exchange_wait(started, plans, after, name):
    send_sems, recv_sems, arrays, _ = started
    n = len(arrays)

    def body(*refs):
        sends, recvs = _remote_copies(refs[:n], refs[n], refs[n + 1], plans)
        for cp in sends:
            cp.wait_send()
        for cp in recvs:
            cp.wait_recv()

    return pl.pallas_call(
        body, name=name, out_shape=tuple(pltpu.HBM(a.shape, a.dtype) for a in arrays),
        in_specs=[_HBM] * n + [_SEM, _SEM, _ANY], out_specs=tuple([_HBM] * n),
        input_output_aliases={i: i for i in range(n)},
        compiler_params=pltpu.CompilerParams(has_side_effects=_EFFECT))(*arrays, send_sems, recv_sems, after)


def _add2(p, q):
    r, c = p.shape
    tr = 128 if r % 128 == 0 else r

    def body(p_ref, q_ref, o_ref):
        o_ref[...] = p_ref[...] + q_ref[...]

    return pl.pallas_call(body, grid=(r // tr,), in_specs=[_row_spec(tr, c), _row_spec(tr, c)], out_specs=_row_spec(tr, c),
                          out_shape=S((r, c), F32), name="add_cores", compiler_params=_cp(("parallel",)))(p, q)


def _adamw(w, g, m, v, name, g2=None, after=()):
    r, c = w.shape
    tr = 256 if r % 256 == 0 else (128 if r % 128 == 0 else r)
    c1 = 1.0 / (1.0 - ADAM_B1 ** ADAM_STEP)
    c2 = 1.0 / (1.0 - ADAM_B2 ** ADAM_STEP)
    gs = [g] if g2 is None else [g, g2]

    def body(*refs):
        w_ref, m_ref, v_ref = refs[0], refs[1], refs[2]
        g_refs = refs[3:3 + len(gs)]
        go_ref, d_ref, nm_ref, nv_ref = refs[3 + len(gs) + len(after):]
        gv = g_refs[0][...]
        if len(gs) == 2:
            gv = gv + g_refs[1][...]
        go_ref[...] = gv
        nm = ADAM_B1 * m_ref[...] + (1.0 - ADAM_B1) * gv
        nv = ADAM_B2 * v_ref[...] + (1.0 - ADAM_B2) * (gv * gv)
        nm_ref[...] = nm
        nv_ref[...] = nv
        d_ref[...] = -ADAM_LR * ((nm * c1) / (jnp.sqrt(nv * c2) + ADAM_EPS) + ADAM_WD * w_ref[...])

    return pl.pallas_call(body, grid=(r // tr,),
                          in_specs=[_row_spec(tr, c)] * (3 + len(gs)) + [pl.BlockSpec(memory_space=pl.ANY)] * len(after),
                          out_specs=[_row_spec(tr, c)] * 4, out_shape=[S((r, c), F32)] * 4, name=name,
                          compiler_params=_cp(("parallel",)))(w, m, v, *gs, *after)


_WEIGHTS = ['pre_norm', 'post_norm', 'rel_bias', 'a_w_in', 'a_lam_re', 'a_lam_im', 'a_log_dt', 'a_b_re', 'a_b_im',
            'a_c_re', 'a_c_im', 'a_d', 'a_w_glu', 'a_b_glu', 'a_w_out', 'b_w_in', 'b_sinks', 'b_w_out', 'c_w_in',
            'c_q_norm', 'c_kv_norm', 'c_w_uq', 'c_w_ukv', 'c_w_out', 'd_w_in', 'd_ln_g', 'd_ln_b', 'd_w_s', 'd_b_s',
            'd_w_out']
_BIG = [('a_w_in', 1024, 2048, 1), ('a_w_glu', 1024, 1024, 0), ('a_w_out', 1024, 1024, 0), ('b_w_in', 1024, 2304, 1),
        ('b_w_out', 1024, 1024, 0), ('c_w_in', 1024, 2080, 1), ('c_w_uq', 768, 1536, 1), ('c_w_ukv', 256, 2048, 1),
        ('c_w_out', 1024, 1024, 0), ('d_w_in', 1024, 3072, 1), ('d_w_out', 1024, 1024, 0)]
_SHARDED_VECS = [('c_q_norm', 768), ('c_kv_norm', 256), ('d_ln_g', 1024), ('d_ln_b', 1024)]
_TRANSPOSED = ('b_w_in',)
_REPLICATED = [n for n in _WEIGHTS if n not in [b[0] for b in _BIG] and n not in [s[0] for s in _SHARDED_VECS]]
_LANES = 1024
_SMALL_Q_ROWS = 136


def kernel(x, pre_norm, post_norm, rel_bias, a_w_in, a_lam_re, a_lam_im, a_log_dt, a_b_re, a_b_im, a_c_re, a_c_im, a_d, a_w_glu, a_b_glu, a_w_out, b_w_in, b_sinks, b_w_out, c_w_in, c_q_norm, c_kv_norm, c_w_uq, c_w_ukv, c_w_out, d_w_in, d_ln_g, d_ln_b, d_w_s, d_b_s, d_w_out, loss_target, m_pre_norm, m_post_norm, m_rel_bias, m_a_w_in, m_a_lam_re, m_a_lam_im, m_a_log_dt, m_a_b_re, m_a_b_im, m_a_c_re, m_a_c_im, m_a_d, m_a_w_glu, m_a_b_glu, m_a_w_out, m_b_w_in, m_b_sinks, m_b_w_out, m_c_w_in, m_c_q_norm, m_c_kv_norm, m_c_w_uq, m_c_w_ukv, m_c_w_out, m_d_w_in, m_d_ln_g, m_d_ln_b, m_d_w_s, m_d_b_s, m_d_w_out, v_pre_norm, v_post_norm, v_rel_bias, v_a_w_in, v_a_lam_re, v_a_lam_im, v_a_log_dt, v_a_b_re, v_a_b_im, v_a_c_re, v_a_c_im, v_a_d, v_a_w_glu, v_a_b_glu, v_a_w_out, v_b_w_in, v_b_sinks, v_b_w_out, v_c_w_in, v_c_q_norm, v_c_kv_norm, v_c_w_uq, v_c_w_ukv, v_c_w_out, v_d_w_in, v_d_ln_g, v_d_ln_b, v_d_w_s, v_d_b_s, v_d_w_out):
    arg = dict(locals())
    me = 2 * lax.axis_index("x") + lax.axis_index("y")
    xin = x[0]
    t = xin.shape[0]

    big = {n: (r, c, dim) for n, r, c, dim in _BIG}

    me1 = me.reshape(1).astype(jnp.int32)

    def gather_start(names, tag, after):
        wholes, plans = [], []
        for i, n in enumerate(names):
            if n in big:
                r, c, dim = big[n]
                src, dtype = arg[n][0], BF16
                if n in _TRANSPOSED:
                    src, dim = src.T, 0
                elif dim == 1 and (c // N_CHIPS) % 128:
                    dim = None
            else:
                src, dim, dtype = arg[n], None, F32
            wholes.append(_cast_place(me1, src, dim, dtype, "place_" + n))
            plans.append(_gather_plan(i, dim))
        return _exchange_start(wholes, plans, "gather_start_" + tag, after=after), plans

    def gather_wait(names, started, after, tag):
        st, plans = started
        out = {}
        for n, a in zip(names, _exchange_wait(st, plans, after, "gather_wait_" + tag)):
            if n in big:
                r, c, dim = big[n]
                out[n] = a if a.ndim == 2 else a.transpose(1, 0, 2).reshape(r, c)
            else:
                out[n] = a.reshape(1, -1)
        return out

    groups = {'a1': ['a_w_in'], 'a2': ['a_w_glu', 'a_w_out'], 'b': ['b_w_in', 'b_w_out'],
              'c': ['c_w_in', 'c_w_uq', 'c_w_ukv', 'c_w_out', 'c_q_norm', 'c_kv_norm'],
              'd': ['d_w_in', 'd_w_out', 'd_ln_g', 'd_ln_b']}
    started, tokens = {}, []
    for tag, names in groups.items():
        started[tag] = gather_start(names, tag, tokens)
        tokens = [started[tag][0][3]]
    wt = gather_wait(groups['a1'], started['a1'], xin, 'a1')
    cos, sin = _rope_tables(t)
    bucket = jnp.asarray(_t5_bucket_table())
    s5p = _s5_params(a_lam_re[0], a_lam_im[0], a_log_dt[0], a_b_re[0], a_b_im[0], a_c_re[0], a_c_im[0])
    bst = d_b_s[0].T

    x0 = xin
    h0 = _prenorm(x0, pre_norm[0:1], "pre0", after=tokens)
    proj0 = _mm(h0, wt['a_w_in'], name="a_in")

    def w_glu_after(x):
        wt.update(gather_wait(groups['a2'], started['a2'], x, 'a2'))
        return wt['a_w_glu']

    p0, res0 = _s5_fwd(proj0, s5p, a_d, w_glu_after, a_b_glu)
    y0 = _mm(p0, wt['a_w_out'], name="a_out")
    x1, h1 = _postnorm_prenorm(x0, y0, post_norm[0:1], pre_norm[1:2], "post0_pre1")

    wt.update(gather_wait(groups['b'], started['b'], x1, 'b'))
    proj1 = _mm(h1, wt['b_w_in'], tb=True, name="b_in")
    bias = _swa_bias(rel_bias, bucket)
    p1 = _swa_fwd(proj1, bias, b_sinks)
    y1 = _mm(p1, wt['b_w_out'], name="b_out")
    x2, h2 = _postnorm_prenorm(x1, y1, post_norm[1:2], pre_norm[2:3], "post1_pre2")

    wt.update(gather_wait(groups['c'], started['c'], x2, 'c'))
    w_c_in = jnp.concatenate([wt['c_w_in'][:, :1056], jnp.zeros((1024, 96), BF16), wt['c_w_in'][:, 1056:],
                              jnp.zeros((1024, MLA_PROJ - MLA_Z1), BF16)], axis=1)
    w_uq = _perm_uq(wt['c_w_uq'])
    w_ukv = _perm_ukv(wt['c_w_ukv'])
    proj2 = _mm(h2, w_c_in, name="c_in")
    p2, res2 = _mla_fwd(proj2, wt['c_q_norm'], wt['c_kv_norm'], w_uq, w_ukv, cos, sin)
    y2 = _mm(p2, wt['c_w_out'], name="c_out")
    x3, h3 = _postnorm_prenorm(x2, y2, post_norm[2:3], pre_norm[3:4], "post2_pre3")

    wt.update(gather_wait(groups['d'], started['d'], x3, 'd'))
    vec = wt
    proj3 = _mm(h3, wt['d_w_in'], name="d_in")
    p3 = _sgu_fwd(proj3, vec['d_ln_g'], vec['d_ln_b'], d_w_s[0], bst)
    y3 = _mm(p3, wt['d_w_out'], name="d_out")
    loss_part, dx = _postnorm_loss(x3, y3, post_norm[3:4], loss_target[0])
    loss = lax.psum(loss_part[0, 0], ("x", "y", "c"))

    gr = {}
    d_pre, d_post = [None] * 4, [None] * 4
    scattered = []

    def scatter_start(items, tag):
        srcs = [a for _, a, _ in items]
        n = len(items)
        plans = [_scatter_plan(i, n + i, dim) for i, (_, _, dim) in enumerate(items)]
        lands = []
        for _, a, dim in items:
            part = a.shape[1:] if dim is None else tuple(s // N_CHIPS if i == dim else s for i, s in enumerate(a.shape))
            lands.append(lax.empty((3,) + part, a.dtype))
        st = _exchange_start(srcs + lands, plans, "scatter_start_" + tag)
        scattered.append((st, plans, items, tag))
        return [st[3]]

    def stacked(g):
        r, c = g.shape
        return g.reshape(r, N_CHIPS, c // N_CHIPS).transpose(1, 0, 2)

    dy, d_post[3] = _postnorm_bwd(dx, y3, post_norm[3:4], "post3_bwd")
    dp = _mm(dy, wt['d_w_out'], tb=True, name="d_out_dx")
    g_out = _mm(p3, dy, ta=True, out_dtype=BF16, name="d_out_dw")
    dproj, gr['d_w_s'], dbst, gr['d_ln_g'], gr['d_ln_b'] = _sgu_bwd(proj3, dp, vec['d_ln_g'], vec['d_ln_b'], d_w_s[0], bst)
    gr['d_b_s'] = dbst.T
    dh = _mm(dproj, wt['d_w_in'], tb=True, name="d_in_dx")
    g_in = _mm(h3, dproj, ta=True, out_dtype=BF16, name="d_in_dw")
    tok = scatter_start([('d_w_in', g_in, 1), ('d_w_out', g_out, 0)], 'd')
    dx, d_pre[3], dy, d_post[2] = _norm_bwd_pair(dx, dh, x3, pre_norm[3:4], y2, post_norm[2:3], "pre3_post2_bwd", after=tok)

    dp = _mm(dy, wt['c_w_out'], tb=True, name="c_out_dx")
    g_out = _mm(p2, dy, ta=True, out_dtype=BF16, name="c_out_dw")
    dproj, gr['c_q_norm'], gr['c_kv_norm'], dwuq, dwukv = _mla_bwd(proj2, res2, dp, wt['c_q_norm'], wt['c_kv_norm'],
                                                                  w_uq, w_ukv, cos, sin)
    dh = _mm(dproj, w_c_in, tb=True, name="c_in_dx")
    dwc = _mm(h2, dproj, ta=True, out_dtype=BF16, name="c_in_dw")
    g_in = stacked(jnp.concatenate([dwc[:, :1056], dwc[:, MLA_Z0:MLA_Z1]], axis=1))
    tok = scatter_start([('c_w_in', g_in, None), ('c_w_uq', _unperm_uq(dwuq), 1), ('c_w_ukv', _unperm_ukv(dwukv), 1),
                         ('c_w_out', g_out, 0)], 'c')
    dx, d_pre[2], dy, d_post[1] = _norm_bwd_pair(dx, dh, x2, pre_norm[2:3], y1, post_norm[1:2], "pre2_post1_bwd", after=tok)

    dp = _mm(dy, wt['b_w_out'], tb=True, name="b_out_dx")
    g_out = _mm(p1, dy, ta=True, out_dtype=BF16, name="b_out_dw")
    dqz, dkv, dss, dsk = _swa_bwd(proj1, dp, bias, b_sinks)
    gr['rel_bias'], gr['b_sinks'] = _swa_dbias(dss, dsk, bucket)
    dproj = jnp.concatenate([dqz[:, :1024], dkv.astype(BF16), dqz[:, 1024:]], axis=1)
    dh = _mm(dproj, wt['b_w_in'], name="b_in_dx")
    g_in = _mm(dproj, h1, ta=True, out_dtype=BF16, name="b_in_dw")
    tok = scatter_start([('b_w_in', g_in, 0), ('b_w_out', g_out, 0)], 'b')
    dx, d_pre[1], dy, d_post[0] = _norm_bwd_pair(dx, dh, x1, pre_norm[1:2], y0, post_norm[0:1], "pre1_post0_bwd", after=tok)

    dp = _mm(dy, wt['a_w_out'], tb=True, name="a_out_dx")
    g_out = _mm(p0, dy, ta=True, out_dtype=BF16, name="a_out_dw")
    dproj, s5cots, gr['a_d'], g_glu, gr['a_b_glu'] = _s5_bwd(proj0, res0, dp, s5p, a_d, wt['a_w_glu'], a_b_glu)
    tok = scatter_start([('a_w_glu', g_glu, 0), ('a_w_out', g_out, 0)], 'a2')
    g_in = _mm(h0, dproj, ta=True, out_dtype=BF16, name="a_in_dw", after=tok)
    tok = scatter_start([('a_w_in', g_in, 1)], 'a1')
    dh = _mm(dproj, wt['a_w_in'], tb=True, name="a_in_dx", after=tok)
    dx, d_pre[0] = _prenorm_bwd(dx, dh, x0, pre_norm[0:1], "pre0_bwd")
    s5g = _s5_param_grads(s5p, s5cots, after=tok)
    for n, g in zip(['a_lam_re', 'a_lam_im', 'a_log_dt', 'a_b_re', 'a_b_im', 'a_c_re', 'a_c_im'], s5g):
        gr[n] = g
    gr['pre_norm'] = jnp.concatenate(d_pre, axis=0)
    gr['post_norm'] = jnp.concatenate(d_post, axis=0)
    small_names = _REPLICATED + [n for n, _ in _SHARDED_VECS]
    small = jnp.concatenate([gr[n].reshape(-1) for n in small_names])
    small = jnp.pad(small, (0, N_CHIPS * _SMALL_Q_ROWS * _LANES - small.shape[0])).reshape(N_CHIPS, _SMALL_Q_ROWS, _LANES)
    tok = scatter_start([('small', small, None)], 'a0')

    partial = {}
    for st, plans, items, tag in scattered[:-1]:
        done = _exchange_wait(st, plans, tok[0], "scatter_wait_" + tag)
        for i, (n, _, dim) in enumerate(items):
            partial[n] = _sum_parts(me1, done[i], done[len(items) + i], dim, "sum_parts_" + n)
    big_names = [n for n, _, _, _ in _BIG]
    other = dict(zip(big_names, _sibling_swap([partial[n] for n in big_names], "swap_cores")))
    st, plans, _, tag = scattered[-1]
    done = _exchange_wait(st, plans, other[big_names[-1]], "scatter_wait_" + tag)
    p_small = _sum_parts(me1, done[0], done[1], None, "sum_parts_small")
    small_q = _add2(p_small, _sibling_swap([p_small], "swap_cores_small")[0])
    gplan = [_gather_plan(0, None)]
    gathering = _exchange_start([_cast_place(me1, small_q, None, F32, "place_small_grads")], gplan, "gather_small_start")

    delta, new_m, new_v, grads = {}, {}, {}, {}
    for n in big_names:
        shp = arg[n].shape
        view = (lambda a: a[0].T) if n in _TRANSPOSED else (lambda a: a[0])
        back = (lambda o: o.T.reshape(shp)) if n in _TRANSPOSED else (lambda o: o.reshape(shp))
        outs = _adamw(view(arg[n]), partial[n], view(arg['m_' + n]), view(arg['v_' + n]), "adamw_" + n, g2=other[n],
                      after=[gathering[3]])
        grads[n], delta[n], new_m[n], new_v[n] = [back(o) for o in outs]
    small_all = _exchange_wait(gathering, gplan, new_v[big_names[-1]], "gather_small_wait")[0].reshape(-1)
    def view(n, a):
        return a[0].transpose(0, 2, 1) if n in ('a_b_re', 'a_b_im') else a

    def back(n, v):
        return v.transpose(0, 2, 1)[None] if n in ('a_b_re', 'a_b_im') else v

    off = 0
    gview = {}
    for n in _REPLICATED:
        shp = view(n, arg[n]).shape
        sz = math.prod(shp)
        gview[n] = small_all[off:off + sz].reshape(shp)
        off += sz
    for n, sz in _SHARDED_VECS:
        q = sz // N_CHIPS
        gview[n] = lax.dynamic_slice(small_all, (off + me * q,), (q,)).reshape(arg[n].shape)
        off += sz

    for n in small_names:
        shp = gview[n].shape
        two = (math.prod(shp[:-1]), shp[-1])
        outs = _adamw(view(n, arg[n]).reshape(two), gview[n].reshape(two), view(n, arg['m_' + n]).reshape(two),
                      view(n, arg['v_' + n]).reshape(two), "adamw_" + n)
        grads[n] = back(n, gview[n])
        delta[n], new_m[n], new_v[n] = [back(n, o.reshape(shp)) for o in outs[1:]]

    return (loss, dx.reshape(x.shape), *[grads[n] for n in _WEIGHTS], *[delta[n] for n in _WEIGHTS],
            *[new_m[n] for n in _WEIGHTS], *[new_v[n] for n in _WEIGHTS])
```

```python
import functools
import math

import numpy as np
import jax
import jax.numpy as jnp
from jax import lax
from jax.experimental import pallas as pl
from jax.experimental.pallas import tpu as pltpu

F32 = jnp.float32
BF16 = jnp.bfloat16
S = jax.ShapeDtypeStruct
MESH = pl.DeviceIdType.MESH

D_MODEL = 1024
EPS = 1e-6
NEG_INF = -1e30
N_CHIPS = 4

SSM_GROUPS = 64
SSM_GROUP = 16
SSM_STATE = 64
SSM_CHUNK = 16
SSM_BLOCK = SSM_CHUNK * SSM_GROUP

HEAD_DIM = 64
SWA_HEADS = 16
SWA_GROUP = 8
WINDOW = 128
REL_BUCKETS = 32
REL_MAX_DIST = 128

MLA_HEADS = 16
MLA_NOPE = 64
MLA_ROPE = 32
MLA_Q_RANK = 768
MLA_KV_RANK = 256
MLA_SCALE = (MLA_NOPE + MLA_ROPE) ** -0.5
MLA_BQ = 512
MLA_BK = 512
ROPE_BASE = 10000.0

SGU_CHUNK = 128
SGU_GROUPS = 16
SGU_GDIM = 64

ADAM_LR = 0.001
ADAM_B1 = 0.9
ADAM_B2 = 0.999
ADAM_EPS = 1e-08
ADAM_WD = 0.01
ADAM_STEP = 10

VMEM_BIG = 56 * 1024 * 1024


def _cp(sem, vmem=None):
    return pltpu.CompilerParams(dimension_semantics=sem, vmem_limit_bytes=vmem)


def _row_spec(tr, c):
    return pl.BlockSpec((tr, c), lambda i: (i, 0))


def _full_spec(shape):
    nd = len(shape)
    return pl.BlockSpec(shape, lambda *_: (0,) * nd)


def _gelu(x):
    c = 0.7978845608028654
    return 0.5 * x * (1.0 + jnp.tanh(c * (x + 0.044715 * x * x * x)))


def _gelu_grad(x):
    c = 0.7978845608028654
    t = jnp.tanh(c * (x + 0.044715 * x * x * x))
    return 0.5 * (1.0 + t) + 0.5 * x * (1.0 - t * t) * c * (1.0 + 3.0 * 0.044715 * x * x)


def _silu(z):
    return z * jax.nn.sigmoid(z)


def _silu_grad(z):
    s = jax.nn.sigmoid(z)
    return s * (1.0 + z * (1.0 - s))


def _dot(a, b):
    return lax.dot_general(a, b, (((1,), (0,)), ((), ())), preferred_element_type=F32)


def _dot_nt(a, b):
    return lax.dot_general(a, b, (((1,), (1,)), ((), ())), preferred_element_type=F32)


def _dot_tn(a, b):
    return lax.dot_general(a, b, (((0,), (0,)), ((), ())), preferred_element_type=F32)


def _pick(n, cap):
    best = 0
    for d in range(128, min(n, cap) + 1, 128):
        if n % d == 0:
            best = d
    if best < 256:
        return n
    return best


def _mm(a, b, *, ta=False, tb=False, out_dtype=F32, name, after=()):
    m, k = (a.shape[1], a.shape[0]) if ta else a.shape
    n = b.shape[0] if tb else b.shape[1]
    assert (b.shape[1] if tb else b.shape[0]) == k
    tm, tn, tk = _pick(m, 1024), _pick(n, 1024), _pick(k, 1024)
    if tn > 1024:
        tm = _pick(m, 256)
    if tk > 1024:
        tm, tn = _pick(m, 256), _pick(n, 256)
    nk = k // tk
    dims = (((0 if ta else 1,), (1 if tb else 0,)), ((), ()))

    def body(a_ref, b_ref, *rest):
        o_ref, acc_ref = rest[-2], rest[-1]
        kk = pl.program_id(2)

        @pl.when(kk == 0)
        def _():
            acc_ref[...] = jnp.zeros_like(acc_ref)

        acc_ref[...] += lax.dot_general(a_ref[...].astype(BF16), b_ref[...].astype(BF16), dims,
                                        preferred_element_type=F32)

        @pl.when(kk == nk - 1)
        def _():
            o_ref[...] = acc_ref[...].astype(out_dtype)

    a_spec = pl.BlockSpec((tk, tm), lambda i, j, kk: (kk, i)) if ta else pl.BlockSpec((tm, tk), lambda i, j, kk: (i, kk))
    b_spec = pl.BlockSpec((tn, tk), lambda i, j, kk: (j, kk)) if tb else pl.BlockSpec((tk, tn), lambda i, j, kk: (kk, j))
    return pl.pallas_call(
        body, grid=(m // tm, n // tn, nk), in_specs=[a_spec, b_spec] + [pl.BlockSpec(memory_space=pl.ANY)] * len(after),
        out_specs=pl.BlockSpec((tm, tn), lambda i, j, kk: (i, j)), out_shape=S((m, n), out_dtype),
        scratch_shapes=[pltpu.VMEM((tm, tn), F32)], name=name,
        compiler_params=_cp(("parallel", "parallel", "arbitrary"), VMEM_BIG))(a, b, *after)


def _prenorm(x, g, name, after=()):
    t, d = x.shape
    tr = 256

    def body(x_ref, g_ref, *rest):
        o_ref = rest[-1]
        xv = x_ref[...]
        r = lax.rsqrt(jnp.mean(xv * xv, axis=-1, keepdims=True) + EPS)
        o_ref[...] = (xv * r * g_ref[...]).astype(BF16)

    return pl.pallas_call(body, grid=(t // tr,),
                          in_specs=[_row_spec(tr, d), _full_spec((1, d))] + [pl.BlockSpec(memory_space=pl.ANY)] * len(after),
                          out_specs=_row_spec(tr, d), out_shape=S((t, d), BF16), name=name,
                          compiler_params=_cp(("parallel",)))(x, g, *after)


def _rms_bwd_rows(dout, xin, g):
    r = lax.rsqrt(jnp.mean(xin * xin, axis=-1, keepdims=True) + EPS)
    xh = xin * r
    gd = dout * g
    dx = r * (gd - xh * jnp.mean(gd * xh, axis=-1, keepdims=True))
    return dx, jnp.sum(dout * xh, axis=0, keepdims=True)


def _postnorm_bwd(dxn, y, g, name, after=()):
    t, d = y.shape
    tr = 256

    def body(dx_ref, y_ref, g_ref, *rest):
        dy_ref, dg_ref = rest[-2], rest[-1]

        @pl.when(pl.program_id(0) == 0)
        def _():
            dg_ref[...] = jnp.zeros_like(dg_ref)

        dy, dg = _rms_bwd_rows(dx_ref[...], y_ref[...], g_ref[...])
        dy_ref[...] = dy.astype(BF16)
        dg_ref[...] += dg

    return pl.pallas_call(body, grid=(t // tr,),
                          in_specs=[_row_spec(tr, d), _row_spec(tr, d), _full_spec((1, d))]
                          + [pl.BlockSpec(memory_space=pl.ANY)] * len(after),
                          out_specs=[_row_spec(tr, d), _full_spec((1, d))],
                          out_shape=[S((t, d), BF16), S((1, d), F32)], name=name,
                          compiler_params=_cp(("arbitrary",)))(dxn, y, g, *after)


def _prenorm_bwd(dxn, dh, x, g, name):
    t, d = x.shape
    tr = 256

    def body(dxn_ref, dh_ref, x_ref, g_ref, dx_ref, dg_ref):
        @pl.when(pl.program_id(0) == 0)
        def _():
            dg_ref[...] = jnp.zeros_like(dg_ref)

        dx, dg = _rms_bwd_rows(dh_ref[...], x_ref[...], g_ref[...])
        dx_ref[...] = dxn_ref[...] + dx
        dg_ref[...] += dg

    return pl.pallas_call(body, grid=(t // tr,),
                          in_specs=[_row_spec(tr, d), _row_spec(tr, d), _row_spec(tr, d), _full_spec((1, d))],
                          out_specs=[_row_spec(tr, d), _full_spec((1, d))],
                          out_shape=[S((t, d), F32), S((1, d), F32)], name=name,
                          compiler_params=_cp(("arbitrary",)))(dxn, dh, x, g)


def _postnorm_prenorm(x, y, gpost, gpre, name):
    t, d = x.shape
    tr = 256

    def body(x_ref, y_ref, gp_ref, gn_ref, xn_ref, h_ref):
        yv = y_ref[...]
        r = lax.rsqrt(jnp.mean(yv * yv, axis=-1, keepdims=True) + EPS)
        xn = x_ref[...] + yv * r * gp_ref[...]
        xn_ref[...] = xn
        r2 = lax.rsqrt(jnp.mean(xn * xn, axis=-1, keepdims=True) + EPS)
        h_ref[...] = (xn * r2 * gn_ref[...]).astype(BF16)

    return pl.pallas_call(body, grid=(t // tr,),
                          in_specs=[_row_spec(tr, d), _row_spec(tr, d), _full_spec((1, d)), _full_spec((1, d))],
                          out_specs=[_row_spec(tr, d), _row_spec(tr, d)], out_shape=[S((t, d), F32), S((t, d), BF16)],
                          name=name, compiler_params=_cp(("parallel",)))(x, y, gpost, gpre)


def _norm_bwd_pair(dxn, dh, x, gpre, yprev, gpost, name, after=()):
    t, d = x.shape
    tr = 256

    def body(dxn_ref, dh_ref, x_ref, gn_ref, y_ref, gp_ref, *rest):
        dx_ref, dgn_ref, dy_ref, dgp_ref = rest[-4:]

        @pl.when(pl.program_id(0) == 0)
        def _():
            dgn_ref[...] = jnp.zeros_like(dgn_ref)
            dgp_ref[...] = jnp.zeros_like(dgp_ref)

        dxa, dgn = _rms_bwd_rows(dh_ref[...], x_ref[...], gn_ref[...])
        dx = dxn_ref[...] + dxa
        dx_ref[...] = dx
        dgn_ref[...] += dgn
        dy, dgp = _rms_bwd_rows(dx, y_ref[...], gp_ref[...])
        dy_ref[...] = dy.astype(BF16)
        dgp_ref[...] += dgp

    rows, vec = _row_spec(tr, d), _full_spec((1, d))
    return pl.pallas_call(body, grid=(t // tr,),
                          in_specs=[rows, rows, rows, vec, rows, vec] + [pl.BlockSpec(memory_space=pl.ANY)] * len(after),
                          out_specs=[rows, vec, rows, vec],
                          out_shape=[S((t, d), F32), S((1, d), F32), S((t, d), BF16), S((1, d), F32)], name=name,
                          compiler_params=_cp(("arbitrary",)))(dxn, dh, x, gpre, yprev, gpost, *after)


def _postnorm_loss(x, y, gpost, target):
    t, d = x.shape
    tr = 256

    def body(x_ref, y_ref, g_ref, t_ref, l_ref, dy_ref):
        @pl.when(pl.program_id(0) == 0)
        def _():
            l_ref[...] = jnp.zeros_like(l_ref)

        yv = y_ref[...]
        r = lax.rsqrt(jnp.mean(yv * yv, axis=-1, keepdims=True) + EPS)
        e = (x_ref[...] + yv * r * g_ref[...]) - t_ref[...]
        dy_ref[...] = e * (1.0 / d)
        l_ref[...] += 0.5 * jnp.sum(jnp.mean(e * e, axis=-1, keepdims=True), axis=0, keepdims=True)

    return pl.pallas_call(body, grid=(t // tr,),
                          in_specs=[_row_spec(tr, d), _row_spec(tr, d), _full_spec((1, d)), _row_spec(tr, d)],
                          out_specs=[_full_spec((1, 1)), _row_spec(tr, d)],
                          out_shape=[S((1, 1), F32), S((t, d), F32)], name="postnorm_loss",
                          compiler_params=_cp(("arbitrary",)))(x, y, gpost, target)


def _sgu_parts(proj, lg, lb, ws_ref, bst):
    a = proj[:, 0:1024]
    bb = proj[:, 1024:2048]
    z = proj[:, 2048:3072]
    u = _gelu(a)
    vp = _gelu(bb)
    mu = jnp.mean(vp, axis=-1, keepdims=True)
    xc = vp - mu
    rstd = lax.rsqrt(jnp.mean(xc * xc, axis=-1, keepdims=True) + EPS)
    xh = xc * rstd
    vn = xh * lg + lb
    row = lax.broadcasted_iota(jnp.int32, (SGU_CHUNK, SGU_CHUNK), 0)
    col = lax.broadcasted_iota(jnp.int32, (SGU_CHUNK, SGU_CHUNK), 1)
    tril = row >= col
    ws = [jnp.where(tril, ws_ref[g], 0.0).astype(BF16) for g in range(SGU_GROUPS)]
    s = jnp.concatenate(
        [_dot(ws[g], vn[:, g * 64:(g + 1) * 64].astype(BF16)) + bst[:, g:g + 1] for g in range(SGU_GROUPS)], axis=1)
    return a, bb, z, u, vp, rstd, xh, vn, ws, tril, s


def _sgu_fwd(proj, lg, lb, w_s, bst):
    t = proj.shape[0]
    tr = SGU_CHUNK

    def body(p_ref, lg_ref, lb_ref, ws_ref, bst_ref, o_ref):
        _, _, z, u, _, _, _, _, _, _, s = _sgu_parts(p_ref[...], lg_ref[...], lb_ref[...], ws_ref, bst_ref[...])
        o_ref[...] = (u * s * _silu(z)).astype(BF16)

    return pl.pallas_call(
        body, grid=(t // tr,),
        in_specs=[_row_spec(tr, 3072), _full_spec((1, 1024)), _full_spec((1, 1024)), _full_spec((16, 128, 128)),
                  _full_spec((128, 16))],
        out_specs=_row_spec(tr, 1024), out_shape=S((t, 1024), BF16), name="sgu_fwd",
        compiler_params=_cp(("parallel",)))(proj, lg, lb, w_s, bst)


def _sgu_bwd(proj, dp, lg, lb, w_s, bst):
    t = proj.shape[0]
    tr = SGU_CHUNK

    def body(p_ref, dp_ref, lg_ref, lb_ref, ws_ref, bst_ref, dproj_ref, dws_ref, dbst_ref, dlg_ref, dlb_ref):
        @pl.when(pl.program_id(0) == 0)
        def _():
            dws_ref[...] = jnp.zeros_like(dws_ref)
            dbst_ref[...] = jnp.zeros_like(dbst_ref)
            dlg_ref[...] = jnp.zeros_like(dlg_ref)
            dlb_ref[...] = jnp.zeros_like(dlb_ref)

        lgv = lg_ref[...]
        a, bb, z, u, vp, rstd, xh, vn, ws, tril, s = _sgu_parts(p_ref[...], lgv, lb_ref[...], ws_ref, bst_ref[...])
        dpv = dp_ref[...].astype(F32)
        sz = _silu(z)
        du = dpv * s * sz
        ds = dpv * u * sz
        dz = dpv * u * s * _silu_grad(z)
        dvn_parts, dbs_cols = [], []
        for g in range(SGU_GROUPS):
            dsg = ds[:, g * 64:(g + 1) * 64]
            dsg_b = dsg.astype(BF16)
            dvn_parts.append(_dot_tn(ws[g], dsg_b))
            dws_ref[g] += jnp.where(tril, _dot_nt(dsg_b, vn[:, g * 64:(g + 1) * 64].astype(BF16)), 0.0)
            dbs_cols.append(jnp.sum(dsg, axis=1, keepdims=True))
        dvn = jnp.concatenate(dvn_parts, axis=1)
        dbst_ref[...] += jnp.concatenate(dbs_cols, axis=1)
        dlg_ref[...] += jnp.sum(dvn * xh, axis=0, keepdims=True)
        dlb_ref[...] += jnp.sum(dvn, axis=0, keepdims=True)
        dxh = dvn * lgv
        dvp = rstd * (dxh - jnp.mean(dxh, axis=-1, keepdims=True) - xh * jnp.mean(dxh * xh, axis=-1, keepdims=True))
        dproj_ref[:, 0:1024] = (du * _gelu_grad(a)).astype(BF16)
        dproj_ref[:, 1024:2048] = (dvp * _gelu_grad(bb)).astype(BF16)
        dproj_ref[:, 2048:3072] = dz.astype(BF16)

    return pl.pallas_call(
        body, grid=(t // tr,),
        in_specs=[_row_spec(tr, 3072), _row_spec(tr, 1024), _full_spec((1, 1024)), _full_spec((1, 1024)),
                  _full_spec((16, 128, 128)), _full_spec((128, 16))],
        out_specs=[_row_spec(tr, 3072), _full_spec((16, 128, 128)), _full_spec((128, 16)), _full_spec((1, 1024)),
                   _full_spec((1, 1024))],
        out_shape=[S((t, 3072), BF16), S((16, 128, 128), F32), S((128, 16), F32), S((1, 1024), F32),
                   S((1, 1024), F32)],
        name="sgu_bwd", compiler_params=_cp(("arbitrary",)))(proj, dp, lg, lb, w_s, bst)


def _t5_bucket_table():
    qi = np.arange(WINDOW)[:, None]
    kj = np.arange(2 * WINDOW)[None, :]
    dist = np.maximum(qi + WINDOW - kj, 0)
    max_exact = REL_BUCKETS // 2
    dist_f = np.maximum(dist, 1).astype(np.float32)
    large = max_exact + (np.log(dist_f / np.float32(max_exact)) / np.float32(math.log(REL_MAX_DIST / max_exact))
                         * np.float32(REL_BUCKETS - max_exact)).astype(np.int32)
    large = np.minimum(large, REL_BUCKETS - 1)
    return np.where(dist < max_exact, dist, large).astype(np.int32)


def _swa_bias(rel_bias, bucket):
    def body(rb_ref, bk_ref, o_ref):
        bk = bk_ref[...]
        for h in range(SWA_HEADS):
            acc = jnp.zeros((WINDOW, 2 * WINDOW), F32)
            for b in range(REL_BUCKETS):
                acc = jnp.where(bk == b, rb_ref[b, h], acc)
            o_ref[h] = acc

    return pl.pallas_call(
        body, in_specs=[pl.BlockSpec(memory_space=pltpu.SMEM), pl.BlockSpec(memory_space=pltpu.VMEM)],
        out_specs=pl.BlockSpec(memory_space=pltpu.VMEM), out_shape=S((SWA_HEADS, WINDOW, 2 * WINDOW), F32),
        name="swa_bias")(rel_bias, bucket)


def _swa_dbias(ds_sum, dsink_rows, bucket):
    def body(ds_ref, dsk_ref, bk_ref, drb_ref, dsink_ref):
        bk = bk_ref[...]
        r = lax.broadcasted_iota(jnp.int32, (REL_BUCKETS, SWA_HEADS), 0)
        c = lax.broadcasted_iota(jnp.int32, (REL_BUCKETS, SWA_HEADS), 1)
        out = jnp.zeros((REL_BUCKETS, SWA_HEADS), F32)
        for b in range(REL_BUCKETS):
            m = bk == b
            for h in range(SWA_HEADS):
                val = jnp.sum(jnp.where(m, ds_ref[h], 0.0))
                out = jnp.where((r == b) & (c == h), val, out)
        drb_ref[...] = out
        dsink_ref[...] = jnp.sum(dsk_ref[...], axis=0, keepdims=True)

    return pl.pallas_call(
        body, out_shape=[S((REL_BUCKETS, SWA_HEADS), F32), S((1, SWA_HEADS), F32)], name="swa_dbias",
    )(ds_sum, dsink_rows, bucket)


def _swa_band(cur_ref, prev_ref):
    kband = jnp.concatenate([prev_ref[:, 0:128], cur_ref[:, 1024:1152]], axis=0).astype(BF16)
    vband = jnp.concatenate([prev_ref[:, 128:256], cur_ref[:, 1152:1280]], axis=0).astype(BF16)
    return kband, vband


SWA_ROWS = SWA_GROUP * WINDOW


def _swa_valid(n, rows=SWA_ROWS):
    qi = lax.broadcasted_iota(jnp.int32, (rows, 2 * WINDOW), 0) & (WINDOW - 1)
    kj = lax.broadcasted_iota(jnp.int32, (rows, 2 * WINDOW), 1)
    dist = qi + WINDOW - kj
    return (dist >= 0) & (dist < WINDOW) & (n * WINDOW + kj - WINDOW >= 0)


def _swa_stack(x, j):
    return jnp.concatenate([x[:, h * 64:(h + 1) * 64] for h in range(j * SWA_GROUP, (j + 1) * SWA_GROUP)], axis=0)


def _swa_unstack(parts):
    return jnp.concatenate([p[i * WINDOW:(i + 1) * WINDOW] for p in parts for i in range(SWA_GROUP)], axis=1)


def _swa_sink_col(sink_ref, j):
    return jnp.concatenate([jnp.full((WINDOW, 1), sink_ref[0, h], F32)
                            for h in range(j * SWA_GROUP, (j + 1) * SWA_GROUP)], axis=0)


def _swa_probs(qh, kh, bias_h, sink, valid):
    s = _dot_nt(qh, kh) * (HEAD_DIM ** -0.5) + bias_h
    s = jnp.where(valid, s, NEG_INF)
    m = jnp.maximum(jnp.max(s, axis=-1, keepdims=True), sink)
    e = jnp.exp(s - m)
    es = jnp.exp(sink - m)
    inv = 1.0 / (jnp.sum(e, axis=-1, keepdims=True) + es)
    return e * inv, es * inv


def _swa_in_specs():
    return [_row_spec(WINDOW, 2304),
            pl.BlockSpec((WINDOW, 256), lambda n: (jnp.maximum(n - 1, 0), 4)),
            _full_spec((SWA_HEADS, WINDOW, 2 * WINDOW)),
            pl.BlockSpec(memory_space=pltpu.SMEM)]


def _swa_fwd(proj, bias, sinks):
    t = proj.shape[0]

    def body(cur_ref, prev_ref, bias_ref, sink_ref, o_ref):
        n = pl.program_id(0)
        kband, vband = _swa_band(cur_ref, prev_ref)
        valid = _swa_valid(n, WINDOW)
        outs = []
        for h in range(SWA_HEADS):
            j = h // SWA_GROUP
            qh = cur_ref[:, h * 64:(h + 1) * 64].astype(BF16)
            p, _ = _swa_probs(qh, kband[:, j * 64:(j + 1) * 64], bias_ref[h], sink_ref[0, h], valid)
            outs.append(_dot(p.astype(BF16), vband[:, j * 64:(j + 1) * 64]))
        o_ref[...] = (jnp.concatenate(outs, axis=1) * _silu(cur_ref[:, 1280:2304])).astype(BF16)

    return pl.pallas_call(body, grid=(t // WINDOW,), in_specs=_swa_in_specs(), out_specs=_row_spec(WINDOW, 1024),
                          out_shape=S((t, 1024), BF16), name="swa_fwd",
                          compiler_params=_cp(("parallel",)))(proj, proj, bias, sinks)


def _swa_bwd(proj, dp, bias, sinks):
    t = proj.shape[0]
    nb = t // WINDOW

    def body(cur_ref, prev_ref, bias_ref, sink_ref, dp_ref, dqz_ref, dkv_ref, dss_ref, dsk_ref):
        n = pl.program_id(0)

        @pl.when(n == 0)
        def _():
            dkv_ref[...] = jnp.zeros_like(dkv_ref)
            dss_ref[...] = jnp.zeros_like(dss_ref)
            dsk_ref[...] = jnp.zeros_like(dsk_ref)

        kband, vband = _swa_band(cur_ref, prev_ref)
        valid = _swa_valid(n)
        z = cur_ref[:, 1280:2304]
        dpv = dp_ref[...].astype(F32)
        do = dpv * _silu(z)
        q = cur_ref[:, 0:1024].astype(BF16)
        dob = do.astype(BF16)
        outs, dqs, dsinks, dk, dv = [], [], [], [], []
        for j in range(SWA_HEADS // SWA_GROUP):
            heads = slice(j * SWA_GROUP, (j + 1) * SWA_GROUP)
            qs, dos = _swa_stack(q, j), _swa_stack(dob, j)
            kh = kband[:, j * 64:(j + 1) * 64]
            vh = vband[:, j * 64:(j + 1) * 64]
            p, ps = _swa_probs(qs, kh, bias_ref[heads].reshape(SWA_ROWS, 2 * WINDOW), _swa_sink_col(sink_ref, j), valid)
            pb = p.astype(BF16)
            outs.append(_dot(pb, vh))
            dpr = _dot_nt(dos, vh)
            dlt = jnp.sum(p * dpr, axis=-1, keepdims=True)
            ds = p * (dpr - dlt)
            dsk = -ps * dlt
            dsinks += [dsk[i * WINDOW:(i + 1) * WINDOW] for i in range(SWA_GROUP)]
            dss_ref[heads] += ds.reshape(SWA_GROUP, WINDOW, 2 * WINDOW)
            dsb = (ds * (HEAD_DIM ** -0.5)).astype(BF16)
            dqs.append(_dot(dsb, kh))
            dk.append(_dot_tn(dsb, qs))
            dv.append(_dot_tn(pb, dos))
        dqz_ref[:, 0:1024] = _swa_unstack(dqs).astype(BF16)
        dqz_ref[:, 1024:2048] = (dpv * _swa_unstack(outs) * _silu_grad(z)).astype(BF16)
        dsk_ref[...] += jnp.concatenate(dsinks, axis=1)
        dband = jnp.concatenate([dk[0], dk[1], dv[0], dv[1]], axis=1)
        prow = pl.multiple_of(jnp.maximum(n - 1, 0) * WINDOW, WINDOW)
        dkv_ref[pl.ds(prow, WINDOW), :] += dband[0:WINDOW]
        crow = pl.multiple_of(n * WINDOW, WINDOW)
        dkv_ref[pl.ds(crow, WINDOW), :] += dband[WINDOW:]

    return pl.pallas_call(
        body, grid=(nb,), in_specs=_swa_in_specs() + [_row_spec(WINDOW, 1024)],
        out_specs=[_row_spec(WINDOW, 2048), _full_spec((t, 256)), _full_spec((SWA_HEADS, WINDOW, 2 * WINDOW)),
                   _full_spec((WINDOW, SWA_HEADS))],
        out_shape=[S((t, 2048), BF16), S((t, 256), F32), S((SWA_HEADS, WINDOW, 2 * WINDOW), F32),
                   S((WINDOW, SWA_HEADS), F32)],
        name="swa_bwd", compiler_params=_cp(("arbitrary",)))(proj, proj, bias, sinks, dp)


MLA_Z0 = MLA_Q_RANK + MLA_KV_RANK + 128
MLA_Z1 = MLA_Z0 + 1024
MLA_PROJ = MLA_Z1 + 128


def _rope_tables(t):
    inv = ROPE_BASE ** (-jnp.arange(0, MLA_ROPE, 2, dtype=F32) / MLA_ROPE)
    ang = jnp.arange(t, dtype=F32)[:, None] * inv[None, :]
    return jnp.tile(jnp.cos(ang), (1, MLA_HEADS)), jnp.tile(jnp.sin(ang), (1, MLA_HEADS))


def _mla_norms(proj, gq, gkv):
    t = proj.shape[0]
    tr = 256

    def body(p_ref, gq_ref, gkv_ref, q_ref, kv_ref):
        for lo, hi, g_ref, o_ref in ((0, 768, gq_ref, q_ref), (768, 1024, gkv_ref, kv_ref)):
            xv = p_ref[:, lo:hi]
            r = lax.rsqrt(jnp.mean(xv * xv, axis=-1, keepdims=True) + EPS)
            o_ref[...] = (xv * r * g_ref[...]).astype(BF16)

    return pl.pallas_call(
        body, grid=(t // tr,), in_specs=[_row_spec(tr, 1024), _full_spec((1, 768)), _full_spec((1, 256))],
        out_specs=[_row_spec(tr, 768), _row_spec(tr, 256)], out_shape=[S((t, 768), BF16), S((t, 256), BF16)],
        name="mla_norms", compiler_params=_cp(("parallel",)))(proj, gq, gkv)


def _mla_norms_bwd(proj, dcqn, dckvn, dkr, dz, gq, gkv):
    t = proj.shape[0]
    tr = 256

    def body(p_ref, dq_ref, dkv_ref, dkr_ref, dz_ref, gq_ref, gkv_ref, dproj_ref, dgq_ref, dgkv_ref):
        @pl.when(pl.program_id(0) == 0)
        def _():
            dgq_ref[...] = jnp.zeros_like(dgq_ref)
            dgkv_ref[...] = jnp.zeros_like(dgkv_ref)

        dx, dg = _rms_bwd_rows(dq_ref[...], p_ref[:, 0:768], gq_ref[...])
        dproj_ref[:, 0:768] = dx.astype(BF16)
        dgq_ref[...] += dg
        dx, dg = _rms_bwd_rows(dkv_ref[...], p_ref[:, 768:1024], gkv_ref[...])
        dproj_ref[:, 768:1024] = dx.astype(BF16)
        dgkv_ref[...] += dg
        dproj_ref[:, 1024:MLA_Z0] = dkr_ref[...].astype(BF16)
        dproj_ref[:, MLA_Z0:MLA_Z1] = dz_ref[...]
        dproj_ref[:, MLA_Z1:MLA_PROJ] = jnp.zeros((tr, MLA_PROJ - MLA_Z1), BF16)

    return pl.pallas_call(
        body, grid=(t // tr,),
        in_specs=[_row_spec(tr, 1024), _row_spec(tr, 768), _row_spec(tr, 256), _row_spec(tr, 128), _row_spec(tr, 1024),
                  _full_spec((1, 768)), _full_spec((1, 256))],
        out_specs=[_row_spec(tr, MLA_PROJ), _full_spec((1, 768)), _full_spec((1, 256))],
        out_shape=[S((t, MLA_PROJ), BF16), S((1, 768), F32), S((1, 256), F32)],
        name="mla_norms_bwd", compiler_params=_cp(("arbitrary",)))(proj, dcqn, dckvn, dkr, dz, gq, gkv)


def _mla_pack(qf, kvf, proj, cos, sin):
    t = qf.shape[0]
    tr = 256

    def body(q_ref, kn_ref, kr_ref, cos_ref, sin_ref, qh_ref, kh_ref):
        cs, sn = cos_ref[...], sin_ref[...]
        x1, x2 = q_ref[:, 1024:1280], q_ref[:, 1280:1536]
        r1 = x1 * cs - x2 * sn
        r2 = x2 * cs + x1 * sn
        c16, s16 = cs[:, 0:16], sn[:, 0:16]
        k1, k2 = kr_ref[:, 0:16], kr_ref[:, 16:32]
        kr1 = k1 * c16 - k2 * s16
        kr2 = k2 * c16 + k1 * s16
        zpad = jnp.zeros((tr, 32), F32)
        for h in range(MLA_HEADS):
            qh_ref[h] = jnp.concatenate([q_ref[:, h * 64:(h + 1) * 64], r1[:, h * 16:(h + 1) * 16],
                                         r2[:, h * 16:(h + 1) * 16], zpad], axis=1).astype(BF16)
            kh_ref[h] = jnp.concatenate([kn_ref[:, h * 64:(h + 1) * 64], kr1, kr2, zpad], axis=1).astype(BF16)

    hspec = pl.BlockSpec((MLA_HEADS, tr, 128), lambda i: (0, i, 0))
    return pl.pallas_call(
        body, grid=(t // tr,),
        in_specs=[_row_spec(tr, 1536), _row_spec(tr, 1024), pl.BlockSpec((tr, 128), lambda i: (i, 8)),
                  _row_spec(tr, 256), _row_spec(tr, 256)],
        out_specs=[hspec, hspec], out_shape=[S((MLA_HEADS, t, 128), BF16)] * 2,
        name="mla_pack", compiler_params=_cp(("parallel",)))(qf, kvf, proj, cos, sin)


def _mla_unpack(dqh, dkh, dv, cos, sin):
    t = dqh.shape[1]
    tr = 256

    def body(dqh_ref, dkh_ref, dv_ref, cos_ref, sin_ref, dq_ref, dkv_ref, dkr_ref):
        cs, sn = cos_ref[...], sin_ref[...]
        dqn = jnp.concatenate([dqh_ref[h, :, 0:64] for h in range(MLA_HEADS)], axis=1)
        dr1 = jnp.concatenate([dqh_ref[h, :, 64:80] for h in range(MLA_HEADS)], axis=1)
        dr2 = jnp.concatenate([dqh_ref[h, :, 80:96] for h in range(MLA_HEADS)], axis=1)
        dq_ref[:, 0:1024] = dqn.astype(BF16)
        dq_ref[:, 1024:1280] = (dr1 * cs + dr2 * sn).astype(BF16)
        dq_ref[:, 1280:1536] = (dr2 * cs - dr1 * sn).astype(BF16)
        dkv_ref[:, 0:1024] = jnp.concatenate([dkh_ref[h, :, 0:64] for h in range(MLA_HEADS)], axis=1).astype(BF16)
        dkv_ref[:, 1024:2048] = dv_ref[...].astype(BF16)
        d1 = dkh_ref[0, :, 64:80]
        d2 = dkh_ref[0, :, 80:96]
        for h in range(1, MLA_HEADS):
            d1 = d1 + dkh_ref[h, :, 64:80]
            d2 = d2 + dkh_ref[h, :, 80:96]
        c16, s16 = cs[:, 0:16], sn[:, 0:16]
        dkr_ref[...] = jnp.concatenate([d1 * c16 + d2 * s16, d2 * c16 - d1 * s16, jnp.zeros((tr, 96), F32)], axis=1)

    hspec = pl.BlockSpec((MLA_HEADS, tr, 128), lambda i: (0, i, 0))
    return pl.pallas_call(
        body, grid=(t // tr,), in_specs=[hspec, hspec, _row_spec(tr, 1024), _row_spec(tr, 256), _row_spec(tr, 256)],
        out_specs=[_row_spec(tr, 1536), _row_spec(tr, 2048), _row_spec(tr, 128)],
        out_shape=[S((t, 1536), BF16), S((t, 2048), BF16), S((t, 128), F32)],
        name="mla_unpack", compiler_params=_cp(("parallel",)))(dqh, dkh, dv, cos, sin)


def _mla_attn_fwd(qh, kh, kvf, proj):
    t = qh.shape[1]
    b, bk = MLA_BQ, MLA_BK

    def body(q_ref, k_ref, v_ref, z_ref, o_ref, p_ref, lse_ref):
        r0 = pl.program_id(1) * b
        nfull = r0 // bk
        diff = (lax.broadcasted_iota(jnp.int32, (b, bk), 1) - lax.broadcasted_iota(jnp.int32, (b, bk), 0))
        qs = [q_ref[0], q_ref[1]]

        def block(c0, carry, masked):
            new = []
            for a in range(2):
                m, l, acc = carry[a]
                k = k_ref[a, pl.ds(c0, bk), :]
                v = v_ref[pl.ds(c0, bk), a * 64:(a + 1) * 64].astype(BF16)
                s = _dot_nt(qs[a], k) * MLA_SCALE
                if masked:
                    s = jnp.where(diff <= r0 - c0, s, NEG_INF)
                mn = jnp.maximum(m, jnp.max(s, axis=-1, keepdims=True))
                alpha = jnp.exp(m - mn)
                p = jnp.exp(s - mn)
                new.append((mn, alpha * l + jnp.sum(p, axis=-1, keepdims=True), alpha * acc + _dot(p.astype(BF16), v)))
            return tuple(new)

        init = (jnp.full((b, 1), NEG_INF, F32), jnp.zeros((b, 1), F32), jnp.zeros((b, 64), F32))
        carry = lax.fori_loop(0, nfull, lambda c, cr: block(pl.multiple_of(c * bk, bk), cr, False), (init, init))
        for i in range(max(1, b // bk)):
            carry = block(pl.multiple_of((nfull + i) * bk, bk), carry, True)
        outs = [acc / l for _, l, acc in carry]
        lses = [jnp.broadcast_to(m + jnp.log(l), (b, 64)) for m, l, _ in carry]
        o = jnp.concatenate(outs, axis=1)
        o_ref[...] = o
        p_ref[...] = (o * _silu(z_ref[...])).astype(BF16)
        lse_ref[0] = jnp.concatenate(lses, axis=1)

    return pl.pallas_call(
        body, grid=(MLA_HEADS // 2, t // b),
        in_specs=[pl.BlockSpec((2, b, 128), lambda i, r: (i, r, 0)), pl.BlockSpec((2, t, 128), lambda i, r: (i, 0, 0)),
                  pl.BlockSpec((t, 128), lambda i, r: (0, 8 + i)), pl.BlockSpec((b, 128), lambda i, r: (r, 9 + i))],
        out_specs=[pl.BlockSpec((b, 128), lambda i, r: (r, i)), pl.BlockSpec((b, 128), lambda i, r: (r, i)),
                   pl.BlockSpec((1, b, 128), lambda i, r: (i, r, 0))],
        out_shape=[S((t, 1024), F32), S((t, 1024), BF16), S((MLA_HEADS // 2, t, 128), F32)],
        name="mla_attn_fwd", compiler_params=_cp(("parallel", "parallel")))(qh, kh, kvf, proj)


def _mla_attn_bwd(qh, kh, kvf, proj, o, dpout, lse):
    t = qh.shape[1]
    b, bk = MLA_BQ, MLA_BK
    nb = t // b

    def body(q_ref, k_ref, v_ref, z_ref, o_ref, dp_ref, lse_ref, dq_ref, dk_ref, dv_ref, dz_ref):
        dk_ref[...] = jnp.zeros_like(dk_ref)
        dv_ref[...] = jnp.zeros_like(dv_ref)
        diff = (lax.broadcasted_iota(jnp.int32, (b, bk), 1) - lax.broadcasted_iota(jnp.int32, (b, bk), 0))
        lanes = [slice(0, 64), slice(64, 128)]

        def rbody(r, carry):
            r0 = pl.multiple_of(r * b, b)
            rows = pl.ds(r0, b)
            qs, dobs, deltas, lses = [], [], [], []
            for a in range(2):
                zz = z_ref[rows, lanes[a]]
                oo = o_ref[rows, lanes[a]]
                dpo = dp_ref[rows, lanes[a]].astype(F32)
                do = dpo * _silu(zz)
                dz_ref[rows, lanes[a]] = (dpo * oo * _silu_grad(zz)).astype(BF16)
                qs.append(q_ref[a, rows, :])
                dobs.append(do.astype(BF16))
                deltas.append(jnp.sum(do * oo, axis=-1, keepdims=True))
                lses.append(lse_ref[0, rows, a * 64:a * 64 + 1])

            def block(c0, dqs, masked):
                cols = pl.ds(c0, bk)
                out = []
                for a in range(2):
                    k = k_ref[a, cols, :]
                    v = v_ref[cols, lanes[a]].astype(BF16)
                    s = _dot_nt(qs[a], k) * MLA_SCALE
                    if masked:
                        s = jnp.where(diff <= r0 - c0, s, NEG_INF)
                    p = jnp.exp(s - lses[a])
                    ds = p * (_dot_nt(dobs[a], v) - deltas[a]) * MLA_SCALE
                    dsb = ds.astype(BF16)
                    dk_ref[a, cols, :] += _dot_tn(dsb, qs[a])
                    dv_ref[cols, lanes[a]] += _dot_tn(p.astype(BF16), dobs[a])
                    out.append(dqs[a] + _dot(dsb, k))
                return tuple(out)

            zero = jnp.zeros((b, 128), F32)
            nfull = r0 // bk
            dqs = lax.fori_loop(0, nfull, lambda c, d: block(pl.multiple_of(c * bk, bk), d, False), (zero, zero))
            for i in range(max(1, b // bk)):
                dqs = block(pl.multiple_of((nfull + i) * bk, bk), dqs, True)
            dq_ref[0, rows, :] = dqs[0]
            dq_ref[1, rows, :] = dqs[1]
            return carry

        lax.fori_loop(0, nb, rbody, 0)

    pair3 = pl.BlockSpec((2, t, 128), lambda i: (i, 0, 0))
    return pl.pallas_call(
        body, grid=(MLA_HEADS // 2,),
        in_specs=[pair3, pair3, pl.BlockSpec((t, 128), lambda i: (0, 8 + i)), pl.BlockSpec((t, 128), lambda i: (0, 9 + i)),
                  pl.BlockSpec((t, 128), lambda i: (0, i)), pl.BlockSpec((t, 128), lambda i: (0, i)),
                  pl.BlockSpec((1, t, 128), lambda i: (i, 0, 0))],
        out_specs=[pair3, pair3, pl.BlockSpec((t, 128), lambda i: (0, i)), pl.BlockSpec((t, 128), lambda i: (0, i))],
        out_shape=[S((MLA_HEADS, t, 128), F32), S((MLA_HEADS, t, 128), F32), S((t, 1024), F32), S((t, 1024), BF16)],
        name="mla_attn_bwd", compiler_params=_cp(("parallel",), VMEM_BIG))(qh, kh, kvf, proj, o, dpout, lse)


def _dot_nt_hi(a, b):
    return lax.dot_general(a, b, (((1,), (1,)), ((), ())), preferred_element_type=F32,
                           precision=lax.Precision.HIGHEST)


SSM_PACK = 2


def _ssm_gen(lr, li, ldt, btr, bti, cr, ci):
    n = SSM_BLOCK
    lanes = SSM_PACK * SSM_STATE
    dt = jnp.exp(ldt)
    x = lr * dt
    w = li * dt
    mag = jnp.exp(x)
    ab_re = mag * jnp.cos(w)
    ab_im = mag * jnp.sin(w)
    den = lr * lr + li * li
    nr = ab_re - 1.0
    f_re = (nr * lr + ab_im * li) / den
    f_im = (ab_im * lr - nr * li) / den
    bb_re = f_re * btr - f_im * bti
    bb_im = f_re * bti + f_im * btr
    inv_mag = jnp.exp(-x)
    inv_re, inv_im = inv_mag * jnp.cos(w), -inv_mag * jnp.sin(w)
    one, zero = jnp.ones_like(x), jnp.zeros_like(x)
    pos, neg = [(one, zero)], [(one, zero)]
    for _ in range(SSM_CHUNK):
        pr, pi = pos[-1]
        pos.append((pr * ab_re - pi * ab_im, pr * ab_im + pi * ab_re))
        pr, pi = neg[-1]
        neg.append((pr * inv_re - pi * inv_im, pr * inv_im + pi * inv_re))

    def stack(table, idx):
        return (jnp.concatenate([jnp.broadcast_to(table[idx(t)][0], (SSM_GROUP, lanes)) for t in range(SSM_CHUNK)], axis=0),
                jnp.concatenate([jnp.broadcast_to(table[idx(t)][1], (SSM_GROUP, lanes)) for t in range(SSM_CHUNK)], axis=0))

    def rep(v):
        return jnp.concatenate([v] * SSM_CHUNK, axis=0)

    ct_re, ct_im, bt_re, bt_im = rep(cr), rep(ci), rep(bb_re), rep(bb_im)
    p_re, p_im = stack(pos, lambda t: t)
    l_re = ct_re * p_re - ct_im * p_im
    l_im = ct_re * p_im + ct_im * p_re
    n_re, n_im = stack(neg, lambda t: t)
    r_re = bt_re * n_re - bt_im * n_im
    r_im = bt_re * n_im + bt_im * n_re
    rk = lax.broadcasted_iota(jnp.int32, (n, n), 0) // SSM_GROUP
    ct = lax.broadcasted_iota(jnp.int32, (n, n), 1) // SSM_GROUP
    mts = []
    for g in range(SSM_PACK):
        sl = slice(g * SSM_STATE, (g + 1) * SSM_STATE)
        mts.append(jnp.where(ct >= rk, _dot_nt_hi(r_re[:, sl], l_re[:, sl]) - _dot_nt_hi(r_im[:, sl], l_im[:, sl]), 0.0))
    e_re, e_im = stack(pos, lambda t: SSM_CHUNK - 1 - t)
    pin_re = bt_re * e_re - bt_im * e_im
    pin_im = bt_re * e_im + bt_im * e_re
    q_re, q_im = stack(pos, lambda t: t + 1)
    qt_re = ct_re * q_re - ct_im * q_im
    qt_im = -(ct_re * q_im + ct_im * q_re)
    ad_re, ad_im = pos[SSM_CHUNK]
    return tuple(mts), pin_re, pin_im, qt_re, qt_im, ad_re, ad_im


_SSM_GEN_OUT = [(SSM_BLOCK, SSM_BLOCK)] + [(SSM_BLOCK, SSM_STATE)] * 4 + [(1, SSM_STATE)] * 2
_SSM_GEN_IN = [(1, SSM_STATE)] * 3 + [(SSM_GROUP, SSM_STATE)] * 4


def _gspec(shape2):
    return pl.BlockSpec((SSM_PACK,) + shape2, lambda g: (g, 0, 0))


def _pack_lanes(ref):
    return jnp.concatenate([ref[g] for g in range(SSM_PACK)], axis=1)


def _unpack_lanes(ref, val):
    for g in range(SSM_PACK):
        ref[g] = val[:, g * SSM_STATE:(g + 1) * SSM_STATE]


def _ssm_operators(params):
    def body(*refs):
        ins, outs = refs[:7], refs[7:]
        vals = _ssm_gen(*[_pack_lanes(r) for r in ins])
        for g in range(SSM_PACK):
            outs[0][g] = vals[0][g]
        for o_ref, val in zip(outs[1:], vals[1:]):
            _unpack_lanes(o_ref, val)

    return pl.pallas_call(
        body, grid=(SSM_GROUPS // SSM_PACK,), in_specs=[_gspec(s) for s in _SSM_GEN_IN],
        out_specs=[_gspec(s) for s in _SSM_GEN_OUT],
        out_shape=[S((SSM_GROUPS,) + s, F32) for s in _SSM_GEN_OUT], name="ssm_operators",
        compiler_params=_cp(("parallel",)))(*params)


def _ssm_operators_bwd(params, cots, after=()):
    def body(*refs):
        ins, cts, outs = refs[:7], refs[7:14], refs[14 + len(after):]
        _, vjp = jax.vjp(_ssm_gen, *[_pack_lanes(r) for r in ins])
        grads = list(vjp((tuple(cts[0][g] for g in range(SSM_PACK)),) + tuple(_pack_lanes(r) for r in cts[1:])))
        for o_ref, val in zip(outs, grads):
            _unpack_lanes(o_ref, val)
        for g in range(SSM_PACK):
            dldt = grads[2][:, g * SSM_STATE:(g + 1) * SSM_STATE]
            outs[2][g] = jnp.broadcast_to(jnp.sum(dldt, axis=-1, keepdims=True), (1, SSM_STATE))

    return pl.pallas_call(
        body, grid=(SSM_GROUPS // SSM_PACK,),
        in_specs=[_gspec(s) for s in _SSM_GEN_IN] + [_gspec(s) for s in _SSM_GEN_OUT]
        + [pl.BlockSpec(memory_space=pl.ANY)] * len(after),
        out_specs=[_gspec(s) for s in _SSM_GEN_IN], out_shape=[S((SSM_GROUPS,) + s, F32) for s in _SSM_GEN_IN],
        name="ssm_operators_bwd", compiler_params=_cp(("parallel",)))(*params, *cots, *after)


def _shift_rows(x, sh, row, up):
    n = x.shape[0]
    if up:
        return jnp.where(row < n - sh, pltpu.roll(x, n - sh, axis=0), 0.0)
    return jnp.where(row >= sh, pltpu.roll(x, sh, axis=0), 0.0)


def _carry_scan(sr, si, ar, ai, row, up):
    n = sr.shape[0]
    sh = 1
    while sh < n:
        tr, ti = _shift_rows(sr, sh, row, up), _shift_rows(si, sh, row, up)
        sr, si = sr + ar * tr - ai * ti, si + ar * ti + ai * tr
        ar, ai = ar * ar - ai * ai, 2.0 * ar * ai
        sh *= 2
    return sr, si


def _ssm_states(u, p_re, p_im, ar, ai, row):
    lre = _dot(u, p_re.astype(BF16))
    lim = _dot(u, p_im.astype(BF16))
    sr, si = _carry_scan(lre, lim, ar, ai, row, False)
    return _shift_rows(sr, 1, row, False), _shift_rows(si, 1, row, False)


SSM_GPB = 128 // SSM_GROUP


def _step_rows(ref, nc):
    return [ref[pl.ds(k, nc, stride=SSM_CHUNK), :].astype(BF16) for k in range(SSM_CHUNK)]


def _group_of(rows, j):
    return jnp.concatenate([r[:, j * SSM_GROUP:(j + 1) * SSM_GROUP] for r in rows], axis=1)


def _store_groups(o_ref, per_group, nc):
    for k in range(SSM_CHUNK):
        o_ref[pl.ds(k, nc, stride=SSM_CHUNK), :] = jnp.concatenate(
            [y[:, k * SSM_GROUP:(k + 1) * SSM_GROUP] for y in per_group], axis=1)


def _bspec(shape2):
    return pl.BlockSpec((SSM_GPB,) + shape2, lambda j: (j, 0, 0))


def _ssm_core_fwd(proj, ops):
    t = proj.shape[0]
    nc = t // SSM_CHUNK

    def body(u_ref, mt_ref, pr_ref, pi_ref, qr_ref, qi_ref, ar_ref, ai_ref, y_ref):
        rows = _step_rows(u_ref, nc)
        row = lax.broadcasted_iota(jnp.int32, (nc, SSM_STATE), 0)
        ys = []
        for j in range(SSM_GPB):
            u = _group_of(rows, j)
            s_re, s_im = _ssm_states(u, pr_ref[j], pi_ref[j], ar_ref[j], ai_ref[j], row)
            ys.append(_dot(u, mt_ref[j].astype(BF16)) + _dot_nt(s_re.astype(BF16), qr_ref[j].astype(BF16))
                      + _dot_nt(s_im.astype(BF16), qi_ref[j].astype(BF16)))
        _store_groups(y_ref, ys, nc)

    blk = pl.BlockSpec((t, 128), lambda j: (0, j))
    return pl.pallas_call(
        body, grid=(SSM_GROUPS // SSM_GPB,), in_specs=[blk] + [_bspec(s) for s in _SSM_GEN_OUT], out_specs=blk,
        out_shape=S((t, 1024), F32), name="ssm_core_fwd", compiler_params=_cp(("parallel",)))(proj, *ops)


def _ssm_core_bwd(proj, dy, ops):
    t = proj.shape[0]
    nc = t // SSM_CHUNK

    def body(u_ref, dy_ref, mt_ref, pr_ref, pi_ref, qr_ref, qi_ref, ar_ref, ai_ref,
             du_ref, dmt_ref, dpr_ref, dpi_ref, dqr_ref, dqi_ref, dar_ref, dai_ref):
        u_rows = _step_rows(u_ref, nc)
        dy_rows = _step_rows(dy_ref, nc)
        row = lax.broadcasted_iota(jnp.int32, (nc, SSM_STATE), 0)
        dus = []
        for j in range(SSM_GPB):
            u = _group_of(u_rows, j)
            dyj = _group_of(dy_rows, j)
            ar, ai = ar_ref[j], ai_ref[j]
            s_re, s_im = _ssm_states(u, pr_ref[j], pi_ref[j], ar, ai, row)
            g_re = _shift_rows(_dot(dyj, qr_ref[j].astype(BF16)), 1, row, True)
            g_im = _shift_rows(_dot(dyj, qi_ref[j].astype(BF16)), 1, row, True)
            l_re, l_im = _carry_scan(g_re, g_im, ar, -ai, row, True)
            lrb, lib = l_re.astype(BF16), l_im.astype(BF16)
            dus.append(_dot_nt(dyj, mt_ref[j].astype(BF16)) + _dot_nt(lrb, pr_ref[j].astype(BF16))
                       + _dot_nt(lib, pi_ref[j].astype(BF16)))
            dmt_ref[j] = _dot_tn(u, dyj)
            dpr_ref[j] = _dot_tn(u, lrb)
            dpi_ref[j] = _dot_tn(u, lib)
            dqr_ref[j] = _dot_tn(dyj, s_re.astype(BF16))
            dqi_ref[j] = _dot_tn(dyj, s_im.astype(BF16))
            dar_ref[j] = jnp.sum(l_re * s_re + l_im * s_im, axis=0, keepdims=True)
            dai_ref[j] = jnp.sum(l_im * s_re - l_re * s_im, axis=0, keepdims=True)
        _store_groups(du_ref, dus, nc)

    blk = pl.BlockSpec((t, 128), lambda j: (0, j))
    return pl.pallas_call(
        body, grid=(SSM_GROUPS // SSM_GPB,), in_specs=[blk, blk] + [_bspec(s) for s in _SSM_GEN_OUT],
        out_specs=[blk] + [_bspec(s) for s in _SSM_GEN_OUT],
        out_shape=[S((t, 1024), F32)] + [S((SSM_GROUPS,) + s, F32) for s in _SSM_GEN_OUT],
        name="ssm_core_bwd", compiler_params=_cp(("parallel",)))(proj, dy, *ops)


def _s5_act(y_core, proj, d_skip):
    t = y_core.shape[0]
    tr = 256

    def body(y_ref, u_ref, d_ref, o_ref):
        o_ref[...] = _gelu(y_ref[...] + d_ref[...] * u_ref[...])

    return pl.pallas_call(body, grid=(t // tr,), in_specs=[_row_spec(tr, 1024), _row_spec(tr, 1024), _full_spec((1, 1024))],
                          out_specs=_row_spec(tr, 1024), out_shape=S((t, 1024), F32), name="s5_act",
                          compiler_params=_cp(("parallel",)))(y_core, proj, d_skip)


def _s5_gate(yg, tg, b_glu, proj):
    t = yg.shape[0]
    tr = 256

    def body(y_ref, t_ref, b_ref, z_ref, o_ref):
        o_ref[...] = (y_ref[...] * jax.nn.sigmoid(t_ref[...] + b_ref[...]) * _silu(z_ref[...])).astype(BF16)

    return pl.pallas_call(
        body, grid=(t // tr,),
        in_specs=[_row_spec(tr, 1024), _row_spec(tr, 1024), _full_spec((1, 1024)), pl.BlockSpec((tr, 1024), lambda i: (i, 1))],
        out_specs=_row_spec(tr, 1024), out_shape=S((t, 1024), BF16), name="s5_gate",
        compiler_params=_cp(("parallel",)))(yg, tg, b_glu, proj)


def _s5_gate_bwd(dp, yg, tg, b_glu, proj):
    t = yg.shape[0]
    tr = 256

    def body(dp_ref, y_ref, t_ref, b_ref, z_ref, dt_ref, dy_ref, dz_ref, db_ref):
        @pl.when(pl.program_id(0) == 0)
        def _():
            db_ref[...] = jnp.zeros_like(db_ref)

        z = z_ref[...]
        yv = y_ref[...]
        sg = jax.nn.sigmoid(t_ref[...] + b_ref[...])
        dpv = dp_ref[...].astype(F32)
        do = dpv * _silu(z)
        dz_ref[...] = (dpv * yv * sg * _silu_grad(z)).astype(BF16)
        dy_ref[...] = do * sg
        dtv = do * yv * sg * (1.0 - sg)
        dt_ref[...] = dtv.astype(BF16)
        db_ref[...] += jnp.sum(dtv, axis=0, keepdims=True)

    return pl.pallas_call(
        body, grid=(t // tr,),
        in_specs=[_row_spec(tr, 1024), _row_spec(tr, 1024), _row_spec(tr, 1024), _full_spec((1, 1024)),
                  pl.BlockSpec((tr, 1024), lambda i: (i, 1))],
        out_specs=[_row_spec(tr, 1024), _row_spec(tr, 1024), _row_spec(tr, 1024), _full_spec((1, 1024))],
        out_shape=[S((t, 1024), BF16), S((t, 1024), F32), S((t, 1024), BF16), S((1, 1024), F32)],
        name="s5_gate_bwd", compiler_params=_cp(("arbitrary",)))(dp, yg, tg, b_glu, proj)


def _s5_act_bwd(dyg1, dyg2, y_core, proj, d_skip):
    t = y_core.shape[0]
    tr = 256

    def body(a_ref, b_ref, y_ref, u_ref, d_ref, dy_ref, du_ref, dd_ref):
        @pl.when(pl.program_id(0) == 0)
        def _():
            dd_ref[...] = jnp.zeros_like(dd_ref)

        u = u_ref[...]
        dv = (a_ref[...] + b_ref[...]) * _gelu_grad(y_ref[...] + d_ref[...] * u)
        dy_ref[...] = dv
        du_ref[...] = dv * d_ref[...]
        dd_ref[...] += jnp.sum(dv * u, axis=0, keepdims=True)

    return pl.pallas_call(
        body, grid=(t // tr,),
        in_specs=[_row_spec(tr, 1024)] * 4 + [_full_spec((1, 1024))],
        out_specs=[_row_spec(tr, 1024), _row_spec(tr, 1024), _full_spec((1, 1024))],
        out_shape=[S((t, 1024), F32), S((t, 1024), F32), S((1, 1024), F32)],
        name="s5_act_bwd", compiler_params=_cp(("arbitrary",)))(dyg1, dyg2, y_core, proj, d_skip)


def _s5_dproj(du_skip, du_core, dz):
    t = du_skip.shape[0]
    tr = 256

    def body(a_ref, b_ref, z_ref, o_ref):
        o_ref[:, 0:1024] = (a_ref[...] + b_ref[...]).astype(BF16)
        o_ref[:, 1024:2048] = z_ref[...]

    return pl.pallas_call(body, grid=(t // tr,), in_specs=[_row_spec(tr, 1024)] * 3, out_specs=_row_spec(tr, 2048),
                          out_shape=S((t, 2048), BF16), name="s5_dproj",
                          compiler_params=_cp(("parallel",)))(du_skip, du_core, dz)


def _s5_params(lam_re, lam_im, log_dt, b_re, b_im, c_re, c_im):
    g = SSM_GROUPS
    return (lam_re.reshape(g, 1, SSM_STATE), lam_im.reshape(g, 1, SSM_STATE),
            jnp.broadcast_to(log_dt.reshape(g, 1, 1), (g, 1, SSM_STATE)),
            b_re.transpose(0, 2, 1), b_im.transpose(0, 2, 1), c_re, c_im)


def _s5_fwd(proj, params, d_skip, w_glu_after, b_glu):
    ops = _ssm_operators(params)
    y_core = _ssm_core_fwd(proj, ops)
    yg = _s5_act(y_core, proj, d_skip)
    tg = _mm(yg, w_glu_after(yg), name="s5_glu")
    return _s5_gate(yg, tg, b_glu, proj), (ops, y_core, yg, tg)


def _s5_bwd(proj, res, dp, params, d_skip, w_glu, b_glu):
    ops, y_core, yg, tg = res
    dtg, dyg1, dz, db_glu = _s5_gate_bwd(dp, yg, tg, b_glu, proj)
    dyg2 = _mm(dtg, w_glu, tb=True, name="s5_glu_dx")
    dw_glu = _mm(yg, dtg, ta=True, out_dtype=BF16, name="s5_glu_dw")
    dy_core, du_skip, dd = _s5_act_bwd(dyg1, dyg2, y_core, proj, d_skip)
    outs = _ssm_core_bwd(proj, dy_core, ops)
    dproj = _s5_dproj(du_skip, outs[0], dz)
    return dproj, outs[1:], dd, dw_glu, db_glu


def _s5_param_grads(params, cots, after=()):
    glr, gli, gdt, gbtr, gbti, gcr, gci = _ssm_operators_bwd(params, cots, after)
    g = SSM_GROUPS
    return (glr.reshape(g, SSM_STATE), gli.reshape(g, SSM_STATE), gdt[:, 0, 0].reshape(1, g), gbtr, gbti, gcr, gci)


def _perm_uq(w):
    w3 = w.reshape(w.shape[0], MLA_HEADS, MLA_NOPE + MLA_ROPE)
    return jnp.concatenate([w3[:, :, :64].reshape(-1, 1024), w3[:, :, 64:80].reshape(-1, 256),
                            w3[:, :, 80:96].reshape(-1, 256)], axis=1)


def _unperm_uq(w):
    r = w.shape[0]
    return jnp.concatenate([w[:, :1024].reshape(r, 16, 64), w[:, 1024:1280].reshape(r, 16, 16),
                            w[:, 1280:1536].reshape(r, 16, 16)], axis=2).reshape(r, 1536)


def _perm_ukv(w):
    w3 = w.reshape(w.shape[0], MLA_HEADS, 128)
    return jnp.concatenate([w3[:, :, :64].reshape(-1, 1024), w3[:, :, 64:].reshape(-1, 1024)], axis=1)


def _unperm_ukv(w):
    r = w.shape[0]
    return jnp.concatenate([w[:, :1024].reshape(r, 16, 64), w[:, 1024:].reshape(r, 16, 64)], axis=2).reshape(r, 2048)


def _mla_fwd(proj, gq, gkv, wuq, wukv, cos, sin):
    cqn, ckvn = _mla_norms(proj, gq, gkv)
    qf = _mm(cqn, wuq, name="mla_uq")
    kvf = _mm(ckvn, wukv, name="mla_ukv")
    qh, kh = _mla_pack(qf, kvf, proj, cos, sin)
    o, pout, lse = _mla_attn_fwd(qh, kh, kvf, proj)
    return pout, (cqn, ckvn, kvf, qh, kh, o, lse)


def _mla_bwd(proj, res, dpout, gq, gkv, wuq, wukv, cos, sin):
    cqn, ckvn, kvf, qh, kh, o, lse = res
    dqh, dkh, dv, dz = _mla_attn_bwd(qh, kh, kvf, proj, o, dpout, lse)
    dqf, dkvf, dkr = _mla_unpack(dqh, dkh, dv, cos, sin)
    dcqn = _mm(dqf, wuq, tb=True, name="mla_uq_dx")
    dwuq = _mm(cqn, dqf, ta=True, out_dtype=BF16, name="mla_uq_dw")
    dckvn = _mm(dkvf, wukv, tb=True, name="mla_ukv_dx")
    dwukv = _mm(ckvn, dkvf, ta=True, out_dtype=BF16, name="mla_ukv_dw")
    dproj, dgq, dgkv = _mla_norms_bwd(proj, dcqn, dckvn, dkr, dz, gq, gkv)
    return dproj, dgq, dgkv, dwuq, dwukv


_ANY = pl.BlockSpec(memory_space=pl.ANY)


def _chip_peers():
    x, y, c = lax.axis_index("x"), lax.axis_index("y"), lax.axis_index("c")
    return 2 * x + y, c, [(1 - x, y), (x, 1 - y), (1 - x, 1 - y)]


def _sibling_swap(ps, name):
    n = len(ps)

    def body(*refs):
        p_refs, q_refs, send_sems, recv_sems = refs[:n], refs[n:2 * n], refs[2 * n], refs[2 * n + 1]
        x, y, c = lax.axis_index("x"), lax.axis_index("y"), lax.axis_index("c")
        cps = [pltpu.make_async_remote_copy(src_ref=p_refs[i], dst_ref=q_refs[i], send_sem=send_sems.at[i],
                                            recv_sem=recv_sems.at[i], device_id=(x, y, 1 - c), device_id_type=MESH)
               for i in range(n)]
        for cp in cps:
            cp.start()
        for cp in cps:
            cp.wait()

    return pl.pallas_call(body, in_specs=[_ANY] * n, out_specs=[_ANY] * n, out_shape=[S(p.shape, p.dtype) for p in ps],
                          scratch_shapes=[pltpu.SemaphoreType.DMA((n,)), pltpu.SemaphoreType.DMA((n,))], name=name)(*ps)


def _quarter_spec(tr, r, c, dim):
    if dim is None:
        return pl.BlockSpec((1, tr, c), lambda i, me: (me[0], i, 0))
    if dim == 0:
        return pl.BlockSpec((tr, c), lambda i, me: (me[0] * (r // tr) + i, 0))
    return pl.BlockSpec((tr, c), lambda i, me: (i, me[0]))


def _whole_shape(r, c, dim):
    return (N_CHIPS, r, c) if dim is None else ((N_CHIPS * r, c) if dim == 0 else (r, N_CHIPS * c))


def _row_tile(r):
    for d in range(min(r, 256), 7, -1):
        if r % d == 0 and d % 8 == 0:
            return d
    return r


def _cast_place(me, x, dim, dtype, name):
    r, c = x.shape
    tr = _row_tile(r)

    def body(me_ref, x_ref, o_ref):
        o_ref[...] = x_ref[...].astype(dtype).reshape(o_ref.shape)

    return pl.pallas_call(
        body, out_shape=S(_whole_shape(r, c, dim), dtype), name=name,
        grid_spec=pltpu.PrefetchScalarGridSpec(num_scalar_prefetch=1, grid=(r // tr,),
                                               in_specs=[pl.BlockSpec((tr, c), lambda i, me: (i, 0))],
                                               out_specs=_quarter_spec(tr, r, c, dim)),
        compiler_params=_cp(("parallel",)))(me, x)


def _sum_parts(me, g, land, dim, name):
    _, r, c = land.shape
    tr = _row_tile(r)

    def body(me_ref, g_ref, l_ref, o_ref):
        own = g_ref[...].astype(F32).reshape(tr, c)
        o_ref[...] = ((own + l_ref[0].astype(F32)) + l_ref[1].astype(F32)) + l_ref[2].astype(F32)

    return pl.pallas_call(
        body, out_shape=S((r, c), F32), name=name,
        grid_spec=pltpu.PrefetchScalarGridSpec(
            num_scalar_prefetch=1, grid=(r // tr,),
            in_specs=[_quarter_spec(tr, r, c, dim), pl.BlockSpec((3, tr, c), lambda i, me: (0, i, 0))],
            out_specs=pl.BlockSpec((tr, c), lambda i, me: (i, 0))),
        compiler_params=_cp(("parallel",)))(me, g, land)


_HBM = pl.BlockSpec(memory_space=pltpu.HBM)
_SEM = pl.BlockSpec(memory_space=pltpu.SEMAPHORE)
_EFFECT = pltpu.SideEffectType.DATAFLOW_SIDE_EFFECTING


def _quarter(ref, s, dim):
    n = ref.shape[dim] // N_CHIPS
    start = pl.multiple_of(s * n, n)
    return ref.at[pl.ds(start, n), :] if dim == 0 else ref.at[:, pl.ds(start, n)]


def _part(ref, s, dim):
    return ref.at[s] if dim is None else _quarter(ref, s, dim)


def _gather_plan(i, dim):
    return dict(src=i, dst=i, send_src=lambda ref, me, peer, k: _part(ref, me, dim),
                send_dst=lambda ref, me, peer, k: _part(ref, me, dim), recv_dst=lambda ref, me, peer, k: _part(ref, peer, dim))


def _scatter_plan(i, j, dim):
    return dict(src=i, dst=j, send_src=lambda ref, me, peer, k: _part(ref, peer, dim),
                send_dst=lambda ref, me, peer, k: ref.at[k], recv_dst=lambda ref, me, peer, k: ref.at[k])


def _hbm(a):
    return pltpu.with_memory_space_constraint(a, pltpu.HBM)


def _remote_copies(refs, send_sems, recv_sems, plans):
    me, c, peers = _chip_peers()
    sends, recvs = [], []
    for i, p in enumerate(plans):
        for k, (px, py) in enumerate(peers):
            peer = 2 * px + py
            sems = dict(send_sem=send_sems.at[3 * i + k], recv_sem=recv_sems.at[3 * i + k], device_id=(px, py, c),
                        device_id_type=MESH)
            src = p['send_src'](refs[p['src']], me, peer, k)
            sends.append(pltpu.make_async_remote_copy(src_ref=src, dst_ref=p['send_dst'](refs[p['dst']], me, peer, k), **sems))
            recvs.append(pltpu.make_async_remote_copy(src_ref=src, dst_ref=p['recv_dst'](refs[p['dst']], me, peer, k), **sems))
    return sends, recvs


def _exchange_start(arrays, plans, name, after=()):
    n, m, extra = len(arrays), len(plans), len(after)

    def body(*refs):
        send_sems, recv_sems, token = refs[n + extra], refs[n + extra + 1], refs[-1]
        sends, _ = _remote_copies(refs[:n], send_sems, recv_sems, plans)
        for cp in sends:
            cp.start()
        token[...] = jnp.zeros_like(token)

    outs = pl.pallas_call(
        body, name=name,
        out_shape=(pltpu.SemaphoreType.DMA((3 * m,)), pltpu.SemaphoreType.DMA((3 * m,)),
                   *[pltpu.HBM(a.shape, a.dtype) for a in arrays], S((8, 128), F32)),
        in_specs=[_HBM] * n + [_ANY] * extra, out_specs=(_SEM, _SEM, *[_HBM] * n, pl.BlockSpec(memory_space=pltpu.VMEM)),
        input_output_aliases={i: 2 + i for i in range(n)},
        compiler_params=pltpu.CompilerParams(has_side_effects=_EFFECT))(*[_hbm(a) for a in arrays], *after)
    return outs[0], outs[1], outs[2:2 + n], outs[-1]


def _exchange_wait(started, plans, after, name):
    send_sems, recv_sems, arrays, _ = started
    n = len(arrays)

    def body(*refs):
        sends, recvs = _remote_copies(refs[:n], refs[n], refs[n + 1], plans)
        for cp in sends:
            cp.wait_send()
        for cp in recvs:
            cp.wait_recv()

    return pl.pallas_call(
        body, name=name, out_shape=tuple(pltpu.HBM(a.shape, a.dtype) for a in arrays),
        in_specs=[_HBM] * n + [_SEM, _SEM, _ANY], out_specs=tuple([_HBM] * n),
        input_output_aliases={i: i for i in range(n)},
        compiler_params=pltpu.CompilerParams(has_side_effects=_EFFECT))(*arrays, send_sems, recv_sems, after)


def _add2(p, q):
    r, c = p.shape
    tr = 128 if r % 128 == 0 else r

    def body(p_ref, q_ref, o_ref):
        o_ref[...] = p_ref[...] + q_ref[...]

    return pl.pallas_call(body, grid=(r // tr,), in_specs=[_row_spec(tr, c), _row_spec(tr, c)], out_specs=_row_spec(tr, c),
                          out_shape=S((r, c), F32), name="add_cores", compiler_params=_cp(("parallel",)))(p, q)


def _adamw(w, g, m, v, name, g2=None, after=()):
    r, c = w.shape
    tr = _row_tile(r)
    c1 = 1.0 / (1.0 - ADAM_B1 ** ADAM_STEP)
    c2 = 1.0 / (1.0 - ADAM_B2 ** ADAM_STEP)
    gs = [g] if g2 is None else [g, g2]

    def body(*refs):
        w_ref, m_ref, v_ref = refs[0], refs[1], refs[2]
        g_refs = refs[3:3 + len(gs)]
        go_ref, d_ref, nm_ref, nv_ref = refs[3 + len(gs) + len(after):]
        gv = g_refs[0][...]
        if len(gs) == 2:
            gv = gv + g_refs[1][...]
        go_ref[...] = gv
        nm = ADAM_B1 * m_ref[...] + (1.0 - ADAM_B1) * gv
        nv = ADAM_B2 * v_ref[...] + (1.0 - ADAM_B2) * (gv * gv)
        nm_ref[...] = nm
        nv_ref[...] = nv
        d_ref[...] = -ADAM_LR * ((nm * c1) / (jnp.sqrt(nv * c2) + ADAM_EPS) + ADAM_WD * w_ref[...])

    return pl.pallas_call(body, grid=(r // tr,),
                          in_specs=[_row_spec(tr, c)] * (3 + len(gs)) + [pl.BlockSpec(memory_space=pl.ANY)] * len(after),
                          out_specs=[_row_spec(tr, c)] * 4, out_shape=[S((r, c), F32)] * 4, name=name,
                          compiler_params=_cp(("parallel",)))(w, m, v, *gs, *after)


_WEIGHTS = ['pre_norm', 'post_norm', 'rel_bias', 'a_w_in', 'a_lam_re', 'a_lam_im', 'a_log_dt', 'a_b_re', 'a_b_im',
            'a_c_re', 'a_c_im', 'a_d', 'a_w_glu', 'a_b_glu', 'a_w_out', 'b_w_in', 'b_sinks', 'b_w_out', 'c_w_in',
            'c_q_norm', 'c_kv_norm', 'c_w_uq', 'c_w_ukv', 'c_w_out', 'd_w_in', 'd_ln_g', 'd_ln_b', 'd_w_s', 'd_b_s',
            'd_w_out']
_BIG = [('a_w_in', 1024, 2048, 1), ('a_w_glu', 1024, 1024, 0), ('a_w_out', 1024, 1024, 0), ('b_w_in', 1024, 2304, 1),
        ('b_w_out', 1024, 1024, 0), ('c_w_in', 1024, 2080, 1), ('c_w_uq', 768, 1536, 1), ('c_w_ukv', 256, 2048, 1),
        ('c_w_out', 1024, 1024, 0), ('d_w_in', 1024, 3072, 1), ('d_w_out', 1024, 1024, 0)]
_SHARDED_VECS = [('c_q_norm', 768), ('c_kv_norm', 256), ('d_ln_g', 1024), ('d_ln_b', 1024)]
_TRANSPOSED = ('b_w_in', 'c_w_in')
_REPLICATED = [n for n in _WEIGHTS if n not in [b[0] for b in _BIG] and n not in [s[0] for s in _SHARDED_VECS]]
_LANES = 1024
_SMALL_Q_ROWS = 136


def kernel(x, pre_norm, post_norm, rel_bias, a_w_in, a_lam_re, a_lam_im, a_log_dt, a_b_re, a_b_im, a_c_re, a_c_im, a_d, a_w_glu, a_b_glu, a_w_out, b_w_in, b_sinks, b_w_out, c_w_in, c_q_norm, c_kv_norm, c_w_uq, c_w_ukv, c_w_out, d_w_in, d_ln_g, d_ln_b, d_w_s, d_b_s, d_w_out, loss_target, m_pre_norm, m_post_norm, m_rel_bias, m_a_w_in, m_a_lam_re, m_a_lam_im, m_a_log_dt, m_a_b_re, m_a_b_im, m_a_c_re, m_a_c_im, m_a_d, m_a_w_glu, m_a_b_glu, m_a_w_out, m_b_w_in, m_b_sinks, m_b_w_out, m_c_w_in, m_c_q_norm, m_c_kv_norm, m_c_w_uq, m_c_w_ukv, m_c_w_out, m_d_w_in, m_d_ln_g, m_d_ln_b, m_d_w_s, m_d_b_s, m_d_w_out, v_pre_norm, v_post_norm, v_rel_bias, v_a_w_in, v_a_lam_re, v_a_lam_im, v_a_log_dt, v_a_b_re, v_a_b_im, v_a_c_re, v_a_c_im, v_a_d, v_a_w_glu, v_a_b_glu, v_a_w_out, v_b_w_in, v_b_sinks, v_b_w_out, v_c_w_in, v_c_q_norm, v_c_kv_norm, v_c_w_uq, v_c_w_ukv, v_c_w_out, v_d_w_in, v_d_ln_g, v_d_ln_b, v_d_w_s, v_d_b_s, v_d_w_out):
    arg = dict(locals())
    me = 2 * lax.axis_index("x") + lax.axis_index("y")
    xin = x[0]
    t = xin.shape[0]

    big = {n: (r, c, dim) for n, r, c, dim in _BIG}

    me1 = me.reshape(1).astype(jnp.int32)

    def gather_start(names, tag, after):
        wholes, plans = [], []
        for i, n in enumerate(names):
            if n in big:
                r, c, dim = big[n]
                src, dtype = arg[n][0], BF16
                if n in _TRANSPOSED:
                    src, dim = src.T, 0
                    if (c // N_CHIPS) % 16:
                        dtype = F32
                elif dim == 1 and (c // N_CHIPS) % 128:
                    dim = None
            else:
                src, dim, dtype = arg[n], None, F32
            wholes.append(_cast_place(me1, src, dim, dtype, "place_" + n))
            plans.append(_gather_plan(i, dim))
        return _exchange_start(wholes, plans, "gather_start_" + tag, after=after), plans

    def gather_wait(names, started, after, tag):
        st, plans = started
        out = {}
        for n, a in zip(names, _exchange_wait(st, plans, after, "gather_wait_" + tag)):
            if n in big:
                r, c, dim = big[n]
                out[n] = a if a.ndim == 2 else a.transpose(1, 0, 2).reshape(r, c)
            else:
                out[n] = a.reshape(1, -1)
        return out

    groups = {'a1': ['a_w_in'], 'a2': ['a_w_glu', 'a_w_out'], 'b': ['b_w_in', 'b_w_out'],
              'c': ['c_w_in', 'c_w_uq', 'c_w_ukv', 'c_w_out', 'c_q_norm', 'c_kv_norm'],
              'd': ['d_w_in', 'd_w_out', 'd_ln_g', 'd_ln_b']}
    started, tokens = {}, []
    for tag, names in groups.items():
        started[tag] = gather_start(names, tag, tokens)
        tokens = [started[tag][0][3]]
    wt = gather_wait(groups['a1'], started['a1'], xin, 'a1')
    cos, sin = _rope_tables(t)
    bucket = jnp.asarray(_t5_bucket_table())
    s5p = _s5_params(a_lam_re[0], a_lam_im[0], a_log_dt[0], a_b_re[0], a_b_im[0], a_c_re[0], a_c_im[0])
    bst = d_b_s[0].T

    x0 = xin
    h0 = _prenorm(x0, pre_norm[0:1], "pre0", after=tokens)
    proj0 = _mm(h0, wt['a_w_in'], name="a_in")

    def w_glu_after(x):
        wt.update(gather_wait(groups['a2'], started['a2'], x, 'a2'))
        return wt['a_w_glu']

    p0, res0 = _s5_fwd(proj0, s5p, a_d, w_glu_after, a_b_glu)
    y0 = _mm(p0, wt['a_w_out'], name="a_out")
    x1, h1 = _postnorm_prenorm(x0, y0, post_norm[0:1], pre_norm[1:2], "post0_pre1")

    wt.update(gather_wait(groups['b'], started['b'], x1, 'b'))
    proj1 = _mm(h1, wt['b_w_in'], tb=True, name="b_in")
    bias = _swa_bias(rel_bias, bucket)
    p1 = _swa_fwd(proj1, bias, b_sinks)
    y1 = _mm(p1, wt['b_w_out'], name="b_out")
    x2, h2 = _postnorm_prenorm(x1, y1, post_norm[1:2], pre_norm[2:3], "post1_pre2")

    wt.update(gather_wait(groups['c'], started['c'], x2, 'c'))
    w_c_in = jnp.concatenate([wt['c_w_in'][:1056], jnp.zeros((96, 1024), F32), wt['c_w_in'][1056:],
                              jnp.zeros((MLA_PROJ - MLA_Z1, 1024), F32)], axis=0).astype(BF16)
    w_uq = _perm_uq(wt['c_w_uq'])
    w_ukv = _perm_ukv(wt['c_w_ukv'])
    proj2 = _mm(h2, w_c_in, tb=True, name="c_in")
    p2, res2 = _mla_fwd(proj2, wt['c_q_norm'], wt['c_kv_norm'], w_uq, w_ukv, cos, sin)
    y2 = _mm(p2, wt['c_w_out'], name="c_out")
    x3, h3 = _postnorm_prenorm(x2, y2, post_norm[2:3], pre_norm[3:4], "post2_pre3")

    wt.update(gather_wait(groups['d'], started['d'], x3, 'd'))
    vec = wt
    proj3 = _mm(h3, wt['d_w_in'], name="d_in")
    p3 = _sgu_fwd(proj3, vec['d_ln_g'], vec['d_ln_b'], d_w_s[0], bst)
    y3 = _mm(p3, wt['d_w_out'], name="d_out")
    loss_part, dx = _postnorm_loss(x3, y3, post_norm[3:4], loss_target[0])
    loss = lax.psum(loss_part[0, 0], ("x", "y", "c"))

    gr = {}
    d_pre, d_post = [None] * 4, [None] * 4
    scattered = []

    def scatter_start(items, tag):
        srcs = [a for _, a, _ in items]
        n = len(items)
        plans = [_scatter_plan(i, n + i, dim) for i, (_, _, dim) in enumerate(items)]
        lands = []
        for _, a, dim in items:
            part = a.shape[1:] if dim is None else tuple(s // N_CHIPS if i == dim else s for i, s in enumerate(a.shape))
            lands.append(lax.empty((3,) + part, a.dtype))
        st = _exchange_start(srcs + lands, plans, "scatter_start_" + tag)
        scattered.append((st, plans, items, tag))
        return [st[3]]

    dy, d_post[3] = _postnorm_bwd(dx, y3, post_norm[3:4], "post3_bwd")
    dp = _mm(dy, wt['d_w_out'], tb=True, name="d_out_dx")
    g_out = _mm(p3, dy, ta=True, out_dtype=BF16, name="d_out_dw")
    dproj, gr['d_w_s'], dbst, gr['d_ln_g'], gr['d_ln_b'] = _sgu_bwd(proj3, dp, vec['d_ln_g'], vec['d_ln_b'], d_w_s[0], bst)
    gr['d_b_s'] = dbst.T
    dh = _mm(dproj, wt['d_w_in'], tb=True, name="d_in_dx")
    g_in = _mm(h3, dproj, ta=True, out_dtype=BF16, name="d_in_dw")
    tok = scatter_start([('d_w_in', g_in, 1), ('d_w_out', g_out, 0)], 'd')
    dx, d_pre[3], dy, d_post[2] = _norm_bwd_pair(dx, dh, x3, pre_norm[3:4], y2, post_norm[2:3], "pre3_post2_bwd", after=tok)

    dp = _mm(dy, wt['c_w_out'], tb=True, name="c_out_dx")
    g_out = _mm(p2, dy, ta=True, out_dtype=BF16, name="c_out_dw")
    dproj, gr['c_q_norm'], gr['c_kv_norm'], dwuq, dwukv = _mla_bwd(proj2, res2, dp, wt['c_q_norm'], wt['c_kv_norm'],
                                                                  w_uq, w_ukv, cos, sin)
    dh = _mm(dproj, w_c_in, name="c_in_dx")
    dwc = _mm(dproj, h2, ta=True, name="c_in_dw")
    g_in = jnp.concatenate([dwc[:1056], dwc[MLA_Z0:MLA_Z1]], axis=0)
    tok = scatter_start([('c_w_in', g_in, 0), ('c_w_uq', _unperm_uq(dwuq), 1), ('c_w_ukv', _unperm_ukv(dwukv), 1),
                         ('c_w_out', g_out, 0)], 'c')
    dx, d_pre[2], dy, d_post[1] = _norm_bwd_pair(dx, dh, x2, pre_norm[2:3], y1, post_norm[1:2], "pre2_post1_bwd", after=tok)

    dp = _mm(dy, wt['b_w_out'], tb=True, name="b_out_dx")
    g_out = _mm(p1, dy, ta=True, out_dtype=BF16, name="b_out_dw")
    dqz, dkv, dss, dsk = _swa_bwd(proj1, dp, bias, b_sinks)
    gr['rel_bias'], gr['b_sinks'] = _swa_dbias(dss, dsk, bucket)
    dproj = jnp.concatenate([dqz[:, :1024], dkv.astype(BF16), dqz[:, 1024:]], axis=1)
    dh = _mm(dproj, wt['b_w_in'], name="b_in_dx")
    g_in = _mm(dproj, h1, ta=True, out_dtype=BF16, name="b_in_dw")
    tok = scatter_start([('b_w_in', g_in, 0), ('b_w_out', g_out, 0)], 'b')
    dx, d_pre[1], dy, d_post[0] = _norm_bwd_pair(dx, dh, x1, pre_norm[1:2], y0, post_norm[0:1], "pre1_post0_bwd", after=tok)

    dp = _mm(dy, wt['a_w_out'], tb=True, name="a_out_dx")
    g_out = _mm(p0, dy, ta=True, out_dtype=BF16, name="a_out_dw")
    dproj, s5cots, gr['a_d'], g_glu, gr['a_b_glu'] = _s5_bwd(proj0, res0, dp, s5p, a_d, wt['a_w_glu'], a_b_glu)
    tok = scatter_start([('a_w_glu', g_glu, 0), ('a_w_out', g_out, 0)], 'a2')
    g_in = _mm(h0, dproj, ta=True, out_dtype=BF16, name="a_in_dw", after=tok)
    tok = scatter_start([('a_w_in', g_in, 1)], 'a1')
    dh = _mm(dproj, wt['a_w_in'], tb=True, name="a_in_dx", after=tok)
    dx, d_pre[0] = _prenorm_bwd(dx, dh, x0, pre_norm[0:1], "pre0_bwd")
    s5g = _s5_param_grads(s5p, s5cots, after=tok)
    for n, g in zip(['a_lam_re', 'a_lam_im', 'a_log_dt', 'a_b_re', 'a_b_im', 'a_c_re', 'a_c_im'], s5g):
        gr[n] = g
    gr['pre_norm'] = jnp.concatenate(d_pre, axis=0)
    gr['post_norm'] = jnp.concatenate(d_post, axis=0)
    small_names = _REPLICATED + [n for n, _ in _SHARDED_VECS]
    small = jnp.concatenate([gr[n].reshape(-1) for n in small_names])
    small = jnp.pad(small, (0, N_CHIPS * _SMALL_Q_ROWS * _LANES - small.shape[0])).reshape(N_CHIPS, _SMALL_Q_ROWS, _LANES)
    tok = scatter_start([('small', small, None)], 'a0')

    partial = {}
    for st, plans, items, tag in scattered[:-1]:
        done = _exchange_wait(st, plans, tok[0], "scatter_wait_" + tag)
        for i, (n, _, dim) in enumerate(items):
            partial[n] = _sum_parts(me1, done[i], done[len(items) + i], dim, "sum_parts_" + n)
    big_names = [n for n, _, _, _ in _BIG]
    other = dict(zip(big_names, _sibling_swap([partial[n] for n in big_names], "swap_cores")))
    st, plans, _, tag = scattered[-1]
    done = _exchange_wait(st, plans, other[big_names[-1]], "scatter_wait_" + tag)
    p_small = _sum_parts(me1, done[0], done[1], None, "sum_parts_small")
    small_q = _add2(p_small, _sibling_swap([p_small], "swap_cores_small")[0])
    gplan = [_gather_plan(0, None)]
    gathering = _exchange_start([_cast_place(me1, small_q, None, F32, "place_small_grads")], gplan, "gather_small_start")

    delta, new_m, new_v, grads = {}, {}, {}, {}
    for n in big_names:
        shp = arg[n].shape
        view = (lambda a: a[0].T) if n in _TRANSPOSED else (lambda a: a[0])
        back = (lambda o: o.T.reshape(shp)) if n in _TRANSPOSED else (lambda o: o.reshape(shp))
        outs = _adamw(view(arg[n]), partial[n], view(arg['m_' + n]), view(arg['v_' + n]), "adamw_" + n, g2=other[n],
                      after=[gathering[3]])
        grads[n], delta[n], new_m[n], new_v[n] = [back(o) for o in outs]
    small_all = _exchange_wait(gathering, gplan, new_v[big_names[-1]], "gather_small_wait")[0].reshape(-1)
    def view(n, a):
        return a[0].transpose(0, 2, 1) if n in ('a_b_re', 'a_b_im') else a

    def back(n, v):
        return v.transpose(0, 2, 1)[None] if n in ('a_b_re', 'a_b_im') else v

    off = 0
    gview = {}
    for n in _REPLICATED:
        shp = view(n, arg[n]).shape
        sz = math.prod(shp)
        gview[n] = small_all[off:off + sz].reshape(shp)
        off += sz
    for n, sz in _SHARDED_VECS:
        q = sz // N_CHIPS
        gview[n] = lax.dynamic_slice(small_all, (off + me * q,), (q,)).reshape(arg[n].shape)
        off += sz

    for n in small_names:
        shp = gview[n].shape
        two = (math.prod(shp[:-1]), shp[-1])
        outs = _adamw(view(n, arg[n]).reshape(two), gview[n].reshape(two), view(n, arg['m_' + n]).reshape(two),
                      view(n, arg['v_' + n]).reshape(two), "adamw_" + n)
        grads[n] = back(n, gview[n])
        delta[n], new_m[n], new_v[n] = [back(n, o.reshape(shp)) for o in outs[1:]]

    return (loss, dx.reshape(x.shape), *[grads[n] for n in _WEIGHTS], *[delta[n] for n in _WEIGHTS],
            *[new_m[n] for n in _WEIGHTS], *[new_v[n] for n in _WEIGHTS])
```

```python
import functools
import math

import numpy as np
import jax
import jax.numpy as jnp
from jax import lax
from jax.experimental import pallas as pl
from jax.experimental.pallas import tpu as pltpu

F32 = jnp.float32
BF16 = jnp.bfloat16
S = jax.ShapeDtypeStruct
MESH = pl.DeviceIdType.MESH

D_MODEL = 1024
EPS = 1e-6
NEG_INF = -1e30
N_CHIPS = 4

SSM_GROUPS = 64
SSM_GROUP = 16
SSM_STATE = 64
SSM_CHUNK = 16
SSM_BLOCK = SSM_CHUNK * SSM_GROUP

HEAD_DIM = 64
SWA_HEADS = 16
SWA_GROUP = 8
WINDOW = 128
REL_BUCKETS = 32
REL_MAX_DIST = 128

MLA_HEADS = 16
MLA_NOPE = 64
MLA_ROPE = 32
MLA_Q_RANK = 768
MLA_KV_RANK = 256
MLA_SCALE = (MLA_NOPE + MLA_ROPE) ** -0.5
MLA_BQ = 512
MLA_BK = 512
ROPE_BASE = 10000.0

SGU_CHUNK = 128
SGU_GROUPS = 16
SGU_GDIM = 64

ADAM_LR = 0.001
ADAM_B1 = 0.9
ADAM_B2 = 0.999
ADAM_EPS = 1e-08
ADAM_WD = 0.01
ADAM_STEP = 10

VMEM_BIG = 56 * 1024 * 1024


def _cp(sem, vmem=None):
    return pltpu.CompilerParams(dimension_semantics=sem, vmem_limit_bytes=vmem)


def _row_spec(tr, c):
    return pl.BlockSpec((tr, c), lambda i: (i, 0))


def _full_spec(shape):
    nd = len(shape)
    return pl.BlockSpec(shape, lambda *_: (0,) * nd)


def _gelu(x):
    c = 0.7978845608028654
    return 0.5 * x * (1.0 + jnp.tanh(c * (x + 0.044715 * x * x * x)))


def _gelu_grad(x):
    c = 0.7978845608028654
    t = jnp.tanh(c * (x + 0.044715 * x * x * x))
    return 0.5 * (1.0 + t) + 0.5 * x * (1.0 - t * t) * c * (1.0 + 3.0 * 0.044715 * x * x)


def _silu(z):
    return z * jax.nn.sigmoid(z)


def _silu_grad(z):
    s = jax.nn.sigmoid(z)
    return s * (1.0 + z * (1.0 - s))


def _dot(a, b):
    return lax.dot_general(a, b, (((1,), (0,)), ((), ())), preferred_element_type=F32)


def _dot_nt(a, b):
    return lax.dot_general(a, b, (((1,), (1,)), ((), ())), preferred_element_type=F32)


def _dot_tn(a, b):
    return lax.dot_general(a, b, (((0,), (0,)), ((), ())), preferred_element_type=F32)


def _pick(n, cap):
    best = 0
    for d in range(128, min(n, cap) + 1, 128):
        if n % d == 0:
            best = d
    if best < 256:
        return n
    return best


def _mm(a, b, *, ta=False, tb=False, out_dtype=F32, name, after=()):
    m, k = (a.shape[1], a.shape[0]) if ta else a.shape
    n = b.shape[0] if tb else b.shape[1]
    assert (b.shape[1] if tb else b.shape[0]) == k
    tm, tn, tk = _pick(m, 1024), _pick(n, 1024), _pick(k, 1024)
    if tn > 1024:
        tm = _pick(m, 256)
    if tk > 1024:
        tm, tn = _pick(m, 256), _pick(n, 256)
    nk = k // tk
    dims = (((0 if ta else 1,), (1 if tb else 0,)), ((), ()))

    def body(a_ref, b_ref, *rest):
        o_ref, acc_ref = rest[-2], rest[-1]
        kk = pl.program_id(2)

        @pl.when(kk == 0)
        def _():
            acc_ref[...] = jnp.zeros_like(acc_ref)

        acc_ref[...] += lax.dot_general(a_ref[...].astype(BF16), b_ref[...].astype(BF16), dims,
                                        preferred_element_type=F32)

        @pl.when(kk == nk - 1)
        def _():
            o_ref[...] = acc_ref[...].astype(out_dtype)

    a_spec = pl.BlockSpec((tk, tm), lambda i, j, kk: (kk, i)) if ta else pl.BlockSpec((tm, tk), lambda i, j, kk: (i, kk))
    b_spec = pl.BlockSpec((tn, tk), lambda i, j, kk: (j, kk)) if tb else pl.BlockSpec((tk, tn), lambda i, j, kk: (kk, j))
    return pl.pallas_call(
        body, grid=(m // tm, n // tn, nk), in_specs=[a_spec, b_spec] + [pl.BlockSpec(memory_space=pl.ANY)] * len(after),
        out_specs=pl.BlockSpec((tm, tn), lambda i, j, kk: (i, j)), out_shape=S((m, n), out_dtype),
        scratch_shapes=[pltpu.VMEM((tm, tn), F32)], name=name,
        compiler_params=_cp(("parallel", "parallel", "arbitrary"), VMEM_BIG))(a, b, *after)


def _prenorm(x, g, name, after=()):
    t, d = x.shape
    tr = 256

    def body(x_ref, g_ref, *rest):
        o_ref = rest[-1]
        xv = x_ref[...]
        r = lax.rsqrt(jnp.mean(xv * xv, axis=-1, keepdims=True) + EPS)
        o_ref[...] = (xv * r * g_ref[...]).astype(BF16)

    return pl.pallas_call(body, grid=(t // tr,),
                          in_specs=[_row_spec(tr, d), _full_spec((1, d))] + [pl.BlockSpec(memory_space=pl.ANY)] * len(after),
                          out_specs=_row_spec(tr, d), out_shape=S((t, d), BF16), name=name,
                          compiler_params=_cp(("parallel",)))(x, g, *after)


def _rms_bwd_rows(dout, xin, g):
    r = lax.rsqrt(jnp.mean(xin * xin, axis=-1, keepdims=True) + EPS)
    xh = xin * r
    gd = dout * g
    dx = r * (gd - xh * jnp.mean(gd * xh, axis=-1, keepdims=True))
    return dx, jnp.sum(dout * xh, axis=0, keepdims=True)


def _postnorm_bwd(dxn, y, g, name, after=()):
    t, d = y.shape
    tr = 256

    def body(dx_ref, y_ref, g_ref, *rest):
        dy_ref, dg_ref = rest[-2], rest[-1]

        @pl.when(pl.program_id(0) == 0)
        def _():
            dg_ref[...] = jnp.zeros_like(dg_ref)

        dy, dg = _rms_bwd_rows(dx_ref[...], y_ref[...], g_ref[...])
        dy_ref[...] = dy.astype(BF16)
        dg_ref[...] += dg

    return pl.pallas_call(body, grid=(t // tr,),
                          in_specs=[_row_spec(tr, d), _row_spec(tr, d), _full_spec((1, d))]
                          + [pl.BlockSpec(memory_space=pl.ANY)] * len(after),
                          out_specs=[_row_spec(tr, d), _full_spec((1, d))],
                          out_shape=[S((t, d), BF16), S((1, d), F32)], name=name,
                          compiler_params=_cp(("arbitrary",)))(dxn, y, g, *after)


def _prenorm_bwd(dxn, dh, x, g, name):
    t, d = x.shape
    tr = 256

    def body(dxn_ref, dh_ref, x_ref, g_ref, dx_ref, dg_ref):
        @pl.when(pl.program_id(0) == 0)
        def _():
            dg_ref[...] = jnp.zeros_like(dg_ref)

        dx, dg = _rms_bwd_rows(dh_ref[...], x_ref[...], g_ref[...])
        dx_ref[...] = dxn_ref[...] + dx
        dg_ref[...] += dg

    return pl.pallas_call(body, grid=(t // tr,),
                          in_specs=[_row_spec(tr, d), _row_spec(tr, d), _row_spec(tr, d), _full_spec((1, d))],
                          out_specs=[_row_spec(tr, d), _full_spec((1, d))],
                          out_shape=[S((t, d), F32), S((1, d), F32)], name=name,
                          compiler_params=_cp(("arbitrary",)))(dxn, dh, x, g)


def _postnorm_prenorm(x, y, gpost, gpre, name):
    t, d = x.shape
    tr = 256

    def body(x_ref, y_ref, gp_ref, gn_ref, xn_ref, h_ref):
        yv = y_ref[...]
        r = lax.rsqrt(jnp.mean(yv * yv, axis=-1, keepdims=True) + EPS)
        xn = x_ref[...] + yv * r * gp_ref[...]
        xn_ref[...] = xn
        r2 = lax.rsqrt(jnp.mean(xn * xn, axis=-1, keepdims=True) + EPS)
        h_ref[...] = (xn * r2 * gn_ref[...]).astype(BF16)

    return pl.pallas_call(body, grid=(t // tr,),
                          in_specs=[_row_spec(tr, d), _row_spec(tr, d), _full_spec((1, d)), _full_spec((1, d))],
                          out_specs=[_row_spec(tr, d), _row_spec(tr, d)], out_shape=[S((t, d), F32), S((t, d), BF16)],
                          name=name, compiler_params=_cp(("parallel",)))(x, y, gpost, gpre)


def _norm_bwd_pair(dxn, dh, x, gpre, yprev, gpost, name, after=()):
    t, d = x.shape
    tr = 256

    def body(dxn_ref, dh_ref, x_ref, gn_ref, y_ref, gp_ref, *rest):
        dx_ref, dgn_ref, dy_ref, dgp_ref = rest[-4:]

        @pl.when(pl.program_id(0) == 0)
        def _():
            dgn_ref[...] = jnp.zeros_like(dgn_ref)
            dgp_ref[...] = jnp.zeros_like(dgp_ref)

        dxa, dgn = _rms_bwd_rows(dh_ref[...], x_ref[...], gn_ref[...])
        dx = dxn_ref[...] + dxa
        dx_ref[...] = dx
        dgn_ref[...] += dgn
        dy, dgp = _rms_bwd_rows(dx, y_ref[...], gp_ref[...])
        dy_ref[...] = dy.astype(BF16)
        dgp_ref[...] += dgp

    rows, vec = _row_spec(tr, d), _full_spec((1, d))
    return pl.pallas_call(body, grid=(t // tr,),
                          in_specs=[rows, rows, rows, vec, rows, vec] + [pl.BlockSpec(memory_space=pl.ANY)] * len(after),
                          out_specs=[rows, vec, rows, vec],
                          out_shape=[S((t, d), F32), S((1, d), F32), S((t, d), BF16), S((1, d), F32)], name=name,
                          compiler_params=_cp(("arbitrary",)))(dxn, dh, x, gpre, yprev, gpost, *after)


def _postnorm_loss(x, y, gpost, target):
    t, d = x.shape
    tr = 256

    def body(x_ref, y_ref, g_ref, t_ref, l_ref, dy_ref):
        @pl.when(pl.program_id(0) == 0)
        def _():
            l_ref[...] = jnp.zeros_like(l_ref)

        yv = y_ref[...]
        r = lax.rsqrt(jnp.mean(yv * yv, axis=-1, keepdims=True) + EPS)
        e = (x_ref[...] + yv * r * g_ref[...]) - t_ref[...]
        dy_ref[...] = e * (1.0 / d)
        l_ref[...] += 0.5 * jnp.sum(jnp.mean(e * e, axis=-1, keepdims=True), axis=0, keepdims=True)

    return pl.pallas_call(body, grid=(t // tr,),
                          in_specs=[_row_spec(tr, d), _row_spec(tr, d), _full_spec((1, d)), _row_spec(tr, d)],
                          out_specs=[_full_spec((1, 1)), _row_spec(tr, d)],
                          out_shape=[S((1, 1), F32), S((t, d), F32)], name="postnorm_loss",
                          compiler_params=_cp(("arbitrary",)))(x, y, gpost, target)


def _sgu_parts(proj, lg, lb, ws_ref, bst):
    a = proj[:, 0:1024]
    bb = proj[:, 1024:2048]
    z = proj[:, 2048:3072]
    u = _gelu(a)
    vp = _gelu(bb)
    mu = jnp.mean(vp, axis=-1, keepdims=True)
    xc = vp - mu
    rstd = lax.rsqrt(jnp.mean(xc * xc, axis=-1, keepdims=True) + EPS)
    xh = xc * rstd
    vn = xh * lg + lb
    row = lax.broadcasted_iota(jnp.int32, (SGU_CHUNK, SGU_CHUNK), 0)
    col = lax.broadcasted_iota(jnp.int32, (SGU_CHUNK, SGU_CHUNK), 1)
    tril = row >= col
    ws = [jnp.where(tril, ws_ref[g], 0.0).astype(BF16) for g in range(SGU_GROUPS)]
    s = jnp.concatenate(
        [_dot(ws[g], vn[:, g * 64:(g + 1) * 64].astype(BF16)) + bst[:, g:g + 1] for g in range(SGU_GROUPS)], axis=1)
    return a, bb, z, u, vp, rstd, xh, vn, ws, tril, s


def _sgu_fwd(proj, lg, lb, w_s, bst):
    t = proj.shape[0]
    tr = SGU_CHUNK

    def body(p_ref, lg_ref, lb_ref, ws_ref, bst_ref, o_ref):
        _, _, z, u, _, _, _, _, _, _, s = _sgu_parts(p_ref[...], lg_ref[...], lb_ref[...], ws_ref, bst_ref[...])
        o_ref[...] = (u * s * _silu(z)).astype(BF16)

    return pl.pallas_call(
        body, grid=(t // tr,),
        in_specs=[_row_spec(tr, 3072), _full_spec((1, 1024)), _full_spec((1, 1024)), _full_spec((16, 128, 128)),
                  _full_spec((128, 16))],
        out_specs=_row_spec(tr, 1024), out_shape=S((t, 1024), BF16), name="sgu_fwd",
        compiler_params=_cp(("parallel",)))(proj, lg, lb, w_s, bst)


def _sgu_bwd(proj, dp, lg, lb, w_s, bst):
    t = proj.shape[0]
    tr = SGU_CHUNK

    def body(p_ref, dp_ref, lg_ref, lb_ref, ws_ref, bst_ref, dproj_ref, dws_ref, dbst_ref, dlg_ref, dlb_ref):
        @pl.when(pl.program_id(0) == 0)
        def _():
            dws_ref[...] = jnp.zeros_like(dws_ref)
            dbst_ref[...] = jnp.zeros_like(dbst_ref)
            dlg_ref[...] = jnp.zeros_like(dlg_ref)
            dlb_ref[...] = jnp.zeros_like(dlb_ref)

        lgv = lg_ref[...]
        a, bb, z, u, vp, rstd, xh, vn, ws, tril, s = _sgu_parts(p_ref[...], lgv, lb_ref[...], ws_ref, bst_ref[...])
        dpv = dp_ref[...].astype(F32)
        sz = _silu(z)
        du = dpv * s * sz
        ds = dpv * u * sz
        dz = dpv * u * s * _silu_grad(z)
        dvn_parts, dbs_cols = [], []
        for g in range(SGU_GROUPS):
            dsg = ds[:, g * 64:(g + 1) * 64]
            dsg_b = dsg.astype(BF16)
            dvn_parts.append(_dot_tn(ws[g], dsg_b))
            dws_ref[g] += jnp.where(tril, _dot_nt(dsg_b, vn[:, g * 64:(g + 1) * 64].astype(BF16)), 0.0)
            dbs_cols.append(jnp.sum(dsg, axis=1, keepdims=True))
        dvn = jnp.concatenate(dvn_parts, axis=1)
        dbst_ref[...] += jnp.concatenate(dbs_cols, axis=1)
        dlg_ref[...] += jnp.sum(dvn * xh, axis=0, keepdims=True)
        dlb_ref[...] += jnp.sum(dvn, axis=0, keepdims=True)
        dxh = dvn * lgv
        dvp = rstd * (dxh - jnp.mean(dxh, axis=-1, keepdims=True) - xh * jnp.mean(dxh * xh, axis=-1, keepdims=True))
        dproj_ref[:, 0:1024] = (du * _gelu_grad(a)).astype(BF16)
        dproj_ref[:, 1024:2048] = (dvp * _gelu_grad(bb)).astype(BF16)
        dproj_ref[:, 2048:3072] = dz.astype(BF16)

    return pl.pallas_call(
        body, grid=(t // tr,),
        in_specs=[_row_spec(tr, 3072), _row_spec(tr, 1024), _full_spec((1, 1024)), _full_spec((1, 1024)),
                  _full_spec((16, 128, 128)), _full_spec((128, 16))],
        out_specs=[_row_spec(tr, 3072), _full_spec((16, 128, 128)), _full_spec((128, 16)), _full_spec((1, 1024)),
                   _full_spec((1, 1024))],
        out_shape=[S((t, 3072), BF16), S((16, 128, 128), F32), S((128, 16), F32), S((1, 1024), F32),
                   S((1, 1024), F32)],
        name="sgu_bwd", compiler_params=_cp(("arbitrary",)))(proj, dp, lg, lb, w_s, bst)


def _t5_bucket_table():
    qi = np.arange(WINDOW)[:, None]
    kj = np.arange(2 * WINDOW)[None, :]
    dist = np.maximum(qi + WINDOW - kj, 0)
    max_exact = REL_BUCKETS // 2
    dist_f = np.maximum(dist, 1).astype(np.float32)
    large = max_exact + (np.log(dist_f / np.float32(max_exact)) / np.float32(math.log(REL_MAX_DIST / max_exact))
                         * np.float32(REL_BUCKETS - max_exact)).astype(np.int32)
    large = np.minimum(large, REL_BUCKETS - 1)
    return np.where(dist < max_exact, dist, large).astype(np.int32)


def _swa_bias(rel_bias, bucket):
    def body(rb_ref, bk_ref, o_ref):
        bk = bk_ref[...]
        for h in range(SWA_HEADS):
            acc = jnp.zeros((WINDOW, 2 * WINDOW), F32)
            for b in range(REL_BUCKETS):
                acc = jnp.where(bk == b, rb_ref[b, h], acc)
            o_ref[h] = acc

    return pl.pallas_call(
        body, in_specs=[pl.BlockSpec(memory_space=pltpu.SMEM), pl.BlockSpec(memory_space=pltpu.VMEM)],
        out_specs=pl.BlockSpec(memory_space=pltpu.VMEM), out_shape=S((SWA_HEADS, WINDOW, 2 * WINDOW), F32),
        name="swa_bias")(rel_bias, bucket)


def _swa_dbias(ds_sum, dsink_rows, bucket):
    def body(ds_ref, dsk_ref, bk_ref, drb_ref, dsink_ref):
        bk = bk_ref[...]
        r = lax.broadcasted_iota(jnp.int32, (REL_BUCKETS, SWA_HEADS), 0)
        c = lax.broadcasted_iota(jnp.int32, (REL_BUCKETS, SWA_HEADS), 1)
        out = jnp.zeros((REL_BUCKETS, SWA_HEADS), F32)
        for b in range(REL_BUCKETS):
            m = bk == b
            for h in range(SWA_HEADS):
                val = jnp.sum(jnp.where(m, ds_ref[h], 0.0))
                out = jnp.where((r == b) & (c == h), val, out)
        drb_ref[...] = out
        dsink_ref[...] = jnp.sum(dsk_ref[...], axis=0, keepdims=True)

    return pl.pallas_call(
        body, out_shape=[S((REL_BUCKETS, SWA_HEADS), F32), S((1, SWA_HEADS), F32)], name="swa_dbias",
    )(ds_sum, dsink_rows, bucket)


def _swa_band(cur_ref, prev_ref):
    kband = jnp.concatenate([prev_ref[:, 0:128], cur_ref[:, 1024:1152]], axis=0).astype(BF16)
    vband = jnp.concatenate([prev_ref[:, 128:256], cur_ref[:, 1152:1280]], axis=0).astype(BF16)
    return kband, vband


SWA_ROWS = SWA_GROUP * WINDOW


def _swa_valid(n, rows=SWA_ROWS):
    qi = lax.broadcasted_iota(jnp.int32, (rows, 2 * WINDOW), 0) & (WINDOW - 1)
    kj = lax.broadcasted_iota(jnp.int32, (rows, 2 * WINDOW), 1)
    dist = qi + WINDOW - kj
    return (dist >= 0) & (dist < WINDOW) & (n * WINDOW + kj - WINDOW >= 0)


def _swa_stack(x, j):
    return jnp.concatenate([x[:, h * 64:(h + 1) * 64] for h in range(j * SWA_GROUP, (j + 1) * SWA_GROUP)], axis=0)


def _swa_unstack(parts):
    return jnp.concatenate([p[i * WINDOW:(i + 1) * WINDOW] for p in parts for i in range(SWA_GROUP)], axis=1)


def _swa_sink_col(sink_ref, j):
    return jnp.concatenate([jnp.full((WINDOW, 1), sink_ref[0, h], F32)
                            for h in range(j * SWA_GROUP, (j + 1) * SWA_GROUP)], axis=0)


def _swa_probs(qh, kh, bias_h, sink, valid):
    s = _dot_nt(qh, kh) * (HEAD_DIM ** -0.5) + bias_h
    s = jnp.where(valid, s, NEG_INF)
    m = jnp.maximum(jnp.max(s, axis=-1, keepdims=True), sink)
    e = jnp.exp(s - m)
    es = jnp.exp(sink - m)
    inv = 1.0 / (jnp.sum(e, axis=-1, keepdims=True) + es)
    return e * inv, es * inv


def _swa_in_specs():
    return [_row_spec(WINDOW, 2304),
            pl.BlockSpec((WINDOW, 256), lambda n: (jnp.maximum(n - 1, 0), 4)),
            _full_spec((SWA_HEADS, WINDOW, 2 * WINDOW)),
            pl.BlockSpec(memory_space=pltpu.SMEM)]


def _swa_fwd(proj, bias, sinks):
    t = proj.shape[0]

    def body(cur_ref, prev_ref, bias_ref, sink_ref, o_ref):
        n = pl.program_id(0)
        kband, vband = _swa_band(cur_ref, prev_ref)
        valid = _swa_valid(n, WINDOW)
        outs = []
        for h in range(SWA_HEADS):
            j = h // SWA_GROUP
            qh = cur_ref[:, h * 64:(h + 1) * 64].astype(BF16)
            p, _ = _swa_probs(qh, kband[:, j * 64:(j + 1) * 64], bias_ref[h], sink_ref[0, h], valid)
            outs.append(_dot(p.astype(BF16), vband[:, j * 64:(j + 1) * 64]))
        o_ref[...] = (jnp.concatenate(outs, axis=1) * _silu(cur_ref[:, 1280:2304])).astype(BF16)

    return pl.pallas_call(body, grid=(t // WINDOW,), in_specs=_swa_in_specs(), out_specs=_row_spec(WINDOW, 1024),
                          out_shape=S((t, 1024), BF16), name="swa_fwd",
                          compiler_params=_cp(("parallel",)))(proj, proj, bias, sinks)


def _swa_bwd(proj, dp, bias, sinks):
    t = proj.shape[0]
    nb = t // WINDOW

    def body(cur_ref, prev_ref, bias_ref, sink_ref, dp_ref, dqz_ref, dkv_ref, dss_ref, dsk_ref):
        n = pl.program_id(0)

        @pl.when(n == 0)
        def _():
            dkv_ref[...] = jnp.zeros_like(dkv_ref)
            dss_ref[...] = jnp.zeros_like(dss_ref)
            dsk_ref[...] = jnp.zeros_like(dsk_ref)

        kband, vband = _swa_band(cur_ref, prev_ref)
        valid = _swa_valid(n)
        z = cur_ref[:, 1280:2304]
        dpv = dp_ref[...].astype(F32)
        do = dpv * _silu(z)
        q = cur_ref[:, 0:1024].astype(BF16)
        dob = do.astype(BF16)
        outs, dqs, dsinks, dk, dv = [], [], [], [], []
        for j in range(SWA_HEADS // SWA_GROUP):
            heads = slice(j * SWA_GROUP, (j + 1) * SWA_GROUP)
            qs, dos = _swa_stack(q, j), _swa_stack(dob, j)
            kh = kband[:, j * 64:(j + 1) * 64]
            vh = vband[:, j * 64:(j + 1) * 64]
            p, ps = _swa_probs(qs, kh, bias_ref[heads].reshape(SWA_ROWS, 2 * WINDOW), _swa_sink_col(sink_ref, j), valid)
            pb = p.astype(BF16)
            outs.append(_dot(pb, vh))
            dpr = _dot_nt(dos, vh)
            dlt = jnp.sum(p * dpr, axis=-1, keepdims=True)
            ds = p * (dpr - dlt)
            dsk = -ps * dlt
            dsinks += [dsk[i * WINDOW:(i + 1) * WINDOW] for i in range(SWA_GROUP)]
            dss_ref[heads] += ds.reshape(SWA_GROUP, WINDOW, 2 * WINDOW)
            dsb = (ds * (HEAD_DIM ** -0.5)).astype(BF16)
            dqs.append(_dot(dsb, kh))
            dk.append(_dot_tn(dsb, qs))
            dv.append(_dot_tn(pb, dos))
        dqz_ref[:, 0:1024] = _swa_unstack(dqs).astype(BF16)
        dqz_ref[:, 1024:2048] = (dpv * _swa_unstack(outs) * _silu_grad(z)).astype(BF16)
        dsk_ref[...] += jnp.concatenate(dsinks, axis=1)
        dband = jnp.concatenate([dk[0], dk[1], dv[0], dv[1]], axis=1)
        prow = pl.multiple_of(jnp.maximum(n - 1, 0) * WINDOW, WINDOW)
        dkv_ref[pl.ds(prow, WINDOW), :] += dband[0:WINDOW]
        crow = pl.multiple_of(n * WINDOW, WINDOW)
        dkv_ref[pl.ds(crow, WINDOW), :] += dband[WINDOW:]

    return pl.pallas_call(
        body, grid=(nb,), in_specs=_swa_in_specs() + [_row_spec(WINDOW, 1024)],
        out_specs=[_row_spec(WINDOW, 2048), _full_spec((t, 256)), _full_spec((SWA_HEADS, WINDOW, 2 * WINDOW)),
                   _full_spec((WINDOW, SWA_HEADS))],
        out_shape=[S((t, 2048), BF16), S((t, 256), F32), S((SWA_HEADS, WINDOW, 2 * WINDOW), F32),
                   S((WINDOW, SWA_HEADS), F32)],
        name="swa_bwd", compiler_params=_cp(("arbitrary",)))(proj, proj, bias, sinks, dp)


MLA_Z0 = MLA_Q_RANK + MLA_KV_RANK + 128
MLA_Z1 = MLA_Z0 + 1024
MLA_PROJ = MLA_Z1 + 128


def _rope_tables(t):
    inv = ROPE_BASE ** (-jnp.arange(0, MLA_ROPE, 2, dtype=F32) / MLA_ROPE)
    ang = jnp.arange(t, dtype=F32)[:, None] * inv[None, :]
    return jnp.tile(jnp.cos(ang), (1, MLA_HEADS)), jnp.tile(jnp.sin(ang), (1, MLA_HEADS))


def _mla_norms(proj, gq, gkv):
    t = proj.shape[0]
    tr = 256

    def body(p_ref, gq_ref, gkv_ref, q_ref, kv_ref):
        for lo, hi, g_ref, o_ref in ((0, 768, gq_ref, q_ref), (768, 1024, gkv_ref, kv_ref)):
            xv = p_ref[:, lo:hi]
            r = lax.rsqrt(jnp.mean(xv * xv, axis=-1, keepdims=True) + EPS)
            o_ref[...] = (xv * r * g_ref[...]).astype(BF16)

    return pl.pallas_call(
        body, grid=(t // tr,), in_specs=[_row_spec(tr, 1024), _full_spec((1, 768)), _full_spec((1, 256))],
        out_specs=[_row_spec(tr, 768), _row_spec(tr, 256)], out_shape=[S((t, 768), BF16), S((t, 256), BF16)],
        name="mla_norms", compiler_params=_cp(("parallel",)))(proj, gq, gkv)


def _mla_norms_bwd(proj, dcqn, dckvn, dkr, dz, gq, gkv):
    t = proj.shape[0]
    tr = 256

    def body(p_ref, dq_ref, dkv_ref, dkr_ref, dz_ref, gq_ref, gkv_ref, dproj_ref, dgq_ref, dgkv_ref):
        @pl.when(pl.program_id(0) == 0)
        def _():
            dgq_ref[...] = jnp.zeros_like(dgq_ref)
            dgkv_ref[...] = jnp.zeros_like(dgkv_ref)

        dx, dg = _rms_bwd_rows(dq_ref[...], p_ref[:, 0:768], gq_ref[...])
        dproj_ref[:, 0:768] = dx.astype(BF16)
        dgq_ref[...] += dg
        dx, dg = _rms_bwd_rows(dkv_ref[...], p_ref[:, 768:1024], gkv_ref[...])
        dproj_ref[:, 768:1024] = dx.astype(BF16)
        dgkv_ref[...] += dg
        dproj_ref[:, 1024:MLA_Z0] = dkr_ref[...].astype(BF16)
        dproj_ref[:, MLA_Z0:MLA_Z1] = dz_ref[...]
        dproj_ref[:, MLA_Z1:MLA_PROJ] = jnp.zeros((tr, MLA_PROJ - MLA_Z1), BF16)

    return pl.pallas_call(
        body, grid=(t // tr,),
        in_specs=[_row_spec(tr, 1024), _row_spec(tr, 768), _row_spec(tr, 256), _row_spec(tr, 128), _row_spec(tr, 1024),
                  _full_spec((1, 768)), _full_spec((1, 256))],
        out_specs=[_row_spec(tr, MLA_PROJ), _full_spec((1, 768)), _full_spec((1, 256))],
        out_shape=[S((t, MLA_PROJ), BF16), S((1, 768), F32), S((1, 256), F32)],
        name="mla_norms_bwd", compiler_params=_cp(("arbitrary",)))(proj, dcqn, dckvn, dkr, dz, gq, gkv)


def _mla_pack(qf, kvf, proj, cos, sin):
    t = qf.shape[0]
    tr = 256

    def body(q_ref, kn_ref, kr_ref, cos_ref, sin_ref, qh_ref, kh_ref):
        cs, sn = cos_ref[...], sin_ref[...]
        x1, x2 = q_ref[:, 1024:1280], q_ref[:, 1280:1536]
        r1 = x1 * cs - x2 * sn
        r2 = x2 * cs + x1 * sn
        c16, s16 = cs[:, 0:16], sn[:, 0:16]
        k1, k2 = kr_ref[:, 0:16], kr_ref[:, 16:32]
        kr1 = k1 * c16 - k2 * s16
        kr2 = k2 * c16 + k1 * s16
        zpad = jnp.zeros((tr, 32), F32)
        for h in range(MLA_HEADS):
            qh_ref[h] = jnp.concatenate([q_ref[:, h * 64:(h + 1) * 64], r1[:, h * 16:(h + 1) * 16],
                                         r2[:, h * 16:(h + 1) * 16], zpad], axis=1).astype(BF16)
            kh_ref[h] = jnp.concatenate([kn_ref[:, h * 64:(h + 1) * 64], kr1, kr2, zpad], axis=1).astype(BF16)

    hspec = pl.BlockSpec((MLA_HEADS, tr, 128), lambda i: (0, i, 0))
    return pl.pallas_call(
        body, grid=(t // tr,),
        in_specs=[_row_spec(tr, 1536), _row_spec(tr, 1024), pl.BlockSpec((tr, 128), lambda i: (i, 8)),
                  _row_spec(tr, 256), _row_spec(tr, 256)],
        out_specs=[hspec, hspec], out_shape=[S((MLA_HEADS, t, 128), BF16)] * 2,
        name="mla_pack", compiler_params=_cp(("parallel",)))(qf, kvf, proj, cos, sin)


def _mla_unpack(dqh, dkh, dv, cos, sin):
    t = dqh.shape[1]
    tr = 256

    def body(dqh_ref, dkh_ref, dv_ref, cos_ref, sin_ref, dq_ref, dkv_ref, dkr_ref):
        cs, sn = cos_ref[...], sin_ref[...]
        dqn = jnp.concatenate([dqh_ref[h, :, 0:64] for h in range(MLA_HEADS)], axis=1)
        dr1 = jnp.concatenate([dqh_ref[h, :, 64:80] for h in range(MLA_HEADS)], axis=1)
        dr2 = jnp.concatenate([dqh_ref[h, :, 80:96] for h in range(MLA_HEADS)], axis=1)
        dq_ref[:, 0:1024] = dqn.astype(BF16)
        dq_ref[:, 1024:1280] = (dr1 * cs + dr2 * sn).astype(BF16)
        dq_ref[:, 1280:1536] = (dr2 * cs - dr1 * sn).astype(BF16)
        dkv_ref[:, 0:1024] = jnp.concatenate([dkh_ref[h, :, 0:64] for h in range(MLA_HEADS)], axis=1).astype(BF16)
        dkv_ref[:, 1024:2048] = dv_ref[...].astype(BF16)
        d1 = dkh_ref[0, :, 64:80]
        d2 = dkh_ref[0, :, 80:96]
        for h in range(1, MLA_HEADS):
            d1 = d1 + dkh_ref[h, :, 64:80]
            d2 = d2 + dkh_ref[h, :, 80:96]
        c16, s16 = cs[:, 0:16], sn[:, 0:16]
        dkr_ref[...] = jnp.concatenate([d1 * c16 + d2 * s16, d2 * c16 - d1 * s16, jnp.zeros((tr, 96), F32)], axis=1)

    hspec = pl.BlockSpec((MLA_HEADS, tr, 128), lambda i: (0, i, 0))
    return pl.pallas_call(
        body, grid=(t // tr,), in_specs=[hspec, hspec, _row_spec(tr, 1024), _row_spec(tr, 256), _row_spec(tr, 256)],
        out_specs=[_row_spec(tr, 1536), _row_spec(tr, 2048), _row_spec(tr, 128)],
        out_shape=[S((t, 1536), BF16), S((t, 2048), BF16), S((t, 128), F32)],
        name="mla_unpack", compiler_params=_cp(("parallel",)))(dqh, dkh, dv, cos, sin)


def _mla_attn_fwd(qh, kh, kvf, proj):
    t = qh.shape[1]
    b, bk = MLA_BQ, MLA_BK

    def body(q_ref, k_ref, v_ref, z_ref, o_ref, p_ref, lse_ref):
        r0 = pl.program_id(1) * b
        nfull = r0 // bk
        diff = (lax.broadcasted_iota(jnp.int32, (b, bk), 1) - lax.broadcasted_iota(jnp.int32, (b, bk), 0))
        qs = [q_ref[0], q_ref[1]]

        def block(c0, carry, masked):
            new = []
            for a in range(2):
                m, l, acc = carry[a]
                k = k_ref[a, pl.ds(c0, bk), :]
                v = v_ref[pl.ds(c0, bk), a * 64:(a + 1) * 64].astype(BF16)
                s = _dot_nt(qs[a], k) * MLA_SCALE
                if masked:
                    s = jnp.where(diff <= r0 - c0, s, NEG_INF)
                mn = jnp.maximum(m, jnp.max(s, axis=-1, keepdims=True))
                alpha = jnp.exp(m - mn)
                p = jnp.exp(s - mn)
                new.append((mn, alpha * l + jnp.sum(p, axis=-1, keepdims=True), alpha * acc + _dot(p.astype(BF16), v)))
            return tuple(new)

        init = (jnp.full((b, 1), NEG_INF, F32), jnp.zeros((b, 1), F32), jnp.zeros((b, 64), F32))
        carry = lax.fori_loop(0, nfull, lambda c, cr: block(pl.multiple_of(c * bk, bk), cr, False), (init, init))
        for i in range(max(1, b // bk)):
            carry = block(pl.multiple_of((nfull + i) * bk, bk), carry, True)
        outs = [acc / l for _, l, acc in carry]
        lses = [jnp.broadcast_to(m + jnp.log(l), (b, 64)) for m, l, _ in carry]
        o = jnp.concatenate(outs, axis=1)
        o_ref[...] = o
        p_ref[...] = (o * _silu(z_ref[...])).astype(BF16)
        lse_ref[0] = jnp.concatenate(lses, axis=1)

    return pl.pallas_call(
        body, grid=(MLA_HEADS // 2, t // b),
        in_specs=[pl.BlockSpec((2, b, 128), lambda i, r: (i, r, 0)), pl.BlockSpec((2, t, 128), lambda i, r: (i, 0, 0)),
                  pl.BlockSpec((t, 128), lambda i, r: (0, 8 + i)), pl.BlockSpec((b, 128), lambda i, r: (r, 9 + i))],
        out_specs=[pl.BlockSpec((b, 128), lambda i, r: (r, i)), pl.BlockSpec((b, 128), lambda i, r: (r, i)),
                   pl.BlockSpec((1, b, 128), lambda i, r: (i, r, 0))],
        out_shape=[S((t, 1024), F32), S((t, 1024), BF16), S((MLA_HEADS // 2, t, 128), F32)],
        name="mla_attn_fwd", compiler_params=_cp(("parallel", "parallel")))(qh, kh, kvf, proj)


def _mla_attn_bwd(qh, kh, kvf, proj, o, dpout, lse):
    t = qh.shape[1]
    b, bk = MLA_BQ, MLA_BK
    nb = t // b

    def body(q_ref, k_ref, v_ref, z_ref, o_ref, dp_ref, lse_ref, dq_ref, dk_ref, dv_ref, dz_ref):
        dk_ref[...] = jnp.zeros_like(dk_ref)
        dv_ref[...] = jnp.zeros_like(dv_ref)
        diff = (lax.broadcasted_iota(jnp.int32, (b, bk), 1) - lax.broadcasted_iota(jnp.int32, (b, bk), 0))
        lanes = [slice(0, 64), slice(64, 128)]

        def rbody(r, carry):
            r0 = pl.multiple_of(r * b, b)
            rows = pl.ds(r0, b)
            qs, dobs, deltas, lses = [], [], [], []
            for a in range(2):
                zz = z_ref[rows, lanes[a]]
                oo = o_ref[rows, lanes[a]]
                dpo = dp_ref[rows, lanes[a]].astype(F32)
                do = dpo * _silu(zz)
                dz_ref[rows, lanes[a]] = (dpo * oo * _silu_grad(zz)).astype(BF16)
                qs.append(q_ref[a, rows, :])
                dobs.append(do.astype(BF16))
                deltas.append(jnp.sum(do * oo, axis=-1, keepdims=True))
                lses.append(lse_ref[0, rows, a * 64:a * 64 + 1])

            def block(c0, dqs, masked):
                cols = pl.ds(c0, bk)
                out = []
                for a in range(2):
                    k = k_ref[a, cols, :]
                    v = v_ref[cols, lanes[a]].astype(BF16)
                    s = _dot_nt(qs[a], k) * MLA_SCALE
                    if masked:
                        s = jnp.where(diff <= r0 - c0, s, NEG_INF)
                    p = jnp.exp(s - lses[a])
                    ds = p * (_dot_nt(dobs[a], v) - deltas[a]) * MLA_SCALE
                    dsb = ds.astype(BF16)
                    dk_ref[a, cols, :] += _dot_tn(dsb, qs[a])
                    dv_ref[cols, lanes[a]] += _dot_tn(p.astype(BF16), dobs[a])
                    out.append(dqs[a] + _dot(dsb, k))
                return tuple(out)

            zero = jnp.zeros((b, 128), F32)
            nfull = r0 // bk
            dqs = lax.fori_loop(0, nfull, lambda c, d: block(pl.multiple_of(c * bk, bk), d, False), (zero, zero))
            for i in range(max(1, b // bk)):
                dqs = block(pl.multiple_of((nfull + i) * bk, bk), dqs, True)
            dq_ref[0, rows, :] = dqs[0]
            dq_ref[1, rows, :] = dqs[1]
            return carry

        lax.fori_loop(0, nb, rbody, 0)

    pair3 = pl.BlockSpec((2, t, 128), lambda i: (i, 0, 0))
    return pl.pallas_call(
        body, grid=(MLA_HEADS // 2,),
        in_specs=[pair3, pair3, pl.BlockSpec((t, 128), lambda i: (0, 8 + i)), pl.BlockSpec((t, 128), lambda i: (0, 9 + i)),
                  pl.BlockSpec((t, 128), lambda i: (0, i)), pl.BlockSpec((t, 128), lambda i: (0, i)),
                  pl.BlockSpec((1, t, 128), lambda i: (i, 0, 0))],
        out_specs=[pair3, pair3, pl.BlockSpec((t, 128), lambda i: (0, i)), pl.BlockSpec((t, 128), lambda i: (0, i))],
        out_shape=[S((MLA_HEADS, t, 128), F32), S((MLA_HEADS, t, 128), F32), S((t, 1024), F32), S((t, 1024), BF16)],
        name="mla_attn_bwd", compiler_params=_cp(("parallel",), VMEM_BIG))(qh, kh, kvf, proj, o, dpout, lse)


def _dot_nt_hi(a, b):
    return lax.dot_general(a, b, (((1,), (1,)), ((), ())), preferred_element_type=F32,
                           precision=lax.Precision.HIGHEST)


SSM_PACK = 2


def _ssm_gen(lr, li, ldt, btr, bti, cr, ci):
    n = SSM_BLOCK
    lanes = SSM_PACK * SSM_STATE
    dt = jnp.exp(ldt)
    x = lr * dt
    w = li * dt
    mag = jnp.exp(x)
    ab_re = mag * jnp.cos(w)
    ab_im = mag * jnp.sin(w)
    den = lr * lr + li * li
    nr = ab_re - 1.0
    f_re = (nr * lr + ab_im * li) / den
    f_im = (ab_im * lr - nr * li) / den
    bb_re = f_re * btr - f_im * bti
    bb_im = f_re * bti + f_im * btr
    inv_mag = jnp.exp(-x)
    inv_re, inv_im = inv_mag * jnp.cos(w), -inv_mag * jnp.sin(w)
    one, zero = jnp.ones_like(x), jnp.zeros_like(x)
    pos, neg = [(one, zero)], [(one, zero)]
    for _ in range(SSM_CHUNK):
        pr, pi = pos[-1]
        pos.append((pr * ab_re - pi * ab_im, pr * ab_im + pi * ab_re))
        pr, pi = neg[-1]
        neg.append((pr * inv_re - pi * inv_im, pr * inv_im + pi * inv_re))

    def stack(table, idx):
        return (jnp.concatenate([jnp.broadcast_to(table[idx(t)][0], (SSM_GROUP, lanes)) for t in range(SSM_CHUNK)], axis=0),
                jnp.concatenate([jnp.broadcast_to(table[idx(t)][1], (SSM_GROUP, lanes)) for t in range(SSM_CHUNK)], axis=0))

    def rep(v):
        return jnp.concatenate([v] * SSM_CHUNK, axis=0)

    ct_re, ct_im, bt_re, bt_im = rep(cr), rep(ci), rep(bb_re), rep(bb_im)
    p_re, p_im = stack(pos, lambda t: t)
    l_re = ct_re * p_re - ct_im * p_im
    l_im = ct_re * p_im + ct_im * p_re
    n_re, n_im = stack(neg, lambda t: t)
    r_re = bt_re * n_re - bt_im * n_im
    r_im = bt_re * n_im + bt_im * n_re
    rk = lax.broadcasted_iota(jnp.int32, (n, n), 0) // SSM_GROUP
    ct = lax.broadcasted_iota(jnp.int32, (n, n), 1) // SSM_GROUP
    mts = []
    for g in range(SSM_PACK):
        sl = slice(g * SSM_STATE, (g + 1) * SSM_STATE)
        mts.append(jnp.where(ct >= rk, _dot_nt_hi(r_re[:, sl], l_re[:, sl]) - _dot_nt_hi(r_im[:, sl], l_im[:, sl]), 0.0))
    e_re, e_im = stack(pos, lambda t: SSM_CHUNK - 1 - t)
    pin_re = bt_re * e_re - bt_im * e_im
    pin_im = bt_re * e_im + bt_im * e_re
    q_re, q_im = stack(pos, lambda t: t + 1)
    qt_re = ct_re * q_re - ct_im * q_im
    qt_im = -(ct_re * q_im + ct_im * q_re)
    ad_re, ad_im = pos[SSM_CHUNK]
    return tuple(mts), pin_re, pin_im, qt_re, qt_im, ad_re, ad_im


_SSM_GEN_OUT = [(SSM_BLOCK, SSM_BLOCK)] + [(SSM_BLOCK, SSM_STATE)] * 4 + [(1, SSM_STATE)] * 2
_SSM_GEN_IN = [(1, SSM_STATE)] * 3 + [(SSM_GROUP, SSM_STATE)] * 4


def _gspec(shape2):
    return pl.BlockSpec((SSM_PACK,) + shape2, lambda g: (g, 0, 0))


def _pack_lanes(ref):
    return jnp.concatenate([ref[g] for g in range(SSM_PACK)], axis=1)


def _unpack_lanes(ref, val):
    for g in range(SSM_PACK):
        ref[g] = val[:, g * SSM_STATE:(g + 1) * SSM_STATE]


def _ssm_operators(params):
    def body(*refs):
        ins, outs = refs[:7], refs[7:]
        vals = _ssm_gen(*[_pack_lanes(r) for r in ins])
        for g in range(SSM_PACK):
            outs[0][g] = vals[0][g]
        for o_ref, val in zip(outs[1:], vals[1:]):
            _unpack_lanes(o_ref, val)

    return pl.pallas_call(
        body, grid=(SSM_GROUPS // SSM_PACK,), in_specs=[_gspec(s) for s in _SSM_GEN_IN],
        out_specs=[_gspec(s) for s in _SSM_GEN_OUT],
        out_shape=[S((SSM_GROUPS,) + s, F32) for s in _SSM_GEN_OUT], name="ssm_operators",
        compiler_params=_cp(("parallel",)))(*params)


def _ssm_operators_bwd(params, cots, after=()):
    def body(*refs):
        ins, cts, outs = refs[:7], refs[7:14], refs[14 + len(after):]
        _, vjp = jax.vjp(_ssm_gen, *[_pack_lanes(r) for r in ins])
        grads = list(vjp((tuple(cts[0][g] for g in range(SSM_PACK)),) + tuple(_pack_lanes(r) for r in cts[1:])))
        for o_ref, val in zip(outs, grads):
            _unpack_lanes(o_ref, val)
        for g in range(SSM_PACK):
            dldt = grads[2][:, g * SSM_STATE:(g + 1) * SSM_STATE]
            outs[2][g] = jnp.broadcast_to(jnp.sum(dldt, axis=-1, keepdims=True), (1, SSM_STATE))

    return pl.pallas_call(
        body, grid=(SSM_GROUPS // SSM_PACK,),
        in_specs=[_gspec(s) for s in _SSM_GEN_IN] + [_gspec(s) for s in _SSM_GEN_OUT]
        + [pl.BlockSpec(memory_space=pl.ANY)] * len(after),
        out_specs=[_gspec(s) for s in _SSM_GEN_IN], out_shape=[S((SSM_GROUPS,) + s, F32) for s in _SSM_GEN_IN],
        name="ssm_operators_bwd", compiler_params=_cp(("parallel",)))(*params, *cots, *after)


def _shift_rows(x, sh, row, up):
    n = x.shape[0]
    if up:
        return jnp.where(row < n - sh, pltpu.roll(x, n - sh, axis=0), 0.0)
    return jnp.where(row >= sh, pltpu.roll(x, sh, axis=0), 0.0)


def _carry_scan(sr, si, ar, ai, row, up):
    n = sr.shape[0]
    sh = 1
    while sh < n:
        tr, ti = _shift_rows(sr, sh, row, up), _shift_rows(si, sh, row, up)
        sr, si = sr + ar * tr - ai * ti, si + ar * ti + ai * tr
        ar, ai = ar * ar - ai * ai, 2.0 * ar * ai
        sh *= 2
    return sr, si


def _ssm_states(u, p_re, p_im, ar, ai, row):
    lre = _dot(u, p_re.astype(BF16))
    lim = _dot(u, p_im.astype(BF16))
    sr, si = _carry_scan(lre, lim, ar, ai, row, False)
    return _shift_rows(sr, 1, row, False), _shift_rows(si, 1, row, False)


SSM_GPB = 128 // SSM_GROUP


def _step_rows(ref, nc):
    return [ref[pl.ds(k, nc, stride=SSM_CHUNK), :].astype(BF16) for k in range(SSM_CHUNK)]


def _group_of(rows, j):
    return jnp.concatenate([r[:, j * SSM_GROUP:(j + 1) * SSM_GROUP] for r in rows], axis=1)


def _store_groups(o_ref, per_group, nc):
    for k in range(SSM_CHUNK):
        o_ref[pl.ds(k, nc, stride=SSM_CHUNK), :] = jnp.concatenate(
            [y[:, k * SSM_GROUP:(k + 1) * SSM_GROUP] for y in per_group], axis=1)


def _bspec(shape2):
    return pl.BlockSpec((SSM_GPB,) + shape2, lambda j: (j, 0, 0))


def _ssm_core_fwd(proj, ops):
    t = proj.shape[0]
    nc = t // SSM_CHUNK

    def body(u_ref, mt_ref, pr_ref, pi_ref, qr_ref, qi_ref, ar_ref, ai_ref, y_ref):
        rows = _step_rows(u_ref, nc)
        row = lax.broadcasted_iota(jnp.int32, (nc, SSM_STATE), 0)
        ys = []
        for j in range(SSM_GPB):
            u = _group_of(rows, j)
            s_re, s_im = _ssm_states(u, pr_ref[j], pi_ref[j], ar_ref[j], ai_ref[j], row)
            ys.append(_dot(u, mt_ref[j].astype(BF16)) + _dot_nt(s_re.astype(BF16), qr_ref[j].astype(BF16))
                      + _dot_nt(s_im.astype(BF16), qi_ref[j].astype(BF16)))
        _store_groups(y_ref, ys, nc)

    blk = pl.BlockSpec((t, 128), lambda j: (0, j))
    return pl.pallas_call(
        body, grid=(SSM_GROUPS // SSM_GPB,), in_specs=[blk] + [_bspec(s) for s in _SSM_GEN_OUT], out_specs=blk,
        out_shape=S((t, 1024), F32), name="ssm_core_fwd", compiler_params=_cp(("parallel",)))(proj, *ops)


def _ssm_core_bwd(proj, dy, ops):
    t = proj.shape[0]
    nc = t // SSM_CHUNK

    def body(u_ref, dy_ref, mt_ref, pr_ref, pi_ref, qr_ref, qi_ref, ar_ref, ai_ref,
             du_ref, dmt_ref, dpr_ref, dpi_ref, dqr_ref, dqi_ref, dar_ref, dai_ref):
        u_rows = _step_rows(u_ref, nc)
        dy_rows = _step_rows(dy_ref, nc)
        row = lax.broadcasted_iota(jnp.int32, (nc, SSM_STATE), 0)
        dus = []
        for j in range(SSM_GPB):
            u = _group_of(u_rows, j)
            dyj = _group_of(dy_rows, j)
            ar, ai = ar_ref[j], ai_ref[j]
            s_re, s_im = _ssm_states(u, pr_ref[j], pi_ref[j], ar, ai, row)
            g_re = _shift_rows(_dot(dyj, qr_ref[j].astype(BF16)), 1, row, True)
            g_im = _shift_rows(_dot(dyj, qi_ref[j].astype(BF16)), 1, row, True)
            l_re, l_im = _carry_scan(g_re, g_im, ar, -ai, row, True)
            lrb, lib = l_re.astype(BF16), l_im.astype(BF16)
            dus.append(_dot_nt(dyj, mt_ref[j].astype(BF16)) + _dot_nt(lrb, pr_ref[j].astype(BF16))
                       + _dot_nt(lib, pi_ref[j].astype(BF16)))
            dmt_ref[j] = _dot_tn(u, dyj)
            dpr_ref[j] = _dot_tn(u, lrb)
            dpi_ref[j] = _dot_tn(u, lib)
            dqr_ref[j] = _dot_tn(dyj, s_re.astype(BF16))
            dqi_ref[j] = _dot_tn(dyj, s_im.astype(BF16))
            dar_ref[j] = jnp.sum(l_re * s_re + l_im * s_im, axis=0, keepdims=True)
            dai_ref[j] = jnp.sum(l_im * s_re - l_re * s_im, axis=0, keepdims=True)
        _store_groups(du_ref, dus, nc)

    blk = pl.BlockSpec((t, 128), lambda j: (0, j))
    return pl.pallas_call(
        body, grid=(SSM_GROUPS // SSM_GPB,), in_specs=[blk, blk] + [_bspec(s) for s in _SSM_GEN_OUT],
        out_specs=[blk] + [_bspec(s) for s in _SSM_GEN_OUT],
        out_shape=[S((t, 1024), F32)] + [S((SSM_GROUPS,) + s, F32) for s in _SSM_GEN_OUT],
        name="ssm_core_bwd", compiler_params=_cp(("parallel",)))(proj, dy, *ops)


def _s5_act(y_core, proj, d_skip):
    t = y_core.shape[0]
    tr = 256

    def body(y_ref, u_ref, d_ref, o_ref):
        o_ref[...] = _gelu(y_ref[...] + d_ref[...] * u_ref[...])

    return pl.pallas_call(body, grid=(t // tr,), in_specs=[_row_spec(tr, 1024), _row_spec(tr, 1024), _full_spec((1, 1024))],
                          out_specs=_row_spec(tr, 1024), out_shape=S((t, 1024), F32), name="s5_act",
                          compiler_params=_cp(("parallel",)))(y_core, proj, d_skip)


def _s5_gate(yg, tg, b_glu, proj):
    t = yg.shape[0]
    tr = 256

    def body(y_ref, t_ref, b_ref, z_ref, o_ref):
        o_ref[...] = (y_ref[...] * jax.nn.sigmoid(t_ref[...] + b_ref[...]) * _silu(z_ref[...])).astype(BF16)

    return pl.pallas_call(
        body, grid=(t // tr,),
        in_specs=[_row_spec(tr, 1024), _row_spec(tr, 1024), _full_spec((1, 1024)), pl.BlockSpec((tr, 1024), lambda i: (i, 1))],
        out_specs=_row_spec(tr, 1024), out_shape=S((t, 1024), BF16), name="s5_gate",
        compiler_params=_cp(("parallel",)))(yg, tg, b_glu, proj)


def _s5_gate_bwd(dp, yg, tg, b_glu, proj):
    t = yg.shape[0]
    tr = 256

    def body(dp_ref, y_ref, t_ref, b_ref, z_ref, dt_ref, dy_ref, dz_ref, db_ref):
        @pl.when(pl.program_id(0) == 0)
        def _():
            db_ref[...] = jnp.zeros_like(db_ref)

        z = z_ref[...]
        yv = y_ref[...]
        sg = jax.nn.sigmoid(t_ref[...] + b_ref[...])
        dpv = dp_ref[...].astype(F32)
        do = dpv * _silu(z)
        dz_ref[...] = (dpv * yv * sg * _silu_grad(z)).astype(BF16)
        dy_ref[...] = do * sg
        dtv = do * yv * sg * (1.0 - sg)
        dt_ref[...] = dtv.astype(BF16)
        db_ref[...] += jnp.sum(dtv, axis=0, keepdims=True)

    return pl.pallas_call(
        body, grid=(t // tr,),
        in_specs=[_row_spec(tr, 1024), _row_spec(tr, 1024), _row_spec(tr, 1024), _full_spec((1, 1024)),
                  pl.BlockSpec((tr, 1024), lambda i: (i, 1))],
        out_specs=[_row_spec(tr, 1024), _row_spec(tr, 1024), _row_spec(tr, 1024), _full_spec((1, 1024))],
        out_shape=[S((t, 1024), BF16), S((t, 1024), F32), S((t, 1024), BF16), S((1, 1024), F32)],
        name="s5_gate_bwd", compiler_params=_cp(("arbitrary",)))(dp, yg, tg, b_glu, proj)


def _s5_act_bwd(dyg1, dyg2, y_core, proj, d_skip):
    t = y_core.shape[0]
    tr = 256

    def body(a_ref, b_ref, y_ref, u_ref, d_ref, dy_ref, du_ref, dd_ref):
        @pl.when(pl.program_id(0) == 0)
        def _():
            dd_ref[...] = jnp.zeros_like(dd_ref)

        u = u_ref[...]
        dv = (a_ref[...] + b_ref[...]) * _gelu_grad(y_ref[...] + d_ref[...] * u)
        dy_ref[...] = dv
        du_ref[...] = dv * d_ref[...]
        dd_ref[...] += jnp.sum(dv * u, axis=0, keepdims=True)

    return pl.pallas_call(
        body, grid=(t // tr,),
        in_specs=[_row_spec(tr, 1024)] * 4 + [_full_spec((1, 1024))],
        out_specs=[_row_spec(tr, 1024), _row_spec(tr, 1024), _full_spec((1, 1024))],
        out_shape=[S((t, 1024), F32), S((t, 1024), F32), S((1, 1024), F32)],
        name="s5_act_bwd", compiler_params=_cp(("arbitrary",)))(dyg1, dyg2, y_core, proj, d_skip)


def _s5_dproj(du_skip, du_core, dz):
    t = du_skip.shape[0]
    tr = 256

    def body(a_ref, b_ref, z_ref, o_ref):
        o_ref[:, 0:1024] = (a_ref[...] + b_ref[...]).astype(BF16)
        o_ref[:, 1024:2048] = z_ref[...]

    return pl.pallas_call(body, grid=(t // tr,), in_specs=[_row_spec(tr, 1024)] * 3, out_specs=_row_spec(tr, 2048),
                          out_shape=S((t, 2048), BF16), name="s5_dproj",
                          compiler_params=_cp(("parallel",)))(du_skip, du_core, dz)


def _s5_params(lam_re, lam_im, log_dt, b_re, b_im, c_re, c_im):
    g = SSM_GROUPS
    return (lam_re.reshape(g, 1, SSM_STATE), lam_im.reshape(g, 1, SSM_STATE),
            jnp.broadcast_to(log_dt.reshape(g, 1, 1), (g, 1, SSM_STATE)),
            b_re.transpose(0, 2, 1), b_im.transpose(0, 2, 1), c_re, c_im)


def _s5_fwd(proj, params, d_skip, w_glu_after, b_glu):
    ops = _ssm_operators(params)
    y_core = _ssm_core_fwd(proj, ops)
    yg = _s5_act(y_core, proj, d_skip)
    tg = _mm(yg, w_glu_after(yg), name="s5_glu")
    return _s5_gate(yg, tg, b_glu, proj), (ops, y_core, yg, tg)


def _s5_bwd(proj, res, dp, params, d_skip, w_glu, b_glu):
    ops, y_core, yg, tg = res
    dtg, dyg1, dz, db_glu = _s5_gate_bwd(dp, yg, tg, b_glu, proj)
    dyg2 = _mm(dtg, w_glu, tb=True, name="s5_glu_dx")
    dw_glu = _mm(yg, dtg, ta=True, out_dtype=BF16, name="s5_glu_dw")
    dy_core, du_skip, dd = _s5_act_bwd(dyg1, dyg2, y_core, proj, d_skip)
    outs = _ssm_core_bwd(proj, dy_core, ops)
    dproj = _s5_dproj(du_skip, outs[0], dz)
    return dproj, outs[1:], dd, dw_glu, db_glu


def _s5_param_grads(params, cots, after=()):
    glr, gli, gdt, gbtr, gbti, gcr, gci = _ssm_operators_bwd(params, cots, after)
    g = SSM_GROUPS
    return (glr.reshape(g, SSM_STATE), gli.reshape(g, SSM_STATE), gdt[:, 0, 0].reshape(1, g), gbtr, gbti, gcr, gci)


def _perm_uq(w):
    w3 = w.reshape(w.shape[0], MLA_HEADS, MLA_NOPE + MLA_ROPE)
    return jnp.concatenate([w3[:, :, :64].reshape(-1, 1024), w3[:, :, 64:80].reshape(-1, 256),
                            w3[:, :, 80:96].reshape(-1, 256)], axis=1)


def _unperm_uq(w):
    r = w.shape[0]
    return jnp.concatenate([w[:, :1024].reshape(r, 16, 64), w[:, 1024:1280].reshape(r, 16, 16),
                            w[:, 1280:1536].reshape(r, 16, 16)], axis=2).reshape(r, 1536)


def _perm_ukv(w):
    w3 = w.reshape(w.shape[0], MLA_HEADS, 128)
    return jnp.concatenate([w3[:, :, :64].reshape(-1, 1024), w3[:, :, 64:].reshape(-1, 1024)], axis=1)


def _unperm_ukv(w):
    r = w.shape[0]
    return jnp.concatenate([w[:, :1024].reshape(r, 16, 64), w[:, 1024:].reshape(r, 16, 64)], axis=2).reshape(r, 2048)


def _mla_fwd(proj, gq, gkv, wuq, wukv, cos, sin):
    cqn, ckvn = _mla_norms(proj, gq, gkv)
    qf = _mm(cqn, wuq, name="mla_uq")
    kvf = _mm(ckvn, wukv, name="mla_ukv")
    qh, kh = _mla_pack(qf, kvf, proj, cos, sin)
    o, pout, lse = _mla_attn_fwd(qh, kh, kvf, proj)
    return pout, (cqn, ckvn, kvf, qh, kh, o, lse)


def _mla_bwd(proj, res, dpout, gq, gkv, wuq, wukv, cos, sin):
    cqn, ckvn, kvf, qh, kh, o, lse = res
    dqh, dkh, dv, dz = _mla_attn_bwd(qh, kh, kvf, proj, o, dpout, lse)
    dqf, dkvf, dkr = _mla_unpack(dqh, dkh, dv, cos, sin)
    dcqn = _mm(dqf, wuq, tb=True, name="mla_uq_dx")
    dwuq = _mm(cqn, dqf, ta=True, out_dtype=BF16, name="mla_uq_dw")
    dckvn = _mm(dkvf, wukv, tb=True, name="mla_ukv_dx")
    dwukv = _mm(ckvn, dkvf, ta=True, out_dtype=BF16, name="mla_ukv_dw")
    dproj, dgq, dgkv = _mla_norms_bwd(proj, dcqn, dckvn, dkr, dz, gq, gkv)
    return dproj, dgq, dgkv, dwuq, dwukv


_ANY = pl.BlockSpec(memory_space=pl.ANY)


def _chip_peers():
    x, y, c = lax.axis_index("x"), lax.axis_index("y"), lax.axis_index("c")
    return 2 * x + y, c, [(1 - x, y), (x, 1 - y), (1 - x, 1 - y)]


def _sibling_swap(ps, name):
    n = len(ps)

    def body(*refs):
        p_refs, q_refs, send_sems, recv_sems = refs[:n], refs[n:2 * n], refs[2 * n], refs[2 * n + 1]
        x, y, c = lax.axis_index("x"), lax.axis_index("y"), lax.axis_index("c")
        cps = [pltpu.make_async_remote_copy(src_ref=p_refs[i], dst_ref=q_refs[i], send_sem=send_sems.at[i],
                                            recv_sem=recv_sems.at[i], device_id=(x, y, 1 - c), device_id_type=MESH)
               for i in range(n)]
        for cp in cps:
            cp.start()
        for cp in cps:
            cp.wait()

    return pl.pallas_call(body, in_specs=[_ANY] * n, out_specs=[_ANY] * n, out_shape=[S(p.shape, p.dtype) for p in ps],
                          scratch_shapes=[pltpu.SemaphoreType.DMA((n,)), pltpu.SemaphoreType.DMA((n,))], name=name)(*ps)


def _quarter_spec(tr, r, c, dim):
    if dim is None:
        return pl.BlockSpec((1, tr, c), lambda i, me: (me[0], i, 0))
    if dim == 0:
        return pl.BlockSpec((tr, c), lambda i, me: (me[0] * (r // tr) + i, 0))
    return pl.BlockSpec((tr, c), lambda i, me: (i, me[0]))


def _whole_shape(r, c, dim):
    return (N_CHIPS, r, c) if dim is None else ((N_CHIPS * r, c) if dim == 0 else (r, N_CHIPS * c))


def _row_tile(r):
    for d in range(min(r, 256), 7, -1):
        if r % d == 0 and d % 8 == 0:
            return d
    return r


def _cast_place(me, x, dim, dtype, name):
    r, c = x.shape
    tr = _row_tile(r)

    def body(me_ref, x_ref, o_ref):
        o_ref[...] = x_ref[...].astype(dtype).reshape(o_ref.shape)

    return pl.pallas_call(
        body, out_shape=S(_whole_shape(r, c, dim), dtype), name=name,
        grid_spec=pltpu.PrefetchScalarGridSpec(num_scalar_prefetch=1, grid=(r // tr,),
                                               in_specs=[pl.BlockSpec((tr, c), lambda i, me: (i, 0))],
                                               out_specs=_quarter_spec(tr, r, c, dim)),
        compiler_params=_cp(("parallel",)))(me, x)


def _sum_parts(me, g, land, dim, name):
    _, r, c = land.shape
    tr = _row_tile(r)

    def body(me_ref, g_ref, l_ref, o_ref):
        own = g_ref[...].astype(F32).reshape(tr, c)
        o_ref[...] = ((own + l_ref[0].astype(F32)) + l_ref[1].astype(F32)) + l_ref[2].astype(F32)

    return pl.pallas_call(
        body, out_shape=S((r, c), F32), name=name,
        grid_spec=pltpu.PrefetchScalarGridSpec(
            num_scalar_prefetch=1, grid=(r // tr,),
            in_specs=[_quarter_spec(tr, r, c, dim), pl.BlockSpec((3, tr, c), lambda i, me: (0, i, 0))],
            out_specs=pl.BlockSpec((tr, c), lambda i, me: (i, 0))),
        compiler_params=_cp(("parallel",)))(me, g, land)


_HBM = pl.BlockSpec(memory_space=pltpu.HBM)
_SEM = pl.BlockSpec(memory_space=pltpu.SEMAPHORE)
_EFFECT = pltpu.SideEffectType.DATAFLOW_SIDE_EFFECTING


def _quarter(ref, s, dim):
    n = ref.shape[dim] // N_CHIPS
    start = pl.multiple_of(s * n, n)
    return ref.at[pl.ds(start, n), :] if dim == 0 else ref.at[:, pl.ds(start, n)]


def _part(ref, s, dim):
    return ref.at[s] if dim is None else _quarter(ref, s, dim)


def _gather_plan(i, dim):
    return dict(src=i, dst=i, send_src=lambda ref, me, peer, k: _part(ref, me, dim),
                send_dst=lambda ref, me, peer, k: _part(ref, me, dim), recv_dst=lambda ref, me, peer, k: _part(ref, peer, dim))


def _scatter_plan(i, j, dim):
    return dict(src=i, dst=j, send_src=lambda ref, me, peer, k: _part(ref, peer, dim),
                send_dst=lambda ref, me, peer, k: ref.at[k], recv_dst=lambda ref, me, peer, k: ref.at[k])


def _hbm(a):
    return pltpu.with_memory_space_constraint(a, pltpu.HBM)


def _swap_plan(i, j):
    whole = lambda ref, me, peer, k: ref
    return dict(src=i, dst=j, send_src=whole, send_dst=whole, recv_dst=whole)


def _remote_copies(refs, send_sems, recv_sems, plans, sibling=False):
    me, c, peers = _chip_peers()
    x, y = lax.axis_index("x"), lax.axis_index("y")
    targets = [((x, y, 1 - c), me)] if sibling else [((px, py, c), 2 * px + py) for px, py in peers]
    n = len(targets)
    sends, recvs = [], []
    for i, p in enumerate(plans):
        for k, (device, peer) in enumerate(targets):
            sems = dict(send_sem=send_sems.at[n * i + k], recv_sem=recv_sems.at[n * i + k], device_id=device,
                        device_id_type=MESH)
            src = p['send_src'](refs[p['src']], me, peer, k)
            sends.append(pltpu.make_async_remote_copy(src_ref=src, dst_ref=p['send_dst'](refs[p['dst']], me, peer, k), **sems))
            recvs.append(pltpu.make_async_remote_copy(src_ref=src, dst_ref=p['recv_dst'](refs[p['dst']], me, peer, k), **sems))
    return sends, recvs


def _exchange_start(arrays, plans, name, after=(), sibling=False):
    n, extra = len(arrays), len(after)
    m = len(plans) * (1 if sibling else 3)

    def body(*refs):
        send_sems, recv_sems, token = refs[n + extra], refs[n + extra + 1], refs[-1]
        sends, _ = _remote_copies(refs[:n], send_sems, recv_sems, plans, sibling)
        for cp in sends:
            cp.start()
        token[...] = jnp.zeros_like(token)

    outs = pl.pallas_call(
        body, name=name,
        out_shape=(pltpu.SemaphoreType.DMA((m,)), pltpu.SemaphoreType.DMA((m,)),
                   *[pltpu.HBM(a.shape, a.dtype) for a in arrays], S((8, 128), F32)),
        in_specs=[_HBM] * n + [_ANY] * extra, out_specs=(_SEM, _SEM, *[_HBM] * n, pl.BlockSpec(memory_space=pltpu.VMEM)),
        input_output_aliases={i: 2 + i for i in range(n)},
        compiler_params=pltpu.CompilerParams(has_side_effects=_EFFECT))(*[_hbm(a) for a in arrays], *after)
    return outs[0], outs[1], outs[2:2 + n], outs[-1]


def _exchange_wait(started, plans, after, name, sibling=False):
    send_sems, recv_sems, arrays, _ = started
    n = len(arrays)

    def body(*refs):
        sends, recvs = _remote_copies(refs[:n], refs[n], refs[n + 1], plans, sibling)
        for cp in sends:
            cp.wait_send()
        for cp in recvs:
            cp.wait_recv()

    return pl.pallas_call(
        body, name=name, out_shape=tuple(pltpu.HBM(a.shape, a.dtype) for a in arrays),
        in_specs=[_HBM] * n + [_SEM, _SEM, _ANY], out_specs=tuple([_HBM] * n),
        input_output_aliases={i: i for i in range(n)},
        compiler_params=pltpu.CompilerParams(has_side_effects=_EFFECT))(*arrays, send_sems, recv_sems, after)


def _add2(p, q):
    r, c = p.shape
    tr = 128 if r % 128 == 0 else r

    def body(p_ref, q_ref, o_ref):
        o_ref[...] = p_ref[...] + q_ref[...]

    return pl.pallas_call(body, grid=(r // tr,), in_specs=[_row_spec(tr, c), _row_spec(tr, c)], out_specs=_row_spec(tr, c),
                          out_shape=S((r, c), F32), name="add_cores", compiler_params=_cp(("parallel",)))(p, q)


def _adamw(w, g, m, v, name, g2=None, after=()):
    r, c = w.shape
    tr = _row_tile(r)
    c1 = 1.0 / (1.0 - ADAM_B1 ** ADAM_STEP)
    c2 = 1.0 / (1.0 - ADAM_B2 ** ADAM_STEP)
    gs = [g] if g2 is None else [g, g2]

    def body(*refs):
        w_ref, m_ref, v_ref = refs[0], refs[1], refs[2]
        g_refs = refs[3:3 + len(gs)]
        go_ref, d_ref, nm_ref, nv_ref = refs[3 + len(gs) + len(after):]
        gv = g_refs[0][...]
        if len(gs) == 2:
            gv = gv + g_refs[1][...]
        go_ref[...] = gv
        nm = ADAM_B1 * m_ref[...] + (1.0 - ADAM_B1) * gv
        nv = ADAM_B2 * v_ref[...] + (1.0 - ADAM_B2) * (gv * gv)
        nm_ref[...] = nm
        nv_ref[...] = nv
        d_ref[...] = -ADAM_LR * ((nm * c1) / (jnp.sqrt(nv * c2) + ADAM_EPS) + ADAM_WD * w_ref[...])

    return pl.pallas_call(body, grid=(r // tr,),
                          in_specs=[_row_spec(tr, c)] * (3 + len(gs)) + [pl.BlockSpec(memory_space=pl.ANY)] * len(after),
                          out_specs=[_row_spec(tr, c)] * 4, out_shape=[S((r, c), F32)] * 4, name=name,
                          compiler_params=_cp(("parallel",)))(w, m, v, *gs, *after)


_WEIGHTS = ['pre_norm', 'post_norm', 'rel_bias', 'a_w_in', 'a_lam_re', 'a_lam_im', 'a_log_dt', 'a_b_re', 'a_b_im',
            'a_c_re', 'a_c_im', 'a_d', 'a_w_glu', 'a_b_glu', 'a_w_out', 'b_w_in', 'b_sinks', 'b_w_out', 'c_w_in',
            'c_q_norm', 'c_kv_norm', 'c_w_uq', 'c_w_ukv', 'c_w_out', 'd_w_in', 'd_ln_g', 'd_ln_b', 'd_w_s', 'd_b_s',
            'd_w_out']
_BIG = [('a_w_in', 1024, 2048, 1), ('a_w_glu', 1024, 1024, 0), ('a_w_out', 1024, 1024, 0), ('b_w_in', 1024, 2304, 1),
        ('b_w_out', 1024, 1024, 0), ('c_w_in', 1024, 2080, 1), ('c_w_uq', 768, 1536, 1), ('c_w_ukv', 256, 2048, 1),
        ('c_w_out', 1024, 1024, 0), ('d_w_in', 1024, 3072, 1), ('d_w_out', 1024, 1024, 0)]
_SHARDED_VECS = [('c_q_norm', 768), ('c_kv_norm', 256), ('d_ln_g', 1024), ('d_ln_b', 1024)]
_TRANSPOSED = ('b_w_in', 'c_w_in')
_REPLICATED = [n for n in _WEIGHTS if n not in [b[0] for b in _BIG] and n not in [s[0] for s in _SHARDED_VECS]]
_LANES = 1024
_SMALL_Q_ROWS = 136


def kernel(x, pre_norm, post_norm, rel_bias, a_w_in, a_lam_re, a_lam_im, a_log_dt, a_b_re, a_b_im, a_c_re, a_c_im, a_d, a_w_glu, a_b_glu, a_w_out, b_w_in, b_sinks, b_w_out, c_w_in, c_q_norm, c_kv_norm, c_w_uq, c_w_ukv, c_w_out, d_w_in, d_ln_g, d_ln_b, d_w_s, d_b_s, d_w_out, loss_target, m_pre_norm, m_post_norm, m_rel_bias, m_a_w_in, m_a_lam_re, m_a_lam_im, m_a_log_dt, m_a_b_re, m_a_b_im, m_a_c_re, m_a_c_im, m_a_d, m_a_w_glu, m_a_b_glu, m_a_w_out, m_b_w_in, m_b_sinks, m_b_w_out, m_c_w_in, m_c_q_norm, m_c_kv_norm, m_c_w_uq, m_c_w_ukv, m_c_w_out, m_d_w_in, m_d_ln_g, m_d_ln_b, m_d_w_s, m_d_b_s, m_d_w_out, v_pre_norm, v_post_norm, v_rel_bias, v_a_w_in, v_a_lam_re, v_a_lam_im, v_a_log_dt, v_a_b_re, v_a_b_im, v_a_c_re, v_a_c_im, v_a_d, v_a_w_glu, v_a_b_glu, v_a_w_out, v_b_w_in, v_b_sinks, v_b_w_out, v_c_w_in, v_c_q_norm, v_c_kv_norm, v_c_w_uq, v_c_w_ukv, v_c_w_out, v_d_w_in, v_d_ln_g, v_d_ln_b, v_d_w_s, v_d_b_s, v_d_w_out):
    arg = dict(locals())
    me = 2 * lax.axis_index("x") + lax.axis_index("y")
    xin = x[0]
    t = xin.shape[0]

    big = {n: (r, c, dim) for n, r, c, dim in _BIG}

    me1 = me.reshape(1).astype(jnp.int32)

    def gather_start(names, tag, after):
        wholes, plans = [], []
        for i, n in enumerate(names):
            if n in big:
                r, c, dim = big[n]
                src, dtype = arg[n][0], BF16
                if n in _TRANSPOSED:
                    src, dim = src.T, 0
                    if (c // N_CHIPS) % 16:
                        dtype = F32
                elif dim == 1 and (c // N_CHIPS) % 128:
                    dim = None
            else:
                src, dim, dtype = arg[n], None, F32
            wholes.append(_cast_place(me1, src, dim, dtype, "place_" + n))
            plans.append(_gather_plan(i, dim))
        return _exchange_start(wholes, plans, "gather_start_" + tag, after=after), plans

    def gather_wait(names, started, after, tag):
        st, plans = started
        out = {}
        for n, a in zip(names, _exchange_wait(st, plans, after, "gather_wait_" + tag)):
            if n in big:
                r, c, dim = big[n]
                out[n] = a if a.ndim == 2 else a.transpose(1, 0, 2).reshape(r, c)
            else:
                out[n] = a.reshape(1, -1)
        return out

    groups = {'a1': ['a_w_in'], 'a2': ['a_w_glu', 'a_w_out'], 'b': ['b_w_in', 'b_w_out'],
              'c': ['c_w_in', 'c_w_uq', 'c_w_ukv', 'c_w_out', 'c_q_norm', 'c_kv_norm'],
              'd': ['d_w_in', 'd_w_out', 'd_ln_g', 'd_ln_b']}
    started, tokens = {}, []
    for tag, names in groups.items():
        started[tag] = gather_start(names, tag, tokens)
        tokens = [started[tag][0][3]]
    wt = gather_wait(groups['a1'], started['a1'], xin, 'a1')
    cos, sin = _rope_tables(t)
    bucket = jnp.asarray(_t5_bucket_table())
    s5p = _s5_params(a_lam_re[0], a_lam_im[0], a_log_dt[0], a_b_re[0], a_b_im[0], a_c_re[0], a_c_im[0])
    bst = d_b_s[0].T

    x0 = xin
    h0 = _prenorm(x0, pre_norm[0:1], "pre0", after=tokens)
    proj0 = _mm(h0, wt['a_w_in'], name="a_in")

    def w_glu_after(x):
        wt.update(gather_wait(groups['a2'], started['a2'], x, 'a2'))
        return wt['a_w_glu']

    p0, res0 = _s5_fwd(proj0, s5p, a_d, w_glu_after, a_b_glu)
    y0 = _mm(p0, wt['a_w_out'], name="a_out")
    x1, h1 = _postnorm_prenorm(x0, y0, post_norm[0:1], pre_norm[1:2], "post0_pre1")

    wt.update(gather_wait(groups['b'], started['b'], x1, 'b'))
    proj1 = _mm(h1, wt['b_w_in'], tb=True, name="b_in")
    bias = _swa_bias(rel_bias, bucket)
    p1 = _swa_fwd(proj1, bias, b_sinks)
    y1 = _mm(p1, wt['b_w_out'], name="b_out")
    x2, h2 = _postnorm_prenorm(x1, y1, post_norm[1:2], pre_norm[2:3], "post1_pre2")

    wt.update(gather_wait(groups['c'], started['c'], x2, 'c'))
    w_c_in = jnp.concatenate([wt['c_w_in'][:1056], jnp.zeros((96, 1024), F32), wt['c_w_in'][1056:],
                              jnp.zeros((MLA_PROJ - MLA_Z1, 1024), F32)], axis=0).astype(BF16)
    w_uq = _perm_uq(wt['c_w_uq'])
    w_ukv = _perm_ukv(wt['c_w_ukv'])
    proj2 = _mm(h2, w_c_in, tb=True, name="c_in")
    p2, res2 = _mla_fwd(proj2, wt['c_q_norm'], wt['c_kv_norm'], w_uq, w_ukv, cos, sin)
    y2 = _mm(p2, wt['c_w_out'], name="c_out")
    x3, h3 = _postnorm_prenorm(x2, y2, post_norm[2:3], pre_norm[3:4], "post2_pre3")

    wt.update(gather_wait(groups['d'], started['d'], x3, 'd'))
    vec = wt
    proj3 = _mm(h3, wt['d_w_in'], name="d_in")
    p3 = _sgu_fwd(proj3, vec['d_ln_g'], vec['d_ln_b'], d_w_s[0], bst)
    y3 = _mm(p3, wt['d_w_out'], name="d_out")
    loss_part, dx = _postnorm_loss(x3, y3, post_norm[3:4], loss_target[0])
    loss = lax.psum(loss_part[0, 0], ("x", "y", "c"))

    gr = {}
    d_pre, d_post = [None] * 4, [None] * 4
    scattered = []

    def scatter_start(items, tag):
        srcs = [a for _, a, _ in items]
        n = len(items)
        plans = [_scatter_plan(i, n + i, dim) for i, (_, _, dim) in enumerate(items)]
        lands = []
        for _, a, dim in items:
            part = a.shape[1:] if dim is None else tuple(s // N_CHIPS if i == dim else s for i, s in enumerate(a.shape))
            lands.append(lax.empty((3,) + part, a.dtype))
        st = _exchange_start(srcs + lands, plans, "scatter_start_" + tag)
        scattered.append((st, plans, items, tag))
        return [st[3]]

    dy, d_post[3] = _postnorm_bwd(dx, y3, post_norm[3:4], "post3_bwd")
    dp = _mm(dy, wt['d_w_out'], tb=True, name="d_out_dx")
    g_out = _mm(p3, dy, ta=True, out_dtype=BF16, name="d_out_dw")
    dproj, gr['d_w_s'], dbst, gr['d_ln_g'], gr['d_ln_b'] = _sgu_bwd(proj3, dp, vec['d_ln_g'], vec['d_ln_b'], d_w_s[0], bst)
    gr['d_b_s'] = dbst.T
    dh = _mm(dproj, wt['d_w_in'], tb=True, name="d_in_dx")
    g_in = _mm(h3, dproj, ta=True, out_dtype=BF16, name="d_in_dw")
    tok = scatter_start([('d_w_in', g_in, 1), ('d_w_out', g_out, 0)], 'd')
    dx, d_pre[3], dy, d_post[2] = _norm_bwd_pair(dx, dh, x3, pre_norm[3:4], y2, post_norm[2:3], "pre3_post2_bwd", after=tok)

    dp = _mm(dy, wt['c_w_out'], tb=True, name="c_out_dx")
    g_out = _mm(p2, dy, ta=True, out_dtype=BF16, name="c_out_dw")
    dproj, gr['c_q_norm'], gr['c_kv_norm'], dwuq, dwukv = _mla_bwd(proj2, res2, dp, wt['c_q_norm'], wt['c_kv_norm'],
                                                                  w_uq, w_ukv, cos, sin)
    dh = _mm(dproj, w_c_in, name="c_in_dx")
    dwc = _mm(dproj, h2, ta=True, name="c_in_dw")
    g_in = jnp.concatenate([dwc[:1056], dwc[MLA_Z0:MLA_Z1]], axis=0)
    tok = scatter_start([('c_w_in', g_in, 0), ('c_w_uq', _unperm_uq(dwuq), 1), ('c_w_ukv', _unperm_ukv(dwukv), 1),
                         ('c_w_out', g_out, 0)], 'c')
    dx, d_pre[2], dy, d_post[1] = _norm_bwd_pair(dx, dh, x2, pre_norm[2:3], y1, post_norm[1:2], "pre2_post1_bwd", after=tok)

    dp = _mm(dy, wt['b_w_out'], tb=True, name="b_out_dx")
    g_out = _mm(p1, dy, ta=True, out_dtype=BF16, name="b_out_dw")
    dqz, dkv, dss, dsk = _swa_bwd(proj1, dp, bias, b_sinks)
    gr['rel_bias'], gr['b_sinks'] = _swa_dbias(dss, dsk, bucket)
    dproj = jnp.concatenate([dqz[:, :1024], dkv.astype(BF16), dqz[:, 1024:]], axis=1)
    dh = _mm(dproj, wt['b_w_in'], name="b_in_dx")
    g_in = _mm(dproj, h1, ta=True, out_dtype=BF16, name="b_in_dw")
    tok = scatter_start([('b_w_in', g_in, 0), ('b_w_out', g_out, 0)], 'b')
    dx, d_pre[1], dy, d_post[0] = _norm_bwd_pair(dx, dh, x1, pre_norm[1:2], y0, post_norm[0:1], "pre1_post0_bwd", after=tok)

    dp = _mm(dy, wt['a_w_out'], tb=True, name="a_out_dx")
    g_out = _mm(p0, dy, ta=True, out_dtype=BF16, name="a_out_dw")
    dproj, s5cots, gr['a_d'], g_glu, gr['a_b_glu'] = _s5_bwd(proj0, res0, dp, s5p, a_d, wt['a_w_glu'], a_b_glu)
    tok = scatter_start([('a_w_glu', g_glu, 0), ('a_w_out', g_out, 0)], 'a2')
    g_in = _mm(h0, dproj, ta=True, out_dtype=BF16, name="a_in_dw", after=tok)
    tok = scatter_start([('a_w_in', g_in, 1)], 'a1')
    dh = _mm(dproj, wt['a_w_in'], tb=True, name="a_in_dx", after=tok)
    dx, d_pre[0] = _prenorm_bwd(dx, dh, x0, pre_norm[0:1], "pre0_bwd")
    s5g = _s5_param_grads(s5p, s5cots, after=tok)
    for n, g in zip(['a_lam_re', 'a_lam_im', 'a_log_dt', 'a_b_re', 'a_b_im', 'a_c_re', 'a_c_im'], s5g):
        gr[n] = g
    gr['pre_norm'] = jnp.concatenate(d_pre, axis=0)
    gr['post_norm'] = jnp.concatenate(d_post, axis=0)
    small_names = _REPLICATED + [n for n, _ in _SHARDED_VECS]
    small = jnp.concatenate([gr[n].reshape(-1) for n in small_names])
    small = jnp.pad(small, (0, N_CHIPS * _SMALL_Q_ROWS * _LANES - small.shape[0])).reshape(N_CHIPS, _SMALL_Q_ROWS, _LANES)
    tok = scatter_start([('small', small, None)], 'a0')

    partial = {}
    for st, plans, items, tag in scattered[:-1]:
        done = _exchange_wait(st, plans, tok[0], "scatter_wait_" + tag)
        for i, (n, _, dim) in enumerate(items):
            partial[n] = _sum_parts(me1, done[i], done[len(items) + i], dim, "sum_parts_" + n)
    big_names = [n for n, _, _, _ in _BIG]
    nb = len(big_names)
    swap_plans = [_swap_plan(i, nb + i) for i in range(nb)]
    swapping = _exchange_start([partial[n] for n in big_names] + [lax.empty(partial[n].shape, F32) for n in big_names],
                               swap_plans, "swap_cores_start", sibling=True)
    st, plans, _, tag = scattered[-1]
    done = _exchange_wait(st, plans, swapping[3], "scatter_wait_" + tag)
    p_small = _sum_parts(me1, done[0], done[1], None, "sum_parts_small")
    small_q = _add2(p_small, _sibling_swap([p_small], "swap_cores_small")[0])
    gplan = [_gather_plan(0, None)]
    gathering = _exchange_start([_cast_place(me1, small_q, None, F32, "place_small_grads")], gplan, "gather_small_start")

    swapped = _exchange_wait(swapping, swap_plans, gathering[3], "swap_cores_wait", sibling=True)
    partial.update(zip(big_names, swapped[:nb]))
    other = dict(zip(big_names, swapped[nb:]))

    delta, new_m, new_v, grads = {}, {}, {}, {}
    for n in big_names:
        shp = arg[n].shape
        view = (lambda a: a[0].T) if n in _TRANSPOSED else (lambda a: a[0])
        back = (lambda o: o.T.reshape(shp)) if n in _TRANSPOSED else (lambda o: o.reshape(shp))
        outs = _adamw(view(arg[n]), partial[n], view(arg['m_' + n]), view(arg['v_' + n]), "adamw_" + n, g2=other[n],
                      after=[gathering[3]])
        grads[n], delta[n], new_m[n], new_v[n] = [back(o) for o in outs]
    small_all = _exchange_wait(gathering, gplan, new_v[big_names[-1]], "gather_small_wait")[0].reshape(-1)
    def view(n, a):
        return a[0].transpose(0, 2, 1) if n in ('a_b_re', 'a_b_im') else a

    def back(n, v):
        return v.transpose(0, 2, 1)[None] if n in ('a_b_re', 'a_b_im') else v

    off = 0
    gview = {}
    for n in _REPLICATED:
        shp = view(n, arg[n]).shape
        sz = math.prod(shp)
        gview[n] = small_all[off:off + sz].reshape(shp)
        off += sz
    for n, sz in _SHARDED_VECS:
        q = sz // N_CHIPS
        gview[n] = lax.dynamic_slice(small_all, (off + me * q,), (q,)).reshape(arg[n].shape)
        off += sz

    for n in small_names:
        shp = gview[n].shape
        two = (math.prod(shp[:-1]), shp[-1])
        outs = _adamw(view(n, arg[n]).reshape(two), gview[n].reshape(two), view(n, arg['m_' + n]).reshape(two),
                      view(n, arg['v_' + n]).reshape(two), "adamw_" + n)
        grads[n] = back(n, gview[n])
        delta[n], new_m[n], new_v[n] = [back(n, o.reshape(shp)) for o in outs[1:]]

    return (loss, dx.reshape(x.shape), *[grads[n] for n in _WEIGHTS], *[delta[n] for n in _WEIGHTS],
            *[new_m[n] for n in _WEIGHTS], *[new_v[n] for n in _WEIGHTS])
```

```python
import functools
import math

import numpy as np
import jax
import jax.numpy as jnp
from jax import lax
from jax.experimental import pallas as pl
from jax.experimental.pallas import tpu as pltpu

F32 = jnp.float32
BF16 = jnp.bfloat16
S = jax.ShapeDtypeStruct
MESH = pl.DeviceIdType.MESH

D_MODEL = 1024
EPS = 1e-6
NEG_INF = -1e30
N_CHIPS = 4

SSM_GROUPS = 64
SSM_GROUP = 16
SSM_STATE = 64
SSM_CHUNK = 16
SSM_BLOCK = SSM_CHUNK * SSM_GROUP

HEAD_DIM = 64
SWA_HEADS = 16
SWA_GROUP = 8
WINDOW = 128
REL_BUCKETS = 32
REL_MAX_DIST = 128

MLA_HEADS = 16
MLA_NOPE = 64
MLA_ROPE = 32
MLA_Q_RANK = 768
MLA_KV_RANK = 256
MLA_SCALE = (MLA_NOPE + MLA_ROPE) ** -0.5
MLA_BQ = 512
MLA_BK = 512
ROPE_BASE = 10000.0

SGU_CHUNK = 128
SGU_GROUPS = 16
SGU_GDIM = 64

ADAM_LR = 0.001
ADAM_B1 = 0.9
ADAM_B2 = 0.999
ADAM_EPS = 1e-08
ADAM_WD = 0.01
ADAM_STEP = 10

VMEM_BIG = 56 * 1024 * 1024


def _cp(sem, vmem=None):
    return pltpu.CompilerParams(dimension_semantics=sem, vmem_limit_bytes=vmem)


def _row_spec(tr, c):
    return pl.BlockSpec((tr, c), lambda i: (i, 0))


def _full_spec(shape):
    nd = len(shape)
    return pl.BlockSpec(shape, lambda *_: (0,) * nd)


def _gelu(x):
    c = 0.7978845608028654
    return 0.5 * x * (1.0 + jnp.tanh(c * (x + 0.044715 * x * x * x)))


def _gelu_grad(x):
    c = 0.7978845608028654
    t = jnp.tanh(c * (x + 0.044715 * x * x * x))
    return 0.5 * (1.0 + t) + 0.5 * x * (1.0 - t * t) * c * (1.0 + 3.0 * 0.044715 * x * x)


def _silu(z):
    return z * jax.nn.sigmoid(z)


def _silu_grad(z):
    s = jax.nn.sigmoid(z)
    return s * (1.0 + z * (1.0 - s))


def _dot(a, b):
    return lax.dot_general(a, b, (((1,), (0,)), ((), ())), preferred_element_type=F32)


def _dot_nt(a, b):
    return lax.dot_general(a, b, (((1,), (1,)), ((), ())), preferred_element_type=F32)


def _dot_tn(a, b):
    return lax.dot_general(a, b, (((0,), (0,)), ((), ())), preferred_element_type=F32)


def _pick(n, cap):
    best = 0
    for d in range(128, min(n, cap) + 1, 128):
        if n % d == 0:
            best = d
    if best < 256:
        return n
    return best


def _mm(a, b, *, ta=False, tb=False, out_dtype=F32, name, after=()):
    m, k = (a.shape[1], a.shape[0]) if ta else a.shape
    n = b.shape[0] if tb else b.shape[1]
    assert (b.shape[1] if tb else b.shape[0]) == k
    tm, tn, tk = _pick(m, 1024), _pick(n, 1024), _pick(k, 2048)
    if tn > 1024:
        tm = _pick(m, 256)
    if tk > 2048:
        tm, tn = _pick(m, 256), _pick(n, 256)
    nk = k // tk
    dims = (((0 if ta else 1,), (1 if tb else 0,)), ((), ()))

    def body(a_ref, b_ref, *rest):
        o_ref, acc_ref = rest[-2], rest[-1]
        kk = pl.program_id(2)

        @pl.when(kk == 0)
        def _():
            acc_ref[...] = jnp.zeros_like(acc_ref)

        acc_ref[...] += lax.dot_general(a_ref[...].astype(BF16), b_ref[...].astype(BF16), dims,
                                        preferred_element_type=F32)

        @pl.when(kk == nk - 1)
        def _():
            o_ref[...] = acc_ref[...].astype(out_dtype)

    a_spec = pl.BlockSpec((tk, tm), lambda i, j, kk: (kk, i)) if ta else pl.BlockSpec((tm, tk), lambda i, j, kk: (i, kk))
    b_spec = pl.BlockSpec((tn, tk), lambda i, j, kk: (j, kk)) if tb else pl.BlockSpec((tk, tn), lambda i, j, kk: (kk, j))
    return pl.pallas_call(
        body, grid=(m // tm, n // tn, nk), in_specs=[a_spec, b_spec] + [pl.BlockSpec(memory_space=pl.ANY)] * len(after),
        out_specs=pl.BlockSpec((tm, tn), lambda i, j, kk: (i, j)), out_shape=S((m, n), out_dtype),
        scratch_shapes=[pltpu.VMEM((tm, tn), F32)], name=name,
        compiler_params=_cp(("parallel", "parallel", "arbitrary"), VMEM_BIG))(a, b, *after)


def _prenorm(x, g, name, after=()):
    t, d = x.shape
    tr = 256

    def body(x_ref, g_ref, *rest):
        o_ref = rest[-1]
        xv = x_ref[...]
        r = lax.rsqrt(jnp.mean(xv * xv, axis=-1, keepdims=True) + EPS)
        o_ref[...] = (xv * r * g_ref[...]).astype(BF16)

    return pl.pallas_call(body, grid=(t // tr,),
                          in_specs=[_row_spec(tr, d), _full_spec((1, d))] + [pl.BlockSpec(memory_space=pl.ANY)] * len(after),
                          out_specs=_row_spec(tr, d), out_shape=S((t, d), BF16), name=name,
                          compiler_params=_cp(("parallel",)))(x, g, *after)


def _rms_bwd_rows(dout, xin, g):
    r = lax.rsqrt(jnp.mean(xin * xin, axis=-1, keepdims=True) + EPS)
    xh = xin * r
    gd = dout * g
    dx = r * (gd - xh * jnp.mean(gd * xh, axis=-1, keepdims=True))
    return dx, jnp.sum(dout * xh, axis=0, keepdims=True)


def _postnorm_bwd(dxn, y, g, name, after=()):
    t, d = y.shape
    tr = 256

    def body(dx_ref, y_ref, g_ref, *rest):
        dy_ref, dg_ref = rest[-2], rest[-1]

        @pl.when(pl.program_id(0) == 0)
        def _():
            dg_ref[...] = jnp.zeros_like(dg_ref)

        dy, dg = _rms_bwd_rows(dx_ref[...], y_ref[...], g_ref[...])
        dy_ref[...] = dy.astype(BF16)
        dg_ref[...] += dg

    return pl.pallas_call(body, grid=(t // tr,),
                          in_specs=[_row_spec(tr, d), _row_spec(tr, d), _full_spec((1, d))]
                          + [pl.BlockSpec(memory_space=pl.ANY)] * len(after),
                          out_specs=[_row_spec(tr, d), _full_spec((1, d))],
                          out_shape=[S((t, d), BF16), S((1, d), F32)], name=name,
                          compiler_params=_cp(("arbitrary",)))(dxn, y, g, *after)


def _prenorm_bwd(dxn, dh, x, g, name):
    t, d = x.shape
    tr = 256

    def body(dxn_ref, dh_ref, x_ref, g_ref, dx_ref, dg_ref):
        @pl.when(pl.program_id(0) == 0)
        def _():
            dg_ref[...] = jnp.zeros_like(dg_ref)

        dx, dg = _rms_bwd_rows(dh_ref[...], x_ref[...], g_ref[...])
        dx_ref[...] = dxn_ref[...] + dx
        dg_ref[...] += dg

    return pl.pallas_call(body, grid=(t // tr,),
                          in_specs=[_row_spec(tr, d), _row_spec(tr, d), _row_spec(tr, d), _full_spec((1, d))],
                          out_specs=[_row_spec(tr, d), _full_spec((1, d))],
                          out_shape=[S((t, d), F32), S((1, d), F32)], name=name,
                          compiler_params=_cp(("arbitrary",)))(dxn, dh, x, g)


def _postnorm_prenorm(x, y, gpost, gpre, name):
    t, d = x.shape
    tr = 256

    def body(x_ref, y_ref, gp_ref, gn_ref, xn_ref, h_ref):
        yv = y_ref[...]
        r = lax.rsqrt(jnp.mean(yv * yv, axis=-1, keepdims=True) + EPS)
        xn = x_ref[...] + yv * r * gp_ref[...]
        xn_ref[...] = xn
        r2 = lax.rsqrt(jnp.mean(xn * xn, axis=-1, keepdims=True) + EPS)
        h_ref[...] = (xn * r2 * gn_ref[...]).astype(BF16)

    return pl.pallas_call(body, grid=(t // tr,),
                          in_specs=[_row_spec(tr, d), _row_spec(tr, d), _full_spec((1, d)), _full_spec((1, d))],
                          out_specs=[_row_spec(tr, d), _row_spec(tr, d)], out_shape=[S((t, d), F32), S((t, d), BF16)],
                          name=name, compiler_params=_cp(("parallel",)))(x, y, gpost, gpre)


def _norm_bwd_pair(dxn, dh, x, gpre, yprev, gpost, name, after=()):
    t, d = x.shape
    tr = 256

    def body(dxn_ref, dh_ref, x_ref, gn_ref, y_ref, gp_ref, *rest):
        dx_ref, dgn_ref, dy_ref, dgp_ref = rest[-4:]

        @pl.when(pl.program_id(0) == 0)
        def _():
            dgn_ref[...] = jnp.zeros_like(dgn_ref)
            dgp_ref[...] = jnp.zeros_like(dgp_ref)

        dxa, dgn = _rms_bwd_rows(dh_ref[...], x_ref[...], gn_ref[...])
        dx = dxn_ref[...] + dxa
        dx_ref[...] = dx
        dgn_ref[...] += dgn
        dy, dgp = _rms_bwd_rows(dx, y_ref[...], gp_ref[...])
        dy_ref[...] = dy.astype(BF16)
        dgp_ref[...] += dgp

    rows, vec = _row_spec(tr, d), _full_spec((1, d))
    return pl.pallas_call(body, grid=(t // tr,),
                          in_specs=[rows, rows, rows, vec, rows, vec] + [pl.BlockSpec(memory_space=pl.ANY)] * len(after),
                          out_specs=[rows, vec, rows, vec],
                          out_shape=[S((t, d), F32), S((1, d), F32), S((t, d), BF16), S((1, d), F32)], name=name,
                          compiler_params=_cp(("arbitrary",)))(dxn, dh, x, gpre, yprev, gpost, *after)


def _postnorm_loss(x, y, gpost, target):
    t, d = x.shape
    tr = 256

    def body(x_ref, y_ref, g_ref, t_ref, l_ref, dy_ref):
        @pl.when(pl.program_id(0) == 0)
        def _():
            l_ref[...] = jnp.zeros_like(l_ref)

        yv = y_ref[...]
        r = lax.rsqrt(jnp.mean(yv * yv, axis=-1, keepdims=True) + EPS)
        e = (x_ref[...] + yv * r * g_ref[...]) - t_ref[...]
        dy_ref[...] = e * (1.0 / d)
        l_ref[...] += 0.5 * jnp.sum(jnp.mean(e * e, axis=-1, keepdims=True), axis=0, keepdims=True)

    return pl.pallas_call(body, grid=(t // tr,),
                          in_specs=[_row_spec(tr, d), _row_spec(tr, d), _full_spec((1, d)), _row_spec(tr, d)],
                          out_specs=[_full_spec((1, 1)), _row_spec(tr, d)],
                          out_shape=[S((1, 1), F32), S((t, d), F32)], name="postnorm_loss",
                          compiler_params=_cp(("arbitrary",)))(x, y, gpost, target)


def _sgu_parts(proj, lg, lb, ws_ref, bst):
    a = proj[:, 0:1024]
    bb = proj[:, 1024:2048]
    z = proj[:, 2048:3072]
    u = _gelu(a)
    vp = _gelu(bb)
    mu = jnp.mean(vp, axis=-1, keepdims=True)
    xc = vp - mu
    rstd = lax.rsqrt(jnp.mean(xc * xc, axis=-1, keepdims=True) + EPS)
    xh = xc * rstd
    vn = xh * lg + lb
    row = lax.broadcasted_iota(jnp.int32, (SGU_CHUNK, SGU_CHUNK), 0)
    col = lax.broadcasted_iota(jnp.int32, (SGU_CHUNK, SGU_CHUNK), 1)
    tril = row >= col
    ws = [jnp.where(tril, ws_ref[g], 0.0).astype(BF16) for g in range(SGU_GROUPS)]
    s = jnp.concatenate(
        [_dot(ws[g], vn[:, g * 64:(g + 1) * 64].astype(BF16)) + bst[:, g:g + 1] for g in range(SGU_GROUPS)], axis=1)
    return a, bb, z, u, vp, rstd, xh, vn, ws, tril, s


def _sgu_fwd(proj, lg, lb, w_s, bst):
    t = proj.shape[0]
    tr = SGU_CHUNK

    def body(p_ref, lg_ref, lb_ref, ws_ref, bst_ref, o_ref):
        _, _, z, u, _, _, _, _, _, _, s = _sgu_parts(p_ref[...], lg_ref[...], lb_ref[...], ws_ref, bst_ref[...])
        o_ref[...] = (u * s * _silu(z)).astype(BF16)

    return pl.pallas_call(
        body, grid=(t // tr,),
        in_specs=[_row_spec(tr, 3072), _full_spec((1, 1024)), _full_spec((1, 1024)), _full_spec((16, 128, 128)),
                  _full_spec((128, 16))],
        out_specs=_row_spec(tr, 1024), out_shape=S((t, 1024), BF16), name="sgu_fwd",
        compiler_params=_cp(("parallel",)))(proj, lg, lb, w_s, bst)


def _sgu_bwd(proj, dp, lg, lb, w_s, bst):
    t = proj.shape[0]
    tr = SGU_CHUNK

    def body(p_ref, dp_ref, lg_ref, lb_ref, ws_ref, bst_ref, dproj_ref, dws_ref, dbst_ref, dlg_ref, dlb_ref):
        @pl.when(pl.program_id(0) == 0)
        def _():
            dws_ref[...] = jnp.zeros_like(dws_ref)
            dbst_ref[...] = jnp.zeros_like(dbst_ref)
            dlg_ref[...] = jnp.zeros_like(dlg_ref)
            dlb_ref[...] = jnp.zeros_like(dlb_ref)

        lgv = lg_ref[...]
        a, bb, z, u, vp, rstd, xh, vn, ws, tril, s = _sgu_parts(p_ref[...], lgv, lb_ref[...], ws_ref, bst_ref[...])
        dpv = dp_ref[...].astype(F32)
        sz = _silu(z)
        du = dpv * s * sz
        ds = dpv * u * sz
        dz = dpv * u * s * _silu_grad(z)
        dvn_parts, dbs_cols = [], []
        for g in range(SGU_GROUPS):
            dsg = ds[:, g * 64:(g + 1) * 64]
            dsg_b = dsg.astype(BF16)
            dvn_parts.append(_dot_tn(ws[g], dsg_b))
            dws_ref[g] += jnp.where(tril, _dot_nt(dsg_b, vn[:, g * 64:(g + 1) * 64].astype(BF16)), 0.0)
            dbs_cols.append(jnp.sum(dsg, axis=1, keepdims=True))
        dvn = jnp.concatenate(dvn_parts, axis=1)
        dbst_ref[...] += jnp.concatenate(dbs_cols, axis=1)
        dlg_ref[...] += jnp.sum(dvn * xh, axis=0, keepdims=True)
        dlb_ref[...] += jnp.sum(dvn, axis=0, keepdims=True)
        dxh = dvn * lgv
        dvp = rstd * (dxh - jnp.mean(dxh, axis=-1, keepdims=True) - xh * jnp.mean(dxh * xh, axis=-1, keepdims=True))
        dproj_ref[:, 0:1024] = (du * _gelu_grad(a)).astype(BF16)
        dproj_ref[:, 1024:2048] = (dvp * _gelu_grad(bb)).astype(BF16)
        dproj_ref[:, 2048:3072] = dz.astype(BF16)

    return pl.pallas_call(
        body, grid=(t // tr,),
        in_specs=[_row_spec(tr, 3072), _row_spec(tr, 1024), _full_spec((1, 1024)), _full_spec((1, 1024)),
                  _full_spec((16, 128, 128)), _full_spec((128, 16))],
        out_specs=[_row_spec(tr, 3072), _full_spec((16, 128, 128)), _full_spec((128, 16)), _full_spec((1, 1024)),
                   _full_spec((1, 1024))],
        out_shape=[S((t, 3072), BF16), S((16, 128, 128), F32), S((128, 16), F32), S((1, 1024), F32),
                   S((1, 1024), F32)],
        name="sgu_bwd", compiler_params=_cp(("arbitrary",)))(proj, dp, lg, lb, w_s, bst)


def _t5_bucket_table():
    qi = np.arange(WINDOW)[:, None]
    kj = np.arange(2 * WINDOW)[None, :]
    dist = np.maximum(qi + WINDOW - kj, 0)
    max_exact = REL_BUCKETS // 2
    dist_f = np.maximum(dist, 1).astype(np.float32)
    large = max_exact + (np.log(dist_f / np.float32(max_exact)) / np.float32(math.log(REL_MAX_DIST / max_exact))
                         * np.float32(REL_BUCKETS - max_exact)).astype(np.int32)
    large = np.minimum(large, REL_BUCKETS - 1)
    return np.where(dist < max_exact, dist, large).astype(np.int32)


def _swa_bias(rel_bias, bucket):
    def body(rb_ref, bk_ref, o_ref):
        bk = bk_ref[...]
        for h in range(SWA_HEADS):
            acc = jnp.zeros((WINDOW, 2 * WINDOW), F32)
            for b in range(REL_BUCKETS):
                acc = jnp.where(bk == b, rb_ref[b, h], acc)
            o_ref[h] = acc

    return pl.pallas_call(
        body, in_specs=[pl.BlockSpec(memory_space=pltpu.SMEM), pl.BlockSpec(memory_space=pltpu.VMEM)],
        out_specs=pl.BlockSpec(memory_space=pltpu.VMEM), out_shape=S((SWA_HEADS, WINDOW, 2 * WINDOW), F32),
        name="swa_bias")(rel_bias, bucket)


def _swa_dbias(ds_sum, dsink_rows, bucket):
    def body(ds_ref, dsk_ref, bk_ref, drb_ref, dsink_ref):
        bk = bk_ref[...]
        r = lax.broadcasted_iota(jnp.int32, (REL_BUCKETS, SWA_HEADS), 0)
        c = lax.broadcasted_iota(jnp.int32, (REL_BUCKETS, SWA_HEADS), 1)
        out = jnp.zeros((REL_BUCKETS, SWA_HEADS), F32)
        for b in range(REL_BUCKETS):
            m = bk == b
            for h in range(SWA_HEADS):
                val = jnp.sum(jnp.where(m, ds_ref[h], 0.0))
                out = jnp.where((r == b) & (c == h), val, out)
        drb_ref[...] = out
        dsink_ref[...] = jnp.sum(dsk_ref[...], axis=0, keepdims=True)

    return pl.pallas_call(
        body, out_shape=[S((REL_BUCKETS, SWA_HEADS), F32), S((1, SWA_HEADS), F32)], name="swa_dbias",
    )(ds_sum, dsink_rows, bucket)


def _swa_band(cur_ref, prev_ref):
    kband = jnp.concatenate([prev_ref[:, 0:128], cur_ref[:, 1024:1152]], axis=0).astype(BF16)
    vband = jnp.concatenate([prev_ref[:, 128:256], cur_ref[:, 1152:1280]], axis=0).astype(BF16)
    return kband, vband


SWA_ROWS = SWA_GROUP * WINDOW


def _swa_valid(n, rows=SWA_ROWS):
    qi = lax.broadcasted_iota(jnp.int32, (rows, 2 * WINDOW), 0) & (WINDOW - 1)
    kj = lax.broadcasted_iota(jnp.int32, (rows, 2 * WINDOW), 1)
    dist = qi + WINDOW - kj
    return (dist >= 0) & (dist < WINDOW) & (n * WINDOW + kj - WINDOW >= 0)


def _swa_stack(x, j):
    return jnp.concatenate([x[:, h * 64:(h + 1) * 64] for h in range(j * SWA_GROUP, (j + 1) * SWA_GROUP)], axis=0)


def _swa_unstack(parts):
    return jnp.concatenate([p[i * WINDOW:(i + 1) * WINDOW] for p in parts for i in range(SWA_GROUP)], axis=1)


def _swa_sink_col(sink_ref, j):
    return jnp.concatenate([jnp.full((WINDOW, 1), sink_ref[0, h], F32)
                            for h in range(j * SWA_GROUP, (j + 1) * SWA_GROUP)], axis=0)


def _swa_probs(qh, kh, bias_h, sink, valid):
    s = _dot_nt(qh, kh) * (HEAD_DIM ** -0.5) + bias_h
    s = jnp.where(valid, s, NEG_INF)
    m = jnp.maximum(jnp.max(s, axis=-1, keepdims=True), sink)
    e = jnp.exp(s - m)
    es = jnp.exp(sink - m)
    inv = 1.0 / (jnp.sum(e, axis=-1, keepdims=True) + es)
    return e * inv, es * inv


def _swa_in_specs():
    return [_row_spec(WINDOW, 2304),
            pl.BlockSpec((WINDOW, 256), lambda n: (jnp.maximum(n - 1, 0), 4)),
            _full_spec((SWA_HEADS, WINDOW, 2 * WINDOW)),
            pl.BlockSpec(memory_space=pltpu.SMEM)]


def _swa_fwd(proj, bias, sinks):
    t = proj.shape[0]

    def body(cur_ref, prev_ref, bias_ref, sink_ref, o_ref):
        n = pl.program_id(0)
        kband, vband = _swa_band(cur_ref, prev_ref)
        valid = _swa_valid(n, WINDOW)
        outs = []
        for h in range(SWA_HEADS):
            j = h // SWA_GROUP
            qh = cur_ref[:, h * 64:(h + 1) * 64].astype(BF16)
            p, _ = _swa_probs(qh, kband[:, j * 64:(j + 1) * 64], bias_ref[h], sink_ref[0, h], valid)
            outs.append(_dot(p.astype(BF16), vband[:, j * 64:(j + 1) * 64]))
        o_ref[...] = (jnp.concatenate(outs, axis=1) * _silu(cur_ref[:, 1280:2304])).astype(BF16)

    return pl.pallas_call(body, grid=(t // WINDOW,), in_specs=_swa_in_specs(), out_specs=_row_spec(WINDOW, 1024),
                          out_shape=S((t, 1024), BF16), name="swa_fwd",
                          compiler_params=_cp(("parallel",)))(proj, proj, bias, sinks)


def _swa_bwd(proj, dp, bias, sinks):
    t = proj.shape[0]
    nb = t // WINDOW

    def body(cur_ref, prev_ref, bias_ref, sink_ref, dp_ref, dqz_ref, dkv_ref, dss_ref, dsk_ref):
        n = pl.program_id(0)

        @pl.when(n == 0)
        def _():
            dkv_ref[...] = jnp.zeros_like(dkv_ref)
            dss_ref[...] = jnp.zeros_like(dss_ref)
            dsk_ref[...] = jnp.zeros_like(dsk_ref)

        kband, vband = _swa_band(cur_ref, prev_ref)
        valid = _swa_valid(n)
        z = cur_ref[:, 1280:2304]
        dpv = dp_ref[...].astype(F32)
        do = dpv * _silu(z)
        q = cur_ref[:, 0:1024].astype(BF16)
        dob = do.astype(BF16)
        outs, dqs, dsinks, dk, dv = [], [], [], [], []
        for j in range(SWA_HEADS // SWA_GROUP):
            heads = slice(j * SWA_GROUP, (j + 1) * SWA_GROUP)
            qs, dos = _swa_stack(q, j), _swa_stack(dob, j)
            kh = kband[:, j * 64:(j + 1) * 64]
            vh = vband[:, j * 64:(j + 1) * 64]
            p, ps = _swa_probs(qs, kh, bias_ref[heads].reshape(SWA_ROWS, 2 * WINDOW), _swa_sink_col(sink_ref, j), valid)
            pb = p.astype(BF16)
            outs.append(_dot(pb, vh))
            dpr = _dot_nt(dos, vh)
            dlt = jnp.sum(p * dpr, axis=-1, keepdims=True)
            ds = p * (dpr - dlt)
            dsk = -ps * dlt
            dsinks += [dsk[i * WINDOW:(i + 1) * WINDOW] for i in range(SWA_GROUP)]
            dss_ref[heads] += ds.reshape(SWA_GROUP, WINDOW, 2 * WINDOW)
            dsb = (ds * (HEAD_DIM ** -0.5)).astype(BF16)
            dqs.append(_dot(dsb, kh))
            dk.append(_dot_tn(dsb, qs))
            dv.append(_dot_tn(pb, dos))
        dqz_ref[:, 0:1024] = _swa_unstack(dqs).astype(BF16)
        dqz_ref[:, 1024:2048] = (dpv * _swa_unstack(outs) * _silu_grad(z)).astype(BF16)
        dsk_ref[...] += jnp.concatenate(dsinks, axis=1)
        dband = jnp.concatenate([dk[0], dk[1], dv[0], dv[1]], axis=1)
        prow = pl.multiple_of(jnp.maximum(n - 1, 0) * WINDOW, WINDOW)
        dkv_ref[pl.ds(prow, WINDOW), :] += dband[0:WINDOW]
        crow = pl.multiple_of(n * WINDOW, WINDOW)
        dkv_ref[pl.ds(crow, WINDOW), :] += dband[WINDOW:]

    return pl.pallas_call(
        body, grid=(nb,), in_specs=_swa_in_specs() + [_row_spec(WINDOW, 1024)],
        out_specs=[_row_spec(WINDOW, 2048), _full_spec((t, 256)), _full_spec((SWA_HEADS, WINDOW, 2 * WINDOW)),
                   _full_spec((WINDOW, SWA_HEADS))],
        out_shape=[S((t, 2048), BF16), S((t, 256), F32), S((SWA_HEADS, WINDOW, 2 * WINDOW), F32),
                   S((WINDOW, SWA_HEADS), F32)],
        name="swa_bwd", compiler_params=_cp(("arbitrary",)))(proj, proj, bias, sinks, dp)


MLA_Z0 = MLA_Q_RANK + MLA_KV_RANK + 128
MLA_Z1 = MLA_Z0 + 1024
MLA_PROJ = MLA_Z1 + 128


def _rope_tables(t):
    inv = ROPE_BASE ** (-jnp.arange(0, MLA_ROPE, 2, dtype=F32) / MLA_ROPE)
    ang = jnp.arange(t, dtype=F32)[:, None] * inv[None, :]
    return jnp.tile(jnp.cos(ang), (1, MLA_HEADS)), jnp.tile(jnp.sin(ang), (1, MLA_HEADS))


def _mla_norms(proj, gq, gkv):
    t = proj.shape[0]
    tr = 256

    def body(p_ref, gq_ref, gkv_ref, q_ref, kv_ref):
        for lo, hi, g_ref, o_ref in ((0, 768, gq_ref, q_ref), (768, 1024, gkv_ref, kv_ref)):
            xv = p_ref[:, lo:hi]
            r = lax.rsqrt(jnp.mean(xv * xv, axis=-1, keepdims=True) + EPS)
            o_ref[...] = (xv * r * g_ref[...]).astype(BF16)

    return pl.pallas_call(
        body, grid=(t // tr,), in_specs=[_row_spec(tr, 1024), _full_spec((1, 768)), _full_spec((1, 256))],
        out_specs=[_row_spec(tr, 768), _row_spec(tr, 256)], out_shape=[S((t, 768), BF16), S((t, 256), BF16)],
        name="mla_norms", compiler_params=_cp(("parallel",)))(proj, gq, gkv)


def _mla_norms_bwd(proj, dcqn, dckvn, dkr, dz, gq, gkv):
    t = proj.shape[0]
    tr = 256

    def body(p_ref, dq_ref, dkv_ref, dkr_ref, dz_ref, gq_ref, gkv_ref, dproj_ref, dgq_ref, dgkv_ref):
        @pl.when(pl.program_id(0) == 0)
        def _():
            dgq_ref[...] = jnp.zeros_like(dgq_ref)
            dgkv_ref[...] = jnp.zeros_like(dgkv_ref)

        dx, dg = _rms_bwd_rows(dq_ref[...], p_ref[:, 0:768], gq_ref[...])
        dproj_ref[:, 0:768] = dx.astype(BF16)
        dgq_ref[...] += dg
        dx, dg = _rms_bwd_rows(dkv_ref[...], p_ref[:, 768:1024], gkv_ref[...])
        dproj_ref[:, 768:1024] = dx.astype(BF16)
        dgkv_ref[...] += dg
        dproj_ref[:, 1024:MLA_Z0] = dkr_ref[...].astype(BF16)
        dproj_ref[:, MLA_Z0:MLA_Z1] = dz_ref[...]
        dproj_ref[:, MLA_Z1:MLA_PROJ] = jnp.zeros((tr, MLA_PROJ - MLA_Z1), BF16)

    return pl.pallas_call(
        body, grid=(t // tr,),
        in_specs=[_row_spec(tr, 1024), _row_spec(tr, 768), _row_spec(tr, 256), _row_spec(tr, 128), _row_spec(tr, 1024),
                  _full_spec((1, 768)), _full_spec((1, 256))],
        out_specs=[_row_spec(tr, MLA_PROJ), _full_spec((1, 768)), _full_spec((1, 256))],
        out_shape=[S((t, MLA_PROJ), BF16), S((1, 768), F32), S((1, 256), F32)],
        name="mla_norms_bwd", compiler_params=_cp(("arbitrary",)))(proj, dcqn, dckvn, dkr, dz, gq, gkv)


def _mla_pack(qf, kvf, proj, cos, sin):
    t = qf.shape[0]
    tr = 256

    def body(q_ref, kn_ref, kr_ref, cos_ref, sin_ref, qh_ref, kh_ref):
        cs, sn = cos_ref[...], sin_ref[...]
        x1, x2 = q_ref[:, 1024:1280], q_ref[:, 1280:1536]
        r1 = x1 * cs - x2 * sn
        r2 = x2 * cs + x1 * sn
        c16, s16 = cs[:, 0:16], sn[:, 0:16]
        k1, k2 = kr_ref[:, 0:16], kr_ref[:, 16:32]
        kr1 = k1 * c16 - k2 * s16
        kr2 = k2 * c16 + k1 * s16
        zpad = jnp.zeros((tr, 32), F32)
        for h in range(MLA_HEADS):
            qh_ref[h] = jnp.concatenate([q_ref[:, h * 64:(h + 1) * 64], r1[:, h * 16:(h + 1) * 16],
                                         r2[:, h * 16:(h + 1) * 16], zpad], axis=1).astype(BF16)
            kh_ref[h] = jnp.concatenate([kn_ref[:, h * 64:(h + 1) * 64], kr1, kr2, zpad], axis=1).astype(BF16)

    hspec = pl.BlockSpec((MLA_HEADS, tr, 128), lambda i: (0, i, 0))
    return pl.pallas_call(
        body, grid=(t // tr,),
        in_specs=[_row_spec(tr, 1536), _row_spec(tr, 1024), pl.BlockSpec((tr, 128), lambda i: (i, 8)),
                  _row_spec(tr, 256), _row_spec(tr, 256)],
        out_specs=[hspec, hspec], out_shape=[S((MLA_HEADS, t, 128), BF16)] * 2,
        name="mla_pack", compiler_params=_cp(("parallel",)))(qf, kvf, proj, cos, sin)


def _mla_unpack(dqh, dkh, dv, cos, sin):
    t = dqh.shape[1]
    tr = 256

    def body(dqh_ref, dkh_ref, dv_ref, cos_ref, sin_ref, dq_ref, dkv_ref, dkr_ref):
        cs, sn = cos_ref[...], sin_ref[...]
        dqn = jnp.concatenate([dqh_ref[h, :, 0:64] for h in range(MLA_HEADS)], axis=1)
        dr1 = jnp.concatenate([dqh_ref[h, :, 64:80] for h in range(MLA_HEADS)], axis=1)
        dr2 = jnp.concatenate([dqh_ref[h, :, 80:96] for h in range(MLA_HEADS)], axis=1)
        dq_ref[:, 0:1024] = dqn.astype(BF16)
        dq_ref[:, 1024:1280] = (dr1 * cs + dr2 * sn).astype(BF16)
        dq_ref[:, 1280:1536] = (dr2 * cs - dr1 * sn).astype(BF16)
        dkv_ref[:, 0:1024] = jnp.concatenate([dkh_ref[h, :, 0:64] for h in range(MLA_HEADS)], axis=1).astype(BF16)
        dkv_ref[:, 1024:2048] = dv_ref[...].astype(BF16)
        d1 = dkh_ref[0, :, 64:80]
        d2 = dkh_ref[0, :, 80:96]
        for h in range(1, MLA_HEADS):
            d1 = d1 + dkh_ref[h, :, 64:80]
            d2 = d2 + dkh_ref[h, :, 80:96]
        c16, s16 = cs[:, 0:16], sn[:, 0:16]
        dkr_ref[...] = jnp.concatenate([d1 * c16 + d2 * s16, d2 * c16 - d1 * s16, jnp.zeros((tr, 96), F32)], axis=1)

    hspec = pl.BlockSpec((MLA_HEADS, tr, 128), lambda i: (0, i, 0))
    return pl.pallas_call(
        body, grid=(t // tr,), in_specs=[hspec, hspec, _row_spec(tr, 1024), _row_spec(tr, 256), _row_spec(tr, 256)],
        out_specs=[_row_spec(tr, 1536), _row_spec(tr, 2048), _row_spec(tr, 128)],
        out_shape=[S((t, 1536), BF16), S((t, 2048), BF16), S((t, 128), F32)],
        name="mla_unpack", compiler_params=_cp(("parallel",)))(dqh, dkh, dv, cos, sin)


def _mla_attn_fwd(qh, kh, kvf, proj):
    t = qh.shape[1]
    b, bk = MLA_BQ, MLA_BK

    def body(q_ref, k_ref, v_ref, z_ref, o_ref, p_ref, lse_ref):
        r0 = pl.program_id(1) * b
        nfull = r0 // bk
        diff = (lax.broadcasted_iota(jnp.int32, (b, bk), 1) - lax.broadcasted_iota(jnp.int32, (b, bk), 0))
        qs = [q_ref[0], q_ref[1]]

        def block(c0, carry, masked):
            new = []
            for a in range(2):
                m, l, acc = carry[a]
                k = k_ref[a, pl.ds(c0, bk), :]
                v = v_ref[pl.ds(c0, bk), a * 64:(a + 1) * 64].astype(BF16)
                s = _dot_nt(qs[a], k) * MLA_SCALE
                if masked:
                    s = jnp.where(diff <= r0 - c0, s, NEG_INF)
                mn = jnp.maximum(m, jnp.max(s, axis=-1, keepdims=True))
                alpha = jnp.exp(m - mn)
                p = jnp.exp(s - mn)
                new.append((mn, alpha * l + jnp.sum(p, axis=-1, keepdims=True), alpha * acc + _dot(p.astype(BF16), v)))
            return tuple(new)

        init = (jnp.full((b, 1), NEG_INF, F32), jnp.zeros((b, 1), F32), jnp.zeros((b, 64), F32))
        carry = lax.fori_loop(0, nfull, lambda c, cr: block(pl.multiple_of(c * bk, bk), cr, False), (init, init))
        for i in range(max(1, b // bk)):
            carry = block(pl.multiple_of((nfull + i) * bk, bk), carry, True)
        outs = [acc / l for _, l, acc in carry]
        lses = [jnp.broadcast_to(m + jnp.log(l), (b, 64)) for m, l, _ in carry]
        o = jnp.concatenate(outs, axis=1)
        o_ref[...] = o
        p_ref[...] = (o * _silu(z_ref[...])).astype(BF16)
        lse_ref[0] = jnp.concatenate(lses, axis=1)

    return pl.pallas_call(
        body, grid=(MLA_HEADS // 2, t // b),
        in_specs=[pl.BlockSpec((2, b, 128), lambda i, r: (i, r, 0)), pl.BlockSpec((2, t, 128), lambda i, r: (i, 0, 0)),
                  pl.BlockSpec((t, 128), lambda i, r: (0, 8 + i)), pl.BlockSpec((b, 128), lambda i, r: (r, 9 + i))],
        out_specs=[pl.BlockSpec((b, 128), lambda i, r: (r, i)), pl.BlockSpec((b, 128), lambda i, r: (r, i)),
                   pl.BlockSpec((1, b, 128), lambda i, r: (i, r, 0))],
        out_shape=[S((t, 1024), F32), S((t, 1024), BF16), S((MLA_HEADS // 2, t, 128), F32)],
        name="mla_attn_fwd", compiler_params=_cp(("parallel", "parallel")))(qh, kh, kvf, proj)


def _mla_attn_bwd(qh, kh, kvf, proj, o, dpout, lse):
    t = qh.shape[1]
    b, bk = MLA_BQ, MLA_BK
    nb = t // b

    def body(q_ref, k_ref, v_ref, z_ref, o_ref, dp_ref, lse_ref, dq_ref, dk_ref, dv_ref, dz_ref):
        dk_ref[...] = jnp.zeros_like(dk_ref)
        dv_ref[...] = jnp.zeros_like(dv_ref)
        diff = (lax.broadcasted_iota(jnp.int32, (b, bk), 1) - lax.broadcasted_iota(jnp.int32, (b, bk), 0))
        lanes = [slice(0, 64), slice(64, 128)]

        def rbody(r, carry):
            r0 = pl.multiple_of(r * b, b)
            rows = pl.ds(r0, b)
            qs, dobs, deltas, lses = [], [], [], []
            for a in range(2):
                zz = z_ref[rows, lanes[a]]
                oo = o_ref[rows, lanes[a]]
                dpo = dp_ref[rows, lanes[a]].astype(F32)
                do = dpo * _silu(zz)
                dz_ref[rows, lanes[a]] = (dpo * oo * _silu_grad(zz)).astype(BF16)
                qs.append(q_ref[a, rows, :])
                dobs.append(do.astype(BF16))
                deltas.append(jnp.sum(do * oo, axis=-1, keepdims=True))
                lses.append(lse_ref[0, rows, a * 64:a * 64 + 1])

            def block(c0, dqs, masked):
                cols = pl.ds(c0, bk)
                out = []
                for a in range(2):
                    k = k_ref[a, cols, :]
                    v = v_ref[cols, lanes[a]].astype(BF16)
                    s = _dot_nt(qs[a], k) * MLA_SCALE
                    if masked:
                        s = jnp.where(diff <= r0 - c0, s, NEG_INF)
                    p = jnp.exp(s - lses[a])
                    ds = p * (_dot_nt(dobs[a], v) - deltas[a]) * MLA_SCALE
                    dsb = ds.astype(BF16)
                    dk_ref[a, cols, :] += _dot_tn(dsb, qs[a])
                    dv_ref[cols, lanes[a]] += _dot_tn(p.astype(BF16), dobs[a])
                    out.append(dqs[a] + _dot(dsb, k))
                return tuple(out)

            zero = jnp.zeros((b, 128), F32)
            nfull = r0 // bk
            dqs = lax.fori_loop(0, nfull, lambda c, d: block(pl.multiple_of(c * bk, bk), d, False), (zero, zero))
            for i in range(max(1, b // bk)):
                dqs = block(pl.multiple_of((nfull + i) * bk, bk), dqs, True)
            dq_ref[0, rows, :] = dqs[0]
            dq_ref[1, rows, :] = dqs[1]
            return carry

        lax.fori_loop(0, nb, rbody, 0)

    pair3 = pl.BlockSpec((2, t, 128), lambda i: (i, 0, 0))
    return pl.pallas_call(
        body, grid=(MLA_HEADS // 2,),
        in_specs=[pair3, pair3, pl.BlockSpec((t, 128), lambda i: (0, 8 + i)), pl.BlockSpec((t, 128), lambda i: (0, 9 + i)),
                  pl.BlockSpec((t, 128), lambda i: (0, i)), pl.BlockSpec((t, 128), lambda i: (0, i)),
                  pl.BlockSpec((1, t, 128), lambda i: (i, 0, 0))],
        out_specs=[pair3, pair3, pl.BlockSpec((t, 128), lambda i: (0, i)), pl.BlockSpec((t, 128), lambda i: (0, i))],
        out_shape=[S((MLA_HEADS, t, 128), F32), S((MLA_HEADS, t, 128), F32), S((t, 1024), F32), S((t, 1024), BF16)],
        name="mla_attn_bwd", compiler_params=_cp(("parallel",), VMEM_BIG))(qh, kh, kvf, proj, o, dpout, lse)


def _dot_nt_hi(a, b):
    return lax.dot_general(a, b, (((1,), (1,)), ((), ())), preferred_element_type=F32,
                           precision=lax.Precision.HIGHEST)


SSM_PACK = 2


def _ssm_gen(lr, li, ldt, btr, bti, cr, ci):
    n = SSM_BLOCK
    lanes = SSM_PACK * SSM_STATE
    dt = jnp.exp(ldt)
    x = lr * dt
    w = li * dt
    mag = jnp.exp(x)
    ab_re = mag * jnp.cos(w)
    ab_im = mag * jnp.sin(w)
    den = lr * lr + li * li
    nr = ab_re - 1.0
    f_re = (nr * lr + ab_im * li) / den
    f_im = (ab_im * lr - nr * li) / den
    bb_re = f_re * btr - f_im * bti
    bb_im = f_re * bti + f_im * btr
    inv_mag = jnp.exp(-x)
    inv_re, inv_im = inv_mag * jnp.cos(w), -inv_mag * jnp.sin(w)
    one, zero = jnp.ones_like(x), jnp.zeros_like(x)
    pos, neg = [(one, zero)], [(one, zero)]
    for _ in range(SSM_CHUNK):
        pr, pi = pos[-1]
        pos.append((pr * ab_re - pi * ab_im, pr * ab_im + pi * ab_re))
        pr, pi = neg[-1]
        neg.append((pr * inv_re - pi * inv_im, pr * inv_im + pi * inv_re))

    def stack(table, idx):
        return (jnp.concatenate([jnp.broadcast_to(table[idx(t)][0], (SSM_GROUP, lanes)) for t in range(SSM_CHUNK)], axis=0),
                jnp.concatenate([jnp.broadcast_to(table[idx(t)][1], (SSM_GROUP, lanes)) for t in range(SSM_CHUNK)], axis=0))

    def rep(v):
        return jnp.concatenate([v] * SSM_CHUNK, axis=0)

    ct_re, ct_im, bt_re, bt_im = rep(cr), rep(ci), rep(bb_re), rep(bb_im)
    p_re, p_im = stack(pos, lambda t: t)
    l_re = ct_re * p_re - ct_im * p_im
    l_im = ct_re * p_im + ct_im * p_re
    n_re, n_im = stack(neg, lambda t: t)
    r_re = bt_re * n_re - bt_im * n_im
    r_im = bt_re * n_im + bt_im * n_re
    rk = lax.broadcasted_iota(jnp.int32, (n, n), 0) // SSM_GROUP
    ct = lax.broadcasted_iota(jnp.int32, (n, n), 1) // SSM_GROUP
    mts = []
    for g in range(SSM_PACK):
        sl = slice(g * SSM_STATE, (g + 1) * SSM_STATE)
        mts.append(jnp.where(ct >= rk, _dot_nt_hi(r_re[:, sl], l_re[:, sl]) - _dot_nt_hi(r_im[:, sl], l_im[:, sl]), 0.0))
    e_re, e_im = stack(pos, lambda t: SSM_CHUNK - 1 - t)
    pin_re = bt_re * e_re - bt_im * e_im
    pin_im = bt_re * e_im + bt_im * e_re
    q_re, q_im = stack(pos, lambda t: t + 1)
    qt_re = ct_re * q_re - ct_im * q_im
    qt_im = -(ct_re * q_im + ct_im * q_re)
    ad_re, ad_im = pos[SSM_CHUNK]
    return tuple(mts), pin_re, pin_im, qt_re, qt_im, ad_re, ad_im


_SSM_GEN_OUT = [(SSM_BLOCK, SSM_BLOCK)] + [(SSM_BLOCK, SSM_STATE)] * 4 + [(1, SSM_STATE)] * 2
_SSM_GEN_IN = [(1, SSM_STATE)] * 3 + [(SSM_GROUP, SSM_STATE)] * 4


def _gspec(shape2):
    return pl.BlockSpec((SSM_PACK,) + shape2, lambda g: (g, 0, 0))


def _pack_lanes(ref):
    return jnp.concatenate([ref[g] for g in range(SSM_PACK)], axis=1)


def _unpack_lanes(ref, val):
    for g in range(SSM_PACK):
        ref[g] = val[:, g * SSM_STATE:(g + 1) * SSM_STATE]


def _ssm_operators(params):
    def body(*refs):
        ins, outs = refs[:7], refs[7:]
        vals = _ssm_gen(*[_pack_lanes(r) for r in ins])
        for g in range(SSM_PACK):
            outs[0][g] = vals[0][g]
        for o_ref, val in zip(outs[1:], vals[1:]):
            _unpack_lanes(o_ref, val)

    return pl.pallas_call(
        body, grid=(SSM_GROUPS // SSM_PACK,), in_specs=[_gspec(s) for s in _SSM_GEN_IN],
        out_specs=[_gspec(s) for s in _SSM_GEN_OUT],
        out_shape=[S((SSM_GROUPS,) + s, F32) for s in _SSM_GEN_OUT], name="ssm_operators",
        compiler_params=_cp(("parallel",)))(*params)


def _ssm_operators_bwd(params, cots, after=()):
    def body(*refs):
        ins, cts, outs = refs[:7], refs[7:14], refs[14 + len(after):]
        _, vjp = jax.vjp(_ssm_gen, *[_pack_lanes(r) for r in ins])
        grads = list(vjp((tuple(cts[0][g] for g in range(SSM_PACK)),) + tuple(_pack_lanes(r) for r in cts[1:])))
        for o_ref, val in zip(outs, grads):
            _unpack_lanes(o_ref, val)
        for g in range(SSM_PACK):
            dldt = grads[2][:, g * SSM_STATE:(g + 1) * SSM_STATE]
            outs[2][g] = jnp.broadcast_to(jnp.sum(dldt, axis=-1, keepdims=True), (1, SSM_STATE))

    return pl.pallas_call(
        body, grid=(SSM_GROUPS // SSM_PACK,),
        in_specs=[_gspec(s) for s in _SSM_GEN_IN] + [_gspec(s) for s in _SSM_GEN_OUT]
        + [pl.BlockSpec(memory_space=pl.ANY)] * len(after),
        out_specs=[_gspec(s) for s in _SSM_GEN_IN], out_shape=[S((SSM_GROUPS,) + s, F32) for s in _SSM_GEN_IN],
        name="ssm_operators_bwd", compiler_params=_cp(("parallel",)))(*params, *cots, *after)


def _shift_rows(x, sh, row, up):
    n = x.shape[0]
    if up:
        return jnp.where(row < n - sh, pltpu.roll(x, n - sh, axis=0), 0.0)
    return jnp.where(row >= sh, pltpu.roll(x, sh, axis=0), 0.0)


def _carry_scan(sr, si, ar, ai, row, up):
    n = sr.shape[0]
    sh = 1
    while sh < n:
        tr, ti = _shift_rows(sr, sh, row, up), _shift_rows(si, sh, row, up)
        sr, si = sr + ar * tr - ai * ti, si + ar * ti + ai * tr
        ar, ai = ar * ar - ai * ai, 2.0 * ar * ai
        sh *= 2
    return sr, si


def _ssm_states(u, p_re, p_im, ar, ai, row):
    lre = _dot(u, p_re.astype(BF16))
    lim = _dot(u, p_im.astype(BF16))
    sr, si = _carry_scan(lre, lim, ar, ai, row, False)
    return _shift_rows(sr, 1, row, False), _shift_rows(si, 1, row, False)


SSM_GPB = 128 // SSM_GROUP


def _step_rows(ref, nc):
    return [ref[pl.ds(k, nc, stride=SSM_CHUNK), :].astype(BF16) for k in range(SSM_CHUNK)]


def _group_of(rows, j):
    return jnp.concatenate([r[:, j * SSM_GROUP:(j + 1) * SSM_GROUP] for r in rows], axis=1)


def _store_groups(o_ref, per_group, nc):
    for k in range(SSM_CHUNK):
        o_ref[pl.ds(k, nc, stride=SSM_CHUNK), :] = jnp.concatenate(
            [y[:, k * SSM_GROUP:(k + 1) * SSM_GROUP] for y in per_group], axis=1)


def _bspec(shape2):
    return pl.BlockSpec((SSM_GPB,) + shape2, lambda j: (j, 0, 0))


def _ssm_core_fwd(proj, ops):
    t = proj.shape[0]
    nc = t // SSM_CHUNK

    def body(u_ref, mt_ref, pr_ref, pi_ref, qr_ref, qi_ref, ar_ref, ai_ref, y_ref):
        rows = _step_rows(u_ref, nc)
        row = lax.broadcasted_iota(jnp.int32, (nc, SSM_STATE), 0)
        ys = []
        for j in range(SSM_GPB):
            u = _group_of(rows, j)
            s_re, s_im = _ssm_states(u, pr_ref[j], pi_ref[j], ar_ref[j], ai_ref[j], row)
            ys.append(_dot(u, mt_ref[j].astype(BF16)) + _dot_nt(s_re.astype(BF16), qr_ref[j].astype(BF16))
                      + _dot_nt(s_im.astype(BF16), qi_ref[j].astype(BF16)))
        _store_groups(y_ref, ys, nc)

    blk = pl.BlockSpec((t, 128), lambda j: (0, j))
    return pl.pallas_call(
        body, grid=(SSM_GROUPS // SSM_GPB,), in_specs=[blk] + [_bspec(s) for s in _SSM_GEN_OUT], out_specs=blk,
        out_shape=S((t, 1024), F32), name="ssm_core_fwd", compiler_params=_cp(("parallel",)))(proj, *ops)


def _ssm_core_bwd(proj, dy, ops):
    t = proj.shape[0]
    nc = t // SSM_CHUNK

    def body(u_ref, dy_ref, mt_ref, pr_ref, pi_ref, qr_ref, qi_ref, ar_ref, ai_ref,
             du_ref, dmt_ref, dpr_ref, dpi_ref, dqr_ref, dqi_ref, dar_ref, dai_ref):
        u_rows = _step_rows(u_ref, nc)
        dy_rows = _step_rows(dy_ref, nc)
        row = lax.broadcasted_iota(jnp.int32, (nc, SSM_STATE), 0)
        dus = []
        for j in range(SSM_GPB):
            u = _group_of(u_rows, j)
            dyj = _group_of(dy_rows, j)
            ar, ai = ar_ref[j], ai_ref[j]
            s_re, s_im = _ssm_states(u, pr_ref[j], pi_ref[j], ar, ai, row)
            g_re = _shift_rows(_dot(dyj, qr_ref[j].astype(BF16)), 1, row, True)
            g_im = _shift_rows(_dot(dyj, qi_ref[j].astype(BF16)), 1, row, True)
            l_re, l_im = _carry_scan(g_re, g_im, ar, -ai, row, True)
            lrb, lib = l_re.astype(BF16), l_im.astype(BF16)
            dus.append(_dot_nt(dyj, mt_ref[j].astype(BF16)) + _dot_nt(lrb, pr_ref[j].astype(BF16))
                       + _dot_nt(lib, pi_ref[j].astype(BF16)))
            dmt_ref[j] = _dot_tn(u, dyj)
            dpr_ref[j] = _dot_tn(u, lrb)
            dpi_ref[j] = _dot_tn(u, lib)
            dqr_ref[j] = _dot_tn(dyj, s_re.astype(BF16))
            dqi_ref[j] = _dot_tn(dyj, s_im.astype(BF16))
            dar_ref[j] = jnp.sum(l_re * s_re + l_im * s_im, axis=0, keepdims=True)
            dai_ref[j] = jnp.sum(l_im * s_re - l_re * s_im, axis=0, keepdims=True)
        _store_groups(du_ref, dus, nc)

    blk = pl.BlockSpec((t, 128), lambda j: (0, j))
    return pl.pallas_call(
        body, grid=(SSM_GROUPS // SSM_GPB,), in_specs=[blk, blk] + [_bspec(s) for s in _SSM_GEN_OUT],
        out_specs=[blk] + [_bspec(s) for s in _SSM_GEN_OUT],
        out_shape=[S((t, 1024), F32)] + [S((SSM_GROUPS,) + s, F32) for s in _SSM_GEN_OUT],
        name="ssm_core_bwd", compiler_params=_cp(("parallel",)))(proj, dy, *ops)


def _s5_act(y_core, proj, d_skip):
    t = y_core.shape[0]
    tr = 256

    def body(y_ref, u_ref, d_ref, o_ref):
        o_ref[...] = _gelu(y_ref[...] + d_ref[...] * u_ref[...])

    return pl.pallas_call(body, grid=(t // tr,), in_specs=[_row_spec(tr, 1024), _row_spec(tr, 1024), _full_spec((1, 1024))],
                          out_specs=_row_spec(tr, 1024), out_shape=S((t, 1024), F32), name="s5_act",
                          compiler_params=_cp(("parallel",)))(y_core, proj, d_skip)


def _s5_gate(yg, tg, b_glu, proj):
    t = yg.shape[0]
    tr = 256

    def body(y_ref, t_ref, b_ref, z_ref, o_ref):
        o_ref[...] = (y_ref[...] * jax.nn.sigmoid(t_ref[...] + b_ref[...]) * _silu(z_ref[...])).astype(BF16)

    return pl.pallas_call(
        body, grid=(t // tr,),
        in_specs=[_row_spec(tr, 1024), _row_spec(tr, 1024), _full_spec((1, 1024)), pl.BlockSpec((tr, 1024), lambda i: (i, 1))],
        out_specs=_row_spec(tr, 1024), out_shape=S((t, 1024), BF16), name="s5_gate",
        compiler_params=_cp(("parallel",)))(yg, tg, b_glu, proj)


def _s5_gate_bwd(dp, yg, tg, b_glu, proj):
    t = yg.shape[0]
    tr = 256

    def body(dp_ref, y_ref, t_ref, b_ref, z_ref, dt_ref, dy_ref, dz_ref, db_ref):
        @pl.when(pl.program_id(0) == 0)
        def _():
            db_ref[...] = jnp.zeros_like(db_ref)

        z = z_ref[...]
        yv = y_ref[...]
        sg = jax.nn.sigmoid(t_ref[...] + b_ref[...])
        dpv = dp_ref[...].astype(F32)
        do = dpv * _silu(z)
        dz_ref[...] = (dpv * yv * sg * _silu_grad(z)).astype(BF16)
        dy_ref[...] = do * sg
        dtv = do * yv * sg * (1.0 - sg)
        dt_ref[...] = dtv.astype(BF16)
        db_ref[...] += jnp.sum(dtv, axis=0, keepdims=True)

    return pl.pallas_call(
        body, grid=(t // tr,),
        in_specs=[_row_spec(tr, 1024), _row_spec(tr, 1024), _row_spec(tr, 1024), _full_spec((1, 1024)),
                  pl.BlockSpec((tr, 1024), lambda i: (i, 1))],
        out_specs=[_row_spec(tr, 1024), _row_spec(tr, 1024), _row_spec(tr, 1024), _full_spec((1, 1024))],
        out_shape=[S((t, 1024), BF16), S((t, 1024), F32), S((t, 1024), BF16), S((1, 1024), F32)],
        name="s5_gate_bwd", compiler_params=_cp(("arbitrary",)))(dp, yg, tg, b_glu, proj)


def _s5_act_bwd(dyg1, dyg2, y_core, proj, d_skip):
    t = y_core.shape[0]
    tr = 256

    def body(a_ref, b_ref, y_ref, u_ref, d_ref, dy_ref, du_ref, dd_ref):
        @pl.when(pl.program_id(0) == 0)
        def _():
            dd_ref[...] = jnp.zeros_like(dd_ref)

        u = u_ref[...]
        dv = (a_ref[...] + b_ref[...]) * _gelu_grad(y_ref[...] + d_ref[...] * u)
        dy_ref[...] = dv
        du_ref[...] = dv * d_ref[...]
        dd_ref[...] += jnp.sum(dv * u, axis=0, keepdims=True)

    return pl.pallas_call(
        body, grid=(t // tr,),
        in_specs=[_row_spec(tr, 1024)] * 4 + [_full_spec((1, 1024))],
        out_specs=[_row_spec(tr, 1024), _row_spec(tr, 1024), _full_spec((1, 1024))],
        out_shape=[S((t, 1024), F32), S((t, 1024), F32), S((1, 1024), F32)],
        name="s5_act_bwd", compiler_params=_cp(("arbitrary",)))(dyg1, dyg2, y_core, proj, d_skip)


def _s5_dproj(du_skip, du_core, dz):
    t = du_skip.shape[0]
    tr = 256

    def body(a_ref, b_ref, z_ref, o_ref):
        o_ref[:, 0:1024] = (a_ref[...] + b_ref[...]).astype(BF16)
        o_ref[:, 1024:2048] = z_ref[...]

    return pl.pallas_call(body, grid=(t // tr,), in_specs=[_row_spec(tr, 1024)] * 3, out_specs=_row_spec(tr, 2048),
                          out_shape=S((t, 2048), BF16), name="s5_dproj",
                          compiler_params=_cp(("parallel",)))(du_skip, du_core, dz)


def _s5_params(lam_re, lam_im, log_dt, b_re, b_im, c_re, c_im):
    g = SSM_GROUPS
    return (lam_re.reshape(g, 1, SSM_STATE), lam_im.reshape(g, 1, SSM_STATE),
            jnp.broadcast_to(log_dt.reshape(g, 1, 1), (g, 1, SSM_STATE)),
            b_re.transpose(0, 2, 1), b_im.transpose(0, 2, 1), c_re, c_im)


def _s5_fwd(proj, params, d_skip, w_glu_after, b_glu):
    ops = _ssm_operators(params)
    y_core = _ssm_core_fwd(proj, ops)
    yg = _s5_act(y_core, proj, d_skip)
    tg = _mm(yg, w_glu_after(yg), name="s5_glu")
    return _s5_gate(yg, tg, b_glu, proj), (ops, y_core, yg, tg)


def _s5_bwd(proj, res, dp, params, d_skip, w_glu, b_glu):
    ops, y_core, yg, tg = res
    dtg, dyg1, dz, db_glu = _s5_gate_bwd(dp, yg, tg, b_glu, proj)
    dyg2 = _mm(dtg, w_glu, tb=True, name="s5_glu_dx")
    dw_glu = _mm(yg, dtg, ta=True, out_dtype=BF16, name="s5_glu_dw")
    dy_core, du_skip, dd = _s5_act_bwd(dyg1, dyg2, y_core, proj, d_skip)
    outs = _ssm_core_bwd(proj, dy_core, ops)
    dproj = _s5_dproj(du_skip, outs[0], dz)
    return dproj, outs[1:], dd, dw_glu, db_glu


def _s5_param_grads(params, cots, after=()):
    glr, gli, gdt, gbtr, gbti, gcr, gci = _ssm_operators_bwd(params, cots, after)
    g = SSM_GROUPS
    return (glr.reshape(g, SSM_STATE), gli.reshape(g, SSM_STATE), gdt[:, 0, 0].reshape(1, g), gbtr, gbti, gcr, gci)


def _perm_uq(w):
    w3 = w.reshape(w.shape[0], MLA_HEADS, MLA_NOPE + MLA_ROPE)
    return jnp.concatenate([w3[:, :, :64].reshape(-1, 1024), w3[:, :, 64:80].reshape(-1, 256),
                            w3[:, :, 80:96].reshape(-1, 256)], axis=1)


def _unperm_uq(w):
    r = w.shape[0]
    return jnp.concatenate([w[:, :1024].reshape(r, 16, 64), w[:, 1024:1280].reshape(r, 16, 16),
                            w[:, 1280:1536].reshape(r, 16, 16)], axis=2).reshape(r, 1536)


def _perm_ukv(w):
    w3 = w.reshape(w.shape[0], MLA_HEADS, 128)
    return jnp.concatenate([w3[:, :, :64].reshape(-1, 1024), w3[:, :, 64:].reshape(-1, 1024)], axis=1)


def _unperm_ukv(w):
    r = w.shape[0]
    return jnp.concatenate([w[:, :1024].reshape(r, 16, 64), w[:, 1024:].reshape(r, 16, 64)], axis=2).reshape(r, 2048)


def _mla_fwd(proj, gq, gkv, wuq, wukv, cos, sin):
    cqn, ckvn = _mla_norms(proj, gq, gkv)
    qf = _mm(cqn, wuq, name="mla_uq")
    kvf = _mm(ckvn, wukv, name="mla_ukv")
    qh, kh = _mla_pack(qf, kvf, proj, cos, sin)
    o, pout, lse = _mla_attn_fwd(qh, kh, kvf, proj)
    return pout, (cqn, ckvn, kvf, qh, kh, o, lse)


def _mla_bwd(proj, res, dpout, gq, gkv, wuq, wukv, cos, sin):
    cqn, ckvn, kvf, qh, kh, o, lse = res
    dqh, dkh, dv, dz = _mla_attn_bwd(qh, kh, kvf, proj, o, dpout, lse)
    dqf, dkvf, dkr = _mla_unpack(dqh, dkh, dv, cos, sin)
    dcqn = _mm(dqf, wuq, tb=True, name="mla_uq_dx")
    dwuq = _mm(cqn, dqf, ta=True, out_dtype=BF16, name="mla_uq_dw")
    dckvn = _mm(dkvf, wukv, tb=True, name="mla_ukv_dx")
    dwukv = _mm(ckvn, dkvf, ta=True, out_dtype=BF16, name="mla_ukv_dw")
    dproj, dgq, dgkv = _mla_norms_bwd(proj, dcqn, dckvn, dkr, dz, gq, gkv)
    return dproj, dgq, dgkv, dwuq, dwukv


_ANY = pl.BlockSpec(memory_space=pl.ANY)


def _chip_peers():
    x, y, c = lax.axis_index("x"), lax.axis_index("y"), lax.axis_index("c")
    return 2 * x + y, c, [(1 - x, y), (x, 1 - y), (1 - x, 1 - y)]


def _sibling_swap(ps, name):
    n = len(ps)

    def body(*refs):
        p_refs, q_refs, send_sems, recv_sems = refs[:n], refs[n:2 * n], refs[2 * n], refs[2 * n + 1]
        x, y, c = lax.axis_index("x"), lax.axis_index("y"), lax.axis_index("c")
        cps = [pltpu.make_async_remote_copy(src_ref=p_refs[i], dst_ref=q_refs[i], send_sem=send_sems.at[i],
                                            recv_sem=recv_sems.at[i], device_id=(x, y, 1 - c), device_id_type=MESH)
               for i in range(n)]
        for cp in cps:
            cp.start()
        for cp in cps:
            cp.wait()

    return pl.pallas_call(body, in_specs=[_ANY] * n, out_specs=[_ANY] * n, out_shape=[S(p.shape, p.dtype) for p in ps],
                          scratch_shapes=[pltpu.SemaphoreType.DMA((n,)), pltpu.SemaphoreType.DMA((n,))], name=name)(*ps)


def _quarter_spec(tr, r, c, dim):
    if dim is None:
        return pl.BlockSpec((1, tr, c), lambda i, me: (me[0], i, 0))
    if dim == 0:
        return pl.BlockSpec((tr, c), lambda i, me: (me[0] * (r // tr) + i, 0))
    return pl.BlockSpec((tr, c), lambda i, me: (i, me[0]))


def _whole_shape(r, c, dim):
    return (N_CHIPS, r, c) if dim is None else ((N_CHIPS * r, c) if dim == 0 else (r, N_CHIPS * c))


def _row_tile(r):
    for d in range(min(r, 256), 7, -1):
        if r % d == 0 and d % 8 == 0:
            return d
    return r


def _cast_place(me, x, dim, dtype, name):
    r, c = x.shape
    tr = _row_tile(r)

    def body(me_ref, x_ref, o_ref):
        o_ref[...] = x_ref[...].astype(dtype).reshape(o_ref.shape)

    return pl.pallas_call(
        body, out_shape=S(_whole_shape(r, c, dim), dtype), name=name,
        grid_spec=pltpu.PrefetchScalarGridSpec(num_scalar_prefetch=1, grid=(r // tr,),
                                               in_specs=[pl.BlockSpec((tr, c), lambda i, me: (i, 0))],
                                               out_specs=_quarter_spec(tr, r, c, dim)),
        compiler_params=_cp(("parallel",)))(me, x)


def _sum_parts(me, g, land, dim, name):
    _, r, c = land.shape
    tr = _row_tile(r)

    def body(me_ref, g_ref, l_ref, o_ref):
        own = g_ref[...].astype(F32).reshape(tr, c)
        o_ref[...] = ((own + l_ref[0].astype(F32)) + l_ref[1].astype(F32)) + l_ref[2].astype(F32)

    return pl.pallas_call(
        body, out_shape=S((r, c), F32), name=name,
        grid_spec=pltpu.PrefetchScalarGridSpec(
            num_scalar_prefetch=1, grid=(r // tr,),
            in_specs=[_quarter_spec(tr, r, c, dim), pl.BlockSpec((3, tr, c), lambda i, me: (0, i, 0))],
            out_specs=pl.BlockSpec((tr, c), lambda i, me: (i, 0))),
        compiler_params=_cp(("parallel",)))(me, g, land)


_HBM = pl.BlockSpec(memory_space=pltpu.HBM)
_SEM = pl.BlockSpec(memory_space=pltpu.SEMAPHORE)
_EFFECT = pltpu.SideEffectType.DATAFLOW_SIDE_EFFECTING


def _quarter(ref, s, dim):
    n = ref.shape[dim] // N_CHIPS
    start = pl.multiple_of(s * n, n)
    return ref.at[pl.ds(start, n), :] if dim == 0 else ref.at[:, pl.ds(start, n)]


def _part(ref, s, dim):
    return ref.at[s] if dim is None else _quarter(ref, s, dim)


def _gather_plan(i, dim):
    return dict(src=i, dst=i, send_src=lambda ref, me, peer, k: _part(ref, me, dim),
                send_dst=lambda ref, me, peer, k: _part(ref, me, dim), recv_dst=lambda ref, me, peer, k: _part(ref, peer, dim))


def _scatter_plan(i, j, dim):
    return dict(src=i, dst=j, send_src=lambda ref, me, peer, k: _part(ref, peer, dim),
                send_dst=lambda ref, me, peer, k: ref.at[k], recv_dst=lambda ref, me, peer, k: ref.at[k])


def _hbm(a):
    return pltpu.with_memory_space_constraint(a, pltpu.HBM)


def _swap_plan(i, j):
    whole = lambda ref, me, peer, k: ref
    return dict(src=i, dst=j, send_src=whole, send_dst=whole, recv_dst=whole)


def _remote_copies(refs, send_sems, recv_sems, plans, sibling=False):
    me, c, peers = _chip_peers()
    x, y = lax.axis_index("x"), lax.axis_index("y")
    targets = [((x, y, 1 - c), me)] if sibling else [((px, py, c), 2 * px + py) for px, py in peers]
    n = len(targets)
    sends, recvs = [], []
    for i, p in enumerate(plans):
        for k, (device, peer) in enumerate(targets):
            sems = dict(send_sem=send_sems.at[n * i + k], recv_sem=recv_sems.at[n * i + k], device_id=device,
                        device_id_type=MESH)
            src = p['send_src'](refs[p['src']], me, peer, k)
            sends.append(pltpu.make_async_remote_copy(src_ref=src, dst_ref=p['send_dst'](refs[p['dst']], me, peer, k), **sems))
            recvs.append(pltpu.make_async_remote_copy(src_ref=src, dst_ref=p['recv_dst'](refs[p['dst']], me, peer, k), **sems))
    return sends, recvs


def _exchange_start(arrays, plans, name, after=(), sibling=False):
    n, extra = len(arrays), len(after)
    m = len(plans) * (1 if sibling else 3)

    def body(*refs):
        send_sems, recv_sems, token = refs[n + extra], refs[n + extra + 1], refs[-1]
        sends, _ = _remote_copies(refs[:n], send_sems, recv_sems, plans, sibling)
        for cp in sends:
            cp.start()
        token[...] = jnp.zeros_like(token)

    outs = pl.pallas_call(
        body, name=name,
        out_shape=(pltpu.SemaphoreType.DMA((m,)), pltpu.SemaphoreType.DMA((m,)),
                   *[pltpu.HBM(a.shape, a.dtype) for a in arrays], S((8, 128), F32)),
        in_specs=[_HBM] * n + [_ANY] * extra, out_specs=(_SEM, _SEM, *[_HBM] * n, pl.BlockSpec(memory_space=pltpu.VMEM)),
        input_output_aliases={i: 2 + i for i in range(n)},
        compiler_params=pltpu.CompilerParams(has_side_effects=_EFFECT))(*[_hbm(a) for a in arrays], *after)
    return outs[0], outs[1], outs[2:2 + n], outs[-1]


def _exchange_wait(started, plans, after, name, sibling=False):
    send_sems, recv_sems, arrays, _ = started
    n = len(arrays)

    def body(*refs):
        sends, recvs = _remote_copies(refs[:n], refs[n], refs[n + 1], plans, sibling)
        for cp in sends:
            cp.wait_send()
        for cp in recvs:
            cp.wait_recv()

    return pl.pallas_call(
        body, name=name, out_shape=tuple(pltpu.HBM(a.shape, a.dtype) for a in arrays),
        in_specs=[_HBM] * n + [_SEM, _SEM, _ANY], out_specs=tuple([_HBM] * n),
        input_output_aliases={i: i for i in range(n)},
        compiler_params=pltpu.CompilerParams(has_side_effects=_EFFECT))(*arrays, send_sems, recv_sems, after)


def _add2(p, q):
    r, c = p.shape
    tr = 128 if r % 128 == 0 else r

    def body(p_ref, q_ref, o_ref):
        o_ref[...] = p_ref[...] + q_ref[...]

    return pl.pallas_call(body, grid=(r // tr,), in_specs=[_row_spec(tr, c), _row_spec(tr, c)], out_specs=_row_spec(tr, c),
                          out_shape=S((r, c), F32), name="add_cores", compiler_params=_cp(("parallel",)))(p, q)


def _adamw(w, g, m, v, name, g2=None, after=()):
    r, c = w.shape
    tr = _row_tile(r)
    c1 = 1.0 / (1.0 - ADAM_B1 ** ADAM_STEP)
    c2 = 1.0 / (1.0 - ADAM_B2 ** ADAM_STEP)
    gs = [g] if g2 is None else [g, g2]

    def body(*refs):
        w_ref, m_ref, v_ref = refs[0], refs[1], refs[2]
        g_refs = refs[3:3 + len(gs)]
        go_ref, d_ref, nm_ref, nv_ref = refs[3 + len(gs) + len(after):]
        gv = g_refs[0][...]
        if len(gs) == 2:
            gv = gv + g_refs[1][...]
        go_ref[...] = gv
        nm = ADAM_B1 * m_ref[...] + (1.0 - ADAM_B1) * gv
        nv = ADAM_B2 * v_ref[...] + (1.0 - ADAM_B2) * (gv * gv)
        nm_ref[...] = nm
        nv_ref[...] = nv
        d_ref[...] = -ADAM_LR * ((nm * c1) / (jnp.sqrt(nv * c2) + ADAM_EPS) + ADAM_WD * w_ref[...])

    return pl.pallas_call(body, grid=(r // tr,),
                          in_specs=[_row_spec(tr, c)] * (3 + len(gs)) + [pl.BlockSpec(memory_space=pl.ANY)] * len(after),
                          out_specs=[_row_spec(tr, c)] * 4, out_shape=[S((r, c), F32)] * 4, name=name,
                          compiler_params=_cp(("parallel",)))(w, m, v, *gs, *after)


_WEIGHTS = ['pre_norm', 'post_norm', 'rel_bias', 'a_w_in', 'a_lam_re', 'a_lam_im', 'a_log_dt', 'a_b_re', 'a_b_im',
            'a_c_re', 'a_c_im', 'a_d', 'a_w_glu', 'a_b_glu', 'a_w_out', 'b_w_in', 'b_sinks', 'b_w_out', 'c_w_in',
            'c_q_norm', 'c_kv_norm', 'c_w_uq', 'c_w_ukv', 'c_w_out', 'd_w_in', 'd_ln_g', 'd_ln_b', 'd_w_s', 'd_b_s',
            'd_w_out']
_BIG = [('a_w_in', 1024, 2048, 1), ('a_w_glu', 1024, 1024, 0), ('a_w_out', 1024, 1024, 0), ('b_w_in', 1024, 2304, 1),
        ('b_w_out', 1024, 1024, 0), ('c_w_in', 1024, 2080, 1), ('c_w_uq', 768, 1536, 1), ('c_w_ukv', 256, 2048, 1),
        ('c_w_out', 1024, 1024, 0), ('d_w_in', 1024, 3072, 1), ('d_w_out', 1024, 1024, 0)]
_SHARDED_VECS = [('c_q_norm', 768), ('c_kv_norm', 256), ('d_ln_g', 1024), ('d_ln_b', 1024)]
_TRANSPOSED = ('b_w_in', 'c_w_in')
_REPLICATED = [n for n in _WEIGHTS if n not in [b[0] for b in _BIG] and n not in [s[0] for s in _SHARDED_VECS]]
_LANES = 1024
_SMALL_Q_ROWS = 136


def kernel(x, pre_norm, post_norm, rel_bias, a_w_in, a_lam_re, a_lam_im, a_log_dt, a_b_re, a_b_im, a_c_re, a_c_im, a_d, a_w_glu, a_b_glu, a_w_out, b_w_in, b_sinks, b_w_out, c_w_in, c_q_norm, c_kv_norm, c_w_uq, c_w_ukv, c_w_out, d_w_in, d_ln_g, d_ln_b, d_w_s, d_b_s, d_w_out, loss_target, m_pre_norm, m_post_norm, m_rel_bias, m_a_w_in, m_a_lam_re, m_a_lam_im, m_a_log_dt, m_a_b_re, m_a_b_im, m_a_c_re, m_a_c_im, m_a_d, m_a_w_glu, m_a_b_glu, m_a_w_out, m_b_w_in, m_b_sinks, m_b_w_out, m_c_w_in, m_c_q_norm, m_c_kv_norm, m_c_w_uq, m_c_w_ukv, m_c_w_out, m_d_w_in, m_d_ln_g, m_d_ln_b, m_d_w_s, m_d_b_s, m_d_w_out, v_pre_norm, v_post_norm, v_rel_bias, v_a_w_in, v_a_lam_re, v_a_lam_im, v_a_log_dt, v_a_b_re, v_a_b_im, v_a_c_re, v_a_c_im, v_a_d, v_a_w_glu, v_a_b_glu, v_a_w_out, v_b_w_in, v_b_sinks, v_b_w_out, v_c_w_in, v_c_q_norm, v_c_kv_norm, v_c_w_uq, v_c_w_ukv, v_c_w_out, v_d_w_in, v_d_ln_g, v_d_ln_b, v_d_w_s, v_d_b_s, v_d_w_out):
    arg = dict(locals())
    me = 2 * lax.axis_index("x") + lax.axis_index("y")
    xin = x[0]
    t = xin.shape[0]

    big = {n: (r, c, dim) for n, r, c, dim in _BIG}

    me1 = me.reshape(1).astype(jnp.int32)

    def gather_start(names, tag, after):
        wholes, plans = [], []
        for i, n in enumerate(names):
            if n in big:
                r, c, dim = big[n]
                src, dtype = arg[n][0], BF16
                if n in _TRANSPOSED:
                    src, dim = src.T, 0
                    if (c // N_CHIPS) % 16:
                        dtype = F32
                elif dim == 1 and (c // N_CHIPS) % 128:
                    dim = None
            else:
                src, dim, dtype = arg[n], None, F32
            wholes.append(_cast_place(me1, src, dim, dtype, "place_" + n))
            plans.append(_gather_plan(i, dim))
        return _exchange_start(wholes, plans, "gather_start_" + tag, after=after), plans

    def gather_wait(names, started, after, tag):
        st, plans = started
        out = {}
        for n, a in zip(names, _exchange_wait(st, plans, after, "gather_wait_" + tag)):
            if n in big:
                r, c, dim = big[n]
                out[n] = a if a.ndim == 2 else a.transpose(1, 0, 2).reshape(r, c)
            else:
                out[n] = a.reshape(1, -1)
        return out

    groups = {'a1': ['a_w_in'], 'a2': ['a_w_glu', 'a_w_out'], 'b': ['b_w_in', 'b_w_out'],
              'c': ['c_w_in', 'c_w_uq', 'c_w_ukv', 'c_w_out', 'c_q_norm', 'c_kv_norm'],
              'd': ['d_w_in', 'd_w_out', 'd_ln_g', 'd_ln_b']}
    started, tokens = {}, []
    for tag, names in groups.items():
        started[tag] = gather_start(names, tag, tokens)
        tokens = [started[tag][0][3]]
    wt = gather_wait(groups['a1'], started['a1'], xin, 'a1')
    cos, sin = _rope_tables(t)
    bucket = jnp.asarray(_t5_bucket_table())
    s5p = _s5_params(a_lam_re[0], a_lam_im[0], a_log_dt[0], a_b_re[0], a_b_im[0], a_c_re[0], a_c_im[0])
    bst = d_b_s[0].T

    x0 = xin
    h0 = _prenorm(x0, pre_norm[0:1], "pre0", after=tokens)
    proj0 = _mm(h0, wt['a_w_in'], name="a_in")

    def w_glu_after(x):
        wt.update(gather_wait(groups['a2'], started['a2'], x, 'a2'))
        return wt['a_w_glu']

    p0, res0 = _s5_fwd(proj0, s5p, a_d, w_glu_after, a_b_glu)
    y0 = _mm(p0, wt['a_w_out'], name="a_out")
    x1, h1 = _postnorm_prenorm(x0, y0, post_norm[0:1], pre_norm[1:2], "post0_pre1")

    wt.update(gather_wait(groups['b'], started['b'], x1, 'b'))
    proj1 = _mm(h1, wt['b_w_in'], tb=True, name="b_in")
    bias = _swa_bias(rel_bias, bucket)
    p1 = _swa_fwd(proj1, bias, b_sinks)
    y1 = _mm(p1, wt['b_w_out'], name="b_out")
    x2, h2 = _postnorm_prenorm(x1, y1, post_norm[1:2], pre_norm[2:3], "post1_pre2")

    wt.update(gather_wait(groups['c'], started['c'], x2, 'c'))
    w_c_in = jnp.concatenate([wt['c_w_in'][:1056], jnp.zeros((96, 1024), F32), wt['c_w_in'][1056:],
                              jnp.zeros((MLA_PROJ - MLA_Z1, 1024), F32)], axis=0).astype(BF16)
    w_uq = _perm_uq(wt['c_w_uq'])
    w_ukv = _perm_ukv(wt['c_w_ukv'])
    proj2 = _mm(h2, w_c_in, tb=True, name="c_in")
    p2, res2 = _mla_fwd(proj2, wt['c_q_norm'], wt['c_kv_norm'], w_uq, w_ukv, cos, sin)
    y2 = _mm(p2, wt['c_w_out'], name="c_out")
    x3, h3 = _postnorm_prenorm(x2, y2, post_norm[2:3], pre_norm[3:4], "post2_pre3")

    wt.update(gather_wait(groups['d'], started['d'], x3, 'd'))
    vec = wt
    proj3 = _mm(h3, wt['d_w_in'], name="d_in")
    p3 = _sgu_fwd(proj3, vec['d_ln_g'], vec['d_ln_b'], d_w_s[0], bst)
    y3 = _mm(p3, wt['d_w_out'], name="d_out")
    loss_part, dx = _postnorm_loss(x3, y3, post_norm[3:4], loss_target[0])
    loss = lax.psum(loss_part[0, 0], ("x", "y", "c"))

    gr = {}
    d_pre, d_post = [None] * 4, [None] * 4
    scattered = []

    def scatter_start(items, tag):
        srcs = [a for _, a, _ in items]
        n = len(items)
        plans = [_scatter_plan(i, n + i, dim) for i, (_, _, dim) in enumerate(items)]
        lands = []
        for _, a, dim in items:
            part = a.shape[1:] if dim is None else tuple(s // N_CHIPS if i == dim else s for i, s in enumerate(a.shape))
            lands.append(lax.empty((3,) + part, a.dtype))
        st = _exchange_start(srcs + lands, plans, "scatter_start_" + tag)
        scattered.append((st, plans, items, tag))
        return [st[3]]

    dy, d_post[3] = _postnorm_bwd(dx, y3, post_norm[3:4], "post3_bwd")
    dp = _mm(dy, wt['d_w_out'], tb=True, name="d_out_dx")
    g_out = _mm(p3, dy, ta=True, out_dtype=BF16, name="d_out_dw")
    dproj, gr['d_w_s'], dbst, gr['d_ln_g'], gr['d_ln_b'] = _sgu_bwd(proj3, dp, vec['d_ln_g'], vec['d_ln_b'], d_w_s[0], bst)
    gr['d_b_s'] = dbst.T
    dh = _mm(dproj, wt['d_w_in'], tb=True, name="d_in_dx")
    g_in = _mm(h3, dproj, ta=True, out_dtype=BF16, name="d_in_dw")
    tok = scatter_start([('d_w_in', g_in, 1), ('d_w_out', g_out, 0)], 'd')
    dx, d_pre[3], dy, d_post[2] = _norm_bwd_pair(dx, dh, x3, pre_norm[3:4], y2, post_norm[2:3], "pre3_post2_bwd", after=tok)

    dp = _mm(dy, wt['c_w_out'], tb=True, name="c_out_dx")
    g_out = _mm(p2, dy, ta=True, out_dtype=BF16, name="c_out_dw")
    dproj, gr['c_q_norm'], gr['c_kv_norm'], dwuq, dwukv = _mla_bwd(proj2, res2, dp, wt['c_q_norm'], wt['c_kv_norm'],
                                                                  w_uq, w_ukv, cos, sin)
    dh = _mm(dproj, w_c_in, name="c_in_dx")
    dwc = _mm(dproj, h2, ta=True, name="c_in_dw")
    g_in = jnp.concatenate([dwc[:1056], dwc[MLA_Z0:MLA_Z1]], axis=0)
    tok = scatter_start([('c_w_in', g_in, 0), ('c_w_uq', _unperm_uq(dwuq), 1), ('c_w_ukv', _unperm_ukv(dwukv), 1),
                         ('c_w_out', g_out, 0)], 'c')
    dx, d_pre[2], dy, d_post[1] = _norm_bwd_pair(dx, dh, x2, pre_norm[2:3], y1, post_norm[1:2], "pre2_post1_bwd", after=tok)

    dp = _mm(dy, wt['b_w_out'], tb=True, name="b_out_dx")
    g_out = _mm(p1, dy, ta=True, out_dtype=BF16, name="b_out_dw")
    dqz, dkv, dss, dsk = _swa_bwd(proj1, dp, bias, b_sinks)
    gr['rel_bias'], gr['b_sinks'] = _swa_dbias(dss, dsk, bucket)
    dproj = jnp.concatenate([dqz[:, :1024], dkv.astype(BF16), dqz[:, 1024:]], axis=1)
    dh = _mm(dproj, wt['b_w_in'], name="b_in_dx")
    g_in = _mm(dproj, h1, ta=True, out_dtype=BF16, name="b_in_dw")
    tok = scatter_start([('b_w_in', g_in, 0), ('b_w_out', g_out, 0)], 'b')
    dx, d_pre[1], dy, d_post[0] = _norm_bwd_pair(dx, dh, x1, pre_norm[1:2], y0, post_norm[0:1], "pre1_post0_bwd", after=tok)

    dp = _mm(dy, wt['a_w_out'], tb=True, name="a_out_dx")
    g_out = _mm(p0, dy, ta=True, out_dtype=BF16, name="a_out_dw")
    dproj, s5cots, gr['a_d'], g_glu, gr['a_b_glu'] = _s5_bwd(proj0, res0, dp, s5p, a_d, wt['a_w_glu'], a_b_glu)
    tok = scatter_start([('a_w_glu', g_glu, 0), ('a_w_out', g_out, 0)], 'a2')
    g_in = _mm(h0, dproj, ta=True, out_dtype=BF16, name="a_in_dw", after=tok)
    tok = scatter_start([('a_w_in', g_in, 1)], 'a1')
    dh = _mm(dproj, wt['a_w_in'], tb=True, name="a_in_dx", after=tok)
    dx, d_pre[0] = _prenorm_bwd(dx, dh, x0, pre_norm[0:1], "pre0_bwd")
    s5g = _s5_param_grads(s5p, s5cots, after=tok)
    for n, g in zip(['a_lam_re', 'a_lam_im', 'a_log_dt', 'a_b_re', 'a_b_im', 'a_c_re', 'a_c_im'], s5g):
        gr[n] = g
    gr['pre_norm'] = jnp.concatenate(d_pre, axis=0)
    gr['post_norm'] = jnp.concatenate(d_post, axis=0)
    small_names = _REPLICATED + [n for n, _ in _SHARDED_VECS]
    small = jnp.concatenate([gr[n].reshape(-1) for n in small_names])
    small = jnp.pad(small, (0, N_CHIPS * _SMALL_Q_ROWS * _LANES - small.shape[0])).reshape(N_CHIPS, _SMALL_Q_ROWS, _LANES)
    tok = scatter_start([('small', small, None)], 'a0')

    partial = {}
    for st, plans, items, tag in scattered[:-1]:
        done = _exchange_wait(st, plans, tok[0], "scatter_wait_" + tag)
        for i, (n, _, dim) in enumerate(items):
            partial[n] = _sum_parts(me1, done[i], done[len(items) + i], dim, "sum_parts_" + n)
    big_names = [n for n, _, _, _ in _BIG]
    nb = len(big_names)
    swap_plans = [_swap_plan(i, nb + i) for i in range(nb)]
    swapping = _exchange_start([partial[n] for n in big_names] + [lax.empty(partial[n].shape, F32) for n in big_names],
                               swap_plans, "swap_cores_start", sibling=True)
    st, plans, _, tag = scattered[-1]
    done = _exchange_wait(st, plans, swapping[3], "scatter_wait_" + tag)
    p_small = _sum_parts(me1, done[0], done[1], None, "sum_parts_small")
    small_q = _add2(p_small, _sibling_swap([p_small], "swap_cores_small")[0])
    gplan = [_gather_plan(0, None)]
    gathering = _exchange_start([_cast_place(me1, small_q, None, F32, "place_small_grads")], gplan, "gather_small_start")

    swapped = _exchange_wait(swapping, swap_plans, gathering[3], "swap_cores_wait", sibling=True)
    partial.update(zip(big_names, swapped[:nb]))
    other = dict(zip(big_names, swapped[nb:]))

    delta, new_m, new_v, grads = {}, {}, {}, {}
    for n in big_names:
        shp = arg[n].shape
        view = (lambda a: a[0].T) if n in _TRANSPOSED else (lambda a: a[0])
        back = (lambda o: o.T.reshape(shp)) if n in _TRANSPOSED else (lambda o: o.reshape(shp))
        outs = _adamw(view(arg[n]), partial[n], view(arg['m_' + n]), view(arg['v_' + n]), "adamw_" + n, g2=other[n],
                      after=[gathering[3]])
        grads[n], delta[n], new_m[n], new_v[n] = [back(o) for o in outs]
    small_all = _exchange_wait(gathering, gplan, new_v[big_names[-1]], "gather_small_wait")[0].reshape(-1)
    def view(n, a):
        return a[0].transpose(0, 2, 1) if n in ('a_b_re', 'a_b_im') else a

    def back(n, v):
        return v.transpose(0, 2, 1)[None] if n in ('a_b_re', 'a_b_im') else v

    off = 0
    gview = {}
    for n in _REPLICATED:
        shp = view(n, arg[n]).shape
        sz = math.prod(shp)
        gview[n] = small_all[off:off + sz].reshape(shp)
        off += sz
    for n, sz in _SHARDED_VECS:
        q = sz // N_CHIPS
        gview[n] = lax.dynamic_slice(small_all, (off + me * q,), (q,)).reshape(arg[n].shape)
        off += sz

    for n in small_names:
        shp = gview[n].shape
        two = (math.prod(shp[:-1]), shp[-1])
        outs = _adamw(view(n, arg[n]).reshape(two), gview[n].reshape(two), view(n, arg['m_' + n]).reshape(two),
                      view(n, arg['v_' + n]).reshape(two), "adamw_" + n)
        grads[n] = back(n, gview[n])
        delta[n], new_m[n], new_v[n] = [back(n, o.reshape(shp)) for o in outs[1:]]

    return (loss, dx.reshape(x.shape), *[grads[n] for n in _WEIGHTS], *[delta[n] for n in _WEIGHTS],
            *[new_m[n] for n in _WEIGHTS], *[new_v[n] for n in _WEIGHTS])
```
